```python
import math
import jax, jax.numpy as jnp
from jax import lax
import numpy as np

D_MODEL = 2048
BATCH = 8
SEQ = 2048
DEPTH = 4

N_MIXERS = 3
HEAD_DIM = 64
N_HEADS = D_MODEL // HEAD_DIM
BRANCH = N_HEADS * HEAD_DIM
N_KV_A = N_HEADS // 8
KV_A = N_KV_A * HEAD_DIM
WINDOW = 128
BLOCK = 128
NORM_EPS = 1e-6
NEG = -1e30

A_COLS = BRANCH + 2 * KV_A + BRANCH
B_COLS = 4 * BRANCH
C_COLS = 4 * BRANCH + N_HEADS

kernel_name = "hybrid_swa_stickbreak_fox_trunk"


def _n_layers_of(kind):
    return sum(1 for i in range(DEPTH) if i % N_MIXERS == kind)


def _alibi_slopes(n):
    return jnp.asarray(2.0 ** (-8.0 * np.arange(1, n + 1, dtype=np.float32) / n), dtype=jnp.float32)


def rmsnorm(x, g):
    xf = x.astype(jnp.float32)
    r = lax.rsqrt(jnp.mean(xf * xf, axis=-1, keepdims=True) + NORM_EPS)
    return (xf * r * g.astype(jnp.float32)).astype(x.dtype)


def swa_sink_mixer(h, w_in, sinks):
    B, S, _ = h.shape
    nb = S // BLOCK
    G = N_HEADS // N_KV_A
    q, k, v, z = jnp.split(h @ w_in, [BRANCH, BRANCH + KV_A, BRANCH + 2 * KV_A], axis=-1)
    q = q.reshape(B, nb, BLOCK, N_KV_A, G, HEAD_DIM).astype(jnp.float32)
    k = k.reshape(B, nb, BLOCK, N_KV_A, HEAD_DIM).astype(jnp.float32)
    v = v.reshape(B, nb, BLOCK, N_KV_A, HEAD_DIM).astype(jnp.float32)

    def band(a):
        prev = jnp.concatenate([jnp.zeros_like(a[:, :1]), a[:, :-1]], axis=1)
        return jnp.concatenate([prev, a], axis=2)

    kb, vb = band(k), band(v)
    scores = jnp.einsum('bnqhgd,bnkhd->bnhgqk', q, kb) * (HEAD_DIM ** -0.5)
    qi = jnp.arange(BLOCK)[:, None]
    kj = jnp.arange(2 * BLOCK)[None, :]
    dist = (qi + BLOCK - kj).astype(jnp.float32)
    s_pos = jnp.arange(nb)[:, None, None] * BLOCK - BLOCK + kj[None]
    valid = (dist >= 0) & (dist < WINDOW) & (s_pos >= 0)
    slopes = _alibi_slopes(N_HEADS).reshape(N_KV_A, G)
    scores = scores - slopes[:, :, None, None] * dist
    scores = jnp.where(valid[None, :, None, None], scores, NEG)
    sink = sinks.astype(jnp.float32).reshape(N_KV_A, G)
    sink_col = jnp.broadcast_to(sink[None, None, :, :, None, None], scores.shape[:-1] + (1,))
    p = jax.nn.softmax(jnp.concatenate([scores, sink_col], axis=-1), axis=-1)[..., :-1]
    o = jnp.einsum('bnhgqk,bnkhd->bnqhgd', p, vb).reshape(B, S, BRANCH)
    return o, z


def stick_breaking_mixer(h, w_in):
    B, S, _ = h.shape
    nb = S // BLOCK
    q, k, v, z = jnp.split(h @ w_in, 4, axis=-1)
    q = q.reshape(B, nb, BLOCK, N_HEADS, HEAD_DIM).transpose(1, 0, 3, 2, 4).astype(jnp.float32)
    k = k.reshape(B, S, N_HEADS, HEAD_DIM).astype(jnp.float32)
    v = v.reshape(B, S, N_HEADS, HEAD_DIM).astype(jnp.float32)
    s_pos = jnp.arange(S)
    scale = HEAD_DIM ** -0.5

    def block(args):
        qb, n = args
        t_pos = n * BLOCK + jnp.arange(BLOCK)
        logits = jnp.einsum('bhqd,bshd->bhqs', qb, k) * scale
        before = s_pos[None, :] < t_pos[:, None]
        log_fail = jnp.where(before, jax.nn.log_sigmoid(-logits), 0.0)
        incl = lax.cumsum(log_fail, axis=3, reverse=True)
        suffix = jnp.concatenate([incl[..., 1:], jnp.zeros_like(incl[..., :1])], axis=-1)
        a = jnp.where(before, jnp.exp(jax.nn.log_sigmoid(logits) + suffix), 0.0)
        return jnp.einsum('bhqs,bshd->bqhd', a, v)

    o = lax.map(block, (q, jnp.arange(nb)))
    o = o.transpose(1, 0, 2, 3, 4).reshape(B, S, BRANCH)
    return o, z


def forgetting_mixer(h, w_in, b_f):
    B, S, _ = h.shape
    nb = S // BLOCK
    q, k, v, z, f_logit = jnp.split(h @ w_in, [BRANCH, 2 * BRANCH, 3 * BRANCH, 4 * BRANCH], axis=-1)
    log_f = jax.nn.log_sigmoid(f_logit.astype(jnp.float32) + b_f.astype(jnp.float32))
    cum = lax.cumsum(log_f, axis=1).transpose(0, 2, 1)
    q = q.reshape(B, nb, BLOCK, N_HEADS, HEAD_DIM).transpose(1, 0, 3, 2, 4).astype(jnp.float32)
    cq = cum.reshape(B, N_HEADS, nb, BLOCK).transpose(2, 0, 1, 3)
    k = k.reshape(B, S, N_HEADS, HEAD_DIM).astype(jnp.float32)
    v = v.reshape(B, S, N_HEADS, HEAD_DIM).astype(jnp.float32)
    s_pos = jnp.arange(S)
    scale = HEAD_DIM ** -0.5

    def block(args):
        qb, cqb, n = args
        t_pos = n * BLOCK + jnp.arange(BLOCK)
        logits = jnp.einsum('bhqd,bshd->bhqs', qb, k) * scale + cqb[..., :, None] - cum[:, :, None, :]
        causal = s_pos[None, :] <= t_pos[:, None]
        p = jax.nn.softmax(jnp.where(causal, logits, NEG), axis=-1)
        return jnp.einsum('bhqs,bshd->bqhd', p, v)

    o = lax.map(block, (q, cq, jnp.arange(nb)))
    o = o.transpose(1, 0, 2, 3, 4).reshape(B, S, BRANCH)
    return o, z


def _fwd_setup_inputs(seed: int = 0) -> dict:
    key = jax.random.key(seed)
    ks = jax.random.split(key, 12)
    n_a, n_b, n_c = _n_layers_of(0), _n_layers_of(1), _n_layers_of(2)
    f32 = jnp.float32
    nrm = jax.random.normal
    return {
        "x": nrm(ks[0], (BATCH, SEQ, D_MODEL), f32),
        "g_pre": 1.0 + 0.02 * nrm(ks[1], (DEPTH, D_MODEL), f32),
        "g_post": 1.0 + 0.02 * nrm(ks[2], (DEPTH, D_MODEL), f32),
        "w_in_a": nrm(ks[3], (n_a, D_MODEL, A_COLS), f32) * D_MODEL ** -0.5,
        "w_out_a": nrm(ks[4], (n_a, BRANCH, D_MODEL), f32) * BRANCH ** -0.5,
        "sinks_a": 0.5 * nrm(ks[5], (n_a, N_HEADS), f32),
        "w_in_b": nrm(ks[6], (n_b, D_MODEL, B_COLS), f32) * D_MODEL ** -0.5,
        "w_out_b": nrm(ks[7], (n_b, BRANCH, D_MODEL), f32) * BRANCH ** -0.5,
        "w_in_c": nrm(ks[8], (n_c, D_MODEL, C_COLS), f32) * D_MODEL ** -0.5,
        "b_f_c": 1.0 + 5.0 * jax.random.uniform(ks[9], (n_c, N_HEADS), f32),
        "w_out_c": nrm(ks[10], (n_c, BRANCH, D_MODEL), f32) * BRANCH ** -0.5,
    }


def _fwd_reference(x, g_pre, g_post, w_in_a, w_out_a, sinks_a, w_in_b, w_out_b, w_in_c, b_f_c, w_out_c):
    for i in range(DEPTH):
        kind, j = i % N_MIXERS, i // N_MIXERS
        h = rmsnorm(x, g_pre[i])
        if kind == 0:
            o, z = swa_sink_mixer(h, w_in_a[j], sinks_a[j])
            w_out = w_out_a[j]
        elif kind == 1:
            o, z = stick_breaking_mixer(h, w_in_b[j])
            w_out = w_out_b[j]
        else:
            o, z = forgetting_mixer(h, w_in_c[j], b_f_c[j])
            w_out = w_out_c[j]
        y = (o.astype(z.dtype) * jax.nn.silu(z)) @ w_out
        x = x + rmsnorm(y, g_post[i])
    return x


import jax as _jax
import jax.numpy as _jnp

TWIN_FORMAT = 'train_step'
FWD_PARAMS = ['x', 'g_pre', 'g_post', 'w_in_a', 'w_out_a', 'sinks_a', 'w_in_b', 'w_out_b', 'w_in_c', 'b_f_c', 'w_out_c']
TWIN_WEIGHTS = ['g_pre', 'g_post', 'w_in_a', 'w_out_a', 'sinks_a', 'w_in_b', 'w_out_b', 'w_in_c', 'b_f_c', 'w_out_c']
TWIN_DIFF_INPUT = 'x'
TWIN_INPUTS = ['x', 'g_pre', 'g_post', 'w_in_a', 'w_out_a', 'sinks_a', 'w_in_b', 'w_out_b', 'w_in_c', 'b_f_c', 'w_out_c', 'loss_target', 'm_g_pre', 'm_g_post', 'm_w_in_a', 'm_w_out_a', 'm_sinks_a', 'm_w_in_b', 'm_w_out_b', 'm_w_in_c', 'm_b_f_c', 'm_w_out_c', 'v_g_pre', 'v_g_post', 'v_w_in_a', 'v_w_out_a', 'v_sinks_a', 'v_w_in_b', 'v_w_out_b', 'v_w_in_c', 'v_b_f_c', 'v_w_out_c']
TWIN_OUTPUTS = ['loss', 'grad_x', 'grad_g_pre', 'grad_g_post', 'grad_w_in_a', 'grad_w_out_a', 'grad_sinks_a', 'grad_w_in_b', 'grad_w_out_b', 'grad_w_in_c', 'grad_b_f_c', 'grad_w_out_c', 'delta_g_pre', 'delta_g_post', 'delta_w_in_a', 'delta_w_out_a', 'delta_sinks_a', 'delta_w_in_b', 'delta_w_out_b', 'delta_w_in_c', 'delta_b_f_c', 'delta_w_out_c', 'new_m_g_pre', 'new_m_g_post', 'new_m_w_in_a', 'new_m_w_out_a', 'new_m_sinks_a', 'new_m_w_in_b', 'new_m_w_out_b', 'new_m_w_in_c', 'new_m_b_f_c', 'new_m_w_out_c', 'new_v_g_pre', 'new_v_g_post', 'new_v_w_in_a', 'new_v_w_out_a', 'new_v_sinks_a', 'new_v_w_in_b', 'new_v_w_out_b', 'new_v_w_in_c', 'new_v_b_f_c', 'new_v_w_out_c']
TWIN_LEAF_KINDS = {'loss': 'loss', 'grad_x': 'grad_x', 'grad_g_pre': 'grad_w', 'grad_g_post': 'grad_w', 'grad_w_in_a': 'grad_w', 'grad_w_out_a': 'grad_w', 'grad_sinks_a': 'grad_w', 'grad_w_in_b': 'grad_w', 'grad_w_out_b': 'grad_w', 'grad_w_in_c': 'grad_w', 'grad_b_f_c': 'grad_w', 'grad_w_out_c': 'grad_w', 'delta_g_pre': 'delta_w', 'delta_g_post': 'delta_w', 'delta_w_in_a': 'delta_w', 'delta_w_out_a': 'delta_w', 'delta_sinks_a': 'delta_w', 'delta_w_in_b': 'delta_w', 'delta_w_out_b': 'delta_w', 'delta_w_in_c': 'delta_w', 'delta_b_f_c': 'delta_w', 'delta_w_out_c': 'delta_w', 'new_m_g_pre': 'new_m', 'new_m_g_post': 'new_m', 'new_m_w_in_a': 'new_m', 'new_m_w_out_a': 'new_m', 'new_m_sinks_a': 'new_m', 'new_m_w_in_b': 'new_m', 'new_m_w_out_b': 'new_m', 'new_m_w_in_c': 'new_m', 'new_m_b_f_c': 'new_m', 'new_m_w_out_c': 'new_m', 'new_v_g_pre': 'new_v', 'new_v_g_post': 'new_v', 'new_v_w_in_a': 'new_v', 'new_v_w_out_a': 'new_v', 'new_v_sinks_a': 'new_v', 'new_v_w_in_b': 'new_v', 'new_v_w_out_b': 'new_v', 'new_v_w_in_c': 'new_v', 'new_v_b_f_c': 'new_v', 'new_v_w_out_c': 'new_v'}


def _forward(args):
    return _fwd_reference(*[args[k] for k in FWD_PARAMS])


def _output_shape():
    out = _jax.eval_shape(lambda: _forward(_fwd_setup_inputs(0)))
    return out.shape, out.dtype

N_MICROBATCH = 1
ADAM_LR = 0.001
ADAM_B1 = 0.9
ADAM_B2 = 0.999
ADAM_EPS = 1e-08
ADAM_WD = 0.01
ADAM_STEP = 10
PER_EXAMPLE_BATCH_AXIS = {'x': 0, 'loss_target': 0}
SHARED_INPUTS = []
_WEIGHT_DTYPES = {'g_pre': _jnp.float32, 'g_post': _jnp.float32, 'w_in_a': _jnp.float32, 'w_out_a': _jnp.float32, 'sinks_a': _jnp.float32, 'w_in_b': _jnp.float32, 'w_out_b': _jnp.float32, 'w_in_c': _jnp.float32, 'b_f_c': _jnp.float32, 'w_out_c': _jnp.float32}
MOMENT_SCALE = {'g_pre': 4.864801e-01, 'g_post': 7.983895e+00, 'w_in_a': 3.651296e-01, 'w_out_a': 3.075149e-01, 'sinks_a': 6.127894e-01, 'w_in_b': 2.120783e-01, 'w_out_b': 2.719090e-01, 'w_in_c': 1.694146e-01, 'b_f_c': 1.298862e+00, 'w_out_c': 1.938131e-01}


def _to_microbatches(a, axis):
    t = _jnp.moveaxis(a, axis, 0)
    t = t.reshape((N_MICROBATCH, t.shape[0] // N_MICROBATCH) + t.shape[1:])
    return _jnp.moveaxis(t, 1, axis + 1)


def setup_inputs(seed: int = 0) -> dict:
    inp = _fwd_setup_inputs(seed)
    key = _jax.random.fold_in(_jax.random.key(seed), 7919)
    shape, _ = _output_shape()
    out = dict(inp)
    out["loss_target"] = _jax.random.normal(_jax.random.fold_in(key, 0), shape, _jnp.float32)
    for i, name in enumerate(TWIN_WEIGHTS):
        w = inp[name].astype(_jnp.float32)
        if MOMENT_SCALE is None:
            s = _jnp.sqrt(_jnp.mean(_jnp.square(w)) + 1e-30)
        else:
            s = MOMENT_SCALE[name]
        km, kv = _jax.random.split(_jax.random.fold_in(key, i + 1))
        out[name] = w
        out["m_" + name] = s * _jax.random.normal(km, w.shape, _jnp.float32)
        out["v_" + name] = (s * s) * _jax.random.uniform(kv, w.shape, _jnp.float32, 0.5, 1.5)
    if N_MICROBATCH > 1:
        for name, axis in PER_EXAMPLE_BATCH_AXIS.items():
            out[name] = _to_microbatches(out[name], axis)
    return {'x': out['x'], 'g_pre': out['g_pre'], 'g_post': out['g_post'], 'w_in_a': out['w_in_a'], 'w_out_a': out['w_out_a'], 'sinks_a': out['sinks_a'], 'w_in_b': out['w_in_b'], 'w_out_b': out['w_out_b'], 'w_in_c': out['w_in_c'], 'b_f_c': out['b_f_c'], 'w_out_c': out['w_out_c'], 'loss_target': out['loss_target'], 'm_g_pre': out['m_g_pre'], 'm_g_post': out['m_g_post'], 'm_w_in_a': out['m_w_in_a'], 'm_w_out_a': out['m_w_out_a'], 'm_sinks_a': out['m_sinks_a'], 'm_w_in_b': out['m_w_in_b'], 'm_w_out_b': out['m_w_out_b'], 'm_w_in_c': out['m_w_in_c'], 'm_b_f_c': out['m_b_f_c'], 'm_w_out_c': out['m_w_out_c'], 'v_g_pre': out['v_g_pre'], 'v_g_post': out['v_g_post'], 'v_w_in_a': out['v_w_in_a'], 'v_w_out_a': out['v_w_out_a'], 'v_sinks_a': out['v_sinks_a'], 'v_w_in_b': out['v_w_in_b'], 'v_w_out_b': out['v_w_out_b'], 'v_w_in_c': out['v_w_in_c'], 'v_b_f_c': out['v_b_f_c'], 'v_w_out_c': out['v_w_out_c']}


def _loss(weights, diff, rest, loss_target):
    with _jax.named_scope("forward"):
        args = {**rest, TWIN_DIFF_INPUT: diff, **{k: w.astype(_WEIGHT_DTYPES[k]) for k, w in weights.items()}}
        y = _forward(args)
    with _jax.named_scope("loss_head"):
        err = _jnp.square(y.astype(_jnp.float32) - loss_target)
        return 0.5 * _jnp.sum(_jnp.mean(err, axis=-1)) if err.ndim else 0.5 * err


def _adamw(w, g, m, v):
    m = ADAM_B1 * m + (1.0 - ADAM_B1) * g
    v = ADAM_B2 * v + (1.0 - ADAM_B2) * _jnp.square(g)
    m_hat = m / (1.0 - ADAM_B1 ** ADAM_STEP)
    v_hat = v / (1.0 - ADAM_B2 ** ADAM_STEP)
    delta = -ADAM_LR * (m_hat / (_jnp.sqrt(v_hat) + ADAM_EPS) + ADAM_WD * w)
    return delta, m, v


def reference(x, g_pre, g_post, w_in_a, w_out_a, sinks_a, w_in_b, w_out_b, w_in_c, b_f_c, w_out_c, loss_target, m_g_pre, m_g_post, m_w_in_a, m_w_out_a, m_sinks_a, m_w_in_b, m_w_out_b, m_w_in_c, m_b_f_c, m_w_out_c, v_g_pre, v_g_post, v_w_in_a, v_w_out_a, v_sinks_a, v_w_in_b, v_w_out_b, v_w_in_c, v_b_f_c, v_w_out_c):
    given = dict(x=x, g_pre=g_pre, g_post=g_post, w_in_a=w_in_a, w_out_a=w_out_a, sinks_a=sinks_a, w_in_b=w_in_b, w_out_b=w_out_b, w_in_c=w_in_c, b_f_c=b_f_c, w_out_c=w_out_c, loss_target=loss_target, m_g_pre=m_g_pre, m_g_post=m_g_post, m_w_in_a=m_w_in_a, m_w_out_a=m_w_out_a, m_sinks_a=m_sinks_a, m_w_in_b=m_w_in_b, m_w_out_b=m_w_out_b, m_w_in_c=m_w_in_c, m_b_f_c=m_b_f_c, m_w_out_c=m_w_out_c, v_g_pre=v_g_pre, v_g_post=v_g_post, v_w_in_a=v_w_in_a, v_w_out_a=v_w_out_a, v_sinks_a=v_sinks_a, v_w_in_b=v_w_in_b, v_w_out_b=v_w_out_b, v_w_in_c=v_w_in_c, v_b_f_c=v_b_f_c, v_w_out_c=v_w_out_c)
    weights = {n: given[n] for n in TWIN_WEIGHTS}
    shared = {n: given[n] for n in SHARED_INPUTS}
    per_example = {n: given[n] for n in ['x']}
    grad_fn = _jax.value_and_grad(_loss, argnums=(0, 1))

    def one_microbatch(ex, loss_target):
        ex = dict(ex)
        diff = ex.pop(TWIN_DIFF_INPUT)
        return grad_fn(weights, diff, {**shared, **ex}, loss_target)

    if N_MICROBATCH == 1:
        loss, (grad_w, grad_x) = one_microbatch(per_example, given["loss_target"])
    else:
        def body(carry, xs):
            loss_sum, grad_sum = carry
            l_k, (gw_k, gx_k) = one_microbatch(xs[0], xs[1])
            with _jax.named_scope("update"):
                return (loss_sum + l_k, _jax.tree.map(_jnp.add, grad_sum, gw_k)), gx_k

        init = (_jnp.zeros((), _jnp.float32), _jax.tree.map(_jnp.zeros_like, weights))
        (loss, grad_w), grad_x = _jax.lax.scan(body, init, (per_example, given["loss_target"]))
    with _jax.named_scope("update"):
        delta_w, new_m, new_v = {}, {}, {}
        for n in TWIN_WEIGHTS:
            delta_w[n], new_m[n], new_v[n] = _adamw(weights[n], grad_w[n], given["m_" + n], given["v_" + n])
    return (loss, grad_x, *[grad_w[n] for n in TWIN_WEIGHTS], *[delta_w[n] for n in TWIN_WEIGHTS],
            *[new_m[n] for n in TWIN_WEIGHTS], *[new_v[n] for n in TWIN_WEIGHTS])
```

```python
import functools
import math

import numpy as np
import jax
import jax.numpy as jnp
from jax import lax
from jax.experimental import pallas as pl
from jax.experimental.pallas import tpu as pltpu

F32 = jnp.float32
BF16 = jnp.bfloat16

D_MODEL = 2048
DEPTH = 4
N_HEADS = 32
HEAD_DIM = 64
LANES = 128
N_PAIRS = N_HEADS * HEAD_DIM // LANES
BRANCH = N_HEADS * HEAD_DIM
N_KV_A = 4
KV_A = N_KV_A * HEAD_DIM
WINDOW = 128
NORM_EPS = 1e-6
NEG = -1e30
Q_SCALE = HEAD_DIM ** -0.5

A_QKV = BRANCH + 2 * KV_A
B_QKV = 3 * BRANCH

ADAM_LR = 0.001
ADAM_B1 = 0.9
ADAM_B2 = 0.999
ADAM_EPS = 1e-08
ADAM_WD = 0.01
ADAM_STEP = 10

MESH = pl.DeviceIdType.MESH

_NT = (((1,), (1,)), ((), ()))
_TN = (((0,), (0,)), ((), ()))


def _params(sem=None):
    return pltpu.CompilerParams(dimension_semantics=sem)


def _matmul(a, b, *, mode, out_dtype, name, n=None, b_off=0, tm=1024, tn=1024, tk=2048):
    if mode == "nn":
        (m, k), nn = a.shape, (n or b.shape[1])
    elif mode == "nt":
        (m, k), nn = a.shape, b.shape[0]
    else:
        (k, m), nn = a.shape, b.shape[1]
    tm, tn, tk = min(tm, m), min(tn, nn), min(tk, k)
    assert m % tm == 0 and nn % tn == 0 and k % tk == 0, (name, m, nn, k, tm, tn, tk)
    nk = k // tk

    def body(a_ref, b_ref, o_ref, acc_ref):
        kk = pl.program_id(2)
        if mode == "nn":
            p = jnp.dot(a_ref[...], b_ref[...], preferred_element_type=F32)
        elif mode == "nt":
            p = lax.dot_general(a_ref[...], b_ref[...], _NT, preferred_element_type=F32)
        else:
            p = lax.dot_general(a_ref[...], b_ref[...], _TN, preferred_element_type=F32)
        if nk == 1:
            o_ref[...] = p.astype(o_ref.dtype)
        else:
            @pl.when(kk == 0)
            def _():
                acc_ref[...] = p

            @pl.when(kk > 0)
            def _():
                acc_ref[...] += p

            @pl.when(kk == nk - 1)
            def _():
                o_ref[...] = acc_ref[...].astype(o_ref.dtype)

    if mode == "nn":
        in_specs = [pl.BlockSpec((tm, tk), lambda i, j, kk: (i, kk)),
                    pl.BlockSpec((tk, tn), lambda i, j, kk: (kk, j + b_off))]
    elif mode == "nt":
        in_specs = [pl.BlockSpec((tm, tk), lambda i, j, kk: (i, kk)),
                    pl.BlockSpec((tn, tk), lambda i, j, kk: (j, kk))]
    else:
        in_specs = [pl.BlockSpec((tk, tm), lambda i, j, kk: (kk, i)),
                    pl.BlockSpec((tk, tn), lambda i, j, kk: (kk, j))]
    return pl.pallas_call(
        body, name=name, grid=(m // tm, nn // tn, nk),
        in_specs=in_specs,
        out_specs=pl.BlockSpec((tm, tn), lambda i, j, kk: (i, j)),
        out_shape=jax.ShapeDtypeStruct((m, nn), out_dtype),
        scratch_shapes=[pltpu.VMEM((tm, tn), F32)],
        compiler_params=_params(("parallel", "parallel", "arbitrary")),
    )(a, b)


ROW_TILE = 256


def _row_call(body, name, ins, outs, *, s, acc_outs=()):
    tr = min(ROW_TILE, s)
    row = lambda w: pl.BlockSpec((tr, w), lambda i: (i, 0))
    vec = lambda w: pl.BlockSpec((1, w), lambda i: (0, 0))
    in_specs = [row(a.shape[1]) if kind == "row" else vec(a.shape[1]) for a, kind in ins]
    out_specs = [row(sh.shape[1]) if kind == "row" else vec(sh.shape[1]) for sh, kind in outs]
    return pl.pallas_call(
        body, name=name, grid=(s // tr,), in_specs=in_specs, out_specs=out_specs,
        out_shape=[sh for sh, _ in outs],
        compiler_params=_params(("arbitrary",)),
    )(*[a for a, _ in ins])


def _rsqrt_ms(v):
    return lax.rsqrt(jnp.mean(v * v, axis=-1, keepdims=True) + NORM_EPS)


def _rmsnorm_fwd(x, g, name):
    s, d = x.shape

    def body(x_ref, g_ref, h_ref):
        xv = x_ref[...]
        h_ref[...] = (xv * _rsqrt_ms(xv) * g_ref[...]).astype(BF16)

    return _row_call(body, name, [(x, "row"), (g, "vec")],
                     [(jax.ShapeDtypeStruct((s, d), BF16), "row")], s=s)[0]


def _gate_fwd(o, z, name):
    s, d = o.shape

    def body(o_ref, z_ref, u_ref):
        zv = z_ref[...]
        u_ref[...] = (o_ref[...] * (zv * jax.nn.sigmoid(zv))).astype(BF16)

    return _row_call(body, name, [(o, "row"), (z, "row")],
                     [(jax.ShapeDtypeStruct((s, d), BF16), "row")], s=s)[0]


def _post_fwd(x, y, g, name):
    s, d = x.shape

    def body(x_ref, y_ref, g_ref, o_ref):
        yv = y_ref[...]
        o_ref[...] = x_ref[...] + yv * _rsqrt_ms(yv) * g_ref[...]

    return _row_call(body, name, [(x, "row"), (y, "row"), (g, "vec")],
                     [(jax.ShapeDtypeStruct((s, d), F32), "row")], s=s)[0]


def _loss_and_grad(x, target):
    s, d = x.shape

    def body(x_ref, t_ref, dx_ref, l_ref):
        err = x_ref[...] - t_ref[...]
        dx_ref[...] = err * (1.0 / d)
        part = jnp.sum(jnp.sum(err * err, axis=1, keepdims=True), axis=0, keepdims=True) * (0.5 / d)

        @pl.when(pl.program_id(0) == 0)
        def _():
            l_ref[...] = jnp.zeros_like(l_ref)

        l_ref[...] += jnp.broadcast_to(part, l_ref.shape)

    return _row_call(body, "loss_head", [(x, "row"), (target, "row")],
                     [(jax.ShapeDtypeStruct((s, d), F32), "row"),
                      (jax.ShapeDtypeStruct((1, LANES), F32), "vec")], s=s)


def _norm_bwd_rows(dn, v, g):
    r = _rsqrt_ms(v)
    a = dn * g
    dv = r * (a - v * (r * r) * jnp.mean(a * v, axis=-1, keepdims=True))
    return dv, dn * v * r


def _post_bwd(dx, y, g, name):
    s, d = dx.shape

    def body(dx_ref, y_ref, g_ref, dy_ref, dg_ref):
        dy, dg = _norm_bwd_rows(dx_ref[...], y_ref[...], g_ref[...])
        dy_ref[...] = dy.astype(BF16)

        @pl.when(pl.program_id(0) == 0)
        def _():
            dg_ref[...] = jnp.zeros_like(dg_ref)

        dg_ref[...] += jnp.sum(dg, axis=0, keepdims=True)

    return _row_call(body, name, [(dx, "row"), (y, "row"), (g, "vec")],
                     [(jax.ShapeDtypeStruct((s, d), BF16), "row"),
                      (jax.ShapeDtypeStruct((1, d), F32), "vec")], s=s)


def _gate_bwd(du, o, z, name):
    s, d = du.shape

    def body(du_ref, o_ref, z_ref, do_ref, dz_ref):
        duv, zv = du_ref[...], z_ref[...]
        sig = jax.nn.sigmoid(zv)
        do_ref[...] = (duv * (zv * sig)).astype(BF16)
        dz_ref[...] = (duv * o_ref[...] * (sig * (1.0 + zv * (1.0 - sig)))).astype(BF16)

    return _row_call(body, name, [(du, "row"), (o, "row"), (z, "row")],
                     [(jax.ShapeDtypeStruct((s, d), BF16), "row"),
                      (jax.ShapeDtypeStruct((s, d), BF16), "row")], s=s)


def _pre_bwd(dx, dhs, x, g, name):
    s, d = dx.shape
    n_dh = len(dhs)

    def body(*refs):
        dx_ref, dh_refs, (x_ref, g_ref, o_ref, dg_ref) = refs[0], refs[1:1 + n_dh], refs[1 + n_dh:]
        dh = dh_refs[0][...].astype(F32)
        for r in dh_refs[1:]:
            dh = dh + r[...].astype(F32)
        dv, dg = _norm_bwd_rows(dh, x_ref[...], g_ref[...])
        o_ref[...] = dx_ref[...] + dv

        @pl.when(pl.program_id(0) == 0)
        def _():
            dg_ref[...] = jnp.zeros_like(dg_ref)

        dg_ref[...] += jnp.sum(dg, axis=0, keepdims=True)

    return _row_call(body, name, [(dx, "row")] + [(h, "row") for h in dhs] + [(x, "row"), (g, "vec")],
                     [(jax.ShapeDtypeStruct((s, d), F32), "row"),
                      (jax.ShapeDtypeStruct((1, d), F32), "vec")], s=s)


def _lane_is_first_head():
    return lax.broadcasted_iota(jnp.int32, (1, LANES), 1) < HEAD_DIM


def _bcast_lanes(col):
    return jnp.broadcast_to(col, (col.shape[0], LANES))


def _pair_spec(s, off=0, width=LANES):
    return pl.BlockSpec((s, width), lambda p: (0, p + off))


def _rowsum_heads(prod, first):
    return (jnp.sum(jnp.where(first, prod, 0.0), axis=1, keepdims=True),
            jnp.sum(jnp.where(first, 0.0, prod), axis=1, keepdims=True))


def _softplus_parts(z):
    e = jnp.exp(-jnp.abs(z))
    sp = jnp.maximum(z, 0.0) + jnp.log(1.0 + e)
    r = 1.0 / (1.0 + e)
    return sp, jnp.where(z >= 0, r, e * r)


def _split_dot(x, t):
    hi = x.astype(BF16)
    lo = (x - hi.astype(F32)).astype(BF16)
    return jnp.dot(hi, t, preferred_element_type=F32) + jnp.dot(lo, t, preferred_element_type=F32)


def _sb_tile(s):
    return min(256, s)


def _attn_b_fwd(qkv, name):
    s = qkv.shape[0]
    t = _sb_tile(s)
    nq = s // t

    def body(q_ref, k_ref, v_ref, o_ref, lt_ref):
        first = _lane_is_first_head()
        row = lax.broadcasted_iota(jnp.int32, (t, t), 0)
        col = lax.broadcasted_iota(jnp.int32, (t, t), 1)
        before = col < row
        tri = (row >= col).astype(BF16)

        def tile(j, carry, diag, qa, qb):
            ca, cb, acc = carry
            c0 = pl.multiple_of(j * t, t)
            k2 = k_ref[pl.ds(c0, t), :]
            v2 = v_ref[pl.ds(c0, t), :]
            res = []
            for qh, c in ((qa, ca), (qb, cb)):
                z = lax.dot_general(qh, k2, _NT, preferred_element_type=F32)
                sp, _ = _softplus_parts(z)
                lf = jnp.where(before, -sp, 0.0) if diag else -sp
                incl = _split_dot(lf, tri)
                a = jnp.exp(z + c + incl)
                if diag:
                    a = jnp.where(before, a, 0.0)
                res.append((jnp.dot(a.astype(BF16), v2, preferred_element_type=F32), c + incl[:, 0:1]))
            (pa, ca), (pb, cb) = res
            return ca, cb, acc + jnp.where(first, pa, pb)

        def qblock(i, _):
            r0 = pl.multiple_of(i * t, t)
            q2 = q_ref[pl.ds(r0, t), :] * Q_SCALE
            qa = jnp.where(first, q2, 0).astype(BF16)
            qb = jnp.where(first, 0, q2).astype(BF16)
            zero = jnp.zeros((t, 1), F32)
            carry = tile(i, (zero, zero, jnp.zeros((t, LANES), F32)), True, qa, qb)
            carry = lax.fori_loop(0, i, lambda jj, c: tile(i - 1 - jj, c, False, qa, qb), carry)
            o_ref[pl.ds(r0, t), :] = carry[2]
            lt_ref[pl.ds(r0, t), 0:LANES] = _bcast_lanes(carry[0])
            lt_ref[pl.ds(r0, t), LANES:2 * LANES] = _bcast_lanes(carry[1])
            return 0

        lax.fori_loop(0, nq, qblock, 0)

    return pl.pallas_call(
        body, name=name, grid=(N_PAIRS,),
        in_specs=[_pair_spec(s), _pair_spec(s, N_PAIRS), _pair_spec(s, 2 * N_PAIRS)],
        out_specs=[_pair_spec(s), _stat_spec(s)],
        out_shape=[jax.ShapeDtypeStruct((s, BRANCH), F32), jax.ShapeDtypeStruct((s, N_HEADS * LANES), F32)],
        compiler_params=_params(("parallel",)),
    )(qkv, qkv, qkv)


def _attn_b_bwd(qkv, ltot, do, name):
    s = qkv.shape[0]
    t = _sb_tile(s)
    nq = s // t

    def body(q_ref, k_ref, v_ref, lt_ref, do_ref, dq_ref, dk_ref, dv_ref, dk_acc, dv_acc):
        first = _lane_is_first_head()
        row = lax.broadcasted_iota(jnp.int32, (t, t), 0)
        col = lax.broadcasted_iota(jnp.int32, (t, t), 1)
        before = col < row
        tri = (row <= col).astype(BF16)
        dk_acc[...] = jnp.zeros_like(dk_acc)
        dv_acc[...] = jnp.zeros_like(dv_acc)

        def tile(j, carry, diag, heads):
            pl_a, pg_a, pl_b, pg_b, dq_acc = carry
            c0 = pl.multiple_of(j * t, t)
            k2 = k_ref[pl.ds(c0, t), :]
            v2 = v_ref[pl.ds(c0, t), :]
            out = []
            dk_t = jnp.zeros((t, LANES), F32)
            dv_t = jnp.zeros((t, LANES), F32)
            dq_parts = []
            for (qh, doh, lt), (p_l, p_g) in zip(heads, ((pl_a, pg_a), (pl_b, pg_b))):
                z = lax.dot_general(qh, k2, _NT, preferred_element_type=F32)
                sp, sig = _softplus_parts(z)
                lf = jnp.where(before, -sp, 0.0) if diag else -sp
                pref_l = _split_dot(lf, tri)
                a = jnp.exp(z + ((lt - p_l) - pref_l + lf))
                if diag:
                    a = jnp.where(before, a, 0.0)
                da = lax.dot_general(doh, v2, _NT, preferred_element_type=F32)
                g = a * da
                pref_g = _split_dot(g, tri)
                dz = g - sig * (p_g + pref_g)
                if diag:
                    dz = jnp.where(before, dz, 0.0)
                dzb = dz.astype(BF16)
                dq_parts.append(jnp.dot(dzb, k2, preferred_element_type=F32))
                dk_t = dk_t + lax.dot_general(dzb, qh, _TN, preferred_element_type=F32)
                dv_t = dv_t + lax.dot_general(a.astype(BF16), doh, _TN, preferred_element_type=F32)
                out.append((p_l + pref_l[:, t - 1:t], p_g + pref_g[:, t - 1:t]))
            dk_acc[pl.ds(c0, t), :] += dk_t
            dv_acc[pl.ds(c0, t), :] += dv_t
            dq_acc = dq_acc + jnp.where(first, dq_parts[0], dq_parts[1])
            return out[0][0], out[0][1], out[1][0], out[1][1], dq_acc

        def qblock(i, _):
            r0 = pl.multiple_of(i * t, t)
            q2 = q_ref[pl.ds(r0, t), :] * Q_SCALE
            do2 = do_ref[pl.ds(r0, t), :]
            heads = ((jnp.where(first, q2, 0).astype(BF16), jnp.where(first, do2, 0).astype(BF16),
                      lt_ref[pl.ds(r0, t), 0:1]),
                     (jnp.where(first, 0, q2).astype(BF16), jnp.where(first, 0, do2).astype(BF16),
                      lt_ref[pl.ds(r0, t), LANES:LANES + 1]))
            zero = jnp.zeros((t, 1), F32)
            carry = (zero, zero, zero, zero, jnp.zeros((t, LANES), F32))
            carry = lax.fori_loop(0, i, lambda j, c: tile(j, c, False, heads), carry)
            carry = tile(i, carry, True, heads)
            dq_ref[pl.ds(r0, t), :] = (carry[4] * Q_SCALE).astype(BF16)
            return 0

        lax.fori_loop(0, nq, qblock, 0)
        dk_ref[...] = dk_acc[...].astype(BF16)
        dv_ref[...] = dv_acc[...].astype(BF16)

    out = jax.ShapeDtypeStruct((s, BRANCH), BF16)
    return pl.pallas_call(
        body, name=name, grid=(N_PAIRS,),
        in_specs=[_pair_spec(s), _pair_spec(s, N_PAIRS), _pair_spec(s, 2 * N_PAIRS), _stat_spec(s), _pair_spec(s)],
        out_specs=[_pair_spec(s)] * 3, out_shape=[out] * 3,
        scratch_shapes=[pltpu.VMEM((s, LANES), F32), pltpu.VMEM((s, LANES), F32)],
        compiler_params=_params(("parallel",)),
    )(qkv, qkv, qkv, ltot, do)


def _fox_tile(s):
    return min(256, s)


def _stat_spec(s):
    return pl.BlockSpec((s, 2 * LANES), lambda p: (0, p))


def _cum_spec(nt, t):
    return pl.BlockSpec((1, nt, 2, t), lambda p: (p, 0, 0, 0))


def _attn_c_fwd(qkv, cum4, name):
    s = qkv.shape[0]
    t = _fox_tile(s)
    nq = s // t

    def body(q_ref, k_ref, v_ref, c_ref, o_ref, lse_ref):
        first = _lane_is_first_head()
        row = lax.broadcasted_iota(jnp.int32, (t, t), 0)
        col = lax.broadcasted_iota(jnp.int32, (t, t), 1)
        causal = col <= row

        def tile(j, carry, diag, qa, qb):
            c0 = pl.multiple_of(j * t, t)
            k2 = k_ref[pl.ds(c0, t), :]
            v2 = v_ref[pl.ds(c0, t), :]
            cs = c_ref[0, j]
            m_a, l_a, m_b, l_b, acc = carry
            new = []
            for h, (qh, m_prev, l_prev) in enumerate(((qa, m_a, l_a), (qb, m_b, l_b))):
                sc = lax.dot_general(qh, k2, _NT, preferred_element_type=F32) - cs[h:h + 1, :]
                if diag:
                    sc = jnp.where(causal, sc, NEG)
                m_new = jnp.maximum(m_prev, jnp.max(sc, axis=1, keepdims=True))
                alpha = jnp.exp(m_prev - m_new)
                p = jnp.exp(sc - m_new)
                l_new = alpha * l_prev + jnp.sum(p, axis=1, keepdims=True)
                new.append((m_new, l_new, alpha, jnp.dot(p.astype(BF16), v2, preferred_element_type=F32)))
            (m_a, l_a, al_a, pv_a), (m_b, l_b, al_b, pv_b) = new
            acc = jnp.where(first, acc * al_a + pv_a, acc * al_b + pv_b)
            return m_a, l_a, m_b, l_b, acc

        def qblock(i, _):
            r0 = pl.multiple_of(i * t, t)
            q2 = q_ref[pl.ds(r0, t), :] * Q_SCALE
            qa = jnp.where(first, q2, 0).astype(BF16)
            qb = jnp.where(first, 0, q2).astype(BF16)
            neg = jnp.full((t, 1), NEG, F32)
            zero = jnp.zeros((t, 1), F32)
            carry = (neg, zero, neg, zero, jnp.zeros((t, LANES), F32))
            carry = lax.fori_loop(0, i, lambda j, c: tile(j, c, False, qa, qb), carry)
            m_a, l_a, m_b, l_b, acc = tile(i, carry, True, qa, qb)
            o_ref[pl.ds(r0, t), :] = acc * jnp.where(first, 1.0 / l_a, 1.0 / l_b)
            lse_ref[pl.ds(r0, t), 0:LANES] = _bcast_lanes(m_a + jnp.log(l_a))
            lse_ref[pl.ds(r0, t), LANES:2 * LANES] = _bcast_lanes(m_b + jnp.log(l_b))
            return 0

        lax.fori_loop(0, nq, qblock, 0)

    return pl.pallas_call(
        body, name=name, grid=(N_PAIRS,),
        in_specs=[_pair_spec(s), _pair_spec(s, N_PAIRS), _pair_spec(s, 2 * N_PAIRS), _cum_spec(nq, t)],
        out_specs=[_pair_spec(s), _stat_spec(s)],
        out_shape=[jax.ShapeDtypeStruct((s, BRANCH), F32), jax.ShapeDtypeStruct((s, N_HEADS * LANES), F32)],
        compiler_params=_params(("parallel",)),
    )(qkv, qkv, qkv, cum4)


def _attn_c_bwd(qkv, cum4, o, lse, do, name):
    s = qkv.shape[0]
    t = _fox_tile(s)
    nq = s // t

    def body(q_ref, k_ref, v_ref, c_ref, o_ref, lse_ref, do_ref, dq_ref, dk_ref, dv_ref, dc_ref, dk_acc, dv_acc):
        first = _lane_is_first_head()
        row = lax.broadcasted_iota(jnp.int32, (t, t), 0)
        col = lax.broadcasted_iota(jnp.int32, (t, t), 1)
        causal = col <= row
        eye = col == row
        dk_acc[...] = jnp.zeros_like(dk_acc)
        dv_acc[...] = jnp.zeros_like(dv_acc)
        dc_ref[...] = jnp.zeros_like(dc_ref)

        def tile(j, carry, diag, heads):
            dq_acc, rs_a, rs_b = carry
            c0 = pl.multiple_of(j * t, t)
            k2 = k_ref[pl.ds(c0, t), :]
            v2 = v_ref[pl.ds(c0, t), :]
            cs = c_ref[0, j]
            dk_t = jnp.zeros((t, LANES), F32)
            dv_t = jnp.zeros((t, LANES), F32)
            dq_parts, dc_rows, row_sums = [], [], []
            for h, (qh, doh, delta, lse_h) in enumerate(heads):
                sc = lax.dot_general(qh, k2, _NT, preferred_element_type=F32) - cs[h:h + 1, :]
                p = jnp.exp(sc - lse_h)
                if diag:
                    p = jnp.where(causal, p, 0.0)
                dp = lax.dot_general(doh, v2, _NT, preferred_element_type=F32)
                ds = p * (dp - delta)
                dsb = ds.astype(BF16)
                dq_parts.append(jnp.dot(dsb, k2, preferred_element_type=F32))
                dk_t = dk_t + lax.dot_general(dsb, qh, _TN, preferred_element_type=F32)
                dv_t = dv_t + lax.dot_general(p.astype(BF16), doh, _TN, preferred_element_type=F32)
                dc_rows.append(jnp.sum(ds, axis=0, keepdims=True))
                row_sums.append(jnp.sum(ds, axis=1, keepdims=True))
            dk_acc[pl.ds(c0, t), :] += dk_t
            dv_acc[pl.ds(c0, t), :] += dv_t
            dc_ref[0, j] = dc_ref[0, j] - jnp.concatenate(dc_rows, axis=0)
            return dq_acc + jnp.where(first, dq_parts[0], dq_parts[1]), rs_a + row_sums[0], rs_b + row_sums[1]

        def qblock(i, _):
            r0 = pl.multiple_of(i * t, t)
            q2 = q_ref[pl.ds(r0, t), :] * Q_SCALE
            do2 = do_ref[pl.ds(r0, t), :]
            delta_a, delta_b = _rowsum_heads(do2.astype(F32) * o_ref[pl.ds(r0, t), :], first)
            heads = ((jnp.where(first, q2, 0).astype(BF16), jnp.where(first, do2, 0).astype(BF16), delta_a,
                      lse_ref[pl.ds(r0, t), 0:1]),
                     (jnp.where(first, 0, q2).astype(BF16), jnp.where(first, 0, do2).astype(BF16), delta_b,
                      lse_ref[pl.ds(r0, t), LANES:LANES + 1]))
            zero = jnp.zeros((t, 1), F32)
            carry = lax.fori_loop(0, i, lambda j, c: tile(j, c, False, heads), (jnp.zeros((t, LANES), F32), zero, zero))
            dq_acc, rs_a, rs_b = tile(i, carry, True, heads)
            dq_ref[pl.ds(r0, t), :] = (dq_acc * Q_SCALE).astype(BF16)
            as_row = lambda col_vec: jnp.sum(jnp.where(eye, col_vec, 0.0), axis=0, keepdims=True)
            dc_ref[0, i] = dc_ref[0, i] + jnp.concatenate([as_row(rs_a), as_row(rs_b)], axis=0)
            return 0

        lax.fori_loop(0, nq, qblock, 0)
        dk_ref[...] = dk_acc[...].astype(BF16)
        dv_ref[...] = dv_acc[...].astype(BF16)

    out = jax.ShapeDtypeStruct((s, BRANCH), BF16)
    return pl.pallas_call(
        body, name=name, grid=(N_PAIRS,),
        in_specs=[_pair_spec(s), _pair_spec(s, N_PAIRS), _pair_spec(s, 2 * N_PAIRS), _cum_spec(nq, t),
                  _pair_spec(s), _stat_spec(s), _pair_spec(s)],
        out_specs=[_pair_spec(s)] * 3 + [_cum_spec(nq, t)],
        out_shape=[out] * 3 + [jax.ShapeDtypeStruct(cum4.shape, F32)],
        scratch_shapes=[pltpu.VMEM((s, LANES), F32), pltpu.VMEM((s, LANES), F32)],
        compiler_params=_params(("parallel",)),
    )(qkv, qkv, qkv, cum4, o, lse, do)


FG_CHUNK = 512


def _tri_dot3(x, t):
    hi = x.astype(BF16)
    r1 = x - hi.astype(F32)
    mid = r1.astype(BF16)
    lo = (r1 - mid.astype(F32)).astype(BF16)
    return (jnp.dot(hi, t, preferred_element_type=F32) + jnp.dot(mid, t, preferred_element_type=F32)
            + jnp.dot(lo, t, preferred_element_type=F32))


def _fgate_fwd(h, wf_t, b_col, name):
    s = h.shape[0]
    c = min(FG_CHUNK, s)

    def body(h_ref, w_ref, b_ref, xf_ref, cum_ref, carry_ref):
        @pl.when(pl.program_id(0) == 0)
        def _():
            carry_ref[...] = jnp.zeros_like(carry_ref)

        xf = lax.dot_general(w_ref[...], h_ref[...], _NT, preferred_element_type=F32) + b_ref[:, 0:1]
        xf_ref[...] = xf
        logf = jnp.minimum(xf, 0.0) - jnp.log(1.0 + jnp.exp(-jnp.abs(xf)))
        row = lax.broadcasted_iota(jnp.int32, (c, c), 0)
        col = lax.broadcasted_iota(jnp.int32, (c, c), 1)
        cum = _tri_dot3(logf, (row <= col).astype(BF16)) + carry_ref[:, 0:1]
        cum_ref[...] = cum
        carry_ref[...] = _bcast_lanes(cum[:, c - 1:c])

    out = jax.ShapeDtypeStruct((N_HEADS, s), F32)
    return pl.pallas_call(
        body, name=name, grid=(s // c,),
        in_specs=[pl.BlockSpec((c, D_MODEL), lambda i: (i, 0)),
                  pl.BlockSpec((N_HEADS, D_MODEL), lambda i: (0, 0)),
                  pl.BlockSpec((N_HEADS, LANES), lambda i: (0, 0))],
        out_specs=[pl.BlockSpec((N_HEADS, c), lambda i: (0, i))] * 2,
        out_shape=[out, out],
        scratch_shapes=[pltpu.VMEM((N_HEADS, LANES), F32)],
        compiler_params=_params(("arbitrary",)),
    )(h, wf_t, b_col)


def _fgate_bwd(dcum, xf, h, wf_t, name):
    s = h.shape[0]
    c = min(FG_CHUNK, s)
    n = s // c

    def body(dc_ref, xf_ref, h_ref, w_ref, dw_ref, dh_ref, db_ref, carry_ref):
        @pl.when(pl.program_id(0) == 0)
        def _():
            carry_ref[...] = jnp.zeros_like(carry_ref)
            dw_ref[...] = jnp.zeros_like(dw_ref)
            db_ref[...] = jnp.zeros_like(db_ref)

        row = lax.broadcasted_iota(jnp.int32, (c, c), 0)
        col = lax.broadcasted_iota(jnp.int32, (c, c), 1)
        dlogf = _tri_dot3(dc_ref[...], (row >= col).astype(BF16)) + carry_ref[:, 0:1]
        carry_ref[...] = _bcast_lanes(dlogf[:, 0:1])
        xf = xf_ref[...]
        e = jnp.exp(-jnp.abs(xf))
        r = 1.0 / (1.0 + e)
        dxf = dlogf * jnp.where(xf >= 0, e * r, r)
        db_ref[...] += _bcast_lanes(jnp.sum(dxf, axis=1, keepdims=True))
        dxb = dxf.astype(BF16)
        dw_ref[...] += jnp.dot(dxb, h_ref[...], preferred_element_type=F32)
        dh_ref[...] = lax.dot_general(dxb, w_ref[...], _TN, preferred_element_type=F32)

    rev = lambda i: n - 1 - i
    return pl.pallas_call(
        body, name=name, grid=(n,),
        in_specs=[pl.BlockSpec((N_HEADS, c), lambda i: (0, rev(i))),
                  pl.BlockSpec((N_HEADS, c), lambda i: (0, rev(i))),
                  pl.BlockSpec((c, D_MODEL), lambda i: (rev(i), 0)),
                  pl.BlockSpec((N_HEADS, D_MODEL), lambda i: (0, 0))],
        out_specs=[pl.BlockSpec((N_HEADS, D_MODEL), lambda i: (0, 0)),
                   pl.BlockSpec((c, D_MODEL), lambda i: (rev(i), 0)),
                   pl.BlockSpec((N_HEADS, LANES), lambda i: (0, 0))],
        out_shape=[jax.ShapeDtypeStruct((N_HEADS, D_MODEL), F32), jax.ShapeDtypeStruct((s, D_MODEL), F32),
                   jax.ShapeDtypeStruct((N_HEADS, LANES), F32)],
        scratch_shapes=[pltpu.VMEM((N_HEADS, LANES), F32)],
        compiler_params=_params(("arbitrary",)),
    )(dcum, xf, h, wf_t)


def _to_cum4(v, t):
    s = v.shape[1]
    return v.reshape(N_PAIRS, 2, s // t, t).transpose(0, 2, 1, 3)


def _from_cum4(v4):
    p, nt, two, t = v4.shape
    return v4.transpose(0, 2, 1, 3).reshape(p * two, nt * t)


def _alibi_slopes():
    return (2.0 ** (-8.0 * np.arange(1, N_HEADS + 1, dtype=np.float32) / N_HEADS)).astype(np.float32)


def _per_head_lanes(v):
    return jnp.repeat(v.astype(F32).reshape(N_PAIRS, 1, 2), LANES, axis=2)


def _attn_a_specs(s):
    q = _pair_spec(s)
    k = pl.BlockSpec((s, LANES), lambda p: (0, N_PAIRS + p // 8))
    v = pl.BlockSpec((s, LANES), lambda p: (0, N_PAIRS + KV_A // LANES + p // 8))
    head = pl.BlockSpec((1, 1, 2 * LANES), lambda p: (p, 0, 0))
    return q, k, v, head


def _attn_a_geometry(p):
    kv_half = (p // 4) % 2
    kv_first = kv_half == 0
    lane_first = _lane_is_first_head()
    kv_lanes = (lax.broadcasted_iota(jnp.int32, (1, LANES), 1) // HEAD_DIM) == kv_half
    ti = lax.broadcasted_iota(jnp.int32, (WINDOW, 2 * WINDOW), 0)
    cj = lax.broadcasted_iota(jnp.int32, (WINDOW, 2 * WINDOW), 1)
    dist = WINDOW + ti - cj
    valid = (dist >= 0) & (dist < WINDOW)
    return kv_first, lane_first, kv_lanes, dist.astype(F32), valid


def _swap_halves(x):
    return pltpu.roll(x, HEAD_DIM, 1)


def _attn_a_fwd(qkv, slopes, sinks, name):
    s = qkv.shape[0]
    nb = s // WINDOW

    def body(q_ref, k_ref, v_ref, sl_ref, sk_ref, o_ref, lse_ref):
        kv_first, lane_first, kv_lanes, dist, valid = _attn_a_geometry(pl.program_id(0))
        slope = (sl_ref[0, :, 0:1], sl_ref[0, :, LANES:LANES + 1])
        sink = (sk_ref[0, :, 0:1], sk_ref[0, :, LANES:LANES + 1])

        def block(n, r0, k0, width):
            q2 = q_ref[pl.ds(r0, WINDOW), :].astype(F32) * Q_SCALE
            q2r = _swap_halves(q2)
            x = (jnp.where(kv_first, q2, q2r).astype(BF16), jnp.where(kv_first, q2r, q2).astype(BF16))
            km = jnp.where(kv_lanes, k_ref[pl.ds(k0, width), :], 0).astype(BF16)
            vm = jnp.where(kv_lanes, v_ref[pl.ds(k0, width), :], 0).astype(BF16)
            dist_w, valid_w = dist[:, 2 * WINDOW - width:], valid[:, 2 * WINDOW - width:]
            outs = []
            for h in range(2):
                sc = lax.dot_general(x[h], km, _NT, preferred_element_type=F32) - slope[h] * dist_w
                sc = jnp.where(valid_w, sc, NEG)
                m = jnp.maximum(jnp.max(sc, axis=1, keepdims=True), sink[h])
                pr = jnp.exp(sc - m)
                l = jnp.sum(pr, axis=1, keepdims=True) + jnp.exp(sink[h] - m)
                oh = jnp.dot(pr.astype(BF16), vm, preferred_element_type=F32) * (1.0 / l)
                outs.append(oh)
                lse_ref[pl.ds(r0, WINDOW), h * LANES:(h + 1) * LANES] = _bcast_lanes(m + jnp.log(l))
            oa = jnp.where(kv_first, outs[0], _swap_halves(outs[0]))
            ob = jnp.where(kv_first, _swap_halves(outs[1]), outs[1])
            o_ref[pl.ds(r0, WINDOW), :] = jnp.where(lane_first, oa, ob)

        block(0, 0, 0, WINDOW)

        def loop(n, _):
            r0 = pl.multiple_of(n * WINDOW, WINDOW)
            block(n, r0, pl.multiple_of(r0 - WINDOW, WINDOW), 2 * WINDOW)
            return 0

        lax.fori_loop(1, nb, loop, 0)

    q, k, v, head = _attn_a_specs(s)
    return pl.pallas_call(
        body, name=name, grid=(N_PAIRS,),
        in_specs=[q, k, v, head, head],
        out_specs=[_pair_spec(s), _stat_spec(s)],
        out_shape=[jax.ShapeDtypeStruct((s, BRANCH), F32), jax.ShapeDtypeStruct((s, N_HEADS * LANES), F32)],
        compiler_params=_params(("parallel",)),
    )(qkv, qkv, qkv, slopes, sinks)


def _attn_a_bwd(qkv, slopes, sinks, o, lse, do, name):
    s = qkv.shape[0]
    nb = s // WINDOW

    def body(q_ref, k_ref, v_ref, sl_ref, sk_ref, o_ref, lse_ref, do_ref, dq_ref, dk_ref, dv_ref, dsk_ref):
        p_id = pl.program_id(0)
        kv_first, lane_first, kv_lanes, dist, valid = _attn_a_geometry(p_id)
        slope = (sl_ref[0, :, 0:1], sl_ref[0, :, LANES:LANES + 1])
        sink = (sk_ref[0, :, 0:1], sk_ref[0, :, LANES:LANES + 1])

        @pl.when(p_id % 8 == 0)
        def _():
            dk_ref[...] = jnp.zeros_like(dk_ref)
            dv_ref[...] = jnp.zeros_like(dv_ref)

        def align(v2):
            v2r = _swap_halves(v2)
            return jnp.where(kv_first, v2, v2r), jnp.where(kv_first, v2r, v2)

        def block(r0, k0, width, dsink):
            q_al = align(q_ref[pl.ds(r0, WINDOW), :].astype(F32) * Q_SCALE)
            do2 = do_ref[pl.ds(r0, WINDOW), :].astype(F32)
            do_al = align(do2)
            delta = _rowsum_heads(do2 * o_ref[pl.ds(r0, WINDOW), :], lane_first)
            km = jnp.where(kv_lanes, k_ref[pl.ds(k0, width), :], 0).astype(BF16)
            vm = jnp.where(kv_lanes, v_ref[pl.ds(k0, width), :], 0).astype(BF16)
            dist_w, valid_w = dist[:, 2 * WINDOW - width:], valid[:, 2 * WINDOW - width:]
            dk_t = jnp.zeros((width, LANES), F32)
            dv_t = jnp.zeros((width, LANES), F32)
            dq_al, new_dsink = [], []
            for h in range(2):
                lse_h = lse_ref[pl.ds(r0, WINDOW), h * LANES:h * LANES + 1]
                xq = jnp.where(kv_lanes, q_al[h], 0.0).astype(BF16)
                xdo = jnp.where(kv_lanes, do_al[h], 0.0).astype(BF16)
                sc = lax.dot_general(xq, km, _NT, preferred_element_type=F32) - slope[h] * dist_w
                pr = jnp.where(valid_w, jnp.exp(sc - lse_h), 0.0)
                dp = lax.dot_general(xdo, vm, _NT, preferred_element_type=F32)
                ds = pr * (dp - delta[h])
                dsb = ds.astype(BF16)
                dq_al.append(jnp.dot(dsb, km, preferred_element_type=F32))
                dk_t = dk_t + lax.dot_general(dsb, xq, _TN, preferred_element_type=F32)
                dv_t = dv_t + lax.dot_general(pr.astype(BF16), xdo, _TN, preferred_element_type=F32)
                new_dsink.append(dsink[h] - jnp.sum(jnp.exp(sink[h] - lse_h) * delta[h], axis=0, keepdims=True))
            dk_ref[pl.ds(k0, width), :] += dk_t
            dv_ref[pl.ds(k0, width), :] += dv_t
            dqa = jnp.where(kv_first, dq_al[0], _swap_halves(dq_al[0]))
            dqb = jnp.where(kv_first, _swap_halves(dq_al[1]), dq_al[1])
            dq_ref[pl.ds(r0, WINDOW), :] = (jnp.where(lane_first, dqa, dqb) * Q_SCALE).astype(BF16)
            return tuple(new_dsink)

        zero = jnp.zeros((1, 1), F32)
        dsink = block(0, 0, WINDOW, (zero, zero))

        def loop(n, c):
            r0 = pl.multiple_of(n * WINDOW, WINDOW)
            return block(r0, pl.multiple_of(r0 - WINDOW, WINDOW), 2 * WINDOW, c)

        dsink = lax.fori_loop(1, nb, loop, dsink)
        dsk_ref[0, :, 0:LANES] = jnp.broadcast_to(dsink[0], (1, LANES))
        dsk_ref[0, :, LANES:2 * LANES] = jnp.broadcast_to(dsink[1], (1, LANES))

    q, k, v, head = _attn_a_specs(s)
    kv_out = pl.BlockSpec((s, LANES), lambda p: (0, p // 8))
    return pl.pallas_call(
        body, name=name, grid=(N_PAIRS,),
        in_specs=[q, k, v, head, head, _pair_spec(s), _stat_spec(s), _pair_spec(s)],
        out_specs=[_pair_spec(s), kv_out, kv_out, head],
        out_shape=[jax.ShapeDtypeStruct((s, BRANCH), BF16), jax.ShapeDtypeStruct((s, KV_A), F32),
                   jax.ShapeDtypeStruct((s, KV_A), F32), jax.ShapeDtypeStruct((N_PAIRS, 1, 2 * LANES), F32)],
        compiler_params=_params(("arbitrary",)),
    )(qkv, qkv, qkv, slopes, sinks, o, lse, do)


def _layer_kind(i):
    return i % 3, i // 3


def _forward_backward(x, target, g_pre, g_post, sinks_a, b_f_c, w_in, w_out, wf_t):
    s = x.shape[0]
    slopes = _per_head_lanes(jnp.asarray(_alibi_slopes()))
    saved = []
    for i in range(DEPTH):
        kind, j = _layer_kind(i)
        tag = f"l{i}"
        w = w_in[kind][j]
        nqkv = A_QKV if kind == 0 else B_QKV
        tn = 512 if kind == 0 else 1024
        h = _rmsnorm_fwd(x, g_pre[i:i + 1], f"prenorm_{tag}")
        qkv = _matmul(h, w, mode="nn", out_dtype=BF16, name=f"inproj_qkv_{tag}", n=nqkv, tn=tn)
        z = _matmul(h, w, mode="nn", out_dtype=F32, name=f"inproj_gate_{tag}", n=BRANCH, b_off=nqkv // tn, tn=tn)
        extra = None
        if kind == 0:
            sink_l = _per_head_lanes(sinks_a[j])
            o, lse = _attn_a_fwd(qkv, slopes, sink_l, f"attn_a_fwd_{tag}")
            extra = (sink_l, lse)
        elif kind == 1:
            o, extra = _attn_b_fwd(qkv, f"attn_b_fwd_{tag}")
        else:
            b_col = jnp.broadcast_to(b_f_c[j].astype(F32)[:, None], (N_HEADS, LANES))
            xf, cum = _fgate_fwd(h, wf_t[j], b_col, f"fgate_fwd_{tag}")
            cum4 = _to_cum4(cum, _fox_tile(s))
            o, lse = _attn_c_fwd(qkv, cum4, f"attn_c_fwd_{tag}")
            extra = (xf, cum4, lse)
        u = _gate_fwd(o, z, f"gate_{tag}")
        y = _matmul(u, w_out[kind][j], mode="nn", out_dtype=F32, name=f"outproj_{tag}")
        saved.append((x, h, qkv, z, o, u, y, extra))
        x = _post_fwd(x, y, g_post[i:i + 1], f"postnorm_{tag}")

    dx, loss_part = _loss_and_grad(x, target)

    d_g_pre, d_g_post = [None] * DEPTH, [None] * DEPTH
    d_w_in = {0: [None, None], 1: [None], 2: [None]}
    d_w_out = {0: [None, None], 1: [None], 2: [None]}
    d_sinks = [None, None]
    d_b_f = None
    for i in reversed(range(DEPTH)):
        kind, j = _layer_kind(i)
        tag = f"l{i}"
        x_in, h, qkv, z, o, u, y, extra = saved[i]
        tn = 512 if kind == 0 else 1024
        dy, d_g_post[i] = _post_bwd(dx, y, g_post[i:i + 1], f"postnorm_bwd_{tag}")
        d_w_out[kind][j] = _matmul(u, dy, mode="tn", out_dtype=F32, name=f"dw_out_{tag}", tk=512)
        du = _matmul(dy, w_out[kind][j], mode="nt", out_dtype=F32, name=f"d_gated_{tag}")
        do, dz = _gate_bwd(du, o, z, f"gate_bwd_{tag}")
        dhs = []
        if kind == 0:
            sink_l, lse = extra
            dq, dk, dv, dsk = _attn_a_bwd(qkv, slopes, sink_l, o, lse, do, f"attn_a_bwd_{tag}")
            d_sinks[j] = dsk[:, 0, ::LANES].reshape(N_HEADS)
            parts = [dq, dk.astype(BF16), dv.astype(BF16), dz]
        elif kind == 1:
            dq, dk, dv = _attn_b_bwd(qkv, extra, do, f"attn_b_bwd_{tag}")
            parts = [dq, dk, dv, dz]
        else:
            xf, cum4, lse = extra
            dq, dk, dv, dcum4 = _attn_c_bwd(qkv, cum4, o, lse, do, f"attn_c_bwd_{tag}")
            d_wf_t, dh_f, db = _fgate_bwd(_from_cum4(dcum4), xf, h, wf_t[j], f"fgate_bwd_{tag}")
            d_b_f = db[:, 0]
            dhs.append(dh_f)
            parts = [dq, dk, dv, dz]
        dproj = jnp.concatenate(parts, axis=1)
        dw = _matmul(h, dproj, mode="tn", out_dtype=F32, name=f"dw_in_{tag}", tk=512, tn=tn)
        if kind == 2:
            dw = jnp.concatenate([dw, d_wf_t.T], axis=1)
        d_w_in[kind][j] = dw
        dhs.insert(0, _matmul(dproj, w_in[kind][j], mode="nt", out_dtype=F32, name=f"dh_{tag}", tk=512))
        dx, d_g_pre[i] = _pre_bwd(dx, dhs, x_in, g_pre[i:i + 1], f"prenorm_bwd_{tag}")

    return dict(loss=loss_part, dx=dx, g_pre=jnp.concatenate(d_g_pre, axis=0), g_post=jnp.concatenate(d_g_post, axis=0),
                sinks_a=jnp.stack(d_sinks), b_f_c=d_b_f[None, :], w_in=d_w_in, w_out=d_w_out)


def _place():
    x, y, c = lax.axis_index("x"), lax.axis_index("y"), lax.axis_index("c")
    others = [(1 - x, y), (x, 1 - y), (1 - x, 1 - y)]
    return x, y, c, others


def _half_rows(ref_rows, which):
    half = ref_rows // 2
    return pl.ds(pl.multiple_of(which * half, half), half)


def _remote(src, dst, sems, k, device):
    send, recv = sems
    return pltpu.make_async_remote_copy(src_ref=src, dst_ref=dst, send_sem=send.at[k], recv_sem=recv.at[k],
                                        device_id=device, device_id_type=MESH)


def _hbm_call(body, name, ins, out_shapes, n_remote, n_local):
    any_spec = pl.BlockSpec(memory_space=pl.ANY)
    return pl.pallas_call(
        body, name=name, in_specs=[any_spec] * len(ins), out_specs=[any_spec] * len(out_shapes),
        out_shape=out_shapes,
        scratch_shapes=[pltpu.SemaphoreType.DMA((n_remote,)), pltpu.SemaphoreType.DMA((n_remote,)),
                        pltpu.SemaphoreType.DMA((n_local,))],
    )(*ins)


def _all_gather_shards(shards):
    n = len(shards)

    def body(*refs):
        ins, outs, sems, loc = refs[:n], refs[n:2 * n], refs[2 * n:2 * n + 2], refs[2 * n + 2]
        x, y, c, others = _place()
        me = 2 * x + y
        local, sends = [], []
        for w in range(n):
            rows = ins[w].shape[1]
            mine = _half_rows(rows, c)
            lc = pltpu.make_async_copy(ins[w], outs[w].at[:, me], loc.at[w])
            lc.start()
            local.append(lc)
            for j, (px, py) in enumerate(others):
                cp = _remote(ins[w].at[:, mine], outs[w].at[:, me, mine], sems, 6 * w + j, (px, py, c))
                cp.start()
                sends.append(cp)
        for w in range(n):
            mine = _half_rows(ins[w].shape[1], c)
            for j, (px, py) in enumerate(others):
                landed = outs[w].at[:, 2 * px + py, mine]
                _remote(landed, landed, sems, 6 * w + j, (px, py, c)).wait_recv()
                fw = _remote(landed, landed, sems, 6 * w + 3 + j, (x, y, 1 - c))
                fw.start()
                sends.append(fw)
        for w in range(n):
            theirs = _half_rows(ins[w].shape[1], 1 - c)
            for j, (px, py) in enumerate(others):
                landed = outs[w].at[:, 2 * px + py, theirs]
                _remote(landed, landed, sems, 6 * w + 3 + j, (x, y, 1 - c)).wait_recv()
        for cp in sends:
            cp.wait_send()
        for lc in local:
            lc.wait()

    out_shapes = [jax.ShapeDtypeStruct((a.shape[0], 4) + a.shape[1:], a.dtype) for a in shards]
    return _hbm_call(body, "all_gather_weights", shards, out_shapes, 6 * n, n)


def _sibling_split(parts):
    n = len(parts)

    def body(*refs):
        ins, outs, sems, loc = refs[:n], refs[n:3 * n], refs[3 * n:3 * n + 2], refs[3 * n + 2]
        x, y, c, _ = _place()
        pend = []
        for w in range(n):
            rows = ins[w].shape[2]
            lc = pltpu.make_async_copy(ins[w].at[:, :, _half_rows(rows, c)], outs[2 * w], loc.at[w])
            lc.start()
            cp = _remote(ins[w].at[:, :, _half_rows(rows, 1 - c)], outs[2 * w + 1], sems, w, (x, y, 1 - c))
            cp.start()
            pend.append((lc, cp))
        for lc, cp in pend:
            cp.wait_recv()
            cp.wait_send()
            lc.wait()

    out_shapes = []
    for a in parts:
        sh = jax.ShapeDtypeStruct((a.shape[0], 4, a.shape[2] // 2, a.shape[3]), a.dtype)
        out_shapes += [sh, sh]
    res = _hbm_call(body, "grad_sibling_exchange", parts, out_shapes, n, n)
    return res[0::2], res[1::2]


def _chip_scatter(sums):
    n = len(sums)

    def body(*refs):
        ins, outs, sems, loc = refs[:n], refs[n:2 * n], refs[2 * n:2 * n + 2], refs[2 * n + 2]
        x, y, c, others = _place()
        me = 2 * x + y
        local, sends = [], []
        for w in range(n):
            lc = pltpu.make_async_copy(ins[w].at[:, me], outs[w].at[:, me], loc.at[w])
            lc.start()
            local.append(lc)
            for j, (px, py) in enumerate(others):
                cp = _remote(ins[w].at[:, 2 * px + py], outs[w].at[:, me], sems, 3 * w + j, (px, py, c))
                cp.start()
                sends.append(cp)
        for w in range(n):
            for j, (px, py) in enumerate(others):
                landed = outs[w].at[:, 2 * px + py]
                _remote(landed, landed, sems, 3 * w + j, (px, py, c)).wait_recv()
        for cp in sends:
            cp.wait_send()
        for lc in local:
            lc.wait()

    out_shapes = [jax.ShapeDtypeStruct(a.shape, a.dtype) for a in sums]
    return _hbm_call(body, "grad_chip_scatter", sums, out_shapes, 3 * n, n)


def _sibling_join(halves):
    n = len(halves)

    def body(*refs):
        ins, outs, sems, loc = refs[:n], refs[n:2 * n], refs[2 * n:2 * n + 2], refs[2 * n + 2]
        x, y, c, _ = _place()
        pend = []
        for w in range(n):
            rows = outs[w].shape[1]
            mine, theirs = _half_rows(rows, c), _half_rows(rows, 1 - c)
            lc = pltpu.make_async_copy(ins[w], outs[w].at[:, mine], loc.at[w])
            lc.start()
            cp = _remote(ins[w], outs[w].at[:, mine], sems, w, (x, y, 1 - c))
            cp.start()
            pend.append((lc, cp, _remote(ins[w], outs[w].at[:, theirs], sems, w, (x, y, 1 - c))))
        for lc, cp, landed in pend:
            landed.wait_recv()
            cp.wait_send()
            lc.wait()

    out_shapes = [jax.ShapeDtypeStruct((a.shape[0], 2 * a.shape[1], a.shape[2]), a.dtype) for a in halves]
    return _hbm_call(body, "grad_sibling_join", halves, out_shapes, n, n)


SMALL_ROWS = 136


def _all_reduce_small(vec):
    def body(v_ref, o_ref, buf, send, recv, loc):
        x, y, c, _ = _place()
        me = 4 * x + 2 * y + c
        lc = pltpu.make_async_copy(v_ref, buf.at[me], loc.at[0])
        lc.start()
        cps = []
        for k in range(1, 8):
            fx, fy, fc = (k >> 2) & 1, (k >> 1) & 1, k & 1
            peer = (x ^ fx, y ^ fy, c ^ fc)
            cp = pltpu.make_async_remote_copy(src_ref=v_ref, dst_ref=buf.at[me], send_sem=send.at[k - 1],
                                              recv_sem=recv.at[k - 1], device_id=peer, device_id_type=MESH)
            cp.start()
            cps.append((cp, 4 * peer[0] + 2 * peer[1] + peer[2]))
        for k, (cp, src) in enumerate(cps):
            pltpu.make_async_remote_copy(src_ref=v_ref, dst_ref=buf.at[src], send_sem=send.at[k], recv_sem=recv.at[k],
                                         device_id=(x, y, c), device_id_type=MESH).wait_recv()
        for cp, _ in cps:
            cp.wait_send()
        lc.wait()
        total = buf[0]
        for k in range(1, 8):
            total = total + buf[k]
        o_ref[...] = total

    vm = pl.BlockSpec(memory_space=pltpu.VMEM)
    return pl.pallas_call(
        body, name="all_reduce_small", in_specs=[vm], out_specs=vm,
        out_shape=jax.ShapeDtypeStruct(vec.shape, F32),
        scratch_shapes=[pltpu.VMEM((8,) + vec.shape, F32), pltpu.SemaphoreType.DMA((7,)),
                        pltpu.SemaphoreType.DMA((7,)), pltpu.SemaphoreType.DMA((1,))],
    )(vec)


SUM_ROWS = 256


def _add_pairs(a, b, name):
    l, four, rh, cc = a.shape
    tr = min(SUM_ROWS, rh)

    def body(a_ref, b_ref, o_ref):
        o_ref[...] = (a_ref[...].astype(F32) + b_ref[...].astype(F32)).astype(o_ref.dtype)

    spec = pl.BlockSpec((1, 1, tr, cc), lambda i, k, r: (i, k, r, 0))
    return pl.pallas_call(
        body, name=name, grid=(l, four, rh // tr), in_specs=[spec, spec], out_specs=spec,
        out_shape=jax.ShapeDtypeStruct(a.shape, a.dtype),
        compiler_params=_params(("parallel", "parallel", "parallel")),
    )(a, b)


def _sum_chips(r, name):
    l, four, rh, cc = r.shape
    tr = min(SUM_ROWS, rh)

    def body(r_ref, o_ref):
        tot = r_ref[0, 0].astype(F32)
        for k in range(1, 4):
            tot = tot + r_ref[0, k].astype(F32)
        o_ref[0] = tot

    return pl.pallas_call(
        body, name=name, grid=(l, rh // tr),
        in_specs=[pl.BlockSpec((1, 4, tr, cc), lambda i, r_: (i, 0, r_, 0))],
        out_specs=pl.BlockSpec((1, tr, cc), lambda i, r_: (i, r_, 0)),
        out_shape=jax.ShapeDtypeStruct((l, rh, cc), F32),
        compiler_params=_params(("parallel", "parallel")),
    )(r)


ADAM_ROWS = 256


def _adamw(w, g, m, v, name):
    shape = w.shape
    cc = shape[-1]
    flat = lambda a: a.reshape(-1, cc)
    rows = flat(w).shape[0]
    tr = min(ADAM_ROWS, rows)
    assert rows % tr == 0
    c1 = 1.0 - ADAM_B1 ** ADAM_STEP
    c2 = 1.0 - ADAM_B2 ** ADAM_STEP

    def body(w_ref, g_ref, m_ref, v_ref, d_ref, nm_ref, nv_ref):
        gv = g_ref[...]
        nm = ADAM_B1 * m_ref[...] + (1.0 - ADAM_B1) * gv
        nv = ADAM_B2 * v_ref[...] + (1.0 - ADAM_B2) * (gv * gv)
        nm_ref[...] = nm
        nv_ref[...] = nv
        d_ref[...] = -ADAM_LR * ((nm / c1) / (jnp.sqrt(nv / c2) + ADAM_EPS) + ADAM_WD * w_ref[...])

    spec = pl.BlockSpec((tr, cc), lambda i: (i, 0))
    sh = jax.ShapeDtypeStruct((rows, cc), F32)
    outs = pl.pallas_call(
        body, name=name, grid=(rows // tr,), in_specs=[spec] * 4, out_specs=[spec] * 3, out_shape=[sh] * 3,
        compiler_params=_params(("parallel",)),
    )(flat(w), flat(g), flat(m), flat(v))
    return [o.reshape(shape) for o in outs]


def _pack_small(g_pre, g_post, sinks_a, b_f_c, loss_row):
    pad = lambda a: jnp.pad(a.reshape(1, -1).astype(F32), ((0, 0), (0, LANES - a.size)))
    rows = [g_pre.astype(F32).reshape(-1, LANES), g_post.astype(F32).reshape(-1, LANES), pad(sinks_a), pad(b_f_c), loss_row]
    packed = jnp.concatenate(rows, axis=0)
    return jnp.pad(packed, ((0, SMALL_ROWS - packed.shape[0]), (0, 0)))


def _unpack_small(p):
    n = DEPTH * D_MODEL // LANES
    return (p[:n].reshape(DEPTH, D_MODEL), p[n:2 * n].reshape(DEPTH, D_MODEL), p[2 * n, :2 * N_HEADS].reshape(2, N_HEADS),
            p[2 * n + 1, :N_HEADS].reshape(1, N_HEADS), p[2 * n + 2, 0])


def kernel(x, g_pre, g_post, w_in_a, w_out_a, sinks_a, w_in_b, w_out_b, w_in_c, b_f_c, w_out_c, loss_target, m_g_pre, m_g_post, m_w_in_a, m_w_out_a, m_sinks_a, m_w_in_b, m_w_out_b, m_w_in_c, m_b_f_c, m_w_out_c, v_g_pre, v_g_post, v_w_in_a, v_w_out_a, v_sinks_a, v_w_in_b, v_w_out_b, v_w_in_c, v_b_f_c, v_w_out_c):
    big_w = [w_in_a, w_out_a, w_in_b, w_out_b, w_in_c, w_out_c]
    big_m = [m_w_in_a, m_w_out_a, m_w_in_b, m_w_out_b, m_w_in_c, m_w_out_c]
    big_v = [v_w_in_a, v_w_out_a, v_w_in_b, v_w_out_b, v_w_in_c, v_w_out_c]

    gathered = _all_gather_shards([w.astype(BF16) for w in big_w])
    cols = lambda g, j: g[j].transpose(1, 0, 2).reshape(g.shape[2], 4 * g.shape[3])
    rows = lambda g, j: g[j].reshape(4 * g.shape[2], g.shape[3])
    w_c = cols(gathered[4], 0)
    w_in = {0: [cols(gathered[0], j) for j in range(2)], 1: [cols(gathered[2], 0)], 2: [w_c[:, :4 * BRANCH]]}
    w_out = {0: [rows(gathered[1], j) for j in range(2)], 1: [rows(gathered[3], 0)], 2: [rows(gathered[5], 0)]}
    wf_t = [w_c[:, 4 * BRANCH:].T]

    res = _forward_backward(x[0], loss_target[0], g_pre, g_post, sinks_a, b_f_c, w_in, w_out, wf_t)

    col_parts = lambda gs: jnp.stack([g.reshape(g.shape[0], 4, g.shape[1] // 4).transpose(1, 0, 2) for g in gs]).astype(BF16)
    row_parts = lambda gs: jnp.stack([g.reshape(4, g.shape[0] // 4, g.shape[1]) for g in gs]).astype(BF16)
    parts = [col_parts(res["w_in"][0]), row_parts(res["w_out"][0]), col_parts(res["w_in"][1]), row_parts(res["w_out"][1]),
             col_parts(res["w_in"][2]), row_parts(res["w_out"][2])]
    names = ["w_in_a", "w_out_a", "w_in_b", "w_out_b", "w_in_c", "w_out_c"]
    mine, theirs = _sibling_split(parts)
    chip_sums = [_add_pairs(a, b, f"chip_sum_{nm}") for a, b, nm in zip(mine, theirs, names)]
    arrived = _chip_scatter(chip_sums)
    halves = [_sum_chips(r, f"shard_sum_{nm}") for r, nm in zip(arrived, names)]
    grads = _sibling_join(halves)

    small = _unpack_small(_all_reduce_small(
        _pack_small(res["g_pre"], res["g_post"], res["sinks_a"], res["b_f_c"], res["loss"])))
    g_small, loss = small[:4], small[4]

    zero_row = jnp.zeros((1, LANES), F32)
    pk = lambda a: _pack_small(a[0], a[1], a[2], a[3], zero_row)
    sm = _adamw(pk([g_pre, g_post, sinks_a, b_f_c]), pk(g_small), pk([m_g_pre, m_g_post, m_sinks_a, m_b_f_c]),
                pk([v_g_pre, v_g_post, v_sinks_a, v_b_f_c]), "adamw_small")
    sm = [_unpack_small(a)[:4] for a in sm]
    bigs = [_adamw(w, g, m, v, f"adamw_{nm}") for w, g, m, v, nm in zip(big_w, grads, big_m, big_v, names)]

    def ordered(small4, big6):
        return [small4[0], small4[1], big6[0], big6[1], small4[2], big6[2], big6[3], big6[4], small4[3], big6[5]]

    out = [loss, res["dx"][None], *ordered(g_small, grads)]
    for k in range(3):
        out += ordered(sm[k], [b[k] for b in bigs])
    return tuple(out)
```

```python
import functools
import math

import numpy as np
import jax
import jax.numpy as jnp
from jax import lax
from jax.experimental import pallas as pl
from jax.experimental.pallas import tpu as pltpu

F32 = jnp.float32
BF16 = jnp.bfloat16

D_MODEL = 2048
DEPTH = 4
N_HEADS = 32
HEAD_DIM = 64
LANES = 128
N_PAIRS = N_HEADS * HEAD_DIM // LANES
BRANCH = N_HEADS * HEAD_DIM
N_KV_A = 4
KV_A = N_KV_A * HEAD_DIM
WINDOW = 128
NORM_EPS = 1e-6
NEG = -1e30
Q_SCALE = HEAD_DIM ** -0.5

A_QKV = BRANCH + 2 * KV_A
B_QKV = 3 * BRANCH

ADAM_LR = 0.001
ADAM_B1 = 0.9
ADAM_B2 = 0.999
ADAM_EPS = 1e-08
ADAM_WD = 0.01
ADAM_STEP = 10

MESH = pl.DeviceIdType.MESH

_NT = (((1,), (1,)), ((), ()))
_TN = (((0,), (0,)), ((), ()))


def _params(sem=None):
    return pltpu.CompilerParams(dimension_semantics=sem)


def _matmul(a, b, *, mode, out_dtype, name, n=None, b_off=0, tm=1024, tn=1024, tk=2048):
    if mode == "nn":
        (m, k), nn = a.shape, (n or b.shape[1])
    elif mode == "nt":
        (m, k), nn = a.shape, b.shape[0]
    else:
        (k, m), nn = a.shape, b.shape[1]
    tm, tn, tk = min(tm, m), min(tn, nn), min(tk, k)
    assert m % tm == 0 and nn % tn == 0 and k % tk == 0, (name, m, nn, k, tm, tn, tk)
    nk = k // tk

    def body(a_ref, b_ref, o_ref, acc_ref):
        kk = pl.program_id(2)
        if mode == "nn":
            p = jnp.dot(a_ref[...], b_ref[...], preferred_element_type=F32)
        elif mode == "nt":
            p = lax.dot_general(a_ref[...], b_ref[...], _NT, preferred_element_type=F32)
        else:
            p = lax.dot_general(a_ref[...], b_ref[...], _TN, preferred_element_type=F32)
        if nk == 1:
            o_ref[...] = p.astype(o_ref.dtype)
        else:
            @pl.when(kk == 0)
            def _():
                acc_ref[...] = p

            @pl.when(kk > 0)
            def _():
                acc_ref[...] += p

            @pl.when(kk == nk - 1)
            def _():
                o_ref[...] = acc_ref[...].astype(o_ref.dtype)

    if mode == "nn":
        in_specs = [pl.BlockSpec((tm, tk), lambda i, j, kk: (i, kk)),
                    pl.BlockSpec((tk, tn), lambda i, j, kk: (kk, j + b_off))]
    elif mode == "nt":
        in_specs = [pl.BlockSpec((tm, tk), lambda i, j, kk: (i, kk)),
                    pl.BlockSpec((tn, tk), lambda i, j, kk: (j, kk))]
    else:
        in_specs = [pl.BlockSpec((tk, tm), lambda i, j, kk: (kk, i)),
                    pl.BlockSpec((tk, tn), lambda i, j, kk: (kk, j))]
    return pl.pallas_call(
        body, name=name, grid=(m // tm, nn // tn, nk),
        in_specs=in_specs,
        out_specs=pl.BlockSpec((tm, tn), lambda i, j, kk: (i, j)),
        out_shape=jax.ShapeDtypeStruct((m, nn), out_dtype),
        scratch_shapes=[pltpu.VMEM((tm, tn), F32)],
        compiler_params=_params(("parallel", "parallel", "arbitrary")),
    )(a, b)


ROW_TILE = 256


def _row_call(body, name, ins, outs, *, s, acc_outs=()):
    tr = min(ROW_TILE, s)
    row = lambda w: pl.BlockSpec((tr, w), lambda i: (i, 0))
    vec = lambda w: pl.BlockSpec((1, w), lambda i: (0, 0))
    in_specs = [row(a.shape[1]) if kind == "row" else vec(a.shape[1]) for a, kind in ins]
    out_specs = [row(sh.shape[1]) if kind == "row" else vec(sh.shape[1]) for sh, kind in outs]
    return pl.pallas_call(
        body, name=name, grid=(s // tr,), in_specs=in_specs, out_specs=out_specs,
        out_shape=[sh for sh, _ in outs],
        compiler_params=_params(("arbitrary",)),
    )(*[a for a, _ in ins])


def _rsqrt_ms(v):
    return lax.rsqrt(jnp.mean(v * v, axis=-1, keepdims=True) + NORM_EPS)


def _rmsnorm_fwd(x, g, name):
    s, d = x.shape

    def body(x_ref, g_ref, h_ref):
        xv = x_ref[...]
        h_ref[...] = (xv * _rsqrt_ms(xv) * g_ref[...]).astype(BF16)

    return _row_call(body, name, [(x, "row"), (g, "vec")],
                     [(jax.ShapeDtypeStruct((s, d), BF16), "row")], s=s)[0]


def _gate_fwd(o, z, name):
    s, d = o.shape

    def body(o_ref, z_ref, u_ref):
        zv = z_ref[...]
        u_ref[...] = (o_ref[...] * (zv * jax.nn.sigmoid(zv))).astype(BF16)

    return _row_call(body, name, [(o, "row"), (z, "row")],
                     [(jax.ShapeDtypeStruct((s, d), BF16), "row")], s=s)[0]


def _post_fwd(x, y, g, name):
    s, d = x.shape

    def body(x_ref, y_ref, g_ref, o_ref):
        yv = y_ref[...]
        o_ref[...] = x_ref[...] + yv * _rsqrt_ms(yv) * g_ref[...]

    return _row_call(body, name, [(x, "row"), (y, "row"), (g, "vec")],
                     [(jax.ShapeDtypeStruct((s, d), F32), "row")], s=s)[0]


def _loss_and_grad(x, target):
    s, d = x.shape

    def body(x_ref, t_ref, dx_ref, l_ref):
        err = x_ref[...] - t_ref[...]
        dx_ref[...] = err * (1.0 / d)
        part = jnp.sum(jnp.sum(err * err, axis=1, keepdims=True), axis=0, keepdims=True) * (0.5 / d)

        @pl.when(pl.program_id(0) == 0)
        def _():
            l_ref[...] = jnp.zeros_like(l_ref)

        l_ref[...] += jnp.broadcast_to(part, l_ref.shape)

    return _row_call(body, "loss_head", [(x, "row"), (target, "row")],
                     [(jax.ShapeDtypeStruct((s, d), F32), "row"),
                      (jax.ShapeDtypeStruct((1, LANES), F32), "vec")], s=s)


def _norm_bwd_rows(dn, v, g):
    r = _rsqrt_ms(v)
    a = dn * g
    dv = r * (a - v * (r * r) * jnp.mean(a * v, axis=-1, keepdims=True))
    return dv, dn * v * r


def _post_bwd(dx, y, g, name):
    s, d = dx.shape

    def body(dx_ref, y_ref, g_ref, dy_ref, dg_ref):
        dy, dg = _norm_bwd_rows(dx_ref[...], y_ref[...], g_ref[...])
        dy_ref[...] = dy.astype(BF16)

        @pl.when(pl.program_id(0) == 0)
        def _():
            dg_ref[...] = jnp.zeros_like(dg_ref)

        dg_ref[...] += jnp.sum(dg, axis=0, keepdims=True)

    return _row_call(body, name, [(dx, "row"), (y, "row"), (g, "vec")],
                     [(jax.ShapeDtypeStruct((s, d), BF16), "row"),
                      (jax.ShapeDtypeStruct((1, d), F32), "vec")], s=s)


def _gate_bwd(du, o, z, name):
    s, d = du.shape

    def body(du_ref, o_ref, z_ref, do_ref, dz_ref):
        duv, zv = du_ref[...], z_ref[...]
        sig = jax.nn.sigmoid(zv)
        do_ref[...] = (duv * (zv * sig)).astype(BF16)
        dz_ref[...] = (duv * o_ref[...] * (sig * (1.0 + zv * (1.0 - sig)))).astype(BF16)

    return _row_call(body, name, [(du, "row"), (o, "row"), (z, "row")],
                     [(jax.ShapeDtypeStruct((s, d), BF16), "row"),
                      (jax.ShapeDtypeStruct((s, d), BF16), "row")], s=s)


def _pre_bwd(dx, dhs, x, g, name):
    s, d = dx.shape
    n_dh = len(dhs)

    def body(*refs):
        dx_ref, dh_refs, (x_ref, g_ref, o_ref, dg_ref) = refs[0], refs[1:1 + n_dh], refs[1 + n_dh:]
        dh = dh_refs[0][...].astype(F32)
        for r in dh_refs[1:]:
            dh = dh + r[...].astype(F32)
        dv, dg = _norm_bwd_rows(dh, x_ref[...], g_ref[...])
        o_ref[...] = dx_ref[...] + dv

        @pl.when(pl.program_id(0) == 0)
        def _():
            dg_ref[...] = jnp.zeros_like(dg_ref)

        dg_ref[...] += jnp.sum(dg, axis=0, keepdims=True)

    return _row_call(body, name, [(dx, "row")] + [(h, "row") for h in dhs] + [(x, "row"), (g, "vec")],
                     [(jax.ShapeDtypeStruct((s, d), F32), "row"),
                      (jax.ShapeDtypeStruct((1, d), F32), "vec")], s=s)


def _lane_is_first_head():
    return lax.broadcasted_iota(jnp.int32, (1, LANES), 1) < HEAD_DIM


def _bcast_lanes(col):
    return jnp.broadcast_to(col, (col.shape[0], LANES))


def _pair_spec(s, off=0, width=LANES):
    return pl.BlockSpec((s, width), lambda p: (0, p + off))


def _rowsum_heads(prod, first):
    return (jnp.sum(jnp.where(first, prod, 0.0), axis=1, keepdims=True),
            jnp.sum(jnp.where(first, 0.0, prod), axis=1, keepdims=True))


def _softplus_parts(z):
    e = jnp.exp(-jnp.abs(z))
    sp = jnp.maximum(z, 0.0) + jnp.log(1.0 + e)
    r = 1.0 / (1.0 + e)
    return sp, jnp.where(z >= 0, r, e * r)


def _split_dot(x, t):
    hi = x.astype(BF16)
    lo = (x - hi.astype(F32)).astype(BF16)
    return jnp.dot(hi, t, preferred_element_type=F32) + jnp.dot(lo, t, preferred_element_type=F32)


def _sb_tile(s):
    return min(256, s)


def _attn_b_fwd(qkv, name):
    s = qkv.shape[0]
    t = _sb_tile(s)
    nq = s // t

    def body(q_ref, k_ref, v_ref, o_ref, lt_ref):
        first = _lane_is_first_head()
        row = lax.broadcasted_iota(jnp.int32, (t, t), 0)
        col = lax.broadcasted_iota(jnp.int32, (t, t), 1)
        before = col < row
        tri = (row >= col).astype(BF16)

        def tile(j, carry, diag, qa, qb):
            ca, cb, acc = carry
            c0 = pl.multiple_of(j * t, t)
            k2 = k_ref[pl.ds(c0, t), :]
            v2 = v_ref[pl.ds(c0, t), :]
            res = []
            for qh, c in ((qa, ca), (qb, cb)):
                z = lax.dot_general(qh, k2, _NT, preferred_element_type=F32)
                sp, _ = _softplus_parts(z)
                lf = jnp.where(before, -sp, 0.0) if diag else -sp
                incl = _split_dot(lf, tri)
                a = jnp.exp(z + c + incl)
                if diag:
                    a = jnp.where(before, a, 0.0)
                res.append((jnp.dot(a.astype(BF16), v2, preferred_element_type=F32), c + incl[:, 0:1]))
            (pa, ca), (pb, cb) = res
            return ca, cb, acc + jnp.where(first, pa, pb)

        def qblock(i, _):
            r0 = pl.multiple_of(i * t, t)
            q2 = q_ref[pl.ds(r0, t), :] * Q_SCALE
            qa = jnp.where(first, q2, 0).astype(BF16)
            qb = jnp.where(first, 0, q2).astype(BF16)
            zero = jnp.zeros((t, 1), F32)
            carry = tile(i, (zero, zero, jnp.zeros((t, LANES), F32)), True, qa, qb)
            carry = lax.fori_loop(0, i, lambda jj, c: tile(i - 1 - jj, c, False, qa, qb), carry)
            o_ref[pl.ds(r0, t), :] = carry[2]
            lt_ref[pl.ds(r0, t), 0:LANES] = _bcast_lanes(carry[0])
            lt_ref[pl.ds(r0, t), LANES:2 * LANES] = _bcast_lanes(carry[1])
            return 0

        lax.fori_loop(0, nq, qblock, 0)

    return pl.pallas_call(
        body, name=name, grid=(N_PAIRS,),
        in_specs=[_pair_spec(s), _pair_spec(s, N_PAIRS), _pair_spec(s, 2 * N_PAIRS)],
        out_specs=[_pair_spec(s), _stat_spec(s)],
        out_shape=[jax.ShapeDtypeStruct((s, BRANCH), F32), jax.ShapeDtypeStruct((s, N_HEADS * LANES), F32)],
        compiler_params=_params(("parallel",)),
    )(qkv, qkv, qkv)


def _attn_b_bwd(qkv, ltot, do, name):
    s = qkv.shape[0]
    t = _sb_tile(s)
    nq = s // t

    def body(q_ref, k_ref, v_ref, lt_ref, do_ref, dq_ref, dk_ref, dv_ref, dk_acc, dv_acc):
        first = _lane_is_first_head()
        row = lax.broadcasted_iota(jnp.int32, (t, t), 0)
        col = lax.broadcasted_iota(jnp.int32, (t, t), 1)
        before = col < row
        tri = (row <= col).astype(BF16)
        dk_acc[...] = jnp.zeros_like(dk_acc)
        dv_acc[...] = jnp.zeros_like(dv_acc)

        def tile(j, carry, diag, heads):
            pl_a, pg_a, pl_b, pg_b, dq_acc = carry
            c0 = pl.multiple_of(j * t, t)
            k2 = k_ref[pl.ds(c0, t), :]
            v2 = v_ref[pl.ds(c0, t), :]
            out = []
            dk_t = jnp.zeros((t, LANES), F32)
            dv_t = jnp.zeros((t, LANES), F32)
            dq_parts = []
            for (qh, doh, lt), (p_l, p_g) in zip(heads, ((pl_a, pg_a), (pl_b, pg_b))):
                z = lax.dot_general(qh, k2, _NT, preferred_element_type=F32)
                sp, sig = _softplus_parts(z)
                lf = jnp.where(before, -sp, 0.0) if diag else -sp
                pref_l = _split_dot(lf, tri)
                a = jnp.exp(z + ((lt - p_l) - pref_l + lf))
                if diag:
                    a = jnp.where(before, a, 0.0)
                da = lax.dot_general(doh, v2, _NT, preferred_element_type=F32)
                g = a * da
                pref_g = _split_dot(g, tri)
                dz = g - sig * (p_g + pref_g)
                if diag:
                    dz = jnp.where(before, dz, 0.0)
                dzb = dz.astype(BF16)
                dq_parts.append(jnp.dot(dzb, k2, preferred_element_type=F32))
                dk_t = dk_t + lax.dot_general(dzb, qh, _TN, preferred_element_type=F32)
                dv_t = dv_t + lax.dot_general(a.astype(BF16), doh, _TN, preferred_element_type=F32)
                out.append((p_l + pref_l[:, t - 1:t], p_g + pref_g[:, t - 1:t]))
            dk_acc[pl.ds(c0, t), :] += dk_t
            dv_acc[pl.ds(c0, t), :] += dv_t
            dq_acc = dq_acc + jnp.where(first, dq_parts[0], dq_parts[1])
            return out[0][0], out[0][1], out[1][0], out[1][1], dq_acc

        def qblock(i, _):
            r0 = pl.multiple_of(i * t, t)
            q2 = q_ref[pl.ds(r0, t), :] * Q_SCALE
            do2 = do_ref[pl.ds(r0, t), :]
            heads = ((jnp.where(first, q2, 0).astype(BF16), jnp.where(first, do2, 0).astype(BF16),
                      lt_ref[pl.ds(r0, t), 0:1]),
                     (jnp.where(first, 0, q2).astype(BF16), jnp.where(first, 0, do2).astype(BF16),
                      lt_ref[pl.ds(r0, t), LANES:LANES + 1]))
            zero = jnp.zeros((t, 1), F32)
            carry = (zero, zero, zero, zero, jnp.zeros((t, LANES), F32))
            carry = lax.fori_loop(0, i, lambda j, c: tile(j, c, False, heads), carry)
            carry = tile(i, carry, True, heads)
            dq_ref[pl.ds(r0, t), :] = (carry[4] * Q_SCALE).astype(BF16)
            return 0

        lax.fori_loop(0, nq, qblock, 0)
        dk_ref[...] = dk_acc[...].astype(BF16)
        dv_ref[...] = dv_acc[...].astype(BF16)

    out = jax.ShapeDtypeStruct((s, BRANCH), BF16)
    return pl.pallas_call(
        body, name=name, grid=(N_PAIRS,),
        in_specs=[_pair_spec(s), _pair_spec(s, N_PAIRS), _pair_spec(s, 2 * N_PAIRS), _stat_spec(s), _pair_spec(s)],
        out_specs=[_pair_spec(s)] * 3, out_shape=[out] * 3,
        scratch_shapes=[pltpu.VMEM((s, LANES), F32), pltpu.VMEM((s, LANES), F32)],
        compiler_params=_params(("parallel",)),
    )(qkv, qkv, qkv, ltot, do)


def _fox_tile(s):
    return min(256, s)


def _stat_spec(s):
    return pl.BlockSpec((s, 2 * LANES), lambda p: (0, p))


def _cum_spec(nt, t):
    return pl.BlockSpec((1, nt, 2, t), lambda p: (p, 0, 0, 0))


def _attn_c_fwd(qkv, cum4, name):
    s = qkv.shape[0]
    t = _fox_tile(s)
    nq = s // t

    def body(q_ref, k_ref, v_ref, c_ref, o_ref, lse_ref):
        first = _lane_is_first_head()
        row = lax.broadcasted_iota(jnp.int32, (t, t), 0)
        col = lax.broadcasted_iota(jnp.int32, (t, t), 1)
        causal = col <= row

        def tile(j, carry, diag, qa, qb):
            c0 = pl.multiple_of(j * t, t)
            k2 = k_ref[pl.ds(c0, t), :]
            v2 = v_ref[pl.ds(c0, t), :]
            cs = c_ref[0, j]
            m_a, l_a, m_b, l_b, acc = carry
            new = []
            for h, (qh, m_prev, l_prev) in enumerate(((qa, m_a, l_a), (qb, m_b, l_b))):
                sc = lax.dot_general(qh, k2, _NT, preferred_element_type=F32) - cs[h:h + 1, :]
                if diag:
                    sc = jnp.where(causal, sc, NEG)
                m_new = jnp.maximum(m_prev, jnp.max(sc, axis=1, keepdims=True))
                alpha = jnp.exp(m_prev - m_new)
                p = jnp.exp(sc - m_new)
                l_new = alpha * l_prev + jnp.sum(p, axis=1, keepdims=True)
                new.append((m_new, l_new, alpha, jnp.dot(p.astype(BF16), v2, preferred_element_type=F32)))
            (m_a, l_a, al_a, pv_a), (m_b, l_b, al_b, pv_b) = new
            acc = jnp.where(first, acc * al_a + pv_a, acc * al_b + pv_b)
            return m_a, l_a, m_b, l_b, acc

        def qblock(i, _):
            r0 = pl.multiple_of(i * t, t)
            q2 = q_ref[pl.ds(r0, t), :] * Q_SCALE
            qa = jnp.where(first, q2, 0).astype(BF16)
            qb = jnp.where(first, 0, q2).astype(BF16)
            neg = jnp.full((t, 1), NEG, F32)
            zero = jnp.zeros((t, 1), F32)
            carry = (neg, zero, neg, zero, jnp.zeros((t, LANES), F32))
            carry = lax.fori_loop(0, i, lambda j, c: tile(j, c, False, qa, qb), carry)
            m_a, l_a, m_b, l_b, acc = tile(i, carry, True, qa, qb)
            o_ref[pl.ds(r0, t), :] = acc * jnp.where(first, 1.0 / l_a, 1.0 / l_b)
            lse_ref[pl.ds(r0, t), 0:LANES] = _bcast_lanes(m_a + jnp.log(l_a))
            lse_ref[pl.ds(r0, t), LANES:2 * LANES] = _bcast_lanes(m_b + jnp.log(l_b))
            return 0

        lax.fori_loop(0, nq, qblock, 0)

    return pl.pallas_call(
        body, name=name, grid=(N_PAIRS,),
        in_specs=[_pair_spec(s), _pair_spec(s, N_PAIRS), _pair_spec(s, 2 * N_PAIRS), _cum_spec(nq, t)],
        out_specs=[_pair_spec(s), _stat_spec(s)],
        out_shape=[jax.ShapeDtypeStruct((s, BRANCH), F32), jax.ShapeDtypeStruct((s, N_HEADS * LANES), F32)],
        compiler_params=_params(("parallel",)),
    )(qkv, qkv, qkv, cum4)


def _attn_c_bwd(qkv, cum4, o, lse, do, name):
    s = qkv.shape[0]
    t = _fox_tile(s)
    nq = s // t

    def body(q_ref, k_ref, v_ref, c_ref, o_ref, lse_ref, do_ref, dq_ref, dk_ref, dv_ref, dc_ref, dk_acc, dv_acc):
        first = _lane_is_first_head()
        row = lax.broadcasted_iota(jnp.int32, (t, t), 0)
        col = lax.broadcasted_iota(jnp.int32, (t, t), 1)
        causal = col <= row
        eye = col == row
        dk_acc[...] = jnp.zeros_like(dk_acc)
        dv_acc[...] = jnp.zeros_like(dv_acc)
        dc_ref[...] = jnp.zeros_like(dc_ref)

        def tile(j, carry, diag, heads):
            dq_acc, rs_a, rs_b = carry
            c0 = pl.multiple_of(j * t, t)
            k2 = k_ref[pl.ds(c0, t), :]
            v2 = v_ref[pl.ds(c0, t), :]
            cs = c_ref[0, j]
            dk_t = jnp.zeros((t, LANES), F32)
            dv_t = jnp.zeros((t, LANES), F32)
            dq_parts, dc_rows, row_sums = [], [], []
            for h, (qh, doh, delta, lse_h) in enumerate(heads):
                sc = lax.dot_general(qh, k2, _NT, preferred_element_type=F32) - cs[h:h + 1, :]
                p = jnp.exp(sc - lse_h)
                if diag:
                    p = jnp.where(causal, p, 0.0)
                dp = lax.dot_general(doh, v2, _NT, preferred_element_type=F32)
                ds = p * (dp - delta)
                dsb = ds.astype(BF16)
                dq_parts.append(jnp.dot(dsb, k2, preferred_element_type=F32))
                dk_t = dk_t + lax.dot_general(dsb, qh, _TN, preferred_element_type=F32)
                dv_t = dv_t + lax.dot_general(p.astype(BF16), doh, _TN, preferred_element_type=F32)
                dc_rows.append(jnp.sum(ds, axis=0, keepdims=True))
                row_sums.append(jnp.sum(ds, axis=1, keepdims=True))
            dk_acc[pl.ds(c0, t), :] += dk_t
            dv_acc[pl.ds(c0, t), :] += dv_t
            dc_ref[0, j] = dc_ref[0, j] - jnp.concatenate(dc_rows, axis=0)
            return dq_acc + jnp.where(first, dq_parts[0], dq_parts[1]), rs_a + row_sums[0], rs_b + row_sums[1]

        def qblock(i, _):
            r0 = pl.multiple_of(i * t, t)
            q2 = q_ref[pl.ds(r0, t), :] * Q_SCALE
            do2 = do_ref[pl.ds(r0, t), :]
            delta_a, delta_b = _rowsum_heads(do2.astype(F32) * o_ref[pl.ds(r0, t), :], first)
            heads = ((jnp.where(first, q2, 0).astype(BF16), jnp.where(first, do2, 0).astype(BF16), delta_a,
                      lse_ref[pl.ds(r0, t), 0:1]),
                     (jnp.where(first, 0, q2).astype(BF16), jnp.where(first, 0, do2).astype(BF16), delta_b,
                      lse_ref[pl.ds(r0, t), LANES:LANES + 1]))
            zero = jnp.zeros((t, 1), F32)
            carry = lax.fori_loop(0, i, lambda j, c: tile(j, c, False, heads), (jnp.zeros((t, LANES), F32), zero, zero))
            dq_acc, rs_a, rs_b = tile(i, carry, True, heads)
            dq_ref[pl.ds(r0, t), :] = (dq_acc * Q_SCALE).astype(BF16)
            as_row = lambda col_vec: jnp.sum(jnp.where(eye, col_vec, 0.0), axis=0, keepdims=True)
            dc_ref[0, i] = dc_ref[0, i] + jnp.concatenate([as_row(rs_a), as_row(rs_b)], axis=0)
            return 0

        lax.fori_loop(0, nq, qblock, 0)
        dk_ref[...] = dk_acc[...].astype(BF16)
        dv_ref[...] = dv_acc[...].astype(BF16)

    out = jax.ShapeDtypeStruct((s, BRANCH), BF16)
    return pl.pallas_call(
        body, name=name, grid=(N_PAIRS,),
        in_specs=[_pair_spec(s), _pair_spec(s, N_PAIRS), _pair_spec(s, 2 * N_PAIRS), _cum_spec(nq, t),
                  _pair_spec(s), _stat_spec(s), _pair_spec(s)],
        out_specs=[_pair_spec(s)] * 3 + [_cum_spec(nq, t)],
        out_shape=[out] * 3 + [jax.ShapeDtypeStruct(cum4.shape, F32)],
        scratch_shapes=[pltpu.VMEM((s, LANES), F32), pltpu.VMEM((s, LANES), F32)],
        compiler_params=_params(("parallel",)),
    )(qkv, qkv, qkv, cum4, o, lse, do)


FG_CHUNK = 512


def _tri_dot3(x, t):
    hi = x.astype(BF16)
    r1 = x - hi.astype(F32)
    mid = r1.astype(BF16)
    lo = (r1 - mid.astype(F32)).astype(BF16)
    return (jnp.dot(hi, t, preferred_element_type=F32) + jnp.dot(mid, t, preferred_element_type=F32)
            + jnp.dot(lo, t, preferred_element_type=F32))


def _fgate_fwd(h, wf_t, b_col, name):
    s = h.shape[0]
    c = min(FG_CHUNK, s)

    def body(h_ref, w_ref, b_ref, xf_ref, cum_ref, carry_ref):
        @pl.when(pl.program_id(0) == 0)
        def _():
            carry_ref[...] = jnp.zeros_like(carry_ref)

        xf = lax.dot_general(w_ref[...], h_ref[...], _NT, preferred_element_type=F32) + b_ref[:, 0:1]
        xf_ref[...] = xf
        logf = jnp.minimum(xf, 0.0) - jnp.log(1.0 + jnp.exp(-jnp.abs(xf)))
        row = lax.broadcasted_iota(jnp.int32, (c, c), 0)
        col = lax.broadcasted_iota(jnp.int32, (c, c), 1)
        cum = _tri_dot3(logf, (row <= col).astype(BF16)) + carry_ref[:, 0:1]
        cum_ref[...] = cum
        carry_ref[...] = _bcast_lanes(cum[:, c - 1:c])

    out = jax.ShapeDtypeStruct((N_HEADS, s), F32)
    return pl.pallas_call(
        body, name=name, grid=(s // c,),
        in_specs=[pl.BlockSpec((c, D_MODEL), lambda i: (i, 0)),
                  pl.BlockSpec((N_HEADS, D_MODEL), lambda i: (0, 0)),
                  pl.BlockSpec((N_HEADS, LANES), lambda i: (0, 0))],
        out_specs=[pl.BlockSpec((N_HEADS, c), lambda i: (0, i))] * 2,
        out_shape=[out, out],
        scratch_shapes=[pltpu.VMEM((N_HEADS, LANES), F32)],
        compiler_params=_params(("arbitrary",)),
    )(h, wf_t, b_col)


def _fgate_bwd(dcum, xf, h, wf_t, name):
    s = h.shape[0]
    c = min(FG_CHUNK, s)
    n = s // c

    def body(dc_ref, xf_ref, h_ref, w_ref, dw_ref, dh_ref, db_ref, carry_ref):
        @pl.when(pl.program_id(0) == 0)
        def _():
            carry_ref[...] = jnp.zeros_like(carry_ref)
            dw_ref[...] = jnp.zeros_like(dw_ref)
            db_ref[...] = jnp.zeros_like(db_ref)

        row = lax.broadcasted_iota(jnp.int32, (c, c), 0)
        col = lax.broadcasted_iota(jnp.int32, (c, c), 1)
        dlogf = _tri_dot3(dc_ref[...], (row >= col).astype(BF16)) + carry_ref[:, 0:1]
        carry_ref[...] = _bcast_lanes(dlogf[:, 0:1])
        xf = xf_ref[...]
        e = jnp.exp(-jnp.abs(xf))
        r = 1.0 / (1.0 + e)
        dxf = dlogf * jnp.where(xf >= 0, e * r, r)
        db_ref[...] += _bcast_lanes(jnp.sum(dxf, axis=1, keepdims=True))
        dxb = dxf.astype(BF16)
        dw_ref[...] += jnp.dot(dxb, h_ref[...], preferred_element_type=F32)
        dh_ref[...] = lax.dot_general(dxb, w_ref[...], _TN, preferred_element_type=F32)

    rev = lambda i: n - 1 - i
    return pl.pallas_call(
        body, name=name, grid=(n,),
        in_specs=[pl.BlockSpec((N_HEADS, c), lambda i: (0, rev(i))),
                  pl.BlockSpec((N_HEADS, c), lambda i: (0, rev(i))),
                  pl.BlockSpec((c, D_MODEL), lambda i: (rev(i), 0)),
                  pl.BlockSpec((N_HEADS, D_MODEL), lambda i: (0, 0))],
        out_specs=[pl.BlockSpec((N_HEADS, D_MODEL), lambda i: (0, 0)),
                   pl.BlockSpec((c, D_MODEL), lambda i: (rev(i), 0)),
                   pl.BlockSpec((N_HEADS, LANES), lambda i: (0, 0))],
        out_shape=[jax.ShapeDtypeStruct((N_HEADS, D_MODEL), F32), jax.ShapeDtypeStruct((s, D_MODEL), F32),
                   jax.ShapeDtypeStruct((N_HEADS, LANES), F32)],
        scratch_shapes=[pltpu.VMEM((N_HEADS, LANES), F32)],
        compiler_params=_params(("arbitrary",)),
    )(dcum, xf, h, wf_t)


def _to_cum4(v, t):
    s = v.shape[1]
    return v.reshape(N_PAIRS, 2, s // t, t).transpose(0, 2, 1, 3)


def _from_cum4(v4):
    p, nt, two, t = v4.shape
    return v4.transpose(0, 2, 1, 3).reshape(p * two, nt * t)


def _alibi_slopes():
    return (2.0 ** (-8.0 * np.arange(1, N_HEADS + 1, dtype=np.float32) / N_HEADS)).astype(np.float32)


def _per_head_lanes(v):
    return jnp.repeat(v.astype(F32).reshape(N_PAIRS, 1, 2), LANES, axis=2)


def _attn_a_specs(s):
    q = _pair_spec(s)
    k = pl.BlockSpec((s, LANES), lambda p: (0, N_PAIRS + p // 8))
    v = pl.BlockSpec((s, LANES), lambda p: (0, N_PAIRS + KV_A // LANES + p // 8))
    head = pl.BlockSpec((1, 1, 2 * LANES), lambda p: (p, 0, 0))
    return q, k, v, head


def _attn_a_geometry(p):
    kv_half = (p // 4) % 2
    kv_first = kv_half == 0
    lane_first = _lane_is_first_head()
    kv_lanes = (lax.broadcasted_iota(jnp.int32, (1, LANES), 1) // HEAD_DIM) == kv_half
    ti = lax.broadcasted_iota(jnp.int32, (WINDOW, 2 * WINDOW), 0)
    cj = lax.broadcasted_iota(jnp.int32, (WINDOW, 2 * WINDOW), 1)
    dist = WINDOW + ti - cj
    valid = (dist >= 0) & (dist < WINDOW)
    return kv_first, lane_first, kv_lanes, dist.astype(F32), valid


def _swap_halves(x):
    return pltpu.roll(x, HEAD_DIM, 1)


def _attn_a_fwd(qkv, slopes, sinks, name):
    s = qkv.shape[0]
    nb = s // WINDOW

    def body(q_ref, k_ref, v_ref, sl_ref, sk_ref, o_ref, lse_ref):
        kv_first, lane_first, kv_lanes, dist, valid = _attn_a_geometry(pl.program_id(0))
        slope = (sl_ref[0, :, 0:1], sl_ref[0, :, LANES:LANES + 1])
        sink = (sk_ref[0, :, 0:1], sk_ref[0, :, LANES:LANES + 1])

        def block(n, r0, k0, width):
            q2 = q_ref[pl.ds(r0, WINDOW), :].astype(F32) * Q_SCALE
            q2r = _swap_halves(q2)
            x = (jnp.where(kv_first, q2, q2r).astype(BF16), jnp.where(kv_first, q2r, q2).astype(BF16))
            km = jnp.where(kv_lanes, k_ref[pl.ds(k0, width), :], 0).astype(BF16)
            vm = jnp.where(kv_lanes, v_ref[pl.ds(k0, width), :], 0).astype(BF16)
            dist_w, valid_w = dist[:, 2 * WINDOW - width:], valid[:, 2 * WINDOW - width:]
            outs = []
            for h in range(2):
                sc = lax.dot_general(x[h], km, _NT, preferred_element_type=F32) - slope[h] * dist_w
                sc = jnp.where(valid_w, sc, NEG)
                m = jnp.maximum(jnp.max(sc, axis=1, keepdims=True), sink[h])
                pr = jnp.exp(sc - m)
                l = jnp.sum(pr, axis=1, keepdims=True) + jnp.exp(sink[h] - m)
                oh = jnp.dot(pr.astype(BF16), vm, preferred_element_type=F32) * (1.0 / l)
                outs.append(oh)
                lse_ref[pl.ds(r0, WINDOW), h * LANES:(h + 1) * LANES] = _bcast_lanes(m + jnp.log(l))
            oa = jnp.where(kv_first, outs[0], _swap_halves(outs[0]))
            ob = jnp.where(kv_first, _swap_halves(outs[1]), outs[1])
            o_ref[pl.ds(r0, WINDOW), :] = jnp.where(lane_first, oa, ob)

        block(0, 0, 0, WINDOW)

        def loop(n, _):
            r0 = pl.multiple_of(n * WINDOW, WINDOW)
            block(n, r0, pl.multiple_of(r0 - WINDOW, WINDOW), 2 * WINDOW)
            return 0

        lax.fori_loop(1, nb, loop, 0)

    q, k, v, head = _attn_a_specs(s)
    return pl.pallas_call(
        body, name=name, grid=(N_PAIRS,),
        in_specs=[q, k, v, head, head],
        out_specs=[_pair_spec(s), _stat_spec(s)],
        out_shape=[jax.ShapeDtypeStruct((s, BRANCH), F32), jax.ShapeDtypeStruct((s, N_HEADS * LANES), F32)],
        compiler_params=_params(("parallel",)),
    )(qkv, qkv, qkv, slopes, sinks)


def _attn_a_bwd(qkv, slopes, sinks, o, lse, do, name):
    s = qkv.shape[0]
    nb = s // WINDOW

    def body(q_ref, k_ref, v_ref, sl_ref, sk_ref, o_ref, lse_ref, do_ref, dq_ref, dk_ref, dv_ref, dsk_ref):
        p_id = pl.program_id(0)
        kv_first, lane_first, kv_lanes, dist, valid = _attn_a_geometry(p_id)
        slope = (sl_ref[0, :, 0:1], sl_ref[0, :, LANES:LANES + 1])
        sink = (sk_ref[0, :, 0:1], sk_ref[0, :, LANES:LANES + 1])

        @pl.when(p_id % 8 == 0)
        def _():
            dk_ref[...] = jnp.zeros_like(dk_ref)
            dv_ref[...] = jnp.zeros_like(dv_ref)

        def align(v2):
            v2r = _swap_halves(v2)
            return jnp.where(kv_first, v2, v2r), jnp.where(kv_first, v2r, v2)

        def block(r0, k0, width, dsink):
            q_al = align(q_ref[pl.ds(r0, WINDOW), :].astype(F32) * Q_SCALE)
            do2 = do_ref[pl.ds(r0, WINDOW), :].astype(F32)
            do_al = align(do2)
            delta = _rowsum_heads(do2 * o_ref[pl.ds(r0, WINDOW), :], lane_first)
            km = jnp.where(kv_lanes, k_ref[pl.ds(k0, width), :], 0).astype(BF16)
            vm = jnp.where(kv_lanes, v_ref[pl.ds(k0, width), :], 0).astype(BF16)
            dist_w, valid_w = dist[:, 2 * WINDOW - width:], valid[:, 2 * WINDOW - width:]
            dk_t = jnp.zeros((width, LANES), F32)
            dv_t = jnp.zeros((width, LANES), F32)
            dq_al, new_dsink = [], []
            for h in range(2):
                lse_h = lse_ref[pl.ds(r0, WINDOW), h * LANES:h * LANES + 1]
                xq = jnp.where(kv_lanes, q_al[h], 0.0).astype(BF16)
                xdo = jnp.where(kv_lanes, do_al[h], 0.0).astype(BF16)
                sc = lax.dot_general(xq, km, _NT, preferred_element_type=F32) - slope[h] * dist_w
                pr = jnp.where(valid_w, jnp.exp(sc - lse_h), 0.0)
                dp = lax.dot_general(xdo, vm, _NT, preferred_element_type=F32)
                ds = pr * (dp - delta[h])
                dsb = ds.astype(BF16)
                dq_al.append(jnp.dot(dsb, km, preferred_element_type=F32))
                dk_t = dk_t + lax.dot_general(dsb, xq, _TN, preferred_element_type=F32)
                dv_t = dv_t + lax.dot_general(pr.astype(BF16), xdo, _TN, preferred_element_type=F32)
                new_dsink.append(dsink[h] - jnp.sum(jnp.exp(sink[h] - lse_h) * delta[h], axis=0, keepdims=True))
            dk_ref[pl.ds(k0, width), :] += dk_t
            dv_ref[pl.ds(k0, width), :] += dv_t
            dqa = jnp.where(kv_first, dq_al[0], _swap_halves(dq_al[0]))
            dqb = jnp.where(kv_first, _swap_halves(dq_al[1]), dq_al[1])
            dq_ref[pl.ds(r0, WINDOW), :] = (jnp.where(lane_first, dqa, dqb) * Q_SCALE).astype(BF16)
            return tuple(new_dsink)

        zero = jnp.zeros((1, 1), F32)
        dsink = block(0, 0, WINDOW, (zero, zero))

        def loop(n, c):
            r0 = pl.multiple_of(n * WINDOW, WINDOW)
            return block(r0, pl.multiple_of(r0 - WINDOW, WINDOW), 2 * WINDOW, c)

        dsink = lax.fori_loop(1, nb, loop, dsink)
        dsk_ref[0, :, 0:LANES] = jnp.broadcast_to(dsink[0], (1, LANES))
        dsk_ref[0, :, LANES:2 * LANES] = jnp.broadcast_to(dsink[1], (1, LANES))

    q, k, v, head = _attn_a_specs(s)
    kv_out = pl.BlockSpec((s, LANES), lambda p: (0, p // 8))
    return pl.pallas_call(
        body, name=name, grid=(N_PAIRS,),
        in_specs=[q, k, v, head, head, _pair_spec(s), _stat_spec(s), _pair_spec(s)],
        out_specs=[_pair_spec(s), kv_out, kv_out, head],
        out_shape=[jax.ShapeDtypeStruct((s, BRANCH), BF16), jax.ShapeDtypeStruct((s, KV_A), F32),
                   jax.ShapeDtypeStruct((s, KV_A), F32), jax.ShapeDtypeStruct((N_PAIRS, 1, 2 * LANES), F32)],
        compiler_params=_params(("arbitrary",)),
    )(qkv, qkv, qkv, slopes, sinks, o, lse, do)


def _layer_kind(i):
    return i % 3, i // 3


def _forward_backward(x, target, g_pre, g_post, sinks_a, b_f_c, w_in, w_out, wf_t):
    s = x.shape[0]
    slopes = _per_head_lanes(jnp.asarray(_alibi_slopes()))
    saved = []
    for i in range(DEPTH):
        kind, j = _layer_kind(i)
        tag = f"l{i}"
        w = w_in[kind][j]
        nqkv = A_QKV if kind == 0 else B_QKV
        tn = 512 if kind == 0 else 1024
        h = _rmsnorm_fwd(x, g_pre[i:i + 1], f"prenorm_{tag}")
        qkv = _matmul(h, w, mode="nn", out_dtype=BF16, name=f"inproj_qkv_{tag}", n=nqkv, tn=tn)
        z = _matmul(h, w, mode="nn", out_dtype=F32, name=f"inproj_gate_{tag}", n=BRANCH, b_off=nqkv // tn, tn=tn)
        extra = None
        if kind == 0:
            sink_l = _per_head_lanes(sinks_a[j])
            o, lse = _attn_a_fwd(qkv, slopes, sink_l, f"attn_a_fwd_{tag}")
            extra = (sink_l, lse)
        elif kind == 1:
            o, extra = _attn_b_fwd(qkv, f"attn_b_fwd_{tag}")
        else:
            b_col = jnp.broadcast_to(b_f_c[j].astype(F32)[:, None], (N_HEADS, LANES))
            xf, cum = _fgate_fwd(h, wf_t[j], b_col, f"fgate_fwd_{tag}")
            cum4 = _to_cum4(cum, _fox_tile(s))
            o, lse = _attn_c_fwd(qkv, cum4, f"attn_c_fwd_{tag}")
            extra = (xf, cum4, lse)
        u = _gate_fwd(o, z, f"gate_{tag}")
        y = _matmul(u, w_out[kind][j], mode="nn", out_dtype=F32, name=f"outproj_{tag}")
        saved.append((x, h, qkv, z, o, u, y, extra))
        x = _post_fwd(x, y, g_post[i:i + 1], f"postnorm_{tag}")

    dx, loss_part = _loss_and_grad(x, target)

    d_g_pre, d_g_post = [None] * DEPTH, [None] * DEPTH
    d_w_in = {0: [None, None], 1: [None], 2: [None]}
    d_w_out = {0: [None, None], 1: [None], 2: [None]}
    d_sinks = [None, None]
    d_b_f = None
    for i in reversed(range(DEPTH)):
        kind, j = _layer_kind(i)
        tag = f"l{i}"
        x_in, h, qkv, z, o, u, y, extra = saved[i]
        tn = 512 if kind == 0 else 1024
        dy, d_g_post[i] = _post_bwd(dx, y, g_post[i:i + 1], f"postnorm_bwd_{tag}")
        d_w_out[kind][j] = _matmul(u, dy, mode="tn", out_dtype=F32, name=f"dw_out_{tag}", tk=512)
        du = _matmul(dy, w_out[kind][j], mode="nt", out_dtype=F32, name=f"d_gated_{tag}")
        do, dz = _gate_bwd(du, o, z, f"gate_bwd_{tag}")
        dhs = []
        if kind == 0:
            sink_l, lse = extra
            dq, dk, dv, dsk = _attn_a_bwd(qkv, slopes, sink_l, o, lse, do, f"attn_a_bwd_{tag}")
            d_sinks[j] = dsk[:, 0, ::LANES].reshape(N_HEADS)
            parts = [dq, dk.astype(BF16), dv.astype(BF16), dz]
        elif kind == 1:
            dq, dk, dv = _attn_b_bwd(qkv, extra, do, f"attn_b_bwd_{tag}")
            parts = [dq, dk, dv, dz]
        else:
            xf, cum4, lse = extra
            dq, dk, dv, dcum4 = _attn_c_bwd(qkv, cum4, o, lse, do, f"attn_c_bwd_{tag}")
            d_wf_t, dh_f, db = _fgate_bwd(_from_cum4(dcum4), xf, h, wf_t[j], f"fgate_bwd_{tag}")
            d_b_f = db[:, 0]
            dhs.append(dh_f)
            parts = [dq, dk, dv, dz]
        dproj = jnp.concatenate(parts, axis=1)
        dw = _matmul(h, dproj, mode="tn", out_dtype=F32, name=f"dw_in_{tag}", tk=512, tn=tn)
        if kind == 2:
            dw = jnp.concatenate([dw, d_wf_t.T], axis=1)
        d_w_in[kind][j] = dw
        dhs.insert(0, _matmul(dproj, w_in[kind][j], mode="nt", out_dtype=F32, name=f"dh_{tag}", tk=512))
        dx, d_g_pre[i] = _pre_bwd(dx, dhs, x_in, g_pre[i:i + 1], f"prenorm_bwd_{tag}")

    return dict(loss=loss_part, dx=dx, g_pre=jnp.concatenate(d_g_pre, axis=0), g_post=jnp.concatenate(d_g_post, axis=0),
                sinks_a=jnp.stack(d_sinks), b_f_c=d_b_f[None, :], w_in=d_w_in, w_out=d_w_out)


def _place():
    x, y, c = lax.axis_index("x"), lax.axis_index("y"), lax.axis_index("c")
    others = [(1 - x, y), (x, 1 - y), (1 - x, 1 - y)]
    return x, y, c, others


def _half_rows(ref_rows, which):
    half = ref_rows // 2
    return pl.ds(pl.multiple_of(which * half, half), half)


def _remote(src, dst, sems, k, device):
    send, recv = sems
    return pltpu.make_async_remote_copy(src_ref=src, dst_ref=dst, send_sem=send.at[k], recv_sem=recv.at[k],
                                        device_id=device, device_id_type=MESH)


def _hbm_call(body, name, ins, out_shapes, n_remote, aliases=None):
    any_spec = pl.BlockSpec(memory_space=pl.ANY)
    return pl.pallas_call(
        body, name=name, in_specs=[any_spec] * len(ins), out_specs=[any_spec] * len(out_shapes),
        out_shape=out_shapes, input_output_aliases=aliases or {},
        scratch_shapes=[pltpu.SemaphoreType.DMA((n_remote,)), pltpu.SemaphoreType.DMA((n_remote,))],
    )(*ins)


def _all_gather_shards(shards):
    n = len(shards)

    def body(*refs):
        ins, outs, sems = refs[:n], refs[n:2 * n], refs[2 * n:2 * n + 2]
        x, y, c, others = _place()
        me = 2 * x + y
        sends = []
        for w in range(n):
            rows = ins[w].shape[1]
            mine = _half_rows(rows, c)
            for j, (px, py) in enumerate(others):
                cp = _remote(ins[w].at[:, mine], outs[w].at[:, me, mine], sems, 6 * w + j, (px, py, c))
                cp.start()
                sends.append(cp)
        for w in range(n):
            mine = _half_rows(ins[w].shape[1], c)
            for j, (px, py) in enumerate(others):
                landed = outs[w].at[:, 2 * px + py, mine]
                _remote(landed, landed, sems, 6 * w + j, (px, py, c)).wait_recv()
                fw = _remote(landed, landed, sems, 6 * w + 3 + j, (x, y, 1 - c))
                fw.start()
                sends.append(fw)
        for w in range(n):
            theirs = _half_rows(ins[w].shape[1], 1 - c)
            for j, (px, py) in enumerate(others):
                landed = outs[w].at[:, 2 * px + py, theirs]
                _remote(landed, landed, sems, 6 * w + 3 + j, (x, y, 1 - c)).wait_recv()
        for cp in sends:
            cp.wait_send()

    out_shapes = [jax.ShapeDtypeStruct((a.shape[0], 4) + a.shape[1:], a.dtype) for a in shards]
    return _hbm_call(body, "all_gather_weights", shards, out_shapes, 6 * n)


def _sibling_send(parts):
    n = len(parts)

    def body(*refs):
        ins, outs, sems = refs[:n], refs[n:2 * n], refs[2 * n:2 * n + 2]
        x, y, c, _ = _place()
        pend = []
        for w in range(n):
            cp = _remote(ins[w].at[:, :, _half_rows(ins[w].shape[2], 1 - c)], outs[w], sems, w, (x, y, 1 - c))
            cp.start()
            pend.append(cp)
        for cp in pend:
            cp.wait_recv()
            cp.wait_send()

    out_shapes = [jax.ShapeDtypeStruct((a.shape[0], 4, a.shape[2] // 2, a.shape[3]), a.dtype) for a in parts]
    return _hbm_call(body, "grad_sibling_exchange", parts, out_shapes, n)


def _chip_scatter(sums):
    n = len(sums)

    def body(*refs):
        ins, outs, sems = refs[:n], refs[n:2 * n], refs[2 * n:2 * n + 2]
        x, y, c, others = _place()
        me = 2 * x + y
        sends = []
        for w in range(n):
            for j, (px, py) in enumerate(others):
                cp = _remote(ins[w].at[:, 2 * px + py], outs[w].at[:, me], sems, 3 * w + j, (px, py, c))
                cp.start()
                sends.append(cp)
        for w in range(n):
            for j, (px, py) in enumerate(others):
                landed = outs[w].at[:, 2 * px + py]
                _remote(landed, landed, sems, 3 * w + j, (px, py, c)).wait_recv()
        for cp in sends:
            cp.wait_send()

    out_shapes = [jax.ShapeDtypeStruct(a.shape, a.dtype) for a in sums]
    return _hbm_call(body, "grad_chip_scatter", sums, out_shapes, 3 * n)


def _sibling_join(shards):
    n = len(shards)

    def body(*refs):
        ins, outs, sems = refs[:n], refs[n:2 * n], refs[2 * n:2 * n + 2]
        x, y, c, _ = _place()
        pend = []
        for w in range(n):
            rows = ins[w].shape[1]
            mine, theirs = _half_rows(rows, c), _half_rows(rows, 1 - c)
            cp = _remote(ins[w].at[:, mine], outs[w].at[:, mine], sems, w, (x, y, 1 - c))
            cp.start()
            pend.append((cp, _remote(ins[w].at[:, theirs], outs[w].at[:, theirs], sems, w, (x, y, 1 - c))))
        for cp, landed in pend:
            landed.wait_recv()
            cp.wait_send()

    out_shapes = [jax.ShapeDtypeStruct(a.shape, a.dtype) for a in shards]
    return _hbm_call(body, "grad_sibling_join", shards, out_shapes, n, aliases={w: w for w in range(n)})


SMALL_ROWS = 136


def _all_reduce_small(vec):
    def body(v_ref, o_ref, buf, send, recv, loc):
        x, y, c, _ = _place()
        me = 4 * x + 2 * y + c
        lc = pltpu.make_async_copy(v_ref, buf.at[me], loc.at[0])
        lc.start()
        cps = []
        for k in range(1, 8):
            fx, fy, fc = (k >> 2) & 1, (k >> 1) & 1, k & 1
            peer = (x ^ fx, y ^ fy, c ^ fc)
            cp = pltpu.make_async_remote_copy(src_ref=v_ref, dst_ref=buf.at[me], send_sem=send.at[k - 1],
                                              recv_sem=recv.at[k - 1], device_id=peer, device_id_type=MESH)
            cp.start()
            cps.append((cp, 4 * peer[0] + 2 * peer[1] + peer[2]))
        for k, (cp, src) in enumerate(cps):
            pltpu.make_async_remote_copy(src_ref=v_ref, dst_ref=buf.at[src], send_sem=send.at[k], recv_sem=recv.at[k],
                                         device_id=(x, y, c), device_id_type=MESH).wait_recv()
        for cp, _ in cps:
            cp.wait_send()
        lc.wait()
        total = buf[0]
        for k in range(1, 8):
            total = total + buf[k]
        o_ref[...] = total

    vm = pl.BlockSpec(memory_space=pltpu.VMEM)
    return pl.pallas_call(
        body, name="all_reduce_small", in_specs=[vm], out_specs=vm,
        out_shape=jax.ShapeDtypeStruct(vec.shape, F32),
        scratch_shapes=[pltpu.VMEM((8,) + vec.shape, F32), pltpu.SemaphoreType.DMA((7,)),
                        pltpu.SemaphoreType.DMA((7,)), pltpu.SemaphoreType.DMA((1,))],
    )(vec)


SUM_ROWS = 256


def _add_pairs(part, theirs, name):
    l, four, rh, cc = theirs.shape
    tr = min(SUM_ROWS, rh)
    halves = part.reshape(l, four, 2, rh, cc)

    def body(a_ref, b_ref, o_ref):
        mine = a_ref[0, 0, lax.axis_index("c")]
        o_ref[0, 0] = (mine.astype(F32) + b_ref[0, 0].astype(F32)).astype(o_ref.dtype)

    spec = pl.BlockSpec((1, 1, tr, cc), lambda i, k, r: (i, k, r, 0))
    return pl.pallas_call(
        body, name=name, grid=(l, four, rh // tr),
        in_specs=[pl.BlockSpec((1, 1, 2, tr, cc), lambda i, k, r: (i, k, 0, r, 0)), spec], out_specs=spec,
        out_shape=jax.ShapeDtypeStruct(theirs.shape, theirs.dtype),
        compiler_params=_params(("parallel", "parallel", "parallel")),
    )(halves, theirs)


def _sum_chips(own, arrived, core, name):
    l, four, rh, cc = own.shape
    tr = min(SUM_ROWS, rh)
    nr = rh // tr

    def body(c_ref, own_ref, arr_ref, o_ref):
        x, y = lax.axis_index("x"), lax.axis_index("y")
        tot = own_ref[0, 2 * x + y].astype(F32)
        for px, py in ((1 - x, y), (x, 1 - y), (1 - x, 1 - y)):
            tot = tot + arr_ref[0, 2 * px + py].astype(F32)
        o_ref[0] = tot

    blk = pl.BlockSpec((1, 4, tr, cc), lambda i, r, c_ref: (i, 0, r, 0))
    return pl.pallas_call(
        body, name=name,
        grid_spec=pltpu.PrefetchScalarGridSpec(
            num_scalar_prefetch=1, grid=(l, nr), in_specs=[blk, blk],
            out_specs=pl.BlockSpec((1, tr, cc), lambda i, r, c_ref: (i, c_ref[0] * nr + r, 0))),
        out_shape=jax.ShapeDtypeStruct((l, 2 * rh, cc), F32),
        compiler_params=_params(("parallel", "parallel")),
    )(core, own, arrived)


ADAM_ROWS = 256


def _adamw(w, g, m, v, name):
    shape = w.shape
    cc = shape[-1]
    flat = lambda a: a.reshape(-1, cc)
    rows = flat(w).shape[0]
    tr = min(ADAM_ROWS, rows)
    assert rows % tr == 0
    c1 = 1.0 - ADAM_B1 ** ADAM_STEP
    c2 = 1.0 - ADAM_B2 ** ADAM_STEP

    def body(w_ref, g_ref, m_ref, v_ref, d_ref, nm_ref, nv_ref):
        gv = g_ref[...]
        nm = ADAM_B1 * m_ref[...] + (1.0 - ADAM_B1) * gv
        nv = ADAM_B2 * v_ref[...] + (1.0 - ADAM_B2) * (gv * gv)
        nm_ref[...] = nm
        nv_ref[...] = nv
        d_ref[...] = -ADAM_LR * ((nm / c1) / (jnp.sqrt(nv / c2) + ADAM_EPS) + ADAM_WD * w_ref[...])

    spec = pl.BlockSpec((tr, cc), lambda i: (i, 0))
    sh = jax.ShapeDtypeStruct((rows, cc), F32)
    outs = pl.pallas_call(
        body, name=name, grid=(rows // tr,), in_specs=[spec] * 4, out_specs=[spec] * 3, out_shape=[sh] * 3,
        compiler_params=_params(("parallel",)),
    )(flat(w), flat(g), flat(m), flat(v))
    return [o.reshape(shape) for o in outs]


def _pack_small(g_pre, g_post, sinks_a, b_f_c, loss_row):
    pad = lambda a: jnp.pad(a.reshape(1, -1).astype(F32), ((0, 0), (0, LANES - a.size)))
    rows = [g_pre.astype(F32).reshape(-1, LANES), g_post.astype(F32).reshape(-1, LANES), pad(sinks_a), pad(b_f_c), loss_row]
    packed = jnp.concatenate(rows, axis=0)
    return jnp.pad(packed, ((0, SMALL_ROWS - packed.shape[0]), (0, 0)))


def _unpack_small(p):
    n = DEPTH * D_MODEL // LANES
    return (p[:n].reshape(DEPTH, D_MODEL), p[n:2 * n].reshape(DEPTH, D_MODEL), p[2 * n, :2 * N_HEADS].reshape(2, N_HEADS),
            p[2 * n + 1, :N_HEADS].reshape(1, N_HEADS), p[2 * n + 2, 0])


def kernel(x, g_pre, g_post, w_in_a, w_out_a, sinks_a, w_in_b, w_out_b, w_in_c, b_f_c, w_out_c, loss_target, m_g_pre, m_g_post, m_w_in_a, m_w_out_a, m_sinks_a, m_w_in_b, m_w_out_b, m_w_in_c, m_b_f_c, m_w_out_c, v_g_pre, v_g_post, v_w_in_a, v_w_out_a, v_sinks_a, v_w_in_b, v_w_out_b, v_w_in_c, v_b_f_c, v_w_out_c):
    big_w = [w_in_a, w_out_a, w_in_b, w_out_b, w_in_c, w_out_c]
    big_m = [m_w_in_a, m_w_out_a, m_w_in_b, m_w_out_b, m_w_in_c, m_w_out_c]
    big_v = [v_w_in_a, v_w_out_a, v_w_in_b, v_w_out_b, v_w_in_c, v_w_out_c]

    chip = 2 * lax.axis_index("x") + lax.axis_index("y")
    core = lax.axis_index("c").astype(jnp.int32).reshape(1)
    shards = [w.astype(BF16) for w in big_w]
    gathered = [lax.dynamic_update_slice(g, sh[:, None], (0, chip, 0, 0))
                for g, sh in zip(_all_gather_shards(shards), shards)]
    cols = lambda g, j: g[j].transpose(1, 0, 2).reshape(g.shape[2], 4 * g.shape[3])
    rows = lambda g, j: g[j].reshape(4 * g.shape[2], g.shape[3])
    w_c = cols(gathered[4], 0)
    w_in = {0: [cols(gathered[0], j) for j in range(2)], 1: [cols(gathered[2], 0)], 2: [w_c[:, :4 * BRANCH]]}
    w_out = {0: [rows(gathered[1], j) for j in range(2)], 1: [rows(gathered[3], 0)], 2: [rows(gathered[5], 0)]}
    wf_t = [w_c[:, 4 * BRANCH:].T]

    res = _forward_backward(x[0], loss_target[0], g_pre, g_post, sinks_a, b_f_c, w_in, w_out, wf_t)

    col_parts = lambda gs: jnp.stack([g.reshape(g.shape[0], 4, g.shape[1] // 4).transpose(1, 0, 2) for g in gs]).astype(BF16)
    row_parts = lambda gs: jnp.stack([g.reshape(4, g.shape[0] // 4, g.shape[1]) for g in gs]).astype(BF16)
    parts = [col_parts(res["w_in"][0]), row_parts(res["w_out"][0]), col_parts(res["w_in"][1]), row_parts(res["w_out"][1]),
             col_parts(res["w_in"][2]), row_parts(res["w_out"][2])]
    names = ["w_in_a", "w_out_a", "w_in_b", "w_out_b", "w_in_c", "w_out_c"]
    theirs = _sibling_send(parts)
    chip_sums = [_add_pairs(a, b, f"chip_sum_{nm}") for a, b, nm in zip(parts, theirs, names)]
    arrived = _chip_scatter(chip_sums)
    halves = [_sum_chips(own, arr, core, f"shard_sum_{nm}") for own, arr, nm in zip(chip_sums, arrived, names)]
    grads = _sibling_join(halves)

    small = _unpack_small(_all_reduce_small(
        _pack_small(res["g_pre"], res["g_post"], res["sinks_a"], res["b_f_c"], res["loss"])))
    g_small, loss = small[:4], small[4]

    zero_row = jnp.zeros((1, LANES), F32)
    pk = lambda a: _pack_small(a[0], a[1], a[2], a[3], zero_row)
    sm = _adamw(pk([g_pre, g_post, sinks_a, b_f_c]), pk(g_small), pk([m_g_pre, m_g_post, m_sinks_a, m_b_f_c]),
                pk([v_g_pre, v_g_post, v_sinks_a, v_b_f_c]), "adamw_small")
    sm = [_unpack_small(a)[:4] for a in sm]
    bigs = [_adamw(w, g, m, v, f"adamw_{nm}") for w, g, m, v, nm in zip(big_w, grads, big_m, big_v, names)]

    def ordered(small4, big6):
        return [small4[0], small4[1], big6[0], big6[1], small4[2], big6[2], big6[3], big6[4], small4[3], big6[5]]

    out = [loss, res["dx"][None], *ordered(g_small, grads)]
    for k in range(3):
        out += ordered(sm[k], [b[k] for b in bigs])
    return tuple(out)
```

```python
import functools
import math

import numpy as np
import jax
import jax.numpy as jnp
from jax import lax
from jax.experimental import pallas as pl
from jax.experimental.pallas import tpu as pltpu

F32 = jnp.float32
BF16 = jnp.bfloat16

D_MODEL = 2048
DEPTH = 4
N_HEADS = 32
HEAD_DIM = 64
LANES = 128
N_PAIRS = N_HEADS * HEAD_DIM // LANES
BRANCH = N_HEADS * HEAD_DIM
N_KV_A = 4
KV_A = N_KV_A * HEAD_DIM
WINDOW = 128
NORM_EPS = 1e-6
NEG = -1e30
Q_SCALE = HEAD_DIM ** -0.5

A_QKV = BRANCH + 2 * KV_A
B_QKV = 3 * BRANCH

ADAM_LR = 0.001
ADAM_B1 = 0.9
ADAM_B2 = 0.999
ADAM_EPS = 1e-08
ADAM_WD = 0.01
ADAM_STEP = 10

MESH = pl.DeviceIdType.MESH

_NT = (((1,), (1,)), ((), ()))
_TN = (((0,), (0,)), ((), ()))


def _params(sem=None):
    return pltpu.CompilerParams(dimension_semantics=sem)


def _matmul(a, b, *, mode, out_dtype, name, n=None, b_off=0, tm=1024, tn=1024, tk=2048):
    if mode == "nn":
        (m, k), nn = a.shape, (n or b.shape[1])
    elif mode == "nt":
        (m, k), nn = a.shape, b.shape[0]
    else:
        (k, m), nn = a.shape, b.shape[1]
    tm, tn, tk = min(tm, m), min(tn, nn), min(tk, k)
    assert m % tm == 0 and nn % tn == 0 and k % tk == 0, (name, m, nn, k, tm, tn, tk)
    nk = k // tk

    def body(a_ref, b_ref, o_ref, acc_ref):
        kk = pl.program_id(2)
        if mode == "nn":
            p = jnp.dot(a_ref[...], b_ref[...], preferred_element_type=F32)
        elif mode == "nt":
            p = lax.dot_general(a_ref[...], b_ref[...], _NT, preferred_element_type=F32)
        else:
            p = lax.dot_general(a_ref[...], b_ref[...], _TN, preferred_element_type=F32)
        if nk == 1:
            o_ref[...] = p.astype(o_ref.dtype)
        else:
            @pl.when(kk == 0)
            def _():
                acc_ref[...] = p

            @pl.when(kk > 0)
            def _():
                acc_ref[...] += p

            @pl.when(kk == nk - 1)
            def _():
                o_ref[...] = acc_ref[...].astype(o_ref.dtype)

    if mode == "nn":
        in_specs = [pl.BlockSpec((tm, tk), lambda i, j, kk: (i, kk)),
                    pl.BlockSpec((tk, tn), lambda i, j, kk: (kk, j + b_off))]
    elif mode == "nt":
        in_specs = [pl.BlockSpec((tm, tk), lambda i, j, kk: (i, kk)),
                    pl.BlockSpec((tn, tk), lambda i, j, kk: (j, kk))]
    else:
        in_specs = [pl.BlockSpec((tk, tm), lambda i, j, kk: (kk, i)),
                    pl.BlockSpec((tk, tn), lambda i, j, kk: (kk, j))]
    return pl.pallas_call(
        body, name=name, grid=(m // tm, nn // tn, nk),
        in_specs=in_specs,
        out_specs=pl.BlockSpec((tm, tn), lambda i, j, kk: (i, j)),
        out_shape=jax.ShapeDtypeStruct((m, nn), out_dtype),
        scratch_shapes=[pltpu.VMEM((tm, tn), F32)],
        compiler_params=_params(("parallel", "parallel", "arbitrary")),
    )(a, b)


ROW_TILE = 256


def _row_call(body, name, ins, outs, *, s, acc_outs=()):
    tr = min(ROW_TILE, s)
    row = lambda w: pl.BlockSpec((tr, w), lambda i: (i, 0))
    vec = lambda w: pl.BlockSpec((1, w), lambda i: (0, 0))
    in_specs = [row(a.shape[1]) if kind == "row" else vec(a.shape[1]) for a, kind in ins]
    out_specs = [row(sh.shape[1]) if kind == "row" else vec(sh.shape[1]) for sh, kind in outs]
    return pl.pallas_call(
        body, name=name, grid=(s // tr,), in_specs=in_specs, out_specs=out_specs,
        out_shape=[sh for sh, _ in outs],
        compiler_params=_params(("arbitrary",)),
    )(*[a for a, _ in ins])


def _rsqrt_ms(v):
    return lax.rsqrt(jnp.mean(v * v, axis=-1, keepdims=True) + NORM_EPS)


def _rmsnorm_fwd(x, g, name):
    s, d = x.shape

    def body(x_ref, g_ref, h_ref):
        xv = x_ref[...]
        h_ref[...] = (xv * _rsqrt_ms(xv) * g_ref[...]).astype(BF16)

    return _row_call(body, name, [(x, "row"), (g, "vec")],
                     [(jax.ShapeDtypeStruct((s, d), BF16), "row")], s=s)[0]


def _gate_fwd(o, z, name):
    s, d = o.shape

    def body(o_ref, z_ref, u_ref):
        zv = z_ref[...]
        u_ref[...] = (o_ref[...] * (zv * jax.nn.sigmoid(zv))).astype(BF16)

    return _row_call(body, name, [(o, "row"), (z, "row")],
                     [(jax.ShapeDtypeStruct((s, d), BF16), "row")], s=s)[0]


def _post_fwd(x, y, g, name):
    s, d = x.shape

    def body(x_ref, y_ref, g_ref, o_ref):
        yv = y_ref[...]
        o_ref[...] = x_ref[...] + yv * _rsqrt_ms(yv) * g_ref[...]

    return _row_call(body, name, [(x, "row"), (y, "row"), (g, "vec")],
                     [(jax.ShapeDtypeStruct((s, d), F32), "row")], s=s)[0]


def _loss_and_grad(x, target):
    s, d = x.shape

    def body(x_ref, t_ref, dx_ref, l_ref):
        err = x_ref[...] - t_ref[...]
        dx_ref[...] = err * (1.0 / d)
        part = jnp.sum(jnp.sum(err * err, axis=1, keepdims=True), axis=0, keepdims=True) * (0.5 / d)

        @pl.when(pl.program_id(0) == 0)
        def _():
            l_ref[...] = jnp.zeros_like(l_ref)

        l_ref[...] += jnp.broadcast_to(part, l_ref.shape)

    return _row_call(body, "loss_head", [(x, "row"), (target, "row")],
                     [(jax.ShapeDtypeStruct((s, d), F32), "row"),
                      (jax.ShapeDtypeStruct((1, LANES), F32), "vec")], s=s)


def _norm_bwd_rows(dn, v, g):
    r = _rsqrt_ms(v)
    a = dn * g
    dv = r * (a - v * (r * r) * jnp.mean(a * v, axis=-1, keepdims=True))
    return dv, dn * v * r


def _post_bwd(dx, y, g, name):
    s, d = dx.shape

    def body(dx_ref, y_ref, g_ref, dy_ref, dg_ref):
        dy, dg = _norm_bwd_rows(dx_ref[...], y_ref[...], g_ref[...])
        dy_ref[...] = dy.astype(BF16)

        @pl.when(pl.program_id(0) == 0)
        def _():
            dg_ref[...] = jnp.zeros_like(dg_ref)

        dg_ref[...] += jnp.sum(dg, axis=0, keepdims=True)

    return _row_call(body, name, [(dx, "row"), (y, "row"), (g, "vec")],
                     [(jax.ShapeDtypeStruct((s, d), BF16), "row"),
                      (jax.ShapeDtypeStruct((1, d), F32), "vec")], s=s)


def _gate_bwd(du, o, z, name):
    s, d = du.shape

    def body(du_ref, o_ref, z_ref, do_ref, dz_ref):
        duv, zv = du_ref[...], z_ref[...]
        sig = jax.nn.sigmoid(zv)
        do_ref[...] = (duv * (zv * sig)).astype(BF16)
        dz_ref[...] = (duv * o_ref[...] * (sig * (1.0 + zv * (1.0 - sig)))).astype(BF16)

    return _row_call(body, name, [(du, "row"), (o, "row"), (z, "row")],
                     [(jax.ShapeDtypeStruct((s, d), BF16), "row"),
                      (jax.ShapeDtypeStruct((s, d), BF16), "row")], s=s)


def _pre_bwd(dx, dhs, x, g, name):
    s, d = dx.shape
    n_dh = len(dhs)

    def body(*refs):
        dx_ref, dh_refs, (x_ref, g_ref, o_ref, dg_ref) = refs[0], refs[1:1 + n_dh], refs[1 + n_dh:]
        dh = dh_refs[0][...].astype(F32)
        for r in dh_refs[1:]:
            dh = dh + r[...].astype(F32)
        dv, dg = _norm_bwd_rows(dh, x_ref[...], g_ref[...])
        o_ref[...] = dx_ref[...] + dv

        @pl.when(pl.program_id(0) == 0)
        def _():
            dg_ref[...] = jnp.zeros_like(dg_ref)

        dg_ref[...] += jnp.sum(dg, axis=0, keepdims=True)

    return _row_call(body, name, [(dx, "row")] + [(h, "row") for h in dhs] + [(x, "row"), (g, "vec")],
                     [(jax.ShapeDtypeStruct((s, d), F32), "row"),
                      (jax.ShapeDtypeStruct((1, d), F32), "vec")], s=s)


def _lane_is_first_head():
    return lax.broadcasted_iota(jnp.int32, (1, LANES), 1) < HEAD_DIM


def _bcast_lanes(col):
    return jnp.broadcast_to(col, (col.shape[0], LANES))


def _pair_spec(s, off=0, width=LANES):
    return pl.BlockSpec((s, width), lambda p: (0, p + off))


def _stack_heads(pair, first):
    return jnp.concatenate([jnp.where(first, pair, 0), jnp.where(first, 0, pair)], axis=0).astype(BF16)


def _stacked_mask(t, strict):
    row = lax.broadcasted_iota(jnp.int32, (2 * t, t), 0)
    col = lax.broadcasted_iota(jnp.int32, (2 * t, t), 1)
    query = jnp.where(row >= t, row - t, row)
    return col < query if strict else col <= query


def _rowsum_heads(prod, first):
    return (jnp.sum(jnp.where(first, prod, 0.0), axis=1, keepdims=True),
            jnp.sum(jnp.where(first, 0.0, prod), axis=1, keepdims=True))


def _softplus_parts(z):
    e = jnp.exp(-jnp.abs(z))
    sp = jnp.maximum(z, 0.0) + jnp.log(1.0 + e)
    r = 1.0 / (1.0 + e)
    return sp, jnp.where(z >= 0, r, e * r)


def _split_dot(x, t):
    hi = x.astype(BF16)
    lo = (x - hi.astype(F32)).astype(BF16)
    return jnp.dot(hi, t, preferred_element_type=F32) + jnp.dot(lo, t, preferred_element_type=F32)


def _sb_tile(s):
    return min(256, s)


def _attn_b_fwd(qkv, name):
    s = qkv.shape[0]
    t = _sb_tile(s)
    nq = s // t

    def body(q_ref, k_ref, v_ref, o_ref, lt_ref):
        first = _lane_is_first_head()
        before = _stacked_mask(t, strict=True)
        tri = (lax.broadcasted_iota(jnp.int32, (t, t), 0) >= lax.broadcasted_iota(jnp.int32, (t, t), 1)).astype(BF16)

        def tile(j, carry, diag, qs):
            c, acc = carry
            c0 = pl.multiple_of(j * t, t)
            k2 = k_ref[pl.ds(c0, t), :]
            v2 = v_ref[pl.ds(c0, t), :]
            z = lax.dot_general(qs, k2, _NT, preferred_element_type=F32)
            sp, _ = _softplus_parts(z)
            lf = jnp.where(before, -sp, 0.0) if diag else -sp
            incl = jnp.dot(lf.astype(BF16), tri, preferred_element_type=F32)
            a = jnp.exp(z + c + incl)
            if diag:
                a = jnp.where(before, a, 0.0)
            pv = jnp.dot(a.astype(BF16), v2, preferred_element_type=F32)
            return c + incl[:, 0:1], acc + jnp.where(first, pv[:t], pv[t:])

        def qblock(i, _):
            r0 = pl.multiple_of(i * t, t)
            qs = _stack_heads(q_ref[pl.ds(r0, t), :] * Q_SCALE, first)
            carry = tile(i, (jnp.zeros((2 * t, 1), F32), jnp.zeros((t, LANES), F32)), True, qs)
            carry = lax.fori_loop(0, i, lambda jj, c: tile(i - 1 - jj, c, False, qs), carry)
            o_ref[pl.ds(r0, t), :] = carry[1]
            lt_ref[pl.ds(r0, t), 0:LANES] = _bcast_lanes(carry[0][:t])
            lt_ref[pl.ds(r0, t), LANES:2 * LANES] = _bcast_lanes(carry[0][t:])
            return 0

        lax.fori_loop(0, nq, qblock, 0)

    return pl.pallas_call(
        body, name=name, grid=(N_PAIRS,),
        in_specs=[_pair_spec(s), _pair_spec(s, N_PAIRS), _pair_spec(s, 2 * N_PAIRS)],
        out_specs=[_pair_spec(s), _stat_spec(s)],
        out_shape=[jax.ShapeDtypeStruct((s, BRANCH), F32), jax.ShapeDtypeStruct((s, N_HEADS * LANES), F32)],
        compiler_params=_params(("parallel",)),
    )(qkv, qkv, qkv)


def _attn_b_bwd(qkv, ltot, do, name):
    s = qkv.shape[0]
    t = _sb_tile(s)
    nq = s // t

    def body(q_ref, k_ref, v_ref, lt_ref, do_ref, dq_ref, dk_ref, dv_ref, dk_acc, dv_acc):
        first = _lane_is_first_head()
        before = _stacked_mask(t, strict=True)
        tri = (lax.broadcasted_iota(jnp.int32, (t, t), 0) <= lax.broadcasted_iota(jnp.int32, (t, t), 1)).astype(BF16)
        dk_acc[...] = jnp.zeros_like(dk_acc)
        dv_acc[...] = jnp.zeros_like(dv_acc)

        def tile(j, carry, diag, qs, dos, lt):
            p_l, p_g, dq_acc = carry
            c0 = pl.multiple_of(j * t, t)
            k2 = k_ref[pl.ds(c0, t), :]
            v2 = v_ref[pl.ds(c0, t), :]
            z = lax.dot_general(qs, k2, _NT, preferred_element_type=F32)
            sp, sig = _softplus_parts(z)
            lf = jnp.where(before, -sp, 0.0) if diag else -sp
            pref_l = jnp.dot(lf.astype(BF16), tri, preferred_element_type=F32)
            a = jnp.exp(z + ((lt - p_l) - pref_l + lf))
            if diag:
                a = jnp.where(before, a, 0.0)
            g = a * lax.dot_general(dos, v2, _NT, preferred_element_type=F32)
            pref_g = jnp.dot(g.astype(BF16), tri, preferred_element_type=F32)
            dz = g - sig * (p_g + pref_g)
            if diag:
                dz = jnp.where(before, dz, 0.0)
            dzb = dz.astype(BF16)
            dq = jnp.dot(dzb, k2, preferred_element_type=F32)
            dk_acc[pl.ds(c0, t), :] += lax.dot_general(dzb, qs, _TN, preferred_element_type=F32)
            dv_acc[pl.ds(c0, t), :] += lax.dot_general(a.astype(BF16), dos, _TN, preferred_element_type=F32)
            return p_l + pref_l[:, t - 1:t], p_g + pref_g[:, t - 1:t], dq_acc + jnp.where(first, dq[:t], dq[t:])

        def qblock(i, _):
            r0 = pl.multiple_of(i * t, t)
            qs = _stack_heads(q_ref[pl.ds(r0, t), :] * Q_SCALE, first)
            dos = _stack_heads(do_ref[pl.ds(r0, t), :], first)
            lt = jnp.concatenate([lt_ref[pl.ds(r0, t), 0:1], lt_ref[pl.ds(r0, t), LANES:LANES + 1]], axis=0)
            zero = jnp.zeros((2 * t, 1), F32)
            carry = (zero, zero, jnp.zeros((t, LANES), F32))
            carry = lax.fori_loop(0, i, lambda j, c: tile(j, c, False, qs, dos, lt), carry)
            carry = tile(i, carry, True, qs, dos, lt)
            dq_ref[pl.ds(r0, t), :] = (carry[2] * Q_SCALE).astype(BF16)
            return 0

        lax.fori_loop(0, nq, qblock, 0)
        dk_ref[...] = dk_acc[...].astype(BF16)
        dv_ref[...] = dv_acc[...].astype(BF16)

    out = jax.ShapeDtypeStruct((s, BRANCH), BF16)
    return pl.pallas_call(
        body, name=name, grid=(N_PAIRS,),
        in_specs=[_pair_spec(s), _pair_spec(s, N_PAIRS), _pair_spec(s, 2 * N_PAIRS), _stat_spec(s), _pair_spec(s)],
        out_specs=[_pair_spec(s)] * 3, out_shape=[out] * 3,
        scratch_shapes=[pltpu.VMEM((s, LANES), F32), pltpu.VMEM((s, LANES), F32)],
        compiler_params=_params(("parallel",)),
    )(qkv, qkv, qkv, ltot, do)


def _fox_tile(s):
    return min(256, s)


def _stat_spec(s):
    return pl.BlockSpec((s, 2 * LANES), lambda p: (0, p))


def _cum_spec(nt, t):
    return pl.BlockSpec((1, nt, 2, t), lambda p: (p, 0, 0, 0))


def _attn_c_fwd(qkv, cum4, name):
    s = qkv.shape[0]
    t = _fox_tile(s)
    nq = s // t

    def body(q_ref, k_ref, v_ref, c_ref, o_ref, lse_ref):
        first = _lane_is_first_head()
        causal = _stacked_mask(t, strict=False)

        def tile(j, carry, diag, qs):
            c0 = pl.multiple_of(j * t, t)
            k2 = k_ref[pl.ds(c0, t), :]
            v2 = v_ref[pl.ds(c0, t), :]
            cs = c_ref[0, j]
            m_prev, l_prev, acc = carry
            z = lax.dot_general(qs, k2, _NT, preferred_element_type=F32)
            sc = jnp.concatenate([z[:t] - cs[0:1, :], z[t:] - cs[1:2, :]], axis=0)
            if diag:
                sc = jnp.where(causal, sc, NEG)
            m_new = jnp.maximum(m_prev, jnp.max(sc, axis=1, keepdims=True))
            alpha = jnp.exp(m_prev - m_new)
            p = jnp.exp(sc - m_new)
            l_new = alpha * l_prev + jnp.sum(p, axis=1, keepdims=True)
            pv = jnp.dot(p.astype(BF16), v2, preferred_element_type=F32)
            acc = jnp.where(first, acc * alpha[:t] + pv[:t], acc * alpha[t:] + pv[t:])
            return m_new, l_new, acc

        def qblock(i, _):
            r0 = pl.multiple_of(i * t, t)
            qs = _stack_heads(q_ref[pl.ds(r0, t), :] * Q_SCALE, first)
            carry = (jnp.full((2 * t, 1), NEG, F32), jnp.zeros((2 * t, 1), F32), jnp.zeros((t, LANES), F32))
            carry = lax.fori_loop(0, i, lambda j, c: tile(j, c, False, qs), carry)
            m, l, acc = tile(i, carry, True, qs)
            inv = 1.0 / l
            lse = m + jnp.log(l)
            o_ref[pl.ds(r0, t), :] = acc * jnp.where(first, inv[:t], inv[t:])
            lse_ref[pl.ds(r0, t), 0:LANES] = _bcast_lanes(lse[:t])
            lse_ref[pl.ds(r0, t), LANES:2 * LANES] = _bcast_lanes(lse[t:])
            return 0

        lax.fori_loop(0, nq, qblock, 0)

    return pl.pallas_call(
        body, name=name, grid=(N_PAIRS,),
        in_specs=[_pair_spec(s), _pair_spec(s, N_PAIRS), _pair_spec(s, 2 * N_PAIRS), _cum_spec(nq, t)],
        out_specs=[_pair_spec(s), _stat_spec(s)],
        out_shape=[jax.ShapeDtypeStruct((s, BRANCH), F32), jax.ShapeDtypeStruct((s, N_HEADS * LANES), F32)],
        compiler_params=_params(("parallel",)),
    )(qkv, qkv, qkv, cum4)


def _attn_c_bwd(qkv, cum4, o, lse, do, name):
    s = qkv.shape[0]
    t = _fox_tile(s)
    nq = s // t

    def body(q_ref, k_ref, v_ref, c_ref, o_ref, lse_ref, do_ref, dq_ref, dk_ref, dv_ref, dc_ref, dk_acc, dv_acc):
        first = _lane_is_first_head()
        causal = _stacked_mask(t, strict=False)
        eye = lax.broadcasted_iota(jnp.int32, (t, t), 0) == lax.broadcasted_iota(jnp.int32, (t, t), 1)
        dk_acc[...] = jnp.zeros_like(dk_acc)
        dv_acc[...] = jnp.zeros_like(dv_acc)
        dc_ref[...] = jnp.zeros_like(dc_ref)

        def tile(j, carry, diag, qs, dos, delta, lse):
            dq_acc, rs = carry
            c0 = pl.multiple_of(j * t, t)
            k2 = k_ref[pl.ds(c0, t), :]
            v2 = v_ref[pl.ds(c0, t), :]
            cs = c_ref[0, j]
            z = lax.dot_general(qs, k2, _NT, preferred_element_type=F32)
            sc = jnp.concatenate([z[:t] - cs[0:1, :], z[t:] - cs[1:2, :]], axis=0)
            p = jnp.exp(sc - lse)
            if diag:
                p = jnp.where(causal, p, 0.0)
            ds = p * (lax.dot_general(dos, v2, _NT, preferred_element_type=F32) - delta)
            dsb = ds.astype(BF16)
            dq = jnp.dot(dsb, k2, preferred_element_type=F32)
            dk_acc[pl.ds(c0, t), :] += lax.dot_general(dsb, qs, _TN, preferred_element_type=F32)
            dv_acc[pl.ds(c0, t), :] += lax.dot_general(p.astype(BF16), dos, _TN, preferred_element_type=F32)
            col_sums = jnp.concatenate([jnp.sum(ds[:t], axis=0, keepdims=True), jnp.sum(ds[t:], axis=0, keepdims=True)], axis=0)
            dc_ref[0, j] = dc_ref[0, j] - col_sums
            return dq_acc + jnp.where(first, dq[:t], dq[t:]), rs + jnp.sum(ds, axis=1, keepdims=True)

        def qblock(i, _):
            r0 = pl.multiple_of(i * t, t)
            do2 = do_ref[pl.ds(r0, t), :]
            qs = _stack_heads(q_ref[pl.ds(r0, t), :] * Q_SCALE, first)
            dos = _stack_heads(do2, first)
            delta = jnp.concatenate(_rowsum_heads(do2.astype(F32) * o_ref[pl.ds(r0, t), :], first), axis=0)
            lse = jnp.concatenate([lse_ref[pl.ds(r0, t), 0:1], lse_ref[pl.ds(r0, t), LANES:LANES + 1]], axis=0)
            carry = (jnp.zeros((t, LANES), F32), jnp.zeros((2 * t, 1), F32))
            carry = lax.fori_loop(0, i, lambda j, c: tile(j, c, False, qs, dos, delta, lse), carry)
            dq_acc, rs = tile(i, carry, True, qs, dos, delta, lse)
            dq_ref[pl.ds(r0, t), :] = (dq_acc * Q_SCALE).astype(BF16)
            as_row = lambda col_vec: jnp.sum(jnp.where(eye, col_vec, 0.0), axis=0, keepdims=True)
            dc_ref[0, i] = dc_ref[0, i] + jnp.concatenate([as_row(rs[:t]), as_row(rs[t:])], axis=0)
            return 0

        lax.fori_loop(0, nq, qblock, 0)
        dk_ref[...] = dk_acc[...].astype(BF16)
        dv_ref[...] = dv_acc[...].astype(BF16)

    out = jax.ShapeDtypeStruct((s, BRANCH), BF16)
    return pl.pallas_call(
        body, name=name, grid=(N_PAIRS,),
        in_specs=[_pair_spec(s), _pair_spec(s, N_PAIRS), _pair_spec(s, 2 * N_PAIRS), _cum_spec(nq, t),
                  _pair_spec(s), _stat_spec(s), _pair_spec(s)],
        out_specs=[_pair_spec(s)] * 3 + [_cum_spec(nq, t)],
        out_shape=[out] * 3 + [jax.ShapeDtypeStruct(cum4.shape, F32)],
        scratch_shapes=[pltpu.VMEM((s, LANES), F32), pltpu.VMEM((s, LANES), F32)],
        compiler_params=_params(("parallel",)),
    )(qkv, qkv, qkv, cum4, o, lse, do)


FG_CHUNK = 512


def _tri_dot3(x, t):
    hi = x.astype(BF16)
    r1 = x - hi.astype(F32)
    mid = r1.astype(BF16)
    lo = (r1 - mid.astype(F32)).astype(BF16)
    return (jnp.dot(hi, t, preferred_element_type=F32) + jnp.dot(mid, t, preferred_element_type=F32)
            + jnp.dot(lo, t, preferred_element_type=F32))


def _fgate_fwd(h, wf_t, b_col, name):
    s = h.shape[0]
    c = min(FG_CHUNK, s)

    def body(h_ref, w_ref, b_ref, xf_ref, cum_ref, carry_ref):
        @pl.when(pl.program_id(0) == 0)
        def _():
            carry_ref[...] = jnp.zeros_like(carry_ref)

        xf = lax.dot_general(w_ref[...], h_ref[...], _NT, preferred_element_type=F32) + b_ref[:, 0:1]
        xf_ref[...] = xf
        logf = jnp.minimum(xf, 0.0) - jnp.log(1.0 + jnp.exp(-jnp.abs(xf)))
        row = lax.broadcasted_iota(jnp.int32, (c, c), 0)
        col = lax.broadcasted_iota(jnp.int32, (c, c), 1)
        cum = _tri_dot3(logf, (row <= col).astype(BF16)) + carry_ref[:, 0:1]
        cum_ref[...] = cum
        carry_ref[...] = _bcast_lanes(cum[:, c - 1:c])

    out = jax.ShapeDtypeStruct((N_HEADS, s), F32)
    return pl.pallas_call(
        body, name=name, grid=(s // c,),
        in_specs=[pl.BlockSpec((c, D_MODEL), lambda i: (i, 0)),
                  pl.BlockSpec((N_HEADS, D_MODEL), lambda i: (0, 0)),
                  pl.BlockSpec((N_HEADS, LANES), lambda i: (0, 0))],
        out_specs=[pl.BlockSpec((N_HEADS, c), lambda i: (0, i))] * 2,
        out_shape=[out, out],
        scratch_shapes=[pltpu.VMEM((N_HEADS, LANES), F32)],
        compiler_params=_params(("arbitrary",)),
    )(h, wf_t, b_col)


def _fgate_bwd(dcum, xf, h, wf_t, name):
    s = h.shape[0]
    c = min(FG_CHUNK, s)
    n = s // c

    def body(dc_ref, xf_ref, h_ref, w_ref, dw_ref, dh_ref, db_ref, carry_ref):
        @pl.when(pl.program_id(0) == 0)
        def _():
            carry_ref[...] = jnp.zeros_like(carry_ref)
            dw_ref[...] = jnp.zeros_like(dw_ref)
            db_ref[...] = jnp.zeros_like(db_ref)

        row = lax.broadcasted_iota(jnp.int32, (c, c), 0)
        col = lax.broadcasted_iota(jnp.int32, (c, c), 1)
        dlogf = _tri_dot3(dc_ref[...], (row >= col).astype(BF16)) + carry_ref[:, 0:1]
        carry_ref[...] = _bcast_lanes(dlogf[:, 0:1])
        xf = xf_ref[...]
        e = jnp.exp(-jnp.abs(xf))
        r = 1.0 / (1.0 + e)
        dxf = dlogf * jnp.where(xf >= 0, e * r, r)
        db_ref[...] += _bcast_lanes(jnp.sum(dxf, axis=1, keepdims=True))
        dxb = dxf.astype(BF16)
        dw_ref[...] += jnp.dot(dxb, h_ref[...], preferred_element_type=F32)
        dh_ref[...] = lax.dot_general(dxb, w_ref[...], _TN, preferred_element_type=F32)

    rev = lambda i: n - 1 - i
    return pl.pallas_call(
        body, name=name, grid=(n,),
        in_specs=[pl.BlockSpec((N_HEADS, c), lambda i: (0, rev(i))),
                  pl.BlockSpec((N_HEADS, c), lambda i: (0, rev(i))),
                  pl.BlockSpec((c, D_MODEL), lambda i: (rev(i), 0)),
                  pl.BlockSpec((N_HEADS, D_MODEL), lambda i: (0, 0))],
        out_specs=[pl.BlockSpec((N_HEADS, D_MODEL), lambda i: (0, 0)),
                   pl.BlockSpec((c, D_MODEL), lambda i: (rev(i), 0)),
                   pl.BlockSpec((N_HEADS, LANES), lambda i: (0, 0))],
        out_shape=[jax.ShapeDtypeStruct((N_HEADS, D_MODEL), F32), jax.ShapeDtypeStruct((s, D_MODEL), F32),
                   jax.ShapeDtypeStruct((N_HEADS, LANES), F32)],
        scratch_shapes=[pltpu.VMEM((N_HEADS, LANES), F32)],
        compiler_params=_params(("arbitrary",)),
    )(dcum, xf, h, wf_t)


def _to_cum4(v, t):
    s = v.shape[1]
    return v.reshape(N_PAIRS, 2, s // t, t).transpose(0, 2, 1, 3)


def _from_cum4(v4):
    p, nt, two, t = v4.shape
    return v4.transpose(0, 2, 1, 3).reshape(p * two, nt * t)


def _alibi_slopes():
    return (2.0 ** (-8.0 * np.arange(1, N_HEADS + 1, dtype=np.float32) / N_HEADS)).astype(np.float32)


def _per_head_lanes(v):
    return jnp.repeat(v.astype(F32).reshape(N_PAIRS, 1, 2), LANES, axis=2)


def _attn_a_specs(s):
    q = _pair_spec(s)
    k = pl.BlockSpec((s, LANES), lambda p: (0, N_PAIRS + p // 8))
    v = pl.BlockSpec((s, LANES), lambda p: (0, N_PAIRS + KV_A // LANES + p // 8))
    head = pl.BlockSpec((1, 1, 2 * LANES), lambda p: (p, 0, 0))
    return q, k, v, head


def _attn_a_geometry(p, slope_ref, sink_ref):
    kv_half = (p // 4) % 2
    kv_first = kv_half == 0
    lane_first = _lane_is_first_head()
    kv_lanes = (lax.broadcasted_iota(jnp.int32, (1, LANES), 1) // HEAD_DIM) == kv_half
    row = lax.broadcasted_iota(jnp.int32, (2 * WINDOW, 2 * WINDOW), 0)
    cj = lax.broadcasted_iota(jnp.int32, (2 * WINDOW, 2 * WINDOW), 1)
    second = row >= WINDOW
    dist = WINDOW + jnp.where(second, row - WINDOW, row) - cj
    valid = (dist >= 0) & (dist < WINDOW)
    per_row = lambda ref: jnp.where(second[:, 0:1], ref[0, :, LANES:LANES + 1], ref[0, :, 0:1])
    return kv_first, lane_first, kv_lanes, per_row(slope_ref) * dist.astype(F32), valid, per_row(sink_ref)


def _swap_halves(x):
    return pltpu.roll(x, HEAD_DIM, 1)


def _attn_a_fwd(qkv, slopes, sinks, name):
    s = qkv.shape[0]
    nb = s // WINDOW

    def body(q_ref, k_ref, v_ref, sl_ref, sk_ref, o_ref, lse_ref):
        kv_first, lane_first, kv_lanes, bias, valid, sink = _attn_a_geometry(pl.program_id(0), sl_ref, sk_ref)

        def block(r0, k0, width):
            q2 = q_ref[pl.ds(r0, WINDOW), :].astype(F32) * Q_SCALE
            q2r = _swap_halves(q2)
            xs = jnp.concatenate([jnp.where(kv_first, q2, q2r), jnp.where(kv_first, q2r, q2)], axis=0).astype(BF16)
            km = jnp.where(kv_lanes, k_ref[pl.ds(k0, width), :], 0).astype(BF16)
            vm = jnp.where(kv_lanes, v_ref[pl.ds(k0, width), :], 0).astype(BF16)
            sc = lax.dot_general(xs, km, _NT, preferred_element_type=F32) - bias[:, 2 * WINDOW - width:]
            sc = jnp.where(valid[:, 2 * WINDOW - width:], sc, NEG)
            m = jnp.maximum(jnp.max(sc, axis=1, keepdims=True), sink)
            pr = jnp.exp(sc - m)
            l = jnp.sum(pr, axis=1, keepdims=True) + jnp.exp(sink - m)
            os = jnp.dot(pr.astype(BF16), vm, preferred_element_type=F32) * (1.0 / l)
            lse = m + jnp.log(l)
            lse_ref[pl.ds(r0, WINDOW), 0:LANES] = _bcast_lanes(lse[:WINDOW])
            lse_ref[pl.ds(r0, WINDOW), LANES:2 * LANES] = _bcast_lanes(lse[WINDOW:])
            oa = jnp.where(kv_first, os[:WINDOW], _swap_halves(os[:WINDOW]))
            ob = jnp.where(kv_first, _swap_halves(os[WINDOW:]), os[WINDOW:])
            o_ref[pl.ds(r0, WINDOW), :] = jnp.where(lane_first, oa, ob)

        block(0, 0, WINDOW)

        def loop(n, _):
            r0 = pl.multiple_of(n * WINDOW, WINDOW)
            block(r0, pl.multiple_of(r0 - WINDOW, WINDOW), 2 * WINDOW)
            return 0

        lax.fori_loop(1, nb, loop, 0)

    q, k, v, head = _attn_a_specs(s)
    return pl.pallas_call(
        body, name=name, grid=(N_PAIRS,),
        in_specs=[q, k, v, head, head],
        out_specs=[_pair_spec(s), _stat_spec(s)],
        out_shape=[jax.ShapeDtypeStruct((s, BRANCH), F32), jax.ShapeDtypeStruct((s, N_HEADS * LANES), F32)],
        compiler_params=_params(("parallel",)),
    )(qkv, qkv, qkv, slopes, sinks)


def _attn_a_bwd(qkv, slopes, sinks, o, lse, do, name):
    s = qkv.shape[0]
    nb = s // WINDOW

    def body(q_ref, k_ref, v_ref, sl_ref, sk_ref, o_ref, lse_ref, do_ref, dq_ref, dk_ref, dv_ref, dsk_ref):
        p_id = pl.program_id(0)
        kv_first, lane_first, kv_lanes, bias, valid, sink = _attn_a_geometry(p_id, sl_ref, sk_ref)

        @pl.when(p_id % 8 == 0)
        def _():
            dk_ref[...] = jnp.zeros_like(dk_ref)
            dv_ref[...] = jnp.zeros_like(dv_ref)

        def align(v2):
            v2r = _swap_halves(v2)
            both = jnp.concatenate([jnp.where(kv_first, v2, v2r), jnp.where(kv_first, v2r, v2)], axis=0)
            return jnp.where(kv_lanes, both, 0.0).astype(BF16)

        def block(r0, k0, width, sink_sum):
            xq = align(q_ref[pl.ds(r0, WINDOW), :].astype(F32) * Q_SCALE)
            do2 = do_ref[pl.ds(r0, WINDOW), :].astype(F32)
            xdo = align(do2)
            delta = jnp.concatenate(_rowsum_heads(do2 * o_ref[pl.ds(r0, WINDOW), :], lane_first), axis=0)
            lse = jnp.concatenate([lse_ref[pl.ds(r0, WINDOW), 0:1], lse_ref[pl.ds(r0, WINDOW), LANES:LANES + 1]], axis=0)
            km = jnp.where(kv_lanes, k_ref[pl.ds(k0, width), :], 0).astype(BF16)
            vm = jnp.where(kv_lanes, v_ref[pl.ds(k0, width), :], 0).astype(BF16)
            sc = lax.dot_general(xq, km, _NT, preferred_element_type=F32) - bias[:, 2 * WINDOW - width:]
            pr = jnp.where(valid[:, 2 * WINDOW - width:], jnp.exp(sc - lse), 0.0)
            ds = pr * (lax.dot_general(xdo, vm, _NT, preferred_element_type=F32) - delta)
            dsb = ds.astype(BF16)
            dq_al = jnp.dot(dsb, km, preferred_element_type=F32)
            dk_ref[pl.ds(k0, width), :] += lax.dot_general(dsb, xq, _TN, preferred_element_type=F32)
            dv_ref[pl.ds(k0, width), :] += lax.dot_general(pr.astype(BF16), xdo, _TN, preferred_element_type=F32)
            dqa = jnp.where(kv_first, dq_al[:WINDOW], _swap_halves(dq_al[:WINDOW]))
            dqb = jnp.where(kv_first, _swap_halves(dq_al[WINDOW:]), dq_al[WINDOW:])
            dq_ref[pl.ds(r0, WINDOW), :] = (jnp.where(lane_first, dqa, dqb) * Q_SCALE).astype(BF16)
            return sink_sum + jnp.exp(sink - lse) * delta

        sink_sum = block(0, 0, WINDOW, jnp.zeros((2 * WINDOW, 1), F32))

        def loop(n, c):
            r0 = pl.multiple_of(n * WINDOW, WINDOW)
            return block(r0, pl.multiple_of(r0 - WINDOW, WINDOW), 2 * WINDOW, c)

        sink_sum = lax.fori_loop(1, nb, loop, sink_sum)
        dsk_ref[0, :, 0:LANES] = jnp.broadcast_to(-jnp.sum(sink_sum[:WINDOW], axis=0, keepdims=True), (1, LANES))
        dsk_ref[0, :, LANES:2 * LANES] = jnp.broadcast_to(-jnp.sum(sink_sum[WINDOW:], axis=0, keepdims=True), (1, LANES))

    q, k, v, head = _attn_a_specs(s)
    kv_out = pl.BlockSpec((s, LANES), lambda p: (0, p // 8))
    return pl.pallas_call(
        body, name=name, grid=(N_PAIRS,),
        in_specs=[q, k, v, head, head, _pair_spec(s), _stat_spec(s), _pair_spec(s)],
        out_specs=[_pair_spec(s), kv_out, kv_out, head],
        out_shape=[jax.ShapeDtypeStruct((s, BRANCH), BF16), jax.ShapeDtypeStruct((s, KV_A), F32),
                   jax.ShapeDtypeStruct((s, KV_A), F32), jax.ShapeDtypeStruct((N_PAIRS, 1, 2 * LANES), F32)],
        compiler_params=_params(("arbitrary",)),
    )(qkv, qkv, qkv, slopes, sinks, o, lse, do)


def _layer_kind(i):
    return i % 3, i // 3


def _forward_backward(x, target, g_pre, g_post, sinks_a, b_f_c, w_in, w_out, wf_t):
    s = x.shape[0]
    slopes = _per_head_lanes(jnp.asarray(_alibi_slopes()))
    saved = []
    for i in range(DEPTH):
        kind, j = _layer_kind(i)
        tag = f"l{i}"
        w = w_in[kind][j]
        nqkv = A_QKV if kind == 0 else B_QKV
        tn = 512 if kind == 0 else 1024
        h = _rmsnorm_fwd(x, g_pre[i:i + 1], f"prenorm_{tag}")
        qkv = _matmul(h, w, mode="nn", out_dtype=BF16, name=f"inproj_qkv_{tag}", n=nqkv, tn=tn)
        z = _matmul(h, w, mode="nn", out_dtype=F32, name=f"inproj_gate_{tag}", n=BRANCH, b_off=nqkv // tn, tn=tn)
        extra = None
        if kind == 0:
            sink_l = _per_head_lanes(sinks_a[j])
            o, lse = _attn_a_fwd(qkv, slopes, sink_l, f"attn_a_fwd_{tag}")
            extra = (sink_l, lse)
        elif kind == 1:
            o, extra = _attn_b_fwd(qkv, f"attn_b_fwd_{tag}")
        else:
            b_col = jnp.broadcast_to(b_f_c[j].astype(F32)[:, None], (N_HEADS, LANES))
            xf, cum = _fgate_fwd(h, wf_t[j], b_col, f"fgate_fwd_{tag}")
            cum4 = _to_cum4(cum, _fox_tile(s))
            o, lse = _attn_c_fwd(qkv, cum4, f"attn_c_fwd_{tag}")
            extra = (xf, cum4, lse)
        u = _gate_fwd(o, z, f"gate_{tag}")
        y = _matmul(u, w_out[kind][j], mode="nn", out_dtype=F32, name=f"outproj_{tag}")
        saved.append((x, h, qkv, z, o, u, y, extra))
        x = _post_fwd(x, y, g_post[i:i + 1], f"postnorm_{tag}")

    dx, loss_part = _loss_and_grad(x, target)

    d_g_pre, d_g_post = [None] * DEPTH, [None] * DEPTH
    d_w_in = {0: [None, None], 1: [None], 2: [None]}
    d_w_out = {0: [None, None], 1: [None], 2: [None]}
    d_sinks = [None, None]
    d_b_f = None
    for i in reversed(range(DEPTH)):
        kind, j = _layer_kind(i)
        tag = f"l{i}"
        x_in, h, qkv, z, o, u, y, extra = saved[i]
        tn = 512 if kind == 0 else 1024
        dy, d_g_post[i] = _post_bwd(dx, y, g_post[i:i + 1], f"postnorm_bwd_{tag}")
        d_w_out[kind][j] = _matmul(u, dy, mode="tn", out_dtype=F32, name=f"dw_out_{tag}", tk=512)
        du = _matmul(dy, w_out[kind][j], mode="nt", out_dtype=F32, name=f"d_gated_{tag}")
        do, dz = _gate_bwd(du, o, z, f"gate_bwd_{tag}")
        dhs = []
        if kind == 0:
            sink_l, lse = extra
            dq, dk, dv, dsk = _attn_a_bwd(qkv, slopes, sink_l, o, lse, do, f"attn_a_bwd_{tag}")
            d_sinks[j] = dsk[:, 0, ::LANES].reshape(N_HEADS)
            parts = [dq, dk.astype(BF16), dv.astype(BF16), dz]
        elif kind == 1:
            dq, dk, dv = _attn_b_bwd(qkv, extra, do, f"attn_b_bwd_{tag}")
            parts = [dq, dk, dv, dz]
        else:
            xf, cum4, lse = extra
            dq, dk, dv, dcum4 = _attn_c_bwd(qkv, cum4, o, lse, do, f"attn_c_bwd_{tag}")
            d_wf_t, dh_f, db = _fgate_bwd(_from_cum4(dcum4), xf, h, wf_t[j], f"fgate_bwd_{tag}")
            d_b_f = db[:, 0]
            dhs.append(dh_f)
            parts = [dq, dk, dv, dz]
        dproj = jnp.concatenate(parts, axis=1)
        dw = _matmul(h, dproj, mode="tn", out_dtype=F32, name=f"dw_in_{tag}", tk=512, tn=tn)
        if kind == 2:
            dw = jnp.concatenate([dw, d_wf_t.T], axis=1)
        d_w_in[kind][j] = dw
        dhs.insert(0, _matmul(dproj, w_in[kind][j], mode="nt", out_dtype=F32, name=f"dh_{tag}", tk=512))
        dx, d_g_pre[i] = _pre_bwd(dx, dhs, x_in, g_pre[i:i + 1], f"prenorm_bwd_{tag}")

    return dict(loss=loss_part, dx=dx, g_pre=jnp.concatenate(d_g_pre, axis=0), g_post=jnp.concatenate(d_g_post, axis=0),
                sinks_a=jnp.stack(d_sinks), b_f_c=d_b_f[None, :], w_in=d_w_in, w_out=d_w_out)


def _place():
    x, y, c = lax.axis_index("x"), lax.axis_index("y"), lax.axis_index("c")
    others = [(1 - x, y), (x, 1 - y), (1 - x, 1 - y)]
    return x, y, c, others


def _half_rows(ref_rows, which):
    half = ref_rows // 2
    return pl.ds(pl.multiple_of(which * half, half), half)


def _remote(src, dst, sems, k, device):
    send, recv = sems
    return pltpu.make_async_remote_copy(src_ref=src, dst_ref=dst, send_sem=send.at[k], recv_sem=recv.at[k],
                                        device_id=device, device_id_type=MESH)


def _hbm_call(body, name, ins, out_shapes, n_remote, aliases=None):
    any_spec = pl.BlockSpec(memory_space=pl.ANY)
    return pl.pallas_call(
        body, name=name, in_specs=[any_spec] * len(ins), out_specs=[any_spec] * len(out_shapes),
        out_shape=out_shapes, input_output_aliases=aliases or {},
        scratch_shapes=[pltpu.SemaphoreType.DMA((n_remote,)), pltpu.SemaphoreType.DMA((n_remote,))],
    )(*ins)


def _all_gather_shards(shards):
    n = len(shards)

    def body(*refs):
        ins, outs, sems = refs[:n], refs[n:2 * n], refs[2 * n:2 * n + 2]
        x, y, c, others = _place()
        me = 2 * x + y
        sends = []
        for w in range(n):
            rows = ins[w].shape[1]
            mine = _half_rows(rows, c)
            for j, (px, py) in enumerate(others):
                cp = _remote(ins[w].at[:, mine], outs[w].at[:, me, mine], sems, 6 * w + j, (px, py, c))
                cp.start()
                sends.append(cp)
        for w in range(n):
            mine = _half_rows(ins[w].shape[1], c)
            for j, (px, py) in enumerate(others):
                landed = outs[w].at[:, 2 * px + py, mine]
                _remote(landed, landed, sems, 6 * w + j, (px, py, c)).wait_recv()
                fw = _remote(landed, landed, sems, 6 * w + 3 + j, (x, y, 1 - c))
                fw.start()
                sends.append(fw)
        for w in range(n):
            theirs = _half_rows(ins[w].shape[1], 1 - c)
            for j, (px, py) in enumerate(others):
                landed = outs[w].at[:, 2 * px + py, theirs]
                _remote(landed, landed, sems, 6 * w + 3 + j, (x, y, 1 - c)).wait_recv()
        for cp in sends:
            cp.wait_send()

    out_shapes = [jax.ShapeDtypeStruct((a.shape[0], 4) + a.shape[1:], a.dtype) for a in shards]
    return _hbm_call(body, "all_gather_weights", shards, out_shapes, 6 * n)


def _sibling_send(parts):
    n = len(parts)

    def body(*refs):
        ins, outs, sems = refs[:n], refs[n:2 * n], refs[2 * n:2 * n + 2]
        x, y, c, _ = _place()
        pend = []
        for w in range(n):
            cp = _remote(ins[w].at[:, :, _half_rows(ins[w].shape[2], 1 - c)], outs[w], sems, w, (x, y, 1 - c))
            cp.start()
            pend.append(cp)
        for cp in pend:
            cp.wait_recv()
            cp.wait_send()

    out_shapes = [jax.ShapeDtypeStruct((a.shape[0], 4, a.shape[2] // 2, a.shape[3]), a.dtype) for a in parts]
    return _hbm_call(body, "grad_sibling_exchange", parts, out_shapes, n)


def _chip_scatter(sums):
    n = len(sums)

    def body(*refs):
        ins, outs, sems = refs[:n], refs[n:2 * n], refs[2 * n:2 * n + 2]
        x, y, c, others = _place()
        me = 2 * x + y
        sends = []
        for w in range(n):
            for j, (px, py) in enumerate(others):
                cp = _remote(ins[w].at[:, 2 * px + py], outs[w].at[:, me], sems, 3 * w + j, (px, py, c))
                cp.start()
                sends.append(cp)
        for w in range(n):
            for j, (px, py) in enumerate(others):
                landed = outs[w].at[:, 2 * px + py]
                _remote(landed, landed, sems, 3 * w + j, (px, py, c)).wait_recv()
        for cp in sends:
            cp.wait_send()

    out_shapes = [jax.ShapeDtypeStruct(a.shape, a.dtype) for a in sums]
    return _hbm_call(body, "grad_chip_scatter", sums, out_shapes, 3 * n)


def _sibling_join(shards):
    n = len(shards)

    def body(*refs):
        ins, outs, sems = refs[:n], refs[n:2 * n], refs[2 * n:2 * n + 2]
        x, y, c, _ = _place()
        pend = []
        for w in range(n):
            rows = ins[w].shape[1]
            mine, theirs = _half_rows(rows, c), _half_rows(rows, 1 - c)
            cp = _remote(ins[w].at[:, mine], outs[w].at[:, mine], sems, w, (x, y, 1 - c))
            cp.start()
            pend.append((cp, _remote(ins[w].at[:, theirs], outs[w].at[:, theirs], sems, w, (x, y, 1 - c))))
        for cp, landed in pend:
            landed.wait_recv()
            cp.wait_send()

    out_shapes = [jax.ShapeDtypeStruct(a.shape, a.dtype) for a in shards]
    return _hbm_call(body, "grad_sibling_join", shards, out_shapes, n, aliases={w: w for w in range(n)})


SMALL_ROWS = 136


def _all_reduce_small(vec):
    def body(v_ref, o_ref, buf, send, recv, loc):
        x, y, c, _ = _place()
        me = 4 * x + 2 * y + c
        lc = pltpu.make_async_copy(v_ref, buf.at[me], loc.at[0])
        lc.start()
        cps = []
        for k in range(1, 8):
            fx, fy, fc = (k >> 2) & 1, (k >> 1) & 1, k & 1
            peer = (x ^ fx, y ^ fy, c ^ fc)
            cp = pltpu.make_async_remote_copy(src_ref=v_ref, dst_ref=buf.at[me], send_sem=send.at[k - 1],
                                              recv_sem=recv.at[k - 1], device_id=peer, device_id_type=MESH)
            cp.start()
            cps.append((cp, 4 * peer[0] + 2 * peer[1] + peer[2]))
        for k, (cp, src) in enumerate(cps):
            pltpu.make_async_remote_copy(src_ref=v_ref, dst_ref=buf.at[src], send_sem=send.at[k], recv_sem=recv.at[k],
                                         device_id=(x, y, c), device_id_type=MESH).wait_recv()
        for cp, _ in cps:
            cp.wait_send()
        lc.wait()
        total = buf[0]
        for k in range(1, 8):
            total = total + buf[k]
        o_ref[...] = total

    vm = pl.BlockSpec(memory_space=pltpu.VMEM)
    return pl.pallas_call(
        body, name="all_reduce_small", in_specs=[vm], out_specs=vm,
        out_shape=jax.ShapeDtypeStruct(vec.shape, F32),
        scratch_shapes=[pltpu.VMEM((8,) + vec.shape, F32), pltpu.SemaphoreType.DMA((7,)),
                        pltpu.SemaphoreType.DMA((7,)), pltpu.SemaphoreType.DMA((1,))],
    )(vec)


SUM_ROWS = 256


def _add_pairs(part, theirs, name):
    l, four, rh, cc = theirs.shape
    tr = min(SUM_ROWS, rh)
    halves = part.reshape(l, four, 2, rh, cc)

    def body(a_ref, b_ref, o_ref):
        mine = a_ref[0, 0, lax.axis_index("c")]
        o_ref[0, 0] = (mine.astype(F32) + b_ref[0, 0].astype(F32)).astype(o_ref.dtype)

    spec = pl.BlockSpec((1, 1, tr, cc), lambda i, k, r: (i, k, r, 0))
    return pl.pallas_call(
        body, name=name, grid=(l, four, rh // tr),
        in_specs=[pl.BlockSpec((1, 1, 2, tr, cc), lambda i, k, r: (i, k, 0, r, 0)), spec], out_specs=spec,
        out_shape=jax.ShapeDtypeStruct(theirs.shape, theirs.dtype),
        compiler_params=_params(("parallel", "parallel", "parallel")),
    )(halves, theirs)


def _sum_chips(own, arrived, core, name):
    l, four, rh, cc = own.shape
    tr = min(SUM_ROWS, rh)
    nr = rh // tr

    def body(c_ref, own_ref, arr_ref, o_ref):
        x, y = lax.axis_index("x"), lax.axis_index("y")
        tot = own_ref[0, 2 * x + y].astype(F32)
        for px, py in ((1 - x, y), (x, 1 - y), (1 - x, 1 - y)):
            tot = tot + arr_ref[0, 2 * px + py].astype(F32)
        o_ref[0] = tot

    blk = pl.BlockSpec((1, 4, tr, cc), lambda i, r, c_ref: (i, 0, r, 0))
    return pl.pallas_call(
        body, name=name,
        grid_spec=pltpu.PrefetchScalarGridSpec(
            num_scalar_prefetch=1, grid=(l, nr), in_specs=[blk, blk],
            out_specs=pl.BlockSpec((1, tr, cc), lambda i, r, c_ref: (i, c_ref[0] * nr + r, 0))),
        out_shape=jax.ShapeDtypeStruct((l, 2 * rh, cc), F32),
        compiler_params=_params(("parallel", "parallel")),
    )(core, own, arrived)


ADAM_ROWS = 256


def _adamw(w, g, m, v, name):
    shape = w.shape
    cc = shape[-1]
    flat = lambda a: a.reshape(-1, cc)
    rows = flat(w).shape[0]
    tr = min(ADAM_ROWS, rows)
    assert rows % tr == 0
    c1 = 1.0 - ADAM_B1 ** ADAM_STEP
    c2 = 1.0 - ADAM_B2 ** ADAM_STEP

    def body(w_ref, g_ref, m_ref, v_ref, d_ref, nm_ref, nv_ref):
        gv = g_ref[...]
        nm = ADAM_B1 * m_ref[...] + (1.0 - ADAM_B1) * gv
        nv = ADAM_B2 * v_ref[...] + (1.0 - ADAM_B2) * (gv * gv)
        nm_ref[...] = nm
        nv_ref[...] = nv
        d_ref[...] = -ADAM_LR * ((nm / c1) / (jnp.sqrt(nv / c2) + ADAM_EPS) + ADAM_WD * w_ref[...])

    spec = pl.BlockSpec((tr, cc), lambda i: (i, 0))
    sh = jax.ShapeDtypeStruct((rows, cc), F32)
    outs = pl.pallas_call(
        body, name=name, grid=(rows // tr,), in_specs=[spec] * 4, out_specs=[spec] * 3, out_shape=[sh] * 3,
        compiler_params=_params(("parallel",)),
    )(flat(w), flat(g), flat(m), flat(v))
    return [o.reshape(shape) for o in outs]


def _pack_small(g_pre, g_post, sinks_a, b_f_c, loss_row):
    pad = lambda a: jnp.pad(a.reshape(1, -1).astype(F32), ((0, 0), (0, LANES - a.size)))
    rows = [g_pre.astype(F32).reshape(-1, LANES), g_post.astype(F32).reshape(-1, LANES), pad(sinks_a), pad(b_f_c), loss_row]
    packed = jnp.concatenate(rows, axis=0)
    return jnp.pad(packed, ((0, SMALL_ROWS - packed.shape[0]), (0, 0)))


def _unpack_small(p):
    n = DEPTH * D_MODEL // LANES
    return (p[:n].reshape(DEPTH, D_MODEL), p[n:2 * n].reshape(DEPTH, D_MODEL), p[2 * n, :2 * N_HEADS].reshape(2, N_HEADS),
            p[2 * n + 1, :N_HEADS].reshape(1, N_HEADS), p[2 * n + 2, 0])


def kernel(x, g_pre, g_post, w_in_a, w_out_a, sinks_a, w_in_b, w_out_b, w_in_c, b_f_c, w_out_c, loss_target, m_g_pre, m_g_post, m_w_in_a, m_w_out_a, m_sinks_a, m_w_in_b, m_w_out_b, m_w_in_c, m_b_f_c, m_w_out_c, v_g_pre, v_g_post, v_w_in_a, v_w_out_a, v_sinks_a, v_w_in_b, v_w_out_b, v_w_in_c, v_b_f_c, v_w_out_c):
    big_w = [w_in_a, w_out_a, w_in_b, w_out_b, w_in_c, w_out_c]
    big_m = [m_w_in_a, m_w_out_a, m_w_in_b, m_w_out_b, m_w_in_c, m_w_out_c]
    big_v = [v_w_in_a, v_w_out_a, v_w_in_b, v_w_out_b, v_w_in_c, v_w_out_c]

    chip = 2 * lax.axis_index("x") + lax.axis_index("y")
    core = lax.axis_index("c").astype(jnp.int32).reshape(1)
    shards = [w.astype(BF16) for w in big_w]
    gathered = [lax.dynamic_update_slice(g, sh[:, None], (0, chip, 0, 0))
                for g, sh in zip(_all_gather_shards(shards), shards)]
    cols = lambda g, j: g[j].transpose(1, 0, 2).reshape(g.shape[2], 4 * g.shape[3])
    rows = lambda g, j: g[j].reshape(4 * g.shape[2], g.shape[3])
    w_c = cols(gathered[4], 0)
    w_in = {0: [cols(gathered[0], j) for j in range(2)], 1: [cols(gathered[2], 0)], 2: [w_c[:, :4 * BRANCH]]}
    w_out = {0: [rows(gathered[1], j) for j in range(2)], 1: [rows(gathered[3], 0)], 2: [rows(gathered[5], 0)]}
    wf_t = [w_c[:, 4 * BRANCH:].T]

    res = _forward_backward(x[0], loss_target[0], g_pre, g_post, sinks_a, b_f_c, w_in, w_out, wf_t)

    col_parts = lambda gs: jnp.stack([g.reshape(g.shape[0], 4, g.shape[1] // 4).transpose(1, 0, 2) for g in gs]).astype(BF16)
    row_parts = lambda gs: jnp.stack([g.reshape(4, g.shape[0] // 4, g.shape[1]) for g in gs]).astype(BF16)
    parts = [col_parts(res["w_in"][0]), row_parts(res["w_out"][0]), col_parts(res["w_in"][1]), row_parts(res["w_out"][1]),
             col_parts(res["w_in"][2]), row_parts(res["w_out"][2])]
    names = ["w_in_a", "w_out_a", "w_in_b", "w_out_b", "w_in_c", "w_out_c"]
    theirs = _sibling_send(parts)
    chip_sums = [_add_pairs(a, b, f"chip_sum_{nm}") for a, b, nm in zip(parts, theirs, names)]
    arrived = _chip_scatter(chip_sums)
    halves = [_sum_chips(own, arr, core, f"shard_sum_{nm}") for own, arr, nm in zip(chip_sums, arrived, names)]
    grads = _sibling_join(halves)

    small = _unpack_small(_all_reduce_small(
        _pack_small(res["g_pre"], res["g_post"], res["sinks_a"], res["b_f_c"], res["loss"])))
    g_small, loss = small[:4], small[4]

    zero_row = jnp.zeros((1, LANES), F32)
    pk = lambda a: _pack_small(a[0], a[1], a[2], a[3], zero_row)
    sm = _adamw(pk([g_pre, g_post, sinks_a, b_f_c]), pk(g_small), pk([m_g_pre, m_g_post, m_sinks_a, m_b_f_c]),
                pk([v_g_pre, v_g_post, v_sinks_a, v_b_f_c]), "adamw_small")
    sm = [_unpack_small(a)[:4] for a in sm]
    bigs = [_adamw(w, g, m, v, f"adamw_{nm}") for w, g, m, v, nm in zip(big_w, grads, big_m, big_v, names)]

    def ordered(small4, big6):
        return [small4[0], small4[1], big6[0], big6[1], small4[2], big6[2], big6[3], big6[4], small4[3], big6[5]]

    out = [loss, res["dx"][None], *ordered(g_small, grads)]
    for k in range(3):
        out += ordered(sm[k], [b[k] for b in bigs])
    return tuple(out)
```

```python
import functools
import math

import numpy as np
import jax
import jax.numpy as jnp
from jax import lax
from jax.experimental import pallas as pl
from jax.experimental.pallas import tpu as pltpu

F32 = jnp.float32
BF16 = jnp.bfloat16

D_MODEL = 2048
DEPTH = 4
N_HEADS = 32
HEAD_DIM = 64
LANES = 128
N_PAIRS = N_HEADS * HEAD_DIM // LANES
BRANCH = N_HEADS * HEAD_DIM
N_KV_A = 4
KV_A = N_KV_A * HEAD_DIM
WINDOW = 128
NORM_EPS = 1e-6
NEG = -1e30
Q_SCALE = HEAD_DIM ** -0.5

A_QKV = BRANCH + 2 * KV_A
B_QKV = 3 * BRANCH

ADAM_LR = 0.001
ADAM_B1 = 0.9
ADAM_B2 = 0.999
ADAM_EPS = 1e-08
ADAM_WD = 0.01
ADAM_STEP = 10

MESH = pl.DeviceIdType.MESH

_NT = (((1,), (1,)), ((), ()))
_TN = (((0,), (0,)), ((), ()))


def _params(sem=None):
    return pltpu.CompilerParams(dimension_semantics=sem)


def _matmul(a, b, *, mode, out_dtype, name, n=None, b_off=0, tm=1024, tn=1024, tk=2048):
    if mode == "nn":
        (m, k), nn = a.shape, (n or b.shape[1])
    elif mode == "nt":
        (m, k), nn = a.shape, b.shape[0]
    else:
        (k, m), nn = a.shape, b.shape[1]
    tm, tn, tk = min(tm, m), min(tn, nn), min(tk, k)
    assert m % tm == 0 and nn % tn == 0 and k % tk == 0, (name, m, nn, k, tm, tn, tk)
    nk = k // tk

    def body(a_ref, b_ref, o_ref, acc_ref):
        kk = pl.program_id(2)
        if mode == "nn":
            p = jnp.dot(a_ref[...], b_ref[...], preferred_element_type=F32)
        elif mode == "nt":
            p = lax.dot_general(a_ref[...], b_ref[...], _NT, preferred_element_type=F32)
        else:
            p = lax.dot_general(a_ref[...], b_ref[...], _TN, preferred_element_type=F32)
        if nk == 1:
            o_ref[...] = p.astype(o_ref.dtype)
        else:
            @pl.when(kk == 0)
            def _():
                acc_ref[...] = p

            @pl.when(kk > 0)
            def _():
                acc_ref[...] += p

            @pl.when(kk == nk - 1)
            def _():
                o_ref[...] = acc_ref[...].astype(o_ref.dtype)

    if mode == "nn":
        in_specs = [pl.BlockSpec((tm, tk), lambda i, j, kk: (i, kk)),
                    pl.BlockSpec((tk, tn), lambda i, j, kk: (kk, j + b_off))]
    elif mode == "nt":
        in_specs = [pl.BlockSpec((tm, tk), lambda i, j, kk: (i, kk)),
                    pl.BlockSpec((tn, tk), lambda i, j, kk: (j, kk))]
    else:
        in_specs = [pl.BlockSpec((tk, tm), lambda i, j, kk: (kk, i)),
                    pl.BlockSpec((tk, tn), lambda i, j, kk: (kk, j))]
    return pl.pallas_call(
        body, name=name, grid=(m // tm, nn // tn, nk),
        in_specs=in_specs,
        out_specs=pl.BlockSpec((tm, tn), lambda i, j, kk: (i, j)),
        out_shape=jax.ShapeDtypeStruct((m, nn), out_dtype),
        scratch_shapes=[pltpu.VMEM((tm, tn), F32)],
        compiler_params=_params(("parallel", "parallel", "arbitrary")),
    )(a, b)


ROW_TILE = 256


def _row_call(body, name, ins, outs, *, s, acc_outs=()):
    tr = min(ROW_TILE, s)
    row = lambda w: pl.BlockSpec((tr, w), lambda i: (i, 0))
    vec = lambda w: pl.BlockSpec((1, w), lambda i: (0, 0))
    in_specs = [row(a.shape[1]) if kind == "row" else vec(a.shape[1]) for a, kind in ins]
    out_specs = [row(sh.shape[1]) if kind == "row" else vec(sh.shape[1]) for sh, kind in outs]
    return pl.pallas_call(
        body, name=name, grid=(s // tr,), in_specs=in_specs, out_specs=out_specs,
        out_shape=[sh for sh, _ in outs],
        compiler_params=_params(("arbitrary",)),
    )(*[a for a, _ in ins])


def _rsqrt_ms(v):
    return lax.rsqrt(jnp.mean(v * v, axis=-1, keepdims=True) + NORM_EPS)


def _rmsnorm_fwd(x, g, name):
    s, d = x.shape

    def body(x_ref, g_ref, h_ref):
        xv = x_ref[...]
        h_ref[...] = (xv * _rsqrt_ms(xv) * g_ref[...]).astype(BF16)

    return _row_call(body, name, [(x, "row"), (g, "vec")],
                     [(jax.ShapeDtypeStruct((s, d), BF16), "row")], s=s)[0]


def _gate_fwd(o, z, name):
    s, d = o.shape

    def body(o_ref, z_ref, u_ref):
        zv = z_ref[...]
        u_ref[...] = (o_ref[...] * (zv * jax.nn.sigmoid(zv))).astype(BF16)

    return _row_call(body, name, [(o, "row"), (z, "row")],
                     [(jax.ShapeDtypeStruct((s, d), BF16), "row")], s=s)[0]


def _post_fwd(x, y, g, name):
    s, d = x.shape

    def body(x_ref, y_ref, g_ref, o_ref):
        yv = y_ref[...]
        o_ref[...] = x_ref[...] + yv * _rsqrt_ms(yv) * g_ref[...]

    return _row_call(body, name, [(x, "row"), (y, "row"), (g, "vec")],
                     [(jax.ShapeDtypeStruct((s, d), F32), "row")], s=s)[0]


def _loss_and_grad(x, target):
    s, d = x.shape

    def body(x_ref, t_ref, dx_ref, l_ref):
        err = x_ref[...] - t_ref[...]
        dx_ref[...] = err * (1.0 / d)
        part = jnp.sum(jnp.sum(err * err, axis=1, keepdims=True), axis=0, keepdims=True) * (0.5 / d)

        @pl.when(pl.program_id(0) == 0)
        def _():
            l_ref[...] = jnp.zeros_like(l_ref)

        l_ref[...] += jnp.broadcast_to(part, l_ref.shape)

    return _row_call(body, "loss_head", [(x, "row"), (target, "row")],
                     [(jax.ShapeDtypeStruct((s, d), F32), "row"),
                      (jax.ShapeDtypeStruct((1, LANES), F32), "vec")], s=s)


def _norm_bwd_rows(dn, v, g):
    r = _rsqrt_ms(v)
    a = dn * g
    dv = r * (a - v * (r * r) * jnp.mean(a * v, axis=-1, keepdims=True))
    return dv, dn * v * r


def _post_bwd(dx, y, g, name):
    s, d = dx.shape

    def body(dx_ref, y_ref, g_ref, dy_ref, dg_ref):
        dy, dg = _norm_bwd_rows(dx_ref[...], y_ref[...], g_ref[...])
        dy_ref[...] = dy.astype(BF16)

        @pl.when(pl.program_id(0) == 0)
        def _():
            dg_ref[...] = jnp.zeros_like(dg_ref)

        dg_ref[...] += jnp.sum(dg, axis=0, keepdims=True)

    return _row_call(body, name, [(dx, "row"), (y, "row"), (g, "vec")],
                     [(jax.ShapeDtypeStruct((s, d), BF16), "row"),
                      (jax.ShapeDtypeStruct((1, d), F32), "vec")], s=s)


def _gate_bwd(du, o, z, name):
    s, d = du.shape

    def body(du_ref, o_ref, z_ref, do_ref, dz_ref):
        duv, zv = du_ref[...], z_ref[...]
        sig = jax.nn.sigmoid(zv)
        do_ref[...] = (duv * (zv * sig)).astype(BF16)
        dz_ref[...] = (duv * o_ref[...] * (sig * (1.0 + zv * (1.0 - sig)))).astype(BF16)

    return _row_call(body, name, [(du, "row"), (o, "row"), (z, "row")],
                     [(jax.ShapeDtypeStruct((s, d), BF16), "row"),
                      (jax.ShapeDtypeStruct((s, d), BF16), "row")], s=s)


def _pre_bwd(dx, dhs, x, g, name):
    s, d = dx.shape
    n_dh = len(dhs)

    def body(*refs):
        dx_ref, dh_refs, (x_ref, g_ref, o_ref, dg_ref) = refs[0], refs[1:1 + n_dh], refs[1 + n_dh:]
        dh = dh_refs[0][...].astype(F32)
        for r in dh_refs[1:]:
            dh = dh + r[...].astype(F32)
        dv, dg = _norm_bwd_rows(dh, x_ref[...], g_ref[...])
        o_ref[...] = dx_ref[...] + dv

        @pl.when(pl.program_id(0) == 0)
        def _():
            dg_ref[...] = jnp.zeros_like(dg_ref)

        dg_ref[...] += jnp.sum(dg, axis=0, keepdims=True)

    return _row_call(body, name, [(dx, "row")] + [(h, "row") for h in dhs] + [(x, "row"), (g, "vec")],
                     [(jax.ShapeDtypeStruct((s, d), F32), "row"),
                      (jax.ShapeDtypeStruct((1, d), F32), "vec")], s=s)


def _lane_is_first_head():
    return lax.broadcasted_iota(jnp.int32, (1, LANES), 1) < HEAD_DIM


def _bcast_lanes(col):
    return jnp.broadcast_to(col, (col.shape[0], LANES))


def _pair_spec(s, off=0, width=LANES):
    return pl.BlockSpec((s, width), lambda p: (0, p + off))


def _stack_heads(pair, first):
    return jnp.concatenate([jnp.where(first, pair, 0), jnp.where(first, 0, pair)], axis=0).astype(BF16)


def _stacked_mask(t, strict):
    row = lax.broadcasted_iota(jnp.int32, (2 * t, t), 0)
    col = lax.broadcasted_iota(jnp.int32, (2 * t, t), 1)
    query = jnp.where(row >= t, row - t, row)
    return col < query if strict else col <= query


def _rowsum_heads(prod, first):
    return (jnp.sum(jnp.where(first, prod, 0.0), axis=1, keepdims=True),
            jnp.sum(jnp.where(first, 0.0, prod), axis=1, keepdims=True))


def _softplus_parts(z):
    e = jnp.exp(-jnp.abs(z))
    sp = jnp.maximum(z, 0.0) + jnp.log(1.0 + e)
    r = 1.0 / (1.0 + e)
    return sp, jnp.where(z >= 0, r, e * r)


def _split_dot(x, t):
    hi = x.astype(BF16)
    lo = (x - hi.astype(F32)).astype(BF16)
    return jnp.dot(hi, t, preferred_element_type=F32) + jnp.dot(lo, t, preferred_element_type=F32)


def _sb_tile(s):
    return min(256, s)


def _attn_b_fwd(qkv, name, exchange=None):
    s = qkv.shape[0]
    t = _sb_tile(s)
    nq = s // t

    def body(q_ref, k_ref, v_ref, o_ref, lt_ref):
        first = _lane_is_first_head()
        before = _stacked_mask(t, strict=True)
        tri = (lax.broadcasted_iota(jnp.int32, (t, t), 0) >= lax.broadcasted_iota(jnp.int32, (t, t), 1)).astype(BF16)

        def tile(j, carry, diag, qs):
            c, acc = carry
            c0 = pl.multiple_of(j * t, t)
            k2 = k_ref[pl.ds(c0, t), :]
            v2 = v_ref[pl.ds(c0, t), :]
            z = lax.dot_general(qs, k2, _NT, preferred_element_type=F32)
            sp, _ = _softplus_parts(z)
            lf = jnp.where(before, -sp, 0.0) if diag else -sp
            incl = jnp.dot(lf.astype(BF16), tri, preferred_element_type=F32)
            a = jnp.exp(z + c + incl)
            if diag:
                a = jnp.where(before, a, 0.0)
            pv = jnp.dot(a.astype(BF16), v2, preferred_element_type=F32)
            return c + incl[:, 0:1], acc + jnp.where(first, pv[:t], pv[t:])

        def qblock(i, _):
            r0 = pl.multiple_of(i * t, t)
            qs = _stack_heads(q_ref[pl.ds(r0, t), :] * Q_SCALE, first)
            carry = tile(i, (jnp.zeros((2 * t, 1), F32), jnp.zeros((t, LANES), F32)), True, qs)
            carry = lax.fori_loop(0, i, lambda jj, c: tile(i - 1 - jj, c, False, qs), carry)
            o_ref[pl.ds(r0, t), :] = carry[1]
            lt_ref[pl.ds(r0, t), 0:LANES] = _bcast_lanes(carry[0][:t])
            lt_ref[pl.ds(r0, t), LANES:2 * LANES] = _bcast_lanes(carry[0][t:])
            return 0

        lax.fori_loop(0, nq, qblock, 0)

    return _grid_call(
        body, name=name, grid=(N_PAIRS,),
        in_specs=[_pair_spec(s), _pair_spec(s, N_PAIRS), _pair_spec(s, 2 * N_PAIRS)],
        out_specs=[_pair_spec(s), _stat_spec(s)],
        out_shape=[jax.ShapeDtypeStruct((s, BRANCH), F32), jax.ShapeDtypeStruct((s, N_HEADS * LANES), F32)],
        args=(qkv, qkv, qkv), semantics=("parallel",), exchange=exchange)


def _attn_b_bwd(qkv, ltot, do, name):
    s = qkv.shape[0]
    t = _sb_tile(s)
    nq = s // t

    def body(q_ref, k_ref, v_ref, lt_ref, do_ref, dq_ref, dk_ref, dv_ref, dk_acc, dv_acc):
        first = _lane_is_first_head()
        before = _stacked_mask(t, strict=True)
        tri = (lax.broadcasted_iota(jnp.int32, (t, t), 0) <= lax.broadcasted_iota(jnp.int32, (t, t), 1)).astype(BF16)
        dk_acc[...] = jnp.zeros_like(dk_acc)
        dv_acc[...] = jnp.zeros_like(dv_acc)

        def tile(j, carry, diag, qs, dos, lt):
            p_l, p_g, dq_acc = carry
            c0 = pl.multiple_of(j * t, t)
            k2 = k_ref[pl.ds(c0, t), :]
            v2 = v_ref[pl.ds(c0, t), :]
            z = lax.dot_general(qs, k2, _NT, preferred_element_type=F32)
            sp, sig = _softplus_parts(z)
            lf = jnp.where(before, -sp, 0.0) if diag else -sp
            pref_l = jnp.dot(lf.astype(BF16), tri, preferred_element_type=F32)
            a = jnp.exp(z + ((lt - p_l) - pref_l + lf))
            if diag:
                a = jnp.where(before, a, 0.0)
            g = a * lax.dot_general(dos, v2, _NT, preferred_element_type=F32)
            pref_g = jnp.dot(g.astype(BF16), tri, preferred_element_type=F32)
            dz = g - sig * (p_g + pref_g)
            if diag:
                dz = jnp.where(before, dz, 0.0)
            dzb = dz.astype(BF16)
            dq = jnp.dot(dzb, k2, preferred_element_type=F32)
            dk_acc[pl.ds(c0, t), :] += lax.dot_general(dzb, qs, _TN, preferred_element_type=F32)
            dv_acc[pl.ds(c0, t), :] += lax.dot_general(a.astype(BF16), dos, _TN, preferred_element_type=F32)
            return p_l + pref_l[:, t - 1:t], p_g + pref_g[:, t - 1:t], dq_acc + jnp.where(first, dq[:t], dq[t:])

        def qblock(i, _):
            r0 = pl.multiple_of(i * t, t)
            qs = _stack_heads(q_ref[pl.ds(r0, t), :] * Q_SCALE, first)
            dos = _stack_heads(do_ref[pl.ds(r0, t), :], first)
            lt = jnp.concatenate([lt_ref[pl.ds(r0, t), 0:1], lt_ref[pl.ds(r0, t), LANES:LANES + 1]], axis=0)
            zero = jnp.zeros((2 * t, 1), F32)
            carry = (zero, zero, jnp.zeros((t, LANES), F32))
            carry = lax.fori_loop(0, i, lambda j, c: tile(j, c, False, qs, dos, lt), carry)
            carry = tile(i, carry, True, qs, dos, lt)
            dq_ref[pl.ds(r0, t), :] = (carry[2] * Q_SCALE).astype(BF16)
            return 0

        lax.fori_loop(0, nq, qblock, 0)
        dk_ref[...] = dk_acc[...].astype(BF16)
        dv_ref[...] = dv_acc[...].astype(BF16)

    out = jax.ShapeDtypeStruct((s, BRANCH), BF16)
    return pl.pallas_call(
        body, name=name, grid=(N_PAIRS,),
        in_specs=[_pair_spec(s), _pair_spec(s, N_PAIRS), _pair_spec(s, 2 * N_PAIRS), _stat_spec(s), _pair_spec(s)],
        out_specs=[_pair_spec(s)] * 3, out_shape=[out] * 3,
        scratch_shapes=[pltpu.VMEM((s, LANES), F32), pltpu.VMEM((s, LANES), F32)],
        compiler_params=_params(("parallel",)),
    )(qkv, qkv, qkv, ltot, do)


def _fox_tile(s):
    return min(256, s)


def _stat_spec(s):
    return pl.BlockSpec((s, 2 * LANES), lambda p: (0, p))


def _cum_spec(nt, t):
    return pl.BlockSpec((1, nt, 2, t), lambda p: (p, 0, 0, 0))


def _attn_c_fwd(qkv, cum4, name, exchange=None):
    s = qkv.shape[0]
    t = _fox_tile(s)
    nq = s // t

    def body(q_ref, k_ref, v_ref, c_ref, o_ref, lse_ref):
        first = _lane_is_first_head()
        causal = _stacked_mask(t, strict=False)

        def tile(j, carry, diag, qs):
            c0 = pl.multiple_of(j * t, t)
            k2 = k_ref[pl.ds(c0, t), :]
            v2 = v_ref[pl.ds(c0, t), :]
            cs = c_ref[0, j]
            m_prev, l_prev, acc = carry
            z = lax.dot_general(qs, k2, _NT, preferred_element_type=F32)
            sc = jnp.concatenate([z[:t] - cs[0:1, :], z[t:] - cs[1:2, :]], axis=0)
            if diag:
                sc = jnp.where(causal, sc, NEG)
            m_new = jnp.maximum(m_prev, jnp.max(sc, axis=1, keepdims=True))
            alpha = jnp.exp(m_prev - m_new)
            p = jnp.exp(sc - m_new)
            l_new = alpha * l_prev + jnp.sum(p, axis=1, keepdims=True)
            pv = jnp.dot(p.astype(BF16), v2, preferred_element_type=F32)
            acc = jnp.where(first, acc * alpha[:t] + pv[:t], acc * alpha[t:] + pv[t:])
            return m_new, l_new, acc

        def qblock(i, _):
            r0 = pl.multiple_of(i * t, t)
            qs = _stack_heads(q_ref[pl.ds(r0, t), :] * Q_SCALE, first)
            carry = (jnp.full((2 * t, 1), NEG, F32), jnp.zeros((2 * t, 1), F32), jnp.zeros((t, LANES), F32))
            carry = lax.fori_loop(0, i, lambda j, c: tile(j, c, False, qs), carry)
            m, l, acc = tile(i, carry, True, qs)
            inv = 1.0 / l
            lse = m + jnp.log(l)
            o_ref[pl.ds(r0, t), :] = acc * jnp.where(first, inv[:t], inv[t:])
            lse_ref[pl.ds(r0, t), 0:LANES] = _bcast_lanes(lse[:t])
            lse_ref[pl.ds(r0, t), LANES:2 * LANES] = _bcast_lanes(lse[t:])
            return 0

        lax.fori_loop(0, nq, qblock, 0)

    return _grid_call(
        body, name=name, grid=(N_PAIRS,),
        in_specs=[_pair_spec(s), _pair_spec(s, N_PAIRS), _pair_spec(s, 2 * N_PAIRS), _cum_spec(nq, t)],
        out_specs=[_pair_spec(s), _stat_spec(s)],
        out_shape=[jax.ShapeDtypeStruct((s, BRANCH), F32), jax.ShapeDtypeStruct((s, N_HEADS * LANES), F32)],
        args=(qkv, qkv, qkv, cum4), semantics=("parallel",), exchange=exchange)


def _attn_c_bwd(qkv, cum4, o, lse, do, name):
    s = qkv.shape[0]
    t = _fox_tile(s)
    nq = s // t

    def body(q_ref, k_ref, v_ref, c_ref, o_ref, lse_ref, do_ref, dq_ref, dk_ref, dv_ref, dc_ref, dk_acc, dv_acc):
        first = _lane_is_first_head()
        causal = _stacked_mask(t, strict=False)
        eye = lax.broadcasted_iota(jnp.int32, (t, t), 0) == lax.broadcasted_iota(jnp.int32, (t, t), 1)
        dk_acc[...] = jnp.zeros_like(dk_acc)
        dv_acc[...] = jnp.zeros_like(dv_acc)
        dc_ref[...] = jnp.zeros_like(dc_ref)

        def tile(j, carry, diag, qs, dos, delta, lse):
            dq_acc, rs = carry
            c0 = pl.multiple_of(j * t, t)
            k2 = k_ref[pl.ds(c0, t), :]
            v2 = v_ref[pl.ds(c0, t), :]
            cs = c_ref[0, j]
            z = lax.dot_general(qs, k2, _NT, preferred_element_type=F32)
            sc = jnp.concatenate([z[:t] - cs[0:1, :], z[t:] - cs[1:2, :]], axis=0)
            p = jnp.exp(sc - lse)
            if diag:
                p = jnp.where(causal, p, 0.0)
            ds = p * (lax.dot_general(dos, v2, _NT, preferred_element_type=F32) - delta)
            dsb = ds.astype(BF16)
            dq = jnp.dot(dsb, k2, preferred_element_type=F32)
            dk_acc[pl.ds(c0, t), :] += lax.dot_general(dsb, qs, _TN, preferred_element_type=F32)
            dv_acc[pl.ds(c0, t), :] += lax.dot_general(p.astype(BF16), dos, _TN, preferred_element_type=F32)
            col_sums = jnp.concatenate([jnp.sum(ds[:t], axis=0, keepdims=True), jnp.sum(ds[t:], axis=0, keepdims=True)], axis=0)
            dc_ref[0, j] = dc_ref[0, j] - col_sums
            return dq_acc + jnp.where(first, dq[:t], dq[t:]), rs + jnp.sum(ds, axis=1, keepdims=True)

        def qblock(i, _):
            r0 = pl.multiple_of(i * t, t)
            do2 = do_ref[pl.ds(r0, t), :]
            qs = _stack_heads(q_ref[pl.ds(r0, t), :] * Q_SCALE, first)
            dos = _stack_heads(do2, first)
            delta = jnp.concatenate(_rowsum_heads(do2.astype(F32) * o_ref[pl.ds(r0, t), :], first), axis=0)
            lse = jnp.concatenate([lse_ref[pl.ds(r0, t), 0:1], lse_ref[pl.ds(r0, t), LANES:LANES + 1]], axis=0)
            carry = (jnp.zeros((t, LANES), F32), jnp.zeros((2 * t, 1), F32))
            carry = lax.fori_loop(0, i, lambda j, c: tile(j, c, False, qs, dos, delta, lse), carry)
            dq_acc, rs = tile(i, carry, True, qs, dos, delta, lse)
            dq_ref[pl.ds(r0, t), :] = (dq_acc * Q_SCALE).astype(BF16)
            as_row = lambda col_vec: jnp.sum(jnp.where(eye, col_vec, 0.0), axis=0, keepdims=True)
            dc_ref[0, i] = dc_ref[0, i] + jnp.concatenate([as_row(rs[:t]), as_row(rs[t:])], axis=0)
            return 0

        lax.fori_loop(0, nq, qblock, 0)
        dk_ref[...] = dk_acc[...].astype(BF16)
        dv_ref[...] = dv_acc[...].astype(BF16)

    out = jax.ShapeDtypeStruct((s, BRANCH), BF16)
    return pl.pallas_call(
        body, name=name, grid=(N_PAIRS,),
        in_specs=[_pair_spec(s), _pair_spec(s, N_PAIRS), _pair_spec(s, 2 * N_PAIRS), _cum_spec(nq, t),
                  _pair_spec(s), _stat_spec(s), _pair_spec(s)],
        out_specs=[_pair_spec(s)] * 3 + [_cum_spec(nq, t)],
        out_shape=[out] * 3 + [jax.ShapeDtypeStruct(cum4.shape, F32)],
        scratch_shapes=[pltpu.VMEM((s, LANES), F32), pltpu.VMEM((s, LANES), F32)],
        compiler_params=_params(("parallel",)),
    )(qkv, qkv, qkv, cum4, o, lse, do)


FG_CHUNK = 512


def _tri_dot3(x, t):
    hi = x.astype(BF16)
    r1 = x - hi.astype(F32)
    mid = r1.astype(BF16)
    lo = (r1 - mid.astype(F32)).astype(BF16)
    return (jnp.dot(hi, t, preferred_element_type=F32) + jnp.dot(mid, t, preferred_element_type=F32)
            + jnp.dot(lo, t, preferred_element_type=F32))


def _fgate_fwd(h, wf_t, b_col, name):
    s = h.shape[0]
    c = min(FG_CHUNK, s)

    def body(h_ref, w_ref, b_ref, xf_ref, cum_ref, carry_ref):
        @pl.when(pl.program_id(0) == 0)
        def _():
            carry_ref[...] = jnp.zeros_like(carry_ref)

        xf = lax.dot_general(w_ref[...], h_ref[...], _NT, preferred_element_type=F32) + b_ref[:, 0:1]
        xf_ref[...] = xf
        logf = jnp.minimum(xf, 0.0) - jnp.log(1.0 + jnp.exp(-jnp.abs(xf)))
        row = lax.broadcasted_iota(jnp.int32, (c, c), 0)
        col = lax.broadcasted_iota(jnp.int32, (c, c), 1)
        cum = _tri_dot3(logf, (row <= col).astype(BF16)) + carry_ref[:, 0:1]
        cum_ref[...] = cum
        carry_ref[...] = _bcast_lanes(cum[:, c - 1:c])

    out = jax.ShapeDtypeStruct((N_HEADS, s), F32)
    return pl.pallas_call(
        body, name=name, grid=(s // c,),
        in_specs=[pl.BlockSpec((c, D_MODEL), lambda i: (i, 0)),
                  pl.BlockSpec((N_HEADS, D_MODEL), lambda i: (0, 0)),
                  pl.BlockSpec((N_HEADS, LANES), lambda i: (0, 0))],
        out_specs=[pl.BlockSpec((N_HEADS, c), lambda i: (0, i))] * 2,
        out_shape=[out, out],
        scratch_shapes=[pltpu.VMEM((N_HEADS, LANES), F32)],
        compiler_params=_params(("arbitrary",)),
    )(h, wf_t, b_col)


def _fgate_bwd(dcum, xf, h, wf_t, name):
    s = h.shape[0]
    c = min(FG_CHUNK, s)
    n = s // c

    def body(dc_ref, xf_ref, h_ref, w_ref, dw_ref, dh_ref, db_ref, carry_ref):
        @pl.when(pl.program_id(0) == 0)
        def _():
            carry_ref[...] = jnp.zeros_like(carry_ref)
            dw_ref[...] = jnp.zeros_like(dw_ref)
            db_ref[...] = jnp.zeros_like(db_ref)

        row = lax.broadcasted_iota(jnp.int32, (c, c), 0)
        col = lax.broadcasted_iota(jnp.int32, (c, c), 1)
        dlogf = _tri_dot3(dc_ref[...], (row >= col).astype(BF16)) + carry_ref[:, 0:1]
        carry_ref[...] = _bcast_lanes(dlogf[:, 0:1])
        xf = xf_ref[...]
        e = jnp.exp(-jnp.abs(xf))
        r = 1.0 / (1.0 + e)
        dxf = dlogf * jnp.where(xf >= 0, e * r, r)
        db_ref[...] += _bcast_lanes(jnp.sum(dxf, axis=1, keepdims=True))
        dxb = dxf.astype(BF16)
        dw_ref[...] += jnp.dot(dxb, h_ref[...], preferred_element_type=F32)
        dh_ref[...] = lax.dot_general(dxb, w_ref[...], _TN, preferred_element_type=F32)

    rev = lambda i: n - 1 - i
    return pl.pallas_call(
        body, name=name, grid=(n,),
        in_specs=[pl.BlockSpec((N_HEADS, c), lambda i: (0, rev(i))),
                  pl.BlockSpec((N_HEADS, c), lambda i: (0, rev(i))),
                  pl.BlockSpec((c, D_MODEL), lambda i: (rev(i), 0)),
                  pl.BlockSpec((N_HEADS, D_MODEL), lambda i: (0, 0))],
        out_specs=[pl.BlockSpec((N_HEADS, D_MODEL), lambda i: (0, 0)),
                   pl.BlockSpec((c, D_MODEL), lambda i: (rev(i), 0)),
                   pl.BlockSpec((N_HEADS, LANES), lambda i: (0, 0))],
        out_shape=[jax.ShapeDtypeStruct((N_HEADS, D_MODEL), F32), jax.ShapeDtypeStruct((s, D_MODEL), F32),
                   jax.ShapeDtypeStruct((N_HEADS, LANES), F32)],
        scratch_shapes=[pltpu.VMEM((N_HEADS, LANES), F32)],
        compiler_params=_params(("arbitrary",)),
    )(dcum, xf, h, wf_t)


def _to_cum4(v, t):
    s = v.shape[1]
    return v.reshape(N_PAIRS, 2, s // t, t).transpose(0, 2, 1, 3)


def _from_cum4(v4):
    p, nt, two, t = v4.shape
    return v4.transpose(0, 2, 1, 3).reshape(p * two, nt * t)


def _alibi_slopes():
    return (2.0 ** (-8.0 * np.arange(1, N_HEADS + 1, dtype=np.float32) / N_HEADS)).astype(np.float32)


def _per_head_lanes(v):
    return jnp.repeat(v.astype(F32).reshape(N_PAIRS, 1, 2), LANES, axis=2)


def _attn_a_specs(s):
    q = _pair_spec(s)
    k = pl.BlockSpec((s, LANES), lambda p: (0, N_PAIRS + p // 8))
    v = pl.BlockSpec((s, LANES), lambda p: (0, N_PAIRS + KV_A // LANES + p // 8))
    head = pl.BlockSpec((1, 1, 2 * LANES), lambda p: (p, 0, 0))
    return q, k, v, head


def _attn_a_geometry(p, slope_ref, sink_ref):
    kv_half = (p // 4) % 2
    kv_first = kv_half == 0
    lane_first = _lane_is_first_head()
    kv_lanes = (lax.broadcasted_iota(jnp.int32, (1, LANES), 1) // HEAD_DIM) == kv_half
    row = lax.broadcasted_iota(jnp.int32, (2 * WINDOW, 2 * WINDOW), 0)
    cj = lax.broadcasted_iota(jnp.int32, (2 * WINDOW, 2 * WINDOW), 1)
    second = row >= WINDOW
    dist = WINDOW + jnp.where(second, row - WINDOW, row) - cj
    valid = (dist >= 0) & (dist < WINDOW)
    per_row = lambda ref: jnp.where(second[:, 0:1], ref[0, :, LANES:LANES + 1], ref[0, :, 0:1])
    return kv_first, lane_first, kv_lanes, per_row(slope_ref) * dist.astype(F32), valid, per_row(sink_ref)


def _swap_halves(x):
    return pltpu.roll(x, HEAD_DIM, 1)


def _attn_a_fwd(qkv, slopes, sinks, name, exchange=None):
    s = qkv.shape[0]
    nb = s // WINDOW

    def body(q_ref, k_ref, v_ref, sl_ref, sk_ref, o_ref, lse_ref):
        kv_first, lane_first, kv_lanes, bias, valid, sink = _attn_a_geometry(pl.program_id(0), sl_ref, sk_ref)

        def block(r0, k0, width):
            q2 = q_ref[pl.ds(r0, WINDOW), :].astype(F32) * Q_SCALE
            q2r = _swap_halves(q2)
            xs = jnp.concatenate([jnp.where(kv_first, q2, q2r), jnp.where(kv_first, q2r, q2)], axis=0).astype(BF16)
            km = jnp.where(kv_lanes, k_ref[pl.ds(k0, width), :], 0).astype(BF16)
            vm = jnp.where(kv_lanes, v_ref[pl.ds(k0, width), :], 0).astype(BF16)
            sc = lax.dot_general(xs, km, _NT, preferred_element_type=F32) - bias[:, 2 * WINDOW - width:]
            sc = jnp.where(valid[:, 2 * WINDOW - width:], sc, NEG)
            m = jnp.maximum(jnp.max(sc, axis=1, keepdims=True), sink)
            pr = jnp.exp(sc - m)
            l = jnp.sum(pr, axis=1, keepdims=True) + jnp.exp(sink - m)
            os = jnp.dot(pr.astype(BF16), vm, preferred_element_type=F32) * (1.0 / l)
            lse = m + jnp.log(l)
            lse_ref[pl.ds(r0, WINDOW), 0:LANES] = _bcast_lanes(lse[:WINDOW])
            lse_ref[pl.ds(r0, WINDOW), LANES:2 * LANES] = _bcast_lanes(lse[WINDOW:])
            oa = jnp.where(kv_first, os[:WINDOW], _swap_halves(os[:WINDOW]))
            ob = jnp.where(kv_first, _swap_halves(os[WINDOW:]), os[WINDOW:])
            o_ref[pl.ds(r0, WINDOW), :] = jnp.where(lane_first, oa, ob)

        block(0, 0, WINDOW)

        def loop(n, _):
            r0 = pl.multiple_of(n * WINDOW, WINDOW)
            block(r0, pl.multiple_of(r0 - WINDOW, WINDOW), 2 * WINDOW)
            return 0

        lax.fori_loop(1, nb, loop, 0)

    q, k, v, head = _attn_a_specs(s)
    return _grid_call(
        body, name=name, grid=(N_PAIRS,),
        in_specs=[q, k, v, head, head],
        out_specs=[_pair_spec(s), _stat_spec(s)],
        out_shape=[jax.ShapeDtypeStruct((s, BRANCH), F32), jax.ShapeDtypeStruct((s, N_HEADS * LANES), F32)],
        args=(qkv, qkv, qkv, slopes, sinks), semantics=("parallel",), exchange=exchange)


def _attn_a_bwd(qkv, slopes, sinks, o, lse, do, name):
    s = qkv.shape[0]
    nb = s // WINDOW

    def body(q_ref, k_ref, v_ref, sl_ref, sk_ref, o_ref, lse_ref, do_ref, dq_ref, dk_ref, dv_ref, dsk_ref):
        p_id = pl.program_id(0)
        kv_first, lane_first, kv_lanes, bias, valid, sink = _attn_a_geometry(p_id, sl_ref, sk_ref)

        @pl.when(p_id % 8 == 0)
        def _():
            dk_ref[...] = jnp.zeros_like(dk_ref)
            dv_ref[...] = jnp.zeros_like(dv_ref)

        def align(v2):
            v2r = _swap_halves(v2)
            both = jnp.concatenate([jnp.where(kv_first, v2, v2r), jnp.where(kv_first, v2r, v2)], axis=0)
            return jnp.where(kv_lanes, both, 0.0).astype(BF16)

        def block(r0, k0, width, sink_sum):
            xq = align(q_ref[pl.ds(r0, WINDOW), :].astype(F32) * Q_SCALE)
            do2 = do_ref[pl.ds(r0, WINDOW), :].astype(F32)
            xdo = align(do2)
            delta = jnp.concatenate(_rowsum_heads(do2 * o_ref[pl.ds(r0, WINDOW), :], lane_first), axis=0)
            lse = jnp.concatenate([lse_ref[pl.ds(r0, WINDOW), 0:1], lse_ref[pl.ds(r0, WINDOW), LANES:LANES + 1]], axis=0)
            km = jnp.where(kv_lanes, k_ref[pl.ds(k0, width), :], 0).astype(BF16)
            vm = jnp.where(kv_lanes, v_ref[pl.ds(k0, width), :], 0).astype(BF16)
            sc = lax.dot_general(xq, km, _NT, preferred_element_type=F32) - bias[:, 2 * WINDOW - width:]
            pr = jnp.where(valid[:, 2 * WINDOW - width:], jnp.exp(sc - lse), 0.0)
            ds = pr * (lax.dot_general(xdo, vm, _NT, preferred_element_type=F32) - delta)
            dsb = ds.astype(BF16)
            dq_al = jnp.dot(dsb, km, preferred_element_type=F32)
            dk_ref[pl.ds(k0, width), :] += lax.dot_general(dsb, xq, _TN, preferred_element_type=F32)
            dv_ref[pl.ds(k0, width), :] += lax.dot_general(pr.astype(BF16), xdo, _TN, preferred_element_type=F32)
            dqa = jnp.where(kv_first, dq_al[:WINDOW], _swap_halves(dq_al[:WINDOW]))
            dqb = jnp.where(kv_first, _swap_halves(dq_al[WINDOW:]), dq_al[WINDOW:])
            dq_ref[pl.ds(r0, WINDOW), :] = (jnp.where(lane_first, dqa, dqb) * Q_SCALE).astype(BF16)
            return sink_sum + jnp.exp(sink - lse) * delta

        sink_sum = block(0, 0, WINDOW, jnp.zeros((2 * WINDOW, 1), F32))

        def loop(n, c):
            r0 = pl.multiple_of(n * WINDOW, WINDOW)
            return block(r0, pl.multiple_of(r0 - WINDOW, WINDOW), 2 * WINDOW, c)

        sink_sum = lax.fori_loop(1, nb, loop, sink_sum)
        dsk_ref[0, :, 0:LANES] = jnp.broadcast_to(-jnp.sum(sink_sum[:WINDOW], axis=0, keepdims=True), (1, LANES))
        dsk_ref[0, :, LANES:2 * LANES] = jnp.broadcast_to(-jnp.sum(sink_sum[WINDOW:], axis=0, keepdims=True), (1, LANES))

    q, k, v, head = _attn_a_specs(s)
    kv_out = pl.BlockSpec((s, LANES), lambda p: (0, p // 8))
    return pl.pallas_call(
        body, name=name, grid=(N_PAIRS,),
        in_specs=[q, k, v, head, head, _pair_spec(s), _stat_spec(s), _pair_spec(s)],
        out_specs=[_pair_spec(s), kv_out, kv_out, head],
        out_shape=[jax.ShapeDtypeStruct((s, BRANCH), BF16), jax.ShapeDtypeStruct((s, KV_A), F32),
                   jax.ShapeDtypeStruct((s, KV_A), F32), jax.ShapeDtypeStruct((N_PAIRS, 1, 2 * LANES), F32)],
        compiler_params=_params(("arbitrary",)),
    )(qkv, qkv, qkv, slopes, sinks, o, lse, do)


def _layer_kind(i):
    return i % 3, i // 3


GATHER_FIRST = [("in", 0)]
GATHER_BEHIND = {0: [("out", 0), ("in", 1)], 1: [("out", 1), ("in", 2), ("out", 2)], 2: [("in", 3), ("out", 3)]}


def _forward_backward(x, target, g_pre, g_post, sinks_a, b_f_c, shards, chip):
    s = x.shape[0]
    slopes = _per_head_lanes(jnp.asarray(_alibi_slopes()))
    w_in, w_out, wf_t = {}, {}, {}

    def deliver(keys, gathered):
        for (side, layer), g in zip(keys, gathered):
            sh = shards[(side, layer)]
            g = lax.dynamic_update_slice(g, sh[None], (chip, 0, 0))
            if side == "out":
                w_out[layer] = g.reshape(4 * sh.shape[0], sh.shape[1])
                continue
            w = g.transpose(1, 0, 2).reshape(sh.shape[0], 4 * sh.shape[1])
            if _layer_kind(layer)[0] == 2:
                w, wf_t[layer] = w[:, :4 * BRANCH], w[:, 4 * BRANCH:].T
            w_in[layer] = w

    deliver(GATHER_FIRST, _exchange_call(_GatherExchange([shards[k] for k in GATHER_FIRST]), "gather_first_weights"))
    saved = []
    for i in range(DEPTH):
        kind, j = _layer_kind(i)
        tag = f"l{i}"
        w = w_in[i]
        nqkv = A_QKV if kind == 0 else B_QKV
        tn = 512 if kind == 0 else 1024
        h = _rmsnorm_fwd(x, g_pre[i:i + 1], f"prenorm_{tag}")
        qkv = _matmul(h, w, mode="nn", out_dtype=BF16, name=f"inproj_qkv_{tag}", n=nqkv, tn=tn)
        z = _matmul(h, w, mode="nn", out_dtype=F32, name=f"inproj_gate_{tag}", n=BRANCH, b_off=nqkv // tn, tn=tn)
        behind = GATHER_BEHIND.get(i)
        exchange = _GatherExchange([shards[k] for k in behind]) if behind else None
        if kind == 0:
            sink_l = _per_head_lanes(sinks_a[j])
            (o, lse), arrived = _attn_a_fwd(qkv, slopes, sink_l, f"attn_a_fwd_{tag}", exchange)
            extra = (sink_l, lse)
        elif kind == 1:
            (o, extra), arrived = _attn_b_fwd(qkv, f"attn_b_fwd_{tag}", exchange)
        else:
            b_col = jnp.broadcast_to(b_f_c[j].astype(F32)[:, None], (N_HEADS, LANES))
            xf, cum = _fgate_fwd(h, wf_t[i], b_col, f"fgate_fwd_{tag}")
            cum4 = _to_cum4(cum, _fox_tile(s))
            (o, lse), arrived = _attn_c_fwd(qkv, cum4, f"attn_c_fwd_{tag}", exchange)
            extra = (xf, cum4, lse)
        if behind:
            deliver(behind, arrived)
        u = _gate_fwd(o, z, f"gate_{tag}")
        y = _matmul(u, w_out[i], mode="nn", out_dtype=F32, name=f"outproj_{tag}")
        saved.append((x, h, qkv, z, o, u, y, extra))
        x = _post_fwd(x, y, g_post[i:i + 1], f"postnorm_{tag}")

    dx, loss_part = _loss_and_grad(x, target)

    d_g_pre, d_g_post = [None] * DEPTH, [None] * DEPTH
    d_w_in = {0: [None, None], 1: [None], 2: [None]}
    d_w_out = {0: [None, None], 1: [None], 2: [None]}
    d_sinks = [None, None]
    d_b_f = None
    for i in reversed(range(DEPTH)):
        kind, j = _layer_kind(i)
        tag = f"l{i}"
        x_in, h, qkv, z, o, u, y, extra = saved[i]
        tn = 512 if kind == 0 else 1024
        dy, d_g_post[i] = _post_bwd(dx, y, g_post[i:i + 1], f"postnorm_bwd_{tag}")
        d_w_out[kind][j] = _matmul(u, dy, mode="tn", out_dtype=F32, name=f"dw_out_{tag}", tk=512)
        du = _matmul(dy, w_out[i], mode="nt", out_dtype=F32, name=f"d_gated_{tag}")
        do, dz = _gate_bwd(du, o, z, f"gate_bwd_{tag}")
        dhs = []
        if kind == 0:
            sink_l, lse = extra
            dq, dk, dv, dsk = _attn_a_bwd(qkv, slopes, sink_l, o, lse, do, f"attn_a_bwd_{tag}")
            d_sinks[j] = dsk[:, 0, ::LANES].reshape(N_HEADS)
            parts = [dq, dk.astype(BF16), dv.astype(BF16), dz]
        elif kind == 1:
            dq, dk, dv = _attn_b_bwd(qkv, extra, do, f"attn_b_bwd_{tag}")
            parts = [dq, dk, dv, dz]
        else:
            xf, cum4, lse = extra
            dq, dk, dv, dcum4 = _attn_c_bwd(qkv, cum4, o, lse, do, f"attn_c_bwd_{tag}")
            d_wf_t, dh_f, db = _fgate_bwd(_from_cum4(dcum4), xf, h, wf_t[i], f"fgate_bwd_{tag}")
            d_b_f = db[:, 0]
            dhs.append(dh_f)
            parts = [dq, dk, dv, dz]
        dproj = jnp.concatenate(parts, axis=1)
        dw = _matmul(h, dproj, mode="tn", out_dtype=F32, name=f"dw_in_{tag}", tk=512, tn=tn)
        if kind == 2:
            dw = jnp.concatenate([dw, d_wf_t.T], axis=1)
        d_w_in[kind][j] = dw
        dhs.insert(0, _matmul(dproj, w_in[i], mode="nt", out_dtype=F32, name=f"dh_{tag}", tk=512))
        dx, d_g_pre[i] = _pre_bwd(dx, dhs, x_in, g_pre[i:i + 1], f"prenorm_bwd_{tag}")

    return dict(loss=loss_part, dx=dx, g_pre=jnp.concatenate(d_g_pre, axis=0), g_post=jnp.concatenate(d_g_post, axis=0),
                sinks_a=jnp.stack(d_sinks), b_f_c=d_b_f[None, :], w_in=d_w_in, w_out=d_w_out)


def _place():
    x, y, c = lax.axis_index("x"), lax.axis_index("y"), lax.axis_index("c")
    others = [(1 - x, y), (x, 1 - y), (1 - x, 1 - y)]
    return x, y, c, others


def _half_rows(ref_rows, which):
    half = ref_rows // 2
    return pl.ds(pl.multiple_of(which * half, half), half)


def _remote(src, dst, sems, k, device):
    send, recv = sems
    return pltpu.make_async_remote_copy(src_ref=src, dst_ref=dst, send_sem=send.at[k], recv_sem=recv.at[k],
                                        device_id=device, device_id_type=MESH)


def _hbm_call(body, name, ins, out_shapes, n_remote, aliases=None):
    any_spec = pl.BlockSpec(memory_space=pl.ANY)
    return pl.pallas_call(
        body, name=name, in_specs=[any_spec] * len(ins), out_specs=[any_spec] * len(out_shapes),
        out_shape=out_shapes, input_output_aliases=aliases or {},
        scratch_shapes=[pltpu.SemaphoreType.DMA((n_remote,)), pltpu.SemaphoreType.DMA((n_remote,))],
    )(*ins)


class _GatherExchange:
    def __init__(self, shards):
        self.ins = list(shards)
        self.out_shapes = [jax.ShapeDtypeStruct((4,) + a.shape, a.dtype) for a in shards]
        self.n_sems = 6 * len(shards)
        self.aliases = {}

    def _copies(self, ins, outs, sems):
        x, y, c, others = _place()
        me = 2 * x + y
        table = []
        for w, (src, dst) in enumerate(zip(ins, outs)):
            mine, theirs = _half_rows(src.shape[0], c), _half_rows(src.shape[0], 1 - c)
            for j, (px, py) in enumerate(others):
                there = 2 * px + py
                send = _remote(src.at[mine], dst.at[me, mine], sems, 6 * w + j, (px, py, c))
                landed = _remote(dst.at[there, mine], dst.at[there, mine], sems, 6 * w + j, (px, py, c))
                passed = _remote(dst.at[there, mine], dst.at[there, mine], sems, 6 * w + 3 + j, (x, y, 1 - c))
                from_sibling = _remote(dst.at[there, theirs], dst.at[there, theirs], sems, 6 * w + 3 + j, (x, y, 1 - c))
                table.append((send, landed, passed, from_sibling))
        return table

    def start(self, ins, outs, sems):
        for send, _, _, _ in self._copies(ins, outs, sems):
            send.start()

    def mid(self, ins, outs, sems):
        for _, landed, passed, _ in self._copies(ins, outs, sems):
            landed.wait_recv()
            passed.start()

    def finish(self, ins, outs, sems):
        table = self._copies(ins, outs, sems)
        for _, _, _, from_sibling in table:
            from_sibling.wait_recv()
        for send, _, passed, _ in table:
            send.wait_send()
            passed.wait_send()


def _exchange_call(ex, name):
    n_in, n_out = len(ex.ins), len(ex.out_shapes)

    def body(*refs):
        ins, outs, sems = refs[:n_in], refs[n_in:n_in + n_out], refs[n_in + n_out:]
        ex.start(ins, outs, sems)
        ex.mid(ins, outs, sems)
        ex.finish(ins, outs, sems)

    return _hbm_call(body, name, ex.ins, ex.out_shapes, ex.n_sems, aliases=ex.aliases)


def _grid_call(body, *, name, grid, in_specs, out_specs, out_shape, args, scratch_shapes=(), semantics, exchange=None):
    if exchange is None:
        res = pl.pallas_call(body, name=name, grid=grid, in_specs=list(in_specs), out_specs=list(out_specs),
                             out_shape=list(out_shape), scratch_shapes=list(scratch_shapes),
                             compiler_params=_params(semantics))(*args)
        return res, []
    n_in, n_out, n_scr = len(args), len(out_shape), len(scratch_shapes)
    x_in, x_out = len(exchange.ins), len(exchange.out_shapes)
    steps = grid[0]

    def wrapped(*refs):
        core_in, ex_in = refs[:n_in], refs[n_in:n_in + x_in]
        rest = refs[n_in + x_in:]
        core_out, ex_out = rest[:n_out], rest[n_out:n_out + x_out]
        scratch, sems = rest[n_out + x_out:n_out + x_out + n_scr], rest[n_out + x_out + n_scr:]
        step = pl.program_id(0)

        @pl.when(step == 0)
        def _():
            exchange.start(ex_in, ex_out, sems)

        body(*core_in, *core_out, *scratch)

        @pl.when(step == (3 * steps) // 4 - 1)
        def _():
            exchange.mid(ex_in, ex_out, sems)

        @pl.when(step == steps - 1)
        def _():
            exchange.finish(ex_in, ex_out, sems)

    any_spec = pl.BlockSpec(memory_space=pl.ANY)
    res = pl.pallas_call(
        wrapped, name=name, grid=grid,
        in_specs=list(in_specs) + [any_spec] * x_in, out_specs=list(out_specs) + [any_spec] * x_out,
        out_shape=list(out_shape) + list(exchange.out_shapes),
        input_output_aliases={n_in + a: n_out + b for a, b in exchange.aliases.items()},
        scratch_shapes=list(scratch_shapes) + [pltpu.SemaphoreType.DMA((exchange.n_sems,)),
                                               pltpu.SemaphoreType.DMA((exchange.n_sems,))],
        compiler_params=_params(("arbitrary",)),
    )(*args, *exchange.ins)
    return res[:n_out], res[n_out:]


def _sibling_send(parts):
    n = len(parts)

    def body(*refs):
        ins, outs, sems = refs[:n], refs[n:2 * n], refs[2 * n:2 * n + 2]
        x, y, c, _ = _place()
        pend = []
        for w in range(n):
            cp = _remote(ins[w].at[:, :, _half_rows(ins[w].shape[2], 1 - c)], outs[w], sems, w, (x, y, 1 - c))
            cp.start()
            pend.append(cp)
        for cp in pend:
            cp.wait_recv()
            cp.wait_send()

    out_shapes = [jax.ShapeDtypeStruct((a.shape[0], 4, a.shape[2] // 2, a.shape[3]), a.dtype) for a in parts]
    return _hbm_call(body, "grad_sibling_exchange", parts, out_shapes, n)


def _chip_scatter(sums):
    n = len(sums)

    def body(*refs):
        ins, outs, sems = refs[:n], refs[n:2 * n], refs[2 * n:2 * n + 2]
        x, y, c, others = _place()
        me = 2 * x + y
        sends = []
        for w in range(n):
            for j, (px, py) in enumerate(others):
                cp = _remote(ins[w].at[:, 2 * px + py], outs[w].at[:, me], sems, 3 * w + j, (px, py, c))
                cp.start()
                sends.append(cp)
        for w in range(n):
            for j, (px, py) in enumerate(others):
                landed = outs[w].at[:, 2 * px + py]
                _remote(landed, landed, sems, 3 * w + j, (px, py, c)).wait_recv()
        for cp in sends:
            cp.wait_send()

    out_shapes = [jax.ShapeDtypeStruct(a.shape, a.dtype) for a in sums]
    return _hbm_call(body, "grad_chip_scatter", sums, out_shapes, 3 * n)


def _sibling_join(shards):
    n = len(shards)

    def body(*refs):
        ins, outs, sems = refs[:n], refs[n:2 * n], refs[2 * n:2 * n + 2]
        x, y, c, _ = _place()
        pend = []
        for w in range(n):
            rows = ins[w].shape[1]
            mine, theirs = _half_rows(rows, c), _half_rows(rows, 1 - c)
            cp = _remote(ins[w].at[:, mine], outs[w].at[:, mine], sems, w, (x, y, 1 - c))
            cp.start()
            pend.append((cp, _remote(ins[w].at[:, theirs], outs[w].at[:, theirs], sems, w, (x, y, 1 - c))))
        for cp, landed in pend:
            landed.wait_recv()
            cp.wait_send()

    out_shapes = [jax.ShapeDtypeStruct(a.shape, a.dtype) for a in shards]
    return _hbm_call(body, "grad_sibling_join", shards, out_shapes, n, aliases={w: w for w in range(n)})


SMALL_ROWS = 136


def _all_reduce_small(vec):
    def body(v_ref, o_ref, buf, send, recv, loc):
        x, y, c, _ = _place()
        me = 4 * x + 2 * y + c
        lc = pltpu.make_async_copy(v_ref, buf.at[me], loc.at[0])
        lc.start()
        cps = []
        for k in range(1, 8):
            fx, fy, fc = (k >> 2) & 1, (k >> 1) & 1, k & 1
            peer = (x ^ fx, y ^ fy, c ^ fc)
            cp = pltpu.make_async_remote_copy(src_ref=v_ref, dst_ref=buf.at[me], send_sem=send.at[k - 1],
                                              recv_sem=recv.at[k - 1], device_id=peer, device_id_type=MESH)
            cp.start()
            cps.append((cp, 4 * peer[0] + 2 * peer[1] + peer[2]))
        for k, (cp, src) in enumerate(cps):
            pltpu.make_async_remote_copy(src_ref=v_ref, dst_ref=buf.at[src], send_sem=send.at[k], recv_sem=recv.at[k],
                                         device_id=(x, y, c), device_id_type=MESH).wait_recv()
        for cp, _ in cps:
            cp.wait_send()
        lc.wait()
        total = buf[0]
        for k in range(1, 8):
            total = total + buf[k]
        o_ref[...] = total

    vm = pl.BlockSpec(memory_space=pltpu.VMEM)
    return pl.pallas_call(
        body, name="all_reduce_small", in_specs=[vm], out_specs=vm,
        out_shape=jax.ShapeDtypeStruct(vec.shape, F32),
        scratch_shapes=[pltpu.VMEM((8,) + vec.shape, F32), pltpu.SemaphoreType.DMA((7,)),
                        pltpu.SemaphoreType.DMA((7,)), pltpu.SemaphoreType.DMA((1,))],
    )(vec)


SUM_ROWS = 256


def _add_pairs(part, theirs, name):
    l, four, rh, cc = theirs.shape
    tr = min(SUM_ROWS, rh)
    halves = part.reshape(l, four, 2, rh, cc)

    def body(a_ref, b_ref, o_ref):
        mine = a_ref[0, 0, lax.axis_index("c")]
        o_ref[0, 0] = (mine.astype(F32) + b_ref[0, 0].astype(F32)).astype(o_ref.dtype)

    spec = pl.BlockSpec((1, 1, tr, cc), lambda i, k, r: (i, k, r, 0))
    return pl.pallas_call(
        body, name=name, grid=(l, four, rh // tr),
        in_specs=[pl.BlockSpec((1, 1, 2, tr, cc), lambda i, k, r: (i, k, 0, r, 0)), spec], out_specs=spec,
        out_shape=jax.ShapeDtypeStruct(theirs.shape, theirs.dtype),
        compiler_params=_params(("parallel", "parallel", "parallel")),
    )(halves, theirs)


def _sum_chips(own, arrived, core, name):
    l, four, rh, cc = own.shape
    tr = min(SUM_ROWS, rh)
    nr = rh // tr

    def body(c_ref, own_ref, arr_ref, o_ref):
        x, y = lax.axis_index("x"), lax.axis_index("y")
        tot = own_ref[0, 2 * x + y].astype(F32)
        for px, py in ((1 - x, y), (x, 1 - y), (1 - x, 1 - y)):
            tot = tot + arr_ref[0, 2 * px + py].astype(F32)
        o_ref[0] = tot

    blk = pl.BlockSpec((1, 4, tr, cc), lambda i, r, c_ref: (i, 0, r, 0))
    return pl.pallas_call(
        body, name=name,
        grid_spec=pltpu.PrefetchScalarGridSpec(
            num_scalar_prefetch=1, grid=(l, nr), in_specs=[blk, blk],
            out_specs=pl.BlockSpec((1, tr, cc), lambda i, r, c_ref: (i, c_ref[0] * nr + r, 0))),
        out_shape=jax.ShapeDtypeStruct((l, 2 * rh, cc), F32),
        compiler_params=_params(("parallel", "parallel")),
    )(core, own, arrived)


ADAM_ROWS = 256


def _adamw(w, g, m, v, name):
    shape = w.shape
    cc = shape[-1]
    flat = lambda a: a.reshape(-1, cc)
    rows = flat(w).shape[0]
    tr = min(ADAM_ROWS, rows)
    assert rows % tr == 0
    c1 = 1.0 - ADAM_B1 ** ADAM_STEP
    c2 = 1.0 - ADAM_B2 ** ADAM_STEP

    def body(w_ref, g_ref, m_ref, v_ref, d_ref, nm_ref, nv_ref):
        gv = g_ref[...]
        nm = ADAM_B1 * m_ref[...] + (1.0 - ADAM_B1) * gv
        nv = ADAM_B2 * v_ref[...] + (1.0 - ADAM_B2) * (gv * gv)
        nm_ref[...] = nm
        nv_ref[...] = nv
        d_ref[...] = -ADAM_LR * ((nm / c1) / (jnp.sqrt(nv / c2) + ADAM_EPS) + ADAM_WD * w_ref[...])

    spec = pl.BlockSpec((tr, cc), lambda i: (i, 0))
    sh = jax.ShapeDtypeStruct((rows, cc), F32)
    outs = pl.pallas_call(
        body, name=name, grid=(rows // tr,), in_specs=[spec] * 4, out_specs=[spec] * 3, out_shape=[sh] * 3,
        compiler_params=_params(("parallel",)),
    )(flat(w), flat(g), flat(m), flat(v))
    return [o.reshape(shape) for o in outs]


def _pack_small(g_pre, g_post, sinks_a, b_f_c, loss_row):
    pad = lambda a: jnp.pad(a.reshape(1, -1).astype(F32), ((0, 0), (0, LANES - a.size)))
    rows = [g_pre.astype(F32).reshape(-1, LANES), g_post.astype(F32).reshape(-1, LANES), pad(sinks_a), pad(b_f_c), loss_row]
    packed = jnp.concatenate(rows, axis=0)
    return jnp.pad(packed, ((0, SMALL_ROWS - packed.shape[0]), (0, 0)))


def _unpack_small(p):
    n = DEPTH * D_MODEL // LANES
    return (p[:n].reshape(DEPTH, D_MODEL), p[n:2 * n].reshape(DEPTH, D_MODEL), p[2 * n, :2 * N_HEADS].reshape(2, N_HEADS),
            p[2 * n + 1, :N_HEADS].reshape(1, N_HEADS), p[2 * n + 2, 0])


def kernel(x, g_pre, g_post, w_in_a, w_out_a, sinks_a, w_in_b, w_out_b, w_in_c, b_f_c, w_out_c, loss_target, m_g_pre, m_g_post, m_w_in_a, m_w_out_a, m_sinks_a, m_w_in_b, m_w_out_b, m_w_in_c, m_b_f_c, m_w_out_c, v_g_pre, v_g_post, v_w_in_a, v_w_out_a, v_sinks_a, v_w_in_b, v_w_out_b, v_w_in_c, v_b_f_c, v_w_out_c):
    big_w = [w_in_a, w_out_a, w_in_b, w_out_b, w_in_c, w_out_c]
    big_m = [m_w_in_a, m_w_out_a, m_w_in_b, m_w_out_b, m_w_in_c, m_w_out_c]
    big_v = [v_w_in_a, v_w_out_a, v_w_in_b, v_w_out_b, v_w_in_c, v_w_out_c]

    chip = 2 * lax.axis_index("x") + lax.axis_index("y")
    core = lax.axis_index("c").astype(jnp.int32).reshape(1)
    by_kind = {0: (w_in_a, w_out_a), 1: (w_in_b, w_out_b), 2: (w_in_c, w_out_c)}
    shards = {}
    for i in range(DEPTH):
        kind, j = _layer_kind(i)
        shards[("in", i)] = by_kind[kind][0][j].astype(BF16)
        shards[("out", i)] = by_kind[kind][1][j].astype(BF16)

    res = _forward_backward(x[0], loss_target[0], g_pre, g_post, sinks_a, b_f_c, shards, chip)

    col_parts = lambda gs: jnp.stack([g.reshape(g.shape[0], 4, g.shape[1] // 4).transpose(1, 0, 2) for g in gs]).astype(BF16)
    row_parts = lambda gs: jnp.stack([g.reshape(4, g.shape[0] // 4, g.shape[1]) for g in gs]).astype(BF16)
    parts = [col_parts(res["w_in"][0]), row_parts(res["w_out"][0]), col_parts(res["w_in"][1]), row_parts(res["w_out"][1]),
             col_parts(res["w_in"][2]), row_parts(res["w_out"][2])]
    names = ["w_in_a", "w_out_a", "w_in_b", "w_out_b", "w_in_c", "w_out_c"]
    theirs = _sibling_send(parts)
    chip_sums = [_add_pairs(a, b, f"chip_sum_{nm}") for a, b, nm in zip(parts, theirs, names)]
    arrived = _chip_scatter(chip_sums)
    halves = [_sum_chips(own, arr, core, f"shard_sum_{nm}") for own, arr, nm in zip(chip_sums, arrived, names)]
    grads = _sibling_join(halves)

    small = _unpack_small(_all_reduce_small(
        _pack_small(res["g_pre"], res["g_post"], res["sinks_a"], res["b_f_c"], res["loss"])))
    g_small, loss = small[:4], small[4]

    zero_row = jnp.zeros((1, LANES), F32)
    pk = lambda a: _pack_small(a[0], a[1], a[2], a[3], zero_row)
    sm = _adamw(pk([g_pre, g_post, sinks_a, b_f_c]), pk(g_small), pk([m_g_pre, m_g_post, m_sinks_a, m_b_f_c]),
                pk([v_g_pre, v_g_post, v_sinks_a, v_b_f_c]), "adamw_small")
    sm = [_unpack_small(a)[:4] for a in sm]
    bigs = [_adamw(w, g, m, v, f"adamw_{nm}") for w, g, m, v, nm in zip(big_w, grads, big_m, big_v, names)]

    def ordered(small4, big6):
        return [small4[0], small4[1], big6[0], big6[1], small4[2], big6[2], big6[3], big6[4], small4[3], big6[5]]

    out = [loss, res["dx"][None], *ordered(g_small, grads)]
    for k in range(3):
        out += ordered(sm[k], [b[k] for b in bigs])
    return tuple(out)
```

```python
import functools
import math

import numpy as np
import jax
import jax.numpy as jnp
from jax import lax
from jax.experimental import pallas as pl
from jax.experimental.pallas import tpu as pltpu

F32 = jnp.float32
BF16 = jnp.bfloat16

D_MODEL = 2048
DEPTH = 4
N_HEADS = 32
HEAD_DIM = 64
LANES = 128
N_PAIRS = N_HEADS * HEAD_DIM // LANES
BRANCH = N_HEADS * HEAD_DIM
N_KV_A = 4
KV_A = N_KV_A * HEAD_DIM
WINDOW = 128
NORM_EPS = 1e-6
NEG = -1e30
Q_SCALE = HEAD_DIM ** -0.5

A_QKV = BRANCH + 2 * KV_A
B_QKV = 3 * BRANCH

ADAM_LR = 0.001
ADAM_B1 = 0.9
ADAM_B2 = 0.999
ADAM_EPS = 1e-08
ADAM_WD = 0.01
ADAM_STEP = 10

MESH = pl.DeviceIdType.MESH

_NT = (((1,), (1,)), ((), ()))
_TN = (((0,), (0,)), ((), ()))


def _params(sem=None):
    return pltpu.CompilerParams(dimension_semantics=sem)


def _matmul(a, b, *, mode, out_dtype, name, n=None, b_off=0, tm=1024, tn=1024, tk=2048):
    if mode == "nn":
        (m, k), nn = a.shape, (n or b.shape[1])
    elif mode == "nt":
        (m, k), nn = a.shape, b.shape[0]
    else:
        (k, m), nn = a.shape, b.shape[1]
    tm, tn, tk = min(tm, m), min(tn, nn), min(tk, k)
    assert m % tm == 0 and nn % tn == 0 and k % tk == 0, (name, m, nn, k, tm, tn, tk)
    nk = k // tk

    def body(a_ref, b_ref, o_ref, acc_ref):
        kk = pl.program_id(2)
        if mode == "nn":
            p = jnp.dot(a_ref[...], b_ref[...], preferred_element_type=F32)
        elif mode == "nt":
            p = lax.dot_general(a_ref[...], b_ref[...], _NT, preferred_element_type=F32)
        else:
            p = lax.dot_general(a_ref[...], b_ref[...], _TN, preferred_element_type=F32)
        if nk == 1:
            o_ref[...] = p.astype(o_ref.dtype)
        else:
            @pl.when(kk == 0)
            def _():
                acc_ref[...] = p

            @pl.when(kk > 0)
            def _():
                acc_ref[...] += p

            @pl.when(kk == nk - 1)
            def _():
                o_ref[...] = acc_ref[...].astype(o_ref.dtype)

    if mode == "nn":
        in_specs = [pl.BlockSpec((tm, tk), lambda i, j, kk: (i, kk)),
                    pl.BlockSpec((tk, tn), lambda i, j, kk: (kk, j + b_off))]
    elif mode == "nt":
        in_specs = [pl.BlockSpec((tm, tk), lambda i, j, kk: (i, kk)),
                    pl.BlockSpec((tn, tk), lambda i, j, kk: (j, kk))]
    else:
        in_specs = [pl.BlockSpec((tk, tm), lambda i, j, kk: (kk, i)),
                    pl.BlockSpec((tk, tn), lambda i, j, kk: (kk, j))]
    return pl.pallas_call(
        body, name=name, grid=(m // tm, nn // tn, nk),
        in_specs=in_specs,
        out_specs=pl.BlockSpec((tm, tn), lambda i, j, kk: (i, j)),
        out_shape=jax.ShapeDtypeStruct((m, nn), out_dtype),
        scratch_shapes=[pltpu.VMEM((tm, tn), F32)],
        compiler_params=_params(("parallel", "parallel", "arbitrary")),
    )(a, b)


ROW_TILE = 256


def _row_call(body, name, ins, outs, *, s, acc_outs=()):
    tr = min(ROW_TILE, s)
    row = lambda w: pl.BlockSpec((tr, w), lambda i: (i, 0))
    vec = lambda w: pl.BlockSpec((1, w), lambda i: (0, 0))
    in_specs = [row(a.shape[1]) if kind == "row" else vec(a.shape[1]) for a, kind in ins]
    out_specs = [row(sh.shape[1]) if kind == "row" else vec(sh.shape[1]) for sh, kind in outs]
    return pl.pallas_call(
        body, name=name, grid=(s // tr,), in_specs=in_specs, out_specs=out_specs,
        out_shape=[sh for sh, _ in outs],
        compiler_params=_params(("arbitrary",)),
    )(*[a for a, _ in ins])


def _rsqrt_ms(v):
    return lax.rsqrt(jnp.mean(v * v, axis=-1, keepdims=True) + NORM_EPS)


def _rmsnorm_fwd(x, g, name):
    s, d = x.shape

    def body(x_ref, g_ref, h_ref):
        xv = x_ref[...]
        h_ref[...] = (xv * _rsqrt_ms(xv) * g_ref[...]).astype(BF16)

    return _row_call(body, name, [(x, "row"), (g, "vec")],
                     [(jax.ShapeDtypeStruct((s, d), BF16), "row")], s=s)[0]


def _gate_fwd(o, z, name):
    s, d = o.shape

    def body(o_ref, z_ref, u_ref):
        zv = z_ref[...]
        u_ref[...] = (o_ref[...] * (zv * jax.nn.sigmoid(zv))).astype(BF16)

    return _row_call(body, name, [(o, "row"), (z, "row")],
                     [(jax.ShapeDtypeStruct((s, d), BF16), "row")], s=s)[0]


def _post_fwd(x, y, g, name):
    s, d = x.shape

    def body(x_ref, y_ref, g_ref, o_ref):
        yv = y_ref[...]
        o_ref[...] = x_ref[...] + yv * _rsqrt_ms(yv) * g_ref[...]

    return _row_call(body, name, [(x, "row"), (y, "row"), (g, "vec")],
                     [(jax.ShapeDtypeStruct((s, d), F32), "row")], s=s)[0]


def _loss_and_grad(x, target):
    s, d = x.shape

    def body(x_ref, t_ref, dx_ref, l_ref):
        err = x_ref[...] - t_ref[...]
        dx_ref[...] = err * (1.0 / d)
        part = jnp.sum(jnp.sum(err * err, axis=1, keepdims=True), axis=0, keepdims=True) * (0.5 / d)

        @pl.when(pl.program_id(0) == 0)
        def _():
            l_ref[...] = jnp.zeros_like(l_ref)

        l_ref[...] += jnp.broadcast_to(part, l_ref.shape)

    return _row_call(body, "loss_head", [(x, "row"), (target, "row")],
                     [(jax.ShapeDtypeStruct((s, d), F32), "row"),
                      (jax.ShapeDtypeStruct((1, LANES), F32), "vec")], s=s)


def _norm_bwd_rows(dn, v, g):
    r = _rsqrt_ms(v)
    a = dn * g
    dv = r * (a - v * (r * r) * jnp.mean(a * v, axis=-1, keepdims=True))
    return dv, dn * v * r


def _post_bwd(dx, y, g, name):
    s, d = dx.shape

    def body(dx_ref, y_ref, g_ref, dy_ref, dg_ref):
        dy, dg = _norm_bwd_rows(dx_ref[...], y_ref[...], g_ref[...])
        dy_ref[...] = dy.astype(BF16)

        @pl.when(pl.program_id(0) == 0)
        def _():
            dg_ref[...] = jnp.zeros_like(dg_ref)

        dg_ref[...] += jnp.sum(dg, axis=0, keepdims=True)

    return _row_call(body, name, [(dx, "row"), (y, "row"), (g, "vec")],
                     [(jax.ShapeDtypeStruct((s, d), BF16), "row"),
                      (jax.ShapeDtypeStruct((1, d), F32), "vec")], s=s)


def _gate_bwd(du, o, z, name):
    s, d = du.shape

    def body(du_ref, o_ref, z_ref, do_ref, dz_ref):
        duv, zv = du_ref[...], z_ref[...]
        sig = jax.nn.sigmoid(zv)
        do_ref[...] = (duv * (zv * sig)).astype(BF16)
        dz_ref[...] = (duv * o_ref[...] * (sig * (1.0 + zv * (1.0 - sig)))).astype(BF16)

    return _row_call(body, name, [(du, "row"), (o, "row"), (z, "row")],
                     [(jax.ShapeDtypeStruct((s, d), BF16), "row"),
                      (jax.ShapeDtypeStruct((s, d), BF16), "row")], s=s)


def _pre_bwd(dx, dhs, x, g, name):
    s, d = dx.shape
    n_dh = len(dhs)

    def body(*refs):
        dx_ref, dh_refs, (x_ref, g_ref, o_ref, dg_ref) = refs[0], refs[1:1 + n_dh], refs[1 + n_dh:]
        dh = dh_refs[0][...].astype(F32)
        for r in dh_refs[1:]:
            dh = dh + r[...].astype(F32)
        dv, dg = _norm_bwd_rows(dh, x_ref[...], g_ref[...])
        o_ref[...] = dx_ref[...] + dv

        @pl.when(pl.program_id(0) == 0)
        def _():
            dg_ref[...] = jnp.zeros_like(dg_ref)

        dg_ref[...] += jnp.sum(dg, axis=0, keepdims=True)

    return _row_call(body, name, [(dx, "row")] + [(h, "row") for h in dhs] + [(x, "row"), (g, "vec")],
                     [(jax.ShapeDtypeStruct((s, d), F32), "row"),
                      (jax.ShapeDtypeStruct((1, d), F32), "vec")], s=s)


def _lane_is_first_head():
    return lax.broadcasted_iota(jnp.int32, (1, LANES), 1) < HEAD_DIM


def _bcast_lanes(col):
    return jnp.broadcast_to(col, (col.shape[0], LANES))


def _pair_spec(s, off=0, width=LANES):
    return pl.BlockSpec((s, width), lambda p: (0, p + off))


def _stack_heads(pair, first):
    return jnp.concatenate([jnp.where(first, pair, 0), jnp.where(first, 0, pair)], axis=0).astype(BF16)


def _stacked_mask(t, strict):
    row = lax.broadcasted_iota(jnp.int32, (2 * t, t), 0)
    col = lax.broadcasted_iota(jnp.int32, (2 * t, t), 1)
    query = jnp.where(row >= t, row - t, row)
    return col < query if strict else col <= query


def _rowsum_heads(prod, first):
    return (jnp.sum(jnp.where(first, prod, 0.0), axis=1, keepdims=True),
            jnp.sum(jnp.where(first, 0.0, prod), axis=1, keepdims=True))


def _softplus_parts(z):
    e = jnp.exp(-jnp.abs(z))
    sp = jnp.maximum(z, 0.0) + jnp.log(1.0 + e)
    r = 1.0 / (1.0 + e)
    return sp, jnp.where(z >= 0, r, e * r)


def _split_dot(x, t):
    hi = x.astype(BF16)
    lo = (x - hi.astype(F32)).astype(BF16)
    return jnp.dot(hi, t, preferred_element_type=F32) + jnp.dot(lo, t, preferred_element_type=F32)


def _sb_tile(s):
    return min(256, s)


def _attn_b_fwd(qkv, name, exchange=None):
    s = qkv.shape[0]
    t = _sb_tile(s)
    nq = s // t

    def body(q_ref, k_ref, v_ref, o_ref, lt_ref):
        first = _lane_is_first_head()
        before = _stacked_mask(t, strict=True)
        tri = (lax.broadcasted_iota(jnp.int32, (t, t), 0) >= lax.broadcasted_iota(jnp.int32, (t, t), 1)).astype(BF16)

        def tile(j, carry, diag, qs):
            c, acc = carry
            c0 = pl.multiple_of(j * t, t)
            k2 = k_ref[pl.ds(c0, t), :]
            v2 = v_ref[pl.ds(c0, t), :]
            z = lax.dot_general(qs, k2, _NT, preferred_element_type=F32)
            sp, _ = _softplus_parts(z)
            lf = jnp.where(before, -sp, 0.0) if diag else -sp
            incl = jnp.dot(lf.astype(BF16), tri, preferred_element_type=F32)
            a = jnp.exp(z + c + incl)
            if diag:
                a = jnp.where(before, a, 0.0)
            pv = jnp.dot(a.astype(BF16), v2, preferred_element_type=F32)
            return c + incl[:, 0:1], acc + jnp.where(first, pv[:t], pv[t:])

        def qblock(i, _):
            r0 = pl.multiple_of(i * t, t)
            qs = _stack_heads(q_ref[pl.ds(r0, t), :] * Q_SCALE, first)
            carry = tile(i, (jnp.zeros((2 * t, 1), F32), jnp.zeros((t, LANES), F32)), True, qs)
            carry = lax.fori_loop(0, i, lambda jj, c: tile(i - 1 - jj, c, False, qs), carry)
            o_ref[pl.ds(r0, t), :] = carry[1]
            lt_ref[pl.ds(r0, t), 0:LANES] = _bcast_lanes(carry[0][:t])
            lt_ref[pl.ds(r0, t), LANES:2 * LANES] = _bcast_lanes(carry[0][t:])
            return 0

        lax.fori_loop(0, nq, qblock, 0)

    return _grid_call(
        body, name=name, grid=(N_PAIRS,),
        in_specs=[_pair_spec(s), _pair_spec(s, N_PAIRS), _pair_spec(s, 2 * N_PAIRS)],
        out_specs=[_pair_spec(s), _stat_spec(s)],
        out_shape=[jax.ShapeDtypeStruct((s, BRANCH), F32), jax.ShapeDtypeStruct((s, N_HEADS * LANES), F32)],
        args=(qkv, qkv, qkv), semantics=("parallel",), exchange=exchange)


def _attn_b_bwd(qkv, ltot, do, name, exchange=None):
    s = qkv.shape[0]
    t = _sb_tile(s)
    nq = s // t

    def body(q_ref, k_ref, v_ref, lt_ref, do_ref, dq_ref, dk_ref, dv_ref, dk_acc, dv_acc):
        first = _lane_is_first_head()
        before = _stacked_mask(t, strict=True)
        tri = (lax.broadcasted_iota(jnp.int32, (t, t), 0) <= lax.broadcasted_iota(jnp.int32, (t, t), 1)).astype(BF16)
        dk_acc[...] = jnp.zeros_like(dk_acc)
        dv_acc[...] = jnp.zeros_like(dv_acc)

        def tile(j, carry, diag, qs, dos, lt):
            p_l, p_g, dq_acc = carry
            c0 = pl.multiple_of(j * t, t)
            k2 = k_ref[pl.ds(c0, t), :]
            v2 = v_ref[pl.ds(c0, t), :]
            z = lax.dot_general(qs, k2, _NT, preferred_element_type=F32)
            sp, sig = _softplus_parts(z)
            lf = jnp.where(before, -sp, 0.0) if diag else -sp
            pref_l = jnp.dot(lf.astype(BF16), tri, preferred_element_type=F32)
            a = jnp.exp(z + ((lt - p_l) - pref_l + lf))
            if diag:
                a = jnp.where(before, a, 0.0)
            g = a * lax.dot_general(dos, v2, _NT, preferred_element_type=F32)
            pref_g = jnp.dot(g.astype(BF16), tri, preferred_element_type=F32)
            dz = g - sig * (p_g + pref_g)
            if diag:
                dz = jnp.where(before, dz, 0.0)
            dzb = dz.astype(BF16)
            dq = jnp.dot(dzb, k2, preferred_element_type=F32)
            dk_acc[pl.ds(c0, t), :] += lax.dot_general(dzb, qs, _TN, preferred_element_type=F32)
            dv_acc[pl.ds(c0, t), :] += lax.dot_general(a.astype(BF16), dos, _TN, preferred_element_type=F32)
            return p_l + pref_l[:, t - 1:t], p_g + pref_g[:, t - 1:t], dq_acc + jnp.where(first, dq[:t], dq[t:])

        def qblock(i, _):
            r0 = pl.multiple_of(i * t, t)
            qs = _stack_heads(q_ref[pl.ds(r0, t), :] * Q_SCALE, first)
            dos = _stack_heads(do_ref[pl.ds(r0, t), :], first)
            lt = jnp.concatenate([lt_ref[pl.ds(r0, t), 0:1], lt_ref[pl.ds(r0, t), LANES:LANES + 1]], axis=0)
            zero = jnp.zeros((2 * t, 1), F32)
            carry = (zero, zero, jnp.zeros((t, LANES), F32))
            carry = lax.fori_loop(0, i, lambda j, c: tile(j, c, False, qs, dos, lt), carry)
            carry = tile(i, carry, True, qs, dos, lt)
            dq_ref[pl.ds(r0, t), :] = (carry[2] * Q_SCALE).astype(BF16)
            return 0

        lax.fori_loop(0, nq, qblock, 0)
        dk_ref[...] = dk_acc[...].astype(BF16)
        dv_ref[...] = dv_acc[...].astype(BF16)

    out = jax.ShapeDtypeStruct((s, BRANCH), BF16)
    return _grid_call(
        body, name=name, grid=(N_PAIRS,),
        in_specs=[_pair_spec(s), _pair_spec(s, N_PAIRS), _pair_spec(s, 2 * N_PAIRS), _stat_spec(s), _pair_spec(s)],
        out_specs=[_pair_spec(s)] * 3, out_shape=[out] * 3,
        scratch_shapes=[pltpu.VMEM((s, LANES), F32), pltpu.VMEM((s, LANES), F32)],
        args=(qkv, qkv, qkv, ltot, do), semantics=("parallel",), exchange=exchange)


def _fox_tile(s):
    return min(256, s)


def _stat_spec(s):
    return pl.BlockSpec((s, 2 * LANES), lambda p: (0, p))


def _cum_spec(nt, t):
    return pl.BlockSpec((1, nt, 2, t), lambda p: (p, 0, 0, 0))


def _attn_c_fwd(qkv, cum4, name, exchange=None):
    s = qkv.shape[0]
    t = _fox_tile(s)
    nq = s // t

    def body(q_ref, k_ref, v_ref, c_ref, o_ref, lse_ref):
        first = _lane_is_first_head()
        causal = _stacked_mask(t, strict=False)

        def tile(j, carry, diag, qs):
            c0 = pl.multiple_of(j * t, t)
            k2 = k_ref[pl.ds(c0, t), :]
            v2 = v_ref[pl.ds(c0, t), :]
            cs = c_ref[0, j]
            m_prev, l_prev, acc = carry
            z = lax.dot_general(qs, k2, _NT, preferred_element_type=F32)
            sc = jnp.concatenate([z[:t] - cs[0:1, :], z[t:] - cs[1:2, :]], axis=0)
            if diag:
                sc = jnp.where(causal, sc, NEG)
            m_new = jnp.maximum(m_prev, jnp.max(sc, axis=1, keepdims=True))
            alpha = jnp.exp(m_prev - m_new)
            p = jnp.exp(sc - m_new)
            l_new = alpha * l_prev + jnp.sum(p, axis=1, keepdims=True)
            pv = jnp.dot(p.astype(BF16), v2, preferred_element_type=F32)
            acc = jnp.where(first, acc * alpha[:t] + pv[:t], acc * alpha[t:] + pv[t:])
            return m_new, l_new, acc

        def qblock(i, _):
            r0 = pl.multiple_of(i * t, t)
            qs = _stack_heads(q_ref[pl.ds(r0, t), :] * Q_SCALE, first)
            carry = (jnp.full((2 * t, 1), NEG, F32), jnp.zeros((2 * t, 1), F32), jnp.zeros((t, LANES), F32))
            carry = lax.fori_loop(0, i, lambda j, c: tile(j, c, False, qs), carry)
            m, l, acc = tile(i, carry, True, qs)
            inv = 1.0 / l
            lse = m + jnp.log(l)
            o_ref[pl.ds(r0, t), :] = acc * jnp.where(first, inv[:t], inv[t:])
            lse_ref[pl.ds(r0, t), 0:LANES] = _bcast_lanes(lse[:t])
            lse_ref[pl.ds(r0, t), LANES:2 * LANES] = _bcast_lanes(lse[t:])
            return 0

        lax.fori_loop(0, nq, qblock, 0)

    return _grid_call(
        body, name=name, grid=(N_PAIRS,),
        in_specs=[_pair_spec(s), _pair_spec(s, N_PAIRS), _pair_spec(s, 2 * N_PAIRS), _cum_spec(nq, t)],
        out_specs=[_pair_spec(s), _stat_spec(s)],
        out_shape=[jax.ShapeDtypeStruct((s, BRANCH), F32), jax.ShapeDtypeStruct((s, N_HEADS * LANES), F32)],
        args=(qkv, qkv, qkv, cum4), semantics=("parallel",), exchange=exchange)


def _attn_c_bwd(qkv, cum4, o, lse, do, name, exchange=None):
    s = qkv.shape[0]
    t = _fox_tile(s)
    nq = s // t

    def body(q_ref, k_ref, v_ref, c_ref, o_ref, lse_ref, do_ref, dq_ref, dk_ref, dv_ref, dc_ref, dk_acc, dv_acc):
        first = _lane_is_first_head()
        causal = _stacked_mask(t, strict=False)
        eye = lax.broadcasted_iota(jnp.int32, (t, t), 0) == lax.broadcasted_iota(jnp.int32, (t, t), 1)
        dk_acc[...] = jnp.zeros_like(dk_acc)
        dv_acc[...] = jnp.zeros_like(dv_acc)
        dc_ref[...] = jnp.zeros_like(dc_ref)

        def tile(j, carry, diag, qs, dos, delta, lse):
            dq_acc, rs = carry
            c0 = pl.multiple_of(j * t, t)
            k2 = k_ref[pl.ds(c0, t), :]
            v2 = v_ref[pl.ds(c0, t), :]
            cs = c_ref[0, j]
            z = lax.dot_general(qs, k2, _NT, preferred_element_type=F32)
            sc = jnp.concatenate([z[:t] - cs[0:1, :], z[t:] - cs[1:2, :]], axis=0)
            p = jnp.exp(sc - lse)
            if diag:
                p = jnp.where(causal, p, 0.0)
            ds = p * (lax.dot_general(dos, v2, _NT, preferred_element_type=F32) - delta)
            dsb = ds.astype(BF16)
            dq = jnp.dot(dsb, k2, preferred_element_type=F32)
            dk_acc[pl.ds(c0, t), :] += lax.dot_general(dsb, qs, _TN, preferred_element_type=F32)
            dv_acc[pl.ds(c0, t), :] += lax.dot_general(p.astype(BF16), dos, _TN, preferred_element_type=F32)
            col_sums = jnp.concatenate([jnp.sum(ds[:t], axis=0, keepdims=True), jnp.sum(ds[t:], axis=0, keepdims=True)], axis=0)
            dc_ref[0, j] = dc_ref[0, j] - col_sums
            return dq_acc + jnp.where(first, dq[:t], dq[t:]), rs + jnp.sum(ds, axis=1, keepdims=True)

        def qblock(i, _):
            r0 = pl.multiple_of(i * t, t)
            do2 = do_ref[pl.ds(r0, t), :]
            qs = _stack_heads(q_ref[pl.ds(r0, t), :] * Q_SCALE, first)
            dos = _stack_heads(do2, first)
            delta = jnp.concatenate(_rowsum_heads(do2.astype(F32) * o_ref[pl.ds(r0, t), :], first), axis=0)
            lse = jnp.concatenate([lse_ref[pl.ds(r0, t), 0:1], lse_ref[pl.ds(r0, t), LANES:LANES + 1]], axis=0)
            carry = (jnp.zeros((t, LANES), F32), jnp.zeros((2 * t, 1), F32))
            carry = lax.fori_loop(0, i, lambda j, c: tile(j, c, False, qs, dos, delta, lse), carry)
            dq_acc, rs = tile(i, carry, True, qs, dos, delta, lse)
            dq_ref[pl.ds(r0, t), :] = (dq_acc * Q_SCALE).astype(BF16)
            as_row = lambda col_vec: jnp.sum(jnp.where(eye, col_vec, 0.0), axis=0, keepdims=True)
            dc_ref[0, i] = dc_ref[0, i] + jnp.concatenate([as_row(rs[:t]), as_row(rs[t:])], axis=0)
            return 0

        lax.fori_loop(0, nq, qblock, 0)
        dk_ref[...] = dk_acc[...].astype(BF16)
        dv_ref[...] = dv_acc[...].astype(BF16)

    out = jax.ShapeDtypeStruct((s, BRANCH), BF16)
    return _grid_call(
        body, name=name, grid=(N_PAIRS,),
        in_specs=[_pair_spec(s), _pair_spec(s, N_PAIRS), _pair_spec(s, 2 * N_PAIRS), _cum_spec(nq, t),
                  _pair_spec(s), _stat_spec(s), _pair_spec(s)],
        out_specs=[_pair_spec(s)] * 3 + [_cum_spec(nq, t)],
        out_shape=[out] * 3 + [jax.ShapeDtypeStruct(cum4.shape, F32)],
        scratch_shapes=[pltpu.VMEM((s, LANES), F32), pltpu.VMEM((s, LANES), F32)],
        args=(qkv, qkv, qkv, cum4, o, lse, do), semantics=("parallel",), exchange=exchange)


FG_CHUNK = 512


def _tri_dot3(x, t):
    hi = x.astype(BF16)
    r1 = x - hi.astype(F32)
    mid = r1.astype(BF16)
    lo = (r1 - mid.astype(F32)).astype(BF16)
    return (jnp.dot(hi, t, preferred_element_type=F32) + jnp.dot(mid, t, preferred_element_type=F32)
            + jnp.dot(lo, t, preferred_element_type=F32))


def _fgate_fwd(h, wf_t, b_col, name):
    s = h.shape[0]
    c = min(FG_CHUNK, s)

    def body(h_ref, w_ref, b_ref, xf_ref, cum_ref, carry_ref):
        @pl.when(pl.program_id(0) == 0)
        def _():
            carry_ref[...] = jnp.zeros_like(carry_ref)

        xf = lax.dot_general(w_ref[...], h_ref[...], _NT, preferred_element_type=F32) + b_ref[:, 0:1]
        xf_ref[...] = xf
        logf = jnp.minimum(xf, 0.0) - jnp.log(1.0 + jnp.exp(-jnp.abs(xf)))
        row = lax.broadcasted_iota(jnp.int32, (c, c), 0)
        col = lax.broadcasted_iota(jnp.int32, (c, c), 1)
        cum = _tri_dot3(logf, (row <= col).astype(BF16)) + carry_ref[:, 0:1]
        cum_ref[...] = cum
        carry_ref[...] = _bcast_lanes(cum[:, c - 1:c])

    out = jax.ShapeDtypeStruct((N_HEADS, s), F32)
    return pl.pallas_call(
        body, name=name, grid=(s // c,),
        in_specs=[pl.BlockSpec((c, D_MODEL), lambda i: (i, 0)),
                  pl.BlockSpec((N_HEADS, D_MODEL), lambda i: (0, 0)),
                  pl.BlockSpec((N_HEADS, LANES), lambda i: (0, 0))],
        out_specs=[pl.BlockSpec((N_HEADS, c), lambda i: (0, i))] * 2,
        out_shape=[out, out],
        scratch_shapes=[pltpu.VMEM((N_HEADS, LANES), F32)],
        compiler_params=_params(("arbitrary",)),
    )(h, wf_t, b_col)


def _fgate_bwd(dcum, xf, h, wf_t, name):
    s = h.shape[0]
    c = min(FG_CHUNK, s)
    n = s // c

    def body(dc_ref, xf_ref, h_ref, w_ref, dw_ref, dh_ref, db_ref, carry_ref):
        @pl.when(pl.program_id(0) == 0)
        def _():
            carry_ref[...] = jnp.zeros_like(carry_ref)
            dw_ref[...] = jnp.zeros_like(dw_ref)
            db_ref[...] = jnp.zeros_like(db_ref)

        row = lax.broadcasted_iota(jnp.int32, (c, c), 0)
        col = lax.broadcasted_iota(jnp.int32, (c, c), 1)
        dlogf = _tri_dot3(dc_ref[...], (row >= col).astype(BF16)) + carry_ref[:, 0:1]
        carry_ref[...] = _bcast_lanes(dlogf[:, 0:1])
        xf = xf_ref[...]
        e = jnp.exp(-jnp.abs(xf))
        r = 1.0 / (1.0 + e)
        dxf = dlogf * jnp.where(xf >= 0, e * r, r)
        db_ref[...] += _bcast_lanes(jnp.sum(dxf, axis=1, keepdims=True))
        dxb = dxf.astype(BF16)
        dw_ref[...] += jnp.dot(dxb, h_ref[...], preferred_element_type=F32)
        dh_ref[...] = lax.dot_general(dxb, w_ref[...], _TN, preferred_element_type=F32)

    rev = lambda i: n - 1 - i
    return pl.pallas_call(
        body, name=name, grid=(n,),
        in_specs=[pl.BlockSpec((N_HEADS, c), lambda i: (0, rev(i))),
                  pl.BlockSpec((N_HEADS, c), lambda i: (0, rev(i))),
                  pl.BlockSpec((c, D_MODEL), lambda i: (rev(i), 0)),
                  pl.BlockSpec((N_HEADS, D_MODEL), lambda i: (0, 0))],
        out_specs=[pl.BlockSpec((N_HEADS, D_MODEL), lambda i: (0, 0)),
                   pl.BlockSpec((c, D_MODEL), lambda i: (rev(i), 0)),
                   pl.BlockSpec((N_HEADS, LANES), lambda i: (0, 0))],
        out_shape=[jax.ShapeDtypeStruct((N_HEADS, D_MODEL), F32), jax.ShapeDtypeStruct((s, D_MODEL), F32),
                   jax.ShapeDtypeStruct((N_HEADS, LANES), F32)],
        scratch_shapes=[pltpu.VMEM((N_HEADS, LANES), F32)],
        compiler_params=_params(("arbitrary",)),
    )(dcum, xf, h, wf_t)


def _to_cum4(v, t):
    s = v.shape[1]
    return v.reshape(N_PAIRS, 2, s // t, t).transpose(0, 2, 1, 3)


def _from_cum4(v4):
    p, nt, two, t = v4.shape
    return v4.transpose(0, 2, 1, 3).reshape(p * two, nt * t)


def _alibi_slopes():
    return (2.0 ** (-8.0 * np.arange(1, N_HEADS + 1, dtype=np.float32) / N_HEADS)).astype(np.float32)


def _per_head_lanes(v):
    return jnp.repeat(v.astype(F32).reshape(N_PAIRS, 1, 2), LANES, axis=2)


def _attn_a_specs(s):
    q = _pair_spec(s)
    k = pl.BlockSpec((s, LANES), lambda p: (0, N_PAIRS + p // 8))
    v = pl.BlockSpec((s, LANES), lambda p: (0, N_PAIRS + KV_A // LANES + p // 8))
    head = pl.BlockSpec((1, 1, 2 * LANES), lambda p: (p, 0, 0))
    return q, k, v, head


def _attn_a_geometry(p, slope_ref, sink_ref):
    kv_half = (p // 4) % 2
    kv_first = kv_half == 0
    lane_first = _lane_is_first_head()
    kv_lanes = (lax.broadcasted_iota(jnp.int32, (1, LANES), 1) // HEAD_DIM) == kv_half
    row = lax.broadcasted_iota(jnp.int32, (2 * WINDOW, 2 * WINDOW), 0)
    cj = lax.broadcasted_iota(jnp.int32, (2 * WINDOW, 2 * WINDOW), 1)
    second = row >= WINDOW
    dist = WINDOW + jnp.where(second, row - WINDOW, row) - cj
    valid = (dist >= 0) & (dist < WINDOW)
    per_row = lambda ref: jnp.where(second[:, 0:1], ref[0, :, LANES:LANES + 1], ref[0, :, 0:1])
    return kv_first, lane_first, kv_lanes, per_row(slope_ref) * dist.astype(F32), valid, per_row(sink_ref)


def _swap_halves(x):
    return pltpu.roll(x, HEAD_DIM, 1)


def _attn_a_fwd(qkv, slopes, sinks, name, exchange=None):
    s = qkv.shape[0]
    nb = s // WINDOW

    def body(q_ref, k_ref, v_ref, sl_ref, sk_ref, o_ref, lse_ref):
        kv_first, lane_first, kv_lanes, bias, valid, sink = _attn_a_geometry(pl.program_id(0), sl_ref, sk_ref)

        def block(r0, k0, width):
            q2 = q_ref[pl.ds(r0, WINDOW), :].astype(F32) * Q_SCALE
            q2r = _swap_halves(q2)
            xs = jnp.concatenate([jnp.where(kv_first, q2, q2r), jnp.where(kv_first, q2r, q2)], axis=0).astype(BF16)
            km = jnp.where(kv_lanes, k_ref[pl.ds(k0, width), :], 0).astype(BF16)
            vm = jnp.where(kv_lanes, v_ref[pl.ds(k0, width), :], 0).astype(BF16)
            sc = lax.dot_general(xs, km, _NT, preferred_element_type=F32) - bias[:, 2 * WINDOW - width:]
            sc = jnp.where(valid[:, 2 * WINDOW - width:], sc, NEG)
            m = jnp.maximum(jnp.max(sc, axis=1, keepdims=True), sink)
            pr = jnp.exp(sc - m)
            l = jnp.sum(pr, axis=1, keepdims=True) + jnp.exp(sink - m)
            os = jnp.dot(pr.astype(BF16), vm, preferred_element_type=F32) * (1.0 / l)
            lse = m + jnp.log(l)
            lse_ref[pl.ds(r0, WINDOW), 0:LANES] = _bcast_lanes(lse[:WINDOW])
            lse_ref[pl.ds(r0, WINDOW), LANES:2 * LANES] = _bcast_lanes(lse[WINDOW:])
            oa = jnp.where(kv_first, os[:WINDOW], _swap_halves(os[:WINDOW]))
            ob = jnp.where(kv_first, _swap_halves(os[WINDOW:]), os[WINDOW:])
            o_ref[pl.ds(r0, WINDOW), :] = jnp.where(lane_first, oa, ob)

        block(0, 0, WINDOW)

        def loop(n, _):
            r0 = pl.multiple_of(n * WINDOW, WINDOW)
            block(r0, pl.multiple_of(r0 - WINDOW, WINDOW), 2 * WINDOW)
            return 0

        lax.fori_loop(1, nb, loop, 0)

    q, k, v, head = _attn_a_specs(s)
    return _grid_call(
        body, name=name, grid=(N_PAIRS,),
        in_specs=[q, k, v, head, head],
        out_specs=[_pair_spec(s), _stat_spec(s)],
        out_shape=[jax.ShapeDtypeStruct((s, BRANCH), F32), jax.ShapeDtypeStruct((s, N_HEADS * LANES), F32)],
        args=(qkv, qkv, qkv, slopes, sinks), semantics=("parallel",), exchange=exchange)


def _attn_a_bwd(qkv, slopes, sinks, o, lse, do, name, exchange=None):
    s = qkv.shape[0]
    nb = s // WINDOW

    def body(q_ref, k_ref, v_ref, sl_ref, sk_ref, o_ref, lse_ref, do_ref, dq_ref, dk_ref, dv_ref, dsk_ref):
        p_id = pl.program_id(0)
        kv_first, lane_first, kv_lanes, bias, valid, sink = _attn_a_geometry(p_id, sl_ref, sk_ref)

        @pl.when(p_id % 8 == 0)
        def _():
            dk_ref[...] = jnp.zeros_like(dk_ref)
            dv_ref[...] = jnp.zeros_like(dv_ref)

        def align(v2):
            v2r = _swap_halves(v2)
            both = jnp.concatenate([jnp.where(kv_first, v2, v2r), jnp.where(kv_first, v2r, v2)], axis=0)
            return jnp.where(kv_lanes, both, 0.0).astype(BF16)

        def block(r0, k0, width, sink_sum):
            xq = align(q_ref[pl.ds(r0, WINDOW), :].astype(F32) * Q_SCALE)
            do2 = do_ref[pl.ds(r0, WINDOW), :].astype(F32)
            xdo = align(do2)
            delta = jnp.concatenate(_rowsum_heads(do2 * o_ref[pl.ds(r0, WINDOW), :], lane_first), axis=0)
            lse = jnp.concatenate([lse_ref[pl.ds(r0, WINDOW), 0:1], lse_ref[pl.ds(r0, WINDOW), LANES:LANES + 1]], axis=0)
            km = jnp.where(kv_lanes, k_ref[pl.ds(k0, width), :], 0).astype(BF16)
            vm = jnp.where(kv_lanes, v_ref[pl.ds(k0, width), :], 0).astype(BF16)
            sc = lax.dot_general(xq, km, _NT, preferred_element_type=F32) - bias[:, 2 * WINDOW - width:]
            pr = jnp.where(valid[:, 2 * WINDOW - width:], jnp.exp(sc - lse), 0.0)
            ds = pr * (lax.dot_general(xdo, vm, _NT, preferred_element_type=F32) - delta)
            dsb = ds.astype(BF16)
            dq_al = jnp.dot(dsb, km, preferred_element_type=F32)
            dk_ref[pl.ds(k0, width), :] += lax.dot_general(dsb, xq, _TN, preferred_element_type=F32)
            dv_ref[pl.ds(k0, width), :] += lax.dot_general(pr.astype(BF16), xdo, _TN, preferred_element_type=F32)
            dqa = jnp.where(kv_first, dq_al[:WINDOW], _swap_halves(dq_al[:WINDOW]))
            dqb = jnp.where(kv_first, _swap_halves(dq_al[WINDOW:]), dq_al[WINDOW:])
            dq_ref[pl.ds(r0, WINDOW), :] = (jnp.where(lane_first, dqa, dqb) * Q_SCALE).astype(BF16)
            return sink_sum + jnp.exp(sink - lse) * delta

        sink_sum = block(0, 0, WINDOW, jnp.zeros((2 * WINDOW, 1), F32))

        def loop(n, c):
            r0 = pl.multiple_of(n * WINDOW, WINDOW)
            return block(r0, pl.multiple_of(r0 - WINDOW, WINDOW), 2 * WINDOW, c)

        sink_sum = lax.fori_loop(1, nb, loop, sink_sum)
        dsk_ref[0, :, 0:LANES] = jnp.broadcast_to(-jnp.sum(sink_sum[:WINDOW], axis=0, keepdims=True), (1, LANES))
        dsk_ref[0, :, LANES:2 * LANES] = jnp.broadcast_to(-jnp.sum(sink_sum[WINDOW:], axis=0, keepdims=True), (1, LANES))

    q, k, v, head = _attn_a_specs(s)
    kv_out = pl.BlockSpec((s, LANES), lambda p: (0, p // 8))
    return _grid_call(
        body, name=name, grid=(N_PAIRS,),
        in_specs=[q, k, v, head, head, _pair_spec(s), _stat_spec(s), _pair_spec(s)],
        out_specs=[_pair_spec(s), kv_out, kv_out, head],
        out_shape=[jax.ShapeDtypeStruct((s, BRANCH), BF16), jax.ShapeDtypeStruct((s, KV_A), F32),
                   jax.ShapeDtypeStruct((s, KV_A), F32), jax.ShapeDtypeStruct((N_PAIRS, 1, 2 * LANES), F32)],
        args=(qkv, qkv, qkv, slopes, sinks, o, lse, do), semantics=("arbitrary",), exchange=exchange)


def _layer_kind(i):
    return i % 3, i // 3


GATHER_FIRST = [("in", 0)]
GATHER_BEHIND = {0: [("out", 0), ("in", 1)], 1: [("out", 1), ("in", 2), ("out", 2)], 2: [("in", 3), ("out", 3)]}


def _forward_backward(x, target, g_pre, g_post, sinks_a, b_f_c, shards, chip, core):
    s = x.shape[0]
    slopes = _per_head_lanes(jnp.asarray(_alibi_slopes()))
    w_in, w_out, wf_t = {}, {}, {}

    def deliver(keys, gathered):
        for (side, layer), g in zip(keys, gathered):
            sh = shards[(side, layer)]
            g = lax.dynamic_update_slice(g, sh[None], (chip, 0, 0))
            if side == "out":
                w_out[layer] = g.reshape(4 * sh.shape[0], sh.shape[1])
                continue
            w = g.transpose(1, 0, 2).reshape(sh.shape[0], 4 * sh.shape[1])
            if _layer_kind(layer)[0] == 2:
                w, wf_t[layer] = w[:, :4 * BRANCH], w[:, 4 * BRANCH:].T
            w_in[layer] = w

    deliver(GATHER_FIRST, _exchange_call(_GatherExchange([shards[k] for k in GATHER_FIRST]), "gather_first_weights"))
    saved = []
    for i in range(DEPTH):
        kind, j = _layer_kind(i)
        tag = f"l{i}"
        w = w_in[i]
        nqkv = A_QKV if kind == 0 else B_QKV
        tn = 512 if kind == 0 else 1024
        h = _rmsnorm_fwd(x, g_pre[i:i + 1], f"prenorm_{tag}")
        qkv = _matmul(h, w, mode="nn", out_dtype=BF16, name=f"inproj_qkv_{tag}", n=nqkv, tn=tn)
        z = _matmul(h, w, mode="nn", out_dtype=F32, name=f"inproj_gate_{tag}", n=BRANCH, b_off=nqkv // tn, tn=tn)
        behind = GATHER_BEHIND.get(i)
        exchange = _GatherExchange([shards[k] for k in behind]) if behind else None
        if kind == 0:
            sink_l = _per_head_lanes(sinks_a[j])
            (o, lse), arrived = _attn_a_fwd(qkv, slopes, sink_l, f"attn_a_fwd_{tag}", exchange)
            extra = (sink_l, lse)
        elif kind == 1:
            (o, extra), arrived = _attn_b_fwd(qkv, f"attn_b_fwd_{tag}", exchange)
        else:
            b_col = jnp.broadcast_to(b_f_c[j].astype(F32)[:, None], (N_HEADS, LANES))
            xf, cum = _fgate_fwd(h, wf_t[i], b_col, f"fgate_fwd_{tag}")
            cum4 = _to_cum4(cum, _fox_tile(s))
            (o, lse), arrived = _attn_c_fwd(qkv, cum4, f"attn_c_fwd_{tag}", exchange)
            extra = (xf, cum4, lse)
        if behind:
            deliver(behind, arrived)
        u = _gate_fwd(o, z, f"gate_{tag}")
        y = _matmul(u, w_out[i], mode="nn", out_dtype=F32, name=f"outproj_{tag}")
        saved.append((x, h, qkv, z, o, u, y, extra))
        x = _post_fwd(x, y, g_post[i:i + 1], f"postnorm_{tag}")

    dx, loss_part = _loss_and_grad(x, target)

    d_g_pre, d_g_post = [None] * DEPTH, [None] * DEPTH
    d_sinks = [None, None]
    d_b_f = None
    reduced = {}
    pending = None

    def finish_reduce(layer, sums, arrived):
        kind, j = _layer_kind(layer)
        for side, own, arr in zip(("in", "out"), sums, arrived):
            reduced[(side, kind)] = _sum_chips(own, arr, core, f"shard_sum_{side}_l{layer}", j, 2 if kind == 0 else 1,
                                               into=reduced.get((side, kind)))

    for i in reversed(range(DEPTH)):
        kind, j = _layer_kind(i)
        tag = f"l{i}"
        x_in, h, qkv, z, o, u, y, extra = saved[i]
        tn = 512 if kind == 0 else 1024
        dy, d_g_post[i] = _post_bwd(dx, y, g_post[i:i + 1], f"postnorm_bwd_{tag}")
        dw_out = _matmul(u, dy, mode="tn", out_dtype=F32, name=f"dw_out_{tag}", tk=512)
        du = _matmul(dy, w_out[i], mode="nt", out_dtype=F32, name=f"d_gated_{tag}")
        do, dz = _gate_bwd(du, o, z, f"gate_bwd_{tag}")
        dhs = []
        exchange = _ScatterExchange(pending[1]) if pending else None
        if kind == 0:
            sink_l, lse = extra
            (dq, dk, dv, dsk), arrived = _attn_a_bwd(qkv, slopes, sink_l, o, lse, do, f"attn_a_bwd_{tag}", exchange)
            d_sinks[j] = dsk[:, 0, ::LANES].reshape(N_HEADS)
            parts = [dq, dk.astype(BF16), dv.astype(BF16), dz]
        elif kind == 1:
            (dq, dk, dv), arrived = _attn_b_bwd(qkv, extra, do, f"attn_b_bwd_{tag}", exchange)
            parts = [dq, dk, dv, dz]
        else:
            xf, cum4, lse = extra
            (dq, dk, dv, dcum4), arrived = _attn_c_bwd(qkv, cum4, o, lse, do, f"attn_c_bwd_{tag}", exchange)
            d_wf_t, dh_f, db = _fgate_bwd(_from_cum4(dcum4), xf, h, wf_t[i], f"fgate_bwd_{tag}")
            d_b_f = db[:, 0]
            dhs.append(dh_f)
            parts = [dq, dk, dv, dz]
        if pending:
            finish_reduce(pending[0], pending[1], arrived)
        dproj = jnp.concatenate(parts, axis=1)
        dw_in = _matmul(h, dproj, mode="tn", out_dtype=F32, name=f"dw_in_{tag}", tk=512, tn=tn)
        if kind == 2:
            dw_in = jnp.concatenate([dw_in, d_wf_t.T], axis=1)
        dhs.insert(0, _matmul(dproj, w_in[i], mode="nt", out_dtype=F32, name=f"dh_{tag}", tk=512))
        dx, d_g_pre[i] = _pre_bwd(dx, dhs, x_in, g_pre[i:i + 1], f"prenorm_bwd_{tag}")

        to_chips = [dw_in.reshape(dw_in.shape[0], 4, dw_in.shape[1] // 4).transpose(1, 0, 2).astype(BF16),
                    dw_out.reshape(4, dw_out.shape[0] // 4, dw_out.shape[1]).astype(BF16)]
        theirs = _sibling_send(to_chips, f"grad_sibling_exchange_{tag}")
        pending = (i, [_add_pairs(a, b, f"chip_sum_{side}_{tag}") for a, b, side in zip(to_chips, theirs, ("in", "out"))])

    arrived = _exchange_call(_ScatterExchange(pending[1]), "grad_chip_scatter_last")
    finish_reduce(pending[0], pending[1], arrived)

    return dict(loss=loss_part, dx=dx, g_pre=jnp.concatenate(d_g_pre, axis=0), g_post=jnp.concatenate(d_g_post, axis=0),
                sinks_a=jnp.stack(d_sinks), b_f_c=d_b_f[None, :], reduced=reduced)


def _place():
    x, y, c = lax.axis_index("x"), lax.axis_index("y"), lax.axis_index("c")
    others = [(1 - x, y), (x, 1 - y), (1 - x, 1 - y)]
    return x, y, c, others


def _half_rows(ref_rows, which):
    half = ref_rows // 2
    return pl.ds(pl.multiple_of(which * half, half), half)


def _remote(src, dst, sems, k, device):
    send, recv = sems
    return pltpu.make_async_remote_copy(src_ref=src, dst_ref=dst, send_sem=send.at[k], recv_sem=recv.at[k],
                                        device_id=device, device_id_type=MESH)


def _hbm_call(body, name, ins, out_shapes, n_remote, aliases=None):
    any_spec = pl.BlockSpec(memory_space=pl.ANY)
    return pl.pallas_call(
        body, name=name, in_specs=[any_spec] * len(ins), out_specs=[any_spec] * len(out_shapes),
        out_shape=out_shapes, input_output_aliases=aliases or {},
        scratch_shapes=[pltpu.SemaphoreType.DMA((n_remote,)), pltpu.SemaphoreType.DMA((n_remote,))],
    )(*ins)


class _GatherExchange:
    def __init__(self, shards):
        self.ins = list(shards)
        self.out_shapes = [jax.ShapeDtypeStruct((4,) + a.shape, a.dtype) for a in shards]
        self.n_sems = 6 * len(shards)
        self.aliases = {}

    def _copies(self, ins, outs, sems):
        x, y, c, others = _place()
        me = 2 * x + y
        table = []
        for w, (src, dst) in enumerate(zip(ins, outs)):
            mine, theirs = _half_rows(src.shape[0], c), _half_rows(src.shape[0], 1 - c)
            for j, (px, py) in enumerate(others):
                there = 2 * px + py
                send = _remote(src.at[mine], dst.at[me, mine], sems, 6 * w + j, (px, py, c))
                landed = _remote(dst.at[there, mine], dst.at[there, mine], sems, 6 * w + j, (px, py, c))
                passed = _remote(dst.at[there, mine], dst.at[there, mine], sems, 6 * w + 3 + j, (x, y, 1 - c))
                from_sibling = _remote(dst.at[there, theirs], dst.at[there, theirs], sems, 6 * w + 3 + j, (x, y, 1 - c))
                table.append((send, landed, passed, from_sibling))
        return table

    def start(self, ins, outs, sems):
        for send, _, _, _ in self._copies(ins, outs, sems):
            send.start()

    def mid(self, ins, outs, sems):
        for _, landed, passed, _ in self._copies(ins, outs, sems):
            landed.wait_recv()
            passed.start()

    def finish(self, ins, outs, sems):
        table = self._copies(ins, outs, sems)
        for _, _, _, from_sibling in table:
            from_sibling.wait_recv()
        for send, _, passed, _ in table:
            send.wait_send()
            passed.wait_send()


def _exchange_call(ex, name):
    n_in, n_out = len(ex.ins), len(ex.out_shapes)

    def body(*refs):
        ins, outs, sems = refs[:n_in], refs[n_in:n_in + n_out], refs[n_in + n_out:]
        ex.start(ins, outs, sems)
        ex.mid(ins, outs, sems)
        ex.finish(ins, outs, sems)

    return _hbm_call(body, name, ex.ins, ex.out_shapes, ex.n_sems, aliases=ex.aliases)


def _grid_call(body, *, name, grid, in_specs, out_specs, out_shape, args, scratch_shapes=(), semantics, exchange=None):
    if exchange is None:
        res = pl.pallas_call(body, name=name, grid=grid, in_specs=list(in_specs), out_specs=list(out_specs),
                             out_shape=list(out_shape), scratch_shapes=list(scratch_shapes),
                             compiler_params=_params(semantics))(*args)
        return res, []
    n_in, n_out, n_scr = len(args), len(out_shape), len(scratch_shapes)
    x_in, x_out = len(exchange.ins), len(exchange.out_shapes)
    steps = grid[0]

    def wrapped(*refs):
        core_in, ex_in = refs[:n_in], refs[n_in:n_in + x_in]
        rest = refs[n_in + x_in:]
        core_out, ex_out = rest[:n_out], rest[n_out:n_out + x_out]
        scratch, sems = rest[n_out + x_out:n_out + x_out + n_scr], rest[n_out + x_out + n_scr:]
        step = pl.program_id(0)

        @pl.when(step == 0)
        def _():
            exchange.start(ex_in, ex_out, sems)

        body(*core_in, *core_out, *scratch)

        @pl.when(step == (3 * steps) // 4 - 1)
        def _():
            exchange.mid(ex_in, ex_out, sems)

        @pl.when(step == steps - 1)
        def _():
            exchange.finish(ex_in, ex_out, sems)

    any_spec = pl.BlockSpec(memory_space=pl.ANY)
    res = pl.pallas_call(
        wrapped, name=name, grid=grid,
        in_specs=list(in_specs) + [any_spec] * x_in, out_specs=list(out_specs) + [any_spec] * x_out,
        out_shape=list(out_shape) + list(exchange.out_shapes),
        input_output_aliases={n_in + a: n_out + b for a, b in exchange.aliases.items()},
        scratch_shapes=list(scratch_shapes) + [pltpu.SemaphoreType.DMA((exchange.n_sems,)),
                                               pltpu.SemaphoreType.DMA((exchange.n_sems,))],
        compiler_params=_params(("arbitrary",)),
    )(*args, *exchange.ins)
    return res[:n_out], res[n_out:]


def _sibling_send(parts, name):
    n = len(parts)

    def body(*refs):
        ins, outs, sems = refs[:n], refs[n:2 * n], refs[2 * n:2 * n + 2]
        x, y, c, _ = _place()
        pend = []
        for w in range(n):
            cp = _remote(ins[w].at[:, _half_rows(ins[w].shape[1], 1 - c)], outs[w], sems, w, (x, y, 1 - c))
            cp.start()
            pend.append(cp)
        for cp in pend:
            cp.wait_recv()
            cp.wait_send()

    out_shapes = [jax.ShapeDtypeStruct((4, a.shape[1] // 2, a.shape[2]), a.dtype) for a in parts]
    return _hbm_call(body, name, parts, out_shapes, n)


class _ScatterExchange:
    def __init__(self, sums):
        self.ins = list(sums)
        self.out_shapes = [jax.ShapeDtypeStruct(a.shape, a.dtype) for a in sums]
        self.n_sems = 3 * len(sums)
        self.aliases = {}

    def _copies(self, ins, outs, sems):
        x, y, c, others = _place()
        me = 2 * x + y
        table = []
        for w, (src, dst) in enumerate(zip(ins, outs)):
            for j, (px, py) in enumerate(others):
                there = 2 * px + py
                send = _remote(src.at[there], dst.at[me], sems, 3 * w + j, (px, py, c))
                landed = _remote(dst.at[there], dst.at[there], sems, 3 * w + j, (px, py, c))
                table.append((send, landed))
        return table

    def start(self, ins, outs, sems):
        for send, _ in self._copies(ins, outs, sems):
            send.start()

    def mid(self, ins, outs, sems):
        pass

    def finish(self, ins, outs, sems):
        table = self._copies(ins, outs, sems)
        for _, landed in table:
            landed.wait_recv()
        for send, _ in table:
            send.wait_send()


def _sibling_join(shards):
    n = len(shards)

    def body(*refs):
        ins, outs, sems = refs[:n], refs[n:2 * n], refs[2 * n:2 * n + 2]
        x, y, c, _ = _place()
        pend = []
        for w in range(n):
            rows = ins[w].shape[1]
            mine, theirs = _half_rows(rows, c), _half_rows(rows, 1 - c)
            cp = _remote(ins[w].at[:, mine], outs[w].at[:, mine], sems, w, (x, y, 1 - c))
            cp.start()
            pend.append((cp, _remote(ins[w].at[:, theirs], outs[w].at[:, theirs], sems, w, (x, y, 1 - c))))
        for cp, landed in pend:
            landed.wait_recv()
            cp.wait_send()

    out_shapes = [jax.ShapeDtypeStruct(a.shape, a.dtype) for a in shards]
    return _hbm_call(body, "grad_sibling_join", shards, out_shapes, n, aliases={w: w for w in range(n)})


SMALL_ROWS = 136


def _all_reduce_small(vec):
    def body(v_ref, o_ref, buf, send, recv, loc):
        x, y, c, _ = _place()
        me = 4 * x + 2 * y + c
        lc = pltpu.make_async_copy(v_ref, buf.at[me], loc.at[0])
        lc.start()
        cps = []
        for k in range(1, 8):
            fx, fy, fc = (k >> 2) & 1, (k >> 1) & 1, k & 1
            peer = (x ^ fx, y ^ fy, c ^ fc)
            cp = pltpu.make_async_remote_copy(src_ref=v_ref, dst_ref=buf.at[me], send_sem=send.at[k - 1],
                                              recv_sem=recv.at[k - 1], device_id=peer, device_id_type=MESH)
            cp.start()
            cps.append((cp, 4 * peer[0] + 2 * peer[1] + peer[2]))
        for k, (cp, src) in enumerate(cps):
            pltpu.make_async_remote_copy(src_ref=v_ref, dst_ref=buf.at[src], send_sem=send.at[k], recv_sem=recv.at[k],
                                         device_id=(x, y, c), device_id_type=MESH).wait_recv()
        for cp, _ in cps:
            cp.wait_send()
        lc.wait()
        total = buf[0]
        for k in range(1, 8):
            total = total + buf[k]
        o_ref[...] = total

    vm = pl.BlockSpec(memory_space=pltpu.VMEM)
    return pl.pallas_call(
        body, name="all_reduce_small", in_specs=[vm], out_specs=vm,
        out_shape=jax.ShapeDtypeStruct(vec.shape, F32),
        scratch_shapes=[pltpu.VMEM((8,) + vec.shape, F32), pltpu.SemaphoreType.DMA((7,)),
                        pltpu.SemaphoreType.DMA((7,)), pltpu.SemaphoreType.DMA((1,))],
    )(vec)


SUM_ROWS = 256


def _add_pairs(part, theirs, name):
    four, rh, cc = theirs.shape
    tr = min(SUM_ROWS, rh)
    halves = part.reshape(four, 2, rh, cc)

    def body(a_ref, b_ref, o_ref):
        mine = a_ref[0, lax.axis_index("c")]
        o_ref[0] = (mine.astype(F32) + b_ref[0].astype(F32)).astype(o_ref.dtype)

    spec = pl.BlockSpec((1, tr, cc), lambda k, r: (k, r, 0))
    return pl.pallas_call(
        body, name=name, grid=(four, rh // tr),
        in_specs=[pl.BlockSpec((1, 2, tr, cc), lambda k, r: (k, 0, r, 0)), spec], out_specs=spec,
        out_shape=jax.ShapeDtypeStruct(theirs.shape, theirs.dtype),
        compiler_params=_params(("parallel", "parallel")),
    )(halves, theirs)


def _sum_chips(own, arrived, core, name, layer, n_layers, into=None):
    four, rh, cc = own.shape
    tr = min(SUM_ROWS, rh)
    nr = rh // tr

    def body(c_ref, own_ref, arr_ref, *rest):
        o_ref = rest[-1]
        x, y = lax.axis_index("x"), lax.axis_index("y")
        tot = own_ref[2 * x + y].astype(F32)
        for px, py in ((1 - x, y), (x, 1 - y), (1 - x, 1 - y)):
            tot = tot + arr_ref[2 * px + py].astype(F32)
        o_ref[0] = tot

    blk = pl.BlockSpec((4, tr, cc), lambda r, c_ref: (0, r, 0))
    in_specs, args, aliases = [blk, blk], [core, own, arrived], {}
    if into is not None:
        in_specs.append(pl.BlockSpec(memory_space=pl.ANY))
        args.append(into)
        aliases = {3: 0}
    return pl.pallas_call(
        body, name=name,
        grid_spec=pltpu.PrefetchScalarGridSpec(
            num_scalar_prefetch=1, grid=(nr,), in_specs=in_specs,
            out_specs=pl.BlockSpec((1, tr, cc), lambda r, c_ref: (layer, c_ref[0] * nr + r, 0))),
        out_shape=jax.ShapeDtypeStruct((n_layers, 2 * rh, cc), F32), input_output_aliases=aliases,
        compiler_params=_params(("parallel",)),
    )(*args)


ADAM_ROWS = 256


def _adamw(w, g, m, v, name):
    shape = w.shape
    cc = shape[-1]
    flat = lambda a: a.reshape(-1, cc)
    rows = flat(w).shape[0]
    tr = min(ADAM_ROWS, rows)
    assert rows % tr == 0
    c1 = 1.0 - ADAM_B1 ** ADAM_STEP
    c2 = 1.0 - ADAM_B2 ** ADAM_STEP

    def body(w_ref, g_ref, m_ref, v_ref, d_ref, nm_ref, nv_ref):
        gv = g_ref[...]
        nm = ADAM_B1 * m_ref[...] + (1.0 - ADAM_B1) * gv
        nv = ADAM_B2 * v_ref[...] + (1.0 - ADAM_B2) * (gv * gv)
        nm_ref[...] = nm
        nv_ref[...] = nv
        d_ref[...] = -ADAM_LR * ((nm / c1) / (jnp.sqrt(nv / c2) + ADAM_EPS) + ADAM_WD * w_ref[...])

    spec = pl.BlockSpec((tr, cc), lambda i: (i, 0))
    sh = jax.ShapeDtypeStruct((rows, cc), F32)
    outs = pl.pallas_call(
        body, name=name, grid=(rows // tr,), in_specs=[spec] * 4, out_specs=[spec] * 3, out_shape=[sh] * 3,
        compiler_params=_params(("parallel",)),
    )(flat(w), flat(g), flat(m), flat(v))
    return [o.reshape(shape) for o in outs]


def _pack_small(g_pre, g_post, sinks_a, b_f_c, loss_row):
    pad = lambda a: jnp.pad(a.reshape(1, -1).astype(F32), ((0, 0), (0, LANES - a.size)))
    rows = [g_pre.astype(F32).reshape(-1, LANES), g_post.astype(F32).reshape(-1, LANES), pad(sinks_a), pad(b_f_c), loss_row]
    packed = jnp.concatenate(rows, axis=0)
    return jnp.pad(packed, ((0, SMALL_ROWS - packed.shape[0]), (0, 0)))


def _unpack_small(p):
    n = DEPTH * D_MODEL // LANES
    return (p[:n].reshape(DEPTH, D_MODEL), p[n:2 * n].reshape(DEPTH, D_MODEL), p[2 * n, :2 * N_HEADS].reshape(2, N_HEADS),
            p[2 * n + 1, :N_HEADS].reshape(1, N_HEADS), p[2 * n + 2, 0])


def kernel(x, g_pre, g_post, w_in_a, w_out_a, sinks_a, w_in_b, w_out_b, w_in_c, b_f_c, w_out_c, loss_target, m_g_pre, m_g_post, m_w_in_a, m_w_out_a, m_sinks_a, m_w_in_b, m_w_out_b, m_w_in_c, m_b_f_c, m_w_out_c, v_g_pre, v_g_post, v_w_in_a, v_w_out_a, v_sinks_a, v_w_in_b, v_w_out_b, v_w_in_c, v_b_f_c, v_w_out_c):
    big_w = [w_in_a, w_out_a, w_in_b, w_out_b, w_in_c, w_out_c]
    big_m = [m_w_in_a, m_w_out_a, m_w_in_b, m_w_out_b, m_w_in_c, m_w_out_c]
    big_v = [v_w_in_a, v_w_out_a, v_w_in_b, v_w_out_b, v_w_in_c, v_w_out_c]

    chip = 2 * lax.axis_index("x") + lax.axis_index("y")
    core = lax.axis_index("c").astype(jnp.int32).reshape(1)
    by_kind = {0: (w_in_a, w_out_a), 1: (w_in_b, w_out_b), 2: (w_in_c, w_out_c)}
    shards = {}
    for i in range(DEPTH):
        kind, j = _layer_kind(i)
        shards[("in", i)] = by_kind[kind][0][j].astype(BF16)
        shards[("out", i)] = by_kind[kind][1][j].astype(BF16)

    res = _forward_backward(x[0], loss_target[0], g_pre, g_post, sinks_a, b_f_c, shards, chip, core)
    names = ["w_in_a", "w_out_a", "w_in_b", "w_out_b", "w_in_c", "w_out_c"]
    grads = _sibling_join([res["reduced"][(side, kind)] for kind in range(3) for side in ("in", "out")])

    small = _unpack_small(_all_reduce_small(
        _pack_small(res["g_pre"], res["g_post"], res["sinks_a"], res["b_f_c"], res["loss"])))
    g_small, loss = small[:4], small[4]

    zero_row = jnp.zeros((1, LANES), F32)
    pk = lambda a: _pack_small(a[0], a[1], a[2], a[3], zero_row)
    sm = _adamw(pk([g_pre, g_post, sinks_a, b_f_c]), pk(g_small), pk([m_g_pre, m_g_post, m_sinks_a, m_b_f_c]),
                pk([v_g_pre, v_g_post, v_sinks_a, v_b_f_c]), "adamw_small")
    sm = [_unpack_small(a)[:4] for a in sm]
    bigs = [_adamw(w, g, m, v, f"adamw_{nm}") for w, g, m, v, nm in zip(big_w, grads, big_m, big_v, names)]

    def ordered(small4, big6):
        return [small4[0], small4[1], big6[0], big6[1], small4[2], big6[2], big6[3], big6[4], small4[3], big6[5]]

    out = [loss, res["dx"][None], *ordered(g_small, grads)]
    for k in range(3):
        out += ordered(sm[k], [b[k] for b in bigs])
    return tuple(out)
```

```python
import functools
import math

import numpy as np
import jax
import jax.numpy as jnp
from jax import lax
from jax.experimental import pallas as pl
from jax.experimental.pallas import tpu as pltpu

F32 = jnp.float32
BF16 = jnp.bfloat16

D_MODEL = 2048
DEPTH = 4
N_HEADS = 32
HEAD_DIM = 64
LANES = 128
N_PAIRS = N_HEADS * HEAD_DIM // LANES
BRANCH = N_HEADS * HEAD_DIM
N_KV_A = 4
KV_A = N_KV_A * HEAD_DIM
WINDOW = 128
NORM_EPS = 1e-6
NEG = -1e30
Q_SCALE = HEAD_DIM ** -0.5

A_QKV = BRANCH + 2 * KV_A
B_QKV = 3 * BRANCH

ADAM_LR = 0.001
ADAM_B1 = 0.9
ADAM_B2 = 0.999
ADAM_EPS = 1e-08
ADAM_WD = 0.01
ADAM_STEP = 10

MESH = pl.DeviceIdType.MESH

_NT = (((1,), (1,)), ((), ()))
_TN = (((0,), (0,)), ((), ()))


def _params(sem=None):
    return pltpu.CompilerParams(dimension_semantics=sem)


def _matmul(a, b, *, mode, out_dtype, name, n=None, b_off=0, tm=1024, tn=1024, tk=2048):
    if mode == "nn":
        (m, k), nn = a.shape, (n or b.shape[1])
    elif mode == "nt":
        (m, k), nn = a.shape, b.shape[0]
    else:
        (k, m), nn = a.shape, b.shape[1]
    tm, tn, tk = min(tm, m), min(tn, nn), min(tk, k)
    assert m % tm == 0 and nn % tn == 0 and k % tk == 0, (name, m, nn, k, tm, tn, tk)
    nk = k // tk

    def body(a_ref, b_ref, o_ref, acc_ref):
        kk = pl.program_id(2)
        if mode == "nn":
            p = jnp.dot(a_ref[...], b_ref[...], preferred_element_type=F32)
        elif mode == "nt":
            p = lax.dot_general(a_ref[...], b_ref[...], _NT, preferred_element_type=F32)
        else:
            p = lax.dot_general(a_ref[...], b_ref[...], _TN, preferred_element_type=F32)
        if nk == 1:
            o_ref[...] = p.astype(o_ref.dtype)
        else:
            @pl.when(kk == 0)
            def _():
                acc_ref[...] = p

            @pl.when(kk > 0)
            def _():
                acc_ref[...] += p

            @pl.when(kk == nk - 1)
            def _():
                o_ref[...] = acc_ref[...].astype(o_ref.dtype)

    if mode == "nn":
        in_specs = [pl.BlockSpec((tm, tk), lambda i, j, kk: (i, kk)),
                    pl.BlockSpec((tk, tn), lambda i, j, kk: (kk, j + b_off))]
    elif mode == "nt":
        in_specs = [pl.BlockSpec((tm, tk), lambda i, j, kk: (i, kk)),
                    pl.BlockSpec((tn, tk), lambda i, j, kk: (j, kk))]
    else:
        in_specs = [pl.BlockSpec((tk, tm), lambda i, j, kk: (kk, i)),
                    pl.BlockSpec((tk, tn), lambda i, j, kk: (kk, j))]
    return pl.pallas_call(
        body, name=name, grid=(m // tm, nn // tn, nk),
        in_specs=in_specs,
        out_specs=pl.BlockSpec((tm, tn), lambda i, j, kk: (i, j)),
        out_shape=jax.ShapeDtypeStruct((m, nn), out_dtype),
        scratch_shapes=[pltpu.VMEM((tm, tn), F32)],
        compiler_params=_params(("parallel", "parallel", "arbitrary")),
    )(a, b)


ROW_TILE = 256


def _row_call(body, name, ins, outs, *, s):
    tr = min(ROW_TILE, s)
    spec = {"row": lambda sh: pl.BlockSpec((tr, sh[1]), lambda i: (i, 0)),
            "vec": lambda sh: pl.BlockSpec((1, sh[1]), lambda i: (0, 0)),
            "col": lambda sh: pl.BlockSpec((sh[0], tr), lambda i: (0, i))}
    in_specs = [spec[kind](a.shape) for a, kind in ins]
    out_specs = [spec[kind](sh.shape) for sh, kind in outs]
    return pl.pallas_call(
        body, name=name, grid=(s // tr,), in_specs=in_specs, out_specs=out_specs,
        out_shape=[sh for sh, _ in outs],
        compiler_params=_params(("arbitrary",)),
    )(*[a for a, _ in ins])


def _rsqrt_ms(v):
    return lax.rsqrt(jnp.mean(v * v, axis=-1, keepdims=True) + NORM_EPS)


def _rmsnorm_fwd(x, g, name):
    s, d = x.shape

    def body(x_ref, g_ref, h_ref, ht_ref):
        xv = x_ref[...]
        h = xv * _rsqrt_ms(xv) * g_ref[...]
        h_ref[...] = h.astype(BF16)
        ht_ref[...] = h.T.astype(BF16)

    return _row_call(body, name, [(x, "row"), (g, "vec")],
                     [(jax.ShapeDtypeStruct((s, d), BF16), "row"), (jax.ShapeDtypeStruct((d, s), BF16), "col")], s=s)


def _gate_fwd(o, z, name):
    s, d = o.shape

    def body(o_ref, z_ref, u_ref, ut_ref):
        zv = z_ref[...]
        u = o_ref[...] * (zv * jax.nn.sigmoid(zv))
        u_ref[...] = u.astype(BF16)
        ut_ref[...] = u.T.astype(BF16)

    return _row_call(body, name, [(o, "row"), (z, "row")],
                     [(jax.ShapeDtypeStruct((s, d), BF16), "row"), (jax.ShapeDtypeStruct((d, s), BF16), "col")], s=s)


def _post_fwd(x, y, g, name):
    s, d = x.shape

    def body(x_ref, y_ref, g_ref, o_ref):
        yv = y_ref[...]
        o_ref[...] = x_ref[...] + yv * _rsqrt_ms(yv) * g_ref[...]

    return _row_call(body, name, [(x, "row"), (y, "row"), (g, "vec")],
                     [(jax.ShapeDtypeStruct((s, d), F32), "row")], s=s)[0]


def _loss_and_grad(x, target):
    s, d = x.shape

    def body(x_ref, t_ref, dx_ref, l_ref):
        err = x_ref[...] - t_ref[...]
        dx_ref[...] = err * (1.0 / d)
        part = jnp.sum(jnp.sum(err * err, axis=1, keepdims=True), axis=0, keepdims=True) * (0.5 / d)

        @pl.when(pl.program_id(0) == 0)
        def _():
            l_ref[...] = jnp.zeros_like(l_ref)

        l_ref[...] += jnp.broadcast_to(part, l_ref.shape)

    return _row_call(body, "loss_head", [(x, "row"), (target, "row")],
                     [(jax.ShapeDtypeStruct((s, d), F32), "row"),
                      (jax.ShapeDtypeStruct((1, LANES), F32), "vec")], s=s)


def _norm_bwd_rows(dn, v, g):
    r = _rsqrt_ms(v)
    a = dn * g
    dv = r * (a - v * (r * r) * jnp.mean(a * v, axis=-1, keepdims=True))
    return dv, dn * v * r


def _post_bwd(dx, y, g, name):
    s, d = dx.shape

    def body(dx_ref, y_ref, g_ref, dy_ref, dg_ref):
        dy, dg = _norm_bwd_rows(dx_ref[...], y_ref[...], g_ref[...])
        dy_ref[...] = dy.astype(BF16)

        @pl.when(pl.program_id(0) == 0)
        def _():
            dg_ref[...] = jnp.zeros_like(dg_ref)

        dg_ref[...] += jnp.sum(dg, axis=0, keepdims=True)

    return _row_call(body, name, [(dx, "row"), (y, "row"), (g, "vec")],
                     [(jax.ShapeDtypeStruct((s, d), BF16), "row"),
                      (jax.ShapeDtypeStruct((1, d), F32), "vec")], s=s)


def _gate_bwd(du, o, z, name):
    s, d = du.shape

    def body(du_ref, o_ref, z_ref, do_ref, dz_ref):
        duv, zv = du_ref[...], z_ref[...]
        sig = jax.nn.sigmoid(zv)
        do_ref[...] = (duv * (zv * sig)).astype(BF16)
        dz_ref[...] = (duv * o_ref[...] * (sig * (1.0 + zv * (1.0 - sig)))).astype(BF16)

    return _row_call(body, name, [(du, "row"), (o, "row"), (z, "row")],
                     [(jax.ShapeDtypeStruct((s, d), BF16), "row"),
                      (jax.ShapeDtypeStruct((s, d), BF16), "row")], s=s)


def _pre_bwd(dx, dhs, x, g, name):
    s, d = dx.shape
    n_dh = len(dhs)

    def body(*refs):
        dx_ref, dh_refs, (x_ref, g_ref, o_ref, dg_ref) = refs[0], refs[1:1 + n_dh], refs[1 + n_dh:]
        dh = dh_refs[0][...].astype(F32)
        for r in dh_refs[1:]:
            dh = dh + r[...].astype(F32)
        dv, dg = _norm_bwd_rows(dh, x_ref[...], g_ref[...])
        o_ref[...] = dx_ref[...] + dv

        @pl.when(pl.program_id(0) == 0)
        def _():
            dg_ref[...] = jnp.zeros_like(dg_ref)

        dg_ref[...] += jnp.sum(dg, axis=0, keepdims=True)

    return _row_call(body, name, [(dx, "row")] + [(h, "row") for h in dhs] + [(x, "row"), (g, "vec")],
                     [(jax.ShapeDtypeStruct((s, d), F32), "row"),
                      (jax.ShapeDtypeStruct((1, d), F32), "vec")], s=s)


def _lane_is_first_head():
    return lax.broadcasted_iota(jnp.int32, (1, LANES), 1) < HEAD_DIM


def _bcast_lanes(col):
    return jnp.broadcast_to(col, (col.shape[0], LANES))


def _pair_spec(s, off=0, width=LANES):
    return pl.BlockSpec((s, width), lambda p: (0, p + off))


def _stack_heads(pair, first):
    return jnp.concatenate([jnp.where(first, pair, 0), jnp.where(first, 0, pair)], axis=0).astype(BF16)


def _stacked_mask(t, strict):
    row = lax.broadcasted_iota(jnp.int32, (2 * t, t), 0)
    col = lax.broadcasted_iota(jnp.int32, (2 * t, t), 1)
    query = jnp.where(row >= t, row - t, row)
    return col < query if strict else col <= query


def _rowsum_heads(prod, first):
    return (jnp.sum(jnp.where(first, prod, 0.0), axis=1, keepdims=True),
            jnp.sum(jnp.where(first, 0.0, prod), axis=1, keepdims=True))


def _softplus_parts(z):
    e = jnp.exp(-jnp.abs(z))
    sp = jnp.maximum(z, 0.0) + jnp.log(1.0 + e)
    r = 1.0 / (1.0 + e)
    return sp, jnp.where(z >= 0, r, e * r)


def _split_dot(x, t):
    hi = x.astype(BF16)
    lo = (x - hi.astype(F32)).astype(BF16)
    return jnp.dot(hi, t, preferred_element_type=F32) + jnp.dot(lo, t, preferred_element_type=F32)


def _sb_tile(s):
    return min(256, s)


def _attn_b_fwd(qkv, name, exchange=None):
    s = qkv.shape[0]
    t = _sb_tile(s)
    nq = s // t

    def body(q_ref, k_ref, v_ref, o_ref, lt_ref):
        first = _lane_is_first_head()
        before = _stacked_mask(t, strict=True)
        tri = (lax.broadcasted_iota(jnp.int32, (t, t), 0) >= lax.broadcasted_iota(jnp.int32, (t, t), 1)).astype(BF16)

        def tile(j, carry, diag, qs):
            c, acc = carry
            c0 = pl.multiple_of(j * t, t)
            k2 = k_ref[pl.ds(c0, t), :]
            v2 = v_ref[pl.ds(c0, t), :]
            z = lax.dot_general(qs, k2, _NT, preferred_element_type=F32)
            sp, _ = _softplus_parts(z)
            lf = jnp.where(before, -sp, 0.0) if diag else -sp
            incl = jnp.dot(lf.astype(BF16), tri, preferred_element_type=F32)
            a = jnp.exp(z + c + incl)
            if diag:
                a = jnp.where(before, a, 0.0)
            pv = jnp.dot(a.astype(BF16), v2, preferred_element_type=F32)
            return c + incl[:, 0:1], acc + jnp.where(first, pv[:t], pv[t:])

        def qblock(i, _):
            r0 = pl.multiple_of(i * t, t)
            qs = _stack_heads(q_ref[pl.ds(r0, t), :] * Q_SCALE, first)
            carry = tile(i, (jnp.zeros((2 * t, 1), F32), jnp.zeros((t, LANES), F32)), True, qs)
            carry = lax.fori_loop(0, i, lambda jj, c: tile(i - 1 - jj, c, False, qs), carry)
            o_ref[pl.ds(r0, t), :] = carry[1]
            lt_ref[pl.ds(r0, t), 0:LANES] = _bcast_lanes(carry[0][:t])
            lt_ref[pl.ds(r0, t), LANES:2 * LANES] = _bcast_lanes(carry[0][t:])
            return 0

        lax.fori_loop(0, nq, qblock, 0)

    return _grid_call(
        body, name=name, grid=(N_PAIRS,),
        in_specs=[_pair_spec(s), _pair_spec(s, N_PAIRS), _pair_spec(s, 2 * N_PAIRS)],
        out_specs=[_pair_spec(s), _stat_spec(s)],
        out_shape=[jax.ShapeDtypeStruct((s, BRANCH), F32), jax.ShapeDtypeStruct((s, N_HEADS * LANES), F32)],
        args=(qkv, qkv, qkv), semantics=("parallel",), exchange=exchange)


def _attn_b_bwd(qkv, ltot, do, name, exchange=None):
    s = qkv.shape[0]
    t = _sb_tile(s)
    nq = s // t

    def body(q_ref, k_ref, v_ref, lt_ref, do_ref, dq_ref, dk_ref, dv_ref, dk_acc, dv_acc):
        first = _lane_is_first_head()
        before = _stacked_mask(t, strict=True)
        tri = (lax.broadcasted_iota(jnp.int32, (t, t), 0) <= lax.broadcasted_iota(jnp.int32, (t, t), 1)).astype(BF16)
        dk_acc[...] = jnp.zeros_like(dk_acc)
        dv_acc[...] = jnp.zeros_like(dv_acc)

        def tile(j, carry, diag, qs, dos, lt):
            p_l, p_g, dq_acc = carry
            c0 = pl.multiple_of(j * t, t)
            k2 = k_ref[pl.ds(c0, t), :]
            v2 = v_ref[pl.ds(c0, t), :]
            z = lax.dot_general(qs, k2, _NT, preferred_element_type=F32)
            sp, sig = _softplus_parts(z)
            lf = jnp.where(before, -sp, 0.0) if diag else -sp
            pref_l = jnp.dot(lf.astype(BF16), tri, preferred_element_type=F32)
            a = jnp.exp(z + ((lt - p_l) - pref_l + lf))
            if diag:
                a = jnp.where(before, a, 0.0)
            g = a * lax.dot_general(dos, v2, _NT, preferred_element_type=F32)
            pref_g = jnp.dot(g.astype(BF16), tri, preferred_element_type=F32)
            dz = g - sig * (p_g + pref_g)
            if diag:
                dz = jnp.where(before, dz, 0.0)
            dzb = dz.astype(BF16)
            dq = jnp.dot(dzb, k2, preferred_element_type=F32)
            dk_acc[pl.ds(c0, t), :] += lax.dot_general(dzb, qs, _TN, preferred_element_type=F32)
            dv_acc[pl.ds(c0, t), :] += lax.dot_general(a.astype(BF16), dos, _TN, preferred_element_type=F32)
            return p_l + pref_l[:, t - 1:t], p_g + pref_g[:, t - 1:t], dq_acc + jnp.where(first, dq[:t], dq[t:])

        def qblock(i, _):
            r0 = pl.multiple_of(i * t, t)
            qs = _stack_heads(q_ref[pl.ds(r0, t), :] * Q_SCALE, first)
            dos = _stack_heads(do_ref[pl.ds(r0, t), :], first)
            lt = jnp.concatenate([lt_ref[pl.ds(r0, t), 0:1], lt_ref[pl.ds(r0, t), LANES:LANES + 1]], axis=0)
            zero = jnp.zeros((2 * t, 1), F32)
            carry = (zero, zero, jnp.zeros((t, LANES), F32))
            carry = lax.fori_loop(0, i, lambda j, c: tile(j, c, False, qs, dos, lt), carry)
            carry = tile(i, carry, True, qs, dos, lt)
            dq_ref[pl.ds(r0, t), :] = (carry[2] * Q_SCALE).astype(BF16)
            return 0

        lax.fori_loop(0, nq, qblock, 0)
        dk_ref[...] = dk_acc[...].astype(BF16)
        dv_ref[...] = dv_acc[...].astype(BF16)

    out = jax.ShapeDtypeStruct((s, BRANCH), BF16)
    return _grid_call(
        body, name=name, grid=(N_PAIRS,),
        in_specs=[_pair_spec(s), _pair_spec(s, N_PAIRS), _pair_spec(s, 2 * N_PAIRS), _stat_spec(s), _pair_spec(s)],
        out_specs=[_pair_spec(s)] * 3, out_shape=[out] * 3,
        scratch_shapes=[pltpu.VMEM((s, LANES), F32), pltpu.VMEM((s, LANES), F32)],
        args=(qkv, qkv, qkv, ltot, do), semantics=("parallel",), exchange=exchange)


def _fox_tile(s):
    return min(256, s)


def _stat_spec(s):
    return pl.BlockSpec((s, 2 * LANES), lambda p: (0, p))


def _cum_spec(nt, t):
    return pl.BlockSpec((1, nt, 2, t), lambda p: (p, 0, 0, 0))


def _attn_c_fwd(qkv, cum4, name, exchange=None):
    s = qkv.shape[0]
    t = _fox_tile(s)
    nq = s // t

    def body(q_ref, k_ref, v_ref, c_ref, o_ref, lse_ref):
        first = _lane_is_first_head()
        causal = _stacked_mask(t, strict=False)

        def tile(j, carry, diag, qs):
            c0 = pl.multiple_of(j * t, t)
            k2 = k_ref[pl.ds(c0, t), :]
            v2 = v_ref[pl.ds(c0, t), :]
            cs = c_ref[0, j]
            m_prev, l_prev, acc = carry
            z = lax.dot_general(qs, k2, _NT, preferred_element_type=F32)
            sc = jnp.concatenate([z[:t] - cs[0:1, :], z[t:] - cs[1:2, :]], axis=0)
            if diag:
                sc = jnp.where(causal, sc, NEG)
            m_new = jnp.maximum(m_prev, jnp.max(sc, axis=1, keepdims=True))
            alpha = jnp.exp(m_prev - m_new)
            p = jnp.exp(sc - m_new)
            l_new = alpha * l_prev + jnp.sum(p, axis=1, keepdims=True)
            pv = jnp.dot(p.astype(BF16), v2, preferred_element_type=F32)
            acc = jnp.where(first, acc * alpha[:t] + pv[:t], acc * alpha[t:] + pv[t:])
            return m_new, l_new, acc

        def qblock(i, _):
            r0 = pl.multiple_of(i * t, t)
            qs = _stack_heads(q_ref[pl.ds(r0, t), :] * Q_SCALE, first)
            carry = (jnp.full((2 * t, 1), NEG, F32), jnp.zeros((2 * t, 1), F32), jnp.zeros((t, LANES), F32))
            carry = lax.fori_loop(0, i, lambda j, c: tile(j, c, False, qs), carry)
            m, l, acc = tile(i, carry, True, qs)
            inv = 1.0 / l
            lse = m + jnp.log(l)
            o_ref[pl.ds(r0, t), :] = acc * jnp.where(first, inv[:t], inv[t:])
            lse_ref[pl.ds(r0, t), 0:LANES] = _bcast_lanes(lse[:t])
            lse_ref[pl.ds(r0, t), LANES:2 * LANES] = _bcast_lanes(lse[t:])
            return 0

        lax.fori_loop(0, nq, qblock, 0)

    return _grid_call(
        body, name=name, grid=(N_PAIRS,),
        in_specs=[_pair_spec(s), _pair_spec(s, N_PAIRS), _pair_spec(s, 2 * N_PAIRS), _cum_spec(nq, t)],
        out_specs=[_pair_spec(s), _stat_spec(s)],
        out_shape=[jax.ShapeDtypeStruct((s, BRANCH), F32), jax.ShapeDtypeStruct((s, N_HEADS * LANES), F32)],
        args=(qkv, qkv, qkv, cum4), semantics=("parallel",), exchange=exchange)


def _attn_c_bwd(qkv, cum4, o, lse, do, name, exchange=None):
    s = qkv.shape[0]
    t = _fox_tile(s)
    nq = s // t

    def body(q_ref, k_ref, v_ref, c_ref, o_ref, lse_ref, do_ref, dq_ref, dk_ref, dv_ref, dc_ref, dk_acc, dv_acc):
        first = _lane_is_first_head()
        causal = _stacked_mask(t, strict=False)
        eye = lax.broadcasted_iota(jnp.int32, (t, t), 0) == lax.broadcasted_iota(jnp.int32, (t, t), 1)
        dk_acc[...] = jnp.zeros_like(dk_acc)
        dv_acc[...] = jnp.zeros_like(dv_acc)
        dc_ref[...] = jnp.zeros_like(dc_ref)

        def tile(j, carry, diag, qs, dos, delta, lse):
            dq_acc, rs = carry
            c0 = pl.multiple_of(j * t, t)
            k2 = k_ref[pl.ds(c0, t), :]
            v2 = v_ref[pl.ds(c0, t), :]
            cs = c_ref[0, j]
            z = lax.dot_general(qs, k2, _NT, preferred_element_type=F32)
            sc = jnp.concatenate([z[:t] - cs[0:1, :], z[t:] - cs[1:2, :]], axis=0)
            p = jnp.exp(sc - lse)
            if diag:
                p = jnp.where(causal, p, 0.0)
            ds = p * (lax.dot_general(dos, v2, _NT, preferred_element_type=F32) - delta)
            dsb = ds.astype(BF16)
            dq = jnp.dot(dsb, k2, preferred_element_type=F32)
            dk_acc[pl.ds(c0, t), :] += lax.dot_general(dsb, qs, _TN, preferred_element_type=F32)
            dv_acc[pl.ds(c0, t), :] += lax.dot_general(p.astype(BF16), dos, _TN, preferred_element_type=F32)
            col_sums = jnp.concatenate([jnp.sum(ds[:t], axis=0, keepdims=True), jnp.sum(ds[t:], axis=0, keepdims=True)], axis=0)
            dc_ref[0, j] = dc_ref[0, j] - col_sums
            return dq_acc + jnp.where(first, dq[:t], dq[t:]), rs + jnp.sum(ds, axis=1, keepdims=True)

        def qblock(i, _):
            r0 = pl.multiple_of(i * t, t)
            do2 = do_ref[pl.ds(r0, t), :]
            qs = _stack_heads(q_ref[pl.ds(r0, t), :] * Q_SCALE, first)
            dos = _stack_heads(do2, first)
            delta = jnp.concatenate(_rowsum_heads(do2.astype(F32) * o_ref[pl.ds(r0, t), :], first), axis=0)
            lse = jnp.concatenate([lse_ref[pl.ds(r0, t), 0:1], lse_ref[pl.ds(r0, t), LANES:LANES + 1]], axis=0)
            carry = (jnp.zeros((t, LANES), F32), jnp.zeros((2 * t, 1), F32))
            carry = lax.fori_loop(0, i, lambda j, c: tile(j, c, False, qs, dos, delta, lse), carry)
            dq_acc, rs = tile(i, carry, True, qs, dos, delta, lse)
            dq_ref[pl.ds(r0, t), :] = (dq_acc * Q_SCALE).astype(BF16)
            as_row = lambda col_vec: jnp.sum(jnp.where(eye, col_vec, 0.0), axis=0, keepdims=True)
            dc_ref[0, i] = dc_ref[0, i] + jnp.concatenate([as_row(rs[:t]), as_row(rs[t:])], axis=0)
            return 0

        lax.fori_loop(0, nq, qblock, 0)
        dk_ref[...] = dk_acc[...].astype(BF16)
        dv_ref[...] = dv_acc[...].astype(BF16)

    out = jax.ShapeDtypeStruct((s, BRANCH), BF16)
    return _grid_call(
        body, name=name, grid=(N_PAIRS,),
        in_specs=[_pair_spec(s), _pair_spec(s, N_PAIRS), _pair_spec(s, 2 * N_PAIRS), _cum_spec(nq, t),
                  _pair_spec(s), _stat_spec(s), _pair_spec(s)],
        out_specs=[_pair_spec(s)] * 3 + [_cum_spec(nq, t)],
        out_shape=[out] * 3 + [jax.ShapeDtypeStruct(cum4.shape, F32)],
        scratch_shapes=[pltpu.VMEM((s, LANES), F32), pltpu.VMEM((s, LANES), F32)],
        args=(qkv, qkv, qkv, cum4, o, lse, do), semantics=("parallel",), exchange=exchange)


FG_CHUNK = 512


def _tri_dot3(x, t):
    hi = x.astype(BF16)
    r1 = x - hi.astype(F32)
    mid = r1.astype(BF16)
    lo = (r1 - mid.astype(F32)).astype(BF16)
    return (jnp.dot(hi, t, preferred_element_type=F32) + jnp.dot(mid, t, preferred_element_type=F32)
            + jnp.dot(lo, t, preferred_element_type=F32))


def _fgate_fwd(h, wf_t, b_col, name):
    s = h.shape[0]
    c = min(FG_CHUNK, s)

    def body(h_ref, w_ref, b_ref, xf_ref, cum_ref, carry_ref):
        @pl.when(pl.program_id(0) == 0)
        def _():
            carry_ref[...] = jnp.zeros_like(carry_ref)

        xf = lax.dot_general(w_ref[...], h_ref[...], _NT, preferred_element_type=F32) + b_ref[:, 0:1]
        xf_ref[...] = xf
        logf = jnp.minimum(xf, 0.0) - jnp.log(1.0 + jnp.exp(-jnp.abs(xf)))
        row = lax.broadcasted_iota(jnp.int32, (c, c), 0)
        col = lax.broadcasted_iota(jnp.int32, (c, c), 1)
        cum = _tri_dot3(logf, (row <= col).astype(BF16)) + carry_ref[:, 0:1]
        cum_ref[...] = cum
        carry_ref[...] = _bcast_lanes(cum[:, c - 1:c])

    out = jax.ShapeDtypeStruct((N_HEADS, s), F32)
    return pl.pallas_call(
        body, name=name, grid=(s // c,),
        in_specs=[pl.BlockSpec((c, D_MODEL), lambda i: (i, 0)),
                  pl.BlockSpec((N_HEADS, D_MODEL), lambda i: (0, 0)),
                  pl.BlockSpec((N_HEADS, LANES), lambda i: (0, 0))],
        out_specs=[pl.BlockSpec((N_HEADS, c), lambda i: (0, i))] * 2,
        out_shape=[out, out],
        scratch_shapes=[pltpu.VMEM((N_HEADS, LANES), F32)],
        compiler_params=_params(("arbitrary",)),
    )(h, wf_t, b_col)


def _fgate_bwd(dcum, xf, h, wf_t, name):
    s = h.shape[0]
    c = min(FG_CHUNK, s)
    n = s // c

    def body(dc_ref, xf_ref, h_ref, w_ref, dw_ref, dh_ref, db_ref, carry_ref):
        @pl.when(pl.program_id(0) == 0)
        def _():
            carry_ref[...] = jnp.zeros_like(carry_ref)
            dw_ref[...] = jnp.zeros_like(dw_ref)
            db_ref[...] = jnp.zeros_like(db_ref)

        row = lax.broadcasted_iota(jnp.int32, (c, c), 0)
        col = lax.broadcasted_iota(jnp.int32, (c, c), 1)
        dlogf = _tri_dot3(dc_ref[...], (row >= col).astype(BF16)) + carry_ref[:, 0:1]
        carry_ref[...] = _bcast_lanes(dlogf[:, 0:1])
        xf = xf_ref[...]
        e = jnp.exp(-jnp.abs(xf))
        r = 1.0 / (1.0 + e)
        dxf = dlogf * jnp.where(xf >= 0, e * r, r)
        db_ref[...] += _bcast_lanes(jnp.sum(dxf, axis=1, keepdims=True))
        dxb = dxf.astype(BF16)
        dw_ref[...] += jnp.dot(dxb, h_ref[...], preferred_element_type=F32)
        dh_ref[...] = lax.dot_general(dxb, w_ref[...], _TN, preferred_element_type=F32)

    rev = lambda i: n - 1 - i
    return pl.pallas_call(
        body, name=name, grid=(n,),
        in_specs=[pl.BlockSpec((N_HEADS, c), lambda i: (0, rev(i))),
                  pl.BlockSpec((N_HEADS, c), lambda i: (0, rev(i))),
                  pl.BlockSpec((c, D_MODEL), lambda i: (rev(i), 0)),
                  pl.BlockSpec((N_HEADS, D_MODEL), lambda i: (0, 0))],
        out_specs=[pl.BlockSpec((N_HEADS, D_MODEL), lambda i: (0, 0)),
                   pl.BlockSpec((c, D_MODEL), lambda i: (rev(i), 0)),
                   pl.BlockSpec((N_HEADS, LANES), lambda i: (0, 0))],
        out_shape=[jax.ShapeDtypeStruct((N_HEADS, D_MODEL), F32), jax.ShapeDtypeStruct((s, D_MODEL), F32),
                   jax.ShapeDtypeStruct((N_HEADS, LANES), F32)],
        scratch_shapes=[pltpu.VMEM((N_HEADS, LANES), F32)],
        compiler_params=_params(("arbitrary",)),
    )(dcum, xf, h, wf_t)


def _to_cum4(v, t):
    s = v.shape[1]
    return v.reshape(N_PAIRS, 2, s // t, t).transpose(0, 2, 1, 3)


def _from_cum4(v4):
    p, nt, two, t = v4.shape
    return v4.transpose(0, 2, 1, 3).reshape(p * two, nt * t)


def _alibi_slopes():
    return (2.0 ** (-8.0 * np.arange(1, N_HEADS + 1, dtype=np.float32) / N_HEADS)).astype(np.float32)


def _per_head_lanes(v):
    return jnp.repeat(v.astype(F32).reshape(N_PAIRS, 1, 2), LANES, axis=2)


def _attn_a_specs(s):
    q = _pair_spec(s)
    k = pl.BlockSpec((s, LANES), lambda p: (0, N_PAIRS + p // 8))
    v = pl.BlockSpec((s, LANES), lambda p: (0, N_PAIRS + KV_A // LANES + p // 8))
    head = pl.BlockSpec((1, 1, 2 * LANES), lambda p: (p, 0, 0))
    return q, k, v, head


def _attn_a_geometry(p, slope_ref, sink_ref):
    kv_half = (p // 4) % 2
    kv_first = kv_half == 0
    lane_first = _lane_is_first_head()
    kv_lanes = (lax.broadcasted_iota(jnp.int32, (1, LANES), 1) // HEAD_DIM) == kv_half
    row = lax.broadcasted_iota(jnp.int32, (2 * WINDOW, 2 * WINDOW), 0)
    cj = lax.broadcasted_iota(jnp.int32, (2 * WINDOW, 2 * WINDOW), 1)
    second = row >= WINDOW
    dist = WINDOW + jnp.where(second, row - WINDOW, row) - cj
    valid = (dist >= 0) & (dist < WINDOW)
    per_row = lambda ref: jnp.where(second[:, 0:1], ref[0, :, LANES:LANES + 1], ref[0, :, 0:1])
    return kv_first, lane_first, kv_lanes, per_row(slope_ref) * dist.astype(F32), valid, per_row(sink_ref)


def _swap_halves(x):
    return pltpu.roll(x, HEAD_DIM, 1)


def _attn_a_fwd(qkv, slopes, sinks, name, exchange=None):
    s = qkv.shape[0]
    nb = s // WINDOW

    def body(q_ref, k_ref, v_ref, sl_ref, sk_ref, o_ref, lse_ref):
        kv_first, lane_first, kv_lanes, bias, valid, sink = _attn_a_geometry(pl.program_id(0), sl_ref, sk_ref)

        def block(r0, k0, width):
            q2 = q_ref[pl.ds(r0, WINDOW), :].astype(F32) * Q_SCALE
            q2r = _swap_halves(q2)
            xs = jnp.concatenate([jnp.where(kv_first, q2, q2r), jnp.where(kv_first, q2r, q2)], axis=0).astype(BF16)
            km = jnp.where(kv_lanes, k_ref[pl.ds(k0, width), :], 0).astype(BF16)
            vm = jnp.where(kv_lanes, v_ref[pl.ds(k0, width), :], 0).astype(BF16)
            sc = lax.dot_general(xs, km, _NT, preferred_element_type=F32) - bias[:, 2 * WINDOW - width:]
            sc = jnp.where(valid[:, 2 * WINDOW - width:], sc, NEG)
            m = jnp.maximum(jnp.max(sc, axis=1, keepdims=True), sink)
            pr = jnp.exp(sc - m)
            l = jnp.sum(pr, axis=1, keepdims=True) + jnp.exp(sink - m)
            os = jnp.dot(pr.astype(BF16), vm, preferred_element_type=F32) * (1.0 / l)
            lse = m + jnp.log(l)
            lse_ref[pl.ds(r0, WINDOW), 0:LANES] = _bcast_lanes(lse[:WINDOW])
            lse_ref[pl.ds(r0, WINDOW), LANES:2 * LANES] = _bcast_lanes(lse[WINDOW:])
            oa = jnp.where(kv_first, os[:WINDOW], _swap_halves(os[:WINDOW]))
            ob = jnp.where(kv_first, _swap_halves(os[WINDOW:]), os[WINDOW:])
            o_ref[pl.ds(r0, WINDOW), :] = jnp.where(lane_first, oa, ob)

        block(0, 0, WINDOW)

        def loop(n, _):
            r0 = pl.multiple_of(n * WINDOW, WINDOW)
            block(r0, pl.multiple_of(r0 - WINDOW, WINDOW), 2 * WINDOW)
            return 0

        lax.fori_loop(1, nb, loop, 0)

    q, k, v, head = _attn_a_specs(s)
    return _grid_call(
        body, name=name, grid=(N_PAIRS,),
        in_specs=[q, k, v, head, head],
        out_specs=[_pair_spec(s), _stat_spec(s)],
        out_shape=[jax.ShapeDtypeStruct((s, BRANCH), F32), jax.ShapeDtypeStruct((s, N_HEADS * LANES), F32)],
        args=(qkv, qkv, qkv, slopes, sinks), semantics=("parallel",), exchange=exchange)


def _attn_a_bwd(qkv, slopes, sinks, o, lse, do, name, exchange=None):
    s = qkv.shape[0]
    nb = s // WINDOW

    def body(q_ref, k_ref, v_ref, sl_ref, sk_ref, o_ref, lse_ref, do_ref, dq_ref, dk_ref, dv_ref, dsk_ref):
        p_id = pl.program_id(0)
        kv_first, lane_first, kv_lanes, bias, valid, sink = _attn_a_geometry(p_id, sl_ref, sk_ref)

        @pl.when(p_id % 8 == 0)
        def _():
            dk_ref[...] = jnp.zeros_like(dk_ref)
            dv_ref[...] = jnp.zeros_like(dv_ref)

        def align(v2):
            v2r = _swap_halves(v2)
            both = jnp.concatenate([jnp.where(kv_first, v2, v2r), jnp.where(kv_first, v2r, v2)], axis=0)
            return jnp.where(kv_lanes, both, 0.0).astype(BF16)

        def block(r0, k0, width, sink_sum):
            xq = align(q_ref[pl.ds(r0, WINDOW), :].astype(F32) * Q_SCALE)
            do2 = do_ref[pl.ds(r0, WINDOW), :].astype(F32)
            xdo = align(do2)
            delta = jnp.concatenate(_rowsum_heads(do2 * o_ref[pl.ds(r0, WINDOW), :], lane_first), axis=0)
            lse = jnp.concatenate([lse_ref[pl.ds(r0, WINDOW), 0:1], lse_ref[pl.ds(r0, WINDOW), LANES:LANES + 1]], axis=0)
            km = jnp.where(kv_lanes, k_ref[pl.ds(k0, width), :], 0).astype(BF16)
            vm = jnp.where(kv_lanes, v_ref[pl.ds(k0, width), :], 0).astype(BF16)
            sc = lax.dot_general(xq, km, _NT, preferred_element_type=F32) - bias[:, 2 * WINDOW - width:]
            pr = jnp.where(valid[:, 2 * WINDOW - width:], jnp.exp(sc - lse), 0.0)
            ds = pr * (lax.dot_general(xdo, vm, _NT, preferred_element_type=F32) - delta)
            dsb = ds.astype(BF16)
            dq_al = jnp.dot(dsb, km, preferred_element_type=F32)
            dk_ref[pl.ds(k0, width), :] += lax.dot_general(dsb, xq, _TN, preferred_element_type=F32)
            dv_ref[pl.ds(k0, width), :] += lax.dot_general(pr.astype(BF16), xdo, _TN, preferred_element_type=F32)
            dqa = jnp.where(kv_first, dq_al[:WINDOW], _swap_halves(dq_al[:WINDOW]))
            dqb = jnp.where(kv_first, _swap_halves(dq_al[WINDOW:]), dq_al[WINDOW:])
            dq_ref[pl.ds(r0, WINDOW), :] = (jnp.where(lane_first, dqa, dqb) * Q_SCALE).astype(BF16)
            return sink_sum + jnp.exp(sink - lse) * delta

        sink_sum = block(0, 0, WINDOW, jnp.zeros((2 * WINDOW, 1), F32))

        def loop(n, c):
            r0 = pl.multiple_of(n * WINDOW, WINDOW)
            return block(r0, pl.multiple_of(r0 - WINDOW, WINDOW), 2 * WINDOW, c)

        sink_sum = lax.fori_loop(1, nb, loop, sink_sum)
        dsk_ref[0, :, 0:LANES] = jnp.broadcast_to(-jnp.sum(sink_sum[:WINDOW], axis=0, keepdims=True), (1, LANES))
        dsk_ref[0, :, LANES:2 * LANES] = jnp.broadcast_to(-jnp.sum(sink_sum[WINDOW:], axis=0, keepdims=True), (1, LANES))

    q, k, v, head = _attn_a_specs(s)
    kv_out = pl.BlockSpec((s, LANES), lambda p: (0, p // 8))
    return _grid_call(
        body, name=name, grid=(N_PAIRS,),
        in_specs=[q, k, v, head, head, _pair_spec(s), _stat_spec(s), _pair_spec(s)],
        out_specs=[_pair_spec(s), kv_out, kv_out, head],
        out_shape=[jax.ShapeDtypeStruct((s, BRANCH), BF16), jax.ShapeDtypeStruct((s, KV_A), F32),
                   jax.ShapeDtypeStruct((s, KV_A), F32), jax.ShapeDtypeStruct((N_PAIRS, 1, 2 * LANES), F32)],
        args=(qkv, qkv, qkv, slopes, sinks, o, lse, do), semantics=("arbitrary",), exchange=exchange)


def _layer_kind(i):
    return i % 3, i // 3


GATHER_FIRST = [("in", 0)]
GATHER_BEHIND = {0: [("out", 0), ("in", 1)], 1: [("out", 1), ("in", 2), ("out", 2)], 2: [("in", 3), ("out", 3)]}


def _forward_backward(x, target, g_pre, g_post, sinks_a, b_f_c, shards, chip, core):
    s = x.shape[0]
    slopes = _per_head_lanes(jnp.asarray(_alibi_slopes()))
    w_in, w_out, wf_t = {}, {}, {}

    def deliver(keys, gathered):
        for (side, layer), g in zip(keys, gathered):
            sh = shards[(side, layer)]
            g = lax.dynamic_update_slice(g, sh[None], (chip, 0, 0))
            if side == "out":
                w_out[layer] = g.reshape(4 * sh.shape[0], sh.shape[1])
                continue
            w = g.transpose(1, 0, 2).reshape(sh.shape[0], 4 * sh.shape[1])
            if _layer_kind(layer)[0] == 2:
                w, wf_t[layer] = w[:, :4 * BRANCH], w[:, 4 * BRANCH:].T
            w_in[layer] = w

    deliver(GATHER_FIRST, _exchange_call(_GatherExchange([shards[k] for k in GATHER_FIRST]), "gather_first_weights"))
    saved = []
    for i in range(DEPTH):
        kind, j = _layer_kind(i)
        tag = f"l{i}"
        w = w_in[i]
        nqkv = A_QKV if kind == 0 else B_QKV
        tn = 512 if kind == 0 else 1024
        h, h_t = _rmsnorm_fwd(x, g_pre[i:i + 1], f"prenorm_{tag}")
        qkv = _matmul(h, w, mode="nn", out_dtype=BF16, name=f"inproj_qkv_{tag}", n=nqkv, tn=tn)
        z = _matmul(h, w, mode="nn", out_dtype=F32, name=f"inproj_gate_{tag}", n=BRANCH, b_off=nqkv // tn, tn=tn)
        behind = GATHER_BEHIND.get(i)
        exchange = _GatherExchange([shards[k] for k in behind]) if behind else None
        if kind == 0:
            sink_l = _per_head_lanes(sinks_a[j])
            (o, lse), arrived = _attn_a_fwd(qkv, slopes, sink_l, f"attn_a_fwd_{tag}", exchange)
            extra = (sink_l, lse)
        elif kind == 1:
            (o, extra), arrived = _attn_b_fwd(qkv, f"attn_b_fwd_{tag}", exchange)
        else:
            b_col = jnp.broadcast_to(b_f_c[j].astype(F32)[:, None], (N_HEADS, LANES))
            xf, cum = _fgate_fwd(h, wf_t[i], b_col, f"fgate_fwd_{tag}")
            cum4 = _to_cum4(cum, _fox_tile(s))
            (o, lse), arrived = _attn_c_fwd(qkv, cum4, f"attn_c_fwd_{tag}", exchange)
            extra = (xf, cum4, lse)
        if behind:
            deliver(behind, arrived)
        u, u_t = _gate_fwd(o, z, f"gate_{tag}")
        y = _matmul(u, w_out[i], mode="nn", out_dtype=F32, name=f"outproj_{tag}")
        saved.append((x, h, h_t, qkv, z, o, u_t, y, extra))
        x = _post_fwd(x, y, g_post[i:i + 1], f"postnorm_{tag}")

    dx, loss_part = _loss_and_grad(x, target)

    d_g_pre, d_g_post = [None] * DEPTH, [None] * DEPTH
    d_sinks = [None, None]
    d_b_f = None
    reduced = {}
    pending = None

    def finish_reduce(layer, sums, arrived):
        kind, j = _layer_kind(layer)
        for side, own, arr in zip(("in", "out"), sums, arrived):
            reduced[(side, kind)] = _sum_chips(own, arr, core, f"shard_sum_{side}_l{layer}", j, 2 if kind == 0 else 1,
                                               into=reduced.get((side, kind)))

    for i in reversed(range(DEPTH)):
        kind, j = _layer_kind(i)
        tag = f"l{i}"
        x_in, h, h_t, qkv, z, o, u_t, y, extra = saved[i]
        tn = 512 if kind == 0 else 1024
        dy, d_g_post[i] = _post_bwd(dx, y, g_post[i:i + 1], f"postnorm_bwd_{tag}")
        dw_out = _matmul(u_t, dy, mode="nn", out_dtype=F32, name=f"dw_out_{tag}")
        du = _matmul(dy, w_out[i], mode="nt", out_dtype=F32, name=f"d_gated_{tag}")
        do, dz = _gate_bwd(du, o, z, f"gate_bwd_{tag}")
        dhs = []
        exchange = _ScatterExchange(pending[1]) if pending else None
        if kind == 0:
            sink_l, lse = extra
            (dq, dk, dv, dsk), arrived = _attn_a_bwd(qkv, slopes, sink_l, o, lse, do, f"attn_a_bwd_{tag}", exchange)
            d_sinks[j] = dsk[:, 0, ::LANES].reshape(N_HEADS)
            parts = [dq, dk.astype(BF16), dv.astype(BF16), dz]
        elif kind == 1:
            (dq, dk, dv), arrived = _attn_b_bwd(qkv, extra, do, f"attn_b_bwd_{tag}", exchange)
            parts = [dq, dk, dv, dz]
        else:
            xf, cum4, lse = extra
            (dq, dk, dv, dcum4), arrived = _attn_c_bwd(qkv, cum4, o, lse, do, f"attn_c_bwd_{tag}", exchange)
            d_wf_t, dh_f, db = _fgate_bwd(_from_cum4(dcum4), xf, h, wf_t[i], f"fgate_bwd_{tag}")
            d_b_f = db[:, 0]
            dhs.append(dh_f)
            parts = [dq, dk, dv, dz]
        if pending:
            finish_reduce(pending[0], pending[1], arrived)
        dproj = jnp.concatenate(parts, axis=1)
        dw_in = _matmul(h_t, dproj, mode="nn", out_dtype=F32, name=f"dw_in_{tag}", tn=tn)
        if kind == 2:
            dw_in = jnp.concatenate([dw_in, d_wf_t.T], axis=1)
        dhs.insert(0, _matmul(dproj, w_in[i], mode="nt", out_dtype=F32, name=f"dh_{tag}", tk=512))
        dx, d_g_pre[i] = _pre_bwd(dx, dhs, x_in, g_pre[i:i + 1], f"prenorm_bwd_{tag}")

        to_chips = [dw_in.reshape(dw_in.shape[0], 4, dw_in.shape[1] // 4).transpose(1, 0, 2).astype(BF16),
                    dw_out.reshape(4, dw_out.shape[0] // 4, dw_out.shape[1]).astype(BF16)]
        theirs = _sibling_send(to_chips, f"grad_sibling_exchange_{tag}")
        pending = (i, [_add_pairs(a, b, f"chip_sum_{side}_{tag}") for a, b, side in zip(to_chips, theirs, ("in", "out"))])

    arrived = _exchange_call(_ScatterExchange(pending[1]), "grad_chip_scatter_last")
    finish_reduce(pending[0], pending[1], arrived)

    return dict(loss=loss_part, dx=dx, g_pre=jnp.concatenate(d_g_pre, axis=0), g_post=jnp.concatenate(d_g_post, axis=0),
                sinks_a=jnp.stack(d_sinks), b_f_c=d_b_f[None, :], reduced=reduced)


def _place():
    x, y, c = lax.axis_index("x"), lax.axis_index("y"), lax.axis_index("c")
    others = [(1 - x, y), (x, 1 - y), (1 - x, 1 - y)]
    return x, y, c, others


def _half_rows(ref_rows, which):
    half = ref_rows // 2
    return pl.ds(pl.multiple_of(which * half, half), half)


def _remote(src, dst, sems, k, device):
    send, recv = sems
    return pltpu.make_async_remote_copy(src_ref=src, dst_ref=dst, send_sem=send.at[k], recv_sem=recv.at[k],
                                        device_id=device, device_id_type=MESH)


def _hbm_call(body, name, ins, out_shapes, n_remote, aliases=None):
    any_spec = pl.BlockSpec(memory_space=pl.ANY)
    return pl.pallas_call(
        body, name=name, in_specs=[any_spec] * len(ins), out_specs=[any_spec] * len(out_shapes),
        out_shape=out_shapes, input_output_aliases=aliases or {},
        scratch_shapes=[pltpu.SemaphoreType.DMA((n_remote,)), pltpu.SemaphoreType.DMA((n_remote,))],
    )(*ins)


class _GatherExchange:
    def __init__(self, shards):
        self.ins = list(shards)
        self.out_shapes = [jax.ShapeDtypeStruct((4,) + a.shape, a.dtype) for a in shards]
        self.n_sems = 6 * len(shards)
        self.aliases = {}

    def _copies(self, ins, outs, sems):
        x, y, c, others = _place()
        me = 2 * x + y
        table = []
        for w, (src, dst) in enumerate(zip(ins, outs)):
            mine, theirs = _half_rows(src.shape[0], c), _half_rows(src.shape[0], 1 - c)
            for j, (px, py) in enumerate(others):
                there = 2 * px + py
                send = _remote(src.at[mine], dst.at[me, mine], sems, 6 * w + j, (px, py, c))
                landed = _remote(dst.at[there, mine], dst.at[there, mine], sems, 6 * w + j, (px, py, c))
                passed = _remote(dst.at[there, mine], dst.at[there, mine], sems, 6 * w + 3 + j, (x, y, 1 - c))
                from_sibling = _remote(dst.at[there, theirs], dst.at[there, theirs], sems, 6 * w + 3 + j, (x, y, 1 - c))
                table.append((send, landed, passed, from_sibling))
        return table

    def start(self, ins, outs, sems):
        for send, _, _, _ in self._copies(ins, outs, sems):
            send.start()

    def mid(self, ins, outs, sems):
        for _, landed, passed, _ in self._copies(ins, outs, sems):
            landed.wait_recv()
            passed.start()

    def finish(self, ins, outs, sems):
        table = self._copies(ins, outs, sems)
        for _, _, _, from_sibling in table:
            from_sibling.wait_recv()
        for send, _, passed, _ in table:
            send.wait_send()
            passed.wait_send()


def _exchange_call(ex, name):
    n_in, n_out = len(ex.ins), len(ex.out_shapes)

    def body(*refs):
        ins, outs, sems = refs[:n_in], refs[n_in:n_in + n_out], refs[n_in + n_out:]
        ex.start(ins, outs, sems)
        ex.mid(ins, outs, sems)
        ex.finish(ins, outs, sems)

    return _hbm_call(body, name, ex.ins, ex.out_shapes, ex.n_sems, aliases=ex.aliases)


def _grid_call(body, *, name, grid, in_specs, out_specs, out_shape, args, scratch_shapes=(), semantics, exchange=None):
    if exchange is None:
        res = pl.pallas_call(body, name=name, grid=grid, in_specs=list(in_specs), out_specs=list(out_specs),
                             out_shape=list(out_shape), scratch_shapes=list(scratch_shapes),
                             compiler_params=_params(semantics))(*args)
        return res, []
    n_in, n_out, n_scr = len(args), len(out_shape), len(scratch_shapes)
    x_in, x_out = len(exchange.ins), len(exchange.out_shapes)
    steps = grid[0]

    def wrapped(*refs):
        core_in, ex_in = refs[:n_in], refs[n_in:n_in + x_in]
        rest = refs[n_in + x_in:]
        core_out, ex_out = rest[:n_out], rest[n_out:n_out + x_out]
        scratch, sems = rest[n_out + x_out:n_out + x_out + n_scr], rest[n_out + x_out + n_scr:]
        step = pl.program_id(0)

        @pl.when(step == 0)
        def _():
            exchange.start(ex_in, ex_out, sems)

        body(*core_in, *core_out, *scratch)

        @pl.when(step == (3 * steps) // 4 - 1)
        def _():
            exchange.mid(ex_in, ex_out, sems)

        @pl.when(step == steps - 1)
        def _():
            exchange.finish(ex_in, ex_out, sems)

    any_spec = pl.BlockSpec(memory_space=pl.ANY)
    res = pl.pallas_call(
        wrapped, name=name, grid=grid,
        in_specs=list(in_specs) + [any_spec] * x_in, out_specs=list(out_specs) + [any_spec] * x_out,
        out_shape=list(out_shape) + list(exchange.out_shapes),
        input_output_aliases={n_in + a: n_out + b for a, b in exchange.aliases.items()},
        scratch_shapes=list(scratch_shapes) + [pltpu.SemaphoreType.DMA((exchange.n_sems,)),
                                               pltpu.SemaphoreType.DMA((exchange.n_sems,))],
        compiler_params=_params(("arbitrary",)),
    )(*args, *exchange.ins)
    return res[:n_out], res[n_out:]


def _sibling_send(parts, name):
    n = len(parts)

    def body(*refs):
        ins, outs, sems = refs[:n], refs[n:2 * n], refs[2 * n:2 * n + 2]
        x, y, c, _ = _place()
        pend = []
        for w in range(n):
            cp = _remote(ins[w].at[:, _half_rows(ins[w].shape[1], 1 - c)], outs[w], sems, w, (x, y, 1 - c))
            cp.start()
            pend.append(cp)
        for cp in pend:
            cp.wait_recv()
            cp.wait_send()

    out_shapes = [jax.ShapeDtypeStruct((4, a.shape[1] // 2, a.shape[2]), a.dtype) for a in parts]
    return _hbm_call(body, name, parts, out_shapes, n)


class _ScatterExchange:
    def __init__(self, sums):
        self.ins = list(sums)
        self.out_shapes = [jax.ShapeDtypeStruct(a.shape, a.dtype) for a in sums]
        self.n_sems = 3 * len(sums)
        self.aliases = {}

    def _copies(self, ins, outs, sems):
        x, y, c, others = _place()
        me = 2 * x + y
        table = []
        for w, (src, dst) in enumerate(zip(ins, outs)):
            for j, (px, py) in enumerate(others):
                there = 2 * px + py
                send = _remote(src.at[there], dst.at[me], sems, 3 * w + j, (px, py, c))
                landed = _remote(dst.at[there], dst.at[there], sems, 3 * w + j, (px, py, c))
                table.append((send, landed))
        return table

    def start(self, ins, outs, sems):
        for send, _ in self._copies(ins, outs, sems):
            send.start()

    def mid(self, ins, outs, sems):
        pass

    def finish(self, ins, outs, sems):
        table = self._copies(ins, outs, sems)
        for _, landed in table:
            landed.wait_recv()
        for send, _ in table:
            send.wait_send()


def _sibling_join(shards):
    n = len(shards)

    def body(*refs):
        ins, outs, sems = refs[:n], refs[n:2 * n], refs[2 * n:2 * n + 2]
        x, y, c, _ = _place()
        pend = []
        for w in range(n):
            rows = ins[w].shape[1]
            mine, theirs = _half_rows(rows, c), _half_rows(rows, 1 - c)
            cp = _remote(ins[w].at[:, mine], outs[w].at[:, mine], sems, w, (x, y, 1 - c))
            cp.start()
            pend.append((cp, _remote(ins[w].at[:, theirs], outs[w].at[:, theirs], sems, w, (x, y, 1 - c))))
        for cp, landed in pend:
            landed.wait_recv()
            cp.wait_send()

    out_shapes = [jax.ShapeDtypeStruct(a.shape, a.dtype) for a in shards]
    return _hbm_call(body, "grad_sibling_join", shards, out_shapes, n, aliases={w: w for w in range(n)})


SMALL_ROWS = 136


def _all_reduce_small(vec):
    def body(v_ref, o_ref, buf, send, recv, loc):
        x, y, c, _ = _place()
        me = 4 * x + 2 * y + c
        lc = pltpu.make_async_copy(v_ref, buf.at[me], loc.at[0])
        lc.start()
        cps = []
        for k in range(1, 8):
            fx, fy, fc = (k >> 2) & 1, (k >> 1) & 1, k & 1
            peer = (x ^ fx, y ^ fy, c ^ fc)
            cp = pltpu.make_async_remote_copy(src_ref=v_ref, dst_ref=buf.at[me], send_sem=send.at[k - 1],
                                              recv_sem=recv.at[k - 1], device_id=peer, device_id_type=MESH)
            cp.start()
            cps.append((cp, 4 * peer[0] + 2 * peer[1] + peer[2]))
        for k, (cp, src) in enumerate(cps):
            pltpu.make_async_remote_copy(src_ref=v_ref, dst_ref=buf.at[src], send_sem=send.at[k], recv_sem=recv.at[k],
                                         device_id=(x, y, c), device_id_type=MESH).wait_recv()
        for cp, _ in cps:
            cp.wait_send()
        lc.wait()
        total = buf[0]
        for k in range(1, 8):
            total = total + buf[k]
        o_ref[...] = total

    vm = pl.BlockSpec(memory_space=pltpu.VMEM)
    return pl.pallas_call(
        body, name="all_reduce_small", in_specs=[vm], out_specs=vm,
        out_shape=jax.ShapeDtypeStruct(vec.shape, F32),
        scratch_shapes=[pltpu.VMEM((8,) + vec.shape, F32), pltpu.SemaphoreType.DMA((7,)),
                        pltpu.SemaphoreType.DMA((7,)), pltpu.SemaphoreType.DMA((1,))],
    )(vec)


SUM_ROWS = 256


def _add_pairs(part, theirs, name):
    four, rh, cc = theirs.shape
    tr = min(SUM_ROWS, rh)
    halves = part.reshape(four, 2, rh, cc)

    def body(a_ref, b_ref, o_ref):
        mine = a_ref[0, lax.axis_index("c")]
        o_ref[0] = (mine.astype(F32) + b_ref[0].astype(F32)).astype(o_ref.dtype)

    spec = pl.BlockSpec((1, tr, cc), lambda k, r: (k, r, 0))
    return pl.pallas_call(
        body, name=name, grid=(four, rh // tr),
        in_specs=[pl.BlockSpec((1, 2, tr, cc), lambda k, r: (k, 0, r, 0)), spec], out_specs=spec,
        out_shape=jax.ShapeDtypeStruct(theirs.shape, theirs.dtype),
        compiler_params=_params(("parallel", "parallel")),
    )(halves, theirs)


def _sum_chips(own, arrived, core, name, layer, n_layers, into=None):
    four, rh, cc = own.shape
    tr = min(SUM_ROWS, rh)
    nr = rh // tr

    def body(c_ref, own_ref, arr_ref, *rest):
        o_ref = rest[-1]
        x, y = lax.axis_index("x"), lax.axis_index("y")
        tot = own_ref[2 * x + y].astype(F32)
        for px, py in ((1 - x, y), (x, 1 - y), (1 - x, 1 - y)):
            tot = tot + arr_ref[2 * px + py].astype(F32)
        o_ref[0] = tot

    blk = pl.BlockSpec((4, tr, cc), lambda r, c_ref: (0, r, 0))
    in_specs, args, aliases = [blk, blk], [core, own, arrived], {}
    if into is not None:
        in_specs.append(pl.BlockSpec(memory_space=pl.ANY))
        args.append(into)
        aliases = {3: 0}
    return pl.pallas_call(
        body, name=name,
        grid_spec=pltpu.PrefetchScalarGridSpec(
            num_scalar_prefetch=1, grid=(nr,), in_specs=in_specs,
            out_specs=pl.BlockSpec((1, tr, cc), lambda r, c_ref: (layer, c_ref[0] * nr + r, 0))),
        out_shape=jax.ShapeDtypeStruct((n_layers, 2 * rh, cc), F32), input_output_aliases=aliases,
        compiler_params=_params(("parallel",)),
    )(*args)


ADAM_ROWS = 256


def _adamw(w, g, m, v, name):
    shape = w.shape
    cc = shape[-1]
    flat = lambda a: a.reshape(-1, cc)
    rows = flat(w).shape[0]
    tr = min(ADAM_ROWS, rows)
    assert rows % tr == 0
    c1 = 1.0 - ADAM_B1 ** ADAM_STEP
    c2 = 1.0 - ADAM_B2 ** ADAM_STEP

    def body(w_ref, g_ref, m_ref, v_ref, d_ref, nm_ref, nv_ref):
        gv = g_ref[...]
        nm = ADAM_B1 * m_ref[...] + (1.0 - ADAM_B1) * gv
        nv = ADAM_B2 * v_ref[...] + (1.0 - ADAM_B2) * (gv * gv)
        nm_ref[...] = nm
        nv_ref[...] = nv
        d_ref[...] = -ADAM_LR * ((nm / c1) / (jnp.sqrt(nv / c2) + ADAM_EPS) + ADAM_WD * w_ref[...])

    spec = pl.BlockSpec((tr, cc), lambda i: (i, 0))
    sh = jax.ShapeDtypeStruct((rows, cc), F32)
    outs = pl.pallas_call(
        body, name=name, grid=(rows // tr,), in_specs=[spec] * 4, out_specs=[spec] * 3, out_shape=[sh] * 3,
        compiler_params=_params(("parallel",)),
    )(flat(w), flat(g), flat(m), flat(v))
    return [o.reshape(shape) for o in outs]


def _pack_small(g_pre, g_post, sinks_a, b_f_c, loss_row):
    pad = lambda a: jnp.pad(a.reshape(1, -1).astype(F32), ((0, 0), (0, LANES - a.size)))
    rows = [g_pre.astype(F32).reshape(-1, LANES), g_post.astype(F32).reshape(-1, LANES), pad(sinks_a), pad(b_f_c), loss_row]
    packed = jnp.concatenate(rows, axis=0)
    return jnp.pad(packed, ((0, SMALL_ROWS - packed.shape[0]), (0, 0)))


def _unpack_small(p):
    n = DEPTH * D_MODEL // LANES
    return (p[:n].reshape(DEPTH, D_MODEL), p[n:2 * n].reshape(DEPTH, D_MODEL), p[2 * n, :2 * N_HEADS].reshape(2, N_HEADS),
            p[2 * n + 1, :N_HEADS].reshape(1, N_HEADS), p[2 * n + 2, 0])


def kernel(x, g_pre, g_post, w_in_a, w_out_a, sinks_a, w_in_b, w_out_b, w_in_c, b_f_c, w_out_c, loss_target, m_g_pre, m_g_post, m_w_in_a, m_w_out_a, m_sinks_a, m_w_in_b, m_w_out_b, m_w_in_c, m_b_f_c, m_w_out_c, v_g_pre, v_g_post, v_w_in_a, v_w_out_a, v_sinks_a, v_w_in_b, v_w_out_b, v_w_in_c, v_b_f_c, v_w_out_c):
    big_w = [w_in_a, w_out_a, w_in_b, w_out_b, w_in_c, w_out_c]
    big_m = [m_w_in_a, m_w_out_a, m_w_in_b, m_w_out_b, m_w_in_c, m_w_out_c]
    big_v = [v_w_in_a, v_w_out_a, v_w_in_b, v_w_out_b, v_w_in_c, v_w_out_c]

    chip = 2 * lax.axis_index("x") + lax.axis_index("y")
    core = lax.axis_index("c").astype(jnp.int32).reshape(1)
    by_kind = {0: (w_in_a, w_out_a), 1: (w_in_b, w_out_b), 2: (w_in_c, w_out_c)}
    shards = {}
    for i in range(DEPTH):
        kind, j = _layer_kind(i)
        shards[("in", i)] = by_kind[kind][0][j].astype(BF16)
        shards[("out", i)] = by_kind[kind][1][j].astype(BF16)

    res = _forward_backward(x[0], loss_target[0], g_pre, g_post, sinks_a, b_f_c, shards, chip, core)
    names = ["w_in_a", "w_out_a", "w_in_b", "w_out_b", "w_in_c", "w_out_c"]
    grads = _sibling_join([res["reduced"][(side, kind)] for kind in range(3) for side in ("in", "out")])

    small = _unpack_small(_all_reduce_small(
        _pack_small(res["g_pre"], res["g_post"], res["sinks_a"], res["b_f_c"], res["loss"])))
    g_small, loss = small[:4], small[4]

    zero_row = jnp.zeros((1, LANES), F32)
    pk = lambda a: _pack_small(a[0], a[1], a[2], a[3], zero_row)
    sm = _adamw(pk([g_pre, g_post, sinks_a, b_f_c]), pk(g_small), pk([m_g_pre, m_g_post, m_sinks_a, m_b_f_c]),
                pk([v_g_pre, v_g_post, v_sinks_a, v_b_f_c]), "adamw_small")
    sm = [_unpack_small(a)[:4] for a in sm]
    bigs = [_adamw(w, g, m, v, f"adamw_{nm}") for w, g, m, v, nm in zip(big_w, grads, big_m, big_v, names)]

    def ordered(small4, big6):
        return [small4[0], small4[1], big6[0], big6[1], small4[2], big6[2], big6[3], big6[4], small4[3], big6[5]]

    out = [loss, res["dx"][None], *ordered(g_small, grads)]
    for k in range(3):
        out += ordered(sm[k], [b[k] for b in bigs])
    return tuple(out)
```

```python
import functools
import math

import numpy as np
import jax
import jax.numpy as jnp
from jax import lax
from jax.experimental import pallas as pl
from jax.experimental.pallas import tpu as pltpu

F32 = jnp.float32
BF16 = jnp.bfloat16

D_MODEL = 2048
DEPTH = 4
N_HEADS = 32
HEAD_DIM = 64
LANES = 128
N_PAIRS = N_HEADS * HEAD_DIM // LANES
BRANCH = N_HEADS * HEAD_DIM
N_KV_A = 4
KV_A = N_KV_A * HEAD_DIM
WINDOW = 128
NORM_EPS = 1e-6
NEG = -1e30
Q_SCALE = HEAD_DIM ** -0.5

A_QKV = BRANCH + 2 * KV_A
B_QKV = 3 * BRANCH

ADAM_LR = 0.001
ADAM_B1 = 0.9
ADAM_B2 = 0.999
ADAM_EPS = 1e-08
ADAM_WD = 0.01
ADAM_STEP = 10

MESH = pl.DeviceIdType.MESH

_NT = (((1,), (1,)), ((), ()))
_TN = (((0,), (0,)), ((), ()))


def _params(sem=None):
    return pltpu.CompilerParams(dimension_semantics=sem)


def _matmul(a, b, *, mode, out_dtype, name, n=None, b_off=0, tm=1024, tn=1024, tk=2048):
    if mode == "nn":
        (m, k), nn = a.shape, (n or b.shape[1])
    elif mode == "nt":
        (m, k), nn = a.shape, b.shape[0]
    else:
        (k, m), nn = a.shape, b.shape[1]
    tm, tn, tk = min(tm, m), min(tn, nn), min(tk, k)
    assert m % tm == 0 and nn % tn == 0 and k % tk == 0, (name, m, nn, k, tm, tn, tk)
    nk = k // tk

    def body(a_ref, b_ref, o_ref, acc_ref):
        kk = pl.program_id(2)
        if mode == "nn":
            p = jnp.dot(a_ref[...], b_ref[...], preferred_element_type=F32)
        elif mode == "nt":
            p = lax.dot_general(a_ref[...], b_ref[...], _NT, preferred_element_type=F32)
        else:
            p = lax.dot_general(a_ref[...], b_ref[...], _TN, preferred_element_type=F32)
        if nk == 1:
            o_ref[...] = p.astype(o_ref.dtype)
        else:
            @pl.when(kk == 0)
            def _():
                acc_ref[...] = p

            @pl.when(kk > 0)
            def _():
                acc_ref[...] += p

            @pl.when(kk == nk - 1)
            def _():
                o_ref[...] = acc_ref[...].astype(o_ref.dtype)

    if mode == "nn":
        in_specs = [pl.BlockSpec((tm, tk), lambda i, j, kk: (i, kk)),
                    pl.BlockSpec((tk, tn), lambda i, j, kk: (kk, j + b_off))]
    elif mode == "nt":
        in_specs = [pl.BlockSpec((tm, tk), lambda i, j, kk: (i, kk)),
                    pl.BlockSpec((tn, tk), lambda i, j, kk: (j, kk))]
    else:
        in_specs = [pl.BlockSpec((tk, tm), lambda i, j, kk: (kk, i)),
                    pl.BlockSpec((tk, tn), lambda i, j, kk: (kk, j))]
    return pl.pallas_call(
        body, name=name, grid=(m // tm, nn // tn, nk),
        in_specs=in_specs,
        out_specs=pl.BlockSpec((tm, tn), lambda i, j, kk: (i, j)),
        out_shape=jax.ShapeDtypeStruct((m, nn), out_dtype),
        scratch_shapes=[pltpu.VMEM((tm, tn), F32)],
        compiler_params=_params(("parallel", "parallel", "arbitrary")),
    )(a, b)


ROW_TILE = 256


def _row_call(body, name, ins, outs, *, s):
    tr = min(ROW_TILE, s)
    spec = {"row": lambda sh: pl.BlockSpec((tr, sh[1]), lambda i: (i, 0)),
            "vec": lambda sh: pl.BlockSpec((1, sh[1]), lambda i: (0, 0)),
            "col": lambda sh: pl.BlockSpec((sh[0], tr), lambda i: (0, i))}
    in_specs = [spec[kind](a.shape) for a, kind in ins]
    out_specs = [spec[kind](sh.shape) for sh, kind in outs]
    return pl.pallas_call(
        body, name=name, grid=(s // tr,), in_specs=in_specs, out_specs=out_specs,
        out_shape=[sh for sh, _ in outs],
        compiler_params=_params(("arbitrary",)),
    )(*[a for a, _ in ins])


def _rsqrt_ms(v):
    return lax.rsqrt(jnp.mean(v * v, axis=-1, keepdims=True) + NORM_EPS)


def _rmsnorm_fwd(x, g, name):
    s, d = x.shape

    def body(x_ref, g_ref, h_ref, ht_ref):
        xv = x_ref[...]
        h = xv * _rsqrt_ms(xv) * g_ref[...]
        h_ref[...] = h.astype(BF16)
        ht_ref[...] = h.T.astype(BF16)

    return _row_call(body, name, [(x, "row"), (g, "vec")],
                     [(jax.ShapeDtypeStruct((s, d), BF16), "row"), (jax.ShapeDtypeStruct((d, s), BF16), "col")], s=s)


def _gate_fwd(o, z, name):
    s, d = o.shape

    def body(o_ref, z_ref, u_ref, ut_ref):
        zv = z_ref[...]
        u = o_ref[...] * (zv * jax.nn.sigmoid(zv))
        u_ref[...] = u.astype(BF16)
        ut_ref[...] = u.T.astype(BF16)

    return _row_call(body, name, [(o, "row"), (z, "row")],
                     [(jax.ShapeDtypeStruct((s, d), BF16), "row"), (jax.ShapeDtypeStruct((d, s), BF16), "col")], s=s)


def _post_fwd(x, y, g, name):
    s, d = x.shape

    def body(x_ref, y_ref, g_ref, o_ref):
        yv = y_ref[...]
        o_ref[...] = x_ref[...] + yv * _rsqrt_ms(yv) * g_ref[...]

    return _row_call(body, name, [(x, "row"), (y, "row"), (g, "vec")],
                     [(jax.ShapeDtypeStruct((s, d), F32), "row")], s=s)[0]


def _loss_and_grad(x, target):
    s, d = x.shape

    def body(x_ref, t_ref, dx_ref, l_ref):
        err = x_ref[...] - t_ref[...]
        dx_ref[...] = err * (1.0 / d)
        part = jnp.sum(jnp.sum(err * err, axis=1, keepdims=True), axis=0, keepdims=True) * (0.5 / d)

        @pl.when(pl.program_id(0) == 0)
        def _():
            l_ref[...] = jnp.zeros_like(l_ref)

        l_ref[...] += jnp.broadcast_to(part, l_ref.shape)

    return _row_call(body, "loss_head", [(x, "row"), (target, "row")],
                     [(jax.ShapeDtypeStruct((s, d), F32), "row"),
                      (jax.ShapeDtypeStruct((1, LANES), F32), "vec")], s=s)


def _norm_bwd_rows(dn, v, g):
    r = _rsqrt_ms(v)
    a = dn * g
    dv = r * (a - v * (r * r) * jnp.mean(a * v, axis=-1, keepdims=True))
    return dv, dn * v * r


def _post_bwd(dx, y, g, name):
    s, d = dx.shape

    def body(dx_ref, y_ref, g_ref, dy_ref, dg_ref):
        dy, dg = _norm_bwd_rows(dx_ref[...], y_ref[...], g_ref[...])
        dy_ref[...] = dy.astype(BF16)

        @pl.when(pl.program_id(0) == 0)
        def _():
            dg_ref[...] = jnp.zeros_like(dg_ref)

        dg_ref[...] += jnp.sum(dg, axis=0, keepdims=True)

    return _row_call(body, name, [(dx, "row"), (y, "row"), (g, "vec")],
                     [(jax.ShapeDtypeStruct((s, d), BF16), "row"),
                      (jax.ShapeDtypeStruct((1, d), F32), "vec")], s=s)


def _gate_bwd(du, o, z, name):
    s, d = du.shape

    def body(du_ref, o_ref, z_ref, do_ref, dz_ref):
        duv, zv = du_ref[...], z_ref[...]
        sig = jax.nn.sigmoid(zv)
        do_ref[...] = (duv * (zv * sig)).astype(BF16)
        dz_ref[...] = (duv * o_ref[...] * (sig * (1.0 + zv * (1.0 - sig)))).astype(BF16)

    return _row_call(body, name, [(du, "row"), (o, "row"), (z, "row")],
                     [(jax.ShapeDtypeStruct((s, d), BF16), "row"),
                      (jax.ShapeDtypeStruct((s, d), BF16), "row")], s=s)


def _pre_bwd(dx, dhs, x, g, name):
    s, d = dx.shape
    n_dh = len(dhs)

    def body(*refs):
        dx_ref, dh_refs, (x_ref, g_ref, o_ref, dg_ref) = refs[0], refs[1:1 + n_dh], refs[1 + n_dh:]
        dh = dh_refs[0][...].astype(F32)
        for r in dh_refs[1:]:
            dh = dh + r[...].astype(F32)
        dv, dg = _norm_bwd_rows(dh, x_ref[...], g_ref[...])
        o_ref[...] = dx_ref[...] + dv

        @pl.when(pl.program_id(0) == 0)
        def _():
            dg_ref[...] = jnp.zeros_like(dg_ref)

        dg_ref[...] += jnp.sum(dg, axis=0, keepdims=True)

    return _row_call(body, name, [(dx, "row")] + [(h, "row") for h in dhs] + [(x, "row"), (g, "vec")],
                     [(jax.ShapeDtypeStruct((s, d), F32), "row"),
                      (jax.ShapeDtypeStruct((1, d), F32), "vec")], s=s)


def _lane_is_first_head():
    return lax.broadcasted_iota(jnp.int32, (1, LANES), 1) < HEAD_DIM


def _bcast_lanes(col):
    return jnp.broadcast_to(col, (col.shape[0], LANES))


def _pair_spec(s, off=0, width=LANES):
    return pl.BlockSpec((s, width), lambda p: (0, p + off))


def _stack_heads(pair, first):
    return jnp.concatenate([jnp.where(first, pair, 0), jnp.where(first, 0, pair)], axis=0).astype(BF16)


def _stacked_mask(t, strict):
    row = lax.broadcasted_iota(jnp.int32, (2 * t, t), 0)
    col = lax.broadcasted_iota(jnp.int32, (2 * t, t), 1)
    query = jnp.where(row >= t, row - t, row)
    return col < query if strict else col <= query


def _two_at_a_time(n, step, carry):
    carry = lax.fori_loop(0, n // 2, lambda jj, c: step(2 * jj + 1, step(2 * jj, c)), carry)
    return lax.fori_loop(2 * (n // 2), n, step, carry)


def _rowsum_heads(prod, first):
    return (jnp.sum(jnp.where(first, prod, 0.0), axis=1, keepdims=True),
            jnp.sum(jnp.where(first, 0.0, prod), axis=1, keepdims=True))


def _softplus_parts(z):
    e = jnp.exp(-jnp.abs(z))
    sp = jnp.maximum(z, 0.0) + jnp.log(1.0 + e)
    r = 1.0 / (1.0 + e)
    return sp, jnp.where(z >= 0, r, e * r)


def _split_dot(x, t):
    hi = x.astype(BF16)
    lo = (x - hi.astype(F32)).astype(BF16)
    return jnp.dot(hi, t, preferred_element_type=F32) + jnp.dot(lo, t, preferred_element_type=F32)


def _sb_tile(s):
    return min(256, s)


def _attn_b_fwd(qkv, name, exchange=None):
    s = qkv.shape[0]
    t = _sb_tile(s)
    nq = s // t

    def body(q_ref, k_ref, v_ref, o_ref, lt_ref):
        first = _lane_is_first_head()
        before = _stacked_mask(t, strict=True)
        tri = (lax.broadcasted_iota(jnp.int32, (t, t), 0) >= lax.broadcasted_iota(jnp.int32, (t, t), 1)).astype(BF16)

        def tile(j, carry, diag, qs):
            c, acc = carry
            c0 = pl.multiple_of(j * t, t)
            k2 = k_ref[pl.ds(c0, t), :]
            v2 = v_ref[pl.ds(c0, t), :]
            z = lax.dot_general(qs, k2, _NT, preferred_element_type=F32)
            sp, _ = _softplus_parts(z)
            lf = jnp.where(before, -sp, 0.0) if diag else -sp
            incl = jnp.dot(lf.astype(BF16), tri, preferred_element_type=F32)
            a = jnp.exp(z + c + incl)
            if diag:
                a = jnp.where(before, a, 0.0)
            pv = jnp.dot(a.astype(BF16), v2, preferred_element_type=F32)
            return c + incl[:, 0:1], acc + jnp.where(first, pv[:t], pv[t:])

        def qblock(i, _):
            r0 = pl.multiple_of(i * t, t)
            qs = _stack_heads(q_ref[pl.ds(r0, t), :] * Q_SCALE, first)
            carry = tile(i, (jnp.zeros((2 * t, 1), F32), jnp.zeros((t, LANES), F32)), True, qs)
            carry = _two_at_a_time(i, lambda j, c: tile(i - 1 - j, c, False, qs), carry)
            o_ref[pl.ds(r0, t), :] = carry[1]
            lt_ref[pl.ds(r0, t), 0:LANES] = _bcast_lanes(carry[0][:t])
            lt_ref[pl.ds(r0, t), LANES:2 * LANES] = _bcast_lanes(carry[0][t:])
            return 0

        lax.fori_loop(0, nq, qblock, 0)

    return _grid_call(
        body, name=name, grid=(N_PAIRS,),
        in_specs=[_pair_spec(s), _pair_spec(s, N_PAIRS), _pair_spec(s, 2 * N_PAIRS)],
        out_specs=[_pair_spec(s), _stat_spec(s)],
        out_shape=[jax.ShapeDtypeStruct((s, BRANCH), F32), jax.ShapeDtypeStruct((s, N_HEADS * LANES), F32)],
        args=(qkv, qkv, qkv), semantics=("parallel",), exchange=exchange)


def _attn_b_bwd(qkv, ltot, do, name, exchange=None):
    s = qkv.shape[0]
    t = _sb_tile(s)
    nq = s // t

    def body(q_ref, k_ref, v_ref, lt_ref, do_ref, dq_ref, dk_ref, dv_ref, dk_acc, dv_acc):
        first = _lane_is_first_head()
        before = _stacked_mask(t, strict=True)
        tri = (lax.broadcasted_iota(jnp.int32, (t, t), 0) <= lax.broadcasted_iota(jnp.int32, (t, t), 1)).astype(BF16)
        dk_acc[...] = jnp.zeros_like(dk_acc)
        dv_acc[...] = jnp.zeros_like(dv_acc)

        def tile(j, carry, diag, qs, dos, lt):
            p_l, p_g, dq_acc = carry
            c0 = pl.multiple_of(j * t, t)
            k2 = k_ref[pl.ds(c0, t), :]
            v2 = v_ref[pl.ds(c0, t), :]
            z = lax.dot_general(qs, k2, _NT, preferred_element_type=F32)
            sp, sig = _softplus_parts(z)
            lf = jnp.where(before, -sp, 0.0) if diag else -sp
            pref_l = jnp.dot(lf.astype(BF16), tri, preferred_element_type=F32)
            a = jnp.exp(z + ((lt - p_l) - pref_l + lf))
            if diag:
                a = jnp.where(before, a, 0.0)
            g = a * lax.dot_general(dos, v2, _NT, preferred_element_type=F32)
            pref_g = jnp.dot(g.astype(BF16), tri, preferred_element_type=F32)
            dz = g - sig * (p_g + pref_g)
            if diag:
                dz = jnp.where(before, dz, 0.0)
            dzb = dz.astype(BF16)
            dq = jnp.dot(dzb, k2, preferred_element_type=F32)
            dk_acc[pl.ds(c0, t), :] += lax.dot_general(dzb, qs, _TN, preferred_element_type=F32)
            dv_acc[pl.ds(c0, t), :] += lax.dot_general(a.astype(BF16), dos, _TN, preferred_element_type=F32)
            return p_l + pref_l[:, t - 1:t], p_g + pref_g[:, t - 1:t], dq_acc + jnp.where(first, dq[:t], dq[t:])

        def qblock(i, _):
            r0 = pl.multiple_of(i * t, t)
            qs = _stack_heads(q_ref[pl.ds(r0, t), :] * Q_SCALE, first)
            dos = _stack_heads(do_ref[pl.ds(r0, t), :], first)
            lt = jnp.concatenate([lt_ref[pl.ds(r0, t), 0:1], lt_ref[pl.ds(r0, t), LANES:LANES + 1]], axis=0)
            zero = jnp.zeros((2 * t, 1), F32)
            carry = (zero, zero, jnp.zeros((t, LANES), F32))
            carry = _two_at_a_time(i, lambda j, c: tile(j, c, False, qs, dos, lt), carry)
            carry = tile(i, carry, True, qs, dos, lt)
            dq_ref[pl.ds(r0, t), :] = (carry[2] * Q_SCALE).astype(BF16)
            return 0

        lax.fori_loop(0, nq, qblock, 0)
        dk_ref[...] = dk_acc[...].astype(BF16)
        dv_ref[...] = dv_acc[...].astype(BF16)

    out = jax.ShapeDtypeStruct((s, BRANCH), BF16)
    return _grid_call(
        body, name=name, grid=(N_PAIRS,),
        in_specs=[_pair_spec(s), _pair_spec(s, N_PAIRS), _pair_spec(s, 2 * N_PAIRS), _stat_spec(s), _pair_spec(s)],
        out_specs=[_pair_spec(s)] * 3, out_shape=[out] * 3,
        scratch_shapes=[pltpu.VMEM((s, LANES), F32), pltpu.VMEM((s, LANES), F32)],
        args=(qkv, qkv, qkv, ltot, do), semantics=("parallel",), exchange=exchange)


def _fox_tile(s):
    return min(256, s)


def _stat_spec(s):
    return pl.BlockSpec((s, 2 * LANES), lambda p: (0, p))


def _cum_spec(nt, t):
    return pl.BlockSpec((1, nt, 2, t), lambda p: (p, 0, 0, 0))


def _attn_c_fwd(qkv, cum4, name, exchange=None):
    s = qkv.shape[0]
    t = _fox_tile(s)
    nq = s // t

    def body(q_ref, k_ref, v_ref, c_ref, o_ref, lse_ref):
        first = _lane_is_first_head()
        causal = _stacked_mask(t, strict=False)

        def tile(j, carry, diag, qs):
            c0 = pl.multiple_of(j * t, t)
            k2 = k_ref[pl.ds(c0, t), :]
            v2 = v_ref[pl.ds(c0, t), :]
            cs = c_ref[0, j]
            m_prev, l_prev, acc = carry
            z = lax.dot_general(qs, k2, _NT, preferred_element_type=F32)
            sc = jnp.concatenate([z[:t] - cs[0:1, :], z[t:] - cs[1:2, :]], axis=0)
            if diag:
                sc = jnp.where(causal, sc, NEG)
            m_new = jnp.maximum(m_prev, jnp.max(sc, axis=1, keepdims=True))
            alpha = jnp.exp(m_prev - m_new)
            p = jnp.exp(sc - m_new)
            l_new = alpha * l_prev + jnp.sum(p, axis=1, keepdims=True)
            pv = jnp.dot(p.astype(BF16), v2, preferred_element_type=F32)
            acc = jnp.where(first, acc * alpha[:t] + pv[:t], acc * alpha[t:] + pv[t:])
            return m_new, l_new, acc

        def qblock(i, _):
            r0 = pl.multiple_of(i * t, t)
            qs = _stack_heads(q_ref[pl.ds(r0, t), :] * Q_SCALE, first)
            carry = (jnp.full((2 * t, 1), NEG, F32), jnp.zeros((2 * t, 1), F32), jnp.zeros((t, LANES), F32))
            carry = _two_at_a_time(i, lambda j, c: tile(j, c, False, qs), carry)
            m, l, acc = tile(i, carry, True, qs)
            inv = 1.0 / l
            lse = m + jnp.log(l)
            o_ref[pl.ds(r0, t), :] = acc * jnp.where(first, inv[:t], inv[t:])
            lse_ref[pl.ds(r0, t), 0:LANES] = _bcast_lanes(lse[:t])
            lse_ref[pl.ds(r0, t), LANES:2 * LANES] = _bcast_lanes(lse[t:])
            return 0

        lax.fori_loop(0, nq, qblock, 0)

    return _grid_call(
        body, name=name, grid=(N_PAIRS,),
        in_specs=[_pair_spec(s), _pair_spec(s, N_PAIRS), _pair_spec(s, 2 * N_PAIRS), _cum_spec(nq, t)],
        out_specs=[_pair_spec(s), _stat_spec(s)],
        out_shape=[jax.ShapeDtypeStruct((s, BRANCH), F32), jax.ShapeDtypeStruct((s, N_HEADS * LANES), F32)],
        args=(qkv, qkv, qkv, cum4), semantics=("parallel",), exchange=exchange)


def _attn_c_bwd(qkv, cum4, o, lse, do, name, exchange=None):
    s = qkv.shape[0]
    t = _fox_tile(s)
    nq = s // t

    def body(q_ref, k_ref, v_ref, c_ref, o_ref, lse_ref, do_ref, dq_ref, dk_ref, dv_ref, dc_ref, dk_acc, dv_acc):
        first = _lane_is_first_head()
        causal = _stacked_mask(t, strict=False)
        eye = lax.broadcasted_iota(jnp.int32, (t, t), 0) == lax.broadcasted_iota(jnp.int32, (t, t), 1)
        dk_acc[...] = jnp.zeros_like(dk_acc)
        dv_acc[...] = jnp.zeros_like(dv_acc)
        dc_ref[...] = jnp.zeros_like(dc_ref)

        def tile(j, carry, diag, qs, dos, delta, lse):
            dq_acc, rs = carry
            c0 = pl.multiple_of(j * t, t)
            k2 = k_ref[pl.ds(c0, t), :]
            v2 = v_ref[pl.ds(c0, t), :]
            cs = c_ref[0, j]
            z = lax.dot_general(qs, k2, _NT, preferred_element_type=F32)
            sc = jnp.concatenate([z[:t] - cs[0:1, :], z[t:] - cs[1:2, :]], axis=0)
            p = jnp.exp(sc - lse)
            if diag:
                p = jnp.where(causal, p, 0.0)
            ds = p * (lax.dot_general(dos, v2, _NT, preferred_element_type=F32) - delta)
            dsb = ds.astype(BF16)
            dq = jnp.dot(dsb, k2, preferred_element_type=F32)
            dk_acc[pl.ds(c0, t), :] += lax.dot_general(dsb, qs, _TN, preferred_element_type=F32)
            dv_acc[pl.ds(c0, t), :] += lax.dot_general(p.astype(BF16), dos, _TN, preferred_element_type=F32)
            col_sums = jnp.concatenate([jnp.sum(ds[:t], axis=0, keepdims=True), jnp.sum(ds[t:], axis=0, keepdims=True)], axis=0)
            dc_ref[0, j] = dc_ref[0, j] - col_sums
            return dq_acc + jnp.where(first, dq[:t], dq[t:]), rs + jnp.sum(ds, axis=1, keepdims=True)

        def qblock(i, _):
            r0 = pl.multiple_of(i * t, t)
            do2 = do_ref[pl.ds(r0, t), :]
            qs = _stack_heads(q_ref[pl.ds(r0, t), :] * Q_SCALE, first)
            dos = _stack_heads(do2, first)
            delta = jnp.concatenate(_rowsum_heads(do2.astype(F32) * o_ref[pl.ds(r0, t), :], first), axis=0)
            lse = jnp.concatenate([lse_ref[pl.ds(r0, t), 0:1], lse_ref[pl.ds(r0, t), LANES:LANES + 1]], axis=0)
            carry = (jnp.zeros((t, LANES), F32), jnp.zeros((2 * t, 1), F32))
            carry = _two_at_a_time(i, lambda j, c: tile(j, c, False, qs, dos, delta, lse), carry)
            dq_acc, rs = tile(i, carry, True, qs, dos, delta, lse)
            dq_ref[pl.ds(r0, t), :] = (dq_acc * Q_SCALE).astype(BF16)
            as_row = lambda col_vec: jnp.sum(jnp.where(eye, col_vec, 0.0), axis=0, keepdims=True)
            dc_ref[0, i] = dc_ref[0, i] + jnp.concatenate([as_row(rs[:t]), as_row(rs[t:])], axis=0)
            return 0

        lax.fori_loop(0, nq, qblock, 0)
        dk_ref[...] = dk_acc[...].astype(BF16)
        dv_ref[...] = dv_acc[...].astype(BF16)

    out = jax.ShapeDtypeStruct((s, BRANCH), BF16)
    return _grid_call(
        body, name=name, grid=(N_PAIRS,),
        in_specs=[_pair_spec(s), _pair_spec(s, N_PAIRS), _pair_spec(s, 2 * N_PAIRS), _cum_spec(nq, t),
                  _pair_spec(s), _stat_spec(s), _pair_spec(s)],
        out_specs=[_pair_spec(s)] * 3 + [_cum_spec(nq, t)],
        out_shape=[out] * 3 + [jax.ShapeDtypeStruct(cum4.shape, F32)],
        scratch_shapes=[pltpu.VMEM((s, LANES), F32), pltpu.VMEM((s, LANES), F32)],
        args=(qkv, qkv, qkv, cum4, o, lse, do), semantics=("parallel",), exchange=exchange)


FG_CHUNK = 512


def _tri_dot3(x, t):
    hi = x.astype(BF16)
    r1 = x - hi.astype(F32)
    mid = r1.astype(BF16)
    lo = (r1 - mid.astype(F32)).astype(BF16)
    return (jnp.dot(hi, t, preferred_element_type=F32) + jnp.dot(mid, t, preferred_element_type=F32)
            + jnp.dot(lo, t, preferred_element_type=F32))


def _fgate_fwd(h, wf_t, b_col, name):
    s = h.shape[0]
    c = min(FG_CHUNK, s)

    def body(h_ref, w_ref, b_ref, xf_ref, cum_ref, carry_ref):
        @pl.when(pl.program_id(0) == 0)
        def _():
            carry_ref[...] = jnp.zeros_like(carry_ref)

        xf = lax.dot_general(w_ref[...], h_ref[...], _NT, preferred_element_type=F32) + b_ref[:, 0:1]
        xf_ref[...] = xf
        logf = jnp.minimum(xf, 0.0) - jnp.log(1.0 + jnp.exp(-jnp.abs(xf)))
        row = lax.broadcasted_iota(jnp.int32, (c, c), 0)
        col = lax.broadcasted_iota(jnp.int32, (c, c), 1)
        cum = _tri_dot3(logf, (row <= col).astype(BF16)) + carry_ref[:, 0:1]
        cum_ref[...] = cum
        carry_ref[...] = _bcast_lanes(cum[:, c - 1:c])

    out = jax.ShapeDtypeStruct((N_HEADS, s), F32)
    return pl.pallas_call(
        body, name=name, grid=(s // c,),
        in_specs=[pl.BlockSpec((c, D_MODEL), lambda i: (i, 0)),
                  pl.BlockSpec((N_HEADS, D_MODEL), lambda i: (0, 0)),
                  pl.BlockSpec((N_HEADS, LANES), lambda i: (0, 0))],
        out_specs=[pl.BlockSpec((N_HEADS, c), lambda i: (0, i))] * 2,
        out_shape=[out, out],
        scratch_shapes=[pltpu.VMEM((N_HEADS, LANES), F32)],
        compiler_params=_params(("arbitrary",)),
    )(h, wf_t, b_col)


def _fgate_bwd(dcum, xf, h, wf_t, name):
    s = h.shape[0]
    c = min(FG_CHUNK, s)
    n = s // c

    def body(dc_ref, xf_ref, h_ref, w_ref, dw_ref, dh_ref, db_ref, carry_ref):
        @pl.when(pl.program_id(0) == 0)
        def _():
            carry_ref[...] = jnp.zeros_like(carry_ref)
            dw_ref[...] = jnp.zeros_like(dw_ref)
            db_ref[...] = jnp.zeros_like(db_ref)

        row = lax.broadcasted_iota(jnp.int32, (c, c), 0)
        col = lax.broadcasted_iota(jnp.int32, (c, c), 1)
        dlogf = _tri_dot3(dc_ref[...], (row >= col).astype(BF16)) + carry_ref[:, 0:1]
        carry_ref[...] = _bcast_lanes(dlogf[:, 0:1])
        xf = xf_ref[...]
        e = jnp.exp(-jnp.abs(xf))
        r = 1.0 / (1.0 + e)
        dxf = dlogf * jnp.where(xf >= 0, e * r, r)
        db_ref[...] += _bcast_lanes(jnp.sum(dxf, axis=1, keepdims=True))
        dxb = dxf.astype(BF16)
        dw_ref[...] += jnp.dot(dxb, h_ref[...], preferred_element_type=F32)
        dh_ref[...] = lax.dot_general(dxb, w_ref[...], _TN, preferred_element_type=F32)

    rev = lambda i: n - 1 - i
    return pl.pallas_call(
        body, name=name, grid=(n,),
        in_specs=[pl.BlockSpec((N_HEADS, c), lambda i: (0, rev(i))),
                  pl.BlockSpec((N_HEADS, c), lambda i: (0, rev(i))),
                  pl.BlockSpec((c, D_MODEL), lambda i: (rev(i), 0)),
                  pl.BlockSpec((N_HEADS, D_MODEL), lambda i: (0, 0))],
        out_specs=[pl.BlockSpec((N_HEADS, D_MODEL), lambda i: (0, 0)),
                   pl.BlockSpec((c, D_MODEL), lambda i: (rev(i), 0)),
                   pl.BlockSpec((N_HEADS, LANES), lambda i: (0, 0))],
        out_shape=[jax.ShapeDtypeStruct((N_HEADS, D_MODEL), F32), jax.ShapeDtypeStruct((s, D_MODEL), F32),
                   jax.ShapeDtypeStruct((N_HEADS, LANES), F32)],
        scratch_shapes=[pltpu.VMEM((N_HEADS, LANES), F32)],
        compiler_params=_params(("arbitrary",)),
    )(dcum, xf, h, wf_t)


def _to_cum4(v, t):
    s = v.shape[1]
    return v.reshape(N_PAIRS, 2, s // t, t).transpose(0, 2, 1, 3)


def _from_cum4(v4):
    p, nt, two, t = v4.shape
    return v4.transpose(0, 2, 1, 3).reshape(p * two, nt * t)


def _alibi_slopes():
    return (2.0 ** (-8.0 * np.arange(1, N_HEADS + 1, dtype=np.float32) / N_HEADS)).astype(np.float32)


def _per_head_lanes(v):
    return jnp.repeat(v.astype(F32).reshape(N_PAIRS, 1, 2), LANES, axis=2)


def _attn_a_specs(s):
    q = _pair_spec(s)
    k = pl.BlockSpec((s, LANES), lambda p: (0, N_PAIRS + p // 8))
    v = pl.BlockSpec((s, LANES), lambda p: (0, N_PAIRS + KV_A // LANES + p // 8))
    head = pl.BlockSpec((1, 1, 2 * LANES), lambda p: (p, 0, 0))
    return q, k, v, head


def _attn_a_geometry(p, slope_ref, sink_ref):
    kv_half = (p // 4) % 2
    kv_first = kv_half == 0
    lane_first = _lane_is_first_head()
    kv_lanes = (lax.broadcasted_iota(jnp.int32, (1, LANES), 1) // HEAD_DIM) == kv_half
    row = lax.broadcasted_iota(jnp.int32, (2 * WINDOW, 2 * WINDOW), 0)
    cj = lax.broadcasted_iota(jnp.int32, (2 * WINDOW, 2 * WINDOW), 1)
    second = row >= WINDOW
    dist = WINDOW + jnp.where(second, row - WINDOW, row) - cj
    valid = (dist >= 0) & (dist < WINDOW)
    per_row = lambda ref: jnp.where(second[:, 0:1], ref[0, :, LANES:LANES + 1], ref[0, :, 0:1])
    return kv_first, lane_first, kv_lanes, per_row(slope_ref) * dist.astype(F32), valid, per_row(sink_ref)


def _swap_halves(x):
    return pltpu.roll(x, HEAD_DIM, 1)


def _attn_a_fwd(qkv, slopes, sinks, name, exchange=None):
    s = qkv.shape[0]
    nb = s // WINDOW

    def body(q_ref, k_ref, v_ref, sl_ref, sk_ref, o_ref, lse_ref):
        kv_first, lane_first, kv_lanes, bias, valid, sink = _attn_a_geometry(pl.program_id(0), sl_ref, sk_ref)

        def block(r0, k0, width):
            q2 = q_ref[pl.ds(r0, WINDOW), :].astype(F32) * Q_SCALE
            q2r = _swap_halves(q2)
            xs = jnp.concatenate([jnp.where(kv_first, q2, q2r), jnp.where(kv_first, q2r, q2)], axis=0).astype(BF16)
            km = jnp.where(kv_lanes, k_ref[pl.ds(k0, width), :], 0).astype(BF16)
            vm = jnp.where(kv_lanes, v_ref[pl.ds(k0, width), :], 0).astype(BF16)
            sc = lax.dot_general(xs, km, _NT, preferred_element_type=F32) - bias[:, 2 * WINDOW - width:]
            sc = jnp.where(valid[:, 2 * WINDOW - width:], sc, NEG)
            m = jnp.maximum(jnp.max(sc, axis=1, keepdims=True), sink)
            pr = jnp.exp(sc - m)
            l = jnp.sum(pr, axis=1, keepdims=True) + jnp.exp(sink - m)
            os = jnp.dot(pr.astype(BF16), vm, preferred_element_type=F32) * (1.0 / l)
            lse = m + jnp.log(l)
            lse_ref[pl.ds(r0, WINDOW), 0:LANES] = _bcast_lanes(lse[:WINDOW])
            lse_ref[pl.ds(r0, WINDOW), LANES:2 * LANES] = _bcast_lanes(lse[WINDOW:])
            oa = jnp.where(kv_first, os[:WINDOW], _swap_halves(os[:WINDOW]))
            ob = jnp.where(kv_first, _swap_halves(os[WINDOW:]), os[WINDOW:])
            o_ref[pl.ds(r0, WINDOW), :] = jnp.where(lane_first, oa, ob)

        block(0, 0, WINDOW)

        def loop(n, _):
            r0 = pl.multiple_of(n * WINDOW, WINDOW)
            block(r0, pl.multiple_of(r0 - WINDOW, WINDOW), 2 * WINDOW)
            return 0

        _two_at_a_time(nb - 1, lambda n, c: loop(n + 1, c), 0)

    q, k, v, head = _attn_a_specs(s)
    return _grid_call(
        body, name=name, grid=(N_PAIRS,),
        in_specs=[q, k, v, head, head],
        out_specs=[_pair_spec(s), _stat_spec(s)],
        out_shape=[jax.ShapeDtypeStruct((s, BRANCH), F32), jax.ShapeDtypeStruct((s, N_HEADS * LANES), F32)],
        args=(qkv, qkv, qkv, slopes, sinks), semantics=("parallel",), exchange=exchange)


def _attn_a_bwd(qkv, slopes, sinks, o, lse, do, name, exchange=None):
    s = qkv.shape[0]
    nb = s // WINDOW

    def body(q_ref, k_ref, v_ref, sl_ref, sk_ref, o_ref, lse_ref, do_ref, dq_ref, dk_ref, dv_ref, dsk_ref):
        p_id = pl.program_id(0)
        kv_first, lane_first, kv_lanes, bias, valid, sink = _attn_a_geometry(p_id, sl_ref, sk_ref)

        @pl.when(p_id % 8 == 0)
        def _():
            dk_ref[...] = jnp.zeros_like(dk_ref)
            dv_ref[...] = jnp.zeros_like(dv_ref)

        def align(v2):
            v2r = _swap_halves(v2)
            both = jnp.concatenate([jnp.where(kv_first, v2, v2r), jnp.where(kv_first, v2r, v2)], axis=0)
            return jnp.where(kv_lanes, both, 0.0).astype(BF16)

        def block(r0, k0, width, sink_sum):
            xq = align(q_ref[pl.ds(r0, WINDOW), :].astype(F32) * Q_SCALE)
            do2 = do_ref[pl.ds(r0, WINDOW), :].astype(F32)
            xdo = align(do2)
            delta = jnp.concatenate(_rowsum_heads(do2 * o_ref[pl.ds(r0, WINDOW), :], lane_first), axis=0)
            lse = jnp.concatenate([lse_ref[pl.ds(r0, WINDOW), 0:1], lse_ref[pl.ds(r0, WINDOW), LANES:LANES + 1]], axis=0)
            km = jnp.where(kv_lanes, k_ref[pl.ds(k0, width), :], 0).astype(BF16)
            vm = jnp.where(kv_lanes, v_ref[pl.ds(k0, width), :], 0).astype(BF16)
            sc = lax.dot_general(xq, km, _NT, preferred_element_type=F32) - bias[:, 2 * WINDOW - width:]
            pr = jnp.where(valid[:, 2 * WINDOW - width:], jnp.exp(sc - lse), 0.0)
            ds = pr * (lax.dot_general(xdo, vm, _NT, preferred_element_type=F32) - delta)
            dsb = ds.astype(BF16)
            dq_al = jnp.dot(dsb, km, preferred_element_type=F32)
            dk_ref[pl.ds(k0, width), :] += lax.dot_general(dsb, xq, _TN, preferred_element_type=F32)
            dv_ref[pl.ds(k0, width), :] += lax.dot_general(pr.astype(BF16), xdo, _TN, preferred_element_type=F32)
            dqa = jnp.where(kv_first, dq_al[:WINDOW], _swap_halves(dq_al[:WINDOW]))
            dqb = jnp.where(kv_first, _swap_halves(dq_al[WINDOW:]), dq_al[WINDOW:])
            dq_ref[pl.ds(r0, WINDOW), :] = (jnp.where(lane_first, dqa, dqb) * Q_SCALE).astype(BF16)
            return sink_sum + jnp.exp(sink - lse) * delta

        sink_sum = block(0, 0, WINDOW, jnp.zeros((2 * WINDOW, 1), F32))

        def loop(n, c):
            r0 = pl.multiple_of(n * WINDOW, WINDOW)
            return block(r0, pl.multiple_of(r0 - WINDOW, WINDOW), 2 * WINDOW, c)

        sink_sum = _two_at_a_time(nb - 1, lambda n, c: loop(n + 1, c), sink_sum)
        dsk_ref[0, :, 0:LANES] = jnp.broadcast_to(-jnp.sum(sink_sum[:WINDOW], axis=0, keepdims=True), (1, LANES))
        dsk_ref[0, :, LANES:2 * LANES] = jnp.broadcast_to(-jnp.sum(sink_sum[WINDOW:], axis=0, keepdims=True), (1, LANES))

    q, k, v, head = _attn_a_specs(s)
    kv_out = pl.BlockSpec((s, LANES), lambda p: (0, p // 8))
    return _grid_call(
        body, name=name, grid=(N_PAIRS,),
        in_specs=[q, k, v, head, head, _pair_spec(s), _stat_spec(s), _pair_spec(s)],
        out_specs=[_pair_spec(s), kv_out, kv_out, head],
        out_shape=[jax.ShapeDtypeStruct((s, BRANCH), BF16), jax.ShapeDtypeStruct((s, KV_A), F32),
                   jax.ShapeDtypeStruct((s, KV_A), F32), jax.ShapeDtypeStruct((N_PAIRS, 1, 2 * LANES), F32)],
        args=(qkv, qkv, qkv, slopes, sinks, o, lse, do), semantics=("arbitrary",), exchange=exchange)


def _layer_kind(i):
    return i % 3, i // 3


GATHER_FIRST = [("in", 0)]
GATHER_BEHIND = {0: [("out", 0), ("in", 1)], 1: [("out", 1), ("in", 2), ("out", 2)], 2: [("in", 3), ("out", 3)]}


def _forward_backward(x, target, g_pre, g_post, sinks_a, b_f_c, shards, chip, core):
    s = x.shape[0]
    slopes = _per_head_lanes(jnp.asarray(_alibi_slopes()))
    w_in, w_out, wf_t = {}, {}, {}

    def deliver(keys, gathered):
        for (side, layer), g in zip(keys, gathered):
            sh = shards[(side, layer)]
            g = lax.dynamic_update_slice(g, sh[None], (chip, 0, 0))
            if side == "out":
                w_out[layer] = g.reshape(4 * sh.shape[0], sh.shape[1])
                continue
            w = g.transpose(1, 0, 2).reshape(sh.shape[0], 4 * sh.shape[1])
            if _layer_kind(layer)[0] == 2:
                w, wf_t[layer] = w[:, :4 * BRANCH], w[:, 4 * BRANCH:].T
            w_in[layer] = w

    deliver(GATHER_FIRST, _exchange_call(_GatherExchange([shards[k] for k in GATHER_FIRST]), "gather_first_weights"))
    saved = []
    for i in range(DEPTH):
        kind, j = _layer_kind(i)
        tag = f"l{i}"
        w = w_in[i]
        nqkv = A_QKV if kind == 0 else B_QKV
        tn = 512 if kind == 0 else 1024
        h, h_t = _rmsnorm_fwd(x, g_pre[i:i + 1], f"prenorm_{tag}")
        qkv = _matmul(h, w, mode="nn", out_dtype=BF16, name=f"inproj_qkv_{tag}", n=nqkv, tn=tn)
        z = _matmul(h, w, mode="nn", out_dtype=F32, name=f"inproj_gate_{tag}", n=BRANCH, b_off=nqkv // tn, tn=tn)
        behind = GATHER_BEHIND.get(i)
        exchange = _GatherExchange([shards[k] for k in behind]) if behind else None
        if kind == 0:
            sink_l = _per_head_lanes(sinks_a[j])
            (o, lse), arrived = _attn_a_fwd(qkv, slopes, sink_l, f"attn_a_fwd_{tag}", exchange)
            extra = (sink_l, lse)
        elif kind == 1:
            (o, extra), arrived = _attn_b_fwd(qkv, f"attn_b_fwd_{tag}", exchange)
        else:
            b_col = jnp.broadcast_to(b_f_c[j].astype(F32)[:, None], (N_HEADS, LANES))
            xf, cum = _fgate_fwd(h, wf_t[i], b_col, f"fgate_fwd_{tag}")
            cum4 = _to_cum4(cum, _fox_tile(s))
            (o, lse), arrived = _attn_c_fwd(qkv, cum4, f"attn_c_fwd_{tag}", exchange)
            extra = (xf, cum4, lse)
        if behind:
            deliver(behind, arrived)
        u, u_t = _gate_fwd(o, z, f"gate_{tag}")
        y = _matmul(u, w_out[i], mode="nn", out_dtype=F32, name=f"outproj_{tag}")
        saved.append((x, h, h_t, qkv, z, o, u_t, y, extra))
        x = _post_fwd(x, y, g_post[i:i + 1], f"postnorm_{tag}")

    dx, loss_part = _loss_and_grad(x, target)

    d_g_pre, d_g_post = [None] * DEPTH, [None] * DEPTH
    d_sinks = [None, None]
    d_b_f = None
    reduced = {}
    pending = None

    def finish_reduce(layer, sums, arrived):
        kind, j = _layer_kind(layer)
        for side, own, arr in zip(("in", "out"), sums, arrived):
            reduced[(side, kind)] = _sum_chips(own, arr, core, f"shard_sum_{side}_l{layer}", j, 2 if kind == 0 else 1,
                                               into=reduced.get((side, kind)))

    for i in reversed(range(DEPTH)):
        kind, j = _layer_kind(i)
        tag = f"l{i}"
        x_in, h, h_t, qkv, z, o, u_t, y, extra = saved[i]
        tn = 512 if kind == 0 else 1024
        dy, d_g_post[i] = _post_bwd(dx, y, g_post[i:i + 1], f"postnorm_bwd_{tag}")
        dw_out = _matmul(u_t, dy, mode="nn", out_dtype=F32, name=f"dw_out_{tag}")
        du = _matmul(dy, w_out[i], mode="nt", out_dtype=F32, name=f"d_gated_{tag}")
        do, dz = _gate_bwd(du, o, z, f"gate_bwd_{tag}")
        dhs = []
        exchange = _ScatterExchange(pending[1]) if pending else None
        if kind == 0:
            sink_l, lse = extra
            (dq, dk, dv, dsk), arrived = _attn_a_bwd(qkv, slopes, sink_l, o, lse, do, f"attn_a_bwd_{tag}", exchange)
            d_sinks[j] = dsk[:, 0, ::LANES].reshape(N_HEADS)
            parts = [dq, dk.astype(BF16), dv.astype(BF16), dz]
        elif kind == 1:
            (dq, dk, dv), arrived = _attn_b_bwd(qkv, extra, do, f"attn_b_bwd_{tag}", exchange)
            parts = [dq, dk, dv, dz]
        else:
            xf, cum4, lse = extra
            (dq, dk, dv, dcum4), arrived = _attn_c_bwd(qkv, cum4, o, lse, do, f"attn_c_bwd_{tag}", exchange)
            d_wf_t, dh_f, db = _fgate_bwd(_from_cum4(dcum4), xf, h, wf_t[i], f"fgate_bwd_{tag}")
            d_b_f = db[:, 0]
            dhs.append(dh_f)
            parts = [dq, dk, dv, dz]
        if pending:
            finish_reduce(pending[0], pending[1], arrived)
        dproj = jnp.concatenate(parts, axis=1)
        dw_in = _matmul(h_t, dproj, mode="nn", out_dtype=F32, name=f"dw_in_{tag}", tn=tn)
        if kind == 2:
            dw_in = jnp.concatenate([dw_in, d_wf_t.T], axis=1)
        dhs.insert(0, _matmul(dproj, w_in[i], mode="nt", out_dtype=F32, name=f"dh_{tag}", tk=1536 if kind == 0 else 2048))
        dx, d_g_pre[i] = _pre_bwd(dx, dhs, x_in, g_pre[i:i + 1], f"prenorm_bwd_{tag}")

        to_chips = [dw_in.reshape(dw_in.shape[0], 4, dw_in.shape[1] // 4).transpose(1, 0, 2).astype(BF16),
                    dw_out.reshape(4, dw_out.shape[0] // 4, dw_out.shape[1]).astype(BF16)]
        theirs = _sibling_send(to_chips, f"grad_sibling_exchange_{tag}")
        pending = (i, [_add_pairs(a, b, f"chip_sum_{side}_{tag}") for a, b, side in zip(to_chips, theirs, ("in", "out"))])

    arrived = _exchange_call(_ScatterExchange(pending[1]), "grad_chip_scatter_last")
    finish_reduce(pending[0], pending[1], arrived)

    return dict(loss=loss_part, dx=dx, g_pre=jnp.concatenate(d_g_pre, axis=0), g_post=jnp.concatenate(d_g_post, axis=0),
                sinks_a=jnp.stack(d_sinks), b_f_c=d_b_f[None, :], reduced=reduced)


def _place():
    x, y, c = lax.axis_index("x"), lax.axis_index("y"), lax.axis_index("c")
    others = [(1 - x, y), (x, 1 - y), (1 - x, 1 - y)]
    return x, y, c, others


def _half_rows(ref_rows, which):
    half = ref_rows // 2
    return pl.ds(pl.multiple_of(which * half, half), half)


def _remote(src, dst, sems, k, device):
    send, recv = sems
    return pltpu.make_async_remote_copy(src_ref=src, dst_ref=dst, send_sem=send.at[k], recv_sem=recv.at[k],
                                        device_id=device, device_id_type=MESH)


def _hbm_call(body, name, ins, out_shapes, n_remote, aliases=None):
    any_spec = pl.BlockSpec(memory_space=pl.ANY)
    return pl.pallas_call(
        body, name=name, in_specs=[any_spec] * len(ins), out_specs=[any_spec] * len(out_shapes),
        out_shape=out_shapes, input_output_aliases=aliases or {},
        scratch_shapes=[pltpu.SemaphoreType.DMA((n_remote,)), pltpu.SemaphoreType.DMA((n_remote,))],
    )(*ins)


class _GatherExchange:
    def __init__(self, shards):
        self.ins = list(shards)
        self.out_shapes = [jax.ShapeDtypeStruct((4,) + a.shape, a.dtype) for a in shards]
        self.n_sems = 6 * len(shards)
        self.aliases = {}

    def _copies(self, ins, outs, sems):
        x, y, c, others = _place()
        me = 2 * x + y
        table = []
        for w, (src, dst) in enumerate(zip(ins, outs)):
            mine, theirs = _half_rows(src.shape[0], c), _half_rows(src.shape[0], 1 - c)
            for j, (px, py) in enumerate(others):
                there = 2 * px + py
                send = _remote(src.at[mine], dst.at[me, mine], sems, 6 * w + j, (px, py, c))
                landed = _remote(dst.at[there, mine], dst.at[there, mine], sems, 6 * w + j, (px, py, c))
                passed = _remote(dst.at[there, mine], dst.at[there, mine], sems, 6 * w + 3 + j, (x, y, 1 - c))
                from_sibling = _remote(dst.at[there, theirs], dst.at[there, theirs], sems, 6 * w + 3 + j, (x, y, 1 - c))
                table.append((send, landed, passed, from_sibling))
        return table

    def start(self, ins, outs, sems):
        for send, _, _, _ in self._copies(ins, outs, sems):
            send.start()

    def mid(self, ins, outs, sems):
        for _, landed, passed, _ in self._copies(ins, outs, sems):
            landed.wait_recv()
            passed.start()

    def finish(self, ins, outs, sems):
        table = self._copies(ins, outs, sems)
        for _, _, _, from_sibling in table:
            from_sibling.wait_recv()
        for send, _, passed, _ in table:
            send.wait_send()
            passed.wait_send()


def _exchange_call(ex, name):
    n_in, n_out = len(ex.ins), len(ex.out_shapes)

    def body(*refs):
        ins, outs, sems = refs[:n_in], refs[n_in:n_in + n_out], refs[n_in + n_out:]
        ex.start(ins, outs, sems)
        ex.mid(ins, outs, sems)
        ex.finish(ins, outs, sems)

    return _hbm_call(body, name, ex.ins, ex.out_shapes, ex.n_sems, aliases=ex.aliases)


def _grid_call(body, *, name, grid, in_specs, out_specs, out_shape, args, scratch_shapes=(), semantics, exchange=None):
    if exchange is None:
        res = pl.pallas_call(body, name=name, grid=grid, in_specs=list(in_specs), out_specs=list(out_specs),
                             out_shape=list(out_shape), scratch_shapes=list(scratch_shapes),
                             compiler_params=_params(semantics))(*args)
        return res, []
    n_in, n_out, n_scr = len(args), len(out_shape), len(scratch_shapes)
    x_in, x_out = len(exchange.ins), len(exchange.out_shapes)
    steps = grid[0]

    def wrapped(*refs):
        core_in, ex_in = refs[:n_in], refs[n_in:n_in + x_in]
        rest = refs[n_in + x_in:]
        core_out, ex_out = rest[:n_out], rest[n_out:n_out + x_out]
        scratch, sems = rest[n_out + x_out:n_out + x_out + n_scr], rest[n_out + x_out + n_scr:]
        step = pl.program_id(0)

        @pl.when(step == 0)
        def _():
            exchange.start(ex_in, ex_out, sems)

        body(*core_in, *core_out, *scratch)

        @pl.when(step == (3 * steps) // 4 - 1)
        def _():
            exchange.mid(ex_in, ex_out, sems)

        @pl.when(step == steps - 1)
        def _():
            exchange.finish(ex_in, ex_out, sems)

    any_spec = pl.BlockSpec(memory_space=pl.ANY)
    res = pl.pallas_call(
        wrapped, name=name, grid=grid,
        in_specs=list(in_specs) + [any_spec] * x_in, out_specs=list(out_specs) + [any_spec] * x_out,
        out_shape=list(out_shape) + list(exchange.out_shapes),
        input_output_aliases={n_in + a: n_out + b for a, b in exchange.aliases.items()},
        scratch_shapes=list(scratch_shapes) + [pltpu.SemaphoreType.DMA((exchange.n_sems,)),
                                               pltpu.SemaphoreType.DMA((exchange.n_sems,))],
        compiler_params=_params(("arbitrary",)),
    )(*args, *exchange.ins)
    return res[:n_out], res[n_out:]


def _sibling_send(parts, name):
    n = len(parts)

    def body(*refs):
        ins, outs, sems = refs[:n], refs[n:2 * n], refs[2 * n:2 * n + 2]
        x, y, c, _ = _place()
        pend = []
        for w in range(n):
            cp = _remote(ins[w].at[:, _half_rows(ins[w].shape[1], 1 - c)], outs[w], sems, w, (x, y, 1 - c))
            cp.start()
            pend.append(cp)
        for cp in pend:
            cp.wait_recv()
            cp.wait_send()

    out_shapes = [jax.ShapeDtypeStruct((4, a.shape[1] // 2, a.shape[2]), a.dtype) for a in parts]
    return _hbm_call(body, name, parts, out_shapes, n)


class _ScatterExchange:
    def __init__(self, sums):
        self.ins = list(sums)
        self.out_shapes = [jax.ShapeDtypeStruct(a.shape, a.dtype) for a in sums]
        self.n_sems = 3 * len(sums)
        self.aliases = {}

    def _copies(self, ins, outs, sems):
        x, y, c, others = _place()
        me = 2 * x + y
        table = []
        for w, (src, dst) in enumerate(zip(ins, outs)):
            for j, (px, py) in enumerate(others):
                there = 2 * px + py
                send = _remote(src.at[there], dst.at[me], sems, 3 * w + j, (px, py, c))
                landed = _remote(dst.at[there], dst.at[there], sems, 3 * w + j, (px, py, c))
                table.append((send, landed))
        return table

    def start(self, ins, outs, sems):
        for send, _ in self._copies(ins, outs, sems):
            send.start()

    def mid(self, ins, outs, sems):
        pass

    def finish(self, ins, outs, sems):
        table = self._copies(ins, outs, sems)
        for _, landed in table:
            landed.wait_recv()
        for send, _ in table:
            send.wait_send()


def _sibling_join(shards):
    n = len(shards)

    def body(*refs):
        ins, outs, sems = refs[:n], refs[n:2 * n], refs[2 * n:2 * n + 2]
        x, y, c, _ = _place()
        pend = []
        for w in range(n):
            rows = ins[w].shape[1]
            mine, theirs = _half_rows(rows, c), _half_rows(rows, 1 - c)
            cp = _remote(ins[w].at[:, mine], outs[w].at[:, mine], sems, w, (x, y, 1 - c))
            cp.start()
            pend.append((cp, _remote(ins[w].at[:, theirs], outs[w].at[:, theirs], sems, w, (x, y, 1 - c))))
        for cp, landed in pend:
            landed.wait_recv()
            cp.wait_send()

    out_shapes = [jax.ShapeDtypeStruct(a.shape, a.dtype) for a in shards]
    return _hbm_call(body, "grad_sibling_join", shards, out_shapes, n, aliases={w: w for w in range(n)})


SMALL_ROWS = 136


def _all_reduce_small(vec):
    def body(v_ref, o_ref, buf, send, recv, loc):
        x, y, c, _ = _place()
        me = 4 * x + 2 * y + c
        lc = pltpu.make_async_copy(v_ref, buf.at[me], loc.at[0])
        lc.start()
        cps = []
        for k in range(1, 8):
            fx, fy, fc = (k >> 2) & 1, (k >> 1) & 1, k & 1
            peer = (x ^ fx, y ^ fy, c ^ fc)
            cp = pltpu.make_async_remote_copy(src_ref=v_ref, dst_ref=buf.at[me], send_sem=send.at[k - 1],
                                              recv_sem=recv.at[k - 1], device_id=peer, device_id_type=MESH)
            cp.start()
            cps.append((cp, 4 * peer[0] + 2 * peer[1] + peer[2]))
        for k, (cp, src) in enumerate(cps):
            pltpu.make_async_remote_copy(src_ref=v_ref, dst_ref=buf.at[src], send_sem=send.at[k], recv_sem=recv.at[k],
                                         device_id=(x, y, c), device_id_type=MESH).wait_recv()
        for cp, _ in cps:
            cp.wait_send()
        lc.wait()
        total = buf[0]
        for k in range(1, 8):
            total = total + buf[k]
        o_ref[...] = total

    vm = pl.BlockSpec(memory_space=pltpu.VMEM)
    return pl.pallas_call(
        body, name="all_reduce_small", in_specs=[vm], out_specs=vm,
        out_shape=jax.ShapeDtypeStruct(vec.shape, F32),
        scratch_shapes=[pltpu.VMEM((8,) + vec.shape, F32), pltpu.SemaphoreType.DMA((7,)),
                        pltpu.SemaphoreType.DMA((7,)), pltpu.SemaphoreType.DMA((1,))],
    )(vec)


SUM_ROWS = 256


def _add_pairs(part, theirs, name):
    four, rh, cc = theirs.shape
    tr = min(SUM_ROWS, rh)
    halves = part.reshape(four, 2, rh, cc)

    def body(a_ref, b_ref, o_ref):
        mine = a_ref[0, lax.axis_index("c")]
        o_ref[0] = (mine.astype(F32) + b_ref[0].astype(F32)).astype(o_ref.dtype)

    spec = pl.BlockSpec((1, tr, cc), lambda k, r: (k, r, 0))
    return pl.pallas_call(
        body, name=name, grid=(four, rh // tr),
        in_specs=[pl.BlockSpec((1, 2, tr, cc), lambda k, r: (k, 0, r, 0)), spec], out_specs=spec,
        out_shape=jax.ShapeDtypeStruct(theirs.shape, theirs.dtype),
        compiler_params=_params(("parallel", "parallel")),
    )(halves, theirs)


def _sum_chips(own, arrived, core, name, layer, n_layers, into=None):
    four, rh, cc = own.shape
    tr = min(SUM_ROWS, rh)
    nr = rh // tr

    def body(c_ref, own_ref, arr_ref, *rest):
        o_ref = rest[-1]
        x, y = lax.axis_index("x"), lax.axis_index("y")
        tot = own_ref[2 * x + y].astype(F32)
        for px, py in ((1 - x, y), (x, 1 - y), (1 - x, 1 - y)):
            tot = tot + arr_ref[2 * px + py].astype(F32)
        o_ref[0] = tot

    blk = pl.BlockSpec((4, tr, cc), lambda r, c_ref: (0, r, 0))
    in_specs, args, aliases = [blk, blk], [core, own, arrived], {}
    if into is not None:
        in_specs.append(pl.BlockSpec(memory_space=pl.ANY))
        args.append(into)
        aliases = {3: 0}
    return pl.pallas_call(
        body, name=name,
        grid_spec=pltpu.PrefetchScalarGridSpec(
            num_scalar_prefetch=1, grid=(nr,), in_specs=in_specs,
            out_specs=pl.BlockSpec((1, tr, cc), lambda r, c_ref: (layer, c_ref[0] * nr + r, 0))),
        out_shape=jax.ShapeDtypeStruct((n_layers, 2 * rh, cc), F32), input_output_aliases=aliases,
        compiler_params=_params(("parallel",)),
    )(*args)


ADAM_ROWS = 256


def _adamw(w, g, m, v, name):
    shape = w.shape
    cc = shape[-1]
    flat = lambda a: a.reshape(-1, cc)
    rows = flat(w).shape[0]
    tr = min(ADAM_ROWS, rows)
    assert rows % tr == 0
    c1 = 1.0 - ADAM_B1 ** ADAM_STEP
    c2 = 1.0 - ADAM_B2 ** ADAM_STEP

    def body(w_ref, g_ref, m_ref, v_ref, d_ref, nm_ref, nv_ref):
        gv = g_ref[...]
        nm = ADAM_B1 * m_ref[...] + (1.0 - ADAM_B1) * gv
        nv = ADAM_B2 * v_ref[...] + (1.0 - ADAM_B2) * (gv * gv)
        nm_ref[...] = nm
        nv_ref[...] = nv
        d_ref[...] = -ADAM_LR * ((nm / c1) / (jnp.sqrt(nv / c2) + ADAM_EPS) + ADAM_WD * w_ref[...])

    spec = pl.BlockSpec((tr, cc), lambda i: (i, 0))
    sh = jax.ShapeDtypeStruct((rows, cc), F32)
    outs = pl.pallas_call(
        body, name=name, grid=(rows // tr,), in_specs=[spec] * 4, out_specs=[spec] * 3, out_shape=[sh] * 3,
        compiler_params=_params(("parallel",)),
    )(flat(w), flat(g), flat(m), flat(v))
    return [o.reshape(shape) for o in outs]


def _pack_small(g_pre, g_post, sinks_a, b_f_c, loss_row):
    pad = lambda a: jnp.pad(a.reshape(1, -1).astype(F32), ((0, 0), (0, LANES - a.size)))
    rows = [g_pre.astype(F32).reshape(-1, LANES), g_post.astype(F32).reshape(-1, LANES), pad(sinks_a), pad(b_f_c), loss_row]
    packed = jnp.concatenate(rows, axis=0)
    return jnp.pad(packed, ((0, SMALL_ROWS - packed.shape[0]), (0, 0)))


def _unpack_small(p):
    n = DEPTH * D_MODEL // LANES
    return (p[:n].reshape(DEPTH, D_MODEL), p[n:2 * n].reshape(DEPTH, D_MODEL), p[2 * n, :2 * N_HEADS].reshape(2, N_HEADS),
            p[2 * n + 1, :N_HEADS].reshape(1, N_HEADS), p[2 * n + 2, 0])


def kernel(x, g_pre, g_post, w_in_a, w_out_a, sinks_a, w_in_b, w_out_b, w_in_c, b_f_c, w_out_c, loss_target, m_g_pre, m_g_post, m_w_in_a, m_w_out_a, m_sinks_a, m_w_in_b, m_w_out_b, m_w_in_c, m_b_f_c, m_w_out_c, v_g_pre, v_g_post, v_w_in_a, v_w_out_a, v_sinks_a, v_w_in_b, v_w_out_b, v_w_in_c, v_b_f_c, v_w_out_c):
    big_w = [w_in_a, w_out_a, w_in_b, w_out_b, w_in_c, w_out_c]
    big_m = [m_w_in_a, m_w_out_a, m_w_in_b, m_w_out_b, m_w_in_c, m_w_out_c]
    big_v = [v_w_in_a, v_w_out_a, v_w_in_b, v_w_out_b, v_w_in_c, v_w_out_c]

    chip = 2 * lax.axis_index("x") + lax.axis_index("y")
    core = lax.axis_index("c").astype(jnp.int32).reshape(1)
    by_kind = {0: (w_in_a, w_out_a), 1: (w_in_b, w_out_b), 2: (w_in_c, w_out_c)}
    shards = {}
    for i in range(DEPTH):
        kind, j = _layer_kind(i)
        shards[("in", i)] = by_kind[kind][0][j].astype(BF16)
        shards[("out", i)] = by_kind[kind][1][j].astype(BF16)

    res = _forward_backward(x[0], loss_target[0], g_pre, g_post, sinks_a, b_f_c, shards, chip, core)
    names = ["w_in_a", "w_out_a", "w_in_b", "w_out_b", "w_in_c", "w_out_c"]
    grads = _sibling_join([res["reduced"][(side, kind)] for kind in range(3) for side in ("in", "out")])

    small = _unpack_small(_all_reduce_small(
        _pack_small(res["g_pre"], res["g_post"], res["sinks_a"], res["b_f_c"], res["loss"])))
    g_small, loss = small[:4], small[4]

    zero_row = jnp.zeros((1, LANES), F32)
    pk = lambda a: _pack_small(a[0], a[1], a[2], a[3], zero_row)
    sm = _adamw(pk([g_pre, g_post, sinks_a, b_f_c]), pk(g_small), pk([m_g_pre, m_g_post, m_sinks_a, m_b_f_c]),
                pk([v_g_pre, v_g_post, v_sinks_a, v_b_f_c]), "adamw_small")
    sm = [_unpack_small(a)[:4] for a in sm]
    bigs = [_adamw(w, g, m, v, f"adamw_{nm}") for w, g, m, v, nm in zip(big_w, grads, big_m, big_v, names)]

    def ordered(small4, big6):
        return [small4[0], small4[1], big6[0], big6[1], small4[2], big6[2], big6[3], big6[4], small4[3], big6[5]]

    out = [loss, res["dx"][None], *ordered(g_small, grads)]
    for k in range(3):
        out += ordered(sm[k], [b[k] for b in bigs])
    return tuple(out)
```

```python
import functools
import math

import numpy as np
import jax
import jax.numpy as jnp
from jax import lax
from jax.experimental import pallas as pl
from jax.experimental.pallas import tpu as pltpu

F32 = jnp.float32
BF16 = jnp.bfloat16

D_MODEL = 2048
DEPTH = 4
N_HEADS = 32
HEAD_DIM = 64
LANES = 128
N_PAIRS = N_HEADS * HEAD_DIM // LANES
BRANCH = N_HEADS * HEAD_DIM
N_KV_A = 4
KV_A = N_KV_A * HEAD_DIM
WINDOW = 128
NORM_EPS = 1e-6
NEG = -1e30
Q_SCALE = HEAD_DIM ** -0.5

A_QKV = BRANCH + 2 * KV_A
B_QKV = 3 * BRANCH

ADAM_LR = 0.001
ADAM_B1 = 0.9
ADAM_B2 = 0.999
ADAM_EPS = 1e-08
ADAM_WD = 0.01
ADAM_STEP = 10

MESH = pl.DeviceIdType.MESH

_NT = (((1,), (1,)), ((), ()))
_TN = (((0,), (0,)), ((), ()))


def _params(sem=None):
    return pltpu.CompilerParams(dimension_semantics=sem)


def _matmul(a, b, *, mode, out_dtype, name, n=None, b_off=0, tm=1024, tn=1024, tk=2048, col_blocks=None, exchange=None):
    (m, k), nn = a.shape, ((n or b.shape[1]) if mode == "nn" else b.shape[0])
    tm, tn, tk = min(tm, m), min(tn, nn), min(tk, k)
    assert m % tm == 0 and nn % tn == 0 and k % tk == 0, (name, m, nn, k, tm, tn, tk)
    nk = k // tk

    def body(a_ref, b_ref, o_ref, acc_ref):
        kk = pl.program_id(2)
        if mode == "nn":
            p = jnp.dot(a_ref[...], b_ref[...], preferred_element_type=F32)
        else:
            p = lax.dot_general(a_ref[...], b_ref[...], _NT, preferred_element_type=F32)
        if nk == 1:
            o_ref[...] = p.astype(o_ref.dtype).reshape(o_ref.shape)
        else:
            @pl.when(kk == 0)
            def _():
                acc_ref[...] = p

            @pl.when(kk > 0)
            def _():
                acc_ref[...] += p

            @pl.when(kk == nk - 1)
            def _():
                o_ref[...] = acc_ref[...].astype(o_ref.dtype).reshape(o_ref.shape)

    if mode == "nn":
        in_specs = [pl.BlockSpec((tm, tk), lambda i, j, kk: (i, kk)),
                    pl.BlockSpec((tk, tn), lambda i, j, kk: (kk, j + b_off))]
    else:
        in_specs = [pl.BlockSpec((tm, tk), lambda i, j, kk: (i, kk)),
                    pl.BlockSpec((tn, tk), lambda i, j, kk: (j, kk))]
    if col_blocks is None:
        out_spec = pl.BlockSpec((tm, tn), lambda i, j, kk: (i, j))
        out_shape = jax.ShapeDtypeStruct((m, nn), out_dtype)
    else:
        per = nn // col_blocks // tn
        assert per * tn * col_blocks == nn, (name, nn, tn, col_blocks)
        out_spec = pl.BlockSpec((1, tm, tn), lambda i, j, kk: (j // per, i, j % per))
        out_shape = jax.ShapeDtypeStruct((col_blocks, m, nn // col_blocks), out_dtype)
    (res,), arrived = _grid_call(
        body, name=name, grid=(m // tm, nn // tn, nk), in_specs=in_specs, out_specs=[out_spec], out_shape=[out_shape],
        args=(a, b), scratch_shapes=[pltpu.VMEM((tm, tn), F32)], semantics=("parallel", "parallel", "arbitrary"),
        exchange=exchange)
    return res if exchange is None else (res, arrived)


ROW_TILE = 256


def _row_call(body, name, ins, outs, *, s):
    tr = min(ROW_TILE, s)
    spec = {"row": lambda sh: pl.BlockSpec((tr, sh[1]), lambda i: (i, 0)),
            "vec": lambda sh: pl.BlockSpec((1, sh[1]), lambda i: (0, 0)),
            "col": lambda sh: pl.BlockSpec((sh[0], tr), lambda i: (0, i))}
    in_specs = [spec[kind](a.shape) for a, kind in ins]
    out_specs = [spec[kind](sh.shape) for sh, kind in outs]
    return pl.pallas_call(
        body, name=name, grid=(s // tr,), in_specs=in_specs, out_specs=out_specs,
        out_shape=[sh for sh, _ in outs],
        compiler_params=_params(("arbitrary",)),
    )(*[a for a, _ in ins])


def _rsqrt_ms(v):
    return lax.rsqrt(jnp.mean(v * v, axis=-1, keepdims=True) + NORM_EPS)


def _rmsnorm_fwd(x, g, name):
    s, d = x.shape

    def body(x_ref, g_ref, h_ref, ht_ref):
        xv = x_ref[...]
        h = xv * _rsqrt_ms(xv) * g_ref[...]
        h_ref[...] = h.astype(BF16)
        ht_ref[...] = h.T.astype(BF16)

    return _row_call(body, name, [(x, "row"), (g, "vec")],
                     [(jax.ShapeDtypeStruct((s, d), BF16), "row"), (jax.ShapeDtypeStruct((d, s), BF16), "col")], s=s)


def _gate_fwd(o, z, name):
    s, d = o.shape

    def body(o_ref, z_ref, u_ref, ut_ref):
        zv = z_ref[...]
        u = o_ref[...] * (zv * jax.nn.sigmoid(zv))
        u_ref[...] = u.astype(BF16)
        ut_ref[...] = u.T.astype(BF16)

    return _row_call(body, name, [(o, "row"), (z, "row")],
                     [(jax.ShapeDtypeStruct((s, d), BF16), "row"), (jax.ShapeDtypeStruct((d, s), BF16), "col")], s=s)


def _post_fwd(x, y, g, name):
    s, d = x.shape

    def body(x_ref, y_ref, g_ref, o_ref):
        yv = y_ref[...]
        o_ref[...] = x_ref[...] + yv * _rsqrt_ms(yv) * g_ref[...]

    return _row_call(body, name, [(x, "row"), (y, "row"), (g, "vec")],
                     [(jax.ShapeDtypeStruct((s, d), F32), "row")], s=s)[0]


def _loss_and_grad(x, target):
    s, d = x.shape

    def body(x_ref, t_ref, dx_ref, l_ref):
        err = x_ref[...] - t_ref[...]
        dx_ref[...] = err * (1.0 / d)
        part = jnp.sum(jnp.sum(err * err, axis=1, keepdims=True), axis=0, keepdims=True) * (0.5 / d)

        @pl.when(pl.program_id(0) == 0)
        def _():
            l_ref[...] = jnp.zeros_like(l_ref)

        l_ref[...] += jnp.broadcast_to(part, l_ref.shape)

    return _row_call(body, "loss_head", [(x, "row"), (target, "row")],
                     [(jax.ShapeDtypeStruct((s, d), F32), "row"),
                      (jax.ShapeDtypeStruct((1, LANES), F32), "vec")], s=s)


def _norm_bwd_rows(dn, v, g):
    r = _rsqrt_ms(v)
    a = dn * g
    dv = r * (a - v * (r * r) * jnp.mean(a * v, axis=-1, keepdims=True))
    return dv, dn * v * r


def _post_bwd(dx, y, g, name):
    s, d = dx.shape

    def body(dx_ref, y_ref, g_ref, dy_ref, dg_ref):
        dy, dg = _norm_bwd_rows(dx_ref[...], y_ref[...], g_ref[...])
        dy_ref[...] = dy.astype(BF16)

        @pl.when(pl.program_id(0) == 0)
        def _():
            dg_ref[...] = jnp.zeros_like(dg_ref)

        dg_ref[...] += jnp.sum(dg, axis=0, keepdims=True)

    return _row_call(body, name, [(dx, "row"), (y, "row"), (g, "vec")],
                     [(jax.ShapeDtypeStruct((s, d), BF16), "row"),
                      (jax.ShapeDtypeStruct((1, d), F32), "vec")], s=s)


def _gate_bwd(du, o, z, name):
    s, d = du.shape

    def body(du_ref, o_ref, z_ref, do_ref, dz_ref):
        duv, zv = du_ref[...], z_ref[...]
        sig = jax.nn.sigmoid(zv)
        do_ref[...] = (duv * (zv * sig)).astype(BF16)
        dz_ref[...] = (duv * o_ref[...] * (sig * (1.0 + zv * (1.0 - sig)))).astype(BF16)

    return _row_call(body, name, [(du, "row"), (o, "row"), (z, "row")],
                     [(jax.ShapeDtypeStruct((s, d), BF16), "row"),
                      (jax.ShapeDtypeStruct((s, d), BF16), "row")], s=s)


def _pre_bwd(dx, dhs, x, g, name):
    s, d = dx.shape
    n_dh = len(dhs)

    def body(*refs):
        dx_ref, dh_refs, (x_ref, g_ref, o_ref, dg_ref) = refs[0], refs[1:1 + n_dh], refs[1 + n_dh:]
        dh = dh_refs[0][...].astype(F32)
        for r in dh_refs[1:]:
            dh = dh + r[...].astype(F32)
        dv, dg = _norm_bwd_rows(dh, x_ref[...], g_ref[...])
        o_ref[...] = dx_ref[...] + dv

        @pl.when(pl.program_id(0) == 0)
        def _():
            dg_ref[...] = jnp.zeros_like(dg_ref)

        dg_ref[...] += jnp.sum(dg, axis=0, keepdims=True)

    return _row_call(body, name, [(dx, "row")] + [(h, "row") for h in dhs] + [(x, "row"), (g, "vec")],
                     [(jax.ShapeDtypeStruct((s, d), F32), "row"),
                      (jax.ShapeDtypeStruct((1, d), F32), "vec")], s=s)


def _lane_is_first_head():
    return lax.broadcasted_iota(jnp.int32, (1, LANES), 1) < HEAD_DIM


def _bcast_lanes(col):
    return jnp.broadcast_to(col, (col.shape[0], LANES))


def _pair_spec(s, off=0, width=LANES):
    return pl.BlockSpec((s, width), lambda p: (0, p + off))


def _stack_heads(pair, first):
    return jnp.concatenate([jnp.where(first, pair, 0), jnp.where(first, 0, pair)], axis=0).astype(BF16)


def _stacked_mask(t, strict):
    row = lax.broadcasted_iota(jnp.int32, (2 * t, t), 0)
    col = lax.broadcasted_iota(jnp.int32, (2 * t, t), 1)
    query = jnp.where(row >= t, row - t, row)
    return col < query if strict else col <= query


LOOP_UNROLL = 2


def _two_at_a_time(n, step, carry):
    def group(jj, c):
        for k in range(LOOP_UNROLL):
            c = step(LOOP_UNROLL * jj + k, c)
        return c

    carry = lax.fori_loop(0, n // LOOP_UNROLL, group, carry)
    return lax.fori_loop(LOOP_UNROLL * (n // LOOP_UNROLL), n, step, carry)


def _rowsum_heads(prod, first):
    return (jnp.sum(jnp.where(first, prod, 0.0), axis=1, keepdims=True),
            jnp.sum(jnp.where(first, 0.0, prod), axis=1, keepdims=True))


def _softplus_parts(z):
    e = jnp.exp(-jnp.abs(z))
    sp = jnp.maximum(z, 0.0) + jnp.log(1.0 + e)
    r = 1.0 / (1.0 + e)
    return sp, jnp.where(z >= 0, r, e * r)


def _split_dot(x, t):
    hi = x.astype(BF16)
    lo = (x - hi.astype(F32)).astype(BF16)
    return jnp.dot(hi, t, preferred_element_type=F32) + jnp.dot(lo, t, preferred_element_type=F32)


def _sb_tile(s):
    return min(256, s)


def _attn_b_fwd(qkv, name, exchange=None):
    s = qkv.shape[0]
    t = _sb_tile(s)
    nq = s // t

    def body(q_ref, k_ref, v_ref, o_ref, lt_ref):
        first = _lane_is_first_head()
        before = _stacked_mask(t, strict=True)
        tri = (lax.broadcasted_iota(jnp.int32, (t, t), 0) >= lax.broadcasted_iota(jnp.int32, (t, t), 1)).astype(BF16)

        def tile(j, carry, diag, qs):
            c, acc = carry
            c0 = pl.multiple_of(j * t, t)
            k2 = k_ref[pl.ds(c0, t), :]
            v2 = v_ref[pl.ds(c0, t), :]
            z = lax.dot_general(qs, k2, _NT, preferred_element_type=F32)
            sp, _ = _softplus_parts(z)
            lf = jnp.where(before, -sp, 0.0) if diag else -sp
            incl = jnp.dot(lf.astype(BF16), tri, preferred_element_type=F32)
            a = jnp.exp(z + c + incl)
            if diag:
                a = jnp.where(before, a, 0.0)
            pv = jnp.dot(a.astype(BF16), v2, preferred_element_type=F32)
            return c + incl[:, 0:1], acc + jnp.where(first, pv[:t], pv[t:])

        def qblock(i, _):
            r0 = pl.multiple_of(i * t, t)
            qs = _stack_heads(q_ref[pl.ds(r0, t), :] * Q_SCALE, first)
            carry = tile(i, (jnp.zeros((2 * t, 1), F32), jnp.zeros((t, LANES), F32)), True, qs)
            carry = _two_at_a_time(i, lambda j, c: tile(i - 1 - j, c, False, qs), carry)
            o_ref[pl.ds(r0, t), :] = carry[1]
            lt_ref[pl.ds(r0, t), 0:LANES] = _bcast_lanes(carry[0][:t])
            lt_ref[pl.ds(r0, t), LANES:2 * LANES] = _bcast_lanes(carry[0][t:])
            return 0

        lax.fori_loop(0, nq, qblock, 0)

    return _grid_call(
        body, name=name, grid=(N_PAIRS,),
        in_specs=[_pair_spec(s), _pair_spec(s, N_PAIRS), _pair_spec(s, 2 * N_PAIRS)],
        out_specs=[_pair_spec(s), _stat_spec(s)],
        out_shape=[jax.ShapeDtypeStruct((s, BRANCH), F32), jax.ShapeDtypeStruct((s, N_HEADS * LANES), F32)],
        args=(qkv, qkv, qkv), semantics=("parallel",), exchange=exchange)


def _attn_b_bwd(qkv, ltot, do, name, exchange=None):
    s = qkv.shape[0]
    t = _sb_tile(s)
    nq = s // t

    def body(q_ref, k_ref, v_ref, lt_ref, do_ref, dq_ref, dk_ref, dv_ref, dk_acc, dv_acc):
        first = _lane_is_first_head()
        before = _stacked_mask(t, strict=True)
        tri = (lax.broadcasted_iota(jnp.int32, (t, t), 0) <= lax.broadcasted_iota(jnp.int32, (t, t), 1)).astype(BF16)
        dk_acc[...] = jnp.zeros_like(dk_acc)
        dv_acc[...] = jnp.zeros_like(dv_acc)

        def tile(j, carry, diag, qs, dos, lt):
            p_l, p_g, dq_acc = carry
            c0 = pl.multiple_of(j * t, t)
            k2 = k_ref[pl.ds(c0, t), :]
            v2 = v_ref[pl.ds(c0, t), :]
            z = lax.dot_general(qs, k2, _NT, preferred_element_type=F32)
            sp, sig = _softplus_parts(z)
            lf = jnp.where(before, -sp, 0.0) if diag else -sp
            pref_l = jnp.dot(lf.astype(BF16), tri, preferred_element_type=F32)
            a = jnp.exp(z + ((lt - p_l) - pref_l + lf))
            if diag:
                a = jnp.where(before, a, 0.0)
            g = a * lax.dot_general(dos, v2, _NT, preferred_element_type=F32)
            pref_g = jnp.dot(g.astype(BF16), tri, preferred_element_type=F32)
            dz = g - sig * (p_g + pref_g)
            if diag:
                dz = jnp.where(before, dz, 0.0)
            dzb = dz.astype(BF16)
            dq = jnp.dot(dzb, k2, preferred_element_type=F32)
            dk_acc[pl.ds(c0, t), :] += lax.dot_general(dzb, qs, _TN, preferred_element_type=F32)
            dv_acc[pl.ds(c0, t), :] += lax.dot_general(a.astype(BF16), dos, _TN, preferred_element_type=F32)
            return p_l + pref_l[:, t - 1:t], p_g + pref_g[:, t - 1:t], dq_acc + jnp.where(first, dq[:t], dq[t:])

        def qblock(i, _):
            r0 = pl.multiple_of(i * t, t)
            qs = _stack_heads(q_ref[pl.ds(r0, t), :] * Q_SCALE, first)
            dos = _stack_heads(do_ref[pl.ds(r0, t), :], first)
            lt = jnp.concatenate([lt_ref[pl.ds(r0, t), 0:1], lt_ref[pl.ds(r0, t), LANES:LANES + 1]], axis=0)
            zero = jnp.zeros((2 * t, 1), F32)
            carry = (zero, zero, jnp.zeros((t, LANES), F32))
            carry = _two_at_a_time(i, lambda j, c: tile(j, c, False, qs, dos, lt), carry)
            carry = tile(i, carry, True, qs, dos, lt)
            dq_ref[pl.ds(r0, t), :] = (carry[2] * Q_SCALE).astype(BF16)
            return 0

        lax.fori_loop(0, nq, qblock, 0)
        dk_ref[...] = dk_acc[...].astype(BF16)
        dv_ref[...] = dv_acc[...].astype(BF16)

    out = jax.ShapeDtypeStruct((s, BRANCH), BF16)
    return _grid_call(
        body, name=name, grid=(N_PAIRS,),
        in_specs=[_pair_spec(s), _pair_spec(s, N_PAIRS), _pair_spec(s, 2 * N_PAIRS), _stat_spec(s), _pair_spec(s)],
        out_specs=[_pair_spec(s)] * 3, out_shape=[out] * 3,
        scratch_shapes=[pltpu.VMEM((s, LANES), F32), pltpu.VMEM((s, LANES), F32)],
        args=(qkv, qkv, qkv, ltot, do), semantics=("parallel",), exchange=exchange)


def _fox_tile(s):
    return min(256, s)


def _stat_spec(s):
    return pl.BlockSpec((s, 2 * LANES), lambda p: (0, p))


def _cum_spec(nt, t):
    return pl.BlockSpec((1, nt, 2, t), lambda p: (p, 0, 0, 0))


def _attn_c_fwd(qkv, cum4, name, exchange=None):
    s = qkv.shape[0]
    t = _fox_tile(s)
    nq = s // t

    def body(q_ref, k_ref, v_ref, c_ref, o_ref, lse_ref):
        first = _lane_is_first_head()
        causal = _stacked_mask(t, strict=False)

        def tile(j, carry, diag, qs):
            c0 = pl.multiple_of(j * t, t)
            k2 = k_ref[pl.ds(c0, t), :]
            v2 = v_ref[pl.ds(c0, t), :]
            cs = c_ref[0, j]
            m_prev, l_prev, acc = carry
            z = lax.dot_general(qs, k2, _NT, preferred_element_type=F32)
            sc = jnp.concatenate([z[:t] - cs[0:1, :], z[t:] - cs[1:2, :]], axis=0)
            if diag:
                sc = jnp.where(causal, sc, NEG)
            m_new = jnp.maximum(m_prev, jnp.max(sc, axis=1, keepdims=True))
            alpha = jnp.exp(m_prev - m_new)
            p = jnp.exp(sc - m_new)
            l_new = alpha * l_prev + jnp.sum(p, axis=1, keepdims=True)
            pv = jnp.dot(p.astype(BF16), v2, preferred_element_type=F32)
            acc = jnp.where(first, acc * alpha[:t] + pv[:t], acc * alpha[t:] + pv[t:])
            return m_new, l_new, acc

        def qblock(i, _):
            r0 = pl.multiple_of(i * t, t)
            qs = _stack_heads(q_ref[pl.ds(r0, t), :] * Q_SCALE, first)
            carry = (jnp.full((2 * t, 1), NEG, F32), jnp.zeros((2 * t, 1), F32), jnp.zeros((t, LANES), F32))
            carry = _two_at_a_time(i, lambda j, c: tile(j, c, False, qs), carry)
            m, l, acc = tile(i, carry, True, qs)
            inv = 1.0 / l
            lse = m + jnp.log(l)
            o_ref[pl.ds(r0, t), :] = acc * jnp.where(first, inv[:t], inv[t:])
            lse_ref[pl.ds(r0, t), 0:LANES] = _bcast_lanes(lse[:t])
            lse_ref[pl.ds(r0, t), LANES:2 * LANES] = _bcast_lanes(lse[t:])
            return 0

        lax.fori_loop(0, nq, qblock, 0)

    return _grid_call(
        body, name=name, grid=(N_PAIRS,),
        in_specs=[_pair_spec(s), _pair_spec(s, N_PAIRS), _pair_spec(s, 2 * N_PAIRS), _cum_spec(nq, t)],
        out_specs=[_pair_spec(s), _stat_spec(s)],
        out_shape=[jax.ShapeDtypeStruct((s, BRANCH), F32), jax.ShapeDtypeStruct((s, N_HEADS * LANES), F32)],
        args=(qkv, qkv, qkv, cum4), semantics=("parallel",), exchange=exchange)


def _attn_c_bwd(qkv, cum4, o, lse, do, name, exchange=None):
    s = qkv.shape[0]
    t = _fox_tile(s)
    nq = s // t

    def body(q_ref, k_ref, v_ref, c_ref, o_ref, lse_ref, do_ref, dq_ref, dk_ref, dv_ref, dc_ref, dk_acc, dv_acc):
        first = _lane_is_first_head()
        causal = _stacked_mask(t, strict=False)
        eye = lax.broadcasted_iota(jnp.int32, (t, t), 0) == lax.broadcasted_iota(jnp.int32, (t, t), 1)
        dk_acc[...] = jnp.zeros_like(dk_acc)
        dv_acc[...] = jnp.zeros_like(dv_acc)
        dc_ref[...] = jnp.zeros_like(dc_ref)

        def tile(j, carry, diag, qs, dos, delta, lse):
            dq_acc, rs = carry
            c0 = pl.multiple_of(j * t, t)
            k2 = k_ref[pl.ds(c0, t), :]
            v2 = v_ref[pl.ds(c0, t), :]
            cs = c_ref[0, j]
            z = lax.dot_general(qs, k2, _NT, preferred_element_type=F32)
            sc = jnp.concatenate([z[:t] - cs[0:1, :], z[t:] - cs[1:2, :]], axis=0)
            p = jnp.exp(sc - lse)
            if diag:
                p = jnp.where(causal, p, 0.0)
            ds = p * (lax.dot_general(dos, v2, _NT, preferred_element_type=F32) - delta)
            dsb = ds.astype(BF16)
            dq = jnp.dot(dsb, k2, preferred_element_type=F32)
            dk_acc[pl.ds(c0, t), :] += lax.dot_general(dsb, qs, _TN, preferred_element_type=F32)
            dv_acc[pl.ds(c0, t), :] += lax.dot_general(p.astype(BF16), dos, _TN, preferred_element_type=F32)
            col_sums = jnp.concatenate([jnp.sum(ds[:t], axis=0, keepdims=True), jnp.sum(ds[t:], axis=0, keepdims=True)], axis=0)
            dc_ref[0, j] = dc_ref[0, j] - col_sums
            return dq_acc + jnp.where(first, dq[:t], dq[t:]), rs + jnp.sum(ds, axis=1, keepdims=True)

        def qblock(i, _):
            r0 = pl.multiple_of(i * t, t)
            do2 = do_ref[pl.ds(r0, t), :]
            qs = _stack_heads(q_ref[pl.ds(r0, t), :] * Q_SCALE, first)
            dos = _stack_heads(do2, first)
            delta = jnp.concatenate(_rowsum_heads(do2.astype(F32) * o_ref[pl.ds(r0, t), :], first), axis=0)
            lse = jnp.concatenate([lse_ref[pl.ds(r0, t), 0:1], lse_ref[pl.ds(r0, t), LANES:LANES + 1]], axis=0)
            carry = (jnp.zeros((t, LANES), F32), jnp.zeros((2 * t, 1), F32))
            carry = _two_at_a_time(i, lambda j, c: tile(j, c, False, qs, dos, delta, lse), carry)
            dq_acc, rs = tile(i, carry, True, qs, dos, delta, lse)
            dq_ref[pl.ds(r0, t), :] = (dq_acc * Q_SCALE).astype(BF16)
            as_row = lambda col_vec: jnp.sum(jnp.where(eye, col_vec, 0.0), axis=0, keepdims=True)
            dc_ref[0, i] = dc_ref[0, i] + jnp.concatenate([as_row(rs[:t]), as_row(rs[t:])], axis=0)
            return 0

        lax.fori_loop(0, nq, qblock, 0)
        dk_ref[...] = dk_acc[...].astype(BF16)
        dv_ref[...] = dv_acc[...].astype(BF16)

    out = jax.ShapeDtypeStruct((s, BRANCH), BF16)
    return _grid_call(
        body, name=name, grid=(N_PAIRS,),
        in_specs=[_pair_spec(s), _pair_spec(s, N_PAIRS), _pair_spec(s, 2 * N_PAIRS), _cum_spec(nq, t),
                  _pair_spec(s), _stat_spec(s), _pair_spec(s)],
        out_specs=[_pair_spec(s)] * 3 + [_cum_spec(nq, t)],
        out_shape=[out] * 3 + [jax.ShapeDtypeStruct(cum4.shape, F32)],
        scratch_shapes=[pltpu.VMEM((s, LANES), F32), pltpu.VMEM((s, LANES), F32)],
        args=(qkv, qkv, qkv, cum4, o, lse, do), semantics=("parallel",), exchange=exchange)


FG_CHUNK = 512


def _tri_dot3(x, t):
    hi = x.astype(BF16)
    r1 = x - hi.astype(F32)
    mid = r1.astype(BF16)
    lo = (r1 - mid.astype(F32)).astype(BF16)
    return (jnp.dot(hi, t, preferred_element_type=F32) + jnp.dot(mid, t, preferred_element_type=F32)
            + jnp.dot(lo, t, preferred_element_type=F32))


def _fgate_fwd(h, wf_t, b_col, name):
    s = h.shape[0]
    c = min(FG_CHUNK, s)

    def body(h_ref, w_ref, b_ref, xf_ref, cum_ref, carry_ref):
        @pl.when(pl.program_id(0) == 0)
        def _():
            carry_ref[...] = jnp.zeros_like(carry_ref)

        xf = lax.dot_general(w_ref[...], h_ref[...], _NT, preferred_element_type=F32) + b_ref[:, 0:1]
        xf_ref[...] = xf
        logf = jnp.minimum(xf, 0.0) - jnp.log(1.0 + jnp.exp(-jnp.abs(xf)))
        row = lax.broadcasted_iota(jnp.int32, (c, c), 0)
        col = lax.broadcasted_iota(jnp.int32, (c, c), 1)
        cum = _tri_dot3(logf, (row <= col).astype(BF16)) + carry_ref[:, 0:1]
        cum_ref[...] = cum
        carry_ref[...] = _bcast_lanes(cum[:, c - 1:c])

    out = jax.ShapeDtypeStruct((N_HEADS, s), F32)
    return pl.pallas_call(
        body, name=name, grid=(s // c,),
        in_specs=[pl.BlockSpec((c, D_MODEL), lambda i: (i, 0)),
                  pl.BlockSpec((N_HEADS, D_MODEL), lambda i: (0, 0)),
                  pl.BlockSpec((N_HEADS, LANES), lambda i: (0, 0))],
        out_specs=[pl.BlockSpec((N_HEADS, c), lambda i: (0, i))] * 2,
        out_shape=[out, out],
        scratch_shapes=[pltpu.VMEM((N_HEADS, LANES), F32)],
        compiler_params=_params(("arbitrary",)),
    )(h, wf_t, b_col)


def _fgate_bwd(dcum, xf, h, wf_t, name):
    s = h.shape[0]
    c = min(FG_CHUNK, s)
    n = s // c

    def body(dc_ref, xf_ref, h_ref, w_ref, dw_ref, dh_ref, db_ref, carry_ref):
        @pl.when(pl.program_id(0) == 0)
        def _():
            carry_ref[...] = jnp.zeros_like(carry_ref)
            dw_ref[...] = jnp.zeros_like(dw_ref)
            db_ref[...] = jnp.zeros_like(db_ref)

        row = lax.broadcasted_iota(jnp.int32, (c, c), 0)
        col = lax.broadcasted_iota(jnp.int32, (c, c), 1)
        dlogf = _tri_dot3(dc_ref[...], (row >= col).astype(BF16)) + carry_ref[:, 0:1]
        carry_ref[...] = _bcast_lanes(dlogf[:, 0:1])
        xf = xf_ref[...]
        e = jnp.exp(-jnp.abs(xf))
        r = 1.0 / (1.0 + e)
        dxf = dlogf * jnp.where(xf >= 0, e * r, r)
        db_ref[...] += _bcast_lanes(jnp.sum(dxf, axis=1, keepdims=True))
        dxb = dxf.astype(BF16)
        dw_ref[...] += jnp.dot(dxb, h_ref[...], preferred_element_type=F32)
        dh_ref[...] = lax.dot_general(dxb, w_ref[...], _TN, preferred_element_type=F32)

    rev = lambda i: n - 1 - i
    return pl.pallas_call(
        body, name=name, grid=(n,),
        in_specs=[pl.BlockSpec((N_HEADS, c), lambda i: (0, rev(i))),
                  pl.BlockSpec((N_HEADS, c), lambda i: (0, rev(i))),
                  pl.BlockSpec((c, D_MODEL), lambda i: (rev(i), 0)),
                  pl.BlockSpec((N_HEADS, D_MODEL), lambda i: (0, 0))],
        out_specs=[pl.BlockSpec((N_HEADS, D_MODEL), lambda i: (0, 0)),
                   pl.BlockSpec((c, D_MODEL), lambda i: (rev(i), 0)),
                   pl.BlockSpec((N_HEADS, LANES), lambda i: (0, 0))],
        out_shape=[jax.ShapeDtypeStruct((N_HEADS, D_MODEL), F32), jax.ShapeDtypeStruct((s, D_MODEL), F32),
                   jax.ShapeDtypeStruct((N_HEADS, LANES), F32)],
        scratch_shapes=[pltpu.VMEM((N_HEADS, LANES), F32)],
        compiler_params=_params(("arbitrary",)),
    )(dcum, xf, h, wf_t)


def _to_cum4(v, t):
    s = v.shape[1]
    return v.reshape(N_PAIRS, 2, s // t, t).transpose(0, 2, 1, 3)


def _from_cum4(v4):
    p, nt, two, t = v4.shape
    return v4.transpose(0, 2, 1, 3).reshape(p * two, nt * t)


def _alibi_slopes():
    return (2.0 ** (-8.0 * np.arange(1, N_HEADS + 1, dtype=np.float32) / N_HEADS)).astype(np.float32)


def _per_head_lanes(v):
    return jnp.repeat(v.astype(F32).reshape(N_PAIRS, 1, 2), LANES, axis=2)


def _attn_a_specs(s):
    q = _pair_spec(s)
    k = pl.BlockSpec((s, LANES), lambda p: (0, N_PAIRS + p // 8))
    v = pl.BlockSpec((s, LANES), lambda p: (0, N_PAIRS + KV_A // LANES + p // 8))
    head = pl.BlockSpec((1, 1, 2 * LANES), lambda p: (p, 0, 0))
    return q, k, v, head


def _attn_a_geometry(p, slope_ref, sink_ref):
    kv_half = (p // 4) % 2
    kv_first = kv_half == 0
    lane_first = _lane_is_first_head()
    kv_lanes = (lax.broadcasted_iota(jnp.int32, (1, LANES), 1) // HEAD_DIM) == kv_half
    row = lax.broadcasted_iota(jnp.int32, (2 * WINDOW, 2 * WINDOW), 0)
    cj = lax.broadcasted_iota(jnp.int32, (2 * WINDOW, 2 * WINDOW), 1)
    second = row >= WINDOW
    dist = WINDOW + jnp.where(second, row - WINDOW, row) - cj
    valid = (dist >= 0) & (dist < WINDOW)
    per_row = lambda ref: jnp.where(second[:, 0:1], ref[0, :, LANES:LANES + 1], ref[0, :, 0:1])
    return kv_first, lane_first, kv_lanes, per_row(slope_ref) * dist.astype(F32), valid, per_row(sink_ref)


def _swap_halves(x):
    return pltpu.roll(x, HEAD_DIM, 1)


def _attn_a_fwd(qkv, slopes, sinks, name, exchange=None):
    s = qkv.shape[0]
    nb = s // WINDOW

    def body(q_ref, k_ref, v_ref, sl_ref, sk_ref, o_ref, lse_ref):
        kv_first, lane_first, kv_lanes, bias, valid, sink = _attn_a_geometry(pl.program_id(0), sl_ref, sk_ref)

        def block(r0, k0, width):
            q2 = q_ref[pl.ds(r0, WINDOW), :].astype(F32) * Q_SCALE
            q2r = _swap_halves(q2)
            xs = jnp.concatenate([jnp.where(kv_first, q2, q2r), jnp.where(kv_first, q2r, q2)], axis=0).astype(BF16)
            km = jnp.where(kv_lanes, k_ref[pl.ds(k0, width), :], 0).astype(BF16)
            vm = jnp.where(kv_lanes, v_ref[pl.ds(k0, width), :], 0).astype(BF16)
            sc = lax.dot_general(xs, km, _NT, preferred_element_type=F32) - bias[:, 2 * WINDOW - width:]
            sc = jnp.where(valid[:, 2 * WINDOW - width:], sc, NEG)
            m = jnp.maximum(jnp.max(sc, axis=1, keepdims=True), sink)
            pr = jnp.exp(sc - m)
            l = jnp.sum(pr, axis=1, keepdims=True) + jnp.exp(sink - m)
            os = jnp.dot(pr.astype(BF16), vm, preferred_element_type=F32) * (1.0 / l)
            lse = m + jnp.log(l)
            lse_ref[pl.ds(r0, WINDOW), 0:LANES] = _bcast_lanes(lse[:WINDOW])
            lse_ref[pl.ds(r0, WINDOW), LANES:2 * LANES] = _bcast_lanes(lse[WINDOW:])
            oa = jnp.where(kv_first, os[:WINDOW], _swap_halves(os[:WINDOW]))
            ob = jnp.where(kv_first, _swap_halves(os[WINDOW:]), os[WINDOW:])
            o_ref[pl.ds(r0, WINDOW), :] = jnp.where(lane_first, oa, ob)

        block(0, 0, WINDOW)

        def loop(n, _):
            r0 = pl.multiple_of(n * WINDOW, WINDOW)
            block(r0, pl.multiple_of(r0 - WINDOW, WINDOW), 2 * WINDOW)
            return 0

        _two_at_a_time(nb - 1, lambda n, c: loop(n + 1, c), 0)

    q, k, v, head = _attn_a_specs(s)
    return _grid_call(
        body, name=name, grid=(N_PAIRS,),
        in_specs=[q, k, v, head, head],
        out_specs=[_pair_spec(s), _stat_spec(s)],
        out_shape=[jax.ShapeDtypeStruct((s, BRANCH), F32), jax.ShapeDtypeStruct((s, N_HEADS * LANES), F32)],
        args=(qkv, qkv, qkv, slopes, sinks), semantics=("parallel",), exchange=exchange)


def _attn_a_bwd(qkv, slopes, sinks, o, lse, do, name, exchange=None):
    s = qkv.shape[0]
    nb = s // WINDOW

    def body(q_ref, k_ref, v_ref, sl_ref, sk_ref, o_ref, lse_ref, do_ref, dq_ref, dk_ref, dv_ref, dsk_ref):
        p_id = pl.program_id(0)
        kv_first, lane_first, kv_lanes, bias, valid, sink = _attn_a_geometry(p_id, sl_ref, sk_ref)

        @pl.when(p_id % 8 == 0)
        def _():
            dk_ref[...] = jnp.zeros_like(dk_ref)
            dv_ref[...] = jnp.zeros_like(dv_ref)

        def align(v2):
            v2r = _swap_halves(v2)
            both = jnp.concatenate([jnp.where(kv_first, v2, v2r), jnp.where(kv_first, v2r, v2)], axis=0)
            return jnp.where(kv_lanes, both, 0.0).astype(BF16)

        def block(r0, k0, width, sink_sum):
            xq = align(q_ref[pl.ds(r0, WINDOW), :].astype(F32) * Q_SCALE)
            do2 = do_ref[pl.ds(r0, WINDOW), :].astype(F32)
            xdo = align(do2)
            delta = jnp.concatenate(_rowsum_heads(do2 * o_ref[pl.ds(r0, WINDOW), :], lane_first), axis=0)
            lse = jnp.concatenate([lse_ref[pl.ds(r0, WINDOW), 0:1], lse_ref[pl.ds(r0, WINDOW), LANES:LANES + 1]], axis=0)
            km = jnp.where(kv_lanes, k_ref[pl.ds(k0, width), :], 0).astype(BF16)
            vm = jnp.where(kv_lanes, v_ref[pl.ds(k0, width), :], 0).astype(BF16)
            sc = lax.dot_general(xq, km, _NT, preferred_element_type=F32) - bias[:, 2 * WINDOW - width:]
            pr = jnp.where(valid[:, 2 * WINDOW - width:], jnp.exp(sc - lse), 0.0)
            ds = pr * (lax.dot_general(xdo, vm, _NT, preferred_element_type=F32) - delta)
            dsb = ds.astype(BF16)
            dq_al = jnp.dot(dsb, km, preferred_element_type=F32)
            dk_ref[pl.ds(k0, width), :] += lax.dot_general(dsb, xq, _TN, preferred_element_type=F32)
            dv_ref[pl.ds(k0, width), :] += lax.dot_general(pr.astype(BF16), xdo, _TN, preferred_element_type=F32)
            dqa = jnp.where(kv_first, dq_al[:WINDOW], _swap_halves(dq_al[:WINDOW]))
            dqb = jnp.where(kv_first, _swap_halves(dq_al[WINDOW:]), dq_al[WINDOW:])
            dq_ref[pl.ds(r0, WINDOW), :] = (jnp.where(lane_first, dqa, dqb) * Q_SCALE).astype(BF16)
            return sink_sum + jnp.exp(sink - lse) * delta

        sink_sum = block(0, 0, WINDOW, jnp.zeros((2 * WINDOW, 1), F32))

        def loop(n, c):
            r0 = pl.multiple_of(n * WINDOW, WINDOW)
            return block(r0, pl.multiple_of(r0 - WINDOW, WINDOW), 2 * WINDOW, c)

        sink_sum = _two_at_a_time(nb - 1, lambda n, c: loop(n + 1, c), sink_sum)
        dsk_ref[0, :, 0:LANES] = jnp.broadcast_to(-jnp.sum(sink_sum[:WINDOW], axis=0, keepdims=True), (1, LANES))
        dsk_ref[0, :, LANES:2 * LANES] = jnp.broadcast_to(-jnp.sum(sink_sum[WINDOW:], axis=0, keepdims=True), (1, LANES))

    q, k, v, head = _attn_a_specs(s)
    kv_out = pl.BlockSpec((s, LANES), lambda p: (0, p // 8))
    return _grid_call(
        body, name=name, grid=(N_PAIRS,),
        in_specs=[q, k, v, head, head, _pair_spec(s), _stat_spec(s), _pair_spec(s)],
        out_specs=[_pair_spec(s), kv_out, kv_out, head],
        out_shape=[jax.ShapeDtypeStruct((s, BRANCH), BF16), jax.ShapeDtypeStruct((s, KV_A), F32),
                   jax.ShapeDtypeStruct((s, KV_A), F32), jax.ShapeDtypeStruct((N_PAIRS, 1, 2 * LANES), F32)],
        args=(qkv, qkv, qkv, slopes, sinks, o, lse, do), semantics=("arbitrary",), exchange=exchange)


def _layer_kind(i):
    return i % 3, i // 3


GATHER_FIRST = [("in", 0)]
GATHER_BEHIND = {0: [("out", 0), ("in", 1)], 1: [("out", 1), ("in", 2), ("out", 2)], 2: [("in", 3), ("out", 3)]}


def _forward_backward(x, target, g_pre, g_post, sinks_a, b_f_c, shards, chip, core):
    s = x.shape[0]
    slopes = _per_head_lanes(jnp.asarray(_alibi_slopes()))
    w_in, w_out, wf_t = {}, {}, {}

    def deliver(keys, gathered):
        for (side, layer), g in zip(keys, gathered):
            sh = shards[(side, layer)]
            g = lax.dynamic_update_slice(g, sh[None], (chip, 0, 0))
            if side == "out":
                w_out[layer] = g.reshape(4 * sh.shape[0], sh.shape[1])
                continue
            w = g.transpose(1, 0, 2).reshape(sh.shape[0], 4 * sh.shape[1])
            if _layer_kind(layer)[0] == 2:
                w, wf_t[layer] = w[:, :4 * BRANCH], w[:, 4 * BRANCH:].T
            w_in[layer] = w

    deliver(GATHER_FIRST, _exchange_call(_GatherExchange([shards[k] for k in GATHER_FIRST]), "gather_first_weights"))
    saved = []
    for i in range(DEPTH):
        kind, j = _layer_kind(i)
        tag = f"l{i}"
        w = w_in[i]
        nqkv = A_QKV if kind == 0 else B_QKV
        tn = 512 if kind == 0 else 1024
        h, h_t = _rmsnorm_fwd(x, g_pre[i:i + 1], f"prenorm_{tag}")
        qkv = _matmul(h, w, mode="nn", out_dtype=BF16, name=f"inproj_qkv_{tag}", n=nqkv, tn=tn)
        z = _matmul(h, w, mode="nn", out_dtype=F32, name=f"inproj_gate_{tag}", n=BRANCH, b_off=nqkv // tn, tn=tn)
        behind = GATHER_BEHIND.get(i)
        exchange = _GatherExchange([shards[k] for k in behind]) if behind else None
        if kind == 0:
            sink_l = _per_head_lanes(sinks_a[j])
            (o, lse), arrived = _attn_a_fwd(qkv, slopes, sink_l, f"attn_a_fwd_{tag}", exchange)
            extra = (sink_l, lse)
        elif kind == 1:
            (o, extra), arrived = _attn_b_fwd(qkv, f"attn_b_fwd_{tag}", exchange)
        else:
            b_col = jnp.broadcast_to(b_f_c[j].astype(F32)[:, None], (N_HEADS, LANES))
            xf, cum = _fgate_fwd(h, wf_t[i], b_col, f"fgate_fwd_{tag}")
            cum4 = _to_cum4(cum, _fox_tile(s))
            (o, lse), arrived = _attn_c_fwd(qkv, cum4, f"attn_c_fwd_{tag}", exchange)
            extra = (xf, cum4, lse)
        if behind:
            deliver(behind, arrived)
        u, u_t = _gate_fwd(o, z, f"gate_{tag}")
        y = _matmul(u, w_out[i], mode="nn", out_dtype=F32, name=f"outproj_{tag}")
        saved.append((x, h, h_t, qkv, z, o, u_t, y, extra))
        x = _post_fwd(x, y, g_post[i:i + 1], f"postnorm_{tag}")

    dx, loss_part = _loss_and_grad(x, target)

    d_g_pre, d_g_post = [None] * DEPTH, [None] * DEPTH
    d_sinks = [None, None]
    d_b_f = None
    reduced = {}
    pending = None

    def finish_reduce(layer, sums, arrived):
        kind, j = _layer_kind(layer)
        for side, own, arr in zip(("in", "out"), sums, arrived):
            reduced[(side, kind)] = _sum_chips(own, arr, core, f"shard_sum_{side}_l{layer}", j, 2 if kind == 0 else 1,
                                               into=reduced.get((side, kind)))

    for i in reversed(range(DEPTH)):
        kind, j = _layer_kind(i)
        tag = f"l{i}"
        x_in, h, h_t, qkv, z, o, u_t, y, extra = saved[i]
        tn = 512 if kind == 0 else 1024
        dy, d_g_post[i] = _post_bwd(dx, y, g_post[i:i + 1], f"postnorm_bwd_{tag}")
        dw_out = _matmul(u_t, dy, mode="nn", out_dtype=BF16, name=f"dw_out_{tag}")
        du = _matmul(dy, w_out[i], mode="nt", out_dtype=F32, name=f"d_gated_{tag}")
        do, dz = _gate_bwd(du, o, z, f"gate_bwd_{tag}")
        dhs = []
        exchange = _ScatterExchange(pending[1]) if pending else None
        if kind == 0:
            sink_l, lse = extra
            (dq, dk, dv, dsk), arrived = _attn_a_bwd(qkv, slopes, sink_l, o, lse, do, f"attn_a_bwd_{tag}", exchange)
            d_sinks[j] = dsk[:, 0, ::LANES].reshape(N_HEADS)
            parts = [dq, dk.astype(BF16), dv.astype(BF16), dz]
        elif kind == 1:
            (dq, dk, dv), arrived = _attn_b_bwd(qkv, extra, do, f"attn_b_bwd_{tag}", exchange)
            parts = [dq, dk, dv, dz]
        else:
            xf, cum4, lse = extra
            (dq, dk, dv, dcum4), arrived = _attn_c_bwd(qkv, cum4, o, lse, do, f"attn_c_bwd_{tag}", exchange)
            d_wf_t, dh_f, db = _fgate_bwd(_from_cum4(dcum4), xf, h, wf_t[i], f"fgate_bwd_{tag}")
            d_b_f = db[:, 0]
            dhs.append(dh_f)
            parts = [dq, dk, dv, dz]
        if pending:
            finish_reduce(pending[0], pending[1], arrived)
        dproj = jnp.concatenate(parts, axis=1)
        if kind == 2:
            dw_in = _matmul(h_t, dproj, mode="nn", out_dtype=F32, name=f"dw_in_{tag}", tn=tn)
            dw_in = jnp.concatenate([dw_in, d_wf_t.T], axis=1)
            dw_in = dw_in.reshape(dw_in.shape[0], 4, dw_in.shape[1] // 4).transpose(1, 0, 2).astype(BF16)
        else:
            dw_in = _matmul(h_t, dproj, mode="nn", out_dtype=BF16, name=f"dw_in_{tag}", col_blocks=4,
                            tn=1152 if kind == 0 else 1024)
        to_chips = [dw_in, dw_out.reshape(4, dw_out.shape[0] // 4, dw_out.shape[1])]
        dh, theirs = _matmul(dproj, w_in[i], mode="nt", out_dtype=F32, name=f"dh_{tag}", tk=1536 if kind == 0 else 2048,
                             exchange=_SiblingExchange(to_chips))
        dhs.insert(0, dh)
        dx, d_g_pre[i] = _pre_bwd(dx, dhs, x_in, g_pre[i:i + 1], f"prenorm_bwd_{tag}")
        pending = (i, [_add_pairs(a, b, f"chip_sum_{side}_{tag}") for a, b, side in zip(to_chips, theirs, ("in", "out"))])

    arrived = _exchange_call(_ScatterExchange(pending[1]), "grad_chip_scatter_last")
    finish_reduce(pending[0], pending[1], arrived)

    return dict(loss=loss_part, dx=dx, g_pre=jnp.concatenate(d_g_pre, axis=0), g_post=jnp.concatenate(d_g_post, axis=0),
                sinks_a=jnp.stack(d_sinks), b_f_c=d_b_f[None, :], reduced=reduced)


def _place():
    x, y, c = lax.axis_index("x"), lax.axis_index("y"), lax.axis_index("c")
    others = [(1 - x, y), (x, 1 - y), (1 - x, 1 - y)]
    return x, y, c, others


def _half_rows(ref_rows, which):
    half = ref_rows // 2
    return pl.ds(pl.multiple_of(which * half, half), half)


def _remote(src, dst, sems, k, device):
    send, recv = sems
    return pltpu.make_async_remote_copy(src_ref=src, dst_ref=dst, send_sem=send.at[k], recv_sem=recv.at[k],
                                        device_id=device, device_id_type=MESH)


def _hbm_call(body, name, ins, out_shapes, n_remote, aliases=None):
    any_spec = pl.BlockSpec(memory_space=pl.ANY)
    return pl.pallas_call(
        body, name=name, in_specs=[any_spec] * len(ins), out_specs=[any_spec] * len(out_shapes),
        out_shape=out_shapes, input_output_aliases=aliases or {},
        scratch_shapes=[pltpu.SemaphoreType.DMA((n_remote,)), pltpu.SemaphoreType.DMA((n_remote,))],
    )(*ins)


class _GatherExchange:
    def __init__(self, shards):
        self.ins = list(shards)
        self.out_shapes = [jax.ShapeDtypeStruct((4,) + a.shape, a.dtype) for a in shards]
        self.n_sems = 6 * len(shards)
        self.aliases = {}

    def _copies(self, ins, outs, sems):
        x, y, c, others = _place()
        me = 2 * x + y
        table = []
        for w, (src, dst) in enumerate(zip(ins, outs)):
            mine, theirs = _half_rows(src.shape[0], c), _half_rows(src.shape[0], 1 - c)
            for j, (px, py) in enumerate(others):
                there = 2 * px + py
                send = _remote(src.at[mine], dst.at[me, mine], sems, 6 * w + j, (px, py, c))
                landed = _remote(dst.at[there, mine], dst.at[there, mine], sems, 6 * w + j, (px, py, c))
                passed = _remote(dst.at[there, mine], dst.at[there, mine], sems, 6 * w + 3 + j, (x, y, 1 - c))
                from_sibling = _remote(dst.at[there, theirs], dst.at[there, theirs], sems, 6 * w + 3 + j, (x, y, 1 - c))
                table.append((send, landed, passed, from_sibling))
        return table

    def start(self, ins, outs, sems):
        for send, _, _, _ in self._copies(ins, outs, sems):
            send.start()

    def mid(self, ins, outs, sems):
        for _, landed, passed, _ in self._copies(ins, outs, sems):
            landed.wait_recv()
            passed.start()

    def finish(self, ins, outs, sems):
        table = self._copies(ins, outs, sems)
        for _, _, _, from_sibling in table:
            from_sibling.wait_recv()
        for send, _, passed, _ in table:
            send.wait_send()
            passed.wait_send()


def _exchange_call(ex, name):
    n_in, n_out = len(ex.ins), len(ex.out_shapes)

    def body(*refs):
        ins, outs, sems = refs[:n_in], refs[n_in:n_in + n_out], refs[n_in + n_out:]
        ex.start(ins, outs, sems)
        ex.mid(ins, outs, sems)
        ex.finish(ins, outs, sems)

    return _hbm_call(body, name, ex.ins, ex.out_shapes, ex.n_sems, aliases=ex.aliases)


def _grid_call(body, *, name, grid, in_specs, out_specs, out_shape, args, scratch_shapes=(), semantics, exchange=None):
    if exchange is None:
        res = pl.pallas_call(body, name=name, grid=grid, in_specs=list(in_specs), out_specs=list(out_specs),
                             out_shape=list(out_shape), scratch_shapes=list(scratch_shapes),
                             compiler_params=_params(semantics))(*args)
        return res, []
    n_in, n_out, n_scr = len(args), len(out_shape), len(scratch_shapes)
    x_in, x_out = len(exchange.ins), len(exchange.out_shapes)
    steps = math.prod(grid)

    def wrapped(*refs):
        core_in, ex_in = refs[:n_in], refs[n_in:n_in + x_in]
        rest = refs[n_in + x_in:]
        core_out, ex_out = rest[:n_out], rest[n_out:n_out + x_out]
        scratch, sems = rest[n_out + x_out:n_out + x_out + n_scr], rest[n_out + x_out + n_scr:]
        step = 0
        for axis, extent in enumerate(grid):
            step = step * extent + pl.program_id(axis)

        @pl.when(step == 0)
        def _():
            exchange.start(ex_in, ex_out, sems)

        body(*core_in, *core_out, *scratch)

        @pl.when(step == max((3 * steps) // 4 - 1, 0))
        def _():
            exchange.mid(ex_in, ex_out, sems)

        @pl.when(step == steps - 1)
        def _():
            exchange.finish(ex_in, ex_out, sems)

    any_spec = pl.BlockSpec(memory_space=pl.ANY)
    res = pl.pallas_call(
        wrapped, name=name, grid=grid,
        in_specs=list(in_specs) + [any_spec] * x_in, out_specs=list(out_specs) + [any_spec] * x_out,
        out_shape=list(out_shape) + list(exchange.out_shapes),
        input_output_aliases={n_in + a: n_out + b for a, b in exchange.aliases.items()},
        scratch_shapes=list(scratch_shapes) + [pltpu.SemaphoreType.DMA((exchange.n_sems,)),
                                               pltpu.SemaphoreType.DMA((exchange.n_sems,))],
        compiler_params=_params(("arbitrary",) * len(grid)),
    )(*args, *exchange.ins)
    return res[:n_out], res[n_out:]


class _SiblingExchange:
    def __init__(self, parts):
        self.ins = list(parts)
        self.out_shapes = [jax.ShapeDtypeStruct((4, a.shape[1] // 2, a.shape[2]), a.dtype) for a in parts]
        self.n_sems = len(parts)
        self.aliases = {}

    def _copies(self, ins, outs, sems):
        x, y, c, _ = _place()
        return [_remote(src.at[:, _half_rows(src.shape[1], 1 - c)], dst, sems, w, (x, y, 1 - c))
                for w, (src, dst) in enumerate(zip(ins, outs))]

    def start(self, ins, outs, sems):
        for cp in self._copies(ins, outs, sems):
            cp.start()

    def mid(self, ins, outs, sems):
        pass

    def finish(self, ins, outs, sems):
        for cp in self._copies(ins, outs, sems):
            cp.wait_recv()
            cp.wait_send()


class _ScatterExchange:
    def __init__(self, sums):
        self.ins = list(sums)
        self.out_shapes = [jax.ShapeDtypeStruct(a.shape, a.dtype) for a in sums]
        self.n_sems = 3 * len(sums)
        self.aliases = {}

    def _copies(self, ins, outs, sems):
        x, y, c, others = _place()
        me = 2 * x + y
        table = []
        for w, (src, dst) in enumerate(zip(ins, outs)):
            for j, (px, py) in enumerate(others):
                there = 2 * px + py
                send = _remote(src.at[there], dst.at[me], sems, 3 * w + j, (px, py, c))
                landed = _remote(dst.at[there], dst.at[there], sems, 3 * w + j, (px, py, c))
                table.append((send, landed))
        return table

    def start(self, ins, outs, sems):
        for send, _ in self._copies(ins, outs, sems):
            send.start()

    def mid(self, ins, outs, sems):
        pass

    def finish(self, ins, outs, sems):
        table = self._copies(ins, outs, sems)
        for _, landed in table:
            landed.wait_recv()
        for send, _ in table:
            send.wait_send()


def _sibling_join(shards):
    n = len(shards)

    def body(*refs):
        ins, outs, sems = refs[:n], refs[n:2 * n], refs[2 * n:2 * n + 2]
        x, y, c, _ = _place()
        pend = []
        for w in range(n):
            rows = ins[w].shape[1]
            mine, theirs = _half_rows(rows, c), _half_rows(rows, 1 - c)
            cp = _remote(ins[w].at[:, mine], outs[w].at[:, mine], sems, w, (x, y, 1 - c))
            cp.start()
            pend.append((cp, _remote(ins[w].at[:, theirs], outs[w].at[:, theirs], sems, w, (x, y, 1 - c))))
        for cp, landed in pend:
            landed.wait_recv()
            cp.wait_send()

    out_shapes = [jax.ShapeDtypeStruct(a.shape, a.dtype) for a in shards]
    return _hbm_call(body, "grad_sibling_join", shards, out_shapes, n, aliases={w: w for w in range(n)})


SMALL_ROWS = 136


def _all_reduce_small(vec):
    def body(v_ref, o_ref, buf, send, recv, loc):
        x, y, c, _ = _place()
        me = 4 * x + 2 * y + c
        lc = pltpu.make_async_copy(v_ref, buf.at[me], loc.at[0])
        lc.start()
        cps = []
        for k in range(1, 8):
            fx, fy, fc = (k >> 2) & 1, (k >> 1) & 1, k & 1
            peer = (x ^ fx, y ^ fy, c ^ fc)
            cp = pltpu.make_async_remote_copy(src_ref=v_ref, dst_ref=buf.at[me], send_sem=send.at[k - 1],
                                              recv_sem=recv.at[k - 1], device_id=peer, device_id_type=MESH)
            cp.start()
            cps.append((cp, 4 * peer[0] + 2 * peer[1] + peer[2]))
        for k, (cp, src) in enumerate(cps):
            pltpu.make_async_remote_copy(src_ref=v_ref, dst_ref=buf.at[src], send_sem=send.at[k], recv_sem=recv.at[k],
                                         device_id=(x, y, c), device_id_type=MESH).wait_recv()
        for cp, _ in cps:
            cp.wait_send()
        lc.wait()
        total = buf[0]
        for k in range(1, 8):
            total = total + buf[k]
        o_ref[...] = total

    vm = pl.BlockSpec(memory_space=pltpu.VMEM)
    return pl.pallas_call(
        body, name="all_reduce_small", in_specs=[vm], out_specs=vm,
        out_shape=jax.ShapeDtypeStruct(vec.shape, F32),
        scratch_shapes=[pltpu.VMEM((8,) + vec.shape, F32), pltpu.SemaphoreType.DMA((7,)),
                        pltpu.SemaphoreType.DMA((7,)), pltpu.SemaphoreType.DMA((1,))],
    )(vec)


SUM_ROWS = 256


def _add_pairs(part, theirs, name):
    four, rh, cc = theirs.shape
    tr = min(SUM_ROWS, rh)
    halves = part.reshape(four, 2, rh, cc)

    def body(a_ref, b_ref, o_ref):
        mine = a_ref[0, lax.axis_index("c")]
        o_ref[0] = (mine.astype(F32) + b_ref[0].astype(F32)).astype(o_ref.dtype)

    spec = pl.BlockSpec((1, tr, cc), lambda k, r: (k, r, 0))
    return pl.pallas_call(
        body, name=name, grid=(four, rh // tr),
        in_specs=[pl.BlockSpec((1, 2, tr, cc), lambda k, r: (k, 0, r, 0)), spec], out_specs=spec,
        out_shape=jax.ShapeDtypeStruct(theirs.shape, theirs.dtype),
        compiler_params=_params(("parallel", "parallel")),
    )(halves, theirs)


def _sum_chips(own, arrived, core, name, layer, n_layers, into=None):
    four, rh, cc = own.shape
    tr = min(SUM_ROWS, rh)
    nr = rh // tr

    def body(c_ref, own_ref, arr_ref, *rest):
        o_ref = rest[-1]
        x, y = lax.axis_index("x"), lax.axis_index("y")
        tot = own_ref[2 * x + y].astype(F32)
        for px, py in ((1 - x, y), (x, 1 - y), (1 - x, 1 - y)):
            tot = tot + arr_ref[2 * px + py].astype(F32)
        o_ref[0] = tot

    blk = pl.BlockSpec((4, tr, cc), lambda r, c_ref: (0, r, 0))
    in_specs, args, aliases = [blk, blk], [core, own, arrived], {}
    if into is not None:
        in_specs.append(pl.BlockSpec(memory_space=pl.ANY))
        args.append(into)
        aliases = {3: 0}
    return pl.pallas_call(
        body, name=name,
        grid_spec=pltpu.PrefetchScalarGridSpec(
            num_scalar_prefetch=1, grid=(nr,), in_specs=in_specs,
            out_specs=pl.BlockSpec((1, tr, cc), lambda r, c_ref: (layer, c_ref[0] * nr + r, 0))),
        out_shape=jax.ShapeDtypeStruct((n_layers, 2 * rh, cc), F32), input_output_aliases=aliases,
        compiler_params=_params(("parallel",)),
    )(*args)


ADAM_ROWS = 256


def _adamw(w, g, m, v, name):
    shape = w.shape
    cc = shape[-1]
    flat = lambda a: a.reshape(-1, cc)
    rows = flat(w).shape[0]
    tr = min(ADAM_ROWS, rows)
    assert rows % tr == 0
    c1 = 1.0 - ADAM_B1 ** ADAM_STEP
    c2 = 1.0 - ADAM_B2 ** ADAM_STEP

    def body(w_ref, g_ref, m_ref, v_ref, d_ref, nm_ref, nv_ref):
        gv = g_ref[...]
        nm = ADAM_B1 * m_ref[...] + (1.0 - ADAM_B1) * gv
        nv = ADAM_B2 * v_ref[...] + (1.0 - ADAM_B2) * (gv * gv)
        nm_ref[...] = nm
        nv_ref[...] = nv
        d_ref[...] = -ADAM_LR * ((nm / c1) / (jnp.sqrt(nv / c2) + ADAM_EPS) + ADAM_WD * w_ref[...])

    spec = pl.BlockSpec((tr, cc), lambda i: (i, 0))
    sh = jax.ShapeDtypeStruct((rows, cc), F32)
    outs = pl.pallas_call(
        body, name=name, grid=(rows // tr,), in_specs=[spec] * 4, out_specs=[spec] * 3, out_shape=[sh] * 3,
        compiler_params=_params(("parallel",)),
    )(flat(w), flat(g), flat(m), flat(v))
    return [o.reshape(shape) for o in outs]


def _pack_small(g_pre, g_post, sinks_a, b_f_c, loss_row):
    pad = lambda a: jnp.pad(a.reshape(1, -1).astype(F32), ((0, 0), (0, LANES - a.size)))
    rows = [g_pre.astype(F32).reshape(-1, LANES), g_post.astype(F32).reshape(-1, LANES), pad(sinks_a), pad(b_f_c), loss_row]
    packed = jnp.concatenate(rows, axis=0)
    return jnp.pad(packed, ((0, SMALL_ROWS - packed.shape[0]), (0, 0)))


def _unpack_small(p):
    n = DEPTH * D_MODEL // LANES
    return (p[:n].reshape(DEPTH, D_MODEL), p[n:2 * n].reshape(DEPTH, D_MODEL), p[2 * n, :2 * N_HEADS].reshape(2, N_HEADS),
            p[2 * n + 1, :N_HEADS].reshape(1, N_HEADS), p[2 * n + 2, 0])


def kernel(x, g_pre, g_post, w_in_a, w_out_a, sinks_a, w_in_b, w_out_b, w_in_c, b_f_c, w_out_c, loss_target, m_g_pre, m_g_post, m_w_in_a, m_w_out_a, m_sinks_a, m_w_in_b, m_w_out_b, m_w_in_c, m_b_f_c, m_w_out_c, v_g_pre, v_g_post, v_w_in_a, v_w_out_a, v_sinks_a, v_w_in_b, v_w_out_b, v_w_in_c, v_b_f_c, v_w_out_c):
    big_w = [w_in_a, w_out_a, w_in_b, w_out_b, w_in_c, w_out_c]
    big_m = [m_w_in_a, m_w_out_a, m_w_in_b, m_w_out_b, m_w_in_c, m_w_out_c]
    big_v = [v_w_in_a, v_w_out_a, v_w_in_b, v_w_out_b, v_w_in_c, v_w_out_c]

    chip = 2 * lax.axis_index("x") + lax.axis_index("y")
    core = lax.axis_index("c").astype(jnp.int32).reshape(1)
    by_kind = {0: (w_in_a, w_out_a), 1: (w_in_b, w_out_b), 2: (w_in_c, w_out_c)}
    shards = {}
    for i in range(DEPTH):
        kind, j = _layer_kind(i)
        shards[("in", i)] = by_kind[kind][0][j].astype(BF16)
        shards[("out", i)] = by_kind[kind][1][j].astype(BF16)

    res = _forward_backward(x[0], loss_target[0], g_pre, g_post, sinks_a, b_f_c, shards, chip, core)
    names = ["w_in_a", "w_out_a", "w_in_b", "w_out_b", "w_in_c", "w_out_c"]
    grads = _sibling_join([res["reduced"][(side, kind)] for kind in range(3) for side in ("in", "out")])

    small = _unpack_small(_all_reduce_small(
        _pack_small(res["g_pre"], res["g_post"], res["sinks_a"], res["b_f_c"], res["loss"])))
    g_small, loss = small[:4], small[4]

    zero_row = jnp.zeros((1, LANES), F32)
    pk = lambda a: _pack_small(a[0], a[1], a[2], a[3], zero_row)
    sm = _adamw(pk([g_pre, g_post, sinks_a, b_f_c]), pk(g_small), pk([m_g_pre, m_g_post, m_sinks_a, m_b_f_c]),
                pk([v_g_pre, v_g_post, v_sinks_a, v_b_f_c]), "adamw_small")
    sm = [_unpack_small(a)[:4] for a in sm]
    bigs = [_adamw(w, g, m, v, f"adamw_{nm}") for w, g, m, v, nm in zip(big_w, grads, big_m, big_v, names)]

    def ordered(small4, big6):
        return [small4[0], small4[1], big6[0], big6[1], small4[2], big6[2], big6[3], big6[4], small4[3], big6[5]]

    out = [loss, res["dx"][None], *ordered(g_small, grads)]
    for k in range(3):
        out += ordered(sm[k], [b[k] for b in bigs])
    return tuple(out)
```

```python
import functools
import math

import numpy as np
import jax
import jax.numpy as jnp
from jax import lax
from jax.experimental import pallas as pl
from jax.experimental.pallas import tpu as pltpu

F32 = jnp.float32
BF16 = jnp.bfloat16

D_MODEL = 2048
DEPTH = 4
N_HEADS = 32
HEAD_DIM = 64
LANES = 128
N_PAIRS = N_HEADS * HEAD_DIM // LANES
BRANCH = N_HEADS * HEAD_DIM
N_KV_A = 4
KV_A = N_KV_A * HEAD_DIM
WINDOW = 128
NORM_EPS = 1e-6
NEG = -1e30
Q_SCALE = HEAD_DIM ** -0.5

A_QKV = BRANCH + 2 * KV_A
B_QKV = 3 * BRANCH

ADAM_LR = 0.001
ADAM_B1 = 0.9
ADAM_B2 = 0.999
ADAM_EPS = 1e-08
ADAM_WD = 0.01
ADAM_STEP = 10

MESH = pl.DeviceIdType.MESH

_NT = (((1,), (1,)), ((), ()))
_TN = (((0,), (0,)), ((), ()))


def _params(sem=None):
    return pltpu.CompilerParams(dimension_semantics=sem)


def _matmul(a, b, *, mode, out_dtype, name, n=None, b_off=0, tm=1024, tn=1024, tk=2048, col_blocks=None, exchange=None):
    (m, k), nn = a.shape, ((n or b.shape[1]) if mode == "nn" else b.shape[0])
    tm, tn, tk = min(tm, m), min(tn, nn), min(tk, k)
    assert m % tm == 0 and nn % tn == 0 and k % tk == 0, (name, m, nn, k, tm, tn, tk)
    nk = k // tk

    def body(a_ref, b_ref, o_ref, acc_ref):
        kk = pl.program_id(2)
        if mode == "nn":
            p = jnp.dot(a_ref[...], b_ref[...], preferred_element_type=F32)
        else:
            p = lax.dot_general(a_ref[...], b_ref[...], _NT, preferred_element_type=F32)
        if nk == 1:
            o_ref[...] = p.astype(o_ref.dtype).reshape(o_ref.shape)
        else:
            @pl.when(kk == 0)
            def _():
                acc_ref[...] = p

            @pl.when(kk > 0)
            def _():
                acc_ref[...] += p

            @pl.when(kk == nk - 1)
            def _():
                o_ref[...] = acc_ref[...].astype(o_ref.dtype).reshape(o_ref.shape)

    if mode == "nn":
        in_specs = [pl.BlockSpec((tm, tk), lambda i, j, kk: (i, kk)),
                    pl.BlockSpec((tk, tn), lambda i, j, kk: (kk, j + b_off))]
    else:
        in_specs = [pl.BlockSpec((tm, tk), lambda i, j, kk: (i, kk)),
                    pl.BlockSpec((tn, tk), lambda i, j, kk: (j, kk))]
    if col_blocks is None:
        out_spec = pl.BlockSpec((tm, tn), lambda i, j, kk: (i, j))
        out_shape = jax.ShapeDtypeStruct((m, nn), out_dtype)
    else:
        per = nn // col_blocks // tn
        assert per * tn * col_blocks == nn, (name, nn, tn, col_blocks)
        out_spec = pl.BlockSpec((1, tm, tn), lambda i, j, kk: (j // per, i, j % per))
        out_shape = jax.ShapeDtypeStruct((col_blocks, m, nn // col_blocks), out_dtype)
    (res,), arrived = _grid_call(
        body, name=name, grid=(m // tm, nn // tn, nk), in_specs=in_specs, out_specs=[out_spec], out_shape=[out_shape],
        args=(a, b), scratch_shapes=[pltpu.VMEM((tm, tn), F32)], semantics=("parallel", "parallel", "arbitrary"),
        exchange=exchange)
    return res if exchange is None else (res, arrived)


ROW_TILE = 256


def _row_call(body, name, ins, outs, *, s):
    tr = min(ROW_TILE, s)
    spec = {"row": lambda sh: pl.BlockSpec((tr, sh[1]), lambda i: (i, 0)),
            "vec": lambda sh: pl.BlockSpec((1, sh[1]), lambda i: (0, 0)),
            "col": lambda sh: pl.BlockSpec((sh[0], tr), lambda i: (0, i))}
    in_specs = [spec[kind](a.shape) for a, kind in ins]
    out_specs = [spec[kind](sh.shape) for sh, kind in outs]
    return pl.pallas_call(
        body, name=name, grid=(s // tr,), in_specs=in_specs, out_specs=out_specs,
        out_shape=[sh for sh, _ in outs],
        compiler_params=_params(("arbitrary",)),
    )(*[a for a, _ in ins])


def _rsqrt_ms(v):
    return lax.rsqrt(jnp.mean(v * v, axis=-1, keepdims=True) + NORM_EPS)


def _rmsnorm_fwd(x, g, name):
    s, d = x.shape

    def body(x_ref, g_ref, h_ref, ht_ref):
        xv = x_ref[...]
        h = xv * _rsqrt_ms(xv) * g_ref[...]
        h_ref[...] = h.astype(BF16)
        ht_ref[...] = h.T.astype(BF16)

    return _row_call(body, name, [(x, "row"), (g, "vec")],
                     [(jax.ShapeDtypeStruct((s, d), BF16), "row"), (jax.ShapeDtypeStruct((d, s), BF16), "col")], s=s)


def _gate_fwd(o, z, name):
    s, d = o.shape

    def body(o_ref, z_ref, u_ref, ut_ref):
        zv = z_ref[...]
        u = o_ref[...] * (zv * jax.nn.sigmoid(zv))
        u_ref[...] = u.astype(BF16)
        ut_ref[...] = u.T.astype(BF16)

    return _row_call(body, name, [(o, "row"), (z, "row")],
                     [(jax.ShapeDtypeStruct((s, d), BF16), "row"), (jax.ShapeDtypeStruct((d, s), BF16), "col")], s=s)


def _post_fwd(x, y, g, name):
    s, d = x.shape

    def body(x_ref, y_ref, g_ref, o_ref):
        yv = y_ref[...]
        o_ref[...] = x_ref[...] + yv * _rsqrt_ms(yv) * g_ref[...]

    return _row_call(body, name, [(x, "row"), (y, "row"), (g, "vec")],
                     [(jax.ShapeDtypeStruct((s, d), F32), "row")], s=s)[0]


def _loss_and_grad(x, target):
    s, d = x.shape

    def body(x_ref, t_ref, dx_ref, l_ref):
        err = x_ref[...] - t_ref[...]
        dx_ref[...] = err * (1.0 / d)
        part = jnp.sum(jnp.sum(err * err, axis=1, keepdims=True), axis=0, keepdims=True) * (0.5 / d)

        @pl.when(pl.program_id(0) == 0)
        def _():
            l_ref[...] = jnp.zeros_like(l_ref)

        l_ref[...] += jnp.broadcast_to(part, l_ref.shape)

    return _row_call(body, "loss_head", [(x, "row"), (target, "row")],
                     [(jax.ShapeDtypeStruct((s, d), F32), "row"),
                      (jax.ShapeDtypeStruct((1, LANES), F32), "vec")], s=s)


def _norm_bwd_rows(dn, v, g):
    r = _rsqrt_ms(v)
    a = dn * g
    dv = r * (a - v * (r * r) * jnp.mean(a * v, axis=-1, keepdims=True))
    return dv, dn * v * r


def _post_bwd(dx, y, g, name):
    s, d = dx.shape

    def body(dx_ref, y_ref, g_ref, dy_ref, dg_ref):
        dy, dg = _norm_bwd_rows(dx_ref[...], y_ref[...], g_ref[...])
        dy_ref[...] = dy.astype(BF16)

        @pl.when(pl.program_id(0) == 0)
        def _():
            dg_ref[...] = jnp.zeros_like(dg_ref)

        dg_ref[...] += jnp.sum(dg, axis=0, keepdims=True)

    return _row_call(body, name, [(dx, "row"), (y, "row"), (g, "vec")],
                     [(jax.ShapeDtypeStruct((s, d), BF16), "row"),
                      (jax.ShapeDtypeStruct((1, d), F32), "vec")], s=s)


def _gate_bwd(du, o, z, name):
    s, d = du.shape

    def body(du_ref, o_ref, z_ref, do_ref, dz_ref):
        duv, zv = du_ref[...], z_ref[...]
        sig = jax.nn.sigmoid(zv)
        do_ref[...] = (duv * (zv * sig)).astype(BF16)
        dz_ref[...] = (duv * o_ref[...] * (sig * (1.0 + zv * (1.0 - sig)))).astype(BF16)

    return _row_call(body, name, [(du, "row"), (o, "row"), (z, "row")],
                     [(jax.ShapeDtypeStruct((s, d), BF16), "row"),
                      (jax.ShapeDtypeStruct((s, d), BF16), "row")], s=s)


def _pre_bwd(dx, dhs, x, g, name):
    s, d = dx.shape
    n_dh = len(dhs)

    def body(*refs):
        dx_ref, dh_refs, (x_ref, g_ref, o_ref, dg_ref) = refs[0], refs[1:1 + n_dh], refs[1 + n_dh:]
        dh = dh_refs[0][...].astype(F32)
        for r in dh_refs[1:]:
            dh = dh + r[...].astype(F32)
        dv, dg = _norm_bwd_rows(dh, x_ref[...], g_ref[...])
        o_ref[...] = dx_ref[...] + dv

        @pl.when(pl.program_id(0) == 0)
        def _():
            dg_ref[...] = jnp.zeros_like(dg_ref)

        dg_ref[...] += jnp.sum(dg, axis=0, keepdims=True)

    return _row_call(body, name, [(dx, "row")] + [(h, "row") for h in dhs] + [(x, "row"), (g, "vec")],
                     [(jax.ShapeDtypeStruct((s, d), F32), "row"),
                      (jax.ShapeDtypeStruct((1, d), F32), "vec")], s=s)


def _lane_is_first_head():
    return lax.broadcasted_iota(jnp.int32, (1, LANES), 1) < HEAD_DIM


def _bcast_lanes(col):
    return jnp.broadcast_to(col, (col.shape[0], LANES))


def _pair_spec(s, off=0, width=LANES):
    return pl.BlockSpec((s, width), lambda p: (0, p + off))


def _stack_heads(pair, first):
    return jnp.concatenate([jnp.where(first, pair, 0), jnp.where(first, 0, pair)], axis=0).astype(BF16)


def _stacked_mask(t, strict):
    row = lax.broadcasted_iota(jnp.int32, (2 * t, t), 0)
    col = lax.broadcasted_iota(jnp.int32, (2 * t, t), 1)
    query = jnp.where(row >= t, row - t, row)
    return col < query if strict else col <= query


LOOP_UNROLL = 2


def _two_at_a_time(n, step, carry):
    def group(jj, c):
        for k in range(LOOP_UNROLL):
            c = step(LOOP_UNROLL * jj + k, c)
        return c

    carry = lax.fori_loop(0, n // LOOP_UNROLL, group, carry)
    return lax.fori_loop(LOOP_UNROLL * (n // LOOP_UNROLL), n, step, carry)


def _rowsum_heads(prod, first):
    return (jnp.sum(jnp.where(first, prod, 0.0), axis=1, keepdims=True),
            jnp.sum(jnp.where(first, 0.0, prod), axis=1, keepdims=True))


def _softplus_parts(z):
    e = jnp.exp(-jnp.abs(z))
    sp = jnp.maximum(z, 0.0) + jnp.log(1.0 + e)
    r = 1.0 / (1.0 + e)
    return sp, jnp.where(z >= 0, r, e * r)


def _split_dot(x, t):
    hi = x.astype(BF16)
    lo = (x - hi.astype(F32)).astype(BF16)
    return jnp.dot(hi, t, preferred_element_type=F32) + jnp.dot(lo, t, preferred_element_type=F32)


def _sb_tile(s):
    return min(256, s)


def _attn_b_fwd(qkv, name, exchange=None):
    s = qkv.shape[0]
    t = _sb_tile(s)
    nq = s // t

    def body(q_ref, k_ref, v_ref, o_ref, lt_ref):
        first = _lane_is_first_head()
        before = _stacked_mask(t, strict=True)
        tri = (lax.broadcasted_iota(jnp.int32, (t, t), 0) >= lax.broadcasted_iota(jnp.int32, (t, t), 1)).astype(BF16)

        def tile(j, carry, diag, qs):
            c, acc = carry
            c0 = pl.multiple_of(j * t, t)
            k2 = k_ref[pl.ds(c0, t), :]
            v2 = v_ref[pl.ds(c0, t), :]
            z = lax.dot_general(qs, k2, _NT, preferred_element_type=F32)
            sp, _ = _softplus_parts(z)
            lf = jnp.where(before, -sp, 0.0) if diag else -sp
            incl = jnp.dot(lf.astype(BF16), tri, preferred_element_type=F32)
            a = jnp.exp(z + c + incl)
            if diag:
                a = jnp.where(before, a, 0.0)
            pv = jnp.dot(a.astype(BF16), v2, preferred_element_type=F32)
            return c + incl[:, 0:1], acc + jnp.where(first, pv[:t], pv[t:])

        def qblock(i, _):
            r0 = pl.multiple_of(i * t, t)
            qs = _stack_heads(q_ref[pl.ds(r0, t), :] * Q_SCALE, first)
            carry = tile(i, (jnp.zeros((2 * t, 1), F32), jnp.zeros((t, LANES), F32)), True, qs)
            carry = _two_at_a_time(i, lambda j, c: tile(i - 1 - j, c, False, qs), carry)
            o_ref[pl.ds(r0, t), :] = carry[1]
            lt_ref[pl.ds(r0, t), 0:LANES] = _bcast_lanes(carry[0][:t])
            lt_ref[pl.ds(r0, t), LANES:2 * LANES] = _bcast_lanes(carry[0][t:])
            return 0

        lax.fori_loop(0, nq, qblock, 0)

    return _grid_call(
        body, name=name, grid=(N_PAIRS,),
        in_specs=[_pair_spec(s), _pair_spec(s, N_PAIRS), _pair_spec(s, 2 * N_PAIRS)],
        out_specs=[_pair_spec(s), _stat_spec(s)],
        out_shape=[jax.ShapeDtypeStruct((s, BRANCH), F32), jax.ShapeDtypeStruct((s, N_HEADS * LANES), F32)],
        args=(qkv, qkv, qkv), semantics=("parallel",), exchange=exchange)


def _attn_b_bwd(qkv, ltot, do, name, exchange=None):
    s = qkv.shape[0]
    t = _sb_tile(s)
    nq = s // t

    def body(q_ref, k_ref, v_ref, lt_ref, do_ref, dq_ref, dk_ref, dv_ref, dk_acc, dv_acc):
        first = _lane_is_first_head()
        before = _stacked_mask(t, strict=True)
        tri = (lax.broadcasted_iota(jnp.int32, (t, t), 0) <= lax.broadcasted_iota(jnp.int32, (t, t), 1)).astype(BF16)
        dk_acc[...] = jnp.zeros_like(dk_acc)
        dv_acc[...] = jnp.zeros_like(dv_acc)

        def tile(j, carry, diag, qs, dos, lt):
            p_l, p_g, dq_acc = carry
            c0 = pl.multiple_of(j * t, t)
            k2 = k_ref[pl.ds(c0, t), :]
            v2 = v_ref[pl.ds(c0, t), :]
            z = lax.dot_general(qs, k2, _NT, preferred_element_type=F32)
            sp, sig = _softplus_parts(z)
            lf = jnp.where(before, -sp, 0.0) if diag else -sp
            pref_l = jnp.dot(lf.astype(BF16), tri, preferred_element_type=F32)
            a = jnp.exp(z + ((lt - p_l) - pref_l + lf))
            if diag:
                a = jnp.where(before, a, 0.0)
            g = a * lax.dot_general(dos, v2, _NT, preferred_element_type=F32)
            pref_g = jnp.dot(g.astype(BF16), tri, preferred_element_type=F32)
            dz = g - sig * (p_g + pref_g)
            if diag:
                dz = jnp.where(before, dz, 0.0)
            dzb = dz.astype(BF16)
            dq = jnp.dot(dzb, k2, preferred_element_type=F32)
            dk_acc[pl.ds(c0, t), :] += lax.dot_general(dzb, qs, _TN, preferred_element_type=F32)
            dv_acc[pl.ds(c0, t), :] += lax.dot_general(a.astype(BF16), dos, _TN, preferred_element_type=F32)
            return p_l + pref_l[:, t - 1:t], p_g + pref_g[:, t - 1:t], dq_acc + jnp.where(first, dq[:t], dq[t:])

        def qblock(i, _):
            r0 = pl.multiple_of(i * t, t)
            qs = _stack_heads(q_ref[pl.ds(r0, t), :] * Q_SCALE, first)
            dos = _stack_heads(do_ref[pl.ds(r0, t), :], first)
            lt = jnp.concatenate([lt_ref[pl.ds(r0, t), 0:1], lt_ref[pl.ds(r0, t), LANES:LANES + 1]], axis=0)
            zero = jnp.zeros((2 * t, 1), F32)
            carry = (zero, zero, jnp.zeros((t, LANES), F32))
            carry = _two_at_a_time(i, lambda j, c: tile(j, c, False, qs, dos, lt), carry)
            carry = tile(i, carry, True, qs, dos, lt)
            dq_ref[pl.ds(r0, t), :] = (carry[2] * Q_SCALE).astype(BF16)
            return 0

        lax.fori_loop(0, nq, qblock, 0)
        dk_ref[...] = dk_acc[...].astype(BF16)
        dv_ref[...] = dv_acc[...].astype(BF16)

    out = jax.ShapeDtypeStruct((s, BRANCH), BF16)
    return _grid_call(
        body, name=name, grid=(N_PAIRS,),
        in_specs=[_pair_spec(s), _pair_spec(s, N_PAIRS), _pair_spec(s, 2 * N_PAIRS), _stat_spec(s), _pair_spec(s)],
        out_specs=[_pair_spec(s)] * 3, out_shape=[out] * 3,
        scratch_shapes=[pltpu.VMEM((s, LANES), F32), pltpu.VMEM((s, LANES), F32)],
        args=(qkv, qkv, qkv, ltot, do), semantics=("parallel",), exchange=exchange)


def _fox_tile(s):
    return min(256, s)


def _stat_spec(s):
    return pl.BlockSpec((s, 2 * LANES), lambda p: (0, p))


def _cum_spec(nt, t):
    return pl.BlockSpec((1, nt, 2, t), lambda p: (p, 0, 0, 0))


def _attn_c_fwd(qkv, cum4, name, exchange=None):
    s = qkv.shape[0]
    t = _fox_tile(s)
    nq = s // t

    def body(q_ref, k_ref, v_ref, c_ref, o_ref, lse_ref):
        first = _lane_is_first_head()
        causal = _stacked_mask(t, strict=False)

        def tile(j, carry, diag, qs):
            c0 = pl.multiple_of(j * t, t)
            k2 = k_ref[pl.ds(c0, t), :]
            v2 = v_ref[pl.ds(c0, t), :]
            cs = c_ref[0, j]
            m_prev, l_prev, acc = carry
            z = lax.dot_general(qs, k2, _NT, preferred_element_type=F32)
            sc = jnp.concatenate([z[:t] - cs[0:1, :], z[t:] - cs[1:2, :]], axis=0)
            if diag:
                sc = jnp.where(causal, sc, NEG)
            m_new = jnp.maximum(m_prev, jnp.max(sc, axis=1, keepdims=True))
            alpha = jnp.exp(m_prev - m_new)
            p = jnp.exp(sc - m_new)
            l_new = alpha * l_prev + jnp.sum(p, axis=1, keepdims=True)
            pv = jnp.dot(p.astype(BF16), v2, preferred_element_type=F32)
            acc = jnp.where(first, acc * alpha[:t] + pv[:t], acc * alpha[t:] + pv[t:])
            return m_new, l_new, acc

        def qblock(i, _):
            r0 = pl.multiple_of(i * t, t)
            qs = _stack_heads(q_ref[pl.ds(r0, t), :] * Q_SCALE, first)
            carry = (jnp.full((2 * t, 1), NEG, F32), jnp.zeros((2 * t, 1), F32), jnp.zeros((t, LANES), F32))
            carry = _two_at_a_time(i, lambda j, c: tile(j, c, False, qs), carry)
            m, l, acc = tile(i, carry, True, qs)
            inv = 1.0 / l
            lse = m + jnp.log(l)
            o_ref[pl.ds(r0, t), :] = acc * jnp.where(first, inv[:t], inv[t:])
            lse_ref[pl.ds(r0, t), 0:LANES] = _bcast_lanes(lse[:t])
            lse_ref[pl.ds(r0, t), LANES:2 * LANES] = _bcast_lanes(lse[t:])
            return 0

        lax.fori_loop(0, nq, qblock, 0)

    return _grid_call(
        body, name=name, grid=(N_PAIRS,),
        in_specs=[_pair_spec(s), _pair_spec(s, N_PAIRS), _pair_spec(s, 2 * N_PAIRS), _cum_spec(nq, t)],
        out_specs=[_pair_spec(s), _stat_spec(s)],
        out_shape=[jax.ShapeDtypeStruct((s, BRANCH), F32), jax.ShapeDtypeStruct((s, N_HEADS * LANES), F32)],
        args=(qkv, qkv, qkv, cum4), semantics=("parallel",), exchange=exchange)


def _attn_c_bwd(qkv, cum4, o, lse, do, name, exchange=None):
    s = qkv.shape[0]
    t = _fox_tile(s)
    nq = s // t

    def body(q_ref, k_ref, v_ref, c_ref, o_ref, lse_ref, do_ref, dq_ref, dk_ref, dv_ref, dc_ref, dk_acc, dv_acc):
        first = _lane_is_first_head()
        causal = _stacked_mask(t, strict=False)
        eye = lax.broadcasted_iota(jnp.int32, (t, t), 0) == lax.broadcasted_iota(jnp.int32, (t, t), 1)
        dk_acc[...] = jnp.zeros_like(dk_acc)
        dv_acc[...] = jnp.zeros_like(dv_acc)
        dc_ref[...] = jnp.zeros_like(dc_ref)

        def tile(j, carry, diag, qs, dos, delta, lse):
            dq_acc, rs = carry
            c0 = pl.multiple_of(j * t, t)
            k2 = k_ref[pl.ds(c0, t), :]
            v2 = v_ref[pl.ds(c0, t), :]
            cs = c_ref[0, j]
            z = lax.dot_general(qs, k2, _NT, preferred_element_type=F32)
            sc = jnp.concatenate([z[:t] - cs[0:1, :], z[t:] - cs[1:2, :]], axis=0)
            p = jnp.exp(sc - lse)
            if diag:
                p = jnp.where(causal, p, 0.0)
            ds = p * (lax.dot_general(dos, v2, _NT, preferred_element_type=F32) - delta)
            dsb = ds.astype(BF16)
            dq = jnp.dot(dsb, k2, preferred_element_type=F32)
            dk_acc[pl.ds(c0, t), :] += lax.dot_general(dsb, qs, _TN, preferred_element_type=F32)
            dv_acc[pl.ds(c0, t), :] += lax.dot_general(p.astype(BF16), dos, _TN, preferred_element_type=F32)
            col_sums = jnp.concatenate([jnp.sum(ds[:t], axis=0, keepdims=True), jnp.sum(ds[t:], axis=0, keepdims=True)], axis=0)
            dc_ref[0, j] = dc_ref[0, j] - col_sums
            return dq_acc + jnp.where(first, dq[:t], dq[t:]), rs + jnp.sum(ds, axis=1, keepdims=True)

        def qblock(i, _):
            r0 = pl.multiple_of(i * t, t)
            do2 = do_ref[pl.ds(r0, t), :]
            qs = _stack_heads(q_ref[pl.ds(r0, t), :] * Q_SCALE, first)
            dos = _stack_heads(do2, first)
            delta = jnp.concatenate(_rowsum_heads(do2.astype(F32) * o_ref[pl.ds(r0, t), :], first), axis=0)
            lse = jnp.concatenate([lse_ref[pl.ds(r0, t), 0:1], lse_ref[pl.ds(r0, t), LANES:LANES + 1]], axis=0)
            carry = (jnp.zeros((t, LANES), F32), jnp.zeros((2 * t, 1), F32))
            carry = _two_at_a_time(i, lambda j, c: tile(j, c, False, qs, dos, delta, lse), carry)
            dq_acc, rs = tile(i, carry, True, qs, dos, delta, lse)
            dq_ref[pl.ds(r0, t), :] = (dq_acc * Q_SCALE).astype(BF16)
            as_row = lambda col_vec: jnp.sum(jnp.where(eye, col_vec, 0.0), axis=0, keepdims=True)
            dc_ref[0, i] = dc_ref[0, i] + jnp.concatenate([as_row(rs[:t]), as_row(rs[t:])], axis=0)
            return 0

        lax.fori_loop(0, nq, qblock, 0)
        dk_ref[...] = dk_acc[...].astype(BF16)
        dv_ref[...] = dv_acc[...].astype(BF16)

    out = jax.ShapeDtypeStruct((s, BRANCH), BF16)
    return _grid_call(
        body, name=name, grid=(N_PAIRS,),
        in_specs=[_pair_spec(s), _pair_spec(s, N_PAIRS), _pair_spec(s, 2 * N_PAIRS), _cum_spec(nq, t),
                  _pair_spec(s), _stat_spec(s), _pair_spec(s)],
        out_specs=[_pair_spec(s)] * 3 + [_cum_spec(nq, t)],
        out_shape=[out] * 3 + [jax.ShapeDtypeStruct(cum4.shape, F32)],
        scratch_shapes=[pltpu.VMEM((s, LANES), F32), pltpu.VMEM((s, LANES), F32)],
        args=(qkv, qkv, qkv, cum4, o, lse, do), semantics=("parallel",), exchange=exchange)


FG_CHUNK = 512


def _tri_dot3(x, t):
    hi = x.astype(BF16)
    r1 = x - hi.astype(F32)
    mid = r1.astype(BF16)
    lo = (r1 - mid.astype(F32)).astype(BF16)
    return (jnp.dot(hi, t, preferred_element_type=F32) + jnp.dot(mid, t, preferred_element_type=F32)
            + jnp.dot(lo, t, preferred_element_type=F32))


def _fgate_fwd(h, wf_t, b_col, name):
    s = h.shape[0]
    c = min(FG_CHUNK, s)

    def body(h_ref, w_ref, b_ref, xf_ref, cum_ref, carry_ref):
        @pl.when(pl.program_id(0) == 0)
        def _():
            carry_ref[...] = jnp.zeros_like(carry_ref)

        xf = lax.dot_general(w_ref[...], h_ref[...], _NT, preferred_element_type=F32) + b_ref[:, 0:1]
        xf_ref[...] = xf
        logf = jnp.minimum(xf, 0.0) - jnp.log(1.0 + jnp.exp(-jnp.abs(xf)))
        row = lax.broadcasted_iota(jnp.int32, (c, c), 0)
        col = lax.broadcasted_iota(jnp.int32, (c, c), 1)
        cum = _tri_dot3(logf, (row <= col).astype(BF16)) + carry_ref[:, 0:1]
        cum_ref[...] = cum
        carry_ref[...] = _bcast_lanes(cum[:, c - 1:c])

    out = jax.ShapeDtypeStruct((N_HEADS, s), F32)
    return pl.pallas_call(
        body, name=name, grid=(s // c,),
        in_specs=[pl.BlockSpec((c, D_MODEL), lambda i: (i, 0)),
                  pl.BlockSpec((N_HEADS, D_MODEL), lambda i: (0, 0)),
                  pl.BlockSpec((N_HEADS, LANES), lambda i: (0, 0))],
        out_specs=[pl.BlockSpec((N_HEADS, c), lambda i: (0, i))] * 2,
        out_shape=[out, out],
        scratch_shapes=[pltpu.VMEM((N_HEADS, LANES), F32)],
        compiler_params=_params(("arbitrary",)),
    )(h, wf_t, b_col)


def _fgate_bwd(dcum, xf, h, wf_t, name):
    s = h.shape[0]
    c = min(FG_CHUNK, s)
    n = s // c

    def body(dc_ref, xf_ref, h_ref, w_ref, dw_ref, dh_ref, db_ref, carry_ref):
        @pl.when(pl.program_id(0) == 0)
        def _():
            carry_ref[...] = jnp.zeros_like(carry_ref)
            dw_ref[...] = jnp.zeros_like(dw_ref)
            db_ref[...] = jnp.zeros_like(db_ref)

        row = lax.broadcasted_iota(jnp.int32, (c, c), 0)
        col = lax.broadcasted_iota(jnp.int32, (c, c), 1)
        dlogf = _tri_dot3(dc_ref[...], (row >= col).astype(BF16)) + carry_ref[:, 0:1]
        carry_ref[...] = _bcast_lanes(dlogf[:, 0:1])
        xf = xf_ref[...]
        e = jnp.exp(-jnp.abs(xf))
        r = 1.0 / (1.0 + e)
        dxf = dlogf * jnp.where(xf >= 0, e * r, r)
        db_ref[...] += _bcast_lanes(jnp.sum(dxf, axis=1, keepdims=True))
        dxb = dxf.astype(BF16)
        dw_ref[...] += jnp.dot(dxb, h_ref[...], preferred_element_type=F32)
        dh_ref[...] = lax.dot_general(dxb, w_ref[...], _TN, preferred_element_type=F32)

    rev = lambda i: n - 1 - i
    return pl.pallas_call(
        body, name=name, grid=(n,),
        in_specs=[pl.BlockSpec((N_HEADS, c), lambda i: (0, rev(i))),
                  pl.BlockSpec((N_HEADS, c), lambda i: (0, rev(i))),
                  pl.BlockSpec((c, D_MODEL), lambda i: (rev(i), 0)),
                  pl.BlockSpec((N_HEADS, D_MODEL), lambda i: (0, 0))],
        out_specs=[pl.BlockSpec((N_HEADS, D_MODEL), lambda i: (0, 0)),
                   pl.BlockSpec((c, D_MODEL), lambda i: (rev(i), 0)),
                   pl.BlockSpec((N_HEADS, LANES), lambda i: (0, 0))],
        out_shape=[jax.ShapeDtypeStruct((N_HEADS, D_MODEL), F32), jax.ShapeDtypeStruct((s, D_MODEL), F32),
                   jax.ShapeDtypeStruct((N_HEADS, LANES), F32)],
        scratch_shapes=[pltpu.VMEM((N_HEADS, LANES), F32)],
        compiler_params=_params(("arbitrary",)),
    )(dcum, xf, h, wf_t)


def _to_cum4(v, t):
    s = v.shape[1]
    return v.reshape(N_PAIRS, 2, s // t, t).transpose(0, 2, 1, 3)


def _from_cum4(v4):
    p, nt, two, t = v4.shape
    return v4.transpose(0, 2, 1, 3).reshape(p * two, nt * t)


def _alibi_slopes():
    return (2.0 ** (-8.0 * np.arange(1, N_HEADS + 1, dtype=np.float32) / N_HEADS)).astype(np.float32)


def _per_head_lanes(v):
    return jnp.repeat(v.astype(F32).reshape(N_PAIRS, 1, 2), LANES, axis=2)


def _attn_a_specs(s):
    q = _pair_spec(s)
    k = pl.BlockSpec((s, LANES), lambda p: (0, N_PAIRS + p // 8))
    v = pl.BlockSpec((s, LANES), lambda p: (0, N_PAIRS + KV_A // LANES + p // 8))
    head = pl.BlockSpec((1, 1, 2 * LANES), lambda p: (p, 0, 0))
    return q, k, v, head


def _attn_a_geometry(p, slope_ref, sink_ref):
    kv_half = (p // 4) % 2
    kv_first = kv_half == 0
    lane_first = _lane_is_first_head()
    kv_lanes = (lax.broadcasted_iota(jnp.int32, (1, LANES), 1) // HEAD_DIM) == kv_half
    row = lax.broadcasted_iota(jnp.int32, (2 * WINDOW, 2 * WINDOW), 0)
    cj = lax.broadcasted_iota(jnp.int32, (2 * WINDOW, 2 * WINDOW), 1)
    second = row >= WINDOW
    dist = WINDOW + jnp.where(second, row - WINDOW, row) - cj
    valid = (dist >= 0) & (dist < WINDOW)
    per_row = lambda ref: jnp.where(second[:, 0:1], ref[0, :, LANES:LANES + 1], ref[0, :, 0:1])
    return kv_first, lane_first, kv_lanes, per_row(slope_ref) * dist.astype(F32), valid, per_row(sink_ref)


def _swap_halves(x):
    return pltpu.roll(x, HEAD_DIM, 1)


def _attn_a_fwd(qkv, slopes, sinks, name, exchange=None):
    s = qkv.shape[0]
    nb = s // WINDOW

    def body(q_ref, k_ref, v_ref, sl_ref, sk_ref, o_ref, lse_ref):
        kv_first, lane_first, kv_lanes, bias, valid, sink = _attn_a_geometry(pl.program_id(0), sl_ref, sk_ref)

        def block(r0, k0, width):
            q2 = q_ref[pl.ds(r0, WINDOW), :].astype(F32) * Q_SCALE
            q2r = _swap_halves(q2)
            xs = jnp.concatenate([jnp.where(kv_first, q2, q2r), jnp.where(kv_first, q2r, q2)], axis=0).astype(BF16)
            km = jnp.where(kv_lanes, k_ref[pl.ds(k0, width), :], 0).astype(BF16)
            vm = jnp.where(kv_lanes, v_ref[pl.ds(k0, width), :], 0).astype(BF16)
            sc = lax.dot_general(xs, km, _NT, preferred_element_type=F32) - bias[:, 2 * WINDOW - width:]
            sc = jnp.where(valid[:, 2 * WINDOW - width:], sc, NEG)
            m = jnp.maximum(jnp.max(sc, axis=1, keepdims=True), sink)
            pr = jnp.exp(sc - m)
            l = jnp.sum(pr, axis=1, keepdims=True) + jnp.exp(sink - m)
            os = jnp.dot(pr.astype(BF16), vm, preferred_element_type=F32) * (1.0 / l)
            lse = m + jnp.log(l)
            lse_ref[pl.ds(r0, WINDOW), 0:LANES] = _bcast_lanes(lse[:WINDOW])
            lse_ref[pl.ds(r0, WINDOW), LANES:2 * LANES] = _bcast_lanes(lse[WINDOW:])
            oa = jnp.where(kv_first, os[:WINDOW], _swap_halves(os[:WINDOW]))
            ob = jnp.where(kv_first, _swap_halves(os[WINDOW:]), os[WINDOW:])
            o_ref[pl.ds(r0, WINDOW), :] = jnp.where(lane_first, oa, ob)

        block(0, 0, WINDOW)

        def loop(n, _):
            r0 = pl.multiple_of(n * WINDOW, WINDOW)
            block(r0, pl.multiple_of(r0 - WINDOW, WINDOW), 2 * WINDOW)
            return 0

        _two_at_a_time(nb - 1, lambda n, c: loop(n + 1, c), 0)

    q, k, v, head = _attn_a_specs(s)
    return _grid_call(
        body, name=name, grid=(N_PAIRS,),
        in_specs=[q, k, v, head, head],
        out_specs=[_pair_spec(s), _stat_spec(s)],
        out_shape=[jax.ShapeDtypeStruct((s, BRANCH), F32), jax.ShapeDtypeStruct((s, N_HEADS * LANES), F32)],
        args=(qkv, qkv, qkv, slopes, sinks), semantics=("parallel",), exchange=exchange)


def _attn_a_bwd(qkv, slopes, sinks, o, lse, do, name, exchange=None):
    s = qkv.shape[0]
    nb = s // WINDOW

    def body(q_ref, k_ref, v_ref, sl_ref, sk_ref, o_ref, lse_ref, do_ref, dq_ref, dk_ref, dv_ref, dsk_ref):
        p_id = pl.program_id(0)
        kv_first, lane_first, kv_lanes, bias, valid, sink = _attn_a_geometry(p_id, sl_ref, sk_ref)

        @pl.when(p_id % 8 == 0)
        def _():
            dk_ref[...] = jnp.zeros_like(dk_ref)
            dv_ref[...] = jnp.zeros_like(dv_ref)

        def align(v2):
            v2r = _swap_halves(v2)
            both = jnp.concatenate([jnp.where(kv_first, v2, v2r), jnp.where(kv_first, v2r, v2)], axis=0)
            return jnp.where(kv_lanes, both, 0.0).astype(BF16)

        def block(r0, k0, width, sink_sum):
            xq = align(q_ref[pl.ds(r0, WINDOW), :].astype(F32) * Q_SCALE)
            do2 = do_ref[pl.ds(r0, WINDOW), :].astype(F32)
            xdo = align(do2)
            delta = jnp.concatenate(_rowsum_heads(do2 * o_ref[pl.ds(r0, WINDOW), :], lane_first), axis=0)
            lse = jnp.concatenate([lse_ref[pl.ds(r0, WINDOW), 0:1], lse_ref[pl.ds(r0, WINDOW), LANES:LANES + 1]], axis=0)
            km = jnp.where(kv_lanes, k_ref[pl.ds(k0, width), :], 0).astype(BF16)
            vm = jnp.where(kv_lanes, v_ref[pl.ds(k0, width), :], 0).astype(BF16)
            sc = lax.dot_general(xq, km, _NT, preferred_element_type=F32) - bias[:, 2 * WINDOW - width:]
            pr = jnp.where(valid[:, 2 * WINDOW - width:], jnp.exp(sc - lse), 0.0)
            ds = pr * (lax.dot_general(xdo, vm, _NT, preferred_element_type=F32) - delta)
            dsb = ds.astype(BF16)
            dq_al = jnp.dot(dsb, km, preferred_element_type=F32)
            dk_ref[pl.ds(k0, width), :] += lax.dot_general(dsb, xq, _TN, preferred_element_type=F32)
            dv_ref[pl.ds(k0, width), :] += lax.dot_general(pr.astype(BF16), xdo, _TN, preferred_element_type=F32)
            dqa = jnp.where(kv_first, dq_al[:WINDOW], _swap_halves(dq_al[:WINDOW]))
            dqb = jnp.where(kv_first, _swap_halves(dq_al[WINDOW:]), dq_al[WINDOW:])
            dq_ref[pl.ds(r0, WINDOW), :] = (jnp.where(lane_first, dqa, dqb) * Q_SCALE).astype(BF16)
            return sink_sum + jnp.exp(sink - lse) * delta

        sink_sum = block(0, 0, WINDOW, jnp.zeros((2 * WINDOW, 1), F32))

        def loop(n, c):
            r0 = pl.multiple_of(n * WINDOW, WINDOW)
            return block(r0, pl.multiple_of(r0 - WINDOW, WINDOW), 2 * WINDOW, c)

        sink_sum = _two_at_a_time(nb - 1, lambda n, c: loop(n + 1, c), sink_sum)
        dsk_ref[0, :, 0:LANES] = jnp.broadcast_to(-jnp.sum(sink_sum[:WINDOW], axis=0, keepdims=True), (1, LANES))
        dsk_ref[0, :, LANES:2 * LANES] = jnp.broadcast_to(-jnp.sum(sink_sum[WINDOW:], axis=0, keepdims=True), (1, LANES))

    q, k, v, head = _attn_a_specs(s)
    kv_out = pl.BlockSpec((s, LANES), lambda p: (0, p // 8))
    return _grid_call(
        body, name=name, grid=(N_PAIRS,),
        in_specs=[q, k, v, head, head, _pair_spec(s), _stat_spec(s), _pair_spec(s)],
        out_specs=[_pair_spec(s), kv_out, kv_out, head],
        out_shape=[jax.ShapeDtypeStruct((s, BRANCH), BF16), jax.ShapeDtypeStruct((s, KV_A), F32),
                   jax.ShapeDtypeStruct((s, KV_A), F32), jax.ShapeDtypeStruct((N_PAIRS, 1, 2 * LANES), F32)],
        args=(qkv, qkv, qkv, slopes, sinks, o, lse, do), semantics=("arbitrary",), exchange=exchange)


def _layer_kind(i):
    return i % 3, i // 3


GATHER_FIRST = [("in", 0)]
GATHER_BEHIND = {("qkv", 0): [("out", 0)], ("attn", 0): [("in", 1)], ("attn", 1): [("out", 1), ("in", 2), ("out", 2)],
                 ("attn", 2): [("in", 3), ("out", 3)]}


def _forward_backward(x, target, g_pre, g_post, sinks_a, b_f_c, shards, chip, core):
    s = x.shape[0]
    slopes = _per_head_lanes(jnp.asarray(_alibi_slopes()))
    w_in, w_out, wf_t = {}, {}, {}

    def lands_side_by_side(key):
        return key[0] == "in" and shards[key].shape[1] % LANES == 0

    def gather(keys):
        return _GatherExchange([shards[k] for k in keys], [lands_side_by_side(k) for k in keys])

    def deliver(keys, gathered):
        for key, g in zip(keys, gathered):
            side, layer = key
            sh = shards[key]
            if side == "out":
                g = lax.dynamic_update_slice(g, sh[None], (chip, 0, 0))
                w_out[layer] = g.reshape(4 * sh.shape[0], sh.shape[1])
            elif lands_side_by_side(key):
                w_in[layer] = lax.dynamic_update_slice(g, sh, (0, chip * sh.shape[1]))
            else:
                g = lax.dynamic_update_slice(g, sh[None], (chip, 0, 0))
                w = g.transpose(1, 0, 2).reshape(sh.shape[0], 4 * sh.shape[1])
                w_in[layer], wf_t[layer] = w[:, :4 * BRANCH], w[:, 4 * BRANCH:].T

    deliver(GATHER_FIRST, _exchange_call(gather(GATHER_FIRST), "gather_first_weights"))
    saved = []
    for i in range(DEPTH):
        kind, j = _layer_kind(i)
        tag = f"l{i}"
        w = w_in[i]
        nqkv = A_QKV if kind == 0 else B_QKV
        tn = 512 if kind == 0 else 1024
        h, h_t = _rmsnorm_fwd(x, g_pre[i:i + 1], f"prenorm_{tag}")
        behind = GATHER_BEHIND.get(("qkv", i))
        qkv = _matmul(h, w, mode="nn", out_dtype=BF16, name=f"inproj_qkv_{tag}", n=nqkv, tn=tn,
                      exchange=gather(behind) if behind else None)
        if behind:
            qkv, arrived = qkv
            deliver(behind, arrived)
        z = _matmul(h, w, mode="nn", out_dtype=F32, name=f"inproj_gate_{tag}", n=BRANCH, b_off=nqkv // tn, tn=tn)
        behind = GATHER_BEHIND.get(("attn", i))
        exchange = gather(behind) if behind else None
        if kind == 0:
            sink_l = _per_head_lanes(sinks_a[j])
            (o, lse), arrived = _attn_a_fwd(qkv, slopes, sink_l, f"attn_a_fwd_{tag}", exchange)
            extra = (sink_l, lse)
        elif kind == 1:
            (o, extra), arrived = _attn_b_fwd(qkv, f"attn_b_fwd_{tag}", exchange)
        else:
            b_col = jnp.broadcast_to(b_f_c[j].astype(F32)[:, None], (N_HEADS, LANES))
            xf, cum = _fgate_fwd(h, wf_t[i], b_col, f"fgate_fwd_{tag}")
            cum4 = _to_cum4(cum, _fox_tile(s))
            (o, lse), arrived = _attn_c_fwd(qkv, cum4, f"attn_c_fwd_{tag}", exchange)
            extra = (xf, cum4, lse)
        if behind:
            deliver(behind, arrived)
        u, u_t = _gate_fwd(o, z, f"gate_{tag}")
        y = _matmul(u, w_out[i], mode="nn", out_dtype=F32, name=f"outproj_{tag}")
        saved.append((x, h, h_t, qkv, z, o, u_t, y, extra))
        x = _post_fwd(x, y, g_post[i:i + 1], f"postnorm_{tag}")

    dx, loss_part = _loss_and_grad(x, target)

    d_g_pre, d_g_post = [None] * DEPTH, [None] * DEPTH
    d_sinks = [None, None]
    d_b_f = None
    reduced = {}
    pending = None

    def finish_reduce(layer, side, own, arr):
        kind, j = _layer_kind(layer)
        reduced[(side, kind)] = _sum_chips(own, arr, core, f"shard_sum_{side}_l{layer}", j, 2 if kind == 0 else 1,
                                           into=reduced.get((side, kind)))

    for i in reversed(range(DEPTH)):
        kind, j = _layer_kind(i)
        tag = f"l{i}"
        x_in, h, h_t, qkv, z, o, u_t, y, extra = saved[i]
        tn = 512 if kind == 0 else 1024
        dy, d_g_post[i] = _post_bwd(dx, y, g_post[i:i + 1], f"postnorm_bwd_{tag}")
        dw_out = _matmul(u_t, dy, mode="nn", out_dtype=BF16, name=f"dw_out_{tag}")
        dw_out = dw_out.reshape(4, dw_out.shape[0] // 4, dw_out.shape[1])
        du, (their_out,) = _matmul(dy, w_out[i], mode="nt", out_dtype=F32, name=f"d_gated_{tag}",
                                   exchange=_SiblingExchange([dw_out]))
        sum_out = _add_pairs(dw_out, their_out, f"chip_sum_out_{tag}")
        do, dz = _gate_bwd(du, o, z, f"gate_bwd_{tag}")
        dhs = []
        exchange = _ScatterExchange([pending[1]]) if pending else None
        if kind == 0:
            sink_l, lse = extra
            (dq, dk, dv, dsk), arrived = _attn_a_bwd(qkv, slopes, sink_l, o, lse, do, f"attn_a_bwd_{tag}", exchange)
            d_sinks[j] = dsk[:, 0, ::LANES].reshape(N_HEADS)
            parts = [dq, dk.astype(BF16), dv.astype(BF16), dz]
        elif kind == 1:
            (dq, dk, dv), arrived = _attn_b_bwd(qkv, extra, do, f"attn_b_bwd_{tag}", exchange)
            parts = [dq, dk, dv, dz]
        else:
            xf, cum4, lse = extra
            (dq, dk, dv, dcum4), arrived = _attn_c_bwd(qkv, cum4, o, lse, do, f"attn_c_bwd_{tag}", exchange)
            d_wf_t, dh_f, db = _fgate_bwd(_from_cum4(dcum4), xf, h, wf_t[i], f"fgate_bwd_{tag}")
            d_b_f = db[:, 0]
            dhs.append(dh_f)
            parts = [dq, dk, dv, dz]
        if pending:
            finish_reduce(pending[0], "in", pending[1], arrived[0])
        dproj = jnp.concatenate(parts, axis=1)
        scatter_out = _ScatterExchange([sum_out])
        if kind == 2:
            dw_in, arrived = _matmul(h_t, dproj, mode="nn", out_dtype=F32, name=f"dw_in_{tag}", tn=tn, exchange=scatter_out)
            dw_in = jnp.concatenate([dw_in, d_wf_t.T], axis=1)
            dw_in = dw_in.reshape(dw_in.shape[0], 4, dw_in.shape[1] // 4).transpose(1, 0, 2).astype(BF16)
        else:
            dw_in, arrived = _matmul(h_t, dproj, mode="nn", out_dtype=BF16, name=f"dw_in_{tag}", col_blocks=4,
                                     tn=1152 if kind == 0 else 1024, exchange=scatter_out)
        finish_reduce(i, "out", sum_out, arrived[0])
        dh, (their_in,) = _matmul(dproj, w_in[i], mode="nt", out_dtype=F32, name=f"dh_{tag}",
                                  tk=1536 if kind == 0 else 2048, exchange=_SiblingExchange([dw_in]))
        dhs.insert(0, dh)
        dx, d_g_pre[i] = _pre_bwd(dx, dhs, x_in, g_pre[i:i + 1], f"prenorm_bwd_{tag}")
        pending = (i, _add_pairs(dw_in, their_in, f"chip_sum_in_{tag}"))

    arrived = _exchange_call(_ScatterExchange([pending[1]]), "grad_chip_scatter_last")
    finish_reduce(pending[0], "in", pending[1], arrived[0])

    return dict(loss=loss_part, dx=dx, g_pre=jnp.concatenate(d_g_pre, axis=0), g_post=jnp.concatenate(d_g_post, axis=0),
                sinks_a=jnp.stack(d_sinks), b_f_c=d_b_f[None, :], reduced=reduced)


def _place():
    x, y, c = lax.axis_index("x"), lax.axis_index("y"), lax.axis_index("c")
    others = [(1 - x, y), (x, 1 - y), (1 - x, 1 - y)]
    return x, y, c, others


def _half_rows(ref_rows, which):
    half = ref_rows // 2
    return pl.ds(pl.multiple_of(which * half, half), half)


def _remote(src, dst, sems, k, device):
    send, recv = sems
    return pltpu.make_async_remote_copy(src_ref=src, dst_ref=dst, send_sem=send.at[k], recv_sem=recv.at[k],
                                        device_id=device, device_id_type=MESH)


def _hbm_call(body, name, ins, out_shapes, n_remote, aliases=None):
    any_spec = pl.BlockSpec(memory_space=pl.ANY)
    return pl.pallas_call(
        body, name=name, in_specs=[any_spec] * len(ins), out_specs=[any_spec] * len(out_shapes),
        out_shape=out_shapes, input_output_aliases=aliases or {},
        scratch_shapes=[pltpu.SemaphoreType.DMA((n_remote,)), pltpu.SemaphoreType.DMA((n_remote,))],
    )(*ins)


class _GatherExchange:
    def __init__(self, shards, side_by_side):
        self.ins = list(shards)
        self.side_by_side = list(side_by_side)
        self.out_shapes = [jax.ShapeDtypeStruct((a.shape[0], 4 * a.shape[1]) if wide else (4,) + a.shape, a.dtype)
                           for a, wide in zip(shards, side_by_side)]
        self.n_sems = 6 * len(shards)
        self.aliases = {}

    def _copies(self, ins, outs, sems):
        x, y, c, others = _place()
        me = 2 * x + y
        table = []
        for w, (src, dst, wide) in enumerate(zip(ins, outs, self.side_by_side)):
            rows, cols = src.shape
            mine, theirs = _half_rows(rows, c), _half_rows(rows, 1 - c)

            def slot(chip, which, dst=dst, wide=wide, cols=cols):
                return dst.at[which, pl.ds(pl.multiple_of(chip * cols, LANES), cols)] if wide else dst.at[chip, which]

            for j, (px, py) in enumerate(others):
                there = 2 * px + py
                send = _remote(src.at[mine], slot(me, mine), sems, 6 * w + j, (px, py, c))
                landed = _remote(slot(there, mine), slot(there, mine), sems, 6 * w + j, (px, py, c))
                passed = _remote(slot(there, mine), slot(there, mine), sems, 6 * w + 3 + j, (x, y, 1 - c))
                from_sibling = _remote(slot(there, theirs), slot(there, theirs), sems, 6 * w + 3 + j, (x, y, 1 - c))
                table.append((send, landed, passed, from_sibling))
        return table

    def start(self, ins, outs, sems):
        for send, _, _, _ in self._copies(ins, outs, sems):
            send.start()

    def mid(self, ins, outs, sems):
        for _, landed, passed, _ in self._copies(ins, outs, sems):
            landed.wait_recv()
            passed.start()

    def finish(self, ins, outs, sems):
        table = self._copies(ins, outs, sems)
        for _, _, _, from_sibling in table:
            from_sibling.wait_recv()
        for send, _, passed, _ in table:
            send.wait_send()
            passed.wait_send()


def _exchange_call(ex, name):
    n_in, n_out = len(ex.ins), len(ex.out_shapes)

    def body(*refs):
        ins, outs, sems = refs[:n_in], refs[n_in:n_in + n_out], refs[n_in + n_out:]
        ex.start(ins, outs, sems)
        ex.mid(ins, outs, sems)
        ex.finish(ins, outs, sems)

    return _hbm_call(body, name, ex.ins, ex.out_shapes, ex.n_sems, aliases=ex.aliases)


def _grid_call(body, *, name, grid, in_specs, out_specs, out_shape, args, scratch_shapes=(), semantics, exchange=None):
    if exchange is None:
        res = pl.pallas_call(body, name=name, grid=grid, in_specs=list(in_specs), out_specs=list(out_specs),
                             out_shape=list(out_shape), scratch_shapes=list(scratch_shapes),
                             compiler_params=_params(semantics))(*args)
        return res, []
    n_in, n_out, n_scr = len(args), len(out_shape), len(scratch_shapes)
    x_in, x_out = len(exchange.ins), len(exchange.out_shapes)
    steps = math.prod(grid)

    def wrapped(*refs):
        core_in, ex_in = refs[:n_in], refs[n_in:n_in + x_in]
        rest = refs[n_in + x_in:]
        core_out, ex_out = rest[:n_out], rest[n_out:n_out + x_out]
        scratch, sems = rest[n_out + x_out:n_out + x_out + n_scr], rest[n_out + x_out + n_scr:]
        step = 0
        for axis, extent in enumerate(grid):
            step = step * extent + pl.program_id(axis)

        @pl.when(step == 0)
        def _():
            exchange.start(ex_in, ex_out, sems)

        body(*core_in, *core_out, *scratch)

        @pl.when(step == max((3 * steps) // 4 - 1, 0))
        def _():
            exchange.mid(ex_in, ex_out, sems)

        @pl.when(step == steps - 1)
        def _():
            exchange.finish(ex_in, ex_out, sems)

    any_spec = pl.BlockSpec(memory_space=pl.ANY)
    res = pl.pallas_call(
        wrapped, name=name, grid=grid,
        in_specs=list(in_specs) + [any_spec] * x_in, out_specs=list(out_specs) + [any_spec] * x_out,
        out_shape=list(out_shape) + list(exchange.out_shapes),
        input_output_aliases={n_in + a: n_out + b for a, b in exchange.aliases.items()},
        scratch_shapes=list(scratch_shapes) + [pltpu.SemaphoreType.DMA((exchange.n_sems,)),
                                               pltpu.SemaphoreType.DMA((exchange.n_sems,))],
        compiler_params=_params(("arbitrary",) * len(grid)),
    )(*args, *exchange.ins)
    return res[:n_out], res[n_out:]


class _SiblingExchange:
    def __init__(self, parts):
        self.ins = list(parts)
        self.out_shapes = [jax.ShapeDtypeStruct((4, a.shape[1] // 2, a.shape[2]), a.dtype) for a in parts]
        self.n_sems = len(parts)
        self.aliases = {}

    def _copies(self, ins, outs, sems):
        x, y, c, _ = _place()
        return [_remote(src.at[:, _half_rows(src.shape[1], 1 - c)], dst, sems, w, (x, y, 1 - c))
                for w, (src, dst) in enumerate(zip(ins, outs))]

    def start(self, ins, outs, sems):
        for cp in self._copies(ins, outs, sems):
            cp.start()

    def mid(self, ins, outs, sems):
        pass

    def finish(self, ins, outs, sems):
        for cp in self._copies(ins, outs, sems):
            cp.wait_recv()
            cp.wait_send()


class _ScatterExchange:
    def __init__(self, sums):
        self.ins = list(sums)
        self.out_shapes = [jax.ShapeDtypeStruct(a.shape, a.dtype) for a in sums]
        self.n_sems = 3 * len(sums)
        self.aliases = {}

    def _copies(self, ins, outs, sems):
        x, y, c, others = _place()
        me = 2 * x + y
        table = []
        for w, (src, dst) in enumerate(zip(ins, outs)):
            for j, (px, py) in enumerate(others):
                there = 2 * px + py
                send = _remote(src.at[there], dst.at[me], sems, 3 * w + j, (px, py, c))
                landed = _remote(dst.at[there], dst.at[there], sems, 3 * w + j, (px, py, c))
                table.append((send, landed))
        return table

    def start(self, ins, outs, sems):
        for send, _ in self._copies(ins, outs, sems):
            send.start()

    def mid(self, ins, outs, sems):
        pass

    def finish(self, ins, outs, sems):
        table = self._copies(ins, outs, sems)
        for _, landed in table:
            landed.wait_recv()
        for send, _ in table:
            send.wait_send()


def _sibling_join(shards):
    n = len(shards)

    def body(*refs):
        ins, outs, sems = refs[:n], refs[n:2 * n], refs[2 * n:2 * n + 2]
        x, y, c, _ = _place()
        pend = []
        for w in range(n):
            rows = ins[w].shape[1]
            mine, theirs = _half_rows(rows, c), _half_rows(rows, 1 - c)
            cp = _remote(ins[w].at[:, mine], outs[w].at[:, mine], sems, w, (x, y, 1 - c))
            cp.start()
            pend.append((cp, _remote(ins[w].at[:, theirs], outs[w].at[:, theirs], sems, w, (x, y, 1 - c))))
        for cp, landed in pend:
            landed.wait_recv()
            cp.wait_send()

    out_shapes = [jax.ShapeDtypeStruct(a.shape, a.dtype) for a in shards]
    return _hbm_call(body, "grad_sibling_join", shards, out_shapes, n, aliases={w: w for w in range(n)})


SMALL_ROWS = 136


def _all_reduce_small(vec):
    def body(v_ref, o_ref, buf, send, recv, loc):
        x, y, c, _ = _place()
        me = 4 * x + 2 * y + c
        lc = pltpu.make_async_copy(v_ref, buf.at[me], loc.at[0])
        lc.start()
        cps = []
        for k in range(1, 8):
            fx, fy, fc = (k >> 2) & 1, (k >> 1) & 1, k & 1
            peer = (x ^ fx, y ^ fy, c ^ fc)
            cp = pltpu.make_async_remote_copy(src_ref=v_ref, dst_ref=buf.at[me], send_sem=send.at[k - 1],
                                              recv_sem=recv.at[k - 1], device_id=peer, device_id_type=MESH)
            cp.start()
            cps.append((cp, 4 * peer[0] + 2 * peer[1] + peer[2]))
        for k, (cp, src) in enumerate(cps):
            pltpu.make_async_remote_copy(src_ref=v_ref, dst_ref=buf.at[src], send_sem=send.at[k], recv_sem=recv.at[k],
                                         device_id=(x, y, c), device_id_type=MESH).wait_recv()
        for cp, _ in cps:
            cp.wait_send()
        lc.wait()
        total = buf[0]
        for k in range(1, 8):
            total = total + buf[k]
        o_ref[...] = total

    vm = pl.BlockSpec(memory_space=pltpu.VMEM)
    return pl.pallas_call(
        body, name="all_reduce_small", in_specs=[vm], out_specs=vm,
        out_shape=jax.ShapeDtypeStruct(vec.shape, F32),
        scratch_shapes=[pltpu.VMEM((8,) + vec.shape, F32), pltpu.SemaphoreType.DMA((7,)),
                        pltpu.SemaphoreType.DMA((7,)), pltpu.SemaphoreType.DMA((1,))],
    )(vec)


SUM_ROWS = 256


def _add_pairs(part, theirs, name):
    four, rh, cc = theirs.shape
    tr = min(SUM_ROWS, rh)
    halves = part.reshape(four, 2, rh, cc)

    def body(a_ref, b_ref, o_ref):
        mine = a_ref[0, lax.axis_index("c")]
        o_ref[0] = (mine.astype(F32) + b_ref[0].astype(F32)).astype(o_ref.dtype)

    spec = pl.BlockSpec((1, tr, cc), lambda k, r: (k, r, 0))
    return pl.pallas_call(
        body, name=name, grid=(four, rh // tr),
        in_specs=[pl.BlockSpec((1, 2, tr, cc), lambda k, r: (k, 0, r, 0)), spec], out_specs=spec,
        out_shape=jax.ShapeDtypeStruct(theirs.shape, theirs.dtype),
        compiler_params=_params(("parallel", "parallel")),
    )(halves, theirs)


def _sum_chips(own, arrived, core, name, layer, n_layers, into=None):
    four, rh, cc = own.shape
    tr = min(SUM_ROWS, rh)
    nr = rh // tr

    def body(c_ref, own_ref, arr_ref, *rest):
        o_ref = rest[-1]
        x, y = lax.axis_index("x"), lax.axis_index("y")
        tot = own_ref[2 * x + y].astype(F32)
        for px, py in ((1 - x, y), (x, 1 - y), (1 - x, 1 - y)):
            tot = tot + arr_ref[2 * px + py].astype(F32)
        o_ref[0] = tot

    blk = pl.BlockSpec((4, tr, cc), lambda r, c_ref: (0, r, 0))
    in_specs, args, aliases = [blk, blk], [core, own, arrived], {}
    if into is not None:
        in_specs.append(pl.BlockSpec(memory_space=pl.ANY))
        args.append(into)
        aliases = {3: 0}
    return pl.pallas_call(
        body, name=name,
        grid_spec=pltpu.PrefetchScalarGridSpec(
            num_scalar_prefetch=1, grid=(nr,), in_specs=in_specs,
            out_specs=pl.BlockSpec((1, tr, cc), lambda r, c_ref: (layer, c_ref[0] * nr + r, 0))),
        out_shape=jax.ShapeDtypeStruct((n_layers, 2 * rh, cc), F32), input_output_aliases=aliases,
        compiler_params=_params(("parallel",)),
    )(*args)


ADAM_ROWS = 256


def _adamw(w, g, m, v, name):
    shape = w.shape
    cc = shape[-1]
    flat = lambda a: a.reshape(-1, cc)
    rows = flat(w).shape[0]
    tr = min(ADAM_ROWS, rows)
    assert rows % tr == 0
    c1 = 1.0 - ADAM_B1 ** ADAM_STEP
    c2 = 1.0 - ADAM_B2 ** ADAM_STEP

    def body(w_ref, g_ref, m_ref, v_ref, d_ref, nm_ref, nv_ref):
        gv = g_ref[...]
        nm = ADAM_B1 * m_ref[...] + (1.0 - ADAM_B1) * gv
        nv = ADAM_B2 * v_ref[...] + (1.0 - ADAM_B2) * (gv * gv)
        nm_ref[...] = nm
        nv_ref[...] = nv
        d_ref[...] = -ADAM_LR * ((nm / c1) / (jnp.sqrt(nv / c2) + ADAM_EPS) + ADAM_WD * w_ref[...])

    spec = pl.BlockSpec((tr, cc), lambda i: (i, 0))
    sh = jax.ShapeDtypeStruct((rows, cc), F32)
    outs = pl.pallas_call(
        body, name=name, grid=(rows // tr,), in_specs=[spec] * 4, out_specs=[spec] * 3, out_shape=[sh] * 3,
        compiler_params=_params(("parallel",)),
    )(flat(w), flat(g), flat(m), flat(v))
    return [o.reshape(shape) for o in outs]


def _pack_small(g_pre, g_post, sinks_a, b_f_c, loss_row):
    pad = lambda a: jnp.pad(a.reshape(1, -1).astype(F32), ((0, 0), (0, LANES - a.size)))
    rows = [g_pre.astype(F32).reshape(-1, LANES), g_post.astype(F32).reshape(-1, LANES), pad(sinks_a), pad(b_f_c), loss_row]
    packed = jnp.concatenate(rows, axis=0)
    return jnp.pad(packed, ((0, SMALL_ROWS - packed.shape[0]), (0, 0)))


def _unpack_small(p):
    n = DEPTH * D_MODEL // LANES
    return (p[:n].reshape(DEPTH, D_MODEL), p[n:2 * n].reshape(DEPTH, D_MODEL), p[2 * n, :2 * N_HEADS].reshape(2, N_HEADS),
            p[2 * n + 1, :N_HEADS].reshape(1, N_HEADS), p[2 * n + 2, 0])


def kernel(x, g_pre, g_post, w_in_a, w_out_a, sinks_a, w_in_b, w_out_b, w_in_c, b_f_c, w_out_c, loss_target, m_g_pre, m_g_post, m_w_in_a, m_w_out_a, m_sinks_a, m_w_in_b, m_w_out_b, m_w_in_c, m_b_f_c, m_w_out_c, v_g_pre, v_g_post, v_w_in_a, v_w_out_a, v_sinks_a, v_w_in_b, v_w_out_b, v_w_in_c, v_b_f_c, v_w_out_c):
    big_w = [w_in_a, w_out_a, w_in_b, w_out_b, w_in_c, w_out_c]
    big_m = [m_w_in_a, m_w_out_a, m_w_in_b, m_w_out_b, m_w_in_c, m_w_out_c]
    big_v = [v_w_in_a, v_w_out_a, v_w_in_b, v_w_out_b, v_w_in_c, v_w_out_c]

    chip = 2 * lax.axis_index("x") + lax.axis_index("y")
    core = lax.axis_index("c").astype(jnp.int32).reshape(1)
    by_kind = {0: (w_in_a, w_out_a), 1: (w_in_b, w_out_b), 2: (w_in_c, w_out_c)}
    shards = {}
    for i in range(DEPTH):
        kind, j = _layer_kind(i)
        shards[("in", i)] = by_kind[kind][0][j].astype(BF16)
        shards[("out", i)] = by_kind[kind][1][j].astype(BF16)

    res = _forward_backward(x[0], loss_target[0], g_pre, g_post, sinks_a, b_f_c, shards, chip, core)
    names = ["w_in_a", "w_out_a", "w_in_b", "w_out_b", "w_in_c", "w_out_c"]
    grads = _sibling_join([res["reduced"][(side, kind)] for kind in range(3) for side in ("in", "out")])

    small = _unpack_small(_all_reduce_small(
        _pack_small(res["g_pre"], res["g_post"], res["sinks_a"], res["b_f_c"], res["loss"])))
    g_small, loss = small[:4], small[4]

    zero_row = jnp.zeros((1, LANES), F32)
    pk = lambda a: _pack_small(a[0], a[1], a[2], a[3], zero_row)
    sm = _adamw(pk([g_pre, g_post, sinks_a, b_f_c]), pk(g_small), pk([m_g_pre, m_g_post, m_sinks_a, m_b_f_c]),
                pk([v_g_pre, v_g_post, v_sinks_a, v_b_f_c]), "adamw_small")
    sm = [_unpack_small(a)[:4] for a in sm]
    bigs = [_adamw(w, g, m, v, f"adamw_{nm}") for w, g, m, v, nm in zip(big_w, grads, big_m, big_v, names)]

    def ordered(small4, big6):
        return [small4[0], small4[1], big6[0], big6[1], small4[2], big6[2], big6[3], big6[4], small4[3], big6[5]]

    out = [loss, res["dx"][None], *ordered(g_small, grads)]
    for k in range(3):
        out += ordered(sm[k], [b[k] for b in bigs])
    return tuple(out)
```

```python
import functools
import math

import numpy as np
import jax
import jax.numpy as jnp
from jax import lax
from jax.experimental import pallas as pl
from jax.experimental.pallas import tpu as pltpu

F32 = jnp.float32
BF16 = jnp.bfloat16

D_MODEL = 2048
DEPTH = 4
N_HEADS = 32
HEAD_DIM = 64
LANES = 128
N_PAIRS = N_HEADS * HEAD_DIM // LANES
BRANCH = N_HEADS * HEAD_DIM
N_KV_A = 4
KV_A = N_KV_A * HEAD_DIM
WINDOW = 128
NORM_EPS = 1e-6
NEG = -1e30
Q_SCALE = HEAD_DIM ** -0.5

A_QKV = BRANCH + 2 * KV_A
B_QKV = 3 * BRANCH

ADAM_LR = 0.001
ADAM_B1 = 0.9
ADAM_B2 = 0.999
ADAM_EPS = 1e-08
ADAM_WD = 0.01
ADAM_STEP = 10

MESH = pl.DeviceIdType.MESH

_NT = (((1,), (1,)), ((), ()))
_TN = (((0,), (0,)), ((), ()))


def _params(sem=None):
    return pltpu.CompilerParams(dimension_semantics=sem)


def _matmul(a, b, *, mode, out_dtype, name, n=None, b_off=0, tm=1024, tn=1024, tk=2048, col_blocks=None, exchange=None):
    (m, k), nn = a.shape, ((n or b.shape[1]) if mode == "nn" else b.shape[0])
    tm, tn, tk = min(tm, m), min(tn, nn), min(tk, k)
    assert m % tm == 0 and nn % tn == 0 and k % tk == 0, (name, m, nn, k, tm, tn, tk)
    nk = k // tk

    def body(a_ref, b_ref, o_ref, acc_ref):
        kk = pl.program_id(2)
        if mode == "nn":
            p = jnp.dot(a_ref[...], b_ref[...], preferred_element_type=F32)
        else:
            p = lax.dot_general(a_ref[...], b_ref[...], _NT, preferred_element_type=F32)
        if nk == 1:
            o_ref[...] = p.astype(o_ref.dtype).reshape(o_ref.shape)
        else:
            @pl.when(kk == 0)
            def _():
                acc_ref[...] = p

            @pl.when(kk > 0)
            def _():
                acc_ref[...] += p

            @pl.when(kk == nk - 1)
            def _():
                o_ref[...] = acc_ref[...].astype(o_ref.dtype).reshape(o_ref.shape)

    if mode == "nn":
        in_specs = [pl.BlockSpec((tm, tk), lambda i, j, kk: (i, kk)),
                    pl.BlockSpec((tk, tn), lambda i, j, kk: (kk, j + b_off))]
    else:
        in_specs = [pl.BlockSpec((tm, tk), lambda i, j, kk: (i, kk)),
                    pl.BlockSpec((tn, tk), lambda i, j, kk: (j, kk))]
    if col_blocks is None:
        out_spec = pl.BlockSpec((tm, tn), lambda i, j, kk: (i, j))
        out_shape = jax.ShapeDtypeStruct((m, nn), out_dtype)
    else:
        per = nn // col_blocks // tn
        assert per * tn * col_blocks == nn, (name, nn, tn, col_blocks)
        out_spec = pl.BlockSpec((1, tm, tn), lambda i, j, kk: (j // per, i, j % per))
        out_shape = jax.ShapeDtypeStruct((col_blocks, m, nn // col_blocks), out_dtype)
    (res,), arrived = _grid_call(
        body, name=name, grid=(m // tm, nn // tn, nk), in_specs=in_specs, out_specs=[out_spec], out_shape=[out_shape],
        args=(a, b), scratch_shapes=[pltpu.VMEM((tm, tn), F32)], semantics=("parallel", "parallel", "arbitrary"),
        exchange=exchange)
    return res if exchange is None else (res, arrived)


ROW_TILE = 256


def _row_call(body, name, ins, outs, *, s):
    tr = min(ROW_TILE, s)
    spec = {"row": lambda sh: pl.BlockSpec((tr, sh[1]), lambda i: (i, 0)),
            "vec": lambda sh: pl.BlockSpec((1, sh[1]), lambda i: (0, 0)),
            "col": lambda sh: pl.BlockSpec((sh[0], tr), lambda i: (0, i))}
    in_specs = [spec[kind](a.shape) for a, kind in ins]
    out_specs = [spec[kind](sh.shape) for sh, kind in outs]
    return pl.pallas_call(
        body, name=name, grid=(s // tr,), in_specs=in_specs, out_specs=out_specs,
        out_shape=[sh for sh, _ in outs],
        compiler_params=_params(("arbitrary",)),
    )(*[a for a, _ in ins])


def _rsqrt_ms(v):
    return lax.rsqrt(jnp.mean(v * v, axis=-1, keepdims=True) + NORM_EPS)


def _rmsnorm_fwd(x, g, name):
    s, d = x.shape

    def body(x_ref, g_ref, h_ref, ht_ref):
        xv = x_ref[...]
        h = xv * _rsqrt_ms(xv) * g_ref[...]
        h_ref[...] = h.astype(BF16)
        ht_ref[...] = h.T.astype(BF16)

    return _row_call(body, name, [(x, "row"), (g, "vec")],
                     [(jax.ShapeDtypeStruct((s, d), BF16), "row"), (jax.ShapeDtypeStruct((d, s), BF16), "col")], s=s)


def _gate_fwd(o, z, name):
    s, d = o.shape

    def body(o_ref, z_ref, u_ref, ut_ref):
        zv = z_ref[...]
        u = o_ref[...] * (zv * jax.nn.sigmoid(zv))
        u_ref[...] = u.astype(BF16)
        ut_ref[...] = u.T.astype(BF16)

    return _row_call(body, name, [(o, "row"), (z, "row")],
                     [(jax.ShapeDtypeStruct((s, d), BF16), "row"), (jax.ShapeDtypeStruct((d, s), BF16), "col")], s=s)


def _post_fwd(x, y, g, name):
    s, d = x.shape

    def body(x_ref, y_ref, g_ref, o_ref):
        yv = y_ref[...]
        o_ref[...] = x_ref[...] + yv * _rsqrt_ms(yv) * g_ref[...]

    return _row_call(body, name, [(x, "row"), (y, "row"), (g, "vec")],
                     [(jax.ShapeDtypeStruct((s, d), F32), "row")], s=s)[0]


def _loss_and_grad(x, target):
    s, d = x.shape

    def body(x_ref, t_ref, dx_ref, l_ref):
        err = x_ref[...] - t_ref[...]
        dx_ref[...] = err * (1.0 / d)
        part = jnp.sum(jnp.sum(err * err, axis=1, keepdims=True), axis=0, keepdims=True) * (0.5 / d)

        @pl.when(pl.program_id(0) == 0)
        def _():
            l_ref[...] = jnp.zeros_like(l_ref)

        l_ref[...] += jnp.broadcast_to(part, l_ref.shape)

    return _row_call(body, "loss_head", [(x, "row"), (target, "row")],
                     [(jax.ShapeDtypeStruct((s, d), F32), "row"),
                      (jax.ShapeDtypeStruct((1, LANES), F32), "vec")], s=s)


def _norm_bwd_rows(dn, v, g):
    r = _rsqrt_ms(v)
    a = dn * g
    dv = r * (a - v * (r * r) * jnp.mean(a * v, axis=-1, keepdims=True))
    return dv, dn * v * r


def _post_bwd(dx, y, g, name):
    s, d = dx.shape

    def body(dx_ref, y_ref, g_ref, dy_ref, dg_ref):
        dy, dg = _norm_bwd_rows(dx_ref[...], y_ref[...], g_ref[...])
        dy_ref[...] = dy.astype(BF16)

        @pl.when(pl.program_id(0) == 0)
        def _():
            dg_ref[...] = jnp.zeros_like(dg_ref)

        dg_ref[...] += jnp.sum(dg, axis=0, keepdims=True)

    return _row_call(body, name, [(dx, "row"), (y, "row"), (g, "vec")],
                     [(jax.ShapeDtypeStruct((s, d), BF16), "row"),
                      (jax.ShapeDtypeStruct((1, d), F32), "vec")], s=s)


def _gate_bwd(du, o, z, name):
    s, d = du.shape

    def body(du_ref, o_ref, z_ref, do_ref, dz_ref):
        duv, zv = du_ref[...], z_ref[...]
        sig = jax.nn.sigmoid(zv)
        do_ref[...] = (duv * (zv * sig)).astype(BF16)
        dz_ref[...] = (duv * o_ref[...] * (sig * (1.0 + zv * (1.0 - sig)))).astype(BF16)

    return _row_call(body, name, [(du, "row"), (o, "row"), (z, "row")],
                     [(jax.ShapeDtypeStruct((s, d), BF16), "row"),
                      (jax.ShapeDtypeStruct((s, d), BF16), "row")], s=s)


def _pre_bwd(dx, dhs, x, g, name):
    s, d = dx.shape
    n_dh = len(dhs)

    def body(*refs):
        dx_ref, dh_refs, (x_ref, g_ref, o_ref, dg_ref) = refs[0], refs[1:1 + n_dh], refs[1 + n_dh:]
        dh = dh_refs[0][...].astype(F32)
        for r in dh_refs[1:]:
            dh = dh + r[...].astype(F32)
        dv, dg = _norm_bwd_rows(dh, x_ref[...], g_ref[...])
        o_ref[...] = dx_ref[...] + dv

        @pl.when(pl.program_id(0) == 0)
        def _():
            dg_ref[...] = jnp.zeros_like(dg_ref)

        dg_ref[...] += jnp.sum(dg, axis=0, keepdims=True)

    return _row_call(body, name, [(dx, "row")] + [(h, "row") for h in dhs] + [(x, "row"), (g, "vec")],
                     [(jax.ShapeDtypeStruct((s, d), F32), "row"),
                      (jax.ShapeDtypeStruct((1, d), F32), "vec")], s=s)


def _lane_is_first_head():
    return lax.broadcasted_iota(jnp.int32, (1, LANES), 1) < HEAD_DIM


def _bcast_lanes(col):
    return jnp.broadcast_to(col, (col.shape[0], LANES))


def _pair_spec(s, off=0, width=LANES):
    return pl.BlockSpec((s, width), lambda p: (0, p + off))


def _stack_heads(pair, first):
    return jnp.concatenate([jnp.where(first, pair, 0), jnp.where(first, 0, pair)], axis=0).astype(BF16)


def _stacked_mask(t, strict):
    row = lax.broadcasted_iota(jnp.int32, (2 * t, t), 0)
    col = lax.broadcasted_iota(jnp.int32, (2 * t, t), 1)
    query = jnp.where(row >= t, row - t, row)
    return col < query if strict else col <= query


LOOP_UNROLL = 2


def _two_at_a_time(n, step, carry):
    def group(jj, c):
        for k in range(LOOP_UNROLL):
            c = step(LOOP_UNROLL * jj + k, c)
        return c

    carry = lax.fori_loop(0, n // LOOP_UNROLL, group, carry)
    return lax.fori_loop(LOOP_UNROLL * (n // LOOP_UNROLL), n, step, carry)


def _rowsum_heads(prod, first):
    return (jnp.sum(jnp.where(first, prod, 0.0), axis=1, keepdims=True),
            jnp.sum(jnp.where(first, 0.0, prod), axis=1, keepdims=True))


def _softplus_parts(z):
    e = jnp.exp(-jnp.abs(z))
    sp = jnp.maximum(z, 0.0) + jnp.log(1.0 + e)
    r = 1.0 / (1.0 + e)
    return sp, jnp.where(z >= 0, r, e * r)


def _split_dot(x, t):
    hi = x.astype(BF16)
    lo = (x - hi.astype(F32)).astype(BF16)
    return jnp.dot(hi, t, preferred_element_type=F32) + jnp.dot(lo, t, preferred_element_type=F32)


def _sb_tile(s):
    return min(256, s)


def _attn_b_fwd(qkv, name, exchange=None):
    s = qkv.shape[0]
    t = _sb_tile(s)
    nq = s // t

    def body(q_ref, k_ref, v_ref, o_ref, lt_ref):
        first = _lane_is_first_head()
        before = _stacked_mask(t, strict=True)
        tri = (lax.broadcasted_iota(jnp.int32, (t, t), 0) >= lax.broadcasted_iota(jnp.int32, (t, t), 1)).astype(BF16)

        def tile(j, carry, diag, qs):
            c, acc = carry
            c0 = pl.multiple_of(j * t, t)
            k2 = k_ref[pl.ds(c0, t), :]
            v2 = v_ref[pl.ds(c0, t), :]
            z = lax.dot_general(qs, k2, _NT, preferred_element_type=F32)
            sp, _ = _softplus_parts(z)
            lf = jnp.where(before, -sp, 0.0) if diag else -sp
            incl = jnp.dot(lf.astype(BF16), tri, preferred_element_type=F32)
            a = jnp.exp(z + c + incl)
            if diag:
                a = jnp.where(before, a, 0.0)
            pv = jnp.dot(a.astype(BF16), v2, preferred_element_type=F32)
            return c + incl[:, 0:1], acc + jnp.where(first, pv[:t], pv[t:])

        def qblock(i, _):
            r0 = pl.multiple_of(i * t, t)
            qs = _stack_heads(q_ref[pl.ds(r0, t), :] * Q_SCALE, first)
            carry = tile(i, (jnp.zeros((2 * t, 1), F32), jnp.zeros((t, LANES), F32)), True, qs)
            carry = _two_at_a_time(i, lambda j, c: tile(i - 1 - j, c, False, qs), carry)
            o_ref[pl.ds(r0, t), :] = carry[1]
            lt_ref[pl.ds(r0, t), 0:LANES] = _bcast_lanes(carry[0][:t])
            lt_ref[pl.ds(r0, t), LANES:2 * LANES] = _bcast_lanes(carry[0][t:])
            return 0

        lax.fori_loop(0, nq, qblock, 0)

    return _grid_call(
        body, name=name, grid=(N_PAIRS,),
        in_specs=[_pair_spec(s), _pair_spec(s, N_PAIRS), _pair_spec(s, 2 * N_PAIRS)],
        out_specs=[_pair_spec(s), _stat_spec(s)],
        out_shape=[jax.ShapeDtypeStruct((s, BRANCH), F32), jax.ShapeDtypeStruct((s, N_HEADS * LANES), F32)],
        args=(qkv, qkv, qkv), semantics=("parallel",), exchange=exchange)


def _attn_b_bwd(qkv, ltot, do, name, exchange=None):
    s = qkv.shape[0]
    t = _sb_tile(s)
    nq = s // t

    def body(q_ref, k_ref, v_ref, lt_ref, do_ref, dq_ref, dk_ref, dv_ref, dk_acc, dv_acc):
        first = _lane_is_first_head()
        before = _stacked_mask(t, strict=True)
        tri = (lax.broadcasted_iota(jnp.int32, (t, t), 0) <= lax.broadcasted_iota(jnp.int32, (t, t), 1)).astype(BF16)
        dk_acc[...] = jnp.zeros_like(dk_acc)
        dv_acc[...] = jnp.zeros_like(dv_acc)

        def tile(j, carry, diag, qs, dos, lt):
            p_l, p_g, dq_acc = carry
            c0 = pl.multiple_of(j * t, t)
            k2 = k_ref[pl.ds(c0, t), :]
            v2 = v_ref[pl.ds(c0, t), :]
            z = lax.dot_general(qs, k2, _NT, preferred_element_type=F32)
            sp, sig = _softplus_parts(z)
            lf = jnp.where(before, -sp, 0.0) if diag else -sp
            pref_l = jnp.dot(lf.astype(BF16), tri, preferred_element_type=F32)
            a = jnp.exp(z + ((lt - p_l) - pref_l + lf))
            if diag:
                a = jnp.where(before, a, 0.0)
            g = a * lax.dot_general(dos, v2, _NT, preferred_element_type=F32)
            pref_g = jnp.dot(g.astype(BF16), tri, preferred_element_type=F32)
            dz = g - sig * (p_g + pref_g)
            if diag:
                dz = jnp.where(before, dz, 0.0)
            dzb = dz.astype(BF16)
            dq = jnp.dot(dzb, k2, preferred_element_type=F32)
            dk_acc[pl.ds(c0, t), :] += lax.dot_general(dzb, qs, _TN, preferred_element_type=F32)
            dv_acc[pl.ds(c0, t), :] += lax.dot_general(a.astype(BF16), dos, _TN, preferred_element_type=F32)
            return p_l + pref_l[:, t - 1:t], p_g + pref_g[:, t - 1:t], dq_acc + jnp.where(first, dq[:t], dq[t:])

        def qblock(i, _):
            r0 = pl.multiple_of(i * t, t)
            qs = _stack_heads(q_ref[pl.ds(r0, t), :] * Q_SCALE, first)
            dos = _stack_heads(do_ref[pl.ds(r0, t), :], first)
            lt = jnp.concatenate([lt_ref[pl.ds(r0, t), 0:1], lt_ref[pl.ds(r0, t), LANES:LANES + 1]], axis=0)
            zero = jnp.zeros((2 * t, 1), F32)
            carry = (zero, zero, jnp.zeros((t, LANES), F32))
            carry = _two_at_a_time(i, lambda j, c: tile(j, c, False, qs, dos, lt), carry)
            carry = tile(i, carry, True, qs, dos, lt)
            dq_ref[pl.ds(r0, t), :] = (carry[2] * Q_SCALE).astype(BF16)
            return 0

        lax.fori_loop(0, nq, qblock, 0)
        dk_ref[...] = dk_acc[...].astype(BF16)
        dv_ref[...] = dv_acc[...].astype(BF16)

    out = jax.ShapeDtypeStruct((s, BRANCH), BF16)
    return _grid_call(
        body, name=name, grid=(N_PAIRS,),
        in_specs=[_pair_spec(s), _pair_spec(s, N_PAIRS), _pair_spec(s, 2 * N_PAIRS), _stat_spec(s), _pair_spec(s)],
        out_specs=[_pair_spec(s)] * 3, out_shape=[out] * 3,
        scratch_shapes=[pltpu.VMEM((s, LANES), F32), pltpu.VMEM((s, LANES), F32)],
        args=(qkv, qkv, qkv, ltot, do), semantics=("parallel",), exchange=exchange)


def _fox_tile(s):
    return min(256, s)


def _stat_spec(s):
    return pl.BlockSpec((s, 2 * LANES), lambda p: (0, p))


def _cum_spec(nt, t):
    return pl.BlockSpec((1, nt, 2, t), lambda p: (p, 0, 0, 0))


def _attn_c_fwd(qkv, cum4, name, exchange=None):
    s = qkv.shape[0]
    t = _fox_tile(s)
    nq = s // t

    def body(q_ref, k_ref, v_ref, c_ref, o_ref, lse_ref):
        first = _lane_is_first_head()
        causal = _stacked_mask(t, strict=False)

        def tile(j, carry, diag, qs):
            c0 = pl.multiple_of(j * t, t)
            k2 = k_ref[pl.ds(c0, t), :]
            v2 = v_ref[pl.ds(c0, t), :]
            cs = c_ref[0, j]
            m_prev, l_prev, acc = carry
            z = lax.dot_general(qs, k2, _NT, preferred_element_type=F32)
            sc = jnp.concatenate([z[:t] - cs[0:1, :], z[t:] - cs[1:2, :]], axis=0)
            if diag:
                sc = jnp.where(causal, sc, NEG)
            m_new = jnp.maximum(m_prev, jnp.max(sc, axis=1, keepdims=True))
            alpha = jnp.exp(m_prev - m_new)
            p = jnp.exp(sc - m_new)
            l_new = alpha * l_prev + jnp.sum(p, axis=1, keepdims=True)
            pv = jnp.dot(p.astype(BF16), v2, preferred_element_type=F32)
            acc = jnp.where(first, acc * alpha[:t] + pv[:t], acc * alpha[t:] + pv[t:])
            return m_new, l_new, acc

        def qblock(i, _):
            r0 = pl.multiple_of(i * t, t)
            qs = _stack_heads(q_ref[pl.ds(r0, t), :] * Q_SCALE, first)
            carry = (jnp.full((2 * t, 1), NEG, F32), jnp.zeros((2 * t, 1), F32), jnp.zeros((t, LANES), F32))
            carry = _two_at_a_time(i, lambda j, c: tile(j, c, False, qs), carry)
            m, l, acc = tile(i, carry, True, qs)
            inv = 1.0 / l
            lse = m + jnp.log(l)
            o_ref[pl.ds(r0, t), :] = acc * jnp.where(first, inv[:t], inv[t:])
            lse_ref[pl.ds(r0, t), 0:LANES] = _bcast_lanes(lse[:t])
            lse_ref[pl.ds(r0, t), LANES:2 * LANES] = _bcast_lanes(lse[t:])
            return 0

        lax.fori_loop(0, nq, qblock, 0)

    return _grid_call(
        body, name=name, grid=(N_PAIRS,),
        in_specs=[_pair_spec(s), _pair_spec(s, N_PAIRS), _pair_spec(s, 2 * N_PAIRS), _cum_spec(nq, t)],
        out_specs=[_pair_spec(s), _stat_spec(s)],
        out_shape=[jax.ShapeDtypeStruct((s, BRANCH), F32), jax.ShapeDtypeStruct((s, N_HEADS * LANES), F32)],
        args=(qkv, qkv, qkv, cum4), semantics=("parallel",), exchange=exchange)


def _attn_c_bwd(qkv, cum4, o, lse, do, name, exchange=None):
    s = qkv.shape[0]
    t = _fox_tile(s)
    nq = s // t

    def body(q_ref, k_ref, v_ref, c_ref, o_ref, lse_ref, do_ref, dq_ref, dk_ref, dv_ref, dc_ref, dk_acc, dv_acc):
        first = _lane_is_first_head()
        causal = _stacked_mask(t, strict=False)
        eye = lax.broadcasted_iota(jnp.int32, (t, t), 0) == lax.broadcasted_iota(jnp.int32, (t, t), 1)
        dk_acc[...] = jnp.zeros_like(dk_acc)
        dv_acc[...] = jnp.zeros_like(dv_acc)
        dc_ref[...] = jnp.zeros_like(dc_ref)

        def tile(j, carry, diag, qs, dos, delta, lse):
            dq_acc, rs = carry
            c0 = pl.multiple_of(j * t, t)
            k2 = k_ref[pl.ds(c0, t), :]
            v2 = v_ref[pl.ds(c0, t), :]
            cs = c_ref[0, j]
            z = lax.dot_general(qs, k2, _NT, preferred_element_type=F32)
            sc = jnp.concatenate([z[:t] - cs[0:1, :], z[t:] - cs[1:2, :]], axis=0)
            p = jnp.exp(sc - lse)
            if diag:
                p = jnp.where(causal, p, 0.0)
            ds = p * (lax.dot_general(dos, v2, _NT, preferred_element_type=F32) - delta)
            dsb = ds.astype(BF16)
            dq = jnp.dot(dsb, k2, preferred_element_type=F32)
            dk_acc[pl.ds(c0, t), :] += lax.dot_general(dsb, qs, _TN, preferred_element_type=F32)
            dv_acc[pl.ds(c0, t), :] += lax.dot_general(p.astype(BF16), dos, _TN, preferred_element_type=F32)
            col_sums = jnp.concatenate([jnp.sum(ds[:t], axis=0, keepdims=True), jnp.sum(ds[t:], axis=0, keepdims=True)], axis=0)
            dc_ref[0, j] = dc_ref[0, j] - col_sums
            return dq_acc + jnp.where(first, dq[:t], dq[t:]), rs + jnp.sum(ds, axis=1, keepdims=True)

        def qblock(i, _):
            r0 = pl.multiple_of(i * t, t)
            do2 = do_ref[pl.ds(r0, t), :]
            qs = _stack_heads(q_ref[pl.ds(r0, t), :] * Q_SCALE, first)
            dos = _stack_heads(do2, first)
            delta = jnp.concatenate(_rowsum_heads(do2.astype(F32) * o_ref[pl.ds(r0, t), :], first), axis=0)
            lse = jnp.concatenate([lse_ref[pl.ds(r0, t), 0:1], lse_ref[pl.ds(r0, t), LANES:LANES + 1]], axis=0)
            carry = (jnp.zeros((t, LANES), F32), jnp.zeros((2 * t, 1), F32))
            carry = _two_at_a_time(i, lambda j, c: tile(j, c, False, qs, dos, delta, lse), carry)
            dq_acc, rs = tile(i, carry, True, qs, dos, delta, lse)
            dq_ref[pl.ds(r0, t), :] = (dq_acc * Q_SCALE).astype(BF16)
            as_row = lambda col_vec: jnp.sum(jnp.where(eye, col_vec, 0.0), axis=0, keepdims=True)
            dc_ref[0, i] = dc_ref[0, i] + jnp.concatenate([as_row(rs[:t]), as_row(rs[t:])], axis=0)
            return 0

        lax.fori_loop(0, nq, qblock, 0)
        dk_ref[...] = dk_acc[...].astype(BF16)
        dv_ref[...] = dv_acc[...].astype(BF16)

    out = jax.ShapeDtypeStruct((s, BRANCH), BF16)
    return _grid_call(
        body, name=name, grid=(N_PAIRS,),
        in_specs=[_pair_spec(s), _pair_spec(s, N_PAIRS), _pair_spec(s, 2 * N_PAIRS), _cum_spec(nq, t),
                  _pair_spec(s), _stat_spec(s), _pair_spec(s)],
        out_specs=[_pair_spec(s)] * 3 + [_cum_spec(nq, t)],
        out_shape=[out] * 3 + [jax.ShapeDtypeStruct(cum4.shape, F32)],
        scratch_shapes=[pltpu.VMEM((s, LANES), F32), pltpu.VMEM((s, LANES), F32)],
        args=(qkv, qkv, qkv, cum4, o, lse, do), semantics=("parallel",), exchange=exchange)


FG_CHUNK = 512


def _tri_dot3(x, t):
    hi = x.astype(BF16)
    r1 = x - hi.astype(F32)
    mid = r1.astype(BF16)
    lo = (r1 - mid.astype(F32)).astype(BF16)
    return (jnp.dot(hi, t, preferred_element_type=F32) + jnp.dot(mid, t, preferred_element_type=F32)
            + jnp.dot(lo, t, preferred_element_type=F32))


def _fgate_fwd(h, wf_t, b_col, name):
    s = h.shape[0]
    c = min(FG_CHUNK, s)

    def body(h_ref, w_ref, b_ref, xf_ref, cum_ref, carry_ref):
        @pl.when(pl.program_id(0) == 0)
        def _():
            carry_ref[...] = jnp.zeros_like(carry_ref)

        xf = lax.dot_general(w_ref[...], h_ref[...], _NT, preferred_element_type=F32) + b_ref[:, 0:1]
        xf_ref[...] = xf
        logf = jnp.minimum(xf, 0.0) - jnp.log(1.0 + jnp.exp(-jnp.abs(xf)))
        row = lax.broadcasted_iota(jnp.int32, (c, c), 0)
        col = lax.broadcasted_iota(jnp.int32, (c, c), 1)
        cum = _tri_dot3(logf, (row <= col).astype(BF16)) + carry_ref[:, 0:1]
        cum_ref[...] = cum
        carry_ref[...] = _bcast_lanes(cum[:, c - 1:c])

    out = jax.ShapeDtypeStruct((N_HEADS, s), F32)
    return pl.pallas_call(
        body, name=name, grid=(s // c,),
        in_specs=[pl.BlockSpec((c, D_MODEL), lambda i: (i, 0)),
                  pl.BlockSpec((N_HEADS, D_MODEL), lambda i: (0, 0)),
                  pl.BlockSpec((N_HEADS, LANES), lambda i: (0, 0))],
        out_specs=[pl.BlockSpec((N_HEADS, c), lambda i: (0, i))] * 2,
        out_shape=[out, out],
        scratch_shapes=[pltpu.VMEM((N_HEADS, LANES), F32)],
        compiler_params=_params(("arbitrary",)),
    )(h, wf_t, b_col)


def _fgate_bwd(dcum, xf, h, wf_t, name):
    s = h.shape[0]
    c = min(FG_CHUNK, s)
    n = s // c

    def body(dc_ref, xf_ref, h_ref, w_ref, dw_ref, dh_ref, db_ref, carry_ref):
        @pl.when(pl.program_id(0) == 0)
        def _():
            carry_ref[...] = jnp.zeros_like(carry_ref)
            dw_ref[...] = jnp.zeros_like(dw_ref)
            db_ref[...] = jnp.zeros_like(db_ref)

        row = lax.broadcasted_iota(jnp.int32, (c, c), 0)
        col = lax.broadcasted_iota(jnp.int32, (c, c), 1)
        dlogf = _tri_dot3(dc_ref[...], (row >= col).astype(BF16)) + carry_ref[:, 0:1]
        carry_ref[...] = _bcast_lanes(dlogf[:, 0:1])
        xf = xf_ref[...]
        e = jnp.exp(-jnp.abs(xf))
        r = 1.0 / (1.0 + e)
        dxf = dlogf * jnp.where(xf >= 0, e * r, r)
        db_ref[...] += _bcast_lanes(jnp.sum(dxf, axis=1, keepdims=True))
        dxb = dxf.astype(BF16)
        dw_ref[...] += jnp.dot(dxb, h_ref[...], preferred_element_type=F32)
        dh_ref[...] = lax.dot_general(dxb, w_ref[...], _TN, preferred_element_type=F32)

    rev = lambda i: n - 1 - i
    return pl.pallas_call(
        body, name=name, grid=(n,),
        in_specs=[pl.BlockSpec((N_HEADS, c), lambda i: (0, rev(i))),
                  pl.BlockSpec((N_HEADS, c), lambda i: (0, rev(i))),
                  pl.BlockSpec((c, D_MODEL), lambda i: (rev(i), 0)),
                  pl.BlockSpec((N_HEADS, D_MODEL), lambda i: (0, 0))],
        out_specs=[pl.BlockSpec((N_HEADS, D_MODEL), lambda i: (0, 0)),
                   pl.BlockSpec((c, D_MODEL), lambda i: (rev(i), 0)),
                   pl.BlockSpec((N_HEADS, LANES), lambda i: (0, 0))],
        out_shape=[jax.ShapeDtypeStruct((N_HEADS, D_MODEL), F32), jax.ShapeDtypeStruct((s, D_MODEL), F32),
                   jax.ShapeDtypeStruct((N_HEADS, LANES), F32)],
        scratch_shapes=[pltpu.VMEM((N_HEADS, LANES), F32)],
        compiler_params=_params(("arbitrary",)),
    )(dcum, xf, h, wf_t)


def _to_cum4(v, t):
    s = v.shape[1]
    return v.reshape(N_PAIRS, 2, s // t, t).transpose(0, 2, 1, 3)


def _from_cum4(v4):
    p, nt, two, t = v4.shape
    return v4.transpose(0, 2, 1, 3).reshape(p * two, nt * t)


def _alibi_slopes():
    return (2.0 ** (-8.0 * np.arange(1, N_HEADS + 1, dtype=np.float32) / N_HEADS)).astype(np.float32)


def _per_head_lanes(v):
    return jnp.repeat(v.astype(F32).reshape(N_PAIRS, 1, 2), LANES, axis=2)


def _attn_a_specs(s):
    q = _pair_spec(s)
    k = pl.BlockSpec((s, LANES), lambda p: (0, N_PAIRS + p // 8))
    v = pl.BlockSpec((s, LANES), lambda p: (0, N_PAIRS + KV_A // LANES + p // 8))
    head = pl.BlockSpec((1, 1, 2 * LANES), lambda p: (p, 0, 0))
    return q, k, v, head


def _attn_a_geometry(p, slope_ref, sink_ref):
    kv_half = (p // 4) % 2
    kv_first = kv_half == 0
    lane_first = _lane_is_first_head()
    kv_lanes = (lax.broadcasted_iota(jnp.int32, (1, LANES), 1) // HEAD_DIM) == kv_half
    row = lax.broadcasted_iota(jnp.int32, (2 * WINDOW, 2 * WINDOW), 0)
    cj = lax.broadcasted_iota(jnp.int32, (2 * WINDOW, 2 * WINDOW), 1)
    second = row >= WINDOW
    dist = WINDOW + jnp.where(second, row - WINDOW, row) - cj
    valid = (dist >= 0) & (dist < WINDOW)
    per_row = lambda ref: jnp.where(second[:, 0:1], ref[0, :, LANES:LANES + 1], ref[0, :, 0:1])
    return kv_first, lane_first, kv_lanes, per_row(slope_ref) * dist.astype(F32), valid, per_row(sink_ref)


def _swap_halves(x):
    return pltpu.roll(x, HEAD_DIM, 1)


def _attn_a_fwd(qkv, slopes, sinks, name, exchange=None):
    s = qkv.shape[0]
    nb = s // WINDOW

    def body(q_ref, k_ref, v_ref, sl_ref, sk_ref, o_ref, lse_ref):
        kv_first, lane_first, kv_lanes, bias, valid, sink = _attn_a_geometry(pl.program_id(0), sl_ref, sk_ref)

        def block(r0, k0, width):
            q2 = q_ref[pl.ds(r0, WINDOW), :].astype(F32) * Q_SCALE
            q2r = _swap_halves(q2)
            xs = jnp.concatenate([jnp.where(kv_first, q2, q2r), jnp.where(kv_first, q2r, q2)], axis=0).astype(BF16)
            km = jnp.where(kv_lanes, k_ref[pl.ds(k0, width), :], 0).astype(BF16)
            vm = jnp.where(kv_lanes, v_ref[pl.ds(k0, width), :], 0).astype(BF16)
            sc = lax.dot_general(xs, km, _NT, preferred_element_type=F32) - bias[:, 2 * WINDOW - width:]
            sc = jnp.where(valid[:, 2 * WINDOW - width:], sc, NEG)
            m = jnp.maximum(jnp.max(sc, axis=1, keepdims=True), sink)
            pr = jnp.exp(sc - m)
            l = jnp.sum(pr, axis=1, keepdims=True) + jnp.exp(sink - m)
            os = jnp.dot(pr.astype(BF16), vm, preferred_element_type=F32) * (1.0 / l)
            lse = m + jnp.log(l)
            lse_ref[pl.ds(r0, WINDOW), 0:LANES] = _bcast_lanes(lse[:WINDOW])
            lse_ref[pl.ds(r0, WINDOW), LANES:2 * LANES] = _bcast_lanes(lse[WINDOW:])
            oa = jnp.where(kv_first, os[:WINDOW], _swap_halves(os[:WINDOW]))
            ob = jnp.where(kv_first, _swap_halves(os[WINDOW:]), os[WINDOW:])
            o_ref[pl.ds(r0, WINDOW), :] = jnp.where(lane_first, oa, ob)

        block(0, 0, WINDOW)

        def loop(n, _):
            r0 = pl.multiple_of(n * WINDOW, WINDOW)
            block(r0, pl.multiple_of(r0 - WINDOW, WINDOW), 2 * WINDOW)
            return 0

        _two_at_a_time(nb - 1, lambda n, c: loop(n + 1, c), 0)

    q, k, v, head = _attn_a_specs(s)
    return _grid_call(
        body, name=name, grid=(N_PAIRS,),
        in_specs=[q, k, v, head, head],
        out_specs=[_pair_spec(s), _stat_spec(s)],
        out_shape=[jax.ShapeDtypeStruct((s, BRANCH), F32), jax.ShapeDtypeStruct((s, N_HEADS * LANES), F32)],
        args=(qkv, qkv, qkv, slopes, sinks), semantics=("parallel",), exchange=exchange)


def _attn_a_bwd(qkv, slopes, sinks, o, lse, do, name, exchange=None):
    s = qkv.shape[0]
    nb = s // WINDOW

    def body(q_ref, k_ref, v_ref, sl_ref, sk_ref, o_ref, lse_ref, do_ref, dq_ref, dk_ref, dv_ref, dsk_ref):
        p_id = pl.program_id(0)
        kv_first, lane_first, kv_lanes, bias, valid, sink = _attn_a_geometry(p_id, sl_ref, sk_ref)

        @pl.when(p_id % 8 == 0)
        def _():
            dk_ref[...] = jnp.zeros_like(dk_ref)
            dv_ref[...] = jnp.zeros_like(dv_ref)

        def align(v2):
            v2r = _swap_halves(v2)
            both = jnp.concatenate([jnp.where(kv_first, v2, v2r), jnp.where(kv_first, v2r, v2)], axis=0)
            return jnp.where(kv_lanes, both, 0.0).astype(BF16)

        def block(r0, k0, width, sink_sum):
            xq = align(q_ref[pl.ds(r0, WINDOW), :].astype(F32) * Q_SCALE)
            do2 = do_ref[pl.ds(r0, WINDOW), :].astype(F32)
            xdo = align(do2)
            delta = jnp.concatenate(_rowsum_heads(do2 * o_ref[pl.ds(r0, WINDOW), :], lane_first), axis=0)
            lse = jnp.concatenate([lse_ref[pl.ds(r0, WINDOW), 0:1], lse_ref[pl.ds(r0, WINDOW), LANES:LANES + 1]], axis=0)
            km = jnp.where(kv_lanes, k_ref[pl.ds(k0, width), :], 0).astype(BF16)
            vm = jnp.where(kv_lanes, v_ref[pl.ds(k0, width), :], 0).astype(BF16)
            sc = lax.dot_general(xq, km, _NT, preferred_element_type=F32) - bias[:, 2 * WINDOW - width:]
            pr = jnp.where(valid[:, 2 * WINDOW - width:], jnp.exp(sc - lse), 0.0)
            ds = pr * (lax.dot_general(xdo, vm, _NT, preferred_element_type=F32) - delta)
            dsb = ds.astype(BF16)
            dq_al = jnp.dot(dsb, km, preferred_element_type=F32)
            dk_ref[pl.ds(k0, width), :] += lax.dot_general(dsb, xq, _TN, preferred_element_type=F32)
            dv_ref[pl.ds(k0, width), :] += lax.dot_general(pr.astype(BF16), xdo, _TN, preferred_element_type=F32)
            dqa = jnp.where(kv_first, dq_al[:WINDOW], _swap_halves(dq_al[:WINDOW]))
            dqb = jnp.where(kv_first, _swap_halves(dq_al[WINDOW:]), dq_al[WINDOW:])
            dq_ref[pl.ds(r0, WINDOW), :] = (jnp.where(lane_first, dqa, dqb) * Q_SCALE).astype(BF16)
            return sink_sum + jnp.exp(sink - lse) * delta

        sink_sum = block(0, 0, WINDOW, jnp.zeros((2 * WINDOW, 1), F32))

        def loop(n, c):
            r0 = pl.multiple_of(n * WINDOW, WINDOW)
            return block(r0, pl.multiple_of(r0 - WINDOW, WINDOW), 2 * WINDOW, c)

        sink_sum = _two_at_a_time(nb - 1, lambda n, c: loop(n + 1, c), sink_sum)
        dsk_ref[0, :, 0:LANES] = jnp.broadcast_to(-jnp.sum(sink_sum[:WINDOW], axis=0, keepdims=True), (1, LANES))
        dsk_ref[0, :, LANES:2 * LANES] = jnp.broadcast_to(-jnp.sum(sink_sum[WINDOW:], axis=0, keepdims=True), (1, LANES))

    q, k, v, head = _attn_a_specs(s)
    kv_out = pl.BlockSpec((s, LANES), lambda p: (0, p // 8))
    return _grid_call(
        body, name=name, grid=(N_PAIRS,),
        in_specs=[q, k, v, head, head, _pair_spec(s), _stat_spec(s), _pair_spec(s)],
        out_specs=[_pair_spec(s), kv_out, kv_out, head],
        out_shape=[jax.ShapeDtypeStruct((s, BRANCH), BF16), jax.ShapeDtypeStruct((s, KV_A), F32),
                   jax.ShapeDtypeStruct((s, KV_A), F32), jax.ShapeDtypeStruct((N_PAIRS, 1, 2 * LANES), F32)],
        args=(qkv, qkv, qkv, slopes, sinks, o, lse, do), semantics=("arbitrary",), exchange=exchange)


def _layer_kind(i):
    return i % 3, i // 3


GATHER_FIRST = [("in", 0)]
GATHER_BEHIND = {("qkv", 0): [("out", 0)], ("attn", 0): [("in", 1)], ("attn", 1): [("out", 1), ("in", 2), ("out", 2)],
                 ("attn", 2): [("in", 3), ("out", 3)]}


def _forward_backward(x, target, g_pre, g_post, sinks_a, b_f_c, shards, chip, core):
    s = x.shape[0]
    slopes = _per_head_lanes(jnp.asarray(_alibi_slopes()))
    w_in, w_out, wf_t = {}, {}, {}

    def lands_side_by_side(key):
        return key[0] == "in" and shards[key].shape[1] % LANES == 0

    def gather(keys):
        return _GatherExchange([shards[k] for k in keys], [lands_side_by_side(k) for k in keys])

    def deliver(keys, gathered):
        for key, g in zip(keys, gathered):
            side, layer = key
            sh = shards[key]
            if side == "out":
                g = lax.dynamic_update_slice(g, sh[None], (chip, 0, 0))
                w_out[layer] = g.reshape(4 * sh.shape[0], sh.shape[1])
            elif lands_side_by_side(key):
                w_in[layer] = _place_columns(g, sh, chip, f"own_block_in_l{layer}")
            else:
                g = lax.dynamic_update_slice(g, sh[None], (chip, 0, 0))
                w = g.transpose(1, 0, 2).reshape(sh.shape[0], 4 * sh.shape[1])
                w_in[layer], wf_t[layer] = w[:, :4 * BRANCH], w[:, 4 * BRANCH:].T

    deliver(GATHER_FIRST, _exchange_call(gather(GATHER_FIRST), "gather_first_weights"))
    saved = []
    for i in range(DEPTH):
        kind, j = _layer_kind(i)
        tag = f"l{i}"
        w = w_in[i]
        nqkv = A_QKV if kind == 0 else B_QKV
        tn = 512 if kind == 0 else 1024
        h, h_t = _rmsnorm_fwd(x, g_pre[i:i + 1], f"prenorm_{tag}")
        behind = GATHER_BEHIND.get(("qkv", i))
        qkv = _matmul(h, w, mode="nn", out_dtype=BF16, name=f"inproj_qkv_{tag}", n=nqkv, tn=tn,
                      exchange=gather(behind) if behind else None)
        if behind:
            qkv, arrived = qkv
            deliver(behind, arrived)
        z = _matmul(h, w, mode="nn", out_dtype=F32, name=f"inproj_gate_{tag}", n=BRANCH, b_off=nqkv // tn, tn=tn)
        behind = GATHER_BEHIND.get(("attn", i))
        exchange = gather(behind) if behind else None
        if kind == 0:
            sink_l = _per_head_lanes(sinks_a[j])
            (o, lse), arrived = _attn_a_fwd(qkv, slopes, sink_l, f"attn_a_fwd_{tag}", exchange)
            extra = (sink_l, lse)
        elif kind == 1:
            (o, extra), arrived = _attn_b_fwd(qkv, f"attn_b_fwd_{tag}", exchange)
        else:
            b_col = jnp.broadcast_to(b_f_c[j].astype(F32)[:, None], (N_HEADS, LANES))
            xf, cum = _fgate_fwd(h, wf_t[i], b_col, f"fgate_fwd_{tag}")
            cum4 = _to_cum4(cum, _fox_tile(s))
            (o, lse), arrived = _attn_c_fwd(qkv, cum4, f"attn_c_fwd_{tag}", exchange)
            extra = (xf, cum4, lse)
        if behind:
            deliver(behind, arrived)
        u, u_t = _gate_fwd(o, z, f"gate_{tag}")
        y = _matmul(u, w_out[i], mode="nn", out_dtype=F32, name=f"outproj_{tag}")
        saved.append((x, h, h_t, qkv, z, o, u_t, y, extra))
        x = _post_fwd(x, y, g_post[i:i + 1], f"postnorm_{tag}")

    dx, loss_part = _loss_and_grad(x, target)

    d_g_pre, d_g_post = [None] * DEPTH, [None] * DEPTH
    d_sinks = [None, None]
    d_b_f = None
    reduced = {}
    pending = None

    def finish_reduce(layer, side, own, arr):
        kind, j = _layer_kind(layer)
        reduced[(side, kind)] = _sum_chips(own, arr, core, f"shard_sum_{side}_l{layer}", j, 2 if kind == 0 else 1,
                                           into=reduced.get((side, kind)))

    for i in reversed(range(DEPTH)):
        kind, j = _layer_kind(i)
        tag = f"l{i}"
        x_in, h, h_t, qkv, z, o, u_t, y, extra = saved[i]
        tn = 512 if kind == 0 else 1024
        dy, d_g_post[i] = _post_bwd(dx, y, g_post[i:i + 1], f"postnorm_bwd_{tag}")
        dw_out = _matmul(u_t, dy, mode="nn", out_dtype=BF16, name=f"dw_out_{tag}")
        dw_out = dw_out.reshape(4, dw_out.shape[0] // 4, dw_out.shape[1])
        du, (their_out,) = _matmul(dy, w_out[i], mode="nt", out_dtype=F32, name=f"d_gated_{tag}",
                                   exchange=_SiblingExchange([dw_out]))
        sum_out = _add_pairs(dw_out, their_out, f"chip_sum_out_{tag}")
        do, dz = _gate_bwd(du, o, z, f"gate_bwd_{tag}")
        dhs = []
        exchange = _ScatterExchange([pending[1]]) if pending else None
        if kind == 0:
            sink_l, lse = extra
            (dq, dk, dv, dsk), arrived = _attn_a_bwd(qkv, slopes, sink_l, o, lse, do, f"attn_a_bwd_{tag}", exchange)
            d_sinks[j] = dsk[:, 0, ::LANES].reshape(N_HEADS)
            parts = [dq, dk.astype(BF16), dv.astype(BF16), dz]
        elif kind == 1:
            (dq, dk, dv), arrived = _attn_b_bwd(qkv, extra, do, f"attn_b_bwd_{tag}", exchange)
            parts = [dq, dk, dv, dz]
        else:
            xf, cum4, lse = extra
            (dq, dk, dv, dcum4), arrived = _attn_c_bwd(qkv, cum4, o, lse, do, f"attn_c_bwd_{tag}", exchange)
            d_wf_t, dh_f, db = _fgate_bwd(_from_cum4(dcum4), xf, h, wf_t[i], f"fgate_bwd_{tag}")
            d_b_f = db[:, 0]
            dhs.append(dh_f)
            parts = [dq, dk, dv, dz]
        if pending:
            finish_reduce(pending[0], "in", pending[1], arrived[0])
        dproj = jnp.concatenate(parts, axis=1)
        scatter_out = _ScatterExchange([sum_out])
        if kind == 2:
            dw_in, arrived = _matmul(h_t, dproj, mode="nn", out_dtype=F32, name=f"dw_in_{tag}", tn=tn, exchange=scatter_out)
            dw_in = jnp.concatenate([dw_in, d_wf_t.T], axis=1)
            dw_in = dw_in.reshape(dw_in.shape[0], 4, dw_in.shape[1] // 4).transpose(1, 0, 2).astype(BF16)
        else:
            dw_in, arrived = _matmul(h_t, dproj, mode="nn", out_dtype=BF16, name=f"dw_in_{tag}", col_blocks=4,
                                     tn=1152 if kind == 0 else 1024, exchange=scatter_out)
        finish_reduce(i, "out", sum_out, arrived[0])
        dh, (their_in,) = _matmul(dproj, w_in[i], mode="nt", out_dtype=F32, name=f"dh_{tag}",
                                  tk=1536 if kind == 0 else 2048, exchange=_SiblingExchange([dw_in]))
        dhs.insert(0, dh)
        dx, d_g_pre[i] = _pre_bwd(dx, dhs, x_in, g_pre[i:i + 1], f"prenorm_bwd_{tag}")
        pending = (i, _add_pairs(dw_in, their_in, f"chip_sum_in_{tag}"))

    arrived = _exchange_call(_ScatterExchange([pending[1]]), "grad_chip_scatter_last")
    finish_reduce(pending[0], "in", pending[1], arrived[0])

    return dict(loss=loss_part, dx=dx, g_pre=jnp.concatenate(d_g_pre, axis=0), g_post=jnp.concatenate(d_g_post, axis=0),
                sinks_a=jnp.stack(d_sinks), b_f_c=d_b_f[None, :], reduced=reduced)


def _place():
    x, y, c = lax.axis_index("x"), lax.axis_index("y"), lax.axis_index("c")
    others = [(1 - x, y), (x, 1 - y), (1 - x, 1 - y)]
    return x, y, c, others


def _half_rows(ref_rows, which):
    half = ref_rows // 2
    return pl.ds(pl.multiple_of(which * half, half), half)


def _remote(src, dst, sems, k, device):
    send, recv = sems
    return pltpu.make_async_remote_copy(src_ref=src, dst_ref=dst, send_sem=send.at[k], recv_sem=recv.at[k],
                                        device_id=device, device_id_type=MESH)


def _hbm_call(body, name, ins, out_shapes, n_remote, aliases=None):
    any_spec = pl.BlockSpec(memory_space=pl.ANY)
    return pl.pallas_call(
        body, name=name, in_specs=[any_spec] * len(ins), out_specs=[any_spec] * len(out_shapes),
        out_shape=out_shapes, input_output_aliases=aliases or {},
        scratch_shapes=[pltpu.SemaphoreType.DMA((n_remote,)), pltpu.SemaphoreType.DMA((n_remote,))],
    )(*ins)


class _GatherExchange:
    def __init__(self, shards, side_by_side):
        self.ins = list(shards)
        self.side_by_side = list(side_by_side)
        self.out_shapes = [jax.ShapeDtypeStruct((a.shape[0], 4 * a.shape[1]) if wide else (4,) + a.shape, a.dtype)
                           for a, wide in zip(shards, side_by_side)]
        self.n_sems = 6 * len(shards)
        self.aliases = {}

    def _copies(self, ins, outs, sems):
        x, y, c, others = _place()
        me = 2 * x + y
        table = []
        for w, (src, dst, wide) in enumerate(zip(ins, outs, self.side_by_side)):
            rows, cols = src.shape
            mine, theirs = _half_rows(rows, c), _half_rows(rows, 1 - c)

            def slot(chip, which, dst=dst, wide=wide, cols=cols):
                return dst.at[which, pl.ds(pl.multiple_of(chip * cols, LANES), cols)] if wide else dst.at[chip, which]

            for j, (px, py) in enumerate(others):
                there = 2 * px + py
                send = _remote(src.at[mine], slot(me, mine), sems, 6 * w + j, (px, py, c))
                landed = _remote(slot(there, mine), slot(there, mine), sems, 6 * w + j, (px, py, c))
                passed = _remote(slot(there, mine), slot(there, mine), sems, 6 * w + 3 + j, (x, y, 1 - c))
                from_sibling = _remote(slot(there, theirs), slot(there, theirs), sems, 6 * w + 3 + j, (x, y, 1 - c))
                table.append((send, landed, passed, from_sibling))
        return table

    def start(self, ins, outs, sems):
        for send, _, _, _ in self._copies(ins, outs, sems):
            send.start()

    def mid(self, ins, outs, sems):
        for _, landed, passed, _ in self._copies(ins, outs, sems):
            landed.wait_recv()
            passed.start()

    def finish(self, ins, outs, sems):
        table = self._copies(ins, outs, sems)
        for _, _, _, from_sibling in table:
            from_sibling.wait_recv()
        for send, _, passed, _ in table:
            send.wait_send()
            passed.wait_send()


def _place_columns(wide, block, chip, name):
    rows, cc = block.shape
    tr = min(512, rows)

    def body(c_ref, b_ref, w_ref, o_ref):
        o_ref[...] = b_ref[...]

    return pl.pallas_call(
        body, name=name,
        grid_spec=pltpu.PrefetchScalarGridSpec(
            num_scalar_prefetch=1, grid=(rows // tr,),
            in_specs=[pl.BlockSpec((tr, cc), lambda r, c_ref: (r, 0)), pl.BlockSpec(memory_space=pl.ANY)],
            out_specs=pl.BlockSpec((tr, cc), lambda r, c_ref: (r, c_ref[0]))),
        out_shape=jax.ShapeDtypeStruct(wide.shape, wide.dtype), input_output_aliases={2: 0},
        compiler_params=_params(("parallel",)),
    )(chip.astype(jnp.int32).reshape(1), block, wide)


def _exchange_call(ex, name):
    n_in, n_out = len(ex.ins), len(ex.out_shapes)

    def body(*refs):
        ins, outs, sems = refs[:n_in], refs[n_in:n_in + n_out], refs[n_in + n_out:]
        ex.start(ins, outs, sems)
        ex.mid(ins, outs, sems)
        ex.finish(ins, outs, sems)

    return _hbm_call(body, name, ex.ins, ex.out_shapes, ex.n_sems, aliases=ex.aliases)


def _grid_call(body, *, name, grid, in_specs, out_specs, out_shape, args, scratch_shapes=(), semantics, exchange=None):
    if exchange is None:
        res = pl.pallas_call(body, name=name, grid=grid, in_specs=list(in_specs), out_specs=list(out_specs),
                             out_shape=list(out_shape), scratch_shapes=list(scratch_shapes),
                             compiler_params=_params(semantics))(*args)
        return res, []
    n_in, n_out, n_scr = len(args), len(out_shape), len(scratch_shapes)
    x_in, x_out = len(exchange.ins), len(exchange.out_shapes)
    steps = math.prod(grid)

    def wrapped(*refs):
        core_in, ex_in = refs[:n_in], refs[n_in:n_in + x_in]
        rest = refs[n_in + x_in:]
        core_out, ex_out = rest[:n_out], rest[n_out:n_out + x_out]
        scratch, sems = rest[n_out + x_out:n_out + x_out + n_scr], rest[n_out + x_out + n_scr:]
        step = 0
        for axis, extent in enumerate(grid):
            step = step * extent + pl.program_id(axis)

        @pl.when(step == 0)
        def _():
            exchange.start(ex_in, ex_out, sems)

        body(*core_in, *core_out, *scratch)

        @pl.when(step == max((3 * steps) // 4 - 1, 0))
        def _():
            exchange.mid(ex_in, ex_out, sems)

        @pl.when(step == steps - 1)
        def _():
            exchange.finish(ex_in, ex_out, sems)

    any_spec = pl.BlockSpec(memory_space=pl.ANY)
    res = pl.pallas_call(
        wrapped, name=name, grid=grid,
        in_specs=list(in_specs) + [any_spec] * x_in, out_specs=list(out_specs) + [any_spec] * x_out,
        out_shape=list(out_shape) + list(exchange.out_shapes),
        input_output_aliases={n_in + a: n_out + b for a, b in exchange.aliases.items()},
        scratch_shapes=list(scratch_shapes) + [pltpu.SemaphoreType.DMA((exchange.n_sems,)),
                                               pltpu.SemaphoreType.DMA((exchange.n_sems,))],
        compiler_params=_params(("arbitrary",) * len(grid)),
    )(*args, *exchange.ins)
    return res[:n_out], res[n_out:]


class _SiblingExchange:
    def __init__(self, parts):
        self.ins = list(parts)
        self.out_shapes = [jax.ShapeDtypeStruct((4, a.shape[1] // 2, a.shape[2]), a.dtype) for a in parts]
        self.n_sems = len(parts)
        self.aliases = {}

    def _copies(self, ins, outs, sems):
        x, y, c, _ = _place()
        return [_remote(src.at[:, _half_rows(src.shape[1], 1 - c)], dst, sems, w, (x, y, 1 - c))
                for w, (src, dst) in enumerate(zip(ins, outs))]

    def start(self, ins, outs, sems):
        for cp in self._copies(ins, outs, sems):
            cp.start()

    def mid(self, ins, outs, sems):
        pass

    def finish(self, ins, outs, sems):
        for cp in self._copies(ins, outs, sems):
            cp.wait_recv()
            cp.wait_send()


class _ScatterExchange:
    def __init__(self, sums):
        self.ins = list(sums)
        self.out_shapes = [jax.ShapeDtypeStruct(a.shape, a.dtype) for a in sums]
        self.n_sems = 3 * len(sums)
        self.aliases = {}

    def _copies(self, ins, outs, sems):
        x, y, c, others = _place()
        me = 2 * x + y
        table = []
        for w, (src, dst) in enumerate(zip(ins, outs)):
            for j, (px, py) in enumerate(others):
                there = 2 * px + py
                send = _remote(src.at[there], dst.at[me], sems, 3 * w + j, (px, py, c))
                landed = _remote(dst.at[there], dst.at[there], sems, 3 * w + j, (px, py, c))
                table.append((send, landed))
        return table

    def start(self, ins, outs, sems):
        for send, _ in self._copies(ins, outs, sems):
            send.start()

    def mid(self, ins, outs, sems):
        pass

    def finish(self, ins, outs, sems):
        table = self._copies(ins, outs, sems)
        for _, landed in table:
            landed.wait_recv()
        for send, _ in table:
            send.wait_send()


def _sibling_join(shards):
    n = len(shards)

    def body(*refs):
        ins, outs, sems = refs[:n], refs[n:2 * n], refs[2 * n:2 * n + 2]
        x, y, c, _ = _place()
        pend = []
        for w in range(n):
            rows = ins[w].shape[1]
            mine, theirs = _half_rows(rows, c), _half_rows(rows, 1 - c)
            cp = _remote(ins[w].at[:, mine], outs[w].at[:, mine], sems, w, (x, y, 1 - c))
            cp.start()
            pend.append((cp, _remote(ins[w].at[:, theirs], outs[w].at[:, theirs], sems, w, (x, y, 1 - c))))
        for cp, landed in pend:
            landed.wait_recv()
            cp.wait_send()

    out_shapes = [jax.ShapeDtypeStruct(a.shape, a.dtype) for a in shards]
    return _hbm_call(body, "grad_sibling_join", shards, out_shapes, n, aliases={w: w for w in range(n)})


SMALL_ROWS = 136


def _all_reduce_small(vec):
    def body(v_ref, o_ref, buf, send, recv, loc):
        x, y, c, _ = _place()
        me = 4 * x + 2 * y + c
        lc = pltpu.make_async_copy(v_ref, buf.at[me], loc.at[0])
        lc.start()
        cps = []
        for k in range(1, 8):
            fx, fy, fc = (k >> 2) & 1, (k >> 1) & 1, k & 1
            peer = (x ^ fx, y ^ fy, c ^ fc)
            cp = pltpu.make_async_remote_copy(src_ref=v_ref, dst_ref=buf.at[me], send_sem=send.at[k - 1],
                                              recv_sem=recv.at[k - 1], device_id=peer, device_id_type=MESH)
            cp.start()
            cps.append((cp, 4 * peer[0] + 2 * peer[1] + peer[2]))
        for k, (cp, src) in enumerate(cps):
            pltpu.make_async_remote_copy(src_ref=v_ref, dst_ref=buf.at[src], send_sem=send.at[k], recv_sem=recv.at[k],
                                         device_id=(x, y, c), device_id_type=MESH).wait_recv()
        for cp, _ in cps:
            cp.wait_send()
        lc.wait()
        total = buf[0]
        for k in range(1, 8):
            total = total + buf[k]
        o_ref[...] = total

    vm = pl.BlockSpec(memory_space=pltpu.VMEM)
    return pl.pallas_call(
        body, name="all_reduce_small", in_specs=[vm], out_specs=vm,
        out_shape=jax.ShapeDtypeStruct(vec.shape, F32),
        scratch_shapes=[pltpu.VMEM((8,) + vec.shape, F32), pltpu.SemaphoreType.DMA((7,)),
                        pltpu.SemaphoreType.DMA((7,)), pltpu.SemaphoreType.DMA((1,))],
    )(vec)


SUM_ROWS = 256


def _add_pairs(part, theirs, name):
    four, rh, cc = theirs.shape
    tr = min(SUM_ROWS, rh)
    halves = part.reshape(four, 2, rh, cc)

    def body(a_ref, b_ref, o_ref):
        mine = a_ref[0, lax.axis_index("c")]
        o_ref[0] = (mine.astype(F32) + b_ref[0].astype(F32)).astype(o_ref.dtype)

    spec = pl.BlockSpec((1, tr, cc), lambda k, r: (k, r, 0))
    return pl.pallas_call(
        body, name=name, grid=(four, rh // tr),
        in_specs=[pl.BlockSpec((1, 2, tr, cc), lambda k, r: (k, 0, r, 0)), spec], out_specs=spec,
        out_shape=jax.ShapeDtypeStruct(theirs.shape, theirs.dtype),
        compiler_params=_params(("parallel", "parallel")),
    )(halves, theirs)


def _sum_chips(own, arrived, core, name, layer, n_layers, into=None):
    four, rh, cc = own.shape
    tr = min(SUM_ROWS, rh)
    nr = rh // tr

    def body(c_ref, own_ref, arr_ref, *rest):
        o_ref = rest[-1]
        x, y = lax.axis_index("x"), lax.axis_index("y")
        tot = own_ref[2 * x + y].astype(F32)
        for px, py in ((1 - x, y), (x, 1 - y), (1 - x, 1 - y)):
            tot = tot + arr_ref[2 * px + py].astype(F32)
        o_ref[0] = tot

    blk = pl.BlockSpec((4, tr, cc), lambda r, c_ref: (0, r, 0))
    in_specs, args, aliases = [blk, blk], [core, own, arrived], {}
    if into is not None:
        in_specs.append(pl.BlockSpec(memory_space=pl.ANY))
        args.append(into)
        aliases = {3: 0}
    return pl.pallas_call(
        body, name=name,
        grid_spec=pltpu.PrefetchScalarGridSpec(
            num_scalar_prefetch=1, grid=(nr,), in_specs=in_specs,
            out_specs=pl.BlockSpec((1, tr, cc), lambda r, c_ref: (layer, c_ref[0] * nr + r, 0))),
        out_shape=jax.ShapeDtypeStruct((n_layers, 2 * rh, cc), F32), input_output_aliases=aliases,
        compiler_params=_params(("parallel",)),
    )(*args)


ADAM_ROWS = 256


def _adamw(w, g, m, v, name):
    shape = w.shape
    as3 = lambda a: a.reshape((-1,) + shape[-2:])
    layers, rows, cc = as3(w).shape
    tr = min(ADAM_ROWS, rows)
    assert rows % tr == 0
    c1 = 1.0 - ADAM_B1 ** ADAM_STEP
    c2 = 1.0 - ADAM_B2 ** ADAM_STEP

    def body(w_ref, g_ref, m_ref, v_ref, d_ref, nm_ref, nv_ref):
        gv = g_ref[...]
        nm = ADAM_B1 * m_ref[...] + (1.0 - ADAM_B1) * gv
        nv = ADAM_B2 * v_ref[...] + (1.0 - ADAM_B2) * (gv * gv)
        nm_ref[...] = nm
        nv_ref[...] = nv
        d_ref[...] = -ADAM_LR * ((nm / c1) / (jnp.sqrt(nv / c2) + ADAM_EPS) + ADAM_WD * w_ref[...])

    spec = pl.BlockSpec((1, tr, cc), lambda l, i: (l, i, 0))
    sh = jax.ShapeDtypeStruct((layers, rows, cc), F32)
    outs = pl.pallas_call(
        body, name=name, grid=(layers, rows // tr), in_specs=[spec] * 4, out_specs=[spec] * 3, out_shape=[sh] * 3,
        compiler_params=_params(("parallel", "parallel")),
    )(as3(w), as3(g), as3(m), as3(v))
    return [o.reshape(shape) for o in outs]


def _pack_small(g_pre, g_post, sinks_a, b_f_c, loss_row):
    pad = lambda a: jnp.pad(a.reshape(1, -1).astype(F32), ((0, 0), (0, LANES - a.size)))
    rows = [g_pre.astype(F32).reshape(-1, LANES), g_post.astype(F32).reshape(-1, LANES), pad(sinks_a), pad(b_f_c), loss_row]
    packed = jnp.concatenate(rows, axis=0)
    return jnp.pad(packed, ((0, SMALL_ROWS - packed.shape[0]), (0, 0)))


def _unpack_small(p):
    n = DEPTH * D_MODEL // LANES
    return (p[:n].reshape(DEPTH, D_MODEL), p[n:2 * n].reshape(DEPTH, D_MODEL), p[2 * n, :2 * N_HEADS].reshape(2, N_HEADS),
            p[2 * n + 1, :N_HEADS].reshape(1, N_HEADS), p[2 * n + 2, 0])


def kernel(x, g_pre, g_post, w_in_a, w_out_a, sinks_a, w_in_b, w_out_b, w_in_c, b_f_c, w_out_c, loss_target, m_g_pre, m_g_post, m_w_in_a, m_w_out_a, m_sinks_a, m_w_in_b, m_w_out_b, m_w_in_c, m_b_f_c, m_w_out_c, v_g_pre, v_g_post, v_w_in_a, v_w_out_a, v_sinks_a, v_w_in_b, v_w_out_b, v_w_in_c, v_b_f_c, v_w_out_c):
    big_w = [w_in_a, w_out_a, w_in_b, w_out_b, w_in_c, w_out_c]
    big_m = [m_w_in_a, m_w_out_a, m_w_in_b, m_w_out_b, m_w_in_c, m_w_out_c]
    big_v = [v_w_in_a, v_w_out_a, v_w_in_b, v_w_out_b, v_w_in_c, v_w_out_c]

    chip = 2 * lax.axis_index("x") + lax.axis_index("y")
    core = lax.axis_index("c").astype(jnp.int32).reshape(1)
    by_kind = {0: (w_in_a, w_out_a), 1: (w_in_b, w_out_b), 2: (w_in_c, w_out_c)}
    shards = {}
    for i in range(DEPTH):
        kind, j = _layer_kind(i)
        shards[("in", i)] = by_kind[kind][0][j].astype(BF16)
        shards[("out", i)] = by_kind[kind][1][j].astype(BF16)

    res = _forward_backward(x[0], loss_target[0], g_pre, g_post, sinks_a, b_f_c, shards, chip, core)
    names = ["w_in_a", "w_out_a", "w_in_b", "w_out_b", "w_in_c", "w_out_c"]
    grads = _sibling_join([res["reduced"][(side, kind)] for kind in range(3) for side in ("in", "out")])

    small = _unpack_small(_all_reduce_small(
        _pack_small(res["g_pre"], res["g_post"], res["sinks_a"], res["b_f_c"], res["loss"])))
    g_small, loss = small[:4], small[4]

    zero_row = jnp.zeros((1, LANES), F32)
    pk = lambda a: _pack_small(a[0], a[1], a[2], a[3], zero_row)
    sm = _adamw(pk([g_pre, g_post, sinks_a, b_f_c]), pk(g_small), pk([m_g_pre, m_g_post, m_sinks_a, m_b_f_c]),
                pk([v_g_pre, v_g_post, v_sinks_a, v_b_f_c]), "adamw_small")
    sm = [_unpack_small(a)[:4] for a in sm]
    bigs = [_adamw(w, g, m, v, f"adamw_{nm}") for w, g, m, v, nm in zip(big_w, grads, big_m, big_v, names)]

    def ordered(small4, big6):
        return [small4[0], small4[1], big6[0], big6[1], small4[2], big6[2], big6[3], big6[4], small4[3], big6[5]]

    out = [loss, res["dx"][None], *ordered(g_small, grads)]
    for k in range(3):
        out += ordered(sm[k], [b[k] for b in bigs])
    return tuple(out)
```

```python
import functools
import math

import numpy as np
import jax
import jax.numpy as jnp
from jax import lax
from jax.experimental import pallas as pl
from jax.experimental.pallas import tpu as pltpu

F32 = jnp.float32
BF16 = jnp.bfloat16

D_MODEL = 2048
DEPTH = 4
N_HEADS = 32
HEAD_DIM = 64
LANES = 128
N_PAIRS = N_HEADS * HEAD_DIM // LANES
BRANCH = N_HEADS * HEAD_DIM
N_KV_A = 4
KV_A = N_KV_A * HEAD_DIM
WINDOW = 128
NORM_EPS = 1e-6
NEG = -1e30
Q_SCALE = HEAD_DIM ** -0.5

A_QKV = BRANCH + 2 * KV_A
B_QKV = 3 * BRANCH

ADAM_LR = 0.001
ADAM_B1 = 0.9
ADAM_B2 = 0.999
ADAM_EPS = 1e-08
ADAM_WD = 0.01
ADAM_STEP = 10

MESH = pl.DeviceIdType.MESH

_NT = (((1,), (1,)), ((), ()))
_TN = (((0,), (0,)), ((), ()))


def _params(sem=None):
    return pltpu.CompilerParams(dimension_semantics=sem)


def _matmul(a, b, *, mode, out_dtype, name, n=None, b_off=0, tm=1024, tn=1024, tk=2048, col_blocks=None, exchange=None):
    (m, k), nn = a.shape, ((n or b.shape[1]) if mode == "nn" else b.shape[0])
    tm, tn, tk = min(tm, m), min(tn, nn), min(tk, k)
    assert m % tm == 0 and nn % tn == 0 and k % tk == 0, (name, m, nn, k, tm, tn, tk)
    nk = k // tk

    def body(a_ref, b_ref, o_ref, acc_ref):
        kk = pl.program_id(2)
        if mode == "nn":
            p = jnp.dot(a_ref[...], b_ref[...], preferred_element_type=F32)
        else:
            p = lax.dot_general(a_ref[...], b_ref[...], _NT, preferred_element_type=F32)
        if nk == 1:
            o_ref[...] = p.astype(o_ref.dtype).reshape(o_ref.shape)
        else:
            @pl.when(kk == 0)
            def _():
                acc_ref[...] = p

            @pl.when(kk > 0)
            def _():
                acc_ref[...] += p

            @pl.when(kk == nk - 1)
            def _():
                o_ref[...] = acc_ref[...].astype(o_ref.dtype).reshape(o_ref.shape)

    if mode == "nn":
        in_specs = [pl.BlockSpec((tm, tk), lambda i, j, kk: (i, kk)),
                    pl.BlockSpec((tk, tn), lambda i, j, kk: (kk, j + b_off))]
    else:
        in_specs = [pl.BlockSpec((tm, tk), lambda i, j, kk: (i, kk)),
                    pl.BlockSpec((tn, tk), lambda i, j, kk: (j, kk))]
    if col_blocks is None:
        out_spec = pl.BlockSpec((tm, tn), lambda i, j, kk: (i, j))
        out_shape = jax.ShapeDtypeStruct((m, nn), out_dtype)
    else:
        per = nn // col_blocks // tn
        assert per * tn * col_blocks == nn, (name, nn, tn, col_blocks)
        out_spec = pl.BlockSpec((1, tm, tn), lambda i, j, kk: (j // per, i, j % per))
        out_shape = jax.ShapeDtypeStruct((col_blocks, m, nn // col_blocks), out_dtype)
    (res,), arrived = _grid_call(
        body, name=name, grid=(m // tm, nn // tn, nk), in_specs=in_specs, out_specs=[out_spec], out_shape=[out_shape],
        args=(a, b), scratch_shapes=[pltpu.VMEM((tm, tn), F32)], semantics=("parallel", "parallel", "arbitrary"),
        exchange=exchange)
    return res if exchange is None else (res, arrived)


ROW_TILE = 256


def _row_call(body, name, ins, outs, *, s):
    tr = min(ROW_TILE, s)
    spec = {"row": lambda sh: pl.BlockSpec((tr, sh[1]), lambda i: (i, 0)),
            "vec": lambda sh: pl.BlockSpec((1, sh[1]), lambda i: (0, 0)),
            "col": lambda sh: pl.BlockSpec((sh[0], tr), lambda i: (0, i))}
    in_specs = [spec[kind](a.shape) for a, kind in ins]
    out_specs = [spec[kind](sh.shape) for sh, kind in outs]
    return pl.pallas_call(
        body, name=name, grid=(s // tr,), in_specs=in_specs, out_specs=out_specs,
        out_shape=[sh for sh, _ in outs],
        compiler_params=_params(("arbitrary",)),
    )(*[a for a, _ in ins])


def _rsqrt_ms(v):
    return lax.rsqrt(jnp.mean(v * v, axis=-1, keepdims=True) + NORM_EPS)


def _rmsnorm_fwd(x, g, name):
    s, d = x.shape

    def body(x_ref, g_ref, h_ref, ht_ref):
        xv = x_ref[...]
        h = xv * _rsqrt_ms(xv) * g_ref[...]
        h_ref[...] = h.astype(BF16)
        ht_ref[...] = h.T.astype(BF16)

    return _row_call(body, name, [(x, "row"), (g, "vec")],
                     [(jax.ShapeDtypeStruct((s, d), BF16), "row"), (jax.ShapeDtypeStruct((d, s), BF16), "col")], s=s)


def _gate_fwd(o, z, name):
    s, d = o.shape

    def body(o_ref, z_ref, u_ref, ut_ref):
        zv = z_ref[...]
        u = o_ref[...] * (zv * jax.nn.sigmoid(zv))
        u_ref[...] = u.astype(BF16)
        ut_ref[...] = u.T.astype(BF16)

    return _row_call(body, name, [(o, "row"), (z, "row")],
                     [(jax.ShapeDtypeStruct((s, d), BF16), "row"), (jax.ShapeDtypeStruct((d, s), BF16), "col")], s=s)


def _post_fwd(x, y, g, name):
    s, d = x.shape

    def body(x_ref, y_ref, g_ref, o_ref):
        yv = y_ref[...]
        o_ref[...] = x_ref[...] + yv * _rsqrt_ms(yv) * g_ref[...]

    return _row_call(body, name, [(x, "row"), (y, "row"), (g, "vec")],
                     [(jax.ShapeDtypeStruct((s, d), F32), "row")], s=s)[0]


def _loss_and_grad(x, target):
    s, d = x.shape

    def body(x_ref, t_ref, dx_ref, l_ref):
        err = x_ref[...] - t_ref[...]
        dx_ref[...] = err * (1.0 / d)
        part = jnp.sum(jnp.sum(err * err, axis=1, keepdims=True), axis=0, keepdims=True) * (0.5 / d)

        @pl.when(pl.program_id(0) == 0)
        def _():
            l_ref[...] = jnp.zeros_like(l_ref)

        l_ref[...] += jnp.broadcast_to(part, l_ref.shape)

    return _row_call(body, "loss_head", [(x, "row"), (target, "row")],
                     [(jax.ShapeDtypeStruct((s, d), F32), "row"),
                      (jax.ShapeDtypeStruct((1, LANES), F32), "vec")], s=s)


def _norm_bwd_rows(dn, v, g):
    r = _rsqrt_ms(v)
    a = dn * g
    dv = r * (a - v * (r * r) * jnp.mean(a * v, axis=-1, keepdims=True))
    return dv, dn * v * r


def _post_bwd(dx, y, g, name):
    s, d = dx.shape

    def body(dx_ref, y_ref, g_ref, dy_ref, dg_ref):
        dy, dg = _norm_bwd_rows(dx_ref[...], y_ref[...], g_ref[...])
        dy_ref[...] = dy.astype(BF16)

        @pl.when(pl.program_id(0) == 0)
        def _():
            dg_ref[...] = jnp.zeros_like(dg_ref)

        dg_ref[...] += jnp.sum(dg, axis=0, keepdims=True)

    return _row_call(body, name, [(dx, "row"), (y, "row"), (g, "vec")],
                     [(jax.ShapeDtypeStruct((s, d), BF16), "row"),
                      (jax.ShapeDtypeStruct((1, d), F32), "vec")], s=s)


def _gate_bwd(du, o, z, name):
    s, d = du.shape

    def body(du_ref, o_ref, z_ref, do_ref, dz_ref):
        duv, zv = du_ref[...], z_ref[...]
        sig = jax.nn.sigmoid(zv)
        do_ref[...] = (duv * (zv * sig)).astype(BF16)
        dz_ref[...] = (duv * o_ref[...] * (sig * (1.0 + zv * (1.0 - sig)))).astype(BF16)

    return _row_call(body, name, [(du, "row"), (o, "row"), (z, "row")],
                     [(jax.ShapeDtypeStruct((s, d), BF16), "row"),
                      (jax.ShapeDtypeStruct((s, d), BF16), "row")], s=s)


def _pre_bwd(dx, dhs, x, g, name):
    s, d = dx.shape
    n_dh = len(dhs)

    def body(*refs):
        dx_ref, dh_refs, (x_ref, g_ref, o_ref, dg_ref) = refs[0], refs[1:1 + n_dh], refs[1 + n_dh:]
        dh = dh_refs[0][...].astype(F32)
        for r in dh_refs[1:]:
            dh = dh + r[...].astype(F32)
        dv, dg = _norm_bwd_rows(dh, x_ref[...], g_ref[...])
        o_ref[...] = dx_ref[...] + dv

        @pl.when(pl.program_id(0) == 0)
        def _():
            dg_ref[...] = jnp.zeros_like(dg_ref)

        dg_ref[...] += jnp.sum(dg, axis=0, keepdims=True)

    return _row_call(body, name, [(dx, "row")] + [(h, "row") for h in dhs] + [(x, "row"), (g, "vec")],
                     [(jax.ShapeDtypeStruct((s, d), F32), "row"),
                      (jax.ShapeDtypeStruct((1, d), F32), "vec")], s=s)


def _lane_is_first_head():
    return lax.broadcasted_iota(jnp.int32, (1, LANES), 1) < HEAD_DIM


def _bcast_lanes(col):
    return jnp.broadcast_to(col, (col.shape[0], LANES))


def _pair_spec(s, off=0, width=LANES):
    return pl.BlockSpec((s, width), lambda p: (0, p + off))


def _stack_heads(pair, first):
    return jnp.concatenate([jnp.where(first, pair, 0), jnp.where(first, 0, pair)], axis=0).astype(BF16)


def _stacked_mask(t, strict):
    row = lax.broadcasted_iota(jnp.int32, (2 * t, t), 0)
    col = lax.broadcasted_iota(jnp.int32, (2 * t, t), 1)
    query = jnp.where(row >= t, row - t, row)
    return col < query if strict else col <= query


LOOP_UNROLL = 2


def _two_at_a_time(n, step, carry):
    def group(jj, c):
        for k in range(LOOP_UNROLL):
            c = step(LOOP_UNROLL * jj + k, c)
        return c

    carry = lax.fori_loop(0, n // LOOP_UNROLL, group, carry)
    return lax.fori_loop(LOOP_UNROLL * (n // LOOP_UNROLL), n, step, carry)


def _rowsum_heads(prod, first):
    return (jnp.sum(jnp.where(first, prod, 0.0), axis=1, keepdims=True),
            jnp.sum(jnp.where(first, 0.0, prod), axis=1, keepdims=True))


def _softplus_parts(z):
    e = jnp.exp(-jnp.abs(z))
    sp = jnp.maximum(z, 0.0) + jnp.log(1.0 + e)
    r = 1.0 / (1.0 + e)
    return sp, jnp.where(z >= 0, r, e * r)


def _split_dot(x, t):
    hi = x.astype(BF16)
    lo = (x - hi.astype(F32)).astype(BF16)
    return jnp.dot(hi, t, preferred_element_type=F32) + jnp.dot(lo, t, preferred_element_type=F32)


def _sb_tile(s):
    return min(256, s)


def _attn_b_fwd(qkv, name, exchange=None):
    s = qkv.shape[0]
    t = _sb_tile(s)
    nq = s // t

    def body(q_ref, k_ref, v_ref, o_ref, lt_ref):
        first = _lane_is_first_head()
        before = _stacked_mask(t, strict=True)
        tri = (lax.broadcasted_iota(jnp.int32, (t, t), 0) >= lax.broadcasted_iota(jnp.int32, (t, t), 1)).astype(BF16)

        def tile(j, carry, diag, qs):
            c, acc = carry
            c0 = pl.multiple_of(j * t, t)
            k2 = k_ref[pl.ds(c0, t), :]
            v2 = v_ref[pl.ds(c0, t), :]
            z = lax.dot_general(qs, k2, _NT, preferred_element_type=F32)
            sp, _ = _softplus_parts(z)
            lf = jnp.where(before, -sp, 0.0) if diag else -sp
            incl = jnp.dot(lf.astype(BF16), tri, preferred_element_type=F32)
            a = jnp.exp(z + c + incl)
            if diag:
                a = jnp.where(before, a, 0.0)
            pv = jnp.dot(a.astype(BF16), v2, preferred_element_type=F32)
            return c + incl[:, 0:1], acc + jnp.where(first, pv[:t], pv[t:])

        def qblock(i, _):
            r0 = pl.multiple_of(i * t, t)
            qs = _stack_heads(q_ref[pl.ds(r0, t), :] * Q_SCALE, first)
            carry = tile(i, (jnp.zeros((2 * t, 1), F32), jnp.zeros((t, LANES), F32)), True, qs)
            carry = _two_at_a_time(i, lambda j, c: tile(i - 1 - j, c, False, qs), carry)
            o_ref[pl.ds(r0, t), :] = carry[1]
            lt_ref[pl.ds(r0, t), 0:LANES] = _bcast_lanes(carry[0][:t])
            lt_ref[pl.ds(r0, t), LANES:2 * LANES] = _bcast_lanes(carry[0][t:])
            return 0

        lax.fori_loop(0, nq, qblock, 0)

    return _grid_call(
        body, name=name, grid=(N_PAIRS,),
        in_specs=[_pair_spec(s), _pair_spec(s, N_PAIRS), _pair_spec(s, 2 * N_PAIRS)],
        out_specs=[_pair_spec(s), _stat_spec(s)],
        out_shape=[jax.ShapeDtypeStruct((s, BRANCH), F32), jax.ShapeDtypeStruct((s, N_HEADS * LANES), F32)],
        args=(qkv, qkv, qkv), semantics=("parallel",), exchange=exchange)


def _attn_b_bwd(qkv, ltot, do, name, exchange=None):
    s = qkv.shape[0]
    t = _sb_tile(s)
    nq = s // t

    def body(q_ref, k_ref, v_ref, lt_ref, do_ref, dq_ref, dk_ref, dv_ref, dk_acc, dv_acc):
        first = _lane_is_first_head()
        before = _stacked_mask(t, strict=True)
        tri = (lax.broadcasted_iota(jnp.int32, (t, t), 0) <= lax.broadcasted_iota(jnp.int32, (t, t), 1)).astype(BF16)
        dk_acc[...] = jnp.zeros_like(dk_acc)
        dv_acc[...] = jnp.zeros_like(dv_acc)

        def tile(j, carry, diag, qs, dos, lt):
            p_l, p_g, dq_acc = carry
            c0 = pl.multiple_of(j * t, t)
            k2 = k_ref[pl.ds(c0, t), :]
            v2 = v_ref[pl.ds(c0, t), :]
            z = lax.dot_general(qs, k2, _NT, preferred_element_type=F32)
            sp, sig = _softplus_parts(z)
            lf = jnp.where(before, -sp, 0.0) if diag else -sp
            pref_l = jnp.dot(lf.astype(BF16), tri, preferred_element_type=F32)
            a = jnp.exp(z + ((lt - p_l) - pref_l + lf))
            if diag:
                a = jnp.where(before, a, 0.0)
            g = a * lax.dot_general(dos, v2, _NT, preferred_element_type=F32)
            pref_g = jnp.dot(g.astype(BF16), tri, preferred_element_type=F32)
            dz = g - sig * (p_g + pref_g)
            if diag:
                dz = jnp.where(before, dz, 0.0)
            dzb = dz.astype(BF16)
            dq = jnp.dot(dzb, k2, preferred_element_type=F32)
            dk_acc[pl.ds(c0, t), :] += lax.dot_general(dzb, qs, _TN, preferred_element_type=F32)
            dv_acc[pl.ds(c0, t), :] += lax.dot_general(a.astype(BF16), dos, _TN, preferred_element_type=F32)
            return p_l + pref_l[:, t - 1:t], p_g + pref_g[:, t - 1:t], dq_acc + jnp.where(first, dq[:t], dq[t:])

        def qblock(i, _):
            r0 = pl.multiple_of(i * t, t)
            qs = _stack_heads(q_ref[pl.ds(r0, t), :] * Q_SCALE, first)
            dos = _stack_heads(do_ref[pl.ds(r0, t), :], first)
            lt = jnp.concatenate([lt_ref[pl.ds(r0, t), 0:1], lt_ref[pl.ds(r0, t), LANES:LANES + 1]], axis=0)
            zero = jnp.zeros((2 * t, 1), F32)
            carry = (zero, zero, jnp.zeros((t, LANES), F32))
            carry = _two_at_a_time(i, lambda j, c: tile(j, c, False, qs, dos, lt), carry)
            carry = tile(i, carry, True, qs, dos, lt)
            dq_ref[pl.ds(r0, t), :] = (carry[2] * Q_SCALE).astype(BF16)
            return 0

        lax.fori_loop(0, nq, qblock, 0)
        dk_ref[...] = dk_acc[...].astype(BF16)
        dv_ref[...] = dv_acc[...].astype(BF16)

    out = jax.ShapeDtypeStruct((s, BRANCH), BF16)
    return _grid_call(
        body, name=name, grid=(N_PAIRS,),
        in_specs=[_pair_spec(s), _pair_spec(s, N_PAIRS), _pair_spec(s, 2 * N_PAIRS), _stat_spec(s), _pair_spec(s)],
        out_specs=[_pair_spec(s)] * 3, out_shape=[out] * 3,
        scratch_shapes=[pltpu.VMEM((s, LANES), F32), pltpu.VMEM((s, LANES), F32)],
        args=(qkv, qkv, qkv, ltot, do), semantics=("parallel",), exchange=exchange)


def _fox_tile(s):
    return min(256, s)


def _stat_spec(s):
    return pl.BlockSpec((s, 2 * LANES), lambda p: (0, p))


def _cum_spec(nt, t):
    return pl.BlockSpec((1, nt, 2, t), lambda p: (p, 0, 0, 0))


def _attn_c_fwd(qkv, cum4, name, exchange=None):
    s = qkv.shape[0]
    t = _fox_tile(s)
    nq = s // t

    def body(q_ref, k_ref, v_ref, c_ref, o_ref, lse_ref):
        first = _lane_is_first_head()
        causal = _stacked_mask(t, strict=False)

        def tile(j, carry, diag, qs):
            c0 = pl.multiple_of(j * t, t)
            k2 = k_ref[pl.ds(c0, t), :]
            v2 = v_ref[pl.ds(c0, t), :]
            cs = c_ref[0, j]
            m_prev, l_prev, acc = carry
            z = lax.dot_general(qs, k2, _NT, preferred_element_type=F32)
            sc = jnp.concatenate([z[:t] - cs[0:1, :], z[t:] - cs[1:2, :]], axis=0)
            if diag:
                sc = jnp.where(causal, sc, NEG)
            m_new = jnp.maximum(m_prev, jnp.max(sc, axis=1, keepdims=True))
            alpha = jnp.exp(m_prev - m_new)
            p = jnp.exp(sc - m_new)
            l_new = alpha * l_prev + jnp.sum(p, axis=1, keepdims=True)
            pv = jnp.dot(p.astype(BF16), v2, preferred_element_type=F32)
            acc = jnp.where(first, acc * alpha[:t] + pv[:t], acc * alpha[t:] + pv[t:])
            return m_new, l_new, acc

        def qblock(i, _):
            r0 = pl.multiple_of(i * t, t)
            qs = _stack_heads(q_ref[pl.ds(r0, t), :] * Q_SCALE, first)
            carry = (jnp.full((2 * t, 1), NEG, F32), jnp.zeros((2 * t, 1), F32), jnp.zeros((t, LANES), F32))
            carry = _two_at_a_time(i, lambda j, c: tile(j, c, False, qs), carry)
            m, l, acc = tile(i, carry, True, qs)
            inv = 1.0 / l
            lse = m + jnp.log(l)
            o_ref[pl.ds(r0, t), :] = acc * jnp.where(first, inv[:t], inv[t:])
            lse_ref[pl.ds(r0, t), 0:LANES] = _bcast_lanes(lse[:t])
            lse_ref[pl.ds(r0, t), LANES:2 * LANES] = _bcast_lanes(lse[t:])
            return 0

        lax.fori_loop(0, nq, qblock, 0)

    return _grid_call(
        body, name=name, grid=(N_PAIRS,),
        in_specs=[_pair_spec(s), _pair_spec(s, N_PAIRS), _pair_spec(s, 2 * N_PAIRS), _cum_spec(nq, t)],
        out_specs=[_pair_spec(s), _stat_spec(s)],
        out_shape=[jax.ShapeDtypeStruct((s, BRANCH), F32), jax.ShapeDtypeStruct((s, N_HEADS * LANES), F32)],
        args=(qkv, qkv, qkv, cum4), semantics=("parallel",), exchange=exchange)


def _attn_c_bwd(qkv, cum4, o, lse, do, name, exchange=None):
    s = qkv.shape[0]
    t = _fox_tile(s)
    nq = s // t

    def body(q_ref, k_ref, v_ref, c_ref, o_ref, lse_ref, do_ref, dq_ref, dk_ref, dv_ref, dc_ref, dk_acc, dv_acc):
        first = _lane_is_first_head()
        causal = _stacked_mask(t, strict=False)
        eye = lax.broadcasted_iota(jnp.int32, (t, t), 0) == lax.broadcasted_iota(jnp.int32, (t, t), 1)
        dk_acc[...] = jnp.zeros_like(dk_acc)
        dv_acc[...] = jnp.zeros_like(dv_acc)
        dc_ref[...] = jnp.zeros_like(dc_ref)

        def tile(j, carry, diag, qs, dos, delta, lse):
            dq_acc, rs = carry
            c0 = pl.multiple_of(j * t, t)
            k2 = k_ref[pl.ds(c0, t), :]
            v2 = v_ref[pl.ds(c0, t), :]
            cs = c_ref[0, j]
            z = lax.dot_general(qs, k2, _NT, preferred_element_type=F32)
            sc = jnp.concatenate([z[:t] - cs[0:1, :], z[t:] - cs[1:2, :]], axis=0)
            p = jnp.exp(sc - lse)
            if diag:
                p = jnp.where(causal, p, 0.0)
            ds = p * (lax.dot_general(dos, v2, _NT, preferred_element_type=F32) - delta)
            dsb = ds.astype(BF16)
            dq = jnp.dot(dsb, k2, preferred_element_type=F32)
            dk_acc[pl.ds(c0, t), :] += lax.dot_general(dsb, qs, _TN, preferred_element_type=F32)
            dv_acc[pl.ds(c0, t), :] += lax.dot_general(p.astype(BF16), dos, _TN, preferred_element_type=F32)
            col_sums = jnp.concatenate([jnp.sum(ds[:t], axis=0, keepdims=True), jnp.sum(ds[t:], axis=0, keepdims=True)], axis=0)
            dc_ref[0, j] = dc_ref[0, j] - col_sums
            return dq_acc + jnp.where(first, dq[:t], dq[t:]), rs + jnp.sum(ds, axis=1, keepdims=True)

        def qblock(i, _):
            r0 = pl.multiple_of(i * t, t)
            do2 = do_ref[pl.ds(r0, t), :]
            qs = _stack_heads(q_ref[pl.ds(r0, t), :] * Q_SCALE, first)
            dos = _stack_heads(do2, first)
            delta = jnp.concatenate(_rowsum_heads(do2.astype(F32) * o_ref[pl.ds(r0, t), :], first), axis=0)
            lse = jnp.concatenate([lse_ref[pl.ds(r0, t), 0:1], lse_ref[pl.ds(r0, t), LANES:LANES + 1]], axis=0)
            carry = (jnp.zeros((t, LANES), F32), jnp.zeros((2 * t, 1), F32))
            carry = _two_at_a_time(i, lambda j, c: tile(j, c, False, qs, dos, delta, lse), carry)
            dq_acc, rs = tile(i, carry, True, qs, dos, delta, lse)
            dq_ref[pl.ds(r0, t), :] = (dq_acc * Q_SCALE).astype(BF16)
            as_row = lambda col_vec: jnp.sum(jnp.where(eye, col_vec, 0.0), axis=0, keepdims=True)
            dc_ref[0, i] = dc_ref[0, i] + jnp.concatenate([as_row(rs[:t]), as_row(rs[t:])], axis=0)
            return 0

        lax.fori_loop(0, nq, qblock, 0)
        dk_ref[...] = dk_acc[...].astype(BF16)
        dv_ref[...] = dv_acc[...].astype(BF16)

    out = jax.ShapeDtypeStruct((s, BRANCH), BF16)
    return _grid_call(
        body, name=name, grid=(N_PAIRS,),
        in_specs=[_pair_spec(s), _pair_spec(s, N_PAIRS), _pair_spec(s, 2 * N_PAIRS), _cum_spec(nq, t),
                  _pair_spec(s), _stat_spec(s), _pair_spec(s)],
        out_specs=[_pair_spec(s)] * 3 + [_cum_spec(nq, t)],
        out_shape=[out] * 3 + [jax.ShapeDtypeStruct(cum4.shape, F32)],
        scratch_shapes=[pltpu.VMEM((s, LANES), F32), pltpu.VMEM((s, LANES), F32)],
        args=(qkv, qkv, qkv, cum4, o, lse, do), semantics=("parallel",), exchange=exchange)


FG_CHUNK = 512


def _tri_dot3(x, t):
    hi = x.astype(BF16)
    r1 = x - hi.astype(F32)
    mid = r1.astype(BF16)
    lo = (r1 - mid.astype(F32)).astype(BF16)
    return (jnp.dot(hi, t, preferred_element_type=F32) + jnp.dot(mid, t, preferred_element_type=F32)
            + jnp.dot(lo, t, preferred_element_type=F32))


def _fgate_fwd(h, wf_t, b_col, name):
    s = h.shape[0]
    c = min(FG_CHUNK, s)

    def body(h_ref, w_ref, b_ref, xf_ref, cum_ref, carry_ref):
        @pl.when(pl.program_id(0) == 0)
        def _():
            carry_ref[...] = jnp.zeros_like(carry_ref)

        xf = lax.dot_general(w_ref[...], h_ref[...], _NT, preferred_element_type=F32) + b_ref[:, 0:1]
        xf_ref[...] = xf
        logf = jnp.minimum(xf, 0.0) - jnp.log(1.0 + jnp.exp(-jnp.abs(xf)))
        row = lax.broadcasted_iota(jnp.int32, (c, c), 0)
        col = lax.broadcasted_iota(jnp.int32, (c, c), 1)
        cum = _tri_dot3(logf, (row <= col).astype(BF16)) + carry_ref[:, 0:1]
        cum_ref[...] = cum
        carry_ref[...] = _bcast_lanes(cum[:, c - 1:c])

    out = jax.ShapeDtypeStruct((N_HEADS, s), F32)
    return pl.pallas_call(
        body, name=name, grid=(s // c,),
        in_specs=[pl.BlockSpec((c, D_MODEL), lambda i: (i, 0)),
                  pl.BlockSpec((N_HEADS, D_MODEL), lambda i: (0, 0)),
                  pl.BlockSpec((N_HEADS, LANES), lambda i: (0, 0))],
        out_specs=[pl.BlockSpec((N_HEADS, c), lambda i: (0, i))] * 2,
        out_shape=[out, out],
        scratch_shapes=[pltpu.VMEM((N_HEADS, LANES), F32)],
        compiler_params=_params(("arbitrary",)),
    )(h, wf_t, b_col)


def _fgate_bwd(dcum, xf, h, wf_t, name):
    s = h.shape[0]
    c = min(FG_CHUNK, s)
    n = s // c

    def body(dc_ref, xf_ref, h_ref, w_ref, dw_ref, dh_ref, db_ref, carry_ref):
        @pl.when(pl.program_id(0) == 0)
        def _():
            carry_ref[...] = jnp.zeros_like(carry_ref)
            dw_ref[...] = jnp.zeros_like(dw_ref)
            db_ref[...] = jnp.zeros_like(db_ref)

        row = lax.broadcasted_iota(jnp.int32, (c, c), 0)
        col = lax.broadcasted_iota(jnp.int32, (c, c), 1)
        dlogf = _tri_dot3(dc_ref[...], (row >= col).astype(BF16)) + carry_ref[:, 0:1]
        carry_ref[...] = _bcast_lanes(dlogf[:, 0:1])
        xf = xf_ref[...]
        e = jnp.exp(-jnp.abs(xf))
        r = 1.0 / (1.0 + e)
        dxf = dlogf * jnp.where(xf >= 0, e * r, r)
        db_ref[...] += _bcast_lanes(jnp.sum(dxf, axis=1, keepdims=True))
        dxb = dxf.astype(BF16)
        dw_ref[...] += jnp.dot(dxb, h_ref[...], preferred_element_type=F32)
        dh_ref[...] = lax.dot_general(dxb, w_ref[...], _TN, preferred_element_type=F32)

    rev = lambda i: n - 1 - i
    return pl.pallas_call(
        body, name=name, grid=(n,),
        in_specs=[pl.BlockSpec((N_HEADS, c), lambda i: (0, rev(i))),
                  pl.BlockSpec((N_HEADS, c), lambda i: (0, rev(i))),
                  pl.BlockSpec((c, D_MODEL), lambda i: (rev(i), 0)),
                  pl.BlockSpec((N_HEADS, D_MODEL), lambda i: (0, 0))],
        out_specs=[pl.BlockSpec((N_HEADS, D_MODEL), lambda i: (0, 0)),
                   pl.BlockSpec((c, D_MODEL), lambda i: (rev(i), 0)),
                   pl.BlockSpec((N_HEADS, LANES), lambda i: (0, 0))],
        out_shape=[jax.ShapeDtypeStruct((N_HEADS, D_MODEL), F32), jax.ShapeDtypeStruct((s, D_MODEL), F32),
                   jax.ShapeDtypeStruct((N_HEADS, LANES), F32)],
        scratch_shapes=[pltpu.VMEM((N_HEADS, LANES), F32)],
        compiler_params=_params(("arbitrary",)),
    )(dcum, xf, h, wf_t)


def _to_cum4(v, t):
    s = v.shape[1]
    return v.reshape(N_PAIRS, 2, s // t, t).transpose(0, 2, 1, 3)


def _from_cum4(v4):
    p, nt, two, t = v4.shape
    return v4.transpose(0, 2, 1, 3).reshape(p * two, nt * t)


def _alibi_slopes():
    return (2.0 ** (-8.0 * np.arange(1, N_HEADS + 1, dtype=np.float32) / N_HEADS)).astype(np.float32)


def _per_head_lanes(v):
    return jnp.repeat(v.astype(F32).reshape(N_PAIRS, 1, 2), LANES, axis=2)


def _attn_a_specs(s):
    q = _pair_spec(s)
    k = pl.BlockSpec((s, LANES), lambda p: (0, N_PAIRS + p // 8))
    v = pl.BlockSpec((s, LANES), lambda p: (0, N_PAIRS + KV_A // LANES + p // 8))
    head = pl.BlockSpec((1, 1, 2 * LANES), lambda p: (p, 0, 0))
    return q, k, v, head


def _attn_a_geometry(p, slope_ref, sink_ref):
    kv_half = (p // 4) % 2
    kv_first = kv_half == 0
    lane_first = _lane_is_first_head()
    kv_lanes = (lax.broadcasted_iota(jnp.int32, (1, LANES), 1) // HEAD_DIM) == kv_half
    row = lax.broadcasted_iota(jnp.int32, (2 * WINDOW, 2 * WINDOW), 0)
    cj = lax.broadcasted_iota(jnp.int32, (2 * WINDOW, 2 * WINDOW), 1)
    second = row >= WINDOW
    dist = WINDOW + jnp.where(second, row - WINDOW, row) - cj
    valid = (dist >= 0) & (dist < WINDOW)
    per_row = lambda ref: jnp.where(second[:, 0:1], ref[0, :, LANES:LANES + 1], ref[0, :, 0:1])
    return kv_first, lane_first, kv_lanes, per_row(slope_ref) * dist.astype(F32), valid, per_row(sink_ref)


def _swap_halves(x):
    return pltpu.roll(x, HEAD_DIM, 1)


def _attn_a_fwd(qkv, slopes, sinks, name, exchange=None):
    s = qkv.shape[0]
    nb = s // WINDOW

    def body(q_ref, k_ref, v_ref, sl_ref, sk_ref, o_ref, lse_ref):
        kv_first, lane_first, kv_lanes, bias, valid, sink = _attn_a_geometry(pl.program_id(0), sl_ref, sk_ref)

        def block(r0, k0, width):
            q2 = q_ref[pl.ds(r0, WINDOW), :].astype(F32) * Q_SCALE
            q2r = _swap_halves(q2)
            xs = jnp.concatenate([jnp.where(kv_first, q2, q2r), jnp.where(kv_first, q2r, q2)], axis=0).astype(BF16)
            km = jnp.where(kv_lanes, k_ref[pl.ds(k0, width), :], 0).astype(BF16)
            vm = jnp.where(kv_lanes, v_ref[pl.ds(k0, width), :], 0).astype(BF16)
            sc = lax.dot_general(xs, km, _NT, preferred_element_type=F32) - bias[:, 2 * WINDOW - width:]
            sc = jnp.where(valid[:, 2 * WINDOW - width:], sc, NEG)
            m = jnp.maximum(jnp.max(sc, axis=1, keepdims=True), sink)
            pr = jnp.exp(sc - m)
            l = jnp.sum(pr, axis=1, keepdims=True) + jnp.exp(sink - m)
            os = jnp.dot(pr.astype(BF16), vm, preferred_element_type=F32) * (1.0 / l)
            lse = m + jnp.log(l)
            lse_ref[pl.ds(r0, WINDOW), 0:LANES] = _bcast_lanes(lse[:WINDOW])
            lse_ref[pl.ds(r0, WINDOW), LANES:2 * LANES] = _bcast_lanes(lse[WINDOW:])
            oa = jnp.where(kv_first, os[:WINDOW], _swap_halves(os[:WINDOW]))
            ob = jnp.where(kv_first, _swap_halves(os[WINDOW:]), os[WINDOW:])
            o_ref[pl.ds(r0, WINDOW), :] = jnp.where(lane_first, oa, ob)

        block(0, 0, WINDOW)

        def loop(n, _):
            r0 = pl.multiple_of(n * WINDOW, WINDOW)
            block(r0, pl.multiple_of(r0 - WINDOW, WINDOW), 2 * WINDOW)
            return 0

        _two_at_a_time(nb - 1, lambda n, c: loop(n + 1, c), 0)

    q, k, v, head = _attn_a_specs(s)
    return _grid_call(
        body, name=name, grid=(N_PAIRS,),
        in_specs=[q, k, v, head, head],
        out_specs=[_pair_spec(s), _stat_spec(s)],
        out_shape=[jax.ShapeDtypeStruct((s, BRANCH), F32), jax.ShapeDtypeStruct((s, N_HEADS * LANES), F32)],
        args=(qkv, qkv, qkv, slopes, sinks), semantics=("parallel",), exchange=exchange)


def _attn_a_bwd(qkv, slopes, sinks, o, lse, do, name, exchange=None):
    s = qkv.shape[0]
    nb = s // WINDOW

    def body(q_ref, k_ref, v_ref, sl_ref, sk_ref, o_ref, lse_ref, do_ref, dq_ref, dk_ref, dv_ref, dsk_ref):
        p_id = pl.program_id(0)
        kv_first, lane_first, kv_lanes, bias, valid, sink = _attn_a_geometry(p_id, sl_ref, sk_ref)

        @pl.when(p_id % 8 == 0)
        def _():
            dk_ref[...] = jnp.zeros_like(dk_ref)
            dv_ref[...] = jnp.zeros_like(dv_ref)

        def align(v2):
            v2r = _swap_halves(v2)
            both = jnp.concatenate([jnp.where(kv_first, v2, v2r), jnp.where(kv_first, v2r, v2)], axis=0)
            return jnp.where(kv_lanes, both, 0.0).astype(BF16)

        def block(r0, k0, width, sink_sum):
            xq = align(q_ref[pl.ds(r0, WINDOW), :].astype(F32) * Q_SCALE)
            do2 = do_ref[pl.ds(r0, WINDOW), :].astype(F32)
            xdo = align(do2)
            delta = jnp.concatenate(_rowsum_heads(do2 * o_ref[pl.ds(r0, WINDOW), :], lane_first), axis=0)
            lse = jnp.concatenate([lse_ref[pl.ds(r0, WINDOW), 0:1], lse_ref[pl.ds(r0, WINDOW), LANES:LANES + 1]], axis=0)
            km = jnp.where(kv_lanes, k_ref[pl.ds(k0, width), :], 0).astype(BF16)
            vm = jnp.where(kv_lanes, v_ref[pl.ds(k0, width), :], 0).astype(BF16)
            sc = lax.dot_general(xq, km, _NT, preferred_element_type=F32) - bias[:, 2 * WINDOW - width:]
            pr = jnp.where(valid[:, 2 * WINDOW - width:], jnp.exp(sc - lse), 0.0)
            ds = pr * (lax.dot_general(xdo, vm, _NT, preferred_element_type=F32) - delta)
            dsb = ds.astype(BF16)
            dq_al = jnp.dot(dsb, km, preferred_element_type=F32)
            dk_ref[pl.ds(k0, width), :] += lax.dot_general(dsb, xq, _TN, preferred_element_type=F32)
            dv_ref[pl.ds(k0, width), :] += lax.dot_general(pr.astype(BF16), xdo, _TN, preferred_element_type=F32)
            dqa = jnp.where(kv_first, dq_al[:WINDOW], _swap_halves(dq_al[:WINDOW]))
            dqb = jnp.where(kv_first, _swap_halves(dq_al[WINDOW:]), dq_al[WINDOW:])
            dq_ref[pl.ds(r0, WINDOW), :] = (jnp.where(lane_first, dqa, dqb) * Q_SCALE).astype(BF16)
            return sink_sum + jnp.exp(sink - lse) * delta

        sink_sum = block(0, 0, WINDOW, jnp.zeros((2 * WINDOW, 1), F32))

        def loop(n, c):
            r0 = pl.multiple_of(n * WINDOW, WINDOW)
            return block(r0, pl.multiple_of(r0 - WINDOW, WINDOW), 2 * WINDOW, c)

        sink_sum = _two_at_a_time(nb - 1, lambda n, c: loop(n + 1, c), sink_sum)
        dsk_ref[0, :, 0:LANES] = jnp.broadcast_to(-jnp.sum(sink_sum[:WINDOW], axis=0, keepdims=True), (1, LANES))
        dsk_ref[0, :, LANES:2 * LANES] = jnp.broadcast_to(-jnp.sum(sink_sum[WINDOW:], axis=0, keepdims=True), (1, LANES))

    q, k, v, head = _attn_a_specs(s)
    kv_out = pl.BlockSpec((s, LANES), lambda p: (0, p // 8))
    return _grid_call(
        body, name=name, grid=(N_PAIRS,),
        in_specs=[q, k, v, head, head, _pair_spec(s), _stat_spec(s), _pair_spec(s)],
        out_specs=[_pair_spec(s), kv_out, kv_out, head],
        out_shape=[jax.ShapeDtypeStruct((s, BRANCH), BF16), jax.ShapeDtypeStruct((s, KV_A), F32),
                   jax.ShapeDtypeStruct((s, KV_A), F32), jax.ShapeDtypeStruct((N_PAIRS, 1, 2 * LANES), F32)],
        args=(qkv, qkv, qkv, slopes, sinks, o, lse, do), semantics=("arbitrary",), exchange=exchange)


def _layer_kind(i):
    return i % 3, i // 3


GATHER_FIRST = [("in", 0)]
GATHER_BEHIND = {("qkv", 0): [("out", 0)], ("attn", 0): [("in", 1)], ("attn", 1): [("out", 1), ("in", 2), ("out", 2)],
                 ("attn", 2): [("in", 3), ("out", 3)]}


def _forward_backward(x, target, g_pre, g_post, sinks_a, b_f_c, shards, chip, core):
    s = x.shape[0]
    slopes = _per_head_lanes(jnp.asarray(_alibi_slopes()))
    w_in, w_out, wf_t = {}, {}, {}

    def lands_side_by_side(key):
        return key[0] == "in" and shards[key].shape[1] % LANES == 0

    def gather(keys):
        return _GatherExchange([shards[k] for k in keys], [lands_side_by_side(k) for k in keys])

    def deliver(keys, gathered):
        for key, g in zip(keys, gathered):
            side, layer = key
            sh = shards[key]
            if side == "out":
                g = lax.dynamic_update_slice(g, sh[None], (chip, 0, 0))
                w_out[layer] = g.reshape(4 * sh.shape[0], sh.shape[1])
            elif lands_side_by_side(key):
                w_in[layer] = _place_columns(g, sh, chip, f"own_block_in_l{layer}")
            else:
                g = lax.dynamic_update_slice(g, sh[None], (chip, 0, 0))
                w = g.transpose(1, 0, 2).reshape(sh.shape[0], 4 * sh.shape[1])
                w_in[layer], wf_t[layer] = w[:, :4 * BRANCH], w[:, 4 * BRANCH:].T

    deliver(GATHER_FIRST, _exchange_call(gather(GATHER_FIRST), "gather_first_weights"))
    saved = []
    for i in range(DEPTH):
        kind, j = _layer_kind(i)
        tag = f"l{i}"
        w = w_in[i]
        nqkv = A_QKV if kind == 0 else B_QKV
        tn = 512 if kind == 0 else 1024
        h, h_t = _rmsnorm_fwd(x, g_pre[i:i + 1], f"prenorm_{tag}")
        behind = GATHER_BEHIND.get(("qkv", i))
        qkv = _matmul(h, w, mode="nn", out_dtype=BF16, name=f"inproj_qkv_{tag}", n=nqkv, tn=tn,
                      exchange=gather(behind) if behind else None)
        if behind:
            qkv, arrived = qkv
            deliver(behind, arrived)
        z = _matmul(h, w, mode="nn", out_dtype=F32, name=f"inproj_gate_{tag}", n=BRANCH, b_off=nqkv // tn, tn=tn)
        behind = GATHER_BEHIND.get(("attn", i))
        exchange = gather(behind) if behind else None
        if kind == 0:
            sink_l = _per_head_lanes(sinks_a[j])
            (o, lse), arrived = _attn_a_fwd(qkv, slopes, sink_l, f"attn_a_fwd_{tag}", exchange)
            extra = (sink_l, lse)
        elif kind == 1:
            (o, extra), arrived = _attn_b_fwd(qkv, f"attn_b_fwd_{tag}", exchange)
        else:
            b_col = jnp.broadcast_to(b_f_c[j].astype(F32)[:, None], (N_HEADS, LANES))
            xf, cum = _fgate_fwd(h, wf_t[i], b_col, f"fgate_fwd_{tag}")
            cum4 = _to_cum4(cum, _fox_tile(s))
            (o, lse), arrived = _attn_c_fwd(qkv, cum4, f"attn_c_fwd_{tag}", exchange)
            extra = (xf, cum4, lse)
        if behind:
            deliver(behind, arrived)
        u, u_t = _gate_fwd(o, z, f"gate_{tag}")
        y = _matmul(u, w_out[i], mode="nn", out_dtype=F32, name=f"outproj_{tag}")
        saved.append((x, h, h_t, qkv, z, o, u_t, y, extra))
        x = _post_fwd(x, y, g_post[i:i + 1], f"postnorm_{tag}")

    dx, loss_part = _loss_and_grad(x, target)

    d_g_pre, d_g_post = [None] * DEPTH, [None] * DEPTH
    d_sinks = [None, None]
    d_b_f = None
    reduced = {}
    pending = None

    def finish_reduce(layer, side, own, arr):
        kind, j = _layer_kind(layer)
        reduced[(side, kind)] = _sum_chips(own, arr, core, f"shard_sum_{side}_l{layer}", j, 2 if kind == 0 else 1,
                                           into=reduced.get((side, kind)))

    for i in reversed(range(DEPTH)):
        kind, j = _layer_kind(i)
        tag = f"l{i}"
        x_in, h, h_t, qkv, z, o, u_t, y, extra = saved[i]
        tn = 512 if kind == 0 else 1024
        dy, d_g_post[i] = _post_bwd(dx, y, g_post[i:i + 1], f"postnorm_bwd_{tag}")
        dw_out = _matmul(u_t, dy, mode="nn", out_dtype=BF16, name=f"dw_out_{tag}")
        dw_out = dw_out.reshape(4, dw_out.shape[0] // 4, dw_out.shape[1])
        du, (their_out,) = _matmul(dy, w_out[i], mode="nt", out_dtype=F32, name=f"d_gated_{tag}",
                                   exchange=_SiblingExchange([dw_out]))
        sum_out = _add_pairs(dw_out, their_out, f"chip_sum_out_{tag}")
        do, dz = _gate_bwd(du, o, z, f"gate_bwd_{tag}")
        dhs = []
        exchange = _ScatterExchange([pending[1]]) if pending else None
        if kind == 0:
            sink_l, lse = extra
            (dq, dk, dv, dsk), arrived = _attn_a_bwd(qkv, slopes, sink_l, o, lse, do, f"attn_a_bwd_{tag}", exchange)
            d_sinks[j] = dsk[:, 0, ::LANES].reshape(N_HEADS)
            parts = [dq, dk.astype(BF16), dv.astype(BF16), dz]
        elif kind == 1:
            (dq, dk, dv), arrived = _attn_b_bwd(qkv, extra, do, f"attn_b_bwd_{tag}", exchange)
            parts = [dq, dk, dv, dz]
        else:
            xf, cum4, lse = extra
            (dq, dk, dv, dcum4), arrived = _attn_c_bwd(qkv, cum4, o, lse, do, f"attn_c_bwd_{tag}", exchange)
            d_wf_t, dh_f, db = _fgate_bwd(_from_cum4(dcum4), xf, h, wf_t[i], f"fgate_bwd_{tag}")
            d_b_f = db[:, 0]
            dhs.append(dh_f)
            parts = [dq, dk, dv, dz]
        if pending:
            finish_reduce(pending[0], "in", pending[1], arrived[0])
        dproj = jnp.concatenate(parts, axis=1)
        scatter_out = _ScatterExchange([sum_out])
        if kind == 2:
            dw_in, arrived = _matmul(h_t, dproj, mode="nn", out_dtype=F32, name=f"dw_in_{tag}", tn=tn, exchange=scatter_out)
            dw_in = jnp.concatenate([dw_in, d_wf_t.T], axis=1)
            dw_in = dw_in.reshape(dw_in.shape[0], 4, dw_in.shape[1] // 4).transpose(1, 0, 2).astype(BF16)
        else:
            dw_in, arrived = _matmul(h_t, dproj, mode="nn", out_dtype=BF16, name=f"dw_in_{tag}", col_blocks=4,
                                     tn=1152 if kind == 0 else 1024, exchange=scatter_out)
        finish_reduce(i, "out", sum_out, arrived[0])
        dh, (their_in,) = _matmul(dproj, w_in[i], mode="nt", out_dtype=F32, name=f"dh_{tag}",
                                  tk=1536 if kind == 0 else 2048, exchange=_SiblingExchange([dw_in]))
        dhs.insert(0, dh)
        dx, d_g_pre[i] = _pre_bwd(dx, dhs, x_in, g_pre[i:i + 1], f"prenorm_bwd_{tag}")
        pending = (i, _add_pairs(dw_in, their_in, f"chip_sum_in_{tag}"))

    arrived = _exchange_call(_ScatterExchange([pending[1]]), "grad_chip_scatter_last")
    finish_reduce(pending[0], "in", pending[1], arrived[0])

    return dict(loss=loss_part, dx=dx, g_pre=jnp.concatenate(d_g_pre, axis=0), g_post=jnp.concatenate(d_g_post, axis=0),
                sinks_a=jnp.stack(d_sinks), b_f_c=d_b_f[None, :], reduced=reduced)


def _place():
    x, y, c = lax.axis_index("x"), lax.axis_index("y"), lax.axis_index("c")
    others = [(1 - x, y), (x, 1 - y), (1 - x, 1 - y)]
    return x, y, c, others


def _half_rows(ref_rows, which):
    half = ref_rows // 2
    return pl.ds(pl.multiple_of(which * half, half), half)


def _remote(src, dst, sems, k, device):
    send, recv = sems
    return pltpu.make_async_remote_copy(src_ref=src, dst_ref=dst, send_sem=send.at[k], recv_sem=recv.at[k],
                                        device_id=device, device_id_type=MESH)


def _hbm_call(body, name, ins, out_shapes, n_remote, aliases=None):
    any_spec = pl.BlockSpec(memory_space=pl.ANY)
    return pl.pallas_call(
        body, name=name, in_specs=[any_spec] * len(ins), out_specs=[any_spec] * len(out_shapes),
        out_shape=out_shapes, input_output_aliases=aliases or {},
        scratch_shapes=[pltpu.SemaphoreType.DMA((n_remote,)), pltpu.SemaphoreType.DMA((n_remote,))],
    )(*ins)


class _GatherExchange:
    def __init__(self, shards, side_by_side):
        self.ins = list(shards)
        self.side_by_side = list(side_by_side)
        self.out_shapes = [jax.ShapeDtypeStruct((a.shape[0], 4 * a.shape[1]) if wide else (4,) + a.shape, a.dtype)
                           for a, wide in zip(shards, side_by_side)]
        self.n_sems = 6 * len(shards)
        self.aliases = {}

    def _copies(self, ins, outs, sems):
        x, y, c, others = _place()
        me = 2 * x + y
        table = []
        for w, (src, dst, wide) in enumerate(zip(ins, outs, self.side_by_side)):
            rows, cols = src.shape
            mine, theirs = _half_rows(rows, c), _half_rows(rows, 1 - c)

            def slot(chip, which, dst=dst, wide=wide, cols=cols):
                return dst.at[which, pl.ds(pl.multiple_of(chip * cols, LANES), cols)] if wide else dst.at[chip, which]

            for j, (px, py) in enumerate(others):
                there = 2 * px + py
                send = _remote(src.at[mine], slot(me, mine), sems, 6 * w + j, (px, py, c))
                landed = _remote(slot(there, mine), slot(there, mine), sems, 6 * w + j, (px, py, c))
                passed = _remote(slot(there, mine), slot(there, mine), sems, 6 * w + 3 + j, (x, y, 1 - c))
                from_sibling = _remote(slot(there, theirs), slot(there, theirs), sems, 6 * w + 3 + j, (x, y, 1 - c))
                table.append((send, landed, passed, from_sibling))
        return table

    def start(self, ins, outs, sems):
        for send, _, _, _ in self._copies(ins, outs, sems):
            send.start()

    def mid(self, ins, outs, sems):
        for _, landed, passed, _ in self._copies(ins, outs, sems):
            landed.wait_recv()
            passed.start()

    def finish(self, ins, outs, sems):
        table = self._copies(ins, outs, sems)
        for _, _, _, from_sibling in table:
            from_sibling.wait_recv()
        for send, _, passed, _ in table:
            send.wait_send()
            passed.wait_send()


def _place_columns(wide, block, chip, name):
    rows, cc = block.shape
    tr = min(512, rows)

    def body(c_ref, b_ref, w_ref, o_ref):
        o_ref[...] = b_ref[...]

    return pl.pallas_call(
        body, name=name,
        grid_spec=pltpu.PrefetchScalarGridSpec(
            num_scalar_prefetch=1, grid=(rows // tr,),
            in_specs=[pl.BlockSpec((tr, cc), lambda r, c_ref: (r, 0)), pl.BlockSpec(memory_space=pl.ANY)],
            out_specs=pl.BlockSpec((tr, cc), lambda r, c_ref: (r, c_ref[0]))),
        out_shape=jax.ShapeDtypeStruct(wide.shape, wide.dtype), input_output_aliases={2: 0},
        compiler_params=_params(("parallel",)),
    )(chip.astype(jnp.int32).reshape(1), block, wide)


def _exchange_call(ex, name):
    n_in, n_out = len(ex.ins), len(ex.out_shapes)

    def body(*refs):
        ins, outs, sems = refs[:n_in], refs[n_in:n_in + n_out], refs[n_in + n_out:]
        ex.start(ins, outs, sems)
        ex.mid(ins, outs, sems)
        ex.finish(ins, outs, sems)

    return _hbm_call(body, name, ex.ins, ex.out_shapes, ex.n_sems, aliases=ex.aliases)


def _grid_call(body, *, name, grid, in_specs, out_specs, out_shape, args, scratch_shapes=(), semantics, exchange=None):
    if exchange is None:
        res = pl.pallas_call(body, name=name, grid=grid, in_specs=list(in_specs), out_specs=list(out_specs),
                             out_shape=list(out_shape), scratch_shapes=list(scratch_shapes),
                             compiler_params=_params(semantics))(*args)
        return res, []
    n_in, n_out, n_scr = len(args), len(out_shape), len(scratch_shapes)
    x_in, x_out = len(exchange.ins), len(exchange.out_shapes)
    steps = math.prod(grid)

    def wrapped(*refs):
        core_in, ex_in = refs[:n_in], refs[n_in:n_in + x_in]
        rest = refs[n_in + x_in:]
        core_out, ex_out = rest[:n_out], rest[n_out:n_out + x_out]
        scratch, sems = rest[n_out + x_out:n_out + x_out + n_scr], rest[n_out + x_out + n_scr:]
        step = 0
        for axis, extent in enumerate(grid):
            step = step * extent + pl.program_id(axis)

        @pl.when(step == 0)
        def _():
            exchange.start(ex_in, ex_out, sems)

        body(*core_in, *core_out, *scratch)

        @pl.when(step == max((3 * steps) // 4 - 1, 0))
        def _():
            exchange.mid(ex_in, ex_out, sems)

        @pl.when(step == steps - 1)
        def _():
            exchange.finish(ex_in, ex_out, sems)

    any_spec = pl.BlockSpec(memory_space=pl.ANY)
    res = pl.pallas_call(
        wrapped, name=name, grid=grid,
        in_specs=list(in_specs) + [any_spec] * x_in, out_specs=list(out_specs) + [any_spec] * x_out,
        out_shape=list(out_shape) + list(exchange.out_shapes),
        input_output_aliases={n_in + a: n_out + b for a, b in exchange.aliases.items()},
        scratch_shapes=list(scratch_shapes) + [pltpu.SemaphoreType.DMA((exchange.n_sems,)),
                                               pltpu.SemaphoreType.DMA((exchange.n_sems,))],
        compiler_params=_params(("arbitrary",) * len(grid)),
    )(*args, *exchange.ins)
    return res[:n_out], res[n_out:]


class _SiblingExchange:
    def __init__(self, parts):
        self.ins = list(parts)
        self.out_shapes = [jax.ShapeDtypeStruct((4, a.shape[1] // 2, a.shape[2]), a.dtype) for a in parts]
        self.n_sems = len(parts)
        self.aliases = {}

    def _copies(self, ins, outs, sems):
        x, y, c, _ = _place()
        return [_remote(src.at[:, _half_rows(src.shape[1], 1 - c)], dst, sems, w, (x, y, 1 - c))
                for w, (src, dst) in enumerate(zip(ins, outs))]

    def start(self, ins, outs, sems):
        for cp in self._copies(ins, outs, sems):
            cp.start()

    def mid(self, ins, outs, sems):
        pass

    def finish(self, ins, outs, sems):
        for cp in self._copies(ins, outs, sems):
            cp.wait_recv()
            cp.wait_send()


class _ScatterExchange:
    def __init__(self, sums):
        self.ins = list(sums)
        self.out_shapes = [jax.ShapeDtypeStruct(a.shape, a.dtype) for a in sums]
        self.n_sems = 3 * len(sums)
        self.aliases = {}

    def _copies(self, ins, outs, sems):
        x, y, c, others = _place()
        me = 2 * x + y
        table = []
        for w, (src, dst) in enumerate(zip(ins, outs)):
            for j, (px, py) in enumerate(others):
                there = 2 * px + py
                send = _remote(src.at[there], dst.at[me], sems, 3 * w + j, (px, py, c))
                landed = _remote(dst.at[there], dst.at[there], sems, 3 * w + j, (px, py, c))
                table.append((send, landed))
        return table

    def start(self, ins, outs, sems):
        for send, _ in self._copies(ins, outs, sems):
            send.start()

    def mid(self, ins, outs, sems):
        pass

    def finish(self, ins, outs, sems):
        table = self._copies(ins, outs, sems)
        for _, landed in table:
            landed.wait_recv()
        for send, _ in table:
            send.wait_send()


def _sibling_join(shards):
    n = len(shards)

    def body(*refs):
        ins, outs, sems = refs[:n], refs[n:2 * n], refs[2 * n:2 * n + 2]
        x, y, c, _ = _place()
        pend = []
        for w in range(n):
            rows = ins[w].shape[1]
            mine, theirs = _half_rows(rows, c), _half_rows(rows, 1 - c)
            cp = _remote(ins[w].at[:, mine], outs[w].at[:, mine], sems, w, (x, y, 1 - c))
            cp.start()
            pend.append((cp, _remote(ins[w].at[:, theirs], outs[w].at[:, theirs], sems, w, (x, y, 1 - c))))
        for cp, landed in pend:
            landed.wait_recv()
            cp.wait_send()

    out_shapes = [jax.ShapeDtypeStruct(a.shape, a.dtype) for a in shards]
    return _hbm_call(body, "grad_sibling_join", shards, out_shapes, n, aliases={w: w for w in range(n)})


SMALL_ROWS = 136


def _all_reduce_small(vec):
    def body(v_ref, o_ref, buf, send, recv, loc):
        x, y, c, _ = _place()
        me = 4 * x + 2 * y + c
        lc = pltpu.make_async_copy(v_ref, buf.at[me], loc.at[0])
        lc.start()
        cps = []
        for k in range(1, 8):
            fx, fy, fc = (k >> 2) & 1, (k >> 1) & 1, k & 1
            peer = (x ^ fx, y ^ fy, c ^ fc)
            cp = pltpu.make_async_remote_copy(src_ref=v_ref, dst_ref=buf.at[me], send_sem=send.at[k - 1],
                                              recv_sem=recv.at[k - 1], device_id=peer, device_id_type=MESH)
            cp.start()
            cps.append((cp, 4 * peer[0] + 2 * peer[1] + peer[2]))
        for k, (cp, src) in enumerate(cps):
            pltpu.make_async_remote_copy(src_ref=v_ref, dst_ref=buf.at[src], send_sem=send.at[k], recv_sem=recv.at[k],
                                         device_id=(x, y, c), device_id_type=MESH).wait_recv()
        for cp, _ in cps:
            cp.wait_send()
        lc.wait()
        total = buf[0]
        for k in range(1, 8):
            total = total + buf[k]
        o_ref[...] = total

    vm = pl.BlockSpec(memory_space=pltpu.VMEM)
    return pl.pallas_call(
        body, name="all_reduce_small", in_specs=[vm], out_specs=vm,
        out_shape=jax.ShapeDtypeStruct(vec.shape, F32),
        scratch_shapes=[pltpu.VMEM((8,) + vec.shape, F32), pltpu.SemaphoreType.DMA((7,)),
                        pltpu.SemaphoreType.DMA((7,)), pltpu.SemaphoreType.DMA((1,))],
    )(vec)


SUM_ROWS = 256


def _add_pairs(part, theirs, name):
    four, rh, cc = theirs.shape
    tr = min(SUM_ROWS, rh)
    halves = part.reshape(four, 2, rh, cc)

    def body(a_ref, b_ref, o_ref):
        mine = a_ref[0, lax.axis_index("c")]
        o_ref[0] = (mine.astype(F32) + b_ref[0].astype(F32)).astype(o_ref.dtype)

    spec = pl.BlockSpec((1, tr, cc), lambda k, r: (k, r, 0))
    return pl.pallas_call(
        body, name=name, grid=(four, rh // tr),
        in_specs=[pl.BlockSpec((1, 2, tr, cc), lambda k, r: (k, 0, r, 0)), spec], out_specs=spec,
        out_shape=jax.ShapeDtypeStruct(theirs.shape, theirs.dtype),
        compiler_params=_params(("parallel", "parallel")),
    )(halves, theirs)


def _sum_chips(own, arrived, core, name, layer, n_layers, into=None):
    four, rh, cc = own.shape
    tr = min(SUM_ROWS, rh)
    nr = rh // tr

    def body(c_ref, own_ref, arr_ref, *rest):
        o_ref = rest[-1]
        x, y = lax.axis_index("x"), lax.axis_index("y")
        tot = own_ref[2 * x + y].astype(F32)
        for px, py in ((1 - x, y), (x, 1 - y), (1 - x, 1 - y)):
            tot = tot + arr_ref[2 * px + py].astype(F32)
        o_ref[0] = tot

    blk = pl.BlockSpec((4, tr, cc), lambda r, c_ref: (0, r, 0))
    in_specs, args, aliases = [blk, blk], [core, own, arrived], {}
    if into is not None:
        in_specs.append(pl.BlockSpec(memory_space=pl.ANY))
        args.append(into)
        aliases = {3: 0}
    return pl.pallas_call(
        body, name=name,
        grid_spec=pltpu.PrefetchScalarGridSpec(
            num_scalar_prefetch=1, grid=(nr,), in_specs=in_specs,
            out_specs=pl.BlockSpec((1, tr, cc), lambda r, c_ref: (layer, c_ref[0] * nr + r, 0))),
        out_shape=jax.ShapeDtypeStruct((n_layers, 2 * rh, cc), F32), input_output_aliases=aliases,
        compiler_params=_params(("parallel",)),
    )(*args)


ADAM_ROWS = 256


def _adamw(w, g, m, v, name):
    shape = w.shape
    as3 = lambda a: a.reshape((-1,) + shape[-2:])
    layers, rows, cc = as3(w).shape
    by_rows = rows % min(ADAM_ROWS, rows) == 0
    tr, tc = (min(ADAM_ROWS, rows), cc) if by_rows else (rows, ADAM_ROWS)
    assert rows % tr == 0 and cc % tc == 0
    c1 = 1.0 - ADAM_B1 ** ADAM_STEP
    c2 = 1.0 - ADAM_B2 ** ADAM_STEP

    def body(w_ref, g_ref, m_ref, v_ref, d_ref, nm_ref, nv_ref):
        gv = g_ref[...]
        nm = ADAM_B1 * m_ref[...] + (1.0 - ADAM_B1) * gv
        nv = ADAM_B2 * v_ref[...] + (1.0 - ADAM_B2) * (gv * gv)
        nm_ref[...] = nm
        nv_ref[...] = nv
        d_ref[...] = -ADAM_LR * ((nm / c1) / (jnp.sqrt(nv / c2) + ADAM_EPS) + ADAM_WD * w_ref[...])

    spec = pl.BlockSpec((1, tr, tc), (lambda l, i: (l, i, 0)) if by_rows else (lambda l, i: (l, 0, i)))
    sh = jax.ShapeDtypeStruct((layers, rows, cc), F32)
    outs = pl.pallas_call(
        body, name=name, grid=(layers, (rows // tr) * (cc // tc)), in_specs=[spec] * 4, out_specs=[spec] * 3,
        out_shape=[sh] * 3,
        compiler_params=_params(("parallel", "parallel")),
    )(as3(w), as3(g), as3(m), as3(v))
    return [o.reshape(shape) for o in outs]


def _pack_small(g_pre, g_post, sinks_a, b_f_c, loss_row):
    pad = lambda a: jnp.pad(a.reshape(1, -1).astype(F32), ((0, 0), (0, LANES - a.size)))
    rows = [g_pre.astype(F32).reshape(-1, LANES), g_post.astype(F32).reshape(-1, LANES), pad(sinks_a), pad(b_f_c), loss_row]
    packed = jnp.concatenate(rows, axis=0)
    return jnp.pad(packed, ((0, SMALL_ROWS - packed.shape[0]), (0, 0)))


def _unpack_small(p):
    n = DEPTH * D_MODEL // LANES
    return (p[:n].reshape(DEPTH, D_MODEL), p[n:2 * n].reshape(DEPTH, D_MODEL), p[2 * n, :2 * N_HEADS].reshape(2, N_HEADS),
            p[2 * n + 1, :N_HEADS].reshape(1, N_HEADS), p[2 * n + 2, 0])


def kernel(x, g_pre, g_post, w_in_a, w_out_a, sinks_a, w_in_b, w_out_b, w_in_c, b_f_c, w_out_c, loss_target, m_g_pre, m_g_post, m_w_in_a, m_w_out_a, m_sinks_a, m_w_in_b, m_w_out_b, m_w_in_c, m_b_f_c, m_w_out_c, v_g_pre, v_g_post, v_w_in_a, v_w_out_a, v_sinks_a, v_w_in_b, v_w_out_b, v_w_in_c, v_b_f_c, v_w_out_c):
    big_w = [w_in_a, w_out_a, w_in_b, w_out_b, w_in_c, w_out_c]
    big_m = [m_w_in_a, m_w_out_a, m_w_in_b, m_w_out_b, m_w_in_c, m_w_out_c]
    big_v = [v_w_in_a, v_w_out_a, v_w_in_b, v_w_out_b, v_w_in_c, v_w_out_c]

    chip = 2 * lax.axis_index("x") + lax.axis_index("y")
    core = lax.axis_index("c").astype(jnp.int32).reshape(1)
    by_kind = {0: (w_in_a, w_out_a), 1: (w_in_b, w_out_b), 2: (w_in_c, w_out_c)}
    shards = {}
    for i in range(DEPTH):
        kind, j = _layer_kind(i)
        shards[("in", i)] = by_kind[kind][0][j].astype(BF16)
        shards[("out", i)] = by_kind[kind][1][j].astype(BF16)

    res = _forward_backward(x[0], loss_target[0], g_pre, g_post, sinks_a, b_f_c, shards, chip, core)
    names = ["w_in_a", "w_out_a", "w_in_b", "w_out_b", "w_in_c", "w_out_c"]
    grads = _sibling_join([res["reduced"][(side, kind)] for kind in range(3) for side in ("in", "out")])

    small = _unpack_small(_all_reduce_small(
        _pack_small(res["g_pre"], res["g_post"], res["sinks_a"], res["b_f_c"], res["loss"])))
    g_small, loss = small[:4], small[4]

    zero_row = jnp.zeros((1, LANES), F32)
    pk = lambda a: _pack_small(a[0], a[1], a[2], a[3], zero_row)
    sm = _adamw(pk([g_pre, g_post, sinks_a, b_f_c]), pk(g_small), pk([m_g_pre, m_g_post, m_sinks_a, m_b_f_c]),
                pk([v_g_pre, v_g_post, v_sinks_a, v_b_f_c]), "adamw_small")
    sm = [_unpack_small(a)[:4] for a in sm]
    turned = lambda a: jnp.swapaxes(a, 1, 2)
    grads = list(grads)
    g_c = lax.optimization_barrier(turned(grads[4]))
    grads[4] = turned(g_c)
    bigs = [[turned(o) for o in _adamw(turned(w), g_c, turned(m), turned(v), f"adamw_{nm}")] if nm == "w_in_c"
            else _adamw(w, g, m, v, f"adamw_{nm}") for w, g, m, v, nm in zip(big_w, grads, big_m, big_v, names)]

    def ordered(small4, big6):
        return [small4[0], small4[1], big6[0], big6[1], small4[2], big6[2], big6[3], big6[4], small4[3], big6[5]]

    out = [loss, res["dx"][None], *ordered(g_small, grads)]
    for k in range(3):
        out += ordered(sm[k], [b[k] for b in bigs])
    return tuple(out)
```

```python
import functools
import math

import numpy as np
import jax
import jax.numpy as jnp
from jax import lax
from jax.experimental import pallas as pl
from jax.experimental.pallas import tpu as pltpu

F32 = jnp.float32
BF16 = jnp.bfloat16

D_MODEL = 2048
DEPTH = 4
N_HEADS = 32
HEAD_DIM = 64
LANES = 128
N_PAIRS = N_HEADS * HEAD_DIM // LANES
BRANCH = N_HEADS * HEAD_DIM
N_KV_A = 4
KV_A = N_KV_A * HEAD_DIM
WINDOW = 128
NORM_EPS = 1e-6
NEG = -1e30
Q_SCALE = HEAD_DIM ** -0.5

A_QKV = BRANCH + 2 * KV_A
B_QKV = 3 * BRANCH

ADAM_LR = 0.001
ADAM_B1 = 0.9
ADAM_B2 = 0.999
ADAM_EPS = 1e-08
ADAM_WD = 0.01
ADAM_STEP = 10

MESH = pl.DeviceIdType.MESH

_NT = (((1,), (1,)), ((), ()))
_TN = (((0,), (0,)), ((), ()))


def _params(sem=None):
    return pltpu.CompilerParams(dimension_semantics=sem)


def _matmul(a, b, *, mode, out_dtype, name, n=None, b_off=0, tm=1024, tn=1024, tk=2048, col_blocks=None, exchange=None):
    (m, k), nn = a.shape, ((n or b.shape[1]) if mode == "nn" else b.shape[0])
    tm, tn, tk = min(tm, m), min(tn, nn), min(tk, k)
    assert m % tm == 0 and nn % tn == 0 and k % tk == 0, (name, m, nn, k, tm, tn, tk)
    nk = k // tk

    def body(a_ref, b_ref, o_ref, acc_ref):
        kk = pl.program_id(2)
        if mode == "nn":
            p = jnp.dot(a_ref[...], b_ref[...], preferred_element_type=F32)
        else:
            p = lax.dot_general(a_ref[...], b_ref[...], _NT, preferred_element_type=F32)
        if nk == 1:
            o_ref[...] = p.astype(o_ref.dtype).reshape(o_ref.shape)
        else:
            @pl.when(kk == 0)
            def _():
                acc_ref[...] = p

            @pl.when(kk > 0)
            def _():
                acc_ref[...] += p

            @pl.when(kk == nk - 1)
            def _():
                o_ref[...] = acc_ref[...].astype(o_ref.dtype).reshape(o_ref.shape)

    if mode == "nn":
        in_specs = [pl.BlockSpec((tm, tk), lambda i, j, kk: (i, kk)),
                    pl.BlockSpec((tk, tn), lambda i, j, kk: (kk, j + b_off))]
    else:
        in_specs = [pl.BlockSpec((tm, tk), lambda i, j, kk: (i, kk)),
                    pl.BlockSpec((tn, tk), lambda i, j, kk: (j, kk))]
    if col_blocks is None:
        out_spec = pl.BlockSpec((tm, tn), lambda i, j, kk: (i, j))
        out_shape = jax.ShapeDtypeStruct((m, nn), out_dtype)
    else:
        per = nn // col_blocks // tn
        assert per * tn * col_blocks == nn, (name, nn, tn, col_blocks)
        out_spec = pl.BlockSpec((1, tm, tn), lambda i, j, kk: (j // per, i, j % per))
        out_shape = jax.ShapeDtypeStruct((col_blocks, m, nn // col_blocks), out_dtype)
    (res,), arrived = _grid_call(
        body, name=name, grid=(m // tm, nn // tn, nk), in_specs=in_specs, out_specs=[out_spec], out_shape=[out_shape],
        args=(a, b), scratch_shapes=[pltpu.VMEM((tm, tn), F32)], semantics=("parallel", "parallel", "arbitrary"),
        exchange=exchange)
    return res if exchange is None else (res, arrived)


ROW_TILE = 256


def _row_call(body, name, ins, outs, *, s):
    tr = min(ROW_TILE, s)
    spec = {"row": lambda sh: pl.BlockSpec((tr, sh[1]), lambda i: (i, 0)),
            "vec": lambda sh: pl.BlockSpec((1, sh[1]), lambda i: (0, 0)),
            "col": lambda sh: pl.BlockSpec((sh[0], tr), lambda i: (0, i))}
    in_specs = [spec[kind](a.shape) for a, kind in ins]
    out_specs = [spec[kind](sh.shape) for sh, kind in outs]
    return pl.pallas_call(
        body, name=name, grid=(s // tr,), in_specs=in_specs, out_specs=out_specs,
        out_shape=[sh for sh, _ in outs],
        compiler_params=_params(("arbitrary",)),
    )(*[a for a, _ in ins])


def _rsqrt_ms(v):
    return lax.rsqrt(jnp.mean(v * v, axis=-1, keepdims=True) + NORM_EPS)


def _rmsnorm_fwd(x, g, name):
    s, d = x.shape

    def body(x_ref, g_ref, h_ref, ht_ref):
        xv = x_ref[...]
        h = xv * _rsqrt_ms(xv) * g_ref[...]
        h_ref[...] = h.astype(BF16)
        ht_ref[...] = h.T.astype(BF16)

    return _row_call(body, name, [(x, "row"), (g, "vec")],
                     [(jax.ShapeDtypeStruct((s, d), BF16), "row"), (jax.ShapeDtypeStruct((d, s), BF16), "col")], s=s)


PROJ_ROWS = 256


def _resident(shape):
    return pl.BlockSpec(shape, lambda i: (0,) * len(shape), pipeline_mode=pl.Buffered(1))


def _gated_out_proj(o, z, w_out, x, g, name):
    s, d = x.shape
    tm = min(PROJ_ROWS, s)

    def body(o_ref, z_ref, w_ref, x_ref, g_ref, xn_ref, y_ref, ut_ref):
        zv = z_ref[...]
        u = o_ref[...] * (zv * jax.nn.sigmoid(zv))
        ut_ref[...] = u.T.astype(BF16)
        y = jnp.dot(u.astype(BF16), w_ref[...], preferred_element_type=F32)
        y_ref[...] = y
        xn_ref[...] = x_ref[...] + y * _rsqrt_ms(y) * g_ref[...]

    row = pl.BlockSpec((tm, d), lambda i: (i, 0))
    return pl.pallas_call(
        body, name=name, grid=(s // tm,),
        in_specs=[row, row, _resident(w_out.shape), row, _resident((1, d))],
        out_specs=[row, row, pl.BlockSpec((d, tm), lambda i: (0, i))],
        out_shape=[jax.ShapeDtypeStruct((s, d), F32), jax.ShapeDtypeStruct((s, d), F32), jax.ShapeDtypeStruct((d, s), BF16)],
        compiler_params=_params(("parallel",)),
    )(o, z, w_out, x, g)


def _gated_out_proj_bwd(dx, y, g, w_out, o, z, name, exchange=None):
    s, d = dx.shape
    tm = min(PROJ_ROWS, s)

    def body(dx_ref, y_ref, g_ref, w_ref, o_ref, z_ref, dy_ref, dg_ref, do_ref, dz_ref):
        dy, dg = _norm_bwd_rows(dx_ref[...], y_ref[...], g_ref[...])
        dyb = dy.astype(BF16)
        dy_ref[...] = dyb

        @pl.when(pl.program_id(0) == 0)
        def _():
            dg_ref[...] = jnp.zeros_like(dg_ref)

        dg_ref[...] += jnp.sum(dg, axis=0, keepdims=True)
        du = lax.dot_general(dyb, w_ref[...], _NT, preferred_element_type=F32)
        zv = z_ref[...]
        sig = jax.nn.sigmoid(zv)
        do_ref[...] = (du * (zv * sig)).astype(BF16)
        dz_ref[...] = (du * o_ref[...] * (sig * (1.0 + zv * (1.0 - sig)))).astype(BF16)

    row = pl.BlockSpec((tm, d), lambda i: (i, 0))
    vec = pl.BlockSpec((1, d), lambda i: (0, 0))
    bf = jax.ShapeDtypeStruct((s, d), BF16)
    return _grid_call(
        body, name=name, grid=(s // tm,),
        in_specs=[row, row, _resident((1, d)), _resident(w_out.shape), row, row],
        out_specs=[row, vec, row, row], out_shape=[bf, jax.ShapeDtypeStruct((1, d), F32), bf, bf],
        args=(dx, y, g, w_out, o, z), semantics=("arbitrary",), exchange=exchange)


def _loss_and_grad(x, target):
    s, d = x.shape

    def body(x_ref, t_ref, dx_ref, l_ref):
        err = x_ref[...] - t_ref[...]
        dx_ref[...] = err * (1.0 / d)
        part = jnp.sum(jnp.sum(err * err, axis=1, keepdims=True), axis=0, keepdims=True) * (0.5 / d)

        @pl.when(pl.program_id(0) == 0)
        def _():
            l_ref[...] = jnp.zeros_like(l_ref)

        l_ref[...] += jnp.broadcast_to(part, l_ref.shape)

    return _row_call(body, "loss_head", [(x, "row"), (target, "row")],
                     [(jax.ShapeDtypeStruct((s, d), F32), "row"),
                      (jax.ShapeDtypeStruct((1, LANES), F32), "vec")], s=s)


def _norm_bwd_rows(dn, v, g):
    r = _rsqrt_ms(v)
    a = dn * g
    dv = r * (a - v * (r * r) * jnp.mean(a * v, axis=-1, keepdims=True))
    return dv, dn * v * r


def _pre_bwd(dx, dhs, x, g, name):
    s, d = dx.shape
    n_dh = len(dhs)

    def body(*refs):
        dx_ref, dh_refs, (x_ref, g_ref, o_ref, dg_ref) = refs[0], refs[1:1 + n_dh], refs[1 + n_dh:]
        dh = dh_refs[0][...].astype(F32)
        for r in dh_refs[1:]:
            dh = dh + r[...].astype(F32)
        dv, dg = _norm_bwd_rows(dh, x_ref[...], g_ref[...])
        o_ref[...] = dx_ref[...] + dv

        @pl.when(pl.program_id(0) == 0)
        def _():
            dg_ref[...] = jnp.zeros_like(dg_ref)

        dg_ref[...] += jnp.sum(dg, axis=0, keepdims=True)

    return _row_call(body, name, [(dx, "row")] + [(h, "row") for h in dhs] + [(x, "row"), (g, "vec")],
                     [(jax.ShapeDtypeStruct((s, d), F32), "row"),
                      (jax.ShapeDtypeStruct((1, d), F32), "vec")], s=s)


def _lane_is_first_head():
    return lax.broadcasted_iota(jnp.int32, (1, LANES), 1) < HEAD_DIM


def _bcast_lanes(col):
    return jnp.broadcast_to(col, (col.shape[0], LANES))


def _pair_spec(s, off=0, width=LANES):
    return pl.BlockSpec((s, width), lambda p: (0, p + off))


def _stack_heads(pair, first):
    return jnp.concatenate([jnp.where(first, pair, 0), jnp.where(first, 0, pair)], axis=0).astype(BF16)


def _stacked_mask(t, strict):
    row = lax.broadcasted_iota(jnp.int32, (2 * t, t), 0)
    col = lax.broadcasted_iota(jnp.int32, (2 * t, t), 1)
    query = jnp.where(row >= t, row - t, row)
    return col < query if strict else col <= query


LOOP_UNROLL = 2


def _two_at_a_time(n, step, carry):
    def group(jj, c):
        for k in range(LOOP_UNROLL):
            c = step(LOOP_UNROLL * jj + k, c)
        return c

    carry = lax.fori_loop(0, n // LOOP_UNROLL, group, carry)
    return lax.fori_loop(LOOP_UNROLL * (n // LOOP_UNROLL), n, step, carry)


def _rowsum_heads(prod, first):
    return (jnp.sum(jnp.where(first, prod, 0.0), axis=1, keepdims=True),
            jnp.sum(jnp.where(first, 0.0, prod), axis=1, keepdims=True))


def _softplus_parts(z):
    e = jnp.exp(-jnp.abs(z))
    sp = jnp.maximum(z, 0.0) + jnp.log(1.0 + e)
    r = 1.0 / (1.0 + e)
    return sp, jnp.where(z >= 0, r, e * r)


def _sb_tile(s):
    return min(256, s)


def _attn_b_fwd(qkv, name, exchange=None):
    s = qkv.shape[0]
    t = _sb_tile(s)
    nq = s // t

    def body(q_ref, k_ref, v_ref, o_ref, lt_ref):
        first = _lane_is_first_head()
        before = _stacked_mask(t, strict=True)
        tri = (lax.broadcasted_iota(jnp.int32, (t, t), 0) >= lax.broadcasted_iota(jnp.int32, (t, t), 1)).astype(BF16)

        def tile(j, carry, diag, qs):
            c, acc = carry
            c0 = pl.multiple_of(j * t, t)
            k2 = k_ref[pl.ds(c0, t), :]
            v2 = v_ref[pl.ds(c0, t), :]
            z = lax.dot_general(qs, k2, _NT, preferred_element_type=F32)
            sp, _ = _softplus_parts(z)
            lf = jnp.where(before, -sp, 0.0) if diag else -sp
            incl = jnp.dot(lf.astype(BF16), tri, preferred_element_type=F32)
            a = jnp.exp(z + c + incl)
            if diag:
                a = jnp.where(before, a, 0.0)
            pv = jnp.dot(a.astype(BF16), v2, preferred_element_type=F32)
            return c + incl[:, 0:1], acc + jnp.where(first, pv[:t], pv[t:])

        def qblock(i, _):
            r0 = pl.multiple_of(i * t, t)
            qs = _stack_heads(q_ref[pl.ds(r0, t), :] * Q_SCALE, first)
            carry = tile(i, (jnp.zeros((2 * t, 1), F32), jnp.zeros((t, LANES), F32)), True, qs)
            carry = _two_at_a_time(i, lambda j, c: tile(i - 1 - j, c, False, qs), carry)
            o_ref[pl.ds(r0, t), :] = carry[1]
            lt_ref[pl.ds(r0, t), 0:LANES] = _bcast_lanes(carry[0][:t])
            lt_ref[pl.ds(r0, t), LANES:2 * LANES] = _bcast_lanes(carry[0][t:])
            return 0

        lax.fori_loop(0, nq, qblock, 0)

    return _grid_call(
        body, name=name, grid=(N_PAIRS,),
        in_specs=[_pair_spec(s), _pair_spec(s, N_PAIRS), _pair_spec(s, 2 * N_PAIRS)],
        out_specs=[_pair_spec(s), _stat_spec(s)],
        out_shape=[jax.ShapeDtypeStruct((s, BRANCH), F32), jax.ShapeDtypeStruct((s, N_HEADS * LANES), F32)],
        args=(qkv, qkv, qkv), semantics=("parallel",), exchange=exchange)


def _attn_b_bwd(qkv, ltot, do, name, exchange=None):
    s = qkv.shape[0]
    t = _sb_tile(s)
    nq = s // t

    def body(q_ref, k_ref, v_ref, lt_ref, do_ref, dq_ref, dk_ref, dv_ref, dk_acc, dv_acc):
        first = _lane_is_first_head()
        before = _stacked_mask(t, strict=True)
        tri = (lax.broadcasted_iota(jnp.int32, (t, t), 0) <= lax.broadcasted_iota(jnp.int32, (t, t), 1)).astype(BF16)
        dk_acc[...] = jnp.zeros_like(dk_acc)
        dv_acc[...] = jnp.zeros_like(dv_acc)

        def tile(j, carry, diag, qs, dos, lt):
            p_l, p_g, dq_acc = carry
            c0 = pl.multiple_of(j * t, t)
            k2 = k_ref[pl.ds(c0, t), :]
            v2 = v_ref[pl.ds(c0, t), :]
            z = lax.dot_general(qs, k2, _NT, preferred_element_type=F32)
            sp, sig = _softplus_parts(z)
            lf = jnp.where(before, -sp, 0.0) if diag else -sp
            pref_l = jnp.dot(lf.astype(BF16), tri, preferred_element_type=F32)
            a = jnp.exp(z + ((lt - p_l) - pref_l + lf))
            if diag:
                a = jnp.where(before, a, 0.0)
            g = a * lax.dot_general(dos, v2, _NT, preferred_element_type=F32)
            pref_g = jnp.dot(g.astype(BF16), tri, preferred_element_type=F32)
            dz = g - sig * (p_g + pref_g)
            if diag:
                dz = jnp.where(before, dz, 0.0)
            dzb = dz.astype(BF16)
            dq = jnp.dot(dzb, k2, preferred_element_type=F32)
            dk_acc[pl.ds(c0, t), :] += lax.dot_general(dzb, qs, _TN, preferred_element_type=F32)
            dv_acc[pl.ds(c0, t), :] += lax.dot_general(a.astype(BF16), dos, _TN, preferred_element_type=F32)
            return p_l + pref_l[:, t - 1:t], p_g + pref_g[:, t - 1:t], dq_acc + jnp.where(first, dq[:t], dq[t:])

        def qblock(i, _):
            r0 = pl.multiple_of(i * t, t)
            qs = _stack_heads(q_ref[pl.ds(r0, t), :] * Q_SCALE, first)
            dos = _stack_heads(do_ref[pl.ds(r0, t), :], first)
            lt = jnp.concatenate([lt_ref[pl.ds(r0, t), 0:1], lt_ref[pl.ds(r0, t), LANES:LANES + 1]], axis=0)
            zero = jnp.zeros((2 * t, 1), F32)
            carry = (zero, zero, jnp.zeros((t, LANES), F32))
            carry = _two_at_a_time(i, lambda j, c: tile(j, c, False, qs, dos, lt), carry)
            carry = tile(i, carry, True, qs, dos, lt)
            dq_ref[pl.ds(r0, t), :] = (carry[2] * Q_SCALE).astype(BF16)
            return 0

        lax.fori_loop(0, nq, qblock, 0)
        dk_ref[...] = dk_acc[...].astype(BF16)
        dv_ref[...] = dv_acc[...].astype(BF16)

    out = jax.ShapeDtypeStruct((s, BRANCH), BF16)
    return _grid_call(
        body, name=name, grid=(N_PAIRS,),
        in_specs=[_pair_spec(s), _pair_spec(s, N_PAIRS), _pair_spec(s, 2 * N_PAIRS), _stat_spec(s), _pair_spec(s)],
        out_specs=[_pair_spec(s)] * 3, out_shape=[out] * 3,
        scratch_shapes=[pltpu.VMEM((s, LANES), F32), pltpu.VMEM((s, LANES), F32)],
        args=(qkv, qkv, qkv, ltot, do), semantics=("parallel",), exchange=exchange)


def _fox_tile(s):
    return min(256, s)


def _stat_spec(s):
    return pl.BlockSpec((s, 2 * LANES), lambda p: (0, p))


def _cum_spec(nt, t):
    return pl.BlockSpec((1, nt, 2, t), lambda p: (p, 0, 0, 0))


def _attn_c_fwd(qkv, cum4, name, exchange=None):
    s = qkv.shape[0]
    t = _fox_tile(s)
    nq = s // t

    def body(q_ref, k_ref, v_ref, c_ref, o_ref, lse_ref):
        first = _lane_is_first_head()
        causal = _stacked_mask(t, strict=False)

        def tile(j, carry, diag, qs):
            c0 = pl.multiple_of(j * t, t)
            k2 = k_ref[pl.ds(c0, t), :]
            v2 = v_ref[pl.ds(c0, t), :]
            cs = c_ref[0, j]
            m_prev, l_prev, acc = carry
            z = lax.dot_general(qs, k2, _NT, preferred_element_type=F32)
            sc = jnp.concatenate([z[:t] - cs[0:1, :], z[t:] - cs[1:2, :]], axis=0)
            if diag:
                sc = jnp.where(causal, sc, NEG)
            m_new = jnp.maximum(m_prev, jnp.max(sc, axis=1, keepdims=True))
            alpha = jnp.exp(m_prev - m_new)
            p = jnp.exp(sc - m_new)
            l_new = alpha * l_prev + jnp.sum(p, axis=1, keepdims=True)
            pv = jnp.dot(p.astype(BF16), v2, preferred_element_type=F32)
            acc = jnp.where(first, acc * alpha[:t] + pv[:t], acc * alpha[t:] + pv[t:])
            return m_new, l_new, acc

        def qblock(i, _):
            r0 = pl.multiple_of(i * t, t)
            qs = _stack_heads(q_ref[pl.ds(r0, t), :] * Q_SCALE, first)
            carry = (jnp.full((2 * t, 1), NEG, F32), jnp.zeros((2 * t, 1), F32), jnp.zeros((t, LANES), F32))
            carry = _two_at_a_time(i, lambda j, c: tile(j, c, False, qs), carry)
            m, l, acc = tile(i, carry, True, qs)
            inv = 1.0 / l
            lse = m + jnp.log(l)
            o_ref[pl.ds(r0, t), :] = acc * jnp.where(first, inv[:t], inv[t:])
            lse_ref[pl.ds(r0, t), 0:LANES] = _bcast_lanes(lse[:t])
            lse_ref[pl.ds(r0, t), LANES:2 * LANES] = _bcast_lanes(lse[t:])
            return 0

        lax.fori_loop(0, nq, qblock, 0)

    return _grid_call(
        body, name=name, grid=(N_PAIRS,),
        in_specs=[_pair_spec(s), _pair_spec(s, N_PAIRS), _pair_spec(s, 2 * N_PAIRS), _cum_spec(nq, t)],
        out_specs=[_pair_spec(s), _stat_spec(s)],
        out_shape=[jax.ShapeDtypeStruct((s, BRANCH), F32), jax.ShapeDtypeStruct((s, N_HEADS * LANES), F32)],
        args=(qkv, qkv, qkv, cum4), semantics=("parallel",), exchange=exchange)


def _attn_c_bwd(qkv, cum4, o, lse, do, name, exchange=None):
    s = qkv.shape[0]
    t = _fox_tile(s)
    nq = s // t

    def body(q_ref, k_ref, v_ref, c_ref, o_ref, lse_ref, do_ref, dq_ref, dk_ref, dv_ref, dc_ref, dk_acc, dv_acc):
        first = _lane_is_first_head()
        causal = _stacked_mask(t, strict=False)
        eye = lax.broadcasted_iota(jnp.int32, (t, t), 0) == lax.broadcasted_iota(jnp.int32, (t, t), 1)
        dk_acc[...] = jnp.zeros_like(dk_acc)
        dv_acc[...] = jnp.zeros_like(dv_acc)
        dc_ref[...] = jnp.zeros_like(dc_ref)

        def tile(j, carry, diag, qs, dos, delta, lse):
            dq_acc, rs = carry
            c0 = pl.multiple_of(j * t, t)
            k2 = k_ref[pl.ds(c0, t), :]
            v2 = v_ref[pl.ds(c0, t), :]
            cs = c_ref[0, j]
            z = lax.dot_general(qs, k2, _NT, preferred_element_type=F32)
            sc = jnp.concatenate([z[:t] - cs[0:1, :], z[t:] - cs[1:2, :]], axis=0)
            p = jnp.exp(sc - lse)
            if diag:
                p = jnp.where(causal, p, 0.0)
            ds = p * (lax.dot_general(dos, v2, _NT, preferred_element_type=F32) - delta)
            dsb = ds.astype(BF16)
            dq = jnp.dot(dsb, k2, preferred_element_type=F32)
            dk_acc[pl.ds(c0, t), :] += lax.dot_general(dsb, qs, _TN, preferred_element_type=F32)
            dv_acc[pl.ds(c0, t), :] += lax.dot_general(p.astype(BF16), dos, _TN, preferred_element_type=F32)
            col_sums = jnp.concatenate([jnp.sum(ds[:t], axis=0, keepdims=True), jnp.sum(ds[t:], axis=0, keepdims=True)], axis=0)
            dc_ref[0, j] = dc_ref[0, j] - col_sums
            return dq_acc + jnp.where(first, dq[:t], dq[t:]), rs + jnp.sum(ds, axis=1, keepdims=True)

        def qblock(i, _):
            r0 = pl.multiple_of(i * t, t)
            do2 = do_ref[pl.ds(r0, t), :]
            qs = _stack_heads(q_ref[pl.ds(r0, t), :] * Q_SCALE, first)
            dos = _stack_heads(do2, first)
            delta = jnp.concatenate(_rowsum_heads(do2.astype(F32) * o_ref[pl.ds(r0, t), :], first), axis=0)
            lse = jnp.concatenate([lse_ref[pl.ds(r0, t), 0:1], lse_ref[pl.ds(r0, t), LANES:LANES + 1]], axis=0)
            carry = (jnp.zeros((t, LANES), F32), jnp.zeros((2 * t, 1), F32))
            carry = _two_at_a_time(i, lambda j, c: tile(j, c, False, qs, dos, delta, lse), carry)
            dq_acc, rs = tile(i, carry, True, qs, dos, delta, lse)
            dq_ref[pl.ds(r0, t), :] = (dq_acc * Q_SCALE).astype(BF16)
            as_row = lambda col_vec: jnp.sum(jnp.where(eye, col_vec, 0.0), axis=0, keepdims=True)
            dc_ref[0, i] = dc_ref[0, i] + jnp.concatenate([as_row(rs[:t]), as_row(rs[t:])], axis=0)
            return 0

        lax.fori_loop(0, nq, qblock, 0)
        dk_ref[...] = dk_acc[...].astype(BF16)
        dv_ref[...] = dv_acc[...].astype(BF16)

    out = jax.ShapeDtypeStruct((s, BRANCH), BF16)
    return _grid_call(
        body, name=name, grid=(N_PAIRS,),
        in_specs=[_pair_spec(s), _pair_spec(s, N_PAIRS), _pair_spec(s, 2 * N_PAIRS), _cum_spec(nq, t),
                  _pair_spec(s), _stat_spec(s), _pair_spec(s)],
        out_specs=[_pair_spec(s)] * 3 + [_cum_spec(nq, t)],
        out_shape=[out] * 3 + [jax.ShapeDtypeStruct(cum4.shape, F32)],
        scratch_shapes=[pltpu.VMEM((s, LANES), F32), pltpu.VMEM((s, LANES), F32)],
        args=(qkv, qkv, qkv, cum4, o, lse, do), semantics=("parallel",), exchange=exchange)


FG_CHUNK = 512


def _tri_dot3(x, t):
    hi = x.astype(BF16)
    r1 = x - hi.astype(F32)
    mid = r1.astype(BF16)
    lo = (r1 - mid.astype(F32)).astype(BF16)
    return (jnp.dot(hi, t, preferred_element_type=F32) + jnp.dot(mid, t, preferred_element_type=F32)
            + jnp.dot(lo, t, preferred_element_type=F32))


def _fgate_fwd(h, wf_t, b_col, name):
    s = h.shape[0]
    c = min(FG_CHUNK, s)

    def body(h_ref, w_ref, b_ref, xf_ref, cum_ref, carry_ref):
        @pl.when(pl.program_id(0) == 0)
        def _():
            carry_ref[...] = jnp.zeros_like(carry_ref)

        xf = lax.dot_general(w_ref[...], h_ref[...], _NT, preferred_element_type=F32) + b_ref[:, 0:1]
        xf_ref[...] = xf
        logf = jnp.minimum(xf, 0.0) - jnp.log(1.0 + jnp.exp(-jnp.abs(xf)))
        row = lax.broadcasted_iota(jnp.int32, (c, c), 0)
        col = lax.broadcasted_iota(jnp.int32, (c, c), 1)
        cum = _tri_dot3(logf, (row <= col).astype(BF16)) + carry_ref[:, 0:1]
        cum_ref[...] = cum
        carry_ref[...] = _bcast_lanes(cum[:, c - 1:c])

    out = jax.ShapeDtypeStruct((N_HEADS, s), F32)
    return pl.pallas_call(
        body, name=name, grid=(s // c,),
        in_specs=[pl.BlockSpec((c, D_MODEL), lambda i: (i, 0)),
                  pl.BlockSpec((N_HEADS, D_MODEL), lambda i: (0, 0)),
                  pl.BlockSpec((N_HEADS, LANES), lambda i: (0, 0))],
        out_specs=[pl.BlockSpec((N_HEADS, c), lambda i: (0, i))] * 2,
        out_shape=[out, out],
        scratch_shapes=[pltpu.VMEM((N_HEADS, LANES), F32)],
        compiler_params=_params(("arbitrary",)),
    )(h, wf_t, b_col)


def _fgate_bwd(dcum, xf, h, wf_t, name):
    s = h.shape[0]
    c = min(FG_CHUNK, s)
    n = s // c

    def body(dc_ref, xf_ref, h_ref, w_ref, dw_ref, dh_ref, db_ref, carry_ref):
        @pl.when(pl.program_id(0) == 0)
        def _():
            carry_ref[...] = jnp.zeros_like(carry_ref)
            dw_ref[...] = jnp.zeros_like(dw_ref)
            db_ref[...] = jnp.zeros_like(db_ref)

        row = lax.broadcasted_iota(jnp.int32, (c, c), 0)
        col = lax.broadcasted_iota(jnp.int32, (c, c), 1)
        dlogf = _tri_dot3(dc_ref[...], (row >= col).astype(BF16)) + carry_ref[:, 0:1]
        carry_ref[...] = _bcast_lanes(dlogf[:, 0:1])
        xf = xf_ref[...]
        e = jnp.exp(-jnp.abs(xf))
        r = 1.0 / (1.0 + e)
        dxf = dlogf * jnp.where(xf >= 0, e * r, r)
        db_ref[...] += _bcast_lanes(jnp.sum(dxf, axis=1, keepdims=True))
        dxb = dxf.astype(BF16)
        dw_ref[...] += jnp.dot(dxb, h_ref[...], preferred_element_type=F32)
        dh_ref[...] = lax.dot_general(dxb, w_ref[...], _TN, preferred_element_type=F32)

    rev = lambda i: n - 1 - i
    return pl.pallas_call(
        body, name=name, grid=(n,),
        in_specs=[pl.BlockSpec((N_HEADS, c), lambda i: (0, rev(i))),
                  pl.BlockSpec((N_HEADS, c), lambda i: (0, rev(i))),
                  pl.BlockSpec((c, D_MODEL), lambda i: (rev(i), 0)),
                  pl.BlockSpec((N_HEADS, D_MODEL), lambda i: (0, 0))],
        out_specs=[pl.BlockSpec((N_HEADS, D_MODEL), lambda i: (0, 0)),
                   pl.BlockSpec((c, D_MODEL), lambda i: (rev(i), 0)),
                   pl.BlockSpec((N_HEADS, LANES), lambda i: (0, 0))],
        out_shape=[jax.ShapeDtypeStruct((N_HEADS, D_MODEL), F32), jax.ShapeDtypeStruct((s, D_MODEL), F32),
                   jax.ShapeDtypeStruct((N_HEADS, LANES), F32)],
        scratch_shapes=[pltpu.VMEM((N_HEADS, LANES), F32)],
        compiler_params=_params(("arbitrary",)),
    )(dcum, xf, h, wf_t)


def _to_cum4(v, t):
    s = v.shape[1]
    return v.reshape(N_PAIRS, 2, s // t, t).transpose(0, 2, 1, 3)


def _from_cum4(v4):
    p, nt, two, t = v4.shape
    return v4.transpose(0, 2, 1, 3).reshape(p * two, nt * t)


def _alibi_slopes():
    return (2.0 ** (-8.0 * np.arange(1, N_HEADS + 1, dtype=np.float32) / N_HEADS)).astype(np.float32)


def _per_head_lanes(v):
    return jnp.repeat(v.astype(F32).reshape(N_PAIRS, 1, 2), LANES, axis=2)


def _attn_a_specs(s):
    q = _pair_spec(s)
    k = pl.BlockSpec((s, LANES), lambda p: (0, N_PAIRS + p // 8))
    v = pl.BlockSpec((s, LANES), lambda p: (0, N_PAIRS + KV_A // LANES + p // 8))
    head = pl.BlockSpec((1, 1, 2 * LANES), lambda p: (p, 0, 0))
    return q, k, v, head


def _attn_a_geometry(p, slope_ref, sink_ref):
    kv_half = (p // 4) % 2
    kv_first = kv_half == 0
    lane_first = _lane_is_first_head()
    kv_lanes = (lax.broadcasted_iota(jnp.int32, (1, LANES), 1) // HEAD_DIM) == kv_half
    row = lax.broadcasted_iota(jnp.int32, (2 * WINDOW, 2 * WINDOW), 0)
    cj = lax.broadcasted_iota(jnp.int32, (2 * WINDOW, 2 * WINDOW), 1)
    second = row >= WINDOW
    dist = WINDOW + jnp.where(second, row - WINDOW, row) - cj
    valid = (dist >= 0) & (dist < WINDOW)
    per_row = lambda ref: jnp.where(second[:, 0:1], ref[0, :, LANES:LANES + 1], ref[0, :, 0:1])
    return kv_first, lane_first, kv_lanes, per_row(slope_ref) * dist.astype(F32), valid, per_row(sink_ref)


def _swap_halves(x):
    return pltpu.roll(x, HEAD_DIM, 1)


def _attn_a_fwd(qkv, slopes, sinks, name, exchange=None):
    s = qkv.shape[0]
    nb = s // WINDOW

    def body(q_ref, k_ref, v_ref, sl_ref, sk_ref, o_ref, lse_ref):
        kv_first, lane_first, kv_lanes, bias, valid, sink = _attn_a_geometry(pl.program_id(0), sl_ref, sk_ref)

        def block(r0, k0, width):
            q2 = q_ref[pl.ds(r0, WINDOW), :].astype(F32) * Q_SCALE
            q2r = _swap_halves(q2)
            xs = jnp.concatenate([jnp.where(kv_first, q2, q2r), jnp.where(kv_first, q2r, q2)], axis=0).astype(BF16)
            km = jnp.where(kv_lanes, k_ref[pl.ds(k0, width), :], 0).astype(BF16)
            vm = jnp.where(kv_lanes, v_ref[pl.ds(k0, width), :], 0).astype(BF16)
            sc = lax.dot_general(xs, km, _NT, preferred_element_type=F32) - bias[:, 2 * WINDOW - width:]
            sc = jnp.where(valid[:, 2 * WINDOW - width:], sc, NEG)
            m = jnp.maximum(jnp.max(sc, axis=1, keepdims=True), sink)
            pr = jnp.exp(sc - m)
            l = jnp.sum(pr, axis=1, keepdims=True) + jnp.exp(sink - m)
            os = jnp.dot(pr.astype(BF16), vm, preferred_element_type=F32) * (1.0 / l)
            lse = m + jnp.log(l)
            lse_ref[pl.ds(r0, WINDOW), 0:LANES] = _bcast_lanes(lse[:WINDOW])
            lse_ref[pl.ds(r0, WINDOW), LANES:2 * LANES] = _bcast_lanes(lse[WINDOW:])
            oa = jnp.where(kv_first, os[:WINDOW], _swap_halves(os[:WINDOW]))
            ob = jnp.where(kv_first, _swap_halves(os[WINDOW:]), os[WINDOW:])
            o_ref[pl.ds(r0, WINDOW), :] = jnp.where(lane_first, oa, ob)

        block(0, 0, WINDOW)

        def loop(n, _):
            r0 = pl.multiple_of(n * WINDOW, WINDOW)
            block(r0, pl.multiple_of(r0 - WINDOW, WINDOW), 2 * WINDOW)
            return 0

        _two_at_a_time(nb - 1, lambda n, c: loop(n + 1, c), 0)

    q, k, v, head = _attn_a_specs(s)
    return _grid_call(
        body, name=name, grid=(N_PAIRS,),
        in_specs=[q, k, v, head, head],
        out_specs=[_pair_spec(s), _stat_spec(s)],
        out_shape=[jax.ShapeDtypeStruct((s, BRANCH), F32), jax.ShapeDtypeStruct((s, N_HEADS * LANES), F32)],
        args=(qkv, qkv, qkv, slopes, sinks), semantics=("parallel",), exchange=exchange)


def _attn_a_bwd(qkv, slopes, sinks, o, lse, do, name, exchange=None):
    s = qkv.shape[0]
    nb = s // WINDOW

    def body(q_ref, k_ref, v_ref, sl_ref, sk_ref, o_ref, lse_ref, do_ref, dq_ref, dk_ref, dv_ref, dsk_ref):
        p_id = pl.program_id(0)
        kv_first, lane_first, kv_lanes, bias, valid, sink = _attn_a_geometry(p_id, sl_ref, sk_ref)

        @pl.when(p_id % 8 == 0)
        def _():
            dk_ref[...] = jnp.zeros_like(dk_ref)
            dv_ref[...] = jnp.zeros_like(dv_ref)

        def align(v2):
            v2r = _swap_halves(v2)
            both = jnp.concatenate([jnp.where(kv_first, v2, v2r), jnp.where(kv_first, v2r, v2)], axis=0)
            return jnp.where(kv_lanes, both, 0.0).astype(BF16)

        def block(r0, k0, width, sink_sum):
            xq = align(q_ref[pl.ds(r0, WINDOW), :].astype(F32) * Q_SCALE)
            do2 = do_ref[pl.ds(r0, WINDOW), :].astype(F32)
            xdo = align(do2)
            delta = jnp.concatenate(_rowsum_heads(do2 * o_ref[pl.ds(r0, WINDOW), :], lane_first), axis=0)
            lse = jnp.concatenate([lse_ref[pl.ds(r0, WINDOW), 0:1], lse_ref[pl.ds(r0, WINDOW), LANES:LANES + 1]], axis=0)
            km = jnp.where(kv_lanes, k_ref[pl.ds(k0, width), :], 0).astype(BF16)
            vm = jnp.where(kv_lanes, v_ref[pl.ds(k0, width), :], 0).astype(BF16)
            sc = lax.dot_general(xq, km, _NT, preferred_element_type=F32) - bias[:, 2 * WINDOW - width:]
            pr = jnp.where(valid[:, 2 * WINDOW - width:], jnp.exp(sc - lse), 0.0)
            ds = pr * (lax.dot_general(xdo, vm, _NT, preferred_element_type=F32) - delta)
            dsb = ds.astype(BF16)
            dq_al = jnp.dot(dsb, km, preferred_element_type=F32)
            dk_ref[pl.ds(k0, width), :] += lax.dot_general(dsb, xq, _TN, preferred_element_type=F32)
            dv_ref[pl.ds(k0, width), :] += lax.dot_general(pr.astype(BF16), xdo, _TN, preferred_element_type=F32)
            dqa = jnp.where(kv_first, dq_al[:WINDOW], _swap_halves(dq_al[:WINDOW]))
            dqb = jnp.where(kv_first, _swap_halves(dq_al[WINDOW:]), dq_al[WINDOW:])
            dq_ref[pl.ds(r0, WINDOW), :] = (jnp.where(lane_first, dqa, dqb) * Q_SCALE).astype(BF16)
            return sink_sum + jnp.exp(sink - lse) * delta

        sink_sum = block(0, 0, WINDOW, jnp.zeros((2 * WINDOW, 1), F32))

        def loop(n, c):
            r0 = pl.multiple_of(n * WINDOW, WINDOW)
            return block(r0, pl.multiple_of(r0 - WINDOW, WINDOW), 2 * WINDOW, c)

        sink_sum = _two_at_a_time(nb - 1, lambda n, c: loop(n + 1, c), sink_sum)
        dsk_ref[0, :, 0:LANES] = jnp.broadcast_to(-jnp.sum(sink_sum[:WINDOW], axis=0, keepdims=True), (1, LANES))
        dsk_ref[0, :, LANES:2 * LANES] = jnp.broadcast_to(-jnp.sum(sink_sum[WINDOW:], axis=0, keepdims=True), (1, LANES))

    q, k, v, head = _attn_a_specs(s)
    kv_out = pl.BlockSpec((s, LANES), lambda p: (0, p // 8))
    return _grid_call(
        body, name=name, grid=(N_PAIRS,),
        in_specs=[q, k, v, head, head, _pair_spec(s), _stat_spec(s), _pair_spec(s)],
        out_specs=[_pair_spec(s), kv_out, kv_out, head],
        out_shape=[jax.ShapeDtypeStruct((s, BRANCH), BF16), jax.ShapeDtypeStruct((s, KV_A), F32),
                   jax.ShapeDtypeStruct((s, KV_A), F32), jax.ShapeDtypeStruct((N_PAIRS, 1, 2 * LANES), F32)],
        args=(qkv, qkv, qkv, slopes, sinks, o, lse, do), semantics=("arbitrary",), exchange=exchange)


def _layer_kind(i):
    return i % 3, i // 3


GATHER_FIRST = [("in", 0)]
GATHER_BEHIND = {("qkv", 0): [("out", 0)], ("attn", 0): [("in", 1)], ("attn", 1): [("out", 1), ("in", 2), ("out", 2)],
                 ("attn", 2): [("in", 3), ("out", 3)]}


def _forward_backward(x, target, g_pre, g_post, sinks_a, b_f_c, shards, chip, core):
    s = x.shape[0]
    slopes = _per_head_lanes(jnp.asarray(_alibi_slopes()))
    w_in, w_out, wf_t = {}, {}, {}

    def lands_side_by_side(key):
        return key[0] == "in" and shards[key].shape[1] % LANES == 0

    def gather(keys):
        return _GatherExchange([shards[k] for k in keys], [lands_side_by_side(k) for k in keys])

    def deliver(keys, gathered):
        for key, g in zip(keys, gathered):
            side, layer = key
            sh = shards[key]
            if side == "out":
                g = lax.dynamic_update_slice(g, sh[None], (chip, 0, 0))
                w_out[layer] = g.reshape(4 * sh.shape[0], sh.shape[1])
            elif lands_side_by_side(key):
                w_in[layer] = _place_columns(g, sh, chip, f"own_block_in_l{layer}")
            else:
                g = lax.dynamic_update_slice(g, sh[None], (chip, 0, 0))
                w = g.transpose(1, 0, 2).reshape(sh.shape[0], 4 * sh.shape[1])
                w_in[layer], wf_t[layer] = w[:, :4 * BRANCH], w[:, 4 * BRANCH:].T

    deliver(GATHER_FIRST, _exchange_call(gather(GATHER_FIRST), "gather_first_weights"))
    saved = []
    for i in range(DEPTH):
        kind, j = _layer_kind(i)
        tag = f"l{i}"
        w = w_in[i]
        nqkv = A_QKV if kind == 0 else B_QKV
        tn = 512 if kind == 0 else 1024
        h, h_t = _rmsnorm_fwd(x, g_pre[i:i + 1], f"prenorm_{tag}")
        behind = GATHER_BEHIND.get(("qkv", i))
        qkv = _matmul(h, w, mode="nn", out_dtype=BF16, name=f"inproj_qkv_{tag}", n=nqkv, tn=tn,
                      exchange=gather(behind) if behind else None)
        if behind:
            qkv, arrived = qkv
            deliver(behind, arrived)
        z = _matmul(h, w, mode="nn", out_dtype=F32, name=f"inproj_gate_{tag}", n=BRANCH, b_off=nqkv // tn, tn=tn)
        behind = GATHER_BEHIND.get(("attn", i))
        exchange = gather(behind) if behind else None
        if kind == 0:
            sink_l = _per_head_lanes(sinks_a[j])
            (o, lse), arrived = _attn_a_fwd(qkv, slopes, sink_l, f"attn_a_fwd_{tag}", exchange)
            extra = (sink_l, lse)
        elif kind == 1:
            (o, extra), arrived = _attn_b_fwd(qkv, f"attn_b_fwd_{tag}", exchange)
        else:
            b_col = jnp.broadcast_to(b_f_c[j].astype(F32)[:, None], (N_HEADS, LANES))
            xf, cum = _fgate_fwd(h, wf_t[i], b_col, f"fgate_fwd_{tag}")
            cum4 = _to_cum4(cum, _fox_tile(s))
            (o, lse), arrived = _attn_c_fwd(qkv, cum4, f"attn_c_fwd_{tag}", exchange)
            extra = (xf, cum4, lse)
        if behind:
            deliver(behind, arrived)
        x_next, y, u_t = _gated_out_proj(o, z, w_out[i], x, g_post[i:i + 1], f"outproj_{tag}")
        saved.append((x, h, h_t, qkv, z, o, u_t, y, extra))
        x = x_next

    dx, loss_part = _loss_and_grad(x, target)

    d_g_pre, d_g_post = [None] * DEPTH, [None] * DEPTH
    d_sinks = [None, None]
    d_b_f = None
    reduced = {}
    pending = None

    def finish_reduce(layer, side, own, arr):
        kind, j = _layer_kind(layer)
        reduced[(side, kind)] = _sum_chips(own, arr, core, f"shard_sum_{side}_l{layer}", j, 2 if kind == 0 else 1,
                                           into=reduced.get((side, kind)))

    for i in reversed(range(DEPTH)):
        kind, j = _layer_kind(i)
        tag = f"l{i}"
        x_in, h, h_t, qkv, z, o, u_t, y, extra = saved[i]
        tn = 512 if kind == 0 else 1024
        (dy, d_g_post[i], do, dz), _ = _gated_out_proj_bwd(dx, y, g_post[i:i + 1], w_out[i], o, z, f"outproj_bwd_{tag}")
        dw_out = _matmul(u_t, dy, mode="nn", out_dtype=BF16, name=f"dw_out_{tag}")
        dw_out = dw_out.reshape(4, dw_out.shape[0] // 4, dw_out.shape[1])
        dhs = []
        exchange = _SiblingExchange([dw_out])
        if pending:
            exchange = _BothExchanges(exchange, _ScatterExchange([pending[1]]))
        if kind == 0:
            sink_l, lse = extra
            (dq, dk, dv, dsk), arrived = _attn_a_bwd(qkv, slopes, sink_l, o, lse, do, f"attn_a_bwd_{tag}", exchange)
            d_sinks[j] = dsk[:, 0, ::LANES].reshape(N_HEADS)
            parts = [dq, dk.astype(BF16), dv.astype(BF16), dz]
        elif kind == 1:
            (dq, dk, dv), arrived = _attn_b_bwd(qkv, extra, do, f"attn_b_bwd_{tag}", exchange)
            parts = [dq, dk, dv, dz]
        else:
            xf, cum4, lse = extra
            (dq, dk, dv, dcum4), arrived = _attn_c_bwd(qkv, cum4, o, lse, do, f"attn_c_bwd_{tag}", exchange)
            d_wf_t, dh_f, db = _fgate_bwd(_from_cum4(dcum4), xf, h, wf_t[i], f"fgate_bwd_{tag}")
            d_b_f = db[:, 0]
            dhs.append(dh_f)
            parts = [dq, dk, dv, dz]
        sum_out = _add_pairs(dw_out, arrived[0], f"chip_sum_out_{tag}")
        if pending:
            finish_reduce(pending[0], "in", pending[1], arrived[1])
        dproj = jnp.concatenate(parts, axis=1)
        scatter_out = _ScatterExchange([sum_out])
        if kind == 2:
            dw_in, arrived = _matmul(h_t, dproj, mode="nn", out_dtype=F32, name=f"dw_in_{tag}", tn=tn, exchange=scatter_out)
            dw_in = jnp.concatenate([dw_in, d_wf_t.T], axis=1)
            dw_in = dw_in.reshape(dw_in.shape[0], 4, dw_in.shape[1] // 4).transpose(1, 0, 2).astype(BF16)
        else:
            dw_in, arrived = _matmul(h_t, dproj, mode="nn", out_dtype=BF16, name=f"dw_in_{tag}", col_blocks=4,
                                     tn=1152 if kind == 0 else 1024, exchange=scatter_out)
        finish_reduce(i, "out", sum_out, arrived[0])
        dh, (their_in,) = _matmul(dproj, w_in[i], mode="nt", out_dtype=F32, name=f"dh_{tag}",
                                  tk=1536 if kind == 0 else 2048, exchange=_SiblingExchange([dw_in]))
        dhs.insert(0, dh)
        dx, d_g_pre[i] = _pre_bwd(dx, dhs, x_in, g_pre[i:i + 1], f"prenorm_bwd_{tag}")
        pending = (i, _add_pairs(dw_in, their_in, f"chip_sum_in_{tag}"))

    arrived = _exchange_call(_ScatterExchange([pending[1]]), "grad_chip_scatter_last")
    finish_reduce(pending[0], "in", pending[1], arrived[0])

    return dict(loss=loss_part, dx=dx, g_pre=jnp.concatenate(d_g_pre, axis=0), g_post=jnp.concatenate(d_g_post, axis=0),
                sinks_a=jnp.stack(d_sinks), b_f_c=d_b_f[None, :], reduced=reduced)


def _place():
    x, y, c = lax.axis_index("x"), lax.axis_index("y"), lax.axis_index("c")
    others = [(1 - x, y), (x, 1 - y), (1 - x, 1 - y)]
    return x, y, c, others


def _half_rows(ref_rows, which):
    half = ref_rows // 2
    return pl.ds(pl.multiple_of(which * half, half), half)


def _remote(src, dst, sems, k, device):
    send, recv = sems
    return pltpu.make_async_remote_copy(src_ref=src, dst_ref=dst, send_sem=send.at[k], recv_sem=recv.at[k],
                                        device_id=device, device_id_type=MESH)


def _hbm_call(body, name, ins, out_shapes, n_remote, aliases=None):
    any_spec = pl.BlockSpec(memory_space=pl.ANY)
    return pl.pallas_call(
        body, name=name, in_specs=[any_spec] * len(ins), out_specs=[any_spec] * len(out_shapes),
        out_shape=out_shapes, input_output_aliases=aliases or {},
        scratch_shapes=[pltpu.SemaphoreType.DMA((n_remote,)), pltpu.SemaphoreType.DMA((n_remote,))],
    )(*ins)


class _GatherExchange:
    def __init__(self, shards, side_by_side):
        self.ins = list(shards)
        self.side_by_side = list(side_by_side)
        self.out_shapes = [jax.ShapeDtypeStruct((a.shape[0], 4 * a.shape[1]) if wide else (4,) + a.shape, a.dtype)
                           for a, wide in zip(shards, side_by_side)]
        self.n_sems = 6 * len(shards)
        self.aliases = {}

    def _copies(self, ins, outs, sems):
        x, y, c, others = _place()
        me = 2 * x + y
        table = []
        for w, (src, dst, wide) in enumerate(zip(ins, outs, self.side_by_side)):
            rows, cols = src.shape
            mine, theirs = _half_rows(rows, c), _half_rows(rows, 1 - c)

            def slot(chip, which, dst=dst, wide=wide, cols=cols):
                return dst.at[which, pl.ds(pl.multiple_of(chip * cols, LANES), cols)] if wide else dst.at[chip, which]

            for j, (px, py) in enumerate(others):
                there = 2 * px + py
                send = _remote(src.at[mine], slot(me, mine), sems, 6 * w + j, (px, py, c))
                landed = _remote(slot(there, mine), slot(there, mine), sems, 6 * w + j, (px, py, c))
                passed = _remote(slot(there, mine), slot(there, mine), sems, 6 * w + 3 + j, (x, y, 1 - c))
                from_sibling = _remote(slot(there, theirs), slot(there, theirs), sems, 6 * w + 3 + j, (x, y, 1 - c))
                table.append((send, landed, passed, from_sibling))
        return table

    def start(self, ins, outs, sems):
        for send, _, _, _ in self._copies(ins, outs, sems):
            send.start()

    def mid(self, ins, outs, sems):
        for _, landed, passed, _ in self._copies(ins, outs, sems):
            landed.wait_recv()
            passed.start()

    def finish(self, ins, outs, sems):
        table = self._copies(ins, outs, sems)
        for _, _, _, from_sibling in table:
            from_sibling.wait_recv()
        for send, _, passed, _ in table:
            send.wait_send()
            passed.wait_send()


class _SemaphoresFrom:
    def __init__(self, ref, start):
        self._ref, self._start = ref, start

    @property
    def at(self):
        return self

    def __getitem__(self, k):
        return self._ref.at[self._start + k]


class _BothExchanges:
    def __init__(self, first, second):
        self.parts = (first, second)
        self.ins = first.ins + second.ins
        self.out_shapes = first.out_shapes + second.out_shapes
        self.n_sems = first.n_sems + second.n_sems
        self.aliases = {}

    def _each(self, phase, ins, outs, sems):
        i0 = o0 = s0 = 0
        for ex in self.parts:
            n_in, n_out = len(ex.ins), len(ex.out_shapes)
            getattr(ex, phase)(ins[i0:i0 + n_in], outs[o0:o0 + n_out], tuple(_SemaphoresFrom(r, s0) for r in sems))
            i0, o0, s0 = i0 + n_in, o0 + n_out, s0 + ex.n_sems

    def start(self, ins, outs, sems):
        self._each("start", ins, outs, sems)

    def mid(self, ins, outs, sems):
        self._each("mid", ins, outs, sems)

    def finish(self, ins, outs, sems):
        self._each("finish", ins, outs, sems)


def _place_columns(wide, block, chip, name):
    rows, cc = block.shape
    tr = min(512, rows)

    def body(c_ref, b_ref, w_ref, o_ref):
        o_ref[...] = b_ref[...]

    return pl.pallas_call(
        body, name=name,
        grid_spec=pltpu.PrefetchScalarGridSpec(
            num_scalar_prefetch=1, grid=(rows // tr,),
            in_specs=[pl.BlockSpec((tr, cc), lambda r, c_ref: (r, 0)), pl.BlockSpec(memory_space=pl.ANY)],
            out_specs=pl.BlockSpec((tr, cc), lambda r, c_ref: (r, c_ref[0]))),
        out_shape=jax.ShapeDtypeStruct(wide.shape, wide.dtype), input_output_aliases={2: 0},
        compiler_params=_params(("parallel",)),
    )(chip.astype(jnp.int32).reshape(1), block, wide)


def _exchange_call(ex, name):
    n_in, n_out = len(ex.ins), len(ex.out_shapes)

    def body(*refs):
        ins, outs, sems = refs[:n_in], refs[n_in:n_in + n_out], refs[n_in + n_out:]
        ex.start(ins, outs, sems)
        ex.mid(ins, outs, sems)
        ex.finish(ins, outs, sems)

    return _hbm_call(body, name, ex.ins, ex.out_shapes, ex.n_sems, aliases=ex.aliases)


def _grid_call(body, *, name, grid, in_specs, out_specs, out_shape, args, scratch_shapes=(), semantics, exchange=None):
    if exchange is None:
        res = pl.pallas_call(body, name=name, grid=grid, in_specs=list(in_specs), out_specs=list(out_specs),
                             out_shape=list(out_shape), scratch_shapes=list(scratch_shapes),
                             compiler_params=_params(semantics))(*args)
        return res, []
    n_in, n_out, n_scr = len(args), len(out_shape), len(scratch_shapes)
    x_in, x_out = len(exchange.ins), len(exchange.out_shapes)
    steps = math.prod(grid)

    def wrapped(*refs):
        core_in, ex_in = refs[:n_in], refs[n_in:n_in + x_in]
        rest = refs[n_in + x_in:]
        core_out, ex_out = rest[:n_out], rest[n_out:n_out + x_out]
        scratch, sems = rest[n_out + x_out:n_out + x_out + n_scr], rest[n_out + x_out + n_scr:]
        step = 0
        for axis, extent in enumerate(grid):
            step = step * extent + pl.program_id(axis)

        @pl.when(step == 0)
        def _():
            exchange.start(ex_in, ex_out, sems)

        body(*core_in, *core_out, *scratch)

        @pl.when(step == max((3 * steps) // 4 - 1, 0))
        def _():
            exchange.mid(ex_in, ex_out, sems)

        @pl.when(step == steps - 1)
        def _():
            exchange.finish(ex_in, ex_out, sems)

    any_spec = pl.BlockSpec(memory_space=pl.ANY)
    res = pl.pallas_call(
        wrapped, name=name, grid=grid,
        in_specs=list(in_specs) + [any_spec] * x_in, out_specs=list(out_specs) + [any_spec] * x_out,
        out_shape=list(out_shape) + list(exchange.out_shapes),
        input_output_aliases={n_in + a: n_out + b for a, b in exchange.aliases.items()},
        scratch_shapes=list(scratch_shapes) + [pltpu.SemaphoreType.DMA((exchange.n_sems,)),
                                               pltpu.SemaphoreType.DMA((exchange.n_sems,))],
        compiler_params=_params(("arbitrary",) * len(grid)),
    )(*args, *exchange.ins)
    return res[:n_out], res[n_out:]


class _SiblingExchange:
    def __init__(self, parts):
        self.ins = list(parts)
        self.out_shapes = [jax.ShapeDtypeStruct((4, a.shape[1] // 2, a.shape[2]), a.dtype) for a in parts]
        self.n_sems = len(parts)
        self.aliases = {}

    def _copies(self, ins, outs, sems):
        x, y, c, _ = _place()
        return [_remote(src.at[:, _half_rows(src.shape[1], 1 - c)], dst, sems, w, (x, y, 1 - c))
                for w, (src, dst) in enumerate(zip(ins, outs))]

    def start(self, ins, outs, sems):
        for cp in self._copies(ins, outs, sems):
            cp.start()

    def mid(self, ins, outs, sems):
        pass

    def finish(self, ins, outs, sems):
        for cp in self._copies(ins, outs, sems):
            cp.wait_recv()
            cp.wait_send()


class _ScatterExchange:
    def __init__(self, sums):
        self.ins = list(sums)
        self.out_shapes = [jax.ShapeDtypeStruct(a.shape, a.dtype) for a in sums]
        self.n_sems = 3 * len(sums)
        self.aliases = {}

    def _copies(self, ins, outs, sems):
        x, y, c, others = _place()
        me = 2 * x + y
        table = []
        for w, (src, dst) in enumerate(zip(ins, outs)):
            for j, (px, py) in enumerate(others):
                there = 2 * px + py
                send = _remote(src.at[there], dst.at[me], sems, 3 * w + j, (px, py, c))
                landed = _remote(dst.at[there], dst.at[there], sems, 3 * w + j, (px, py, c))
                table.append((send, landed))
        return table

    def start(self, ins, outs, sems):
        for send, _ in self._copies(ins, outs, sems):
            send.start()

    def mid(self, ins, outs, sems):
        pass

    def finish(self, ins, outs, sems):
        table = self._copies(ins, outs, sems)
        for _, landed in table:
            landed.wait_recv()
        for send, _ in table:
            send.wait_send()


def _sibling_join(shards):
    n = len(shards)

    def body(*refs):
        ins, outs, sems = refs[:n], refs[n:2 * n], refs[2 * n:2 * n + 2]
        x, y, c, _ = _place()
        pend = []
        for w in range(n):
            rows = ins[w].shape[1]
            mine, theirs = _half_rows(rows, c), _half_rows(rows, 1 - c)
            cp = _remote(ins[w].at[:, mine], outs[w].at[:, mine], sems, w, (x, y, 1 - c))
            cp.start()
            pend.append((cp, _remote(ins[w].at[:, theirs], outs[w].at[:, theirs], sems, w, (x, y, 1 - c))))
        for cp, landed in pend:
            landed.wait_recv()
            cp.wait_send()

    out_shapes = [jax.ShapeDtypeStruct(a.shape, a.dtype) for a in shards]
    return _hbm_call(body, "grad_sibling_join", shards, out_shapes, n, aliases={w: w for w in range(n)})


SMALL_ROWS = 136


def _all_reduce_small(vec):
    def body(v_ref, o_ref, buf, send, recv, loc):
        x, y, c, _ = _place()
        me = 4 * x + 2 * y + c
        lc = pltpu.make_async_copy(v_ref, buf.at[me], loc.at[0])
        lc.start()
        cps = []
        for k in range(1, 8):
            fx, fy, fc = (k >> 2) & 1, (k >> 1) & 1, k & 1
            peer = (x ^ fx, y ^ fy, c ^ fc)
            cp = pltpu.make_async_remote_copy(src_ref=v_ref, dst_ref=buf.at[me], send_sem=send.at[k - 1],
                                              recv_sem=recv.at[k - 1], device_id=peer, device_id_type=MESH)
            cp.start()
            cps.append((cp, 4 * peer[0] + 2 * peer[1] + peer[2]))
        for k, (cp, src) in enumerate(cps):
            pltpu.make_async_remote_copy(src_ref=v_ref, dst_ref=buf.at[src], send_sem=send.at[k], recv_sem=recv.at[k],
                                         device_id=(x, y, c), device_id_type=MESH).wait_recv()
        for cp, _ in cps:
            cp.wait_send()
        lc.wait()
        total = buf[0]
        for k in range(1, 8):
            total = total + buf[k]
        o_ref[...] = total

    vm = pl.BlockSpec(memory_space=pltpu.VMEM)
    return pl.pallas_call(
        body, name="all_reduce_small", in_specs=[vm], out_specs=vm,
        out_shape=jax.ShapeDtypeStruct(vec.shape, F32),
        scratch_shapes=[pltpu.VMEM((8,) + vec.shape, F32), pltpu.SemaphoreType.DMA((7,)),
                        pltpu.SemaphoreType.DMA((7,)), pltpu.SemaphoreType.DMA((1,))],
    )(vec)


SUM_ROWS = 256


def _add_pairs(part, theirs, name):
    four, rh, cc = theirs.shape
    tr = min(SUM_ROWS, rh)
    halves = part.reshape(four, 2, rh, cc)

    def body(a_ref, b_ref, o_ref):
        mine = a_ref[0, lax.axis_index("c")]
        o_ref[0] = (mine.astype(F32) + b_ref[0].astype(F32)).astype(o_ref.dtype)

    spec = pl.BlockSpec((1, tr, cc), lambda k, r: (k, r, 0))
    return pl.pallas_call(
        body, name=name, grid=(four, rh // tr),
        in_specs=[pl.BlockSpec((1, 2, tr, cc), lambda k, r: (k, 0, r, 0)), spec], out_specs=spec,
        out_shape=jax.ShapeDtypeStruct(theirs.shape, theirs.dtype),
        compiler_params=_params(("parallel", "parallel")),
    )(halves, theirs)


def _sum_chips(own, arrived, core, name, layer, n_layers, into=None):
    four, rh, cc = own.shape
    tr = min(SUM_ROWS, rh)
    nr = rh // tr

    def body(c_ref, own_ref, arr_ref, *rest):
        o_ref = rest[-1]
        x, y = lax.axis_index("x"), lax.axis_index("y")
        tot = own_ref[2 * x + y].astype(F32)
        for px, py in ((1 - x, y), (x, 1 - y), (1 - x, 1 - y)):
            tot = tot + arr_ref[2 * px + py].astype(F32)
        o_ref[0] = tot

    blk = pl.BlockSpec((4, tr, cc), lambda r, c_ref: (0, r, 0))
    in_specs, args, aliases = [blk, blk], [core, own, arrived], {}
    if into is not None:
        in_specs.append(pl.BlockSpec(memory_space=pl.ANY))
        args.append(into)
        aliases = {3: 0}
    return pl.pallas_call(
        body, name=name,
        grid_spec=pltpu.PrefetchScalarGridSpec(
            num_scalar_prefetch=1, grid=(nr,), in_specs=in_specs,
            out_specs=pl.BlockSpec((1, tr, cc), lambda r, c_ref: (layer, c_ref[0] * nr + r, 0))),
        out_shape=jax.ShapeDtypeStruct((n_layers, 2 * rh, cc), F32), input_output_aliases=aliases,
        compiler_params=_params(("parallel",)),
    )(*args)


ADAM_ROWS = 256


def _adamw(w, g, m, v, name):
    shape = w.shape
    as3 = lambda a: a.reshape((-1,) + shape[-2:])
    layers, rows, cc = as3(w).shape
    by_rows = rows % min(ADAM_ROWS, rows) == 0
    tr, tc = (min(ADAM_ROWS, rows), cc) if by_rows else (rows, ADAM_ROWS)
    assert rows % tr == 0 and cc % tc == 0
    c1 = 1.0 - ADAM_B1 ** ADAM_STEP
    c2 = 1.0 - ADAM_B2 ** ADAM_STEP

    def body(w_ref, g_ref, m_ref, v_ref, d_ref, nm_ref, nv_ref):
        gv = g_ref[...]
        nm = ADAM_B1 * m_ref[...] + (1.0 - ADAM_B1) * gv
        nv = ADAM_B2 * v_ref[...] + (1.0 - ADAM_B2) * (gv * gv)
        nm_ref[...] = nm
        nv_ref[...] = nv
        d_ref[...] = -ADAM_LR * ((nm / c1) / (jnp.sqrt(nv / c2) + ADAM_EPS) + ADAM_WD * w_ref[...])

    spec = pl.BlockSpec((1, tr, tc), (lambda l, i: (l, i, 0)) if by_rows else (lambda l, i: (l, 0, i)))
    sh = jax.ShapeDtypeStruct((layers, rows, cc), F32)
    outs = pl.pallas_call(
        body, name=name, grid=(layers, (rows // tr) * (cc // tc)), in_specs=[spec] * 4, out_specs=[spec] * 3,
        out_shape=[sh] * 3,
        compiler_params=_params(("parallel", "parallel")),
    )(as3(w), as3(g), as3(m), as3(v))
    return [o.reshape(shape) for o in outs]


def _pack_small(g_pre, g_post, sinks_a, b_f_c, loss_row):
    pad = lambda a: jnp.pad(a.reshape(1, -1).astype(F32), ((0, 0), (0, LANES - a.size)))
    rows = [g_pre.astype(F32).reshape(-1, LANES), g_post.astype(F32).reshape(-1, LANES), pad(sinks_a), pad(b_f_c), loss_row]
    packed = jnp.concatenate(rows, axis=0)
    return jnp.pad(packed, ((0, SMALL_ROWS - packed.shape[0]), (0, 0)))


def _unpack_small(p):
    n = DEPTH * D_MODEL // LANES
    return (p[:n].reshape(DEPTH, D_MODEL), p[n:2 * n].reshape(DEPTH, D_MODEL), p[2 * n, :2 * N_HEADS].reshape(2, N_HEADS),
            p[2 * n + 1, :N_HEADS].reshape(1, N_HEADS), p[2 * n + 2, 0])


def kernel(x, g_pre, g_post, w_in_a, w_out_a, sinks_a, w_in_b, w_out_b, w_in_c, b_f_c, w_out_c, loss_target, m_g_pre, m_g_post, m_w_in_a, m_w_out_a, m_sinks_a, m_w_in_b, m_w_out_b, m_w_in_c, m_b_f_c, m_w_out_c, v_g_pre, v_g_post, v_w_in_a, v_w_out_a, v_sinks_a, v_w_in_b, v_w_out_b, v_w_in_c, v_b_f_c, v_w_out_c):
    big_w = [w_in_a, w_out_a, w_in_b, w_out_b, w_in_c, w_out_c]
    big_m = [m_w_in_a, m_w_out_a, m_w_in_b, m_w_out_b, m_w_in_c, m_w_out_c]
    big_v = [v_w_in_a, v_w_out_a, v_w_in_b, v_w_out_b, v_w_in_c, v_w_out_c]

    chip = 2 * lax.axis_index("x") + lax.axis_index("y")
    core = lax.axis_index("c").astype(jnp.int32).reshape(1)
    by_kind = {0: (w_in_a, w_out_a), 1: (w_in_b, w_out_b), 2: (w_in_c, w_out_c)}
    shards = {}
    for i in range(DEPTH):
        kind, j = _layer_kind(i)
        shards[("in", i)] = by_kind[kind][0][j].astype(BF16)
        shards[("out", i)] = by_kind[kind][1][j].astype(BF16)

    res = _forward_backward(x[0], loss_target[0], g_pre, g_post, sinks_a, b_f_c, shards, chip, core)
    names = ["w_in_a", "w_out_a", "w_in_b", "w_out_b", "w_in_c", "w_out_c"]
    grads = _sibling_join([res["reduced"][(side, kind)] for kind in range(3) for side in ("in", "out")])

    small = _unpack_small(_all_reduce_small(
        _pack_small(res["g_pre"], res["g_post"], res["sinks_a"], res["b_f_c"], res["loss"])))
    g_small, loss = small[:4], small[4]

    zero_row = jnp.zeros((1, LANES), F32)
    pk = lambda a: _pack_small(a[0], a[1], a[2], a[3], zero_row)
    sm = _adamw(pk([g_pre, g_post, sinks_a, b_f_c]), pk(g_small), pk([m_g_pre, m_g_post, m_sinks_a, m_b_f_c]),
                pk([v_g_pre, v_g_post, v_sinks_a, v_b_f_c]), "adamw_small")
    sm = [_unpack_small(a)[:4] for a in sm]
    turned = lambda a: jnp.swapaxes(a, 1, 2)
    grads = list(grads)
    g_c = lax.optimization_barrier(turned(grads[4]))
    grads[4] = turned(g_c)
    bigs = [[turned(o) for o in _adamw(turned(w), g_c, turned(m), turned(v), f"adamw_{nm}")] if nm == "w_in_c"
            else _adamw(w, g, m, v, f"adamw_{nm}") for w, g, m, v, nm in zip(big_w, grads, big_m, big_v, names)]

    def ordered(small4, big6):
        return [small4[0], small4[1], big6[0], big6[1], small4[2], big6[2], big6[3], big6[4], small4[3], big6[5]]

    out = [loss, res["dx"][None], *ordered(g_small, grads)]
    for k in range(3):
        out += ordered(sm[k], [b[k] for b in bigs])
    return tuple(out)
```

```python
import functools
import math

import numpy as np
import jax
import jax.numpy as jnp
from jax import lax
from jax.experimental import pallas as pl
from jax.experimental.pallas import tpu as pltpu

F32 = jnp.float32
BF16 = jnp.bfloat16

D_MODEL = 2048
DEPTH = 4
N_HEADS = 32
HEAD_DIM = 64
LANES = 128
N_PAIRS = N_HEADS * HEAD_DIM // LANES
BRANCH = N_HEADS * HEAD_DIM
N_KV_A = 4
KV_A = N_KV_A * HEAD_DIM
WINDOW = 128
NORM_EPS = 1e-6
NEG = -1e30
Q_SCALE = HEAD_DIM ** -0.5

A_QKV = BRANCH + 2 * KV_A
B_QKV = 3 * BRANCH

ADAM_LR = 0.001
ADAM_B1 = 0.9
ADAM_B2 = 0.999
ADAM_EPS = 1e-08
ADAM_WD = 0.01
ADAM_STEP = 10

MESH = pl.DeviceIdType.MESH

_NT = (((1,), (1,)), ((), ()))
_TN = (((0,), (0,)), ((), ()))


def _params(sem=None):
    return pltpu.CompilerParams(dimension_semantics=sem)


def _matmul(a, b, *, out_dtype, name, n=None, b_off=0, tm=1024, tn=1024, col_blocks=None, exchange=None):
    (m, k), nn = a.shape, (n or b.shape[1])
    tm, tn = min(tm, m), min(tn, nn)
    assert m % tm == 0 and nn % tn == 0, (name, m, nn, tm, tn)

    def body(a_ref, b_ref, o_ref):
        p = jnp.dot(a_ref[...], b_ref[...], preferred_element_type=F32)
        o_ref[...] = p.astype(o_ref.dtype).reshape(o_ref.shape)

    in_specs = [pl.BlockSpec((tm, k), lambda i, j: (i, 0)), pl.BlockSpec((k, tn), lambda i, j: (0, j + b_off))]
    if col_blocks is None:
        out_spec = pl.BlockSpec((tm, tn), lambda i, j: (i, j))
        out_shape = jax.ShapeDtypeStruct((m, nn), out_dtype)
    else:
        per = nn // col_blocks // tn
        assert per * tn * col_blocks == nn, (name, nn, tn, col_blocks)
        out_spec = pl.BlockSpec((1, tm, tn), lambda i, j: (j // per, i, j % per))
        out_shape = jax.ShapeDtypeStruct((col_blocks, m, nn // col_blocks), out_dtype)
    (res,), arrived = _grid_call(
        body, name=name, grid=(m // tm, nn // tn), in_specs=in_specs, out_specs=[out_spec], out_shape=[out_shape],
        args=(a, b), semantics=("parallel", "parallel"), exchange=exchange)
    return res if exchange is None else (res, arrived)


ROW_TILE = 256


def _row_call(body, name, ins, outs, *, s):
    tr = min(ROW_TILE, s)
    spec = {"row": lambda sh: pl.BlockSpec((tr, sh[1]), lambda i: (i, 0)),
            "vec": lambda sh: pl.BlockSpec((1, sh[1]), lambda i: (0, 0)),
            "col": lambda sh: pl.BlockSpec((sh[0], tr), lambda i: (0, i))}
    in_specs = [spec[kind](a.shape) for a, kind in ins]
    out_specs = [spec[kind](sh.shape) for sh, kind in outs]
    return pl.pallas_call(
        body, name=name, grid=(s // tr,), in_specs=in_specs, out_specs=out_specs,
        out_shape=[sh for sh, _ in outs],
        compiler_params=_params(("arbitrary",)),
    )(*[a for a, _ in ins])


def _rsqrt_ms(v):
    return lax.rsqrt(jnp.mean(v * v, axis=-1, keepdims=True) + NORM_EPS)


def _rmsnorm_fwd(x, g, name):
    s, d = x.shape

    def body(x_ref, g_ref, h_ref, ht_ref):
        xv = x_ref[...]
        h = xv * _rsqrt_ms(xv) * g_ref[...]
        h_ref[...] = h.astype(BF16)
        ht_ref[...] = h.T.astype(BF16)

    return _row_call(body, name, [(x, "row"), (g, "vec")],
                     [(jax.ShapeDtypeStruct((s, d), BF16), "row"), (jax.ShapeDtypeStruct((d, s), BF16), "col")], s=s)


PROJ_ROWS = 256


def _resident(shape):
    return pl.BlockSpec(shape, lambda i: (0,) * len(shape), pipeline_mode=pl.Buffered(1))


def _gated_out_proj(o, z, w_out, x, g, name):
    s, d = x.shape
    tm = min(PROJ_ROWS, s)

    def body(o_ref, z_ref, w_ref, x_ref, g_ref, xn_ref, y_ref, ut_ref):
        zv = z_ref[...]
        u = o_ref[...] * (zv * jax.nn.sigmoid(zv))
        ut_ref[...] = u.T.astype(BF16)
        y = jnp.dot(u.astype(BF16), w_ref[...], preferred_element_type=F32)
        y_ref[...] = y
        xn_ref[...] = x_ref[...] + y * _rsqrt_ms(y) * g_ref[...]

    row = pl.BlockSpec((tm, d), lambda i: (i, 0))
    return pl.pallas_call(
        body, name=name, grid=(s // tm,),
        in_specs=[row, row, _resident(w_out.shape), row, _resident((1, d))],
        out_specs=[row, row, pl.BlockSpec((d, tm), lambda i: (0, i))],
        out_shape=[jax.ShapeDtypeStruct((s, d), F32), jax.ShapeDtypeStruct((s, d), F32), jax.ShapeDtypeStruct((d, s), BF16)],
        compiler_params=_params(("parallel",)),
    )(o, z, w_out, x, g)


def _gated_out_proj_bwd(dx, y, g, w_out, o, z, name, exchange=None):
    s, d = dx.shape
    tm = min(PROJ_ROWS, s)

    def body(dx_ref, y_ref, g_ref, w_ref, o_ref, z_ref, dy_ref, dg_ref, do_ref, dz_ref):
        dy, dg = _norm_bwd_rows(dx_ref[...], y_ref[...], g_ref[...])
        dyb = dy.astype(BF16)
        dy_ref[...] = dyb

        @pl.when(pl.program_id(0) == 0)
        def _():
            dg_ref[...] = jnp.zeros_like(dg_ref)

        dg_ref[...] += jnp.sum(dg, axis=0, keepdims=True)
        du = lax.dot_general(dyb, w_ref[...], _NT, preferred_element_type=F32)
        zv = z_ref[...]
        sig = jax.nn.sigmoid(zv)
        do_ref[...] = (du * (zv * sig)).astype(BF16)
        dz_ref[...] = (du * o_ref[...] * (sig * (1.0 + zv * (1.0 - sig)))).astype(BF16)

    row = pl.BlockSpec((tm, d), lambda i: (i, 0))
    vec = pl.BlockSpec((1, d), lambda i: (0, 0))
    bf = jax.ShapeDtypeStruct((s, d), BF16)
    return _grid_call(
        body, name=name, grid=(s // tm,),
        in_specs=[row, row, _resident((1, d)), _resident(w_out.shape), row, row],
        out_specs=[row, vec, row, row], out_shape=[bf, jax.ShapeDtypeStruct((1, d), F32), bf, bf],
        args=(dx, y, g, w_out, o, z), semantics=("arbitrary",), exchange=exchange)


IN_BWD_ROWS = 512


def _in_proj_bwd(dproj, w_in, extra, dx, x, g, name, tk, exchange=None):
    s, d = x.shape
    k = dproj.shape[1]
    tm = min(IN_BWD_ROWS, s)
    nk = k // tk
    assert k % tk == 0 and s % tm == 0, (name, k, tk)
    has_extra = extra is not None

    def body(a_ref, b_ref, *rest):
        if has_extra:
            e_ref, rest = rest[0], rest[1:]
        dx_ref, x_ref, g_ref, o_ref, dg_ref, acc_ref = rest
        i, kk = pl.program_id(0), pl.program_id(1)
        p = lax.dot_general(a_ref[...], b_ref[...], _NT, preferred_element_type=F32)

        @pl.when(kk == 0)
        def _():
            acc_ref[...] = p

        @pl.when(kk > 0)
        def _():
            acc_ref[...] += p

        @pl.when((i == 0) & (kk == 0))
        def _():
            dg_ref[...] = jnp.zeros_like(dg_ref)

        @pl.when(kk == nk - 1)
        def _():
            def rows_chunk(c, _):
                r = pl.ds(pl.multiple_of(c * LANES, LANES), LANES)
                dh = acc_ref[r, :] + e_ref[r, :] if has_extra else acc_ref[r, :]
                dv, dg = _norm_bwd_rows(dh, x_ref[r, :], g_ref[...])
                o_ref[r, :] = dx_ref[r, :] + dv
                dg_ref[...] += jnp.sum(dg, axis=0, keepdims=True)
                return 0

            lax.fori_loop(0, tm // LANES, rows_chunk, 0)

    row = pl.BlockSpec((tm, d), lambda i, kk: (i, 0))
    vec = pl.BlockSpec((1, d), lambda i, kk: (0, 0))
    in_specs = [pl.BlockSpec((tm, tk), lambda i, kk: (i, kk)), pl.BlockSpec((d, tk), lambda i, kk: (0, kk))]
    args = [dproj, w_in]
    if has_extra:
        in_specs.append(row)
        args.append(extra)
    return _grid_call(
        body, name=name, grid=(s // tm, nk), in_specs=in_specs + [row, row, vec], out_specs=[row, vec],
        out_shape=[jax.ShapeDtypeStruct((s, d), F32), jax.ShapeDtypeStruct((1, d), F32)],
        args=tuple(args) + (dx, x, g), scratch_shapes=[pltpu.VMEM((tm, d), F32)], semantics=("arbitrary", "arbitrary"),
        exchange=exchange)


def _loss_and_grad(x, target):
    s, d = x.shape

    def body(x_ref, t_ref, dx_ref, l_ref):
        err = x_ref[...] - t_ref[...]
        dx_ref[...] = err * (1.0 / d)
        part = jnp.sum(jnp.sum(err * err, axis=1, keepdims=True), axis=0, keepdims=True) * (0.5 / d)

        @pl.when(pl.program_id(0) == 0)
        def _():
            l_ref[...] = jnp.zeros_like(l_ref)

        l_ref[...] += jnp.broadcast_to(part, l_ref.shape)

    return _row_call(body, "loss_head", [(x, "row"), (target, "row")],
                     [(jax.ShapeDtypeStruct((s, d), F32), "row"),
                      (jax.ShapeDtypeStruct((1, LANES), F32), "vec")], s=s)


def _norm_bwd_rows(dn, v, g):
    r = _rsqrt_ms(v)
    a = dn * g
    dv = r * (a - v * (r * r) * jnp.mean(a * v, axis=-1, keepdims=True))
    return dv, dn * v * r


def _lane_is_first_head():
    return lax.broadcasted_iota(jnp.int32, (1, LANES), 1) < HEAD_DIM


def _bcast_lanes(col):
    return jnp.broadcast_to(col, (col.shape[0], LANES))


def _pair_spec(s, off=0, width=LANES):
    return pl.BlockSpec((s, width), lambda p: (0, p + off))


def _stack_heads(pair, first):
    return jnp.concatenate([jnp.where(first, pair, 0), jnp.where(first, 0, pair)], axis=0).astype(BF16)


def _stacked_mask(t, strict):
    row = lax.broadcasted_iota(jnp.int32, (2 * t, t), 0)
    col = lax.broadcasted_iota(jnp.int32, (2 * t, t), 1)
    query = jnp.where(row >= t, row - t, row)
    return col < query if strict else col <= query


LOOP_UNROLL = 2


def _two_at_a_time(n, step, carry):
    def group(jj, c):
        for k in range(LOOP_UNROLL):
            c = step(LOOP_UNROLL * jj + k, c)
        return c

    carry = lax.fori_loop(0, n // LOOP_UNROLL, group, carry)
    return lax.fori_loop(LOOP_UNROLL * (n // LOOP_UNROLL), n, step, carry)


def _rowsum_heads(prod, first):
    return (jnp.sum(jnp.where(first, prod, 0.0), axis=1, keepdims=True),
            jnp.sum(jnp.where(first, 0.0, prod), axis=1, keepdims=True))


def _softplus_parts(z):
    e = jnp.exp(-jnp.abs(z))
    sp = jnp.maximum(z, 0.0) + jnp.log(1.0 + e)
    r = 1.0 / (1.0 + e)
    return sp, jnp.where(z >= 0, r, e * r)


def _sb_tile(s):
    return min(256, s)


def _attn_b_fwd(qkv, name, exchange=None):
    s = qkv.shape[0]
    t = _sb_tile(s)
    nq = s // t

    def body(q_ref, k_ref, v_ref, o_ref, lt_ref):
        first = _lane_is_first_head()
        before = _stacked_mask(t, strict=True)
        tri = (lax.broadcasted_iota(jnp.int32, (t, t), 0) >= lax.broadcasted_iota(jnp.int32, (t, t), 1)).astype(BF16)

        def tile(j, carry, diag, qs):
            c, acc = carry
            c0 = pl.multiple_of(j * t, t)
            k2 = k_ref[pl.ds(c0, t), :]
            v2 = v_ref[pl.ds(c0, t), :]
            z = lax.dot_general(qs, k2, _NT, preferred_element_type=F32)
            sp, _ = _softplus_parts(z)
            lf = jnp.where(before, -sp, 0.0) if diag else -sp
            incl = jnp.dot(lf.astype(BF16), tri, preferred_element_type=F32)
            a = jnp.exp(z + c + incl)
            if diag:
                a = jnp.where(before, a, 0.0)
            pv = jnp.dot(a.astype(BF16), v2, preferred_element_type=F32)
            return c + incl[:, 0:1], acc + jnp.where(first, pv[:t], pv[t:])

        def qblock(i, _):
            r0 = pl.multiple_of(i * t, t)
            qs = _stack_heads(q_ref[pl.ds(r0, t), :] * Q_SCALE, first)
            carry = tile(i, (jnp.zeros((2 * t, 1), F32), jnp.zeros((t, LANES), F32)), True, qs)
            carry = _two_at_a_time(i, lambda j, c: tile(i - 1 - j, c, False, qs), carry)
            o_ref[pl.ds(r0, t), :] = carry[1]
            lt_ref[pl.ds(r0, t), 0:LANES] = _bcast_lanes(carry[0][:t])
            lt_ref[pl.ds(r0, t), LANES:2 * LANES] = _bcast_lanes(carry[0][t:])
            return 0

        lax.fori_loop(0, nq, qblock, 0)

    return _grid_call(
        body, name=name, grid=(N_PAIRS,),
        in_specs=[_pair_spec(s), _pair_spec(s, N_PAIRS), _pair_spec(s, 2 * N_PAIRS)],
        out_specs=[_pair_spec(s), _stat_spec(s)],
        out_shape=[jax.ShapeDtypeStruct((s, BRANCH), F32), jax.ShapeDtypeStruct((s, N_HEADS * LANES), F32)],
        args=(qkv, qkv, qkv), semantics=("parallel",), exchange=exchange)


def _attn_b_bwd(qkv, ltot, do, name, exchange=None):
    s = qkv.shape[0]
    t = _sb_tile(s)
    nq = s // t

    def body(q_ref, k_ref, v_ref, lt_ref, do_ref, dq_ref, dk_ref, dv_ref, dk_acc, dv_acc):
        first = _lane_is_first_head()
        before = _stacked_mask(t, strict=True)
        tri = (lax.broadcasted_iota(jnp.int32, (t, t), 0) <= lax.broadcasted_iota(jnp.int32, (t, t), 1)).astype(BF16)
        dk_acc[...] = jnp.zeros_like(dk_acc)
        dv_acc[...] = jnp.zeros_like(dv_acc)

        def tile(j, carry, diag, qs, dos, lt):
            p_l, p_g, dq_acc = carry
            c0 = pl.multiple_of(j * t, t)
            k2 = k_ref[pl.ds(c0, t), :]
            v2 = v_ref[pl.ds(c0, t), :]
            z = lax.dot_general(qs, k2, _NT, preferred_element_type=F32)
            sp, sig = _softplus_parts(z)
            lf = jnp.where(before, -sp, 0.0) if diag else -sp
            pref_l = jnp.dot(lf.astype(BF16), tri, preferred_element_type=F32)
            a = jnp.exp(z + ((lt - p_l) - pref_l + lf))
            if diag:
                a = jnp.where(before, a, 0.0)
            g = a * lax.dot_general(dos, v2, _NT, preferred_element_type=F32)
            pref_g = jnp.dot(g.astype(BF16), tri, preferred_element_type=F32)
            dz = g - sig * (p_g + pref_g)
            if diag:
                dz = jnp.where(before, dz, 0.0)
            dzb = dz.astype(BF16)
            dq = jnp.dot(dzb, k2, preferred_element_type=F32)
            dk_acc[pl.ds(c0, t), :] += lax.dot_general(dzb, qs, _TN, preferred_element_type=F32)
            dv_acc[pl.ds(c0, t), :] += lax.dot_general(a.astype(BF16), dos, _TN, preferred_element_type=F32)
            return p_l + pref_l[:, t - 1:t], p_g + pref_g[:, t - 1:t], dq_acc + jnp.where(first, dq[:t], dq[t:])

        def qblock(i, _):
            r0 = pl.multiple_of(i * t, t)
            qs = _stack_heads(q_ref[pl.ds(r0, t), :] * Q_SCALE, first)
            dos = _stack_heads(do_ref[pl.ds(r0, t), :], first)
            lt = jnp.concatenate([lt_ref[pl.ds(r0, t), 0:1], lt_ref[pl.ds(r0, t), LANES:LANES + 1]], axis=0)
            zero = jnp.zeros((2 * t, 1), F32)
            carry = (zero, zero, jnp.zeros((t, LANES), F32))
            carry = _two_at_a_time(i, lambda j, c: tile(j, c, False, qs, dos, lt), carry)
            carry = tile(i, carry, True, qs, dos, lt)
            dq_ref[pl.ds(r0, t), :] = (carry[2] * Q_SCALE).astype(BF16)
            return 0

        lax.fori_loop(0, nq, qblock, 0)
        dk_ref[...] = dk_acc[...].astype(BF16)
        dv_ref[...] = dv_acc[...].astype(BF16)

    out = jax.ShapeDtypeStruct((s, BRANCH), BF16)
    return _grid_call(
        body, name=name, grid=(N_PAIRS,),
        in_specs=[_pair_spec(s), _pair_spec(s, N_PAIRS), _pair_spec(s, 2 * N_PAIRS), _stat_spec(s), _pair_spec(s)],
        out_specs=[_pair_spec(s)] * 3, out_shape=[out] * 3,
        scratch_shapes=[pltpu.VMEM((s, LANES), F32), pltpu.VMEM((s, LANES), F32)],
        args=(qkv, qkv, qkv, ltot, do), semantics=("parallel",), exchange=exchange)


def _fox_tile(s):
    return min(256, s)


def _stat_spec(s):
    return pl.BlockSpec((s, 2 * LANES), lambda p: (0, p))


def _cum_spec(nt, t):
    return pl.BlockSpec((1, nt, 2, t), lambda p: (p, 0, 0, 0))


def _attn_c_fwd(qkv, cum4, name, exchange=None):
    s = qkv.shape[0]
    t = _fox_tile(s)
    nq = s // t

    def body(q_ref, k_ref, v_ref, c_ref, o_ref, lse_ref):
        first = _lane_is_first_head()
        causal = _stacked_mask(t, strict=False)

        def tile(j, carry, diag, qs):
            c0 = pl.multiple_of(j * t, t)
            k2 = k_ref[pl.ds(c0, t), :]
            v2 = v_ref[pl.ds(c0, t), :]
            cs = c_ref[0, j]
            m_prev, l_prev, acc = carry
            z = lax.dot_general(qs, k2, _NT, preferred_element_type=F32)
            sc = jnp.concatenate([z[:t] - cs[0:1, :], z[t:] - cs[1:2, :]], axis=0)
            if diag:
                sc = jnp.where(causal, sc, NEG)
            m_new = jnp.maximum(m_prev, jnp.max(sc, axis=1, keepdims=True))
            alpha = jnp.exp(m_prev - m_new)
            p = jnp.exp(sc - m_new)
            l_new = alpha * l_prev + jnp.sum(p, axis=1, keepdims=True)
            pv = jnp.dot(p.astype(BF16), v2, preferred_element_type=F32)
            acc = jnp.where(first, acc * alpha[:t] + pv[:t], acc * alpha[t:] + pv[t:])
            return m_new, l_new, acc

        def qblock(i, _):
            r0 = pl.multiple_of(i * t, t)
            qs = _stack_heads(q_ref[pl.ds(r0, t), :] * Q_SCALE, first)
            carry = (jnp.full((2 * t, 1), NEG, F32), jnp.zeros((2 * t, 1), F32), jnp.zeros((t, LANES), F32))
            carry = _two_at_a_time(i, lambda j, c: tile(j, c, False, qs), carry)
            m, l, acc = tile(i, carry, True, qs)
            inv = 1.0 / l
            lse = m + jnp.log(l)
            o_ref[pl.ds(r0, t), :] = acc * jnp.where(first, inv[:t], inv[t:])
            lse_ref[pl.ds(r0, t), 0:LANES] = _bcast_lanes(lse[:t])
            lse_ref[pl.ds(r0, t), LANES:2 * LANES] = _bcast_lanes(lse[t:])
            return 0

        lax.fori_loop(0, nq, qblock, 0)

    return _grid_call(
        body, name=name, grid=(N_PAIRS,),
        in_specs=[_pair_spec(s), _pair_spec(s, N_PAIRS), _pair_spec(s, 2 * N_PAIRS), _cum_spec(nq, t)],
        out_specs=[_pair_spec(s), _stat_spec(s)],
        out_shape=[jax.ShapeDtypeStruct((s, BRANCH), F32), jax.ShapeDtypeStruct((s, N_HEADS * LANES), F32)],
        args=(qkv, qkv, qkv, cum4), semantics=("parallel",), exchange=exchange)


def _attn_c_bwd(qkv, cum4, o, lse, do, name, exchange=None):
    s = qkv.shape[0]
    t = _fox_tile(s)
    nq = s // t

    def body(q_ref, k_ref, v_ref, c_ref, o_ref, lse_ref, do_ref, dq_ref, dk_ref, dv_ref, dc_ref, dk_acc, dv_acc):
        first = _lane_is_first_head()
        causal = _stacked_mask(t, strict=False)
        eye = lax.broadcasted_iota(jnp.int32, (t, t), 0) == lax.broadcasted_iota(jnp.int32, (t, t), 1)
        dk_acc[...] = jnp.zeros_like(dk_acc)
        dv_acc[...] = jnp.zeros_like(dv_acc)
        dc_ref[...] = jnp.zeros_like(dc_ref)

        def tile(j, carry, diag, qs, dos, delta, lse):
            dq_acc, rs = carry
            c0 = pl.multiple_of(j * t, t)
            k2 = k_ref[pl.ds(c0, t), :]
            v2 = v_ref[pl.ds(c0, t), :]
            cs = c_ref[0, j]
            z = lax.dot_general(qs, k2, _NT, preferred_element_type=F32)
            sc = jnp.concatenate([z[:t] - cs[0:1, :], z[t:] - cs[1:2, :]], axis=0)
            p = jnp.exp(sc - lse)
            if diag:
                p = jnp.where(causal, p, 0.0)
            ds = p * (lax.dot_general(dos, v2, _NT, preferred_element_type=F32) - delta)
            dsb = ds.astype(BF16)
            dq = jnp.dot(dsb, k2, preferred_element_type=F32)
            dk_acc[pl.ds(c0, t), :] += lax.dot_general(dsb, qs, _TN, preferred_element_type=F32)
            dv_acc[pl.ds(c0, t), :] += lax.dot_general(p.astype(BF16), dos, _TN, preferred_element_type=F32)
            col_sums = jnp.concatenate([jnp.sum(ds[:t], axis=0, keepdims=True), jnp.sum(ds[t:], axis=0, keepdims=True)], axis=0)
            dc_ref[0, j] = dc_ref[0, j] - col_sums
            return dq_acc + jnp.where(first, dq[:t], dq[t:]), rs + jnp.sum(ds, axis=1, keepdims=True)

        def qblock(i, _):
            r0 = pl.multiple_of(i * t, t)
            do2 = do_ref[pl.ds(r0, t), :]
            qs = _stack_heads(q_ref[pl.ds(r0, t), :] * Q_SCALE, first)
            dos = _stack_heads(do2, first)
            delta = jnp.concatenate(_rowsum_heads(do2.astype(F32) * o_ref[pl.ds(r0, t), :], first), axis=0)
            lse = jnp.concatenate([lse_ref[pl.ds(r0, t), 0:1], lse_ref[pl.ds(r0, t), LANES:LANES + 1]], axis=0)
            carry = (jnp.zeros((t, LANES), F32), jnp.zeros((2 * t, 1), F32))
            carry = _two_at_a_time(i, lambda j, c: tile(j, c, False, qs, dos, delta, lse), carry)
            dq_acc, rs = tile(i, carry, True, qs, dos, delta, lse)
            dq_ref[pl.ds(r0, t), :] = (dq_acc * Q_SCALE).astype(BF16)
            as_row = lambda col_vec: jnp.sum(jnp.where(eye, col_vec, 0.0), axis=0, keepdims=True)
            dc_ref[0, i] = dc_ref[0, i] + jnp.concatenate([as_row(rs[:t]), as_row(rs[t:])], axis=0)
            return 0

        lax.fori_loop(0, nq, qblock, 0)
        dk_ref[...] = dk_acc[...].astype(BF16)
        dv_ref[...] = dv_acc[...].astype(BF16)

    out = jax.ShapeDtypeStruct((s, BRANCH), BF16)
    return _grid_call(
        body, name=name, grid=(N_PAIRS,),
        in_specs=[_pair_spec(s), _pair_spec(s, N_PAIRS), _pair_spec(s, 2 * N_PAIRS), _cum_spec(nq, t),
                  _pair_spec(s), _stat_spec(s), _pair_spec(s)],
        out_specs=[_pair_spec(s)] * 3 + [_cum_spec(nq, t)],
        out_shape=[out] * 3 + [jax.ShapeDtypeStruct(cum4.shape, F32)],
        scratch_shapes=[pltpu.VMEM((s, LANES), F32), pltpu.VMEM((s, LANES), F32)],
        args=(qkv, qkv, qkv, cum4, o, lse, do), semantics=("parallel",), exchange=exchange)


FG_CHUNK = 512


def _tri_dot3(x, t):
    hi = x.astype(BF16)
    r1 = x - hi.astype(F32)
    mid = r1.astype(BF16)
    lo = (r1 - mid.astype(F32)).astype(BF16)
    return (jnp.dot(hi, t, preferred_element_type=F32) + jnp.dot(mid, t, preferred_element_type=F32)
            + jnp.dot(lo, t, preferred_element_type=F32))


def _fgate_fwd(h, wf_t, b_col, name):
    s = h.shape[0]
    c = min(FG_CHUNK, s)

    def body(h_ref, w_ref, b_ref, xf_ref, cum_ref, carry_ref):
        @pl.when(pl.program_id(0) == 0)
        def _():
            carry_ref[...] = jnp.zeros_like(carry_ref)

        xf = lax.dot_general(w_ref[...], h_ref[...], _NT, preferred_element_type=F32) + b_ref[:, 0:1]
        xf_ref[...] = xf
        logf = jnp.minimum(xf, 0.0) - jnp.log(1.0 + jnp.exp(-jnp.abs(xf)))
        row = lax.broadcasted_iota(jnp.int32, (c, c), 0)
        col = lax.broadcasted_iota(jnp.int32, (c, c), 1)
        cum = _tri_dot3(logf, (row <= col).astype(BF16)) + carry_ref[:, 0:1]
        cum_ref[...] = cum
        carry_ref[...] = _bcast_lanes(cum[:, c - 1:c])

    out = jax.ShapeDtypeStruct((N_HEADS, s), F32)
    return pl.pallas_call(
        body, name=name, grid=(s // c,),
        in_specs=[pl.BlockSpec((c, D_MODEL), lambda i: (i, 0)),
                  pl.BlockSpec((N_HEADS, D_MODEL), lambda i: (0, 0)),
                  pl.BlockSpec((N_HEADS, LANES), lambda i: (0, 0))],
        out_specs=[pl.BlockSpec((N_HEADS, c), lambda i: (0, i))] * 2,
        out_shape=[out, out],
        scratch_shapes=[pltpu.VMEM((N_HEADS, LANES), F32)],
        compiler_params=_params(("arbitrary",)),
    )(h, wf_t, b_col)


def _fgate_bwd(dcum, xf, h, wf_t, name):
    s = h.shape[0]
    c = min(FG_CHUNK, s)
    n = s // c

    def body(dc_ref, xf_ref, h_ref, w_ref, dw_ref, dh_ref, db_ref, carry_ref):
        @pl.when(pl.program_id(0) == 0)
        def _():
            carry_ref[...] = jnp.zeros_like(carry_ref)
            dw_ref[...] = jnp.zeros_like(dw_ref)
            db_ref[...] = jnp.zeros_like(db_ref)

        row = lax.broadcasted_iota(jnp.int32, (c, c), 0)
        col = lax.broadcasted_iota(jnp.int32, (c, c), 1)
        dlogf = _tri_dot3(dc_ref[...], (row >= col).astype(BF16)) + carry_ref[:, 0:1]
        carry_ref[...] = _bcast_lanes(dlogf[:, 0:1])
        xf = xf_ref[...]
        e = jnp.exp(-jnp.abs(xf))
        r = 1.0 / (1.0 + e)
        dxf = dlogf * jnp.where(xf >= 0, e * r, r)
        db_ref[...] += _bcast_lanes(jnp.sum(dxf, axis=1, keepdims=True))
        dxb = dxf.astype(BF16)
        dw_ref[...] += jnp.dot(dxb, h_ref[...], preferred_element_type=F32)
        dh_ref[...] = lax.dot_general(dxb, w_ref[...], _TN, preferred_element_type=F32)

    rev = lambda i: n - 1 - i
    return pl.pallas_call(
        body, name=name, grid=(n,),
        in_specs=[pl.BlockSpec((N_HEADS, c), lambda i: (0, rev(i))),
                  pl.BlockSpec((N_HEADS, c), lambda i: (0, rev(i))),
                  pl.BlockSpec((c, D_MODEL), lambda i: (rev(i), 0)),
                  pl.BlockSpec((N_HEADS, D_MODEL), lambda i: (0, 0))],
        out_specs=[pl.BlockSpec((N_HEADS, D_MODEL), lambda i: (0, 0)),
                   pl.BlockSpec((c, D_MODEL), lambda i: (rev(i), 0)),
                   pl.BlockSpec((N_HEADS, LANES), lambda i: (0, 0))],
        out_shape=[jax.ShapeDtypeStruct((N_HEADS, D_MODEL), F32), jax.ShapeDtypeStruct((s, D_MODEL), F32),
                   jax.ShapeDtypeStruct((N_HEADS, LANES), F32)],
        scratch_shapes=[pltpu.VMEM((N_HEADS, LANES), F32)],
        compiler_params=_params(("arbitrary",)),
    )(dcum, xf, h, wf_t)


def _to_cum4(v, t):
    s = v.shape[1]
    return v.reshape(N_PAIRS, 2, s // t, t).transpose(0, 2, 1, 3)


def _from_cum4(v4):
    p, nt, two, t = v4.shape
    return v4.transpose(0, 2, 1, 3).reshape(p * two, nt * t)


def _alibi_slopes():
    return (2.0 ** (-8.0 * np.arange(1, N_HEADS + 1, dtype=np.float32) / N_HEADS)).astype(np.float32)


def _per_head_lanes(v):
    return jnp.repeat(v.astype(F32).reshape(N_PAIRS, 1, 2), LANES, axis=2)


def _attn_a_specs(s):
    q = _pair_spec(s)
    k = pl.BlockSpec((s, LANES), lambda p: (0, N_PAIRS + p // 8))
    v = pl.BlockSpec((s, LANES), lambda p: (0, N_PAIRS + KV_A // LANES + p // 8))
    head = pl.BlockSpec((1, 1, 2 * LANES), lambda p: (p, 0, 0))
    return q, k, v, head


def _attn_a_geometry(p, slope_ref, sink_ref):
    kv_half = (p // 4) % 2
    kv_first = kv_half == 0
    lane_first = _lane_is_first_head()
    kv_lanes = (lax.broadcasted_iota(jnp.int32, (1, LANES), 1) // HEAD_DIM) == kv_half
    row = lax.broadcasted_iota(jnp.int32, (2 * WINDOW, 2 * WINDOW), 0)
    cj = lax.broadcasted_iota(jnp.int32, (2 * WINDOW, 2 * WINDOW), 1)
    second = row >= WINDOW
    dist = WINDOW + jnp.where(second, row - WINDOW, row) - cj
    valid = (dist >= 0) & (dist < WINDOW)
    per_row = lambda ref: jnp.where(second[:, 0:1], ref[0, :, LANES:LANES + 1], ref[0, :, 0:1])
    return kv_first, lane_first, kv_lanes, per_row(slope_ref) * dist.astype(F32), valid, per_row(sink_ref)


def _swap_halves(x):
    return pltpu.roll(x, HEAD_DIM, 1)


def _attn_a_fwd(qkv, slopes, sinks, name, exchange=None):
    s = qkv.shape[0]
    nb = s // WINDOW

    def body(q_ref, k_ref, v_ref, sl_ref, sk_ref, o_ref, lse_ref):
        kv_first, lane_first, kv_lanes, bias, valid, sink = _attn_a_geometry(pl.program_id(0), sl_ref, sk_ref)

        def block(r0, k0, width):
            q2 = q_ref[pl.ds(r0, WINDOW), :].astype(F32) * Q_SCALE
            q2r = _swap_halves(q2)
            xs = jnp.concatenate([jnp.where(kv_first, q2, q2r), jnp.where(kv_first, q2r, q2)], axis=0).astype(BF16)
            km = jnp.where(kv_lanes, k_ref[pl.ds(k0, width), :], 0).astype(BF16)
            vm = jnp.where(kv_lanes, v_ref[pl.ds(k0, width), :], 0).astype(BF16)
            sc = lax.dot_general(xs, km, _NT, preferred_element_type=F32) - bias[:, 2 * WINDOW - width:]
            sc = jnp.where(valid[:, 2 * WINDOW - width:], sc, NEG)
            m = jnp.maximum(jnp.max(sc, axis=1, keepdims=True), sink)
            pr = jnp.exp(sc - m)
            l = jnp.sum(pr, axis=1, keepdims=True) + jnp.exp(sink - m)
            os = jnp.dot(pr.astype(BF16), vm, preferred_element_type=F32) * (1.0 / l)
            lse = m + jnp.log(l)
            lse_ref[pl.ds(r0, WINDOW), 0:LANES] = _bcast_lanes(lse[:WINDOW])
            lse_ref[pl.ds(r0, WINDOW), LANES:2 * LANES] = _bcast_lanes(lse[WINDOW:])
            oa = jnp.where(kv_first, os[:WINDOW], _swap_halves(os[:WINDOW]))
            ob = jnp.where(kv_first, _swap_halves(os[WINDOW:]), os[WINDOW:])
            o_ref[pl.ds(r0, WINDOW), :] = jnp.where(lane_first, oa, ob)

        block(0, 0, WINDOW)

        def loop(n, _):
            r0 = pl.multiple_of(n * WINDOW, WINDOW)
            block(r0, pl.multiple_of(r0 - WINDOW, WINDOW), 2 * WINDOW)
            return 0

        _two_at_a_time(nb - 1, lambda n, c: loop(n + 1, c), 0)

    q, k, v, head = _attn_a_specs(s)
    return _grid_call(
        body, name=name, grid=(N_PAIRS,),
        in_specs=[q, k, v, head, head],
        out_specs=[_pair_spec(s), _stat_spec(s)],
        out_shape=[jax.ShapeDtypeStruct((s, BRANCH), F32), jax.ShapeDtypeStruct((s, N_HEADS * LANES), F32)],
        args=(qkv, qkv, qkv, slopes, sinks), semantics=("parallel",), exchange=exchange)


def _attn_a_bwd(qkv, slopes, sinks, o, lse, do, name, exchange=None):
    s = qkv.shape[0]
    nb = s // WINDOW

    def body(q_ref, k_ref, v_ref, sl_ref, sk_ref, o_ref, lse_ref, do_ref, dq_ref, dk_ref, dv_ref, dsk_ref):
        p_id = pl.program_id(0)
        kv_first, lane_first, kv_lanes, bias, valid, sink = _attn_a_geometry(p_id, sl_ref, sk_ref)

        @pl.when(p_id % 8 == 0)
        def _():
            dk_ref[...] = jnp.zeros_like(dk_ref)
            dv_ref[...] = jnp.zeros_like(dv_ref)

        def align(v2):
            v2r = _swap_halves(v2)
            both = jnp.concatenate([jnp.where(kv_first, v2, v2r), jnp.where(kv_first, v2r, v2)], axis=0)
            return jnp.where(kv_lanes, both, 0.0).astype(BF16)

        def block(r0, k0, width, sink_sum):
            xq = align(q_ref[pl.ds(r0, WINDOW), :].astype(F32) * Q_SCALE)
            do2 = do_ref[pl.ds(r0, WINDOW), :].astype(F32)
            xdo = align(do2)
            delta = jnp.concatenate(_rowsum_heads(do2 * o_ref[pl.ds(r0, WINDOW), :], lane_first), axis=0)
            lse = jnp.concatenate([lse_ref[pl.ds(r0, WINDOW), 0:1], lse_ref[pl.ds(r0, WINDOW), LANES:LANES + 1]], axis=0)
            km = jnp.where(kv_lanes, k_ref[pl.ds(k0, width), :], 0).astype(BF16)
            vm = jnp.where(kv_lanes, v_ref[pl.ds(k0, width), :], 0).astype(BF16)
            sc = lax.dot_general(xq, km, _NT, preferred_element_type=F32) - bias[:, 2 * WINDOW - width:]
            pr = jnp.where(valid[:, 2 * WINDOW - width:], jnp.exp(sc - lse), 0.0)
            ds = pr * (lax.dot_general(xdo, vm, _NT, preferred_element_type=F32) - delta)
            dsb = ds.astype(BF16)
            dq_al = jnp.dot(dsb, km, preferred_element_type=F32)
            dk_ref[pl.ds(k0, width), :] += lax.dot_general(dsb, xq, _TN, preferred_element_type=F32)
            dv_ref[pl.ds(k0, width), :] += lax.dot_general(pr.astype(BF16), xdo, _TN, preferred_element_type=F32)
            dqa = jnp.where(kv_first, dq_al[:WINDOW], _swap_halves(dq_al[:WINDOW]))
            dqb = jnp.where(kv_first, _swap_halves(dq_al[WINDOW:]), dq_al[WINDOW:])
            dq_ref[pl.ds(r0, WINDOW), :] = (jnp.where(lane_first, dqa, dqb) * Q_SCALE).astype(BF16)
            return sink_sum + jnp.exp(sink - lse) * delta

        sink_sum = block(0, 0, WINDOW, jnp.zeros((2 * WINDOW, 1), F32))

        def loop(n, c):
            r0 = pl.multiple_of(n * WINDOW, WINDOW)
            return block(r0, pl.multiple_of(r0 - WINDOW, WINDOW), 2 * WINDOW, c)

        sink_sum = _two_at_a_time(nb - 1, lambda n, c: loop(n + 1, c), sink_sum)
        dsk_ref[0, :, 0:LANES] = jnp.broadcast_to(-jnp.sum(sink_sum[:WINDOW], axis=0, keepdims=True), (1, LANES))
        dsk_ref[0, :, LANES:2 * LANES] = jnp.broadcast_to(-jnp.sum(sink_sum[WINDOW:], axis=0, keepdims=True), (1, LANES))

    q, k, v, head = _attn_a_specs(s)
    kv_out = pl.BlockSpec((s, LANES), lambda p: (0, p // 8))
    return _grid_call(
        body, name=name, grid=(N_PAIRS,),
        in_specs=[q, k, v, head, head, _pair_spec(s), _stat_spec(s), _pair_spec(s)],
        out_specs=[_pair_spec(s), kv_out, kv_out, head],
        out_shape=[jax.ShapeDtypeStruct((s, BRANCH), BF16), jax.ShapeDtypeStruct((s, KV_A), F32),
                   jax.ShapeDtypeStruct((s, KV_A), F32), jax.ShapeDtypeStruct((N_PAIRS, 1, 2 * LANES), F32)],
        args=(qkv, qkv, qkv, slopes, sinks, o, lse, do), semantics=("arbitrary",), exchange=exchange)


def _layer_kind(i):
    return i % 3, i // 3


GATHER_FIRST = [("in", 0)]
GATHER_BEHIND = {("qkv", 0): [("out", 0)], ("attn", 0): [("in", 1)], ("attn", 1): [("out", 1), ("in", 2), ("out", 2)],
                 ("attn", 2): [("in", 3), ("out", 3)]}


def _forward_backward(x, target, g_pre, g_post, sinks_a, b_f_c, shards, chip, place):
    s = x.shape[0]
    slopes = _per_head_lanes(jnp.asarray(_alibi_slopes()))
    w_in, w_out, wf_t = {}, {}, {}

    def lands_side_by_side(key):
        return key[0] == "in" and shards[key].shape[1] % LANES == 0

    def gather(keys):
        return _GatherExchange([shards[k] for k in keys], [lands_side_by_side(k) for k in keys])

    def deliver(keys, gathered):
        for key, g in zip(keys, gathered):
            side, layer = key
            sh = shards[key]
            if side == "out":
                g = lax.dynamic_update_slice(g, sh[None], (chip, 0, 0))
                w_out[layer] = g.reshape(4 * sh.shape[0], sh.shape[1])
            elif lands_side_by_side(key):
                w_in[layer] = _place_columns(g, sh, chip, f"own_block_in_l{layer}")
            else:
                g = lax.dynamic_update_slice(g, sh[None], (chip, 0, 0))
                w = g.transpose(1, 0, 2).reshape(sh.shape[0], 4 * sh.shape[1])
                w_in[layer], wf_t[layer] = w[:, :4 * BRANCH], w[:, 4 * BRANCH:].T

    deliver(GATHER_FIRST, _exchange_call(gather(GATHER_FIRST), "gather_first_weights"))
    saved = []
    for i in range(DEPTH):
        kind, j = _layer_kind(i)
        tag = f"l{i}"
        w = w_in[i]
        nqkv = A_QKV if kind == 0 else B_QKV
        tn = 512 if kind == 0 else 1024
        h, h_t = _rmsnorm_fwd(x, g_pre[i:i + 1], f"prenorm_{tag}")
        behind = GATHER_BEHIND.get(("qkv", i))
        qkv = _matmul(h, w, out_dtype=BF16, name=f"inproj_qkv_{tag}", n=nqkv, tn=tn,
                      exchange=gather(behind) if behind else None)
        if behind:
            qkv, arrived = qkv
            deliver(behind, arrived)
        z = _matmul(h, w, out_dtype=F32, name=f"inproj_gate_{tag}", n=BRANCH, b_off=nqkv // tn, tn=tn)
        behind = GATHER_BEHIND.get(("attn", i))
        exchange = gather(behind) if behind else None
        if kind == 0:
            sink_l = _per_head_lanes(sinks_a[j])
            (o, lse), arrived = _attn_a_fwd(qkv, slopes, sink_l, f"attn_a_fwd_{tag}", exchange)
            extra = (sink_l, lse)
        elif kind == 1:
            (o, extra), arrived = _attn_b_fwd(qkv, f"attn_b_fwd_{tag}", exchange)
        else:
            b_col = jnp.broadcast_to(b_f_c[j].astype(F32)[:, None], (N_HEADS, LANES))
            xf, cum = _fgate_fwd(h, wf_t[i], b_col, f"fgate_fwd_{tag}")
            cum4 = _to_cum4(cum, _fox_tile(s))
            (o, lse), arrived = _attn_c_fwd(qkv, cum4, f"attn_c_fwd_{tag}", exchange)
            extra = (xf, cum4, lse)
        if behind:
            deliver(behind, arrived)
        x_next, y, u_t = _gated_out_proj(o, z, w_out[i], x, g_post[i:i + 1], f"outproj_{tag}")
        saved.append((x, h, h_t, qkv, z, o, u_t, y, extra))
        x = x_next

    dx, loss_part = _loss_and_grad(x, target)

    d_g_pre, d_g_post = [None] * DEPTH, [None] * DEPTH
    d_sinks = [None, None]
    d_b_f = None
    reduced = {}
    pending = None

    def finish_reduce(layer, side, own, arr):
        kind, j = _layer_kind(layer)
        reduced[(side, kind)] = _sum_chips(own, arr, place, f"shard_sum_{side}_l{layer}", j, 2 if kind == 0 else 1,
                                           into=reduced.get((side, kind)))

    for i in reversed(range(DEPTH)):
        kind, j = _layer_kind(i)
        tag = f"l{i}"
        x_in, h, h_t, qkv, z, o, u_t, y, extra = saved[i]
        tn = 512 if kind == 0 else 1024
        (dy, d_g_post[i], do, dz), _ = _gated_out_proj_bwd(dx, y, g_post[i:i + 1], w_out[i], o, z, f"outproj_bwd_{tag}")
        dw_out = _matmul(u_t, dy, out_dtype=BF16, name=f"dw_out_{tag}")
        dw_out = dw_out.reshape(4, dw_out.shape[0] // 4, dw_out.shape[1])
        dh_f = None
        exchange = _SiblingExchange([dw_out])
        if pending:
            exchange = _BothExchanges(exchange, _ScatterExchange([pending[1]]))
        if kind == 0:
            sink_l, lse = extra
            (dq, dk, dv, dsk), arrived = _attn_a_bwd(qkv, slopes, sink_l, o, lse, do, f"attn_a_bwd_{tag}", exchange)
            d_sinks[j] = dsk[:, 0, ::LANES].reshape(N_HEADS)
            parts = [dq, dk.astype(BF16), dv.astype(BF16), dz]
        elif kind == 1:
            (dq, dk, dv), arrived = _attn_b_bwd(qkv, extra, do, f"attn_b_bwd_{tag}", exchange)
            parts = [dq, dk, dv, dz]
        else:
            xf, cum4, lse = extra
            (dq, dk, dv, dcum4), arrived = _attn_c_bwd(qkv, cum4, o, lse, do, f"attn_c_bwd_{tag}", exchange)
            d_wf_t, dh_f, db = _fgate_bwd(_from_cum4(dcum4), xf, h, wf_t[i], f"fgate_bwd_{tag}")
            d_b_f = db[:, 0]
            parts = [dq, dk, dv, dz]
        sum_out = _add_pairs(dw_out, arrived[0], place, f"chip_sum_out_{tag}")
        if pending:
            finish_reduce(pending[0], "in", pending[1], arrived[1])
        dproj = jnp.concatenate(parts, axis=1)
        scatter_out = _ScatterExchange([sum_out])
        if kind == 2:
            dw_in, arrived = _matmul(h_t, dproj, out_dtype=F32, name=f"dw_in_{tag}", tn=tn, exchange=scatter_out)
            dw_in = jnp.concatenate([dw_in, d_wf_t.T], axis=1)
            dw_in = dw_in.reshape(dw_in.shape[0], 4, dw_in.shape[1] // 4).transpose(1, 0, 2).astype(BF16)
        else:
            dw_in, arrived = _matmul(h_t, dproj, out_dtype=BF16, name=f"dw_in_{tag}", col_blocks=4,
                                     tn=1152 if kind == 0 else 1024, exchange=scatter_out)
        finish_reduce(i, "out", sum_out, arrived[0])
        (dx, d_g_pre[i]), (their_in,) = _in_proj_bwd(
            dproj, w_in[i], dh_f, dx, x_in, g_pre[i:i + 1], f"inproj_bwd_{tag}", 1536 if kind == 0 else 1024,
            exchange=_SiblingExchange([dw_in]))
        pending = (i, _add_pairs(dw_in, their_in, place, f"chip_sum_in_{tag}"))

    arrived = _exchange_call(_ScatterExchange([pending[1]]), "grad_chip_scatter_last")
    finish_reduce(pending[0], "in", pending[1], arrived[0])

    return dict(loss=loss_part, dx=dx, g_pre=jnp.concatenate(d_g_pre, axis=0), g_post=jnp.concatenate(d_g_post, axis=0),
                sinks_a=jnp.stack(d_sinks), b_f_c=d_b_f[None, :], reduced=reduced)


def _place():
    x, y, c = lax.axis_index("x"), lax.axis_index("y"), lax.axis_index("c")
    others = [(1 - x, y), (x, 1 - y), (1 - x, 1 - y)]
    return x, y, c, others


def _half_rows(ref_rows, which):
    half = ref_rows // 2
    return pl.ds(pl.multiple_of(which * half, half), half)


def _remote(src, dst, sems, k, device):
    send, recv = sems
    return pltpu.make_async_remote_copy(src_ref=src, dst_ref=dst, send_sem=send.at[k], recv_sem=recv.at[k],
                                        device_id=device, device_id_type=MESH)


def _hbm_call(body, name, ins, out_shapes, n_remote, aliases=None):
    any_spec = pl.BlockSpec(memory_space=pl.ANY)
    return pl.pallas_call(
        body, name=name, in_specs=[any_spec] * len(ins), out_specs=[any_spec] * len(out_shapes),
        out_shape=out_shapes, input_output_aliases=aliases or {},
        scratch_shapes=[pltpu.SemaphoreType.DMA((n_remote,)), pltpu.SemaphoreType.DMA((n_remote,))],
    )(*ins)


class _GatherExchange:
    def __init__(self, shards, side_by_side):
        self.ins = list(shards)
        self.side_by_side = list(side_by_side)
        self.out_shapes = [jax.ShapeDtypeStruct((a.shape[0], 4 * a.shape[1]) if wide else (4,) + a.shape, a.dtype)
                           for a, wide in zip(shards, side_by_side)]
        self.n_sems = 6 * len(shards)
        self.aliases = {}

    def _copies(self, ins, outs, sems):
        x, y, c, others = _place()
        me = 2 * x + y
        table = []
        for w, (src, dst, wide) in enumerate(zip(ins, outs, self.side_by_side)):
            rows, cols = src.shape
            mine, theirs = _half_rows(rows, c), _half_rows(rows, 1 - c)

            def slot(chip, which, dst=dst, wide=wide, cols=cols):
                return dst.at[which, pl.ds(pl.multiple_of(chip * cols, LANES), cols)] if wide else dst.at[chip, which]

            for j, (px, py) in enumerate(others):
                there = 2 * px + py
                send = _remote(src.at[mine], slot(me, mine), sems, 6 * w + j, (px, py, c))
                landed = _remote(slot(there, mine), slot(there, mine), sems, 6 * w + j, (px, py, c))
                passed = _remote(slot(there, mine), slot(there, mine), sems, 6 * w + 3 + j, (x, y, 1 - c))
                from_sibling = _remote(slot(there, theirs), slot(there, theirs), sems, 6 * w + 3 + j, (x, y, 1 - c))
                table.append((send, landed, passed, from_sibling))
        return table

    def start(self, ins, outs, sems):
        for send, _, _, _ in self._copies(ins, outs, sems):
            send.start()

    def mid(self, ins, outs, sems):
        for _, landed, passed, _ in self._copies(ins, outs, sems):
            landed.wait_recv()
            passed.start()

    def finish(self, ins, outs, sems):
        table = self._copies(ins, outs, sems)
        for _, _, _, from_sibling in table:
            from_sibling.wait_recv()
        for send, _, passed, _ in table:
            send.wait_send()
            passed.wait_send()


class _SemaphoresFrom:
    def __init__(self, ref, start):
        self._ref, self._start = ref, start

    @property
    def at(self):
        return self

    def __getitem__(self, k):
        return self._ref.at[self._start + k]


class _BothExchanges:
    def __init__(self, first, second):
        self.parts = (first, second)
        self.ins = first.ins + second.ins
        self.out_shapes = first.out_shapes + second.out_shapes
        self.n_sems = first.n_sems + second.n_sems
        self.aliases = {}

    def _each(self, phase, ins, outs, sems):
        i0 = o0 = s0 = 0
        for ex in self.parts:
            n_in, n_out = len(ex.ins), len(ex.out_shapes)
            getattr(ex, phase)(ins[i0:i0 + n_in], outs[o0:o0 + n_out], tuple(_SemaphoresFrom(r, s0) for r in sems))
            i0, o0, s0 = i0 + n_in, o0 + n_out, s0 + ex.n_sems

    def start(self, ins, outs, sems):
        self._each("start", ins, outs, sems)

    def mid(self, ins, outs, sems):
        self._each("mid", ins, outs, sems)

    def finish(self, ins, outs, sems):
        self._each("finish", ins, outs, sems)


def _place_columns(wide, block, chip, name):
    rows, cc = block.shape
    tr = min(512, rows)

    def body(c_ref, b_ref, w_ref, o_ref):
        o_ref[...] = b_ref[...]

    return pl.pallas_call(
        body, name=name,
        grid_spec=pltpu.PrefetchScalarGridSpec(
            num_scalar_prefetch=1, grid=(rows // tr,),
            in_specs=[pl.BlockSpec((tr, cc), lambda r, c_ref: (r, 0)), pl.BlockSpec(memory_space=pl.ANY)],
            out_specs=pl.BlockSpec((tr, cc), lambda r, c_ref: (r, c_ref[0]))),
        out_shape=jax.ShapeDtypeStruct(wide.shape, wide.dtype), input_output_aliases={2: 0},
        compiler_params=_params(("parallel",)),
    )(chip.astype(jnp.int32).reshape(1), block, wide)


def _exchange_call(ex, name):
    n_in, n_out = len(ex.ins), len(ex.out_shapes)

    def body(*refs):
        ins, outs, sems = refs[:n_in], refs[n_in:n_in + n_out], refs[n_in + n_out:]
        ex.start(ins, outs, sems)
        ex.mid(ins, outs, sems)
        ex.finish(ins, outs, sems)

    return _hbm_call(body, name, ex.ins, ex.out_shapes, ex.n_sems, aliases=ex.aliases)


def _grid_call(body, *, name, grid, in_specs, out_specs, out_shape, args, scratch_shapes=(), semantics, exchange=None):
    if exchange is None:
        res = pl.pallas_call(body, name=name, grid=grid, in_specs=list(in_specs), out_specs=list(out_specs),
                             out_shape=list(out_shape), scratch_shapes=list(scratch_shapes),
                             compiler_params=_params(semantics))(*args)
        return res, []
    n_in, n_out, n_scr = len(args), len(out_shape), len(scratch_shapes)
    x_in, x_out = len(exchange.ins), len(exchange.out_shapes)
    steps = math.prod(grid)

    def wrapped(*refs):
        core_in, ex_in = refs[:n_in], refs[n_in:n_in + x_in]
        rest = refs[n_in + x_in:]
        core_out, ex_out = rest[:n_out], rest[n_out:n_out + x_out]
        scratch, sems = rest[n_out + x_out:n_out + x_out + n_scr], rest[n_out + x_out + n_scr:]
        step = 0
        for axis, extent in enumerate(grid):
            step = step * extent + pl.program_id(axis)

        @pl.when(step == 0)
        def _():
            exchange.start(ex_in, ex_out, sems)

        body(*core_in, *core_out, *scratch)

        @pl.when(step == max((3 * steps) // 4 - 1, 0))
        def _():
            exchange.mid(ex_in, ex_out, sems)

        @pl.when(step == steps - 1)
        def _():
            exchange.finish(ex_in, ex_out, sems)

    any_spec = pl.BlockSpec(memory_space=pl.ANY)
    res = pl.pallas_call(
        wrapped, name=name, grid=grid,
        in_specs=list(in_specs) + [any_spec] * x_in, out_specs=list(out_specs) + [any_spec] * x_out,
        out_shape=list(out_shape) + list(exchange.out_shapes),
        input_output_aliases={n_in + a: n_out + b for a, b in exchange.aliases.items()},
        scratch_shapes=list(scratch_shapes) + [pltpu.SemaphoreType.DMA((exchange.n_sems,)),
                                               pltpu.SemaphoreType.DMA((exchange.n_sems,))],
        compiler_params=_params(("arbitrary",) * len(grid)),
    )(*args, *exchange.ins)
    return res[:n_out], res[n_out:]


class _SiblingExchange:
    def __init__(self, parts):
        self.ins = list(parts)
        self.out_shapes = [jax.ShapeDtypeStruct((4, a.shape[1] // 2, a.shape[2]), a.dtype) for a in parts]
        self.n_sems = len(parts)
        self.aliases = {}

    def _copies(self, ins, outs, sems):
        x, y, c, _ = _place()
        return [_remote(src.at[:, _half_rows(src.shape[1], 1 - c)], dst, sems, w, (x, y, 1 - c))
                for w, (src, dst) in enumerate(zip(ins, outs))]

    def start(self, ins, outs, sems):
        for cp in self._copies(ins, outs, sems):
            cp.start()

    def mid(self, ins, outs, sems):
        pass

    def finish(self, ins, outs, sems):
        for cp in self._copies(ins, outs, sems):
            cp.wait_recv()
            cp.wait_send()


class _ScatterExchange:
    def __init__(self, sums):
        self.ins = list(sums)
        self.out_shapes = [jax.ShapeDtypeStruct(a.shape, a.dtype) for a in sums]
        self.n_sems = 3 * len(sums)
        self.aliases = {}

    def _copies(self, ins, outs, sems):
        x, y, c, others = _place()
        me = 2 * x + y
        table = []
        for w, (src, dst) in enumerate(zip(ins, outs)):
            for j, (px, py) in enumerate(others):
                there = 2 * px + py
                send = _remote(src.at[there], dst.at[me], sems, 3 * w + j, (px, py, c))
                landed = _remote(dst.at[there], dst.at[there], sems, 3 * w + j, (px, py, c))
                table.append((send, landed))
        return table

    def start(self, ins, outs, sems):
        for send, _ in self._copies(ins, outs, sems):
            send.start()

    def mid(self, ins, outs, sems):
        pass

    def finish(self, ins, outs, sems):
        table = self._copies(ins, outs, sems)
        for _, landed in table:
            landed.wait_recv()
        for send, _ in table:
            send.wait_send()


def _sibling_join(shards):
    n = len(shards)

    def body(*refs):
        ins, outs, sems = refs[:n], refs[n:2 * n], refs[2 * n:2 * n + 2]
        x, y, c, _ = _place()
        pend = []
        for w in range(n):
            rows = ins[w].shape[1]
            mine, theirs = _half_rows(rows, c), _half_rows(rows, 1 - c)
            cp = _remote(ins[w].at[:, mine], outs[w].at[:, mine], sems, w, (x, y, 1 - c))
            cp.start()
            pend.append((cp, _remote(ins[w].at[:, theirs], outs[w].at[:, theirs], sems, w, (x, y, 1 - c))))
        for cp, landed in pend:
            landed.wait_recv()
            cp.wait_send()

    out_shapes = [jax.ShapeDtypeStruct(a.shape, a.dtype) for a in shards]
    return _hbm_call(body, "grad_sibling_join", shards, out_shapes, n, aliases={w: w for w in range(n)})


SMALL_ROWS = 136


def _all_reduce_small(vec):
    def body(v_ref, o_ref, buf, send, recv, loc):
        x, y, c, _ = _place()
        me = 4 * x + 2 * y + c
        lc = pltpu.make_async_copy(v_ref, buf.at[me], loc.at[0])
        lc.start()
        cps = []
        for k in range(1, 8):
            fx, fy, fc = (k >> 2) & 1, (k >> 1) & 1, k & 1
            peer = (x ^ fx, y ^ fy, c ^ fc)
            cp = pltpu.make_async_remote_copy(src_ref=v_ref, dst_ref=buf.at[me], send_sem=send.at[k - 1],
                                              recv_sem=recv.at[k - 1], device_id=peer, device_id_type=MESH)
            cp.start()
            cps.append((cp, 4 * peer[0] + 2 * peer[1] + peer[2]))
        for k, (cp, src) in enumerate(cps):
            pltpu.make_async_remote_copy(src_ref=v_ref, dst_ref=buf.at[src], send_sem=send.at[k], recv_sem=recv.at[k],
                                         device_id=(x, y, c), device_id_type=MESH).wait_recv()
        for cp, _ in cps:
            cp.wait_send()
        lc.wait()
        total = buf[0]
        for k in range(1, 8):
            total = total + buf[k]
        o_ref[...] = total

    vm = pl.BlockSpec(memory_space=pltpu.VMEM)
    return pl.pallas_call(
        body, name="all_reduce_small", in_specs=[vm], out_specs=vm,
        out_shape=jax.ShapeDtypeStruct(vec.shape, F32),
        scratch_shapes=[pltpu.VMEM((8,) + vec.shape, F32), pltpu.SemaphoreType.DMA((7,)),
                        pltpu.SemaphoreType.DMA((7,)), pltpu.SemaphoreType.DMA((1,))],
    )(vec)


SUM_ROWS = 256


def _add_pairs(part, theirs, place, name):
    four, rh, cc = theirs.shape
    tr = min(SUM_ROWS, rh)
    halves = part.reshape(four, 2, rh, cc)

    def body(p_ref, a_ref, b_ref, o_ref):
        o_ref[0] = (a_ref[0, 0].astype(F32) + b_ref[0].astype(F32)).astype(o_ref.dtype)

    spec = pl.BlockSpec((1, tr, cc), lambda k, r, p_ref: (k, r, 0))
    return pl.pallas_call(
        body, name=name,
        grid_spec=pltpu.PrefetchScalarGridSpec(
            num_scalar_prefetch=1, grid=(four, rh // tr),
            in_specs=[pl.BlockSpec((1, 1, tr, cc), lambda k, r, p_ref: (k, p_ref[1], r, 0)), spec], out_specs=spec),
        out_shape=jax.ShapeDtypeStruct(theirs.shape, theirs.dtype),
        compiler_params=_params(("parallel", "parallel")),
    )(place, halves, theirs)


def _sum_chips(own, arrived, place, name, layer, n_layers, into=None):
    four, rh, cc = own.shape
    tr = min(SUM_ROWS, rh)
    nr = rh // tr

    def body(p_ref, own_ref, arr_ref, *rest):
        o_ref = rest[-1]
        x, y = lax.axis_index("x"), lax.axis_index("y")
        tot = own_ref[0].astype(F32)
        for px, py in ((1 - x, y), (x, 1 - y), (1 - x, 1 - y)):
            tot = tot + arr_ref[2 * px + py].astype(F32)
        o_ref[0] = tot

    in_specs = [pl.BlockSpec((1, tr, cc), lambda r, p_ref: (p_ref[0], r, 0)),
                pl.BlockSpec((4, tr, cc), lambda r, p_ref: (0, r, 0))]
    args, aliases = [place, own, arrived], {}
    if into is not None:
        in_specs.append(pl.BlockSpec(memory_space=pl.ANY))
        args.append(into)
        aliases = {3: 0}
    return pl.pallas_call(
        body, name=name,
        grid_spec=pltpu.PrefetchScalarGridSpec(
            num_scalar_prefetch=1, grid=(nr,), in_specs=in_specs,
            out_specs=pl.BlockSpec((1, tr, cc), lambda r, p_ref: (layer, p_ref[1] * nr + r, 0))),
        out_shape=jax.ShapeDtypeStruct((n_layers, 2 * rh, cc), F32), input_output_aliases=aliases,
        compiler_params=_params(("parallel",)),
    )(*args)


ADAM_ROWS = 256


def _adamw(w, g, m, v, name):
    shape = w.shape
    as3 = lambda a: a.reshape((-1,) + shape[-2:])
    layers, rows, cc = as3(w).shape
    by_rows = rows % min(ADAM_ROWS, rows) == 0
    tr, tc = (min(ADAM_ROWS, rows), cc) if by_rows else (rows, ADAM_ROWS)
    assert rows % tr == 0 and cc % tc == 0
    c1 = 1.0 - ADAM_B1 ** ADAM_STEP
    c2 = 1.0 - ADAM_B2 ** ADAM_STEP

    def body(w_ref, g_ref, m_ref, v_ref, d_ref, nm_ref, nv_ref):
        gv = g_ref[...]
        nm = ADAM_B1 * m_ref[...] + (1.0 - ADAM_B1) * gv
        nv = ADAM_B2 * v_ref[...] + (1.0 - ADAM_B2) * (gv * gv)
        nm_ref[...] = nm
        nv_ref[...] = nv
        d_ref[...] = -ADAM_LR * ((nm / c1) / (jnp.sqrt(nv / c2) + ADAM_EPS) + ADAM_WD * w_ref[...])

    spec = pl.BlockSpec((1, tr, tc), (lambda l, i: (l, i, 0)) if by_rows else (lambda l, i: (l, 0, i)))
    sh = jax.ShapeDtypeStruct((layers, rows, cc), F32)
    outs = pl.pallas_call(
        body, name=name, grid=(layers, (rows // tr) * (cc // tc)), in_specs=[spec] * 4, out_specs=[spec] * 3,
        out_shape=[sh] * 3,
        compiler_params=_params(("parallel", "parallel")),
    )(as3(w), as3(g), as3(m), as3(v))
    return [o.reshape(shape) for o in outs]


def _pack_small(g_pre, g_post, sinks_a, b_f_c, loss_row):
    pad = lambda a: jnp.pad(a.reshape(1, -1).astype(F32), ((0, 0), (0, LANES - a.size)))
    rows = [g_pre.astype(F32).reshape(-1, LANES), g_post.astype(F32).reshape(-1, LANES), pad(sinks_a), pad(b_f_c), loss_row]
    packed = jnp.concatenate(rows, axis=0)
    return jnp.pad(packed, ((0, SMALL_ROWS - packed.shape[0]), (0, 0)))


def _unpack_small(p):
    n = DEPTH * D_MODEL // LANES
    return (p[:n].reshape(DEPTH, D_MODEL), p[n:2 * n].reshape(DEPTH, D_MODEL), p[2 * n, :2 * N_HEADS].reshape(2, N_HEADS),
            p[2 * n + 1, :N_HEADS].reshape(1, N_HEADS), p[2 * n + 2, 0])


def kernel(x, g_pre, g_post, w_in_a, w_out_a, sinks_a, w_in_b, w_out_b, w_in_c, b_f_c, w_out_c, loss_target, m_g_pre, m_g_post, m_w_in_a, m_w_out_a, m_sinks_a, m_w_in_b, m_w_out_b, m_w_in_c, m_b_f_c, m_w_out_c, v_g_pre, v_g_post, v_w_in_a, v_w_out_a, v_sinks_a, v_w_in_b, v_w_out_b, v_w_in_c, v_b_f_c, v_w_out_c):
    big_w = [w_in_a, w_out_a, w_in_b, w_out_b, w_in_c, w_out_c]
    big_m = [m_w_in_a, m_w_out_a, m_w_in_b, m_w_out_b, m_w_in_c, m_w_out_c]
    big_v = [v_w_in_a, v_w_out_a, v_w_in_b, v_w_out_b, v_w_in_c, v_w_out_c]

    chip = 2 * lax.axis_index("x") + lax.axis_index("y")
    place = jnp.stack([chip, lax.axis_index("c")]).astype(jnp.int32)
    by_kind = {0: (w_in_a, w_out_a), 1: (w_in_b, w_out_b), 2: (w_in_c, w_out_c)}
    shards = {}
    for i in range(DEPTH):
        kind, j = _layer_kind(i)
        shards[("in", i)] = by_kind[kind][0][j].astype(BF16)
        shards[("out", i)] = by_kind[kind][1][j].astype(BF16)

    res = _forward_backward(x[0], loss_target[0], g_pre, g_post, sinks_a, b_f_c, shards, chip, place)
    names = ["w_in_a", "w_out_a", "w_in_b", "w_out_b", "w_in_c", "w_out_c"]
    grads = _sibling_join([res["reduced"][(side, kind)] for kind in range(3) for side in ("in", "out")])

    small = _unpack_small(_all_reduce_small(
        _pack_small(res["g_pre"], res["g_post"], res["sinks_a"], res["b_f_c"], res["loss"])))
    g_small, loss = small[:4], small[4]

    zero_row = jnp.zeros((1, LANES), F32)
    pk = lambda a: _pack_small(a[0], a[1], a[2], a[3], zero_row)
    sm = _adamw(pk([g_pre, g_post, sinks_a, b_f_c]), pk(g_small), pk([m_g_pre, m_g_post, m_sinks_a, m_b_f_c]),
                pk([v_g_pre, v_g_post, v_sinks_a, v_b_f_c]), "adamw_small")
    sm = [_unpack_small(a)[:4] for a in sm]
    turned = lambda a: jnp.swapaxes(a, 1, 2)
    grads = list(grads)
    g_c = lax.optimization_barrier(turned(grads[4]))
    grads[4] = turned(g_c)
    bigs = [[turned(o) for o in _adamw(turned(w), g_c, turned(m), turned(v), f"adamw_{nm}")] if nm == "w_in_c"
            else _adamw(w, g, m, v, f"adamw_{nm}") for w, g, m, v, nm in zip(big_w, grads, big_m, big_v, names)]

    def ordered(small4, big6):
        return [small4[0], small4[1], big6[0], big6[1], small4[2], big6[2], big6[3], big6[4], small4[3], big6[5]]

    out = [loss, res["dx"][None], *ordered(g_small, grads)]
    for k in range(3):
        out += ordered(sm[k], [b[k] for b in bigs])
    return tuple(out)
```

```python
import functools
import math

import numpy as np
import jax
import jax.numpy as jnp
from jax import lax
from jax.experimental import pallas as pl
from jax.experimental.pallas import tpu as pltpu

F32 = jnp.float32
BF16 = jnp.bfloat16

D_MODEL = 2048
DEPTH = 4
N_HEADS = 32
HEAD_DIM = 64
LANES = 128
N_PAIRS = N_HEADS * HEAD_DIM // LANES
BRANCH = N_HEADS * HEAD_DIM
N_KV_A = 4
KV_A = N_KV_A * HEAD_DIM
WINDOW = 128
NORM_EPS = 1e-6
NEG = -1e30
Q_SCALE = HEAD_DIM ** -0.5

A_QKV = BRANCH + 2 * KV_A
B_QKV = 3 * BRANCH

ADAM_LR = 0.001
ADAM_B1 = 0.9
ADAM_B2 = 0.999
ADAM_EPS = 1e-08
ADAM_WD = 0.01
ADAM_STEP = 10

MESH = pl.DeviceIdType.MESH

_NT = (((1,), (1,)), ((), ()))
_TN = (((0,), (0,)), ((), ()))


def _params(sem=None):
    return pltpu.CompilerParams(dimension_semantics=sem)


def _matmul(a, b, *, out_dtype, name, n=None, b_off=0, tm=1024, tn=1024, col_blocks=None, exchange=None):
    (m, k), nn = a.shape, (n or b.shape[1])
    tm, tn = min(tm, m), min(tn, nn)
    assert m % tm == 0 and nn % tn == 0, (name, m, nn, tm, tn)

    def body(a_ref, b_ref, o_ref):
        p = jnp.dot(a_ref[...], b_ref[...], preferred_element_type=F32)
        o_ref[...] = p.astype(o_ref.dtype).reshape(o_ref.shape)

    in_specs = [pl.BlockSpec((tm, k), lambda i, j: (i, 0)), pl.BlockSpec((k, tn), lambda i, j: (0, j + b_off))]
    if col_blocks is None:
        out_spec = pl.BlockSpec((tm, tn), lambda i, j: (i, j))
        out_shape = jax.ShapeDtypeStruct((m, nn), out_dtype)
    else:
        per = nn // col_blocks // tn
        assert per * tn * col_blocks == nn, (name, nn, tn, col_blocks)
        out_spec = pl.BlockSpec((1, tm, tn), lambda i, j: (j // per, i, j % per))
        out_shape = jax.ShapeDtypeStruct((col_blocks, m, nn // col_blocks), out_dtype)
    (res,), arrived = _grid_call(
        body, name=name, grid=(m // tm, nn // tn), in_specs=in_specs, out_specs=[out_spec], out_shape=[out_shape],
        args=(a, b), semantics=("parallel", "parallel"), exchange=exchange)
    return res if exchange is None else (res, arrived)


ROW_TILE = 256


def _row_call(body, name, ins, outs, *, s):
    tr = min(ROW_TILE, s)
    spec = {"row": lambda sh: pl.BlockSpec((tr, sh[1]), lambda i: (i, 0)),
            "vec": lambda sh: pl.BlockSpec((1, sh[1]), lambda i: (0, 0)),
            "col": lambda sh: pl.BlockSpec((sh[0], tr), lambda i: (0, i))}
    in_specs = [spec[kind](a.shape) for a, kind in ins]
    out_specs = [spec[kind](sh.shape) for sh, kind in outs]
    return pl.pallas_call(
        body, name=name, grid=(s // tr,), in_specs=in_specs, out_specs=out_specs,
        out_shape=[sh for sh, _ in outs],
        compiler_params=_params(("arbitrary",)),
    )(*[a for a, _ in ins])


def _rsqrt_ms(v):
    return lax.rsqrt(jnp.mean(v * v, axis=-1, keepdims=True) + NORM_EPS)


def _rmsnorm_fwd(x, g, name):
    s, d = x.shape

    def body(x_ref, g_ref, h_ref, ht_ref):
        xv = x_ref[...]
        h = xv * _rsqrt_ms(xv) * g_ref[...]
        h_ref[...] = h.astype(BF16)
        ht_ref[...] = h.T.astype(BF16)

    return _row_call(body, name, [(x, "row"), (g, "vec")],
                     [(jax.ShapeDtypeStruct((s, d), BF16), "row"), (jax.ShapeDtypeStruct((d, s), BF16), "col")], s=s)


PROJ_ROWS = 256


def _resident(shape):
    return pl.BlockSpec(shape, lambda i: (0,) * len(shape), pipeline_mode=pl.Buffered(1))


def _gated_out_proj(o, z, w_out, x, g, name):
    s, d = x.shape
    tm = min(PROJ_ROWS, s)

    def body(o_ref, z_ref, w_ref, x_ref, g_ref, xn_ref, y_ref, ut_ref):
        zv = z_ref[...]
        u = o_ref[...] * (zv * jax.nn.sigmoid(zv))
        ut_ref[...] = u.T.astype(BF16)
        y = jnp.dot(u.astype(BF16), w_ref[...], preferred_element_type=F32)
        y_ref[...] = y
        xn_ref[...] = x_ref[...] + y * _rsqrt_ms(y) * g_ref[...]

    row = pl.BlockSpec((tm, d), lambda i: (i, 0))
    return pl.pallas_call(
        body, name=name, grid=(s // tm,),
        in_specs=[row, row, _resident(w_out.shape), row, _resident((1, d))],
        out_specs=[row, row, pl.BlockSpec((d, tm), lambda i: (0, i))],
        out_shape=[jax.ShapeDtypeStruct((s, d), F32), jax.ShapeDtypeStruct((s, d), F32), jax.ShapeDtypeStruct((d, s), BF16)],
        compiler_params=_params(("parallel",)),
    )(o, z, w_out, x, g)


def _gated_out_proj_bwd(dx, y, g, w_out, o, z, name, exchange=None):
    s, d = dx.shape
    tm = min(PROJ_ROWS, s)

    def body(dx_ref, y_ref, g_ref, w_ref, o_ref, z_ref, dy_ref, dg_ref, do_ref, dz_ref):
        dy, dg = _norm_bwd_rows(dx_ref[...], y_ref[...], g_ref[...])
        dyb = dy.astype(BF16)
        dy_ref[...] = dyb

        @pl.when(pl.program_id(0) == 0)
        def _():
            dg_ref[...] = jnp.zeros_like(dg_ref)

        dg_ref[...] += jnp.sum(dg, axis=0, keepdims=True)
        du = lax.dot_general(dyb, w_ref[...], _NT, preferred_element_type=F32)
        zv = z_ref[...]
        sig = jax.nn.sigmoid(zv)
        do_ref[...] = (du * (zv * sig)).astype(BF16)
        dz_ref[...] = (du * o_ref[...] * (sig * (1.0 + zv * (1.0 - sig)))).astype(BF16)

    row = pl.BlockSpec((tm, d), lambda i: (i, 0))
    vec = pl.BlockSpec((1, d), lambda i: (0, 0))
    bf = jax.ShapeDtypeStruct((s, d), BF16)
    return _grid_call(
        body, name=name, grid=(s // tm,),
        in_specs=[row, row, _resident((1, d)), _resident(w_out.shape), row, row],
        out_specs=[row, vec, row, row], out_shape=[bf, jax.ShapeDtypeStruct((1, d), F32), bf, bf],
        args=(dx, y, g, w_out, o, z), semantics=("arbitrary",), exchange=exchange)


IN_BWD_ROWS = 512


def _in_proj_bwd(dproj, w_in, extra, dx, x, g, name, tk, exchange=None):
    s, d = x.shape
    k = dproj.shape[1]
    tm = min(IN_BWD_ROWS, s)
    nk = k // tk
    assert k % tk == 0 and s % tm == 0, (name, k, tk)
    has_extra = extra is not None

    def body(a_ref, b_ref, *rest):
        if has_extra:
            e_ref, rest = rest[0], rest[1:]
        dx_ref, x_ref, g_ref, o_ref, dg_ref, acc_ref = rest
        i, kk = pl.program_id(0), pl.program_id(1)
        p = lax.dot_general(a_ref[...], b_ref[...], _NT, preferred_element_type=F32)

        @pl.when(kk == 0)
        def _():
            acc_ref[...] = p

        @pl.when(kk > 0)
        def _():
            acc_ref[...] += p

        @pl.when((i == 0) & (kk == 0))
        def _():
            dg_ref[...] = jnp.zeros_like(dg_ref)

        @pl.when(kk == nk - 1)
        def _():
            def rows_chunk(c, _):
                r = pl.ds(pl.multiple_of(c * LANES, LANES), LANES)
                dh = acc_ref[r, :] + e_ref[r, :] if has_extra else acc_ref[r, :]
                dv, dg = _norm_bwd_rows(dh, x_ref[r, :], g_ref[...])
                o_ref[r, :] = dx_ref[r, :] + dv
                dg_ref[...] += jnp.sum(dg, axis=0, keepdims=True)
                return 0

            lax.fori_loop(0, tm // LANES, rows_chunk, 0)

    row = pl.BlockSpec((tm, d), lambda i, kk: (i, 0))
    vec = pl.BlockSpec((1, d), lambda i, kk: (0, 0))
    in_specs = [pl.BlockSpec((tm, tk), lambda i, kk: (i, kk)), pl.BlockSpec((d, tk), lambda i, kk: (0, kk))]
    args = [dproj, w_in]
    if has_extra:
        in_specs.append(row)
        args.append(extra)
    return _grid_call(
        body, name=name, grid=(s // tm, nk), in_specs=in_specs + [row, row, vec], out_specs=[row, vec],
        out_shape=[jax.ShapeDtypeStruct((s, d), F32), jax.ShapeDtypeStruct((1, d), F32)],
        args=tuple(args) + (dx, x, g), scratch_shapes=[pltpu.VMEM((tm, d), F32)], semantics=("arbitrary", "arbitrary"),
        exchange=exchange)


def _loss_and_grad(x, target):
    s, d = x.shape

    def body(x_ref, t_ref, dx_ref, l_ref):
        err = x_ref[...] - t_ref[...]
        dx_ref[...] = err * (1.0 / d)
        part = jnp.sum(jnp.sum(err * err, axis=1, keepdims=True), axis=0, keepdims=True) * (0.5 / d)

        @pl.when(pl.program_id(0) == 0)
        def _():
            l_ref[...] = jnp.zeros_like(l_ref)

        l_ref[...] += jnp.broadcast_to(part, l_ref.shape)

    return _row_call(body, "loss_head", [(x, "row"), (target, "row")],
                     [(jax.ShapeDtypeStruct((s, d), F32), "row"),
                      (jax.ShapeDtypeStruct((1, LANES), F32), "vec")], s=s)


def _norm_bwd_rows(dn, v, g):
    r = _rsqrt_ms(v)
    a = dn * g
    dv = r * (a - v * (r * r) * jnp.mean(a * v, axis=-1, keepdims=True))
    return dv, dn * v * r


def _lane_is_first_head():
    return lax.broadcasted_iota(jnp.int32, (1, LANES), 1) < HEAD_DIM


def _bcast_lanes(col):
    return jnp.broadcast_to(col, (col.shape[0], LANES))


def _pair_spec(s, off=0, width=LANES):
    return pl.BlockSpec((s, width), lambda p: (0, p + off))


def _stack_heads(pair, first):
    return jnp.concatenate([jnp.where(first, pair, 0), jnp.where(first, 0, pair)], axis=0).astype(BF16)


def _stacked_mask(t, strict):
    row = lax.broadcasted_iota(jnp.int32, (2 * t, t), 0)
    col = lax.broadcasted_iota(jnp.int32, (2 * t, t), 1)
    query = jnp.where(row >= t, row - t, row)
    return col < query if strict else col <= query


LOOP_UNROLL = 2


def _two_at_a_time(n, step, carry):
    def group(jj, c):
        for k in range(LOOP_UNROLL):
            c = step(LOOP_UNROLL * jj + k, c)
        return c

    carry = lax.fori_loop(0, n // LOOP_UNROLL, group, carry)
    return lax.fori_loop(LOOP_UNROLL * (n // LOOP_UNROLL), n, step, carry)


def _rowsum_heads(prod, first):
    return (jnp.sum(jnp.where(first, prod, 0.0), axis=1, keepdims=True),
            jnp.sum(jnp.where(first, 0.0, prod), axis=1, keepdims=True))


def _softplus_parts(z):
    e = jnp.exp(-jnp.abs(z))
    sp = jnp.maximum(z, 0.0) + jnp.log(1.0 + e)
    r = 1.0 / (1.0 + e)
    return sp, jnp.where(z >= 0, r, e * r)


def _sb_tile(s):
    return min(256, s)


def _attn_b_fwd(qkv, name, exchange=None):
    s = qkv.shape[0]
    t = _sb_tile(s)
    nq = s // t

    def body(q_ref, k_ref, v_ref, o_ref, lt_ref):
        first = _lane_is_first_head()
        before = _stacked_mask(t, strict=True)
        tri = (lax.broadcasted_iota(jnp.int32, (t, t), 0) >= lax.broadcasted_iota(jnp.int32, (t, t), 1)).astype(BF16)

        def tile(j, carry, diag, qs):
            c, acc = carry
            c0 = pl.multiple_of(j * t, t)
            k2 = k_ref[pl.ds(c0, t), :]
            v2 = v_ref[pl.ds(c0, t), :]
            z = lax.dot_general(qs, k2, _NT, preferred_element_type=F32)
            sp, _ = _softplus_parts(z)
            lf = jnp.where(before, -sp, 0.0) if diag else -sp
            incl = jnp.dot(lf.astype(BF16), tri, preferred_element_type=F32)
            a = jnp.exp(z + c + incl)
            if diag:
                a = jnp.where(before, a, 0.0)
            pv = jnp.dot(a.astype(BF16), v2, preferred_element_type=F32)
            return c + incl[:, 0:1], acc + jnp.where(first, pv[:t], pv[t:])

        def qblock(i, _):
            r0 = pl.multiple_of(i * t, t)
            qs = _stack_heads(q_ref[pl.ds(r0, t), :] * Q_SCALE, first)
            carry = tile(i, (jnp.zeros((2 * t, 1), F32), jnp.zeros((t, LANES), F32)), True, qs)
            carry = _two_at_a_time(i, lambda j, c: tile(i - 1 - j, c, False, qs), carry)
            o_ref[pl.ds(r0, t), :] = carry[1]
            lt_ref[pl.ds(r0, t), 0:LANES] = _bcast_lanes(carry[0][:t])
            lt_ref[pl.ds(r0, t), LANES:2 * LANES] = _bcast_lanes(carry[0][t:])
            return 0

        lax.fori_loop(0, nq, qblock, 0)

    return _grid_call(
        body, name=name, grid=(N_PAIRS,),
        in_specs=[_pair_spec(s), _pair_spec(s, N_PAIRS), _pair_spec(s, 2 * N_PAIRS)],
        out_specs=[_pair_spec(s), _stat_spec(s)],
        out_shape=[jax.ShapeDtypeStruct((s, BRANCH), F32), jax.ShapeDtypeStruct((s, N_HEADS * LANES), F32)],
        args=(qkv, qkv, qkv), semantics=("parallel",), exchange=exchange)


def _attn_b_bwd(qkv, ltot, do, name, exchange=None):
    s = qkv.shape[0]
    t = _sb_tile(s)
    nq = s // t

    def body(q_ref, k_ref, v_ref, lt_ref, do_ref, dq_ref, dk_ref, dv_ref, dk_acc, dv_acc):
        first = _lane_is_first_head()
        before = _stacked_mask(t, strict=True)
        tri = (lax.broadcasted_iota(jnp.int32, (t, t), 0) <= lax.broadcasted_iota(jnp.int32, (t, t), 1)).astype(BF16)
        dk_acc[...] = jnp.zeros_like(dk_acc)
        dv_acc[...] = jnp.zeros_like(dv_acc)

        def tile(j, carry, diag, qs, dos, lt):
            p_l, p_g, dq_acc = carry
            c0 = pl.multiple_of(j * t, t)
            k2 = k_ref[pl.ds(c0, t), :]
            v2 = v_ref[pl.ds(c0, t), :]
            z = lax.dot_general(qs, k2, _NT, preferred_element_type=F32)
            sp, sig = _softplus_parts(z)
            lf = jnp.where(before, -sp, 0.0) if diag else -sp
            pref_l = jnp.dot(lf.astype(BF16), tri, preferred_element_type=F32)
            a = jnp.exp(z + ((lt - p_l) - pref_l + lf))
            if diag:
                a = jnp.where(before, a, 0.0)
            g = a * lax.dot_general(dos, v2, _NT, preferred_element_type=F32)
            pref_g = jnp.dot(g.astype(BF16), tri, preferred_element_type=F32)
            dz = g - sig * (p_g + pref_g)
            if diag:
                dz = jnp.where(before, dz, 0.0)
            dzb = dz.astype(BF16)
            dq = jnp.dot(dzb, k2, preferred_element_type=F32)
            dk_acc[pl.ds(c0, t), :] += lax.dot_general(dzb, qs, _TN, preferred_element_type=F32)
            dv_acc[pl.ds(c0, t), :] += lax.dot_general(a.astype(BF16), dos, _TN, preferred_element_type=F32)
            return p_l + pref_l[:, t - 1:t], p_g + pref_g[:, t - 1:t], dq_acc + jnp.where(first, dq[:t], dq[t:])

        def qblock(i, _):
            r0 = pl.multiple_of(i * t, t)
            qs = _stack_heads(q_ref[pl.ds(r0, t), :] * Q_SCALE, first)
            dos = _stack_heads(do_ref[pl.ds(r0, t), :], first)
            lt = jnp.concatenate([lt_ref[pl.ds(r0, t), 0:1], lt_ref[pl.ds(r0, t), LANES:LANES + 1]], axis=0)
            zero = jnp.zeros((2 * t, 1), F32)
            carry = (zero, zero, jnp.zeros((t, LANES), F32))
            carry = _two_at_a_time(i, lambda j, c: tile(j, c, False, qs, dos, lt), carry)
            carry = tile(i, carry, True, qs, dos, lt)
            dq_ref[pl.ds(r0, t), :] = (carry[2] * Q_SCALE).astype(BF16)
            return 0

        lax.fori_loop(0, nq, qblock, 0)
        dk_ref[...] = dk_acc[...].astype(BF16)
        dv_ref[...] = dv_acc[...].astype(BF16)

    out = jax.ShapeDtypeStruct((s, BRANCH), BF16)
    return _grid_call(
        body, name=name, grid=(N_PAIRS,),
        in_specs=[_pair_spec(s), _pair_spec(s, N_PAIRS), _pair_spec(s, 2 * N_PAIRS), _stat_spec(s), _pair_spec(s)],
        out_specs=[_pair_spec(s)] * 3, out_shape=[out] * 3,
        scratch_shapes=[pltpu.VMEM((s, LANES), F32), pltpu.VMEM((s, LANES), F32)],
        args=(qkv, qkv, qkv, ltot, do), semantics=("parallel",), exchange=exchange)


def _fox_tile(s):
    return min(256, s)


def _stat_spec(s):
    return pl.BlockSpec((s, 2 * LANES), lambda p: (0, p))


def _cum_spec(nt, t):
    return pl.BlockSpec((1, nt, 2, t), lambda p: (p, 0, 0, 0))


def _attn_c_fwd(qkv, cum4, name, exchange=None):
    s = qkv.shape[0]
    t = _fox_tile(s)
    nq = s // t

    def body(q_ref, k_ref, v_ref, c_ref, o_ref, lse_ref):
        first = _lane_is_first_head()
        causal = _stacked_mask(t, strict=False)

        def tile(j, carry, diag, qs):
            c0 = pl.multiple_of(j * t, t)
            k2 = k_ref[pl.ds(c0, t), :]
            v2 = v_ref[pl.ds(c0, t), :]
            cs = c_ref[0, j]
            m_prev, l_prev, acc = carry
            z = lax.dot_general(qs, k2, _NT, preferred_element_type=F32)
            sc = jnp.concatenate([z[:t] - cs[0:1, :], z[t:] - cs[1:2, :]], axis=0)
            if diag:
                sc = jnp.where(causal, sc, NEG)
            m_new = jnp.maximum(m_prev, jnp.max(sc, axis=1, keepdims=True))
            alpha = jnp.exp(m_prev - m_new)
            p = jnp.exp(sc - m_new)
            l_new = alpha * l_prev + jnp.sum(p, axis=1, keepdims=True)
            pv = jnp.dot(p.astype(BF16), v2, preferred_element_type=F32)
            acc = jnp.where(first, acc * alpha[:t] + pv[:t], acc * alpha[t:] + pv[t:])
            return m_new, l_new, acc

        def qblock(i, _):
            r0 = pl.multiple_of(i * t, t)
            qs = _stack_heads(q_ref[pl.ds(r0, t), :] * Q_SCALE, first)
            carry = (jnp.full((2 * t, 1), NEG, F32), jnp.zeros((2 * t, 1), F32), jnp.zeros((t, LANES), F32))
            carry = _two_at_a_time(i, lambda j, c: tile(j, c, False, qs), carry)
            m, l, acc = tile(i, carry, True, qs)
            inv = 1.0 / l
            lse = m + jnp.log(l)
            o_ref[pl.ds(r0, t), :] = acc * jnp.where(first, inv[:t], inv[t:])
            lse_ref[pl.ds(r0, t), 0:LANES] = _bcast_lanes(lse[:t])
            lse_ref[pl.ds(r0, t), LANES:2 * LANES] = _bcast_lanes(lse[t:])
            return 0

        lax.fori_loop(0, nq, qblock, 0)

    return _grid_call(
        body, name=name, grid=(N_PAIRS,),
        in_specs=[_pair_spec(s), _pair_spec(s, N_PAIRS), _pair_spec(s, 2 * N_PAIRS), _cum_spec(nq, t)],
        out_specs=[_pair_spec(s), _stat_spec(s)],
        out_shape=[jax.ShapeDtypeStruct((s, BRANCH), F32), jax.ShapeDtypeStruct((s, N_HEADS * LANES), F32)],
        args=(qkv, qkv, qkv, cum4), semantics=("parallel",), exchange=exchange)


def _attn_c_bwd(qkv, cum4, o, lse, do, name, exchange=None):
    s = qkv.shape[0]
    t = _fox_tile(s)
    nq = s // t

    def body(q_ref, k_ref, v_ref, c_ref, o_ref, lse_ref, do_ref, dq_ref, dk_ref, dv_ref, dc_ref, dk_acc, dv_acc):
        first = _lane_is_first_head()
        causal = _stacked_mask(t, strict=False)
        eye = lax.broadcasted_iota(jnp.int32, (t, t), 0) == lax.broadcasted_iota(jnp.int32, (t, t), 1)
        dk_acc[...] = jnp.zeros_like(dk_acc)
        dv_acc[...] = jnp.zeros_like(dv_acc)
        dc_ref[...] = jnp.zeros_like(dc_ref)

        def tile(j, carry, diag, qs, dos, delta, lse):
            dq_acc, rs = carry
            c0 = pl.multiple_of(j * t, t)
            k2 = k_ref[pl.ds(c0, t), :]
            v2 = v_ref[pl.ds(c0, t), :]
            cs = c_ref[0, j]
            z = lax.dot_general(qs, k2, _NT, preferred_element_type=F32)
            sc = jnp.concatenate([z[:t] - cs[0:1, :], z[t:] - cs[1:2, :]], axis=0)
            p = jnp.exp(sc - lse)
            if diag:
                p = jnp.where(causal, p, 0.0)
            ds = p * (lax.dot_general(dos, v2, _NT, preferred_element_type=F32) - delta)
            dsb = ds.astype(BF16)
            dq = jnp.dot(dsb, k2, preferred_element_type=F32)
            dk_acc[pl.ds(c0, t), :] += lax.dot_general(dsb, qs, _TN, preferred_element_type=F32)
            dv_acc[pl.ds(c0, t), :] += lax.dot_general(p.astype(BF16), dos, _TN, preferred_element_type=F32)
            col_sums = jnp.concatenate([jnp.sum(ds[:t], axis=0, keepdims=True), jnp.sum(ds[t:], axis=0, keepdims=True)], axis=0)
            dc_ref[0, j] = dc_ref[0, j] - col_sums
            return dq_acc + jnp.where(first, dq[:t], dq[t:]), rs + jnp.sum(ds, axis=1, keepdims=True)

        def qblock(i, _):
            r0 = pl.multiple_of(i * t, t)
            do2 = do_ref[pl.ds(r0, t), :]
            qs = _stack_heads(q_ref[pl.ds(r0, t), :] * Q_SCALE, first)
            dos = _stack_heads(do2, first)
            delta = jnp.concatenate(_rowsum_heads(do2.astype(F32) * o_ref[pl.ds(r0, t), :], first), axis=0)
            lse = jnp.concatenate([lse_ref[pl.ds(r0, t), 0:1], lse_ref[pl.ds(r0, t), LANES:LANES + 1]], axis=0)
            carry = (jnp.zeros((t, LANES), F32), jnp.zeros((2 * t, 1), F32))
            carry = _two_at_a_time(i, lambda j, c: tile(j, c, False, qs, dos, delta, lse), carry)
            dq_acc, rs = tile(i, carry, True, qs, dos, delta, lse)
            dq_ref[pl.ds(r0, t), :] = (dq_acc * Q_SCALE).astype(BF16)
            as_row = lambda col_vec: jnp.sum(jnp.where(eye, col_vec, 0.0), axis=0, keepdims=True)
            dc_ref[0, i] = dc_ref[0, i] + jnp.concatenate([as_row(rs[:t]), as_row(rs[t:])], axis=0)
            return 0

        lax.fori_loop(0, nq, qblock, 0)
        dk_ref[...] = dk_acc[...].astype(BF16)
        dv_ref[...] = dv_acc[...].astype(BF16)

    out = jax.ShapeDtypeStruct((s, BRANCH), BF16)
    return _grid_call(
        body, name=name, grid=(N_PAIRS,),
        in_specs=[_pair_spec(s), _pair_spec(s, N_PAIRS), _pair_spec(s, 2 * N_PAIRS), _cum_spec(nq, t),
                  _pair_spec(s), _stat_spec(s), _pair_spec(s)],
        out_specs=[_pair_spec(s)] * 3 + [_cum_spec(nq, t)],
        out_shape=[out] * 3 + [jax.ShapeDtypeStruct(cum4.shape, F32)],
        scratch_shapes=[pltpu.VMEM((s, LANES), F32), pltpu.VMEM((s, LANES), F32)],
        args=(qkv, qkv, qkv, cum4, o, lse, do), semantics=("parallel",), exchange=exchange)


FG_CHUNK = 512


def _tri_dot3(x, t):
    hi = x.astype(BF16)
    r1 = x - hi.astype(F32)
    mid = r1.astype(BF16)
    lo = (r1 - mid.astype(F32)).astype(BF16)
    return (jnp.dot(hi, t, preferred_element_type=F32) + jnp.dot(mid, t, preferred_element_type=F32)
            + jnp.dot(lo, t, preferred_element_type=F32))


def _fgate_fwd(h, wf_t, b_col, name):
    s = h.shape[0]
    c = min(FG_CHUNK, s)

    def body(h_ref, w_ref, b_ref, xf_ref, cum_ref, carry_ref):
        @pl.when(pl.program_id(0) == 0)
        def _():
            carry_ref[...] = jnp.zeros_like(carry_ref)

        xf = lax.dot_general(w_ref[...], h_ref[...], _NT, preferred_element_type=F32) + b_ref[:, 0:1]
        xf_ref[...] = xf
        logf = jnp.minimum(xf, 0.0) - jnp.log(1.0 + jnp.exp(-jnp.abs(xf)))
        row = lax.broadcasted_iota(jnp.int32, (c, c), 0)
        col = lax.broadcasted_iota(jnp.int32, (c, c), 1)
        cum = _tri_dot3(logf, (row <= col).astype(BF16)) + carry_ref[:, 0:1]
        cum_ref[...] = cum
        carry_ref[...] = _bcast_lanes(cum[:, c - 1:c])

    out = jax.ShapeDtypeStruct((N_HEADS, s), F32)
    return pl.pallas_call(
        body, name=name, grid=(s // c,),
        in_specs=[pl.BlockSpec((c, D_MODEL), lambda i: (i, 0)),
                  pl.BlockSpec((N_HEADS, D_MODEL), lambda i: (0, 0)),
                  pl.BlockSpec((N_HEADS, LANES), lambda i: (0, 0))],
        out_specs=[pl.BlockSpec((N_HEADS, c), lambda i: (0, i))] * 2,
        out_shape=[out, out],
        scratch_shapes=[pltpu.VMEM((N_HEADS, LANES), F32)],
        compiler_params=_params(("arbitrary",)),
    )(h, wf_t, b_col)


def _fgate_bwd(dcum, xf, h, wf_t, name):
    s = h.shape[0]
    c = min(FG_CHUNK, s)
    n = s // c

    def body(dc_ref, xf_ref, h_ref, w_ref, dw_ref, dh_ref, db_ref, carry_ref):
        @pl.when(pl.program_id(0) == 0)
        def _():
            carry_ref[...] = jnp.zeros_like(carry_ref)
            dw_ref[...] = jnp.zeros_like(dw_ref)
            db_ref[...] = jnp.zeros_like(db_ref)

        row = lax.broadcasted_iota(jnp.int32, (c, c), 0)
        col = lax.broadcasted_iota(jnp.int32, (c, c), 1)
        dlogf = _tri_dot3(dc_ref[...], (row >= col).astype(BF16)) + carry_ref[:, 0:1]
        carry_ref[...] = _bcast_lanes(dlogf[:, 0:1])
        xf = xf_ref[...]
        e = jnp.exp(-jnp.abs(xf))
        r = 1.0 / (1.0 + e)
        dxf = dlogf * jnp.where(xf >= 0, e * r, r)
        db_ref[...] += _bcast_lanes(jnp.sum(dxf, axis=1, keepdims=True))
        dxb = dxf.astype(BF16)
        dw_ref[...] += jnp.dot(dxb, h_ref[...], preferred_element_type=F32)
        dh_ref[...] = lax.dot_general(dxb, w_ref[...], _TN, preferred_element_type=F32)

    rev = lambda i: n - 1 - i
    return pl.pallas_call(
        body, name=name, grid=(n,),
        in_specs=[pl.BlockSpec((N_HEADS, c), lambda i: (0, rev(i))),
                  pl.BlockSpec((N_HEADS, c), lambda i: (0, rev(i))),
                  pl.BlockSpec((c, D_MODEL), lambda i: (rev(i), 0)),
                  pl.BlockSpec((N_HEADS, D_MODEL), lambda i: (0, 0))],
        out_specs=[pl.BlockSpec((N_HEADS, D_MODEL), lambda i: (0, 0)),
                   pl.BlockSpec((c, D_MODEL), lambda i: (rev(i), 0)),
                   pl.BlockSpec((N_HEADS, LANES), lambda i: (0, 0))],
        out_shape=[jax.ShapeDtypeStruct((N_HEADS, D_MODEL), F32), jax.ShapeDtypeStruct((s, D_MODEL), F32),
                   jax.ShapeDtypeStruct((N_HEADS, LANES), F32)],
        scratch_shapes=[pltpu.VMEM((N_HEADS, LANES), F32)],
        compiler_params=_params(("arbitrary",)),
    )(dcum, xf, h, wf_t)


def _to_cum4(v, t):
    s = v.shape[1]
    return v.reshape(N_PAIRS, 2, s // t, t).transpose(0, 2, 1, 3)


def _from_cum4(v4):
    p, nt, two, t = v4.shape
    return v4.transpose(0, 2, 1, 3).reshape(p * two, nt * t)


def _alibi_slopes():
    return (2.0 ** (-8.0 * np.arange(1, N_HEADS + 1, dtype=np.float32) / N_HEADS)).astype(np.float32)


def _per_head_lanes(v):
    return jnp.repeat(v.astype(F32).reshape(N_PAIRS, 1, 2), LANES, axis=2)


def _attn_a_specs(s):
    q = _pair_spec(s)
    k = pl.BlockSpec((s, LANES), lambda p: (0, N_PAIRS + p // 8))
    v = pl.BlockSpec((s, LANES), lambda p: (0, N_PAIRS + KV_A // LANES + p // 8))
    head = pl.BlockSpec((1, 1, 2 * LANES), lambda p: (p, 0, 0))
    return q, k, v, head


def _attn_a_geometry(p, slope_ref, sink_ref):
    kv_half = (p // 4) % 2
    kv_first = kv_half == 0
    lane_first = _lane_is_first_head()
    kv_lanes = (lax.broadcasted_iota(jnp.int32, (1, LANES), 1) // HEAD_DIM) == kv_half
    row = lax.broadcasted_iota(jnp.int32, (2 * WINDOW, 2 * WINDOW), 0)
    cj = lax.broadcasted_iota(jnp.int32, (2 * WINDOW, 2 * WINDOW), 1)
    second = row >= WINDOW
    dist = WINDOW + jnp.where(second, row - WINDOW, row) - cj
    valid = (dist >= 0) & (dist < WINDOW)
    per_row = lambda ref: jnp.where(second[:, 0:1], ref[0, :, LANES:LANES + 1], ref[0, :, 0:1])
    return kv_first, lane_first, kv_lanes, per_row(slope_ref) * dist.astype(F32), valid, per_row(sink_ref)


def _swap_halves(x):
    return pltpu.roll(x, HEAD_DIM, 1)


def _attn_a_fwd(qkv, slopes, sinks, name, exchange=None):
    s = qkv.shape[0]
    nb = s // WINDOW

    def body(q_ref, k_ref, v_ref, sl_ref, sk_ref, o_ref, lse_ref):
        kv_first, lane_first, kv_lanes, bias, valid, sink = _attn_a_geometry(pl.program_id(0), sl_ref, sk_ref)

        def block(r0, k0, width):
            q2 = q_ref[pl.ds(r0, WINDOW), :].astype(F32) * Q_SCALE
            q2r = _swap_halves(q2)
            xs = jnp.concatenate([jnp.where(kv_first, q2, q2r), jnp.where(kv_first, q2r, q2)], axis=0).astype(BF16)
            km = jnp.where(kv_lanes, k_ref[pl.ds(k0, width), :], 0).astype(BF16)
            vm = jnp.where(kv_lanes, v_ref[pl.ds(k0, width), :], 0).astype(BF16)
            sc = lax.dot_general(xs, km, _NT, preferred_element_type=F32) - bias[:, 2 * WINDOW - width:]
            sc = jnp.where(valid[:, 2 * WINDOW - width:], sc, NEG)
            m = jnp.maximum(jnp.max(sc, axis=1, keepdims=True), sink)
            pr = jnp.exp(sc - m)
            l = jnp.sum(pr, axis=1, keepdims=True) + jnp.exp(sink - m)
            os = jnp.dot(pr.astype(BF16), vm, preferred_element_type=F32) * (1.0 / l)
            lse = m + jnp.log(l)
            lse_ref[pl.ds(r0, WINDOW), 0:LANES] = _bcast_lanes(lse[:WINDOW])
            lse_ref[pl.ds(r0, WINDOW), LANES:2 * LANES] = _bcast_lanes(lse[WINDOW:])
            oa = jnp.where(kv_first, os[:WINDOW], _swap_halves(os[:WINDOW]))
            ob = jnp.where(kv_first, _swap_halves(os[WINDOW:]), os[WINDOW:])
            o_ref[pl.ds(r0, WINDOW), :] = jnp.where(lane_first, oa, ob)

        block(0, 0, WINDOW)

        def loop(n, _):
            r0 = pl.multiple_of(n * WINDOW, WINDOW)
            block(r0, pl.multiple_of(r0 - WINDOW, WINDOW), 2 * WINDOW)
            return 0

        _two_at_a_time(nb - 1, lambda n, c: loop(n + 1, c), 0)

    q, k, v, head = _attn_a_specs(s)
    return _grid_call(
        body, name=name, grid=(N_PAIRS,),
        in_specs=[q, k, v, head, head],
        out_specs=[_pair_spec(s), _stat_spec(s)],
        out_shape=[jax.ShapeDtypeStruct((s, BRANCH), F32), jax.ShapeDtypeStruct((s, N_HEADS * LANES), F32)],
        args=(qkv, qkv, qkv, slopes, sinks), semantics=("parallel",), exchange=exchange)


def _attn_a_bwd(qkv, slopes, sinks, o, lse, do, name, exchange=None):
    s = qkv.shape[0]
    nb = s // WINDOW

    def body(q_ref, k_ref, v_ref, sl_ref, sk_ref, o_ref, lse_ref, do_ref, dq_ref, dk_ref, dv_ref, dsk_ref):
        p_id = pl.program_id(0)
        kv_first, lane_first, kv_lanes, bias, valid, sink = _attn_a_geometry(p_id, sl_ref, sk_ref)

        @pl.when(p_id % 8 == 0)
        def _():
            dk_ref[...] = jnp.zeros_like(dk_ref)
            dv_ref[...] = jnp.zeros_like(dv_ref)

        def align(v2):
            v2r = _swap_halves(v2)
            both = jnp.concatenate([jnp.where(kv_first, v2, v2r), jnp.where(kv_first, v2r, v2)], axis=0)
            return jnp.where(kv_lanes, both, 0.0).astype(BF16)

        def block(r0, k0, width, sink_sum):
            xq = align(q_ref[pl.ds(r0, WINDOW), :].astype(F32) * Q_SCALE)
            do2 = do_ref[pl.ds(r0, WINDOW), :].astype(F32)
            xdo = align(do2)
            delta = jnp.concatenate(_rowsum_heads(do2 * o_ref[pl.ds(r0, WINDOW), :], lane_first), axis=0)
            lse = jnp.concatenate([lse_ref[pl.ds(r0, WINDOW), 0:1], lse_ref[pl.ds(r0, WINDOW), LANES:LANES + 1]], axis=0)
            km = jnp.where(kv_lanes, k_ref[pl.ds(k0, width), :], 0).astype(BF16)
            vm = jnp.where(kv_lanes, v_ref[pl.ds(k0, width), :], 0).astype(BF16)
            sc = lax.dot_general(xq, km, _NT, preferred_element_type=F32) - bias[:, 2 * WINDOW - width:]
            pr = jnp.where(valid[:, 2 * WINDOW - width:], jnp.exp(sc - lse), 0.0)
            ds = pr * (lax.dot_general(xdo, vm, _NT, preferred_element_type=F32) - delta)
            dsb = ds.astype(BF16)
            dq_al = jnp.dot(dsb, km, preferred_element_type=F32)
            dk_ref[pl.ds(k0, width), :] += lax.dot_general(dsb, xq, _TN, preferred_element_type=F32)
            dv_ref[pl.ds(k0, width), :] += lax.dot_general(pr.astype(BF16), xdo, _TN, preferred_element_type=F32)
            dqa = jnp.where(kv_first, dq_al[:WINDOW], _swap_halves(dq_al[:WINDOW]))
            dqb = jnp.where(kv_first, _swap_halves(dq_al[WINDOW:]), dq_al[WINDOW:])
            dq_ref[pl.ds(r0, WINDOW), :] = (jnp.where(lane_first, dqa, dqb) * Q_SCALE).astype(BF16)
            return sink_sum + jnp.exp(sink - lse) * delta

        sink_sum = block(0, 0, WINDOW, jnp.zeros((2 * WINDOW, 1), F32))

        def loop(n, c):
            r0 = pl.multiple_of(n * WINDOW, WINDOW)
            return block(r0, pl.multiple_of(r0 - WINDOW, WINDOW), 2 * WINDOW, c)

        sink_sum = _two_at_a_time(nb - 1, lambda n, c: loop(n + 1, c), sink_sum)
        dsk_ref[0, :, 0:LANES] = jnp.broadcast_to(-jnp.sum(sink_sum[:WINDOW], axis=0, keepdims=True), (1, LANES))
        dsk_ref[0, :, LANES:2 * LANES] = jnp.broadcast_to(-jnp.sum(sink_sum[WINDOW:], axis=0, keepdims=True), (1, LANES))

    q, k, v, head = _attn_a_specs(s)
    kv_out = pl.BlockSpec((s, LANES), lambda p: (0, p // 8))
    return _grid_call(
        body, name=name, grid=(N_PAIRS,),
        in_specs=[q, k, v, head, head, _pair_spec(s), _stat_spec(s), _pair_spec(s)],
        out_specs=[_pair_spec(s), kv_out, kv_out, head],
        out_shape=[jax.ShapeDtypeStruct((s, BRANCH), BF16), jax.ShapeDtypeStruct((s, KV_A), F32),
                   jax.ShapeDtypeStruct((s, KV_A), F32), jax.ShapeDtypeStruct((N_PAIRS, 1, 2 * LANES), F32)],
        args=(qkv, qkv, qkv, slopes, sinks, o, lse, do), semantics=("arbitrary",), exchange=exchange)


def _layer_kind(i):
    return i % 3, i // 3


GATHER_FIRST = [("in", 0)]
GATHER_BEHIND = {("qkv", 0): [("out", 0)], ("attn", 0): [("in", 1)], ("attn", 1): [("out", 1), ("in", 2), ("out", 2)],
                 ("attn", 2): [("in", 3), ("out", 3)]}


def _forward_backward(x, target, g_pre, g_post, sinks_a, b_f_c, shards, chip, place):
    s = x.shape[0]
    slopes = _per_head_lanes(jnp.asarray(_alibi_slopes()))
    w_in, w_out, wf_t = {}, {}, {}

    def lands_side_by_side(key):
        return key[0] == "in" and shards[key].shape[1] % LANES == 0

    def gather(keys):
        return _GatherExchange([shards[k] for k in keys], [lands_side_by_side(k) for k in keys])

    def deliver(keys, gathered):
        for key, g in zip(keys, gathered):
            side, layer = key
            sh = shards[key]
            if side == "out":
                g = lax.dynamic_update_slice(g, sh[None], (chip, 0, 0))
                w_out[layer] = g.reshape(4 * sh.shape[0], sh.shape[1])
            elif lands_side_by_side(key):
                w_in[layer] = _place_columns(g, sh, chip, f"own_block_in_l{layer}")
            else:
                g = lax.dynamic_update_slice(g, sh[None], (chip, 0, 0))
                w = g.transpose(1, 0, 2).reshape(sh.shape[0], 4 * sh.shape[1])
                w_in[layer], wf_t[layer] = w[:, :4 * BRANCH], w[:, 4 * BRANCH:].T

    deliver(GATHER_FIRST, _exchange_call(gather(GATHER_FIRST), "gather_first_weights"))
    saved = []
    for i in range(DEPTH):
        kind, j = _layer_kind(i)
        tag = f"l{i}"
        w = w_in[i]
        nqkv = A_QKV if kind == 0 else B_QKV
        tn = 512 if kind == 0 else 1024
        h, h_t = _rmsnorm_fwd(x, g_pre[i:i + 1], f"prenorm_{tag}")
        behind = GATHER_BEHIND.get(("qkv", i))
        qkv = _matmul(h, w, out_dtype=BF16, name=f"inproj_qkv_{tag}", n=nqkv, tn=tn,
                      exchange=gather(behind) if behind else None)
        if behind:
            qkv, arrived = qkv
            deliver(behind, arrived)
        z = _matmul(h, w, out_dtype=F32, name=f"inproj_gate_{tag}", n=BRANCH, b_off=nqkv // tn, tn=tn)
        behind = GATHER_BEHIND.get(("attn", i))
        exchange = gather(behind) if behind else None
        if kind == 0:
            sink_l = _per_head_lanes(sinks_a[j])
            (o, lse), arrived = _attn_a_fwd(qkv, slopes, sink_l, f"attn_a_fwd_{tag}", exchange)
            extra = (sink_l, lse)
        elif kind == 1:
            (o, extra), arrived = _attn_b_fwd(qkv, f"attn_b_fwd_{tag}", exchange)
        else:
            b_col = jnp.broadcast_to(b_f_c[j].astype(F32)[:, None], (N_HEADS, LANES))
            xf, cum = _fgate_fwd(h, wf_t[i], b_col, f"fgate_fwd_{tag}")
            cum4 = _to_cum4(cum, _fox_tile(s))
            (o, lse), arrived = _attn_c_fwd(qkv, cum4, f"attn_c_fwd_{tag}", exchange)
            extra = (xf, cum4, lse)
        if behind:
            deliver(behind, arrived)
        x_next, y, u_t = _gated_out_proj(o, z, w_out[i], x, g_post[i:i + 1], f"outproj_{tag}")
        saved.append((x, h, h_t, qkv, z, o, u_t, y, extra))
        x = x_next

    dx, loss_part = _loss_and_grad(x, target)

    d_g_pre, d_g_post = [None] * DEPTH, [None] * DEPTH
    d_sinks = [None, None]
    d_b_f = None
    reduced = {}
    pending = None

    def finish_reduce(layer, side, own, arr):
        kind, j = _layer_kind(layer)
        reduced[(side, kind)] = _sum_chips(own, arr, place, f"shard_sum_{side}_l{layer}", j, 2 if kind == 0 else 1,
                                           into=reduced.get((side, kind)))

    for i in reversed(range(DEPTH)):
        kind, j = _layer_kind(i)
        tag = f"l{i}"
        x_in, h, h_t, qkv, z, o, u_t, y, extra = saved[i]
        tn = 512 if kind == 0 else 1024
        (dy, d_g_post[i], do, dz), _ = _gated_out_proj_bwd(dx, y, g_post[i:i + 1], w_out[i], o, z, f"outproj_bwd_{tag}")
        dw_out = _matmul(u_t, dy, out_dtype=BF16, name=f"dw_out_{tag}")
        dw_out = dw_out.reshape(4, dw_out.shape[0] // 4, dw_out.shape[1])
        dh_f = None
        exchange = _SiblingExchange([dw_out])
        if pending:
            exchange = _BothExchanges(exchange, _ScatterExchange([pending[1]]))
        if kind == 0:
            sink_l, lse = extra
            (dq, dk, dv, dsk), arrived = _attn_a_bwd(qkv, slopes, sink_l, o, lse, do, f"attn_a_bwd_{tag}", exchange)
            d_sinks[j] = dsk[:, 0, ::LANES].reshape(N_HEADS)
            parts = [dq, dk.astype(BF16), dv.astype(BF16), dz]
        elif kind == 1:
            (dq, dk, dv), arrived = _attn_b_bwd(qkv, extra, do, f"attn_b_bwd_{tag}", exchange)
            parts = [dq, dk, dv, dz]
        else:
            xf, cum4, lse = extra
            (dq, dk, dv, dcum4), arrived = _attn_c_bwd(qkv, cum4, o, lse, do, f"attn_c_bwd_{tag}", exchange)
            d_wf_t, dh_f, db = _fgate_bwd(_from_cum4(dcum4), xf, h, wf_t[i], f"fgate_bwd_{tag}")
            d_b_f = db[:, 0]
            parts = [dq, dk, dv, dz]
        sum_out = _add_pairs(dw_out, arrived[0], place, f"chip_sum_out_{tag}")
        if pending:
            finish_reduce(pending[0], "in", pending[1], arrived[1])
        dproj = jnp.concatenate(parts, axis=1)
        scatter_out = _ScatterExchange([sum_out])
        if kind == 2:
            dw_in, arrived = _matmul(h_t, dproj, out_dtype=F32, name=f"dw_in_{tag}", tn=tn, exchange=scatter_out)
            dw_in = jnp.concatenate([dw_in, d_wf_t.T], axis=1)
            dw_in = dw_in.reshape(dw_in.shape[0], 4, dw_in.shape[1] // 4).transpose(1, 0, 2).astype(BF16)
        else:
            dw_in, arrived = _matmul(h_t, dproj, out_dtype=BF16, name=f"dw_in_{tag}", col_blocks=4,
                                     tn=1152 if kind == 0 else 1024, exchange=scatter_out)
        finish_reduce(i, "out", sum_out, arrived[0])
        (dx, d_g_pre[i]), (their_in,) = _in_proj_bwd(
            dproj, w_in[i], dh_f, dx, x_in, g_pre[i:i + 1], f"inproj_bwd_{tag}", 1536 if kind == 0 else 1024,
            exchange=_SiblingExchange([dw_in]))
        pending = (i, _add_pairs(dw_in, their_in, place, f"chip_sum_in_{tag}"))

    arrived = _exchange_call(_ScatterExchange([pending[1]]), "grad_chip_scatter_last")
    finish_reduce(pending[0], "in", pending[1], arrived[0])

    return dict(loss=loss_part, dx=dx, g_pre=jnp.concatenate(d_g_pre, axis=0), g_post=jnp.concatenate(d_g_post, axis=0),
                sinks_a=jnp.stack(d_sinks), b_f_c=d_b_f[None, :], reduced=reduced)


def _place():
    x, y, c = lax.axis_index("x"), lax.axis_index("y"), lax.axis_index("c")
    others = [(1 - x, y), (x, 1 - y), (1 - x, 1 - y)]
    return x, y, c, others


def _half_rows(ref_rows, which):
    half = ref_rows // 2
    return pl.ds(pl.multiple_of(which * half, half), half)


def _remote(src, dst, sems, k, device):
    send, recv = sems
    return pltpu.make_async_remote_copy(src_ref=src, dst_ref=dst, send_sem=send.at[k], recv_sem=recv.at[k],
                                        device_id=device, device_id_type=MESH)


def _hbm_call(body, name, ins, out_shapes, n_remote, aliases=None):
    any_spec = pl.BlockSpec(memory_space=pl.ANY)
    return pl.pallas_call(
        body, name=name, in_specs=[any_spec] * len(ins), out_specs=[any_spec] * len(out_shapes),
        out_shape=out_shapes, input_output_aliases=aliases or {},
        scratch_shapes=[pltpu.SemaphoreType.DMA((n_remote,)), pltpu.SemaphoreType.DMA((n_remote,))],
    )(*ins)


class _GatherExchange:
    SEMS = 8

    def __init__(self, shards, side_by_side):
        self.ins = list(shards)
        self.side_by_side = list(side_by_side)
        self.out_shapes = [jax.ShapeDtypeStruct((a.shape[0], 4 * a.shape[1]) if wide else (4,) + a.shape, a.dtype)
                           for a, wide in zip(shards, side_by_side)]
        self.n_sems = self.SEMS * len(shards)
        self.aliases = {}

    def _copies(self, ins, outs, sems):
        x, y, c, _ = _place()
        me, diag = 2 * x + y, 2 * (1 - x) + (1 - y)
        nbr = [((1 - x, y, c), 2 * (1 - x) + y), ((x, 1 - y, c), 2 * x + (1 - y))]
        sibling = (x, y, 1 - c)
        table = []
        for w, (src, dst, wide) in enumerate(zip(ins, outs, self.side_by_side)):
            rows, cols = src.shape
            half, quarter = rows // 2, rows // 4

            def slot(chip, core, piece=None, dst=dst, wide=wide, cols=cols, half=half, quarter=quarter):
                start, size = (core * half, half) if piece is None else (core * half + piece * quarter, quarter)
                which = pl.ds(pl.multiple_of(start, quarter), size)
                return dst.at[which, pl.ds(pl.multiple_of(chip * cols, LANES), cols)] if wide else dst.at[chip, which]

            k0 = self.SEMS * w
            cp = lambda s_, d_, k, dev: _remote(s_, d_, sems, k0 + k, dev)
            mine_src = src.at[pl.ds(pl.multiple_of(c * half, half), half)]
            d = dict(
                send=[cp(mine_src, slot(me, c), k, nbr[k][0]) for k in range(2)],
                got=[cp(slot(nbr[k][1], c), slot(nbr[k][1], c), k, nbr[k][0]) for k in range(2)],
                fwd=[cp(slot(nbr[k][1], c, k), slot(nbr[k][1], c, k), 2 + k, nbr[1 - k][0]) for k in range(2)],
                got_fwd=[cp(slot(diag, c, k), slot(diag, c, k), 2 + k, nbr[1 - k][0]) for k in range(2)],
                pass_=[cp(slot(nbr[k][1], c), slot(nbr[k][1], c), 4 + k, sibling) for k in range(2)]
                + [cp(slot(diag, c, k), slot(diag, c, k), 6 + k, sibling) for k in range(2)],
                got_pass=[cp(slot(nbr[k][1], 1 - c), slot(nbr[k][1], 1 - c), 4 + k, sibling) for k in range(2)]
                + [cp(slot(diag, 1 - c, k), slot(diag, 1 - c, k), 6 + k, sibling) for k in range(2)])
            table.append(d)
        return table

    def start(self, ins, outs, sems):
        for d in self._copies(ins, outs, sems):
            for cp in d["send"]:
                cp.start()

    def mid(self, ins, outs, sems):
        for d in self._copies(ins, outs, sems):
            for k in range(2):
                d["got"][k].wait_recv()
                d["fwd"][k].start()
                d["pass_"][k].start()

    def finish(self, ins, outs, sems):
        table = self._copies(ins, outs, sems)
        for d in table:
            for k in range(2):
                d["got_fwd"][k].wait_recv()
                d["pass_"][2 + k].start()
        for d in table:
            for cp in d["got_pass"]:
                cp.wait_recv()
            for cp in d["send"] + d["fwd"] + d["pass_"]:
                cp.wait_send()


class _SemaphoresFrom:
    def __init__(self, ref, start):
        self._ref, self._start = ref, start

    @property
    def at(self):
        return self

    def __getitem__(self, k):
        return self._ref.at[self._start + k]


class _BothExchanges:
    def __init__(self, first, second):
        self.parts = (first, second)
        self.ins = first.ins + second.ins
        self.out_shapes = first.out_shapes + second.out_shapes
        self.n_sems = first.n_sems + second.n_sems
        self.aliases = {}

    def _each(self, phase, ins, outs, sems):
        i0 = o0 = s0 = 0
        for ex in self.parts:
            n_in, n_out = len(ex.ins), len(ex.out_shapes)
            getattr(ex, phase)(ins[i0:i0 + n_in], outs[o0:o0 + n_out], tuple(_SemaphoresFrom(r, s0) for r in sems))
            i0, o0, s0 = i0 + n_in, o0 + n_out, s0 + ex.n_sems

    def start(self, ins, outs, sems):
        self._each("start", ins, outs, sems)

    def mid(self, ins, outs, sems):
        self._each("mid", ins, outs, sems)

    def finish(self, ins, outs, sems):
        self._each("finish", ins, outs, sems)


def _place_columns(wide, block, chip, name):
    rows, cc = block.shape
    tr = min(512, rows)

    def body(c_ref, b_ref, w_ref, o_ref):
        o_ref[...] = b_ref[...]

    return pl.pallas_call(
        body, name=name,
        grid_spec=pltpu.PrefetchScalarGridSpec(
            num_scalar_prefetch=1, grid=(rows // tr,),
            in_specs=[pl.BlockSpec((tr, cc), lambda r, c_ref: (r, 0)), pl.BlockSpec(memory_space=pl.ANY)],
            out_specs=pl.BlockSpec((tr, cc), lambda r, c_ref: (r, c_ref[0]))),
        out_shape=jax.ShapeDtypeStruct(wide.shape, wide.dtype), input_output_aliases={2: 0},
        compiler_params=_params(("parallel",)),
    )(chip.astype(jnp.int32).reshape(1), block, wide)


def _exchange_call(ex, name):
    n_in, n_out = len(ex.ins), len(ex.out_shapes)

    def body(*refs):
        ins, outs, sems = refs[:n_in], refs[n_in:n_in + n_out], refs[n_in + n_out:]
        ex.start(ins, outs, sems)
        ex.mid(ins, outs, sems)
        ex.finish(ins, outs, sems)

    return _hbm_call(body, name, ex.ins, ex.out_shapes, ex.n_sems, aliases=ex.aliases)


def _grid_call(body, *, name, grid, in_specs, out_specs, out_shape, args, scratch_shapes=(), semantics, exchange=None):
    if exchange is None:
        res = pl.pallas_call(body, name=name, grid=grid, in_specs=list(in_specs), out_specs=list(out_specs),
                             out_shape=list(out_shape), scratch_shapes=list(scratch_shapes),
                             compiler_params=_params(semantics))(*args)
        return res, []
    n_in, n_out, n_scr = len(args), len(out_shape), len(scratch_shapes)
    x_in, x_out = len(exchange.ins), len(exchange.out_shapes)
    steps = math.prod(grid)

    def wrapped(*refs):
        core_in, ex_in = refs[:n_in], refs[n_in:n_in + x_in]
        rest = refs[n_in + x_in:]
        core_out, ex_out = rest[:n_out], rest[n_out:n_out + x_out]
        scratch, sems = rest[n_out + x_out:n_out + x_out + n_scr], rest[n_out + x_out + n_scr:]
        step = 0
        for axis, extent in enumerate(grid):
            step = step * extent + pl.program_id(axis)

        @pl.when(step == 0)
        def _():
            exchange.start(ex_in, ex_out, sems)

        body(*core_in, *core_out, *scratch)

        @pl.when(step == max((3 * steps) // 4 - 1, 0))
        def _():
            exchange.mid(ex_in, ex_out, sems)

        @pl.when(step == steps - 1)
        def _():
            exchange.finish(ex_in, ex_out, sems)

    any_spec = pl.BlockSpec(memory_space=pl.ANY)
    res = pl.pallas_call(
        wrapped, name=name, grid=grid,
        in_specs=list(in_specs) + [any_spec] * x_in, out_specs=list(out_specs) + [any_spec] * x_out,
        out_shape=list(out_shape) + list(exchange.out_shapes),
        input_output_aliases={n_in + a: n_out + b for a, b in exchange.aliases.items()},
        scratch_shapes=list(scratch_shapes) + [pltpu.SemaphoreType.DMA((exchange.n_sems,)),
                                               pltpu.SemaphoreType.DMA((exchange.n_sems,))],
        compiler_params=_params(("arbitrary",) * len(grid)),
    )(*args, *exchange.ins)
    return res[:n_out], res[n_out:]


class _SiblingExchange:
    def __init__(self, parts):
        self.ins = list(parts)
        self.out_shapes = [jax.ShapeDtypeStruct((4, a.shape[1] // 2, a.shape[2]), a.dtype) for a in parts]
        self.n_sems = len(parts)
        self.aliases = {}

    def _copies(self, ins, outs, sems):
        x, y, c, _ = _place()
        return [_remote(src.at[:, _half_rows(src.shape[1], 1 - c)], dst, sems, w, (x, y, 1 - c))
                for w, (src, dst) in enumerate(zip(ins, outs))]

    def start(self, ins, outs, sems):
        for cp in self._copies(ins, outs, sems):
            cp.start()

    def mid(self, ins, outs, sems):
        pass

    def finish(self, ins, outs, sems):
        for cp in self._copies(ins, outs, sems):
            cp.wait_recv()
            cp.wait_send()


class _ScatterExchange:
    SEMS = 6

    def __init__(self, sums):
        self.ins = list(sums)
        self.out_shapes = ([jax.ShapeDtypeStruct(a.shape, a.dtype) for a in sums]
                           + [jax.ShapeDtypeStruct((2, a.shape[1] // 2, a.shape[2]), a.dtype) for a in sums])
        self.n_sems = self.SEMS * len(sums)
        self.aliases = {}

    def _copies(self, ins, outs, sems):
        x, y, c, _ = _place()
        n = len(ins)
        me, diag = 2 * x + y, 2 * (1 - x) + (1 - y)
        nbr = [((1 - x, y, c), 2 * (1 - x) + y), ((x, 1 - y, c), 2 * x + (1 - y))]
        table = []
        for w, (src, dst, relay) in enumerate(zip(ins, outs[:n], outs[n:])):
            piece_rows = src.shape[1] // 2
            piece = lambda ref, chip, k, piece_rows=piece_rows: ref.at[chip, pl.ds(k * piece_rows, piece_rows)]
            cp = lambda s_, d_, k, dev, w=w: _remote(s_, d_, sems, self.SEMS * w + k, dev)
            table.append(dict(
                send=[cp(src.at[nbr[k][1]], dst.at[me], k, nbr[k][0]) for k in range(2)]
                + [cp(piece(src, diag, k), relay.at[k], 2 + k, nbr[k][0]) for k in range(2)],
                got=[cp(dst.at[nbr[k][1]], dst.at[nbr[k][1]], k, nbr[k][0]) for k in range(2)],
                got_relay=[cp(relay.at[k], relay.at[k], 2 + k, nbr[k][0]) for k in range(2)],
                on=[cp(relay.at[k], piece(dst, nbr[k][1], k), 4 + k, nbr[1 - k][0]) for k in range(2)],
                got_on=[cp(piece(dst, diag, k), piece(dst, diag, k), 4 + k, nbr[1 - k][0]) for k in range(2)]))
        return table

    def start(self, ins, outs, sems):
        for d in self._copies(ins, outs, sems):
            for cp in d["send"]:
                cp.start()

    def mid(self, ins, outs, sems):
        for d in self._copies(ins, outs, sems):
            for k in range(2):
                d["got_relay"][k].wait_recv()
                d["on"][k].start()

    def finish(self, ins, outs, sems):
        table = self._copies(ins, outs, sems)
        for d in table:
            for cp in d["got"] + d["got_on"]:
                cp.wait_recv()
        for d in table:
            for cp in d["send"] + d["on"]:
                cp.wait_send()


def _sibling_join(shards):
    n = len(shards)

    def body(*refs):
        ins, outs, sems = refs[:n], refs[n:2 * n], refs[2 * n:2 * n + 2]
        x, y, c, _ = _place()
        pend = []
        for w in range(n):
            rows = ins[w].shape[1]
            mine, theirs = _half_rows(rows, c), _half_rows(rows, 1 - c)
            cp = _remote(ins[w].at[:, mine], outs[w].at[:, mine], sems, w, (x, y, 1 - c))
            cp.start()
            pend.append((cp, _remote(ins[w].at[:, theirs], outs[w].at[:, theirs], sems, w, (x, y, 1 - c))))
        for cp, landed in pend:
            landed.wait_recv()
            cp.wait_send()

    out_shapes = [jax.ShapeDtypeStruct(a.shape, a.dtype) for a in shards]
    return _hbm_call(body, "grad_sibling_join", shards, out_shapes, n, aliases={w: w for w in range(n)})


SMALL_ROWS = 136


def _all_reduce_small(vec):
    def body(v_ref, o_ref, buf, send, recv, loc):
        x, y, c, _ = _place()
        me = 4 * x + 2 * y + c
        lc = pltpu.make_async_copy(v_ref, buf.at[me], loc.at[0])
        lc.start()
        cps = []
        for k in range(1, 8):
            fx, fy, fc = (k >> 2) & 1, (k >> 1) & 1, k & 1
            peer = (x ^ fx, y ^ fy, c ^ fc)
            cp = pltpu.make_async_remote_copy(src_ref=v_ref, dst_ref=buf.at[me], send_sem=send.at[k - 1],
                                              recv_sem=recv.at[k - 1], device_id=peer, device_id_type=MESH)
            cp.start()
            cps.append((cp, 4 * peer[0] + 2 * peer[1] + peer[2]))
        for k, (cp, src) in enumerate(cps):
            pltpu.make_async_remote_copy(src_ref=v_ref, dst_ref=buf.at[src], send_sem=send.at[k], recv_sem=recv.at[k],
                                         device_id=(x, y, c), device_id_type=MESH).wait_recv()
        for cp, _ in cps:
            cp.wait_send()
        lc.wait()
        total = buf[0]
        for k in range(1, 8):
            total = total + buf[k]
        o_ref[...] = total

    vm = pl.BlockSpec(memory_space=pltpu.VMEM)
    return pl.pallas_call(
        body, name="all_reduce_small", in_specs=[vm], out_specs=vm,
        out_shape=jax.ShapeDtypeStruct(vec.shape, F32),
        scratch_shapes=[pltpu.VMEM((8,) + vec.shape, F32), pltpu.SemaphoreType.DMA((7,)),
                        pltpu.SemaphoreType.DMA((7,)), pltpu.SemaphoreType.DMA((1,))],
    )(vec)


SUM_ROWS = 256


def _add_pairs(part, theirs, place, name):
    four, rh, cc = theirs.shape
    tr = min(SUM_ROWS, rh)
    halves = part.reshape(four, 2, rh, cc)

    def body(p_ref, a_ref, b_ref, o_ref):
        o_ref[0] = (a_ref[0, 0].astype(F32) + b_ref[0].astype(F32)).astype(o_ref.dtype)

    spec = pl.BlockSpec((1, tr, cc), lambda k, r, p_ref: (k, r, 0))
    return pl.pallas_call(
        body, name=name,
        grid_spec=pltpu.PrefetchScalarGridSpec(
            num_scalar_prefetch=1, grid=(four, rh // tr),
            in_specs=[pl.BlockSpec((1, 1, tr, cc), lambda k, r, p_ref: (k, p_ref[1], r, 0)), spec], out_specs=spec),
        out_shape=jax.ShapeDtypeStruct(theirs.shape, theirs.dtype),
        compiler_params=_params(("parallel", "parallel")),
    )(place, halves, theirs)


def _sum_chips(own, arrived, place, name, layer, n_layers, into=None):
    four, rh, cc = own.shape
    tr = min(SUM_ROWS, rh)
    nr = rh // tr

    def body(p_ref, own_ref, arr_ref, *rest):
        o_ref = rest[-1]
        x, y = lax.axis_index("x"), lax.axis_index("y")
        tot = own_ref[0].astype(F32)
        for px, py in ((1 - x, y), (x, 1 - y), (1 - x, 1 - y)):
            tot = tot + arr_ref[2 * px + py].astype(F32)
        o_ref[0] = tot

    in_specs = [pl.BlockSpec((1, tr, cc), lambda r, p_ref: (p_ref[0], r, 0)),
                pl.BlockSpec((4, tr, cc), lambda r, p_ref: (0, r, 0))]
    args, aliases = [place, own, arrived], {}
    if into is not None:
        in_specs.append(pl.BlockSpec(memory_space=pl.ANY))
        args.append(into)
        aliases = {3: 0}
    return pl.pallas_call(
        body, name=name,
        grid_spec=pltpu.PrefetchScalarGridSpec(
            num_scalar_prefetch=1, grid=(nr,), in_specs=in_specs,
            out_specs=pl.BlockSpec((1, tr, cc), lambda r, p_ref: (layer, p_ref[1] * nr + r, 0))),
        out_shape=jax.ShapeDtypeStruct((n_layers, 2 * rh, cc), F32), input_output_aliases=aliases,
        compiler_params=_params(("parallel",)),
    )(*args)


ADAM_ROWS = 256


def _adamw(w, g, m, v, name):
    shape = w.shape
    as3 = lambda a: a.reshape((-1,) + shape[-2:])
    layers, rows, cc = as3(w).shape
    by_rows = rows % min(ADAM_ROWS, rows) == 0
    tr, tc = (min(ADAM_ROWS, rows), cc) if by_rows else (rows, ADAM_ROWS)
    assert rows % tr == 0 and cc % tc == 0
    c1 = 1.0 - ADAM_B1 ** ADAM_STEP
    c2 = 1.0 - ADAM_B2 ** ADAM_STEP

    def body(w_ref, g_ref, m_ref, v_ref, d_ref, nm_ref, nv_ref):
        gv = g_ref[...]
        nm = ADAM_B1 * m_ref[...] + (1.0 - ADAM_B1) * gv
        nv = ADAM_B2 * v_ref[...] + (1.0 - ADAM_B2) * (gv * gv)
        nm_ref[...] = nm
        nv_ref[...] = nv
        d_ref[...] = -ADAM_LR * ((nm / c1) / (jnp.sqrt(nv / c2) + ADAM_EPS) + ADAM_WD * w_ref[...])

    spec = pl.BlockSpec((1, tr, tc), (lambda l, i: (l, i, 0)) if by_rows else (lambda l, i: (l, 0, i)))
    sh = jax.ShapeDtypeStruct((layers, rows, cc), F32)
    outs = pl.pallas_call(
        body, name=name, grid=(layers, (rows // tr) * (cc // tc)), in_specs=[spec] * 4, out_specs=[spec] * 3,
        out_shape=[sh] * 3,
        compiler_params=_params(("parallel", "parallel")),
    )(as3(w), as3(g), as3(m), as3(v))
    return [o.reshape(shape) for o in outs]


def _pack_small(g_pre, g_post, sinks_a, b_f_c, loss_row):
    pad = lambda a: jnp.pad(a.reshape(1, -1).astype(F32), ((0, 0), (0, LANES - a.size)))
    rows = [g_pre.astype(F32).reshape(-1, LANES), g_post.astype(F32).reshape(-1, LANES), pad(sinks_a), pad(b_f_c), loss_row]
    packed = jnp.concatenate(rows, axis=0)
    return jnp.pad(packed, ((0, SMALL_ROWS - packed.shape[0]), (0, 0)))


def _unpack_small(p):
    n = DEPTH * D_MODEL // LANES
    return (p[:n].reshape(DEPTH, D_MODEL), p[n:2 * n].reshape(DEPTH, D_MODEL), p[2 * n, :2 * N_HEADS].reshape(2, N_HEADS),
            p[2 * n + 1, :N_HEADS].reshape(1, N_HEADS), p[2 * n + 2, 0])


def kernel(x, g_pre, g_post, w_in_a, w_out_a, sinks_a, w_in_b, w_out_b, w_in_c, b_f_c, w_out_c, loss_target, m_g_pre, m_g_post, m_w_in_a, m_w_out_a, m_sinks_a, m_w_in_b, m_w_out_b, m_w_in_c, m_b_f_c, m_w_out_c, v_g_pre, v_g_post, v_w_in_a, v_w_out_a, v_sinks_a, v_w_in_b, v_w_out_b, v_w_in_c, v_b_f_c, v_w_out_c):
    big_w = [w_in_a, w_out_a, w_in_b, w_out_b, w_in_c, w_out_c]
    big_m = [m_w_in_a, m_w_out_a, m_w_in_b, m_w_out_b, m_w_in_c, m_w_out_c]
    big_v = [v_w_in_a, v_w_out_a, v_w_in_b, v_w_out_b, v_w_in_c, v_w_out_c]

    chip = 2 * lax.axis_index("x") + lax.axis_index("y")
    place = jnp.stack([chip, lax.axis_index("c")]).astype(jnp.int32)
    by_kind = {0: (w_in_a, w_out_a), 1: (w_in_b, w_out_b), 2: (w_in_c, w_out_c)}
    shards = {}
    for i in range(DEPTH):
        kind, j = _layer_kind(i)
        shards[("in", i)] = by_kind[kind][0][j].astype(BF16)
        shards[("out", i)] = by_kind[kind][1][j].astype(BF16)

    res = _forward_backward(x[0], loss_target[0], g_pre, g_post, sinks_a, b_f_c, shards, chip, place)
    names = ["w_in_a", "w_out_a", "w_in_b", "w_out_b", "w_in_c", "w_out_c"]
    grads = _sibling_join([res["reduced"][(side, kind)] for kind in range(3) for side in ("in", "out")])

    small = _unpack_small(_all_reduce_small(
        _pack_small(res["g_pre"], res["g_post"], res["sinks_a"], res["b_f_c"], res["loss"])))
    g_small, loss = small[:4], small[4]

    zero_row = jnp.zeros((1, LANES), F32)
    pk = lambda a: _pack_small(a[0], a[1], a[2], a[3], zero_row)
    sm = _adamw(pk([g_pre, g_post, sinks_a, b_f_c]), pk(g_small), pk([m_g_pre, m_g_post, m_sinks_a, m_b_f_c]),
                pk([v_g_pre, v_g_post, v_sinks_a, v_b_f_c]), "adamw_small")
    sm = [_unpack_small(a)[:4] for a in sm]
    turned = lambda a: jnp.swapaxes(a, 1, 2)
    grads = list(grads)
    g_c = lax.optimization_barrier(turned(grads[4]))
    grads[4] = turned(g_c)
    bigs = [[turned(o) for o in _adamw(turned(w), g_c, turned(m), turned(v), f"adamw_{nm}")] if nm == "w_in_c"
            else _adamw(w, g, m, v, f"adamw_{nm}") for w, g, m, v, nm in zip(big_w, grads, big_m, big_v, names)]

    def ordered(small4, big6):
        return [small4[0], small4[1], big6[0], big6[1], small4[2], big6[2], big6[3], big6[4], small4[3], big6[5]]

    out = [loss, res["dx"][None], *ordered(g_small, grads)]
    for k in range(3):
        out += ordered(sm[k], [b[k] for b in bigs])
    return tuple(out)
```

```python
import functools
import math

import numpy as np
import jax
import jax.numpy as jnp
from jax import lax
from jax.experimental import pallas as pl
from jax.experimental.pallas import tpu as pltpu

F32 = jnp.float32
BF16 = jnp.bfloat16

D_MODEL = 2048
DEPTH = 4
N_HEADS = 32
HEAD_DIM = 64
LANES = 128
N_PAIRS = N_HEADS * HEAD_DIM // LANES
BRANCH = N_HEADS * HEAD_DIM
N_KV_A = 4
KV_A = N_KV_A * HEAD_DIM
WINDOW = 128
NORM_EPS = 1e-6
NEG = -1e30
Q_SCALE = HEAD_DIM ** -0.5

A_QKV = BRANCH + 2 * KV_A
B_QKV = 3 * BRANCH

ADAM_LR = 0.001
ADAM_B1 = 0.9
ADAM_B2 = 0.999
ADAM_EPS = 1e-08
ADAM_WD = 0.01
ADAM_STEP = 10

MESH = pl.DeviceIdType.MESH

_NT = (((1,), (1,)), ((), ()))
_TN = (((0,), (0,)), ((), ()))


def _params(sem=None):
    return pltpu.CompilerParams(dimension_semantics=sem)


def _matmul(a, b, *, out_dtype, name, n=None, b_off=0, tm=1024, tn=1024, col_blocks=None, exchange=None):
    (m, k), nn = a.shape, (n or b.shape[1])
    tm, tn = min(tm, m), min(tn, nn)
    assert m % tm == 0 and nn % tn == 0, (name, m, nn, tm, tn)

    def body(a_ref, b_ref, o_ref):
        p = jnp.dot(a_ref[...], b_ref[...], preferred_element_type=F32)
        o_ref[...] = p.astype(o_ref.dtype).reshape(o_ref.shape)

    in_specs = [pl.BlockSpec((tm, k), lambda i, j: (i, 0)), pl.BlockSpec((k, tn), lambda i, j: (0, j + b_off))]
    if col_blocks is None:
        out_spec = pl.BlockSpec((tm, tn), lambda i, j: (i, j))
        out_shape = jax.ShapeDtypeStruct((m, nn), out_dtype)
    else:
        per = nn // col_blocks // tn
        assert per * tn * col_blocks == nn, (name, nn, tn, col_blocks)
        out_spec = pl.BlockSpec((1, tm, tn), lambda i, j: (j // per, i, j % per))
        out_shape = jax.ShapeDtypeStruct((col_blocks, m, nn // col_blocks), out_dtype)
    (res,), arrived = _grid_call(
        body, name=name, grid=(m // tm, nn // tn), in_specs=in_specs, out_specs=[out_spec], out_shape=[out_shape],
        args=(a, b), semantics=("parallel", "parallel"), exchange=exchange)
    return res if exchange is None else (res, arrived)


ROW_TILE = 256


def _row_call(body, name, ins, outs, *, s):
    tr = min(ROW_TILE, s)
    spec = {"row": lambda sh: pl.BlockSpec((tr, sh[1]), lambda i: (i, 0)),
            "vec": lambda sh: pl.BlockSpec((1, sh[1]), lambda i: (0, 0)),
            "col": lambda sh: pl.BlockSpec((sh[0], tr), lambda i: (0, i))}
    in_specs = [spec[kind](a.shape) for a, kind in ins]
    out_specs = [spec[kind](sh.shape) for sh, kind in outs]
    return pl.pallas_call(
        body, name=name, grid=(s // tr,), in_specs=in_specs, out_specs=out_specs,
        out_shape=[sh for sh, _ in outs],
        compiler_params=_params(("arbitrary",)),
    )(*[a for a, _ in ins])


def _rsqrt_ms(v):
    return lax.rsqrt(jnp.mean(v * v, axis=-1, keepdims=True) + NORM_EPS)


def _rmsnorm_fwd(x, g, name):
    s, d = x.shape

    def body(x_ref, g_ref, h_ref, ht_ref):
        xv = x_ref[...]
        h = xv * _rsqrt_ms(xv) * g_ref[...]
        h_ref[...] = h.astype(BF16)
        ht_ref[...] = h.T.astype(BF16)

    return _row_call(body, name, [(x, "row"), (g, "vec")],
                     [(jax.ShapeDtypeStruct((s, d), BF16), "row"), (jax.ShapeDtypeStruct((d, s), BF16), "col")], s=s)


PROJ_ROWS = 256


def _resident(shape):
    return pl.BlockSpec(shape, lambda i: (0,) * len(shape), pipeline_mode=pl.Buffered(1))


def _gated_out_proj(o, z, w_out, x, g, name):
    s, d = x.shape
    tm = min(PROJ_ROWS, s)

    def body(o_ref, z_ref, w_ref, x_ref, g_ref, xn_ref, y_ref, ut_ref):
        zv = z_ref[...]
        u = o_ref[...] * (zv * jax.nn.sigmoid(zv))
        ut_ref[...] = u.T.astype(BF16)
        y = jnp.dot(u.astype(BF16), w_ref[...], preferred_element_type=F32)
        y_ref[...] = y
        xn_ref[...] = x_ref[...] + y * _rsqrt_ms(y) * g_ref[...]

    row = pl.BlockSpec((tm, d), lambda i: (i, 0))
    return pl.pallas_call(
        body, name=name, grid=(s // tm,),
        in_specs=[row, row, _resident(w_out.shape), row, _resident((1, d))],
        out_specs=[row, row, pl.BlockSpec((d, tm), lambda i: (0, i))],
        out_shape=[jax.ShapeDtypeStruct((s, d), F32), jax.ShapeDtypeStruct((s, d), F32), jax.ShapeDtypeStruct((d, s), BF16)],
        compiler_params=_params(("parallel",)),
    )(o, z, w_out, x, g)


def _gated_out_proj_bwd(dx, y, g, w_out, o, z, name, exchange=None):
    s, d = dx.shape
    tm = min(PROJ_ROWS, s)

    def body(dx_ref, y_ref, g_ref, w_ref, o_ref, z_ref, dy_ref, dg_ref, do_ref, dz_ref):
        dy, dg = _norm_bwd_rows(dx_ref[...], y_ref[...], g_ref[...])
        dyb = dy.astype(BF16)
        dy_ref[...] = dyb

        @pl.when(pl.program_id(0) == 0)
        def _():
            dg_ref[...] = jnp.zeros_like(dg_ref)

        dg_ref[...] += jnp.sum(dg, axis=0, keepdims=True)
        du = lax.dot_general(dyb, w_ref[...], _NT, preferred_element_type=F32)
        zv = z_ref[...]
        sig = jax.nn.sigmoid(zv)
        do_ref[...] = (du * (zv * sig)).astype(BF16)
        dz_ref[...] = (du * o_ref[...] * (sig * (1.0 + zv * (1.0 - sig)))).astype(BF16)

    row = pl.BlockSpec((tm, d), lambda i: (i, 0))
    vec = pl.BlockSpec((1, d), lambda i: (0, 0))
    bf = jax.ShapeDtypeStruct((s, d), BF16)
    return _grid_call(
        body, name=name, grid=(s // tm,),
        in_specs=[row, row, _resident((1, d)), _resident(w_out.shape), row, row],
        out_specs=[row, vec, row, row], out_shape=[bf, jax.ShapeDtypeStruct((1, d), F32), bf, bf],
        args=(dx, y, g, w_out, o, z), semantics=("arbitrary",), exchange=exchange)


IN_BWD_ROWS = 512


def _in_proj_bwd(dproj, w_in, extra, dx, x, g, name, tk, exchange=None):
    s, d = x.shape
    k = dproj.shape[1]
    tm = min(IN_BWD_ROWS, s)
    nk = k // tk
    assert k % tk == 0 and s % tm == 0, (name, k, tk)
    has_extra = extra is not None

    def body(a_ref, b_ref, *rest):
        if has_extra:
            e_ref, rest = rest[0], rest[1:]
        dx_ref, x_ref, g_ref, o_ref, dg_ref, acc_ref = rest
        i, kk = pl.program_id(0), pl.program_id(1)
        p = lax.dot_general(a_ref[...], b_ref[...], _NT, preferred_element_type=F32)

        @pl.when(kk == 0)
        def _():
            acc_ref[...] = p

        @pl.when(kk > 0)
        def _():
            acc_ref[...] += p

        @pl.when((i == 0) & (kk == 0))
        def _():
            dg_ref[...] = jnp.zeros_like(dg_ref)

        @pl.when(kk == nk - 1)
        def _():
            def rows_chunk(c, _):
                r = pl.ds(pl.multiple_of(c * LANES, LANES), LANES)
                dh = acc_ref[r, :] + e_ref[r, :] if has_extra else acc_ref[r, :]
                dv, dg = _norm_bwd_rows(dh, x_ref[r, :], g_ref[...])
                o_ref[r, :] = dx_ref[r, :] + dv
                dg_ref[...] += jnp.sum(dg, axis=0, keepdims=True)
                return 0

            lax.fori_loop(0, tm // LANES, rows_chunk, 0)

    row = pl.BlockSpec((tm, d), lambda i, kk: (i, 0))
    vec = pl.BlockSpec((1, d), lambda i, kk: (0, 0))
    in_specs = [pl.BlockSpec((tm, tk), lambda i, kk: (i, kk)), pl.BlockSpec((d, tk), lambda i, kk: (0, kk))]
    args = [dproj, w_in]
    if has_extra:
        in_specs.append(row)
        args.append(extra)
    return _grid_call(
        body, name=name, grid=(s // tm, nk), in_specs=in_specs + [row, row, vec], out_specs=[row, vec],
        out_shape=[jax.ShapeDtypeStruct((s, d), F32), jax.ShapeDtypeStruct((1, d), F32)],
        args=tuple(args) + (dx, x, g), scratch_shapes=[pltpu.VMEM((tm, d), F32)], semantics=("arbitrary", "arbitrary"),
        exchange=exchange)


def _loss_and_grad(x, target):
    s, d = x.shape

    def body(x_ref, t_ref, dx_ref, l_ref):
        err = x_ref[...] - t_ref[...]
        dx_ref[...] = err * (1.0 / d)
        part = jnp.sum(jnp.sum(err * err, axis=1, keepdims=True), axis=0, keepdims=True) * (0.5 / d)

        @pl.when(pl.program_id(0) == 0)
        def _():
            l_ref[...] = jnp.zeros_like(l_ref)

        l_ref[...] += jnp.broadcast_to(part, l_ref.shape)

    return _row_call(body, "loss_head", [(x, "row"), (target, "row")],
                     [(jax.ShapeDtypeStruct((s, d), F32), "row"),
                      (jax.ShapeDtypeStruct((1, LANES), F32), "vec")], s=s)


def _norm_bwd_rows(dn, v, g):
    r = _rsqrt_ms(v)
    a = dn * g
    dv = r * (a - v * (r * r) * jnp.mean(a * v, axis=-1, keepdims=True))
    return dv, dn * v * r


def _lane_is_first_head():
    return lax.broadcasted_iota(jnp.int32, (1, LANES), 1) < HEAD_DIM


def _bcast_lanes(col):
    return jnp.broadcast_to(col, (col.shape[0], LANES))


def _pair_spec(s, off=0, width=LANES):
    return pl.BlockSpec((s, width), lambda p: (0, p + off))


def _stack_heads(pair, first):
    return jnp.concatenate([jnp.where(first, pair, 0), jnp.where(first, 0, pair)], axis=0).astype(BF16)


def _stacked_mask(t, strict):
    row = lax.broadcasted_iota(jnp.int32, (2 * t, t), 0)
    col = lax.broadcasted_iota(jnp.int32, (2 * t, t), 1)
    query = jnp.where(row >= t, row - t, row)
    return col < query if strict else col <= query


LOOP_UNROLL = 2


def _two_at_a_time(n, step, carry):
    def group(jj, c):
        for k in range(LOOP_UNROLL):
            c = step(LOOP_UNROLL * jj + k, c)
        return c

    carry = lax.fori_loop(0, n // LOOP_UNROLL, group, carry)
    return lax.fori_loop(LOOP_UNROLL * (n // LOOP_UNROLL), n, step, carry)


def _rowsum_heads(prod, first):
    return (jnp.sum(jnp.where(first, prod, 0.0), axis=1, keepdims=True),
            jnp.sum(jnp.where(first, 0.0, prod), axis=1, keepdims=True))


def _softplus_parts(z):
    e = jnp.exp(-jnp.abs(z))
    sp = jnp.maximum(z, 0.0) + jnp.log(1.0 + e)
    r = 1.0 / (1.0 + e)
    return sp, jnp.where(z >= 0, r, e * r)


def _sb_tile(s):
    return min(256, s)


def _attn_b_fwd(qkv, name, exchange=None):
    s = qkv.shape[0]
    t = _sb_tile(s)
    nq = s // t

    def body(q_ref, k_ref, v_ref, o_ref, lt_ref):
        first = _lane_is_first_head()
        before = _stacked_mask(t, strict=True)
        tri = (lax.broadcasted_iota(jnp.int32, (t, t), 0) >= lax.broadcasted_iota(jnp.int32, (t, t), 1)).astype(BF16)

        def tile(j, carry, diag, qs):
            c, acc = carry
            c0 = pl.multiple_of(j * t, t)
            k2 = k_ref[pl.ds(c0, t), :]
            v2 = v_ref[pl.ds(c0, t), :]
            z = lax.dot_general(qs, k2, _NT, preferred_element_type=F32)
            sp, _ = _softplus_parts(z)
            lf = jnp.where(before, -sp, 0.0) if diag else -sp
            incl = jnp.dot(lf.astype(BF16), tri, preferred_element_type=F32)
            a = jnp.exp(z + c + incl)
            if diag:
                a = jnp.where(before, a, 0.0)
            pv = jnp.dot(a.astype(BF16), v2, preferred_element_type=F32)
            return c + incl[:, 0:1], acc + jnp.where(first, pv[:t], pv[t:])

        def qblock(i, _):
            r0 = pl.multiple_of(i * t, t)
            qs = _stack_heads(q_ref[pl.ds(r0, t), :] * Q_SCALE, first)
            carry = tile(i, (jnp.zeros((2 * t, 1), F32), jnp.zeros((t, LANES), F32)), True, qs)
            carry = _two_at_a_time(i, lambda j, c: tile(i - 1 - j, c, False, qs), carry)
            o_ref[pl.ds(r0, t), :] = carry[1]
            lt_ref[pl.ds(r0, t), 0:LANES] = _bcast_lanes(carry[0][:t])
            lt_ref[pl.ds(r0, t), LANES:2 * LANES] = _bcast_lanes(carry[0][t:])
            return 0

        lax.fori_loop(0, nq, qblock, 0)

    return _grid_call(
        body, name=name, grid=(N_PAIRS,),
        in_specs=[_pair_spec(s), _pair_spec(s, N_PAIRS), _pair_spec(s, 2 * N_PAIRS)],
        out_specs=[_pair_spec(s), _stat_spec(s)],
        out_shape=[jax.ShapeDtypeStruct((s, BRANCH), F32), jax.ShapeDtypeStruct((s, N_HEADS * LANES), F32)],
        args=(qkv, qkv, qkv), semantics=("parallel",), exchange=exchange)


def _attn_b_bwd(qkv, ltot, do, name, exchange=None):
    s = qkv.shape[0]
    t = _sb_tile(s)
    nq = s // t

    def body(q_ref, k_ref, v_ref, lt_ref, do_ref, dq_ref, dk_ref, dv_ref, dk_acc, dv_acc):
        first = _lane_is_first_head()
        before = _stacked_mask(t, strict=True)
        tri = (lax.broadcasted_iota(jnp.int32, (t, t), 0) <= lax.broadcasted_iota(jnp.int32, (t, t), 1)).astype(BF16)
        dk_acc[...] = jnp.zeros_like(dk_acc)
        dv_acc[...] = jnp.zeros_like(dv_acc)

        def tile(j, carry, diag, qs, dos, lt):
            p_l, p_g, dq_acc = carry
            c0 = pl.multiple_of(j * t, t)
            k2 = k_ref[pl.ds(c0, t), :]
            v2 = v_ref[pl.ds(c0, t), :]
            z = lax.dot_general(qs, k2, _NT, preferred_element_type=F32)
            sp, sig = _softplus_parts(z)
            lf = jnp.where(before, -sp, 0.0) if diag else -sp
            pref_l = jnp.dot(lf.astype(BF16), tri, preferred_element_type=F32)
            a = jnp.exp(z + ((lt - p_l) - pref_l + lf))
            if diag:
                a = jnp.where(before, a, 0.0)
            g = a * lax.dot_general(dos, v2, _NT, preferred_element_type=F32)
            pref_g = jnp.dot(g.astype(BF16), tri, preferred_element_type=F32)
            dz = g - sig * (p_g + pref_g)
            if diag:
                dz = jnp.where(before, dz, 0.0)
            dzb = dz.astype(BF16)
            dq = jnp.dot(dzb, k2, preferred_element_type=F32)
            dk_acc[pl.ds(c0, t), :] += lax.dot_general(dzb, qs, _TN, preferred_element_type=F32)
            dv_acc[pl.ds(c0, t), :] += lax.dot_general(a.astype(BF16), dos, _TN, preferred_element_type=F32)
            return p_l + pref_l[:, t - 1:t], p_g + pref_g[:, t - 1:t], dq_acc + jnp.where(first, dq[:t], dq[t:])

        def qblock(i, _):
            r0 = pl.multiple_of(i * t, t)
            qs = _stack_heads(q_ref[pl.ds(r0, t), :] * Q_SCALE, first)
            dos = _stack_heads(do_ref[pl.ds(r0, t), :], first)
            lt = jnp.concatenate([lt_ref[pl.ds(r0, t), 0:1], lt_ref[pl.ds(r0, t), LANES:LANES + 1]], axis=0)
            zero = jnp.zeros((2 * t, 1), F32)
            carry = (zero, zero, jnp.zeros((t, LANES), F32))
            carry = _two_at_a_time(i, lambda j, c: tile(j, c, False, qs, dos, lt), carry)
            carry = tile(i, carry, True, qs, dos, lt)
            dq_ref[pl.ds(r0, t), :] = (carry[2] * Q_SCALE).astype(BF16)
            return 0

        lax.fori_loop(0, nq, qblock, 0)
        dk_ref[...] = dk_acc[...].astype(BF16)
        dv_ref[...] = dv_acc[...].astype(BF16)

    out = jax.ShapeDtypeStruct((s, BRANCH), BF16)
    return _grid_call(
        body, name=name, grid=(N_PAIRS,),
        in_specs=[_pair_spec(s), _pair_spec(s, N_PAIRS), _pair_spec(s, 2 * N_PAIRS), _stat_spec(s), _pair_spec(s)],
        out_specs=[_pair_spec(s)] * 3, out_shape=[out] * 3,
        scratch_shapes=[pltpu.VMEM((s, LANES), F32), pltpu.VMEM((s, LANES), F32)],
        args=(qkv, qkv, qkv, ltot, do), semantics=("parallel",), exchange=exchange)


def _fox_tile(s):
    return min(256, s)


def _stat_spec(s):
    return pl.BlockSpec((s, 2 * LANES), lambda p: (0, p))


def _cum_spec(nt, t):
    return pl.BlockSpec((1, nt, 2, t), lambda p: (p, 0, 0, 0))


def _attn_c_fwd(qkv, cum4, name, exchange=None):
    s = qkv.shape[0]
    t = _fox_tile(s)
    nq = s // t

    def body(q_ref, k_ref, v_ref, c_ref, o_ref, lse_ref):
        first = _lane_is_first_head()
        causal = _stacked_mask(t, strict=False)

        def tile(j, carry, diag, qs):
            c0 = pl.multiple_of(j * t, t)
            k2 = k_ref[pl.ds(c0, t), :]
            v2 = v_ref[pl.ds(c0, t), :]
            cs = c_ref[0, j]
            m_prev, l_prev, acc = carry
            z = lax.dot_general(qs, k2, _NT, preferred_element_type=F32)
            sc = jnp.concatenate([z[:t] - cs[0:1, :], z[t:] - cs[1:2, :]], axis=0)
            if diag:
                sc = jnp.where(causal, sc, NEG)
            m_new = jnp.maximum(m_prev, jnp.max(sc, axis=1, keepdims=True))
            alpha = jnp.exp(m_prev - m_new)
            p = jnp.exp(sc - m_new)
            l_new = alpha * l_prev + jnp.sum(p, axis=1, keepdims=True)
            pv = jnp.dot(p.astype(BF16), v2, preferred_element_type=F32)
            acc = jnp.where(first, acc * alpha[:t] + pv[:t], acc * alpha[t:] + pv[t:])
            return m_new, l_new, acc

        def qblock(i, _):
            r0 = pl.multiple_of(i * t, t)
            qs = _stack_heads(q_ref[pl.ds(r0, t), :] * Q_SCALE, first)
            carry = (jnp.full((2 * t, 1), NEG, F32), jnp.zeros((2 * t, 1), F32), jnp.zeros((t, LANES), F32))
            carry = _two_at_a_time(i, lambda j, c: tile(j, c, False, qs), carry)
            m, l, acc = tile(i, carry, True, qs)
            inv = 1.0 / l
            lse = m + jnp.log(l)
            o_ref[pl.ds(r0, t), :] = acc * jnp.where(first, inv[:t], inv[t:])
            lse_ref[pl.ds(r0, t), 0:LANES] = _bcast_lanes(lse[:t])
            lse_ref[pl.ds(r0, t), LANES:2 * LANES] = _bcast_lanes(lse[t:])
            return 0

        lax.fori_loop(0, nq, qblock, 0)

    return _grid_call(
        body, name=name, grid=(N_PAIRS,),
        in_specs=[_pair_spec(s), _pair_spec(s, N_PAIRS), _pair_spec(s, 2 * N_PAIRS), _cum_spec(nq, t)],
        out_specs=[_pair_spec(s), _stat_spec(s)],
        out_shape=[jax.ShapeDtypeStruct((s, BRANCH), F32), jax.ShapeDtypeStruct((s, N_HEADS * LANES), F32)],
        args=(qkv, qkv, qkv, cum4), semantics=("parallel",), exchange=exchange)


def _attn_c_bwd(qkv, cum4, o, lse, do, name, exchange=None):
    s = qkv.shape[0]
    t = _fox_tile(s)
    nq = s // t

    def body(q_ref, k_ref, v_ref, c_ref, o_ref, lse_ref, do_ref, dq_ref, dk_ref, dv_ref, dc_ref, dk_acc, dv_acc):
        first = _lane_is_first_head()
        causal = _stacked_mask(t, strict=False)
        eye = lax.broadcasted_iota(jnp.int32, (t, t), 0) == lax.broadcasted_iota(jnp.int32, (t, t), 1)
        dk_acc[...] = jnp.zeros_like(dk_acc)
        dv_acc[...] = jnp.zeros_like(dv_acc)
        dc_ref[...] = jnp.zeros_like(dc_ref)

        def tile(j, carry, diag, qs, dos, delta, lse):
            dq_acc, rs = carry
            c0 = pl.multiple_of(j * t, t)
            k2 = k_ref[pl.ds(c0, t), :]
            v2 = v_ref[pl.ds(c0, t), :]
            cs = c_ref[0, j]
            z = lax.dot_general(qs, k2, _NT, preferred_element_type=F32)
            sc = jnp.concatenate([z[:t] - cs[0:1, :], z[t:] - cs[1:2, :]], axis=0)
            p = jnp.exp(sc - lse)
            if diag:
                p = jnp.where(causal, p, 0.0)
            ds = p * (lax.dot_general(dos, v2, _NT, preferred_element_type=F32) - delta)
            dsb = ds.astype(BF16)
            dq = jnp.dot(dsb, k2, preferred_element_type=F32)
            dk_acc[pl.ds(c0, t), :] += lax.dot_general(dsb, qs, _TN, preferred_element_type=F32)
            dv_acc[pl.ds(c0, t), :] += lax.dot_general(p.astype(BF16), dos, _TN, preferred_element_type=F32)
            col_sums = jnp.concatenate([jnp.sum(ds[:t], axis=0, keepdims=True), jnp.sum(ds[t:], axis=0, keepdims=True)], axis=0)
            dc_ref[0, j] = dc_ref[0, j] - col_sums
            return dq_acc + jnp.where(first, dq[:t], dq[t:]), rs + jnp.sum(ds, axis=1, keepdims=True)

        def qblock(i, _):
            r0 = pl.multiple_of(i * t, t)
            do2 = do_ref[pl.ds(r0, t), :]
            qs = _stack_heads(q_ref[pl.ds(r0, t), :] * Q_SCALE, first)
            dos = _stack_heads(do2, first)
            delta = jnp.concatenate(_rowsum_heads(do2.astype(F32) * o_ref[pl.ds(r0, t), :], first), axis=0)
            lse = jnp.concatenate([lse_ref[pl.ds(r0, t), 0:1], lse_ref[pl.ds(r0, t), LANES:LANES + 1]], axis=0)
            carry = (jnp.zeros((t, LANES), F32), jnp.zeros((2 * t, 1), F32))
            carry = _two_at_a_time(i, lambda j, c: tile(j, c, False, qs, dos, delta, lse), carry)
            dq_acc, rs = tile(i, carry, True, qs, dos, delta, lse)
            dq_ref[pl.ds(r0, t), :] = (dq_acc * Q_SCALE).astype(BF16)
            as_row = lambda col_vec: jnp.sum(jnp.where(eye, col_vec, 0.0), axis=0, keepdims=True)
            dc_ref[0, i] = dc_ref[0, i] + jnp.concatenate([as_row(rs[:t]), as_row(rs[t:])], axis=0)
            return 0

        lax.fori_loop(0, nq, qblock, 0)
        dk_ref[...] = dk_acc[...].astype(BF16)
        dv_ref[...] = dv_acc[...].astype(BF16)

    out = jax.ShapeDtypeStruct((s, BRANCH), BF16)
    return _grid_call(
        body, name=name, grid=(N_PAIRS,),
        in_specs=[_pair_spec(s), _pair_spec(s, N_PAIRS), _pair_spec(s, 2 * N_PAIRS), _cum_spec(nq, t),
                  _pair_spec(s), _stat_spec(s), _pair_spec(s)],
        out_specs=[_pair_spec(s)] * 3 + [_cum_spec(nq, t)],
        out_shape=[out] * 3 + [jax.ShapeDtypeStruct(cum4.shape, F32)],
        scratch_shapes=[pltpu.VMEM((s, LANES), F32), pltpu.VMEM((s, LANES), F32)],
        args=(qkv, qkv, qkv, cum4, o, lse, do), semantics=("parallel",), exchange=exchange)


FG_CHUNK = 512


def _tri_dot3(x, t):
    hi = x.astype(BF16)
    r1 = x - hi.astype(F32)
    mid = r1.astype(BF16)
    lo = (r1 - mid.astype(F32)).astype(BF16)
    return (jnp.dot(hi, t, preferred_element_type=F32) + jnp.dot(mid, t, preferred_element_type=F32)
            + jnp.dot(lo, t, preferred_element_type=F32))


def _fgate_fwd(h, wf_t, b_col, name):
    s = h.shape[0]
    c = min(FG_CHUNK, s)

    def body(h_ref, w_ref, b_ref, xf_ref, cum_ref, carry_ref):
        @pl.when(pl.program_id(0) == 0)
        def _():
            carry_ref[...] = jnp.zeros_like(carry_ref)

        xf = lax.dot_general(w_ref[...], h_ref[...], _NT, preferred_element_type=F32) + b_ref[:, 0:1]
        xf_ref[...] = xf
        logf = jnp.minimum(xf, 0.0) - jnp.log(1.0 + jnp.exp(-jnp.abs(xf)))
        row = lax.broadcasted_iota(jnp.int32, (c, c), 0)
        col = lax.broadcasted_iota(jnp.int32, (c, c), 1)
        cum = _tri_dot3(logf, (row <= col).astype(BF16)) + carry_ref[:, 0:1]
        cum_ref[...] = cum
        carry_ref[...] = _bcast_lanes(cum[:, c - 1:c])

    out = jax.ShapeDtypeStruct((N_HEADS, s), F32)
    return pl.pallas_call(
        body, name=name, grid=(s // c,),
        in_specs=[pl.BlockSpec((c, D_MODEL), lambda i: (i, 0)),
                  pl.BlockSpec((N_HEADS, D_MODEL), lambda i: (0, 0)),
                  pl.BlockSpec((N_HEADS, LANES), lambda i: (0, 0))],
        out_specs=[pl.BlockSpec((N_HEADS, c), lambda i: (0, i))] * 2,
        out_shape=[out, out],
        scratch_shapes=[pltpu.VMEM((N_HEADS, LANES), F32)],
        compiler_params=_params(("arbitrary",)),
    )(h, wf_t, b_col)


def _fgate_bwd(dcum, xf, h, wf_t, name):
    s = h.shape[0]
    c = min(FG_CHUNK, s)
    n = s // c

    def body(dc_ref, xf_ref, h_ref, w_ref, dw_ref, dh_ref, db_ref, carry_ref):
        @pl.when(pl.program_id(0) == 0)
        def _():
            carry_ref[...] = jnp.zeros_like(carry_ref)
            dw_ref[...] = jnp.zeros_like(dw_ref)
            db_ref[...] = jnp.zeros_like(db_ref)

        row = lax.broadcasted_iota(jnp.int32, (c, c), 0)
        col = lax.broadcasted_iota(jnp.int32, (c, c), 1)
        dlogf = _tri_dot3(dc_ref[...], (row >= col).astype(BF16)) + carry_ref[:, 0:1]
        carry_ref[...] = _bcast_lanes(dlogf[:, 0:1])
        xf = xf_ref[...]
        e = jnp.exp(-jnp.abs(xf))
        r = 1.0 / (1.0 + e)
        dxf = dlogf * jnp.where(xf >= 0, e * r, r)
        db_ref[...] += _bcast_lanes(jnp.sum(dxf, axis=1, keepdims=True))
        dxb = dxf.astype(BF16)
        dw_ref[...] += jnp.dot(dxb, h_ref[...], preferred_element_type=F32)
        dh_ref[...] = lax.dot_general(dxb, w_ref[...], _TN, preferred_element_type=F32)

    rev = lambda i: n - 1 - i
    return pl.pallas_call(
        body, name=name, grid=(n,),
        in_specs=[pl.BlockSpec((N_HEADS, c), lambda i: (0, rev(i))),
                  pl.BlockSpec((N_HEADS, c), lambda i: (0, rev(i))),
                  pl.BlockSpec((c, D_MODEL), lambda i: (rev(i), 0)),
                  pl.BlockSpec((N_HEADS, D_MODEL), lambda i: (0, 0))],
        out_specs=[pl.BlockSpec((N_HEADS, D_MODEL), lambda i: (0, 0)),
                   pl.BlockSpec((c, D_MODEL), lambda i: (rev(i), 0)),
                   pl.BlockSpec((N_HEADS, LANES), lambda i: (0, 0))],
        out_shape=[jax.ShapeDtypeStruct((N_HEADS, D_MODEL), F32), jax.ShapeDtypeStruct((s, D_MODEL), F32),
                   jax.ShapeDtypeStruct((N_HEADS, LANES), F32)],
        scratch_shapes=[pltpu.VMEM((N_HEADS, LANES), F32)],
        compiler_params=_params(("arbitrary",)),
    )(dcum, xf, h, wf_t)


def _to_cum4(v, t):
    s = v.shape[1]
    return v.reshape(N_PAIRS, 2, s // t, t).transpose(0, 2, 1, 3)


def _from_cum4(v4):
    p, nt, two, t = v4.shape
    return v4.transpose(0, 2, 1, 3).reshape(p * two, nt * t)


def _alibi_slopes():
    return (2.0 ** (-8.0 * np.arange(1, N_HEADS + 1, dtype=np.float32) / N_HEADS)).astype(np.float32)


def _per_head_lanes(v):
    return jnp.repeat(v.astype(F32).reshape(N_PAIRS, 1, 2), LANES, axis=2)


def _attn_a_specs(s):
    q = _pair_spec(s)
    k = pl.BlockSpec((s, LANES), lambda p: (0, N_PAIRS + p // 8))
    v = pl.BlockSpec((s, LANES), lambda p: (0, N_PAIRS + KV_A // LANES + p // 8))
    head = pl.BlockSpec((1, 1, 2 * LANES), lambda p: (p, 0, 0))
    return q, k, v, head


def _attn_a_geometry(p, slope_ref, sink_ref):
    kv_half = (p // 4) % 2
    kv_first = kv_half == 0
    lane_first = _lane_is_first_head()
    kv_lanes = (lax.broadcasted_iota(jnp.int32, (1, LANES), 1) // HEAD_DIM) == kv_half
    row = lax.broadcasted_iota(jnp.int32, (2 * WINDOW, 2 * WINDOW), 0)
    cj = lax.broadcasted_iota(jnp.int32, (2 * WINDOW, 2 * WINDOW), 1)
    second = row >= WINDOW
    dist = WINDOW + jnp.where(second, row - WINDOW, row) - cj
    valid = (dist >= 0) & (dist < WINDOW)
    per_row = lambda ref: jnp.where(second[:, 0:1], ref[0, :, LANES:LANES + 1], ref[0, :, 0:1])
    return kv_first, lane_first, kv_lanes, per_row(slope_ref) * dist.astype(F32), valid, per_row(sink_ref)


def _swap_halves(x):
    return pltpu.roll(x, HEAD_DIM, 1)


def _attn_a_fwd(qkv, slopes, sinks, name, exchange=None):
    s = qkv.shape[0]
    nb = s // WINDOW

    def body(q_ref, k_ref, v_ref, sl_ref, sk_ref, o_ref, lse_ref):
        kv_first, lane_first, kv_lanes, bias, valid, sink = _attn_a_geometry(pl.program_id(0), sl_ref, sk_ref)

        def block(r0, k0, width):
            q2 = q_ref[pl.ds(r0, WINDOW), :].astype(F32) * Q_SCALE
            q2r = _swap_halves(q2)
            xs = jnp.concatenate([jnp.where(kv_first, q2, q2r), jnp.where(kv_first, q2r, q2)], axis=0).astype(BF16)
            km = jnp.where(kv_lanes, k_ref[pl.ds(k0, width), :], 0).astype(BF16)
            vm = jnp.where(kv_lanes, v_ref[pl.ds(k0, width), :], 0).astype(BF16)
            sc = lax.dot_general(xs, km, _NT, preferred_element_type=F32) - bias[:, 2 * WINDOW - width:]
            sc = jnp.where(valid[:, 2 * WINDOW - width:], sc, NEG)
            m = jnp.maximum(jnp.max(sc, axis=1, keepdims=True), sink)
            pr = jnp.exp(sc - m)
            l = jnp.sum(pr, axis=1, keepdims=True) + jnp.exp(sink - m)
            os = jnp.dot(pr.astype(BF16), vm, preferred_element_type=F32) * (1.0 / l)
            lse = m + jnp.log(l)
            lse_ref[pl.ds(r0, WINDOW), 0:LANES] = _bcast_lanes(lse[:WINDOW])
            lse_ref[pl.ds(r0, WINDOW), LANES:2 * LANES] = _bcast_lanes(lse[WINDOW:])
            oa = jnp.where(kv_first, os[:WINDOW], _swap_halves(os[:WINDOW]))
            ob = jnp.where(kv_first, _swap_halves(os[WINDOW:]), os[WINDOW:])
            o_ref[pl.ds(r0, WINDOW), :] = jnp.where(lane_first, oa, ob)

        block(0, 0, WINDOW)

        def loop(n, _):
            r0 = pl.multiple_of(n * WINDOW, WINDOW)
            block(r0, pl.multiple_of(r0 - WINDOW, WINDOW), 2 * WINDOW)
            return 0

        _two_at_a_time(nb - 1, lambda n, c: loop(n + 1, c), 0)

    q, k, v, head = _attn_a_specs(s)
    return _grid_call(
        body, name=name, grid=(N_PAIRS,),
        in_specs=[q, k, v, head, head],
        out_specs=[_pair_spec(s), _stat_spec(s)],
        out_shape=[jax.ShapeDtypeStruct((s, BRANCH), F32), jax.ShapeDtypeStruct((s, N_HEADS * LANES), F32)],
        args=(qkv, qkv, qkv, slopes, sinks), semantics=("parallel",), exchange=exchange)


def _attn_a_bwd(qkv, slopes, sinks, o, lse, do, name, exchange=None):
    s = qkv.shape[0]
    nb = s // WINDOW

    def body(q_ref, k_ref, v_ref, sl_ref, sk_ref, o_ref, lse_ref, do_ref, dq_ref, dk_ref, dv_ref, dsk_ref):
        p_id = pl.program_id(0)
        kv_first, lane_first, kv_lanes, bias, valid, sink = _attn_a_geometry(p_id, sl_ref, sk_ref)

        @pl.when(p_id % 8 == 0)
        def _():
            dk_ref[...] = jnp.zeros_like(dk_ref)
            dv_ref[...] = jnp.zeros_like(dv_ref)

        def align(v2):
            v2r = _swap_halves(v2)
            both = jnp.concatenate([jnp.where(kv_first, v2, v2r), jnp.where(kv_first, v2r, v2)], axis=0)
            return jnp.where(kv_lanes, both, 0.0).astype(BF16)

        def block(r0, k0, width, sink_sum):
            xq = align(q_ref[pl.ds(r0, WINDOW), :].astype(F32) * Q_SCALE)
            do2 = do_ref[pl.ds(r0, WINDOW), :].astype(F32)
            xdo = align(do2)
            delta = jnp.concatenate(_rowsum_heads(do2 * o_ref[pl.ds(r0, WINDOW), :], lane_first), axis=0)
            lse = jnp.concatenate([lse_ref[pl.ds(r0, WINDOW), 0:1], lse_ref[pl.ds(r0, WINDOW), LANES:LANES + 1]], axis=0)
            km = jnp.where(kv_lanes, k_ref[pl.ds(k0, width), :], 0).astype(BF16)
            vm = jnp.where(kv_lanes, v_ref[pl.ds(k0, width), :], 0).astype(BF16)
            sc = lax.dot_general(xq, km, _NT, preferred_element_type=F32) - bias[:, 2 * WINDOW - width:]
            pr = jnp.where(valid[:, 2 * WINDOW - width:], jnp.exp(sc - lse), 0.0)
            ds = pr * (lax.dot_general(xdo, vm, _NT, preferred_element_type=F32) - delta)
            dsb = ds.astype(BF16)
            dq_al = jnp.dot(dsb, km, preferred_element_type=F32)
            dk_ref[pl.ds(k0, width), :] += lax.dot_general(dsb, xq, _TN, preferred_element_type=F32)
            dv_ref[pl.ds(k0, width), :] += lax.dot_general(pr.astype(BF16), xdo, _TN, preferred_element_type=F32)
            dqa = jnp.where(kv_first, dq_al[:WINDOW], _swap_halves(dq_al[:WINDOW]))
            dqb = jnp.where(kv_first, _swap_halves(dq_al[WINDOW:]), dq_al[WINDOW:])
            dq_ref[pl.ds(r0, WINDOW), :] = (jnp.where(lane_first, dqa, dqb) * Q_SCALE).astype(BF16)
            return sink_sum + jnp.exp(sink - lse) * delta

        sink_sum = block(0, 0, WINDOW, jnp.zeros((2 * WINDOW, 1), F32))

        def loop(n, c):
            r0 = pl.multiple_of(n * WINDOW, WINDOW)
            return block(r0, pl.multiple_of(r0 - WINDOW, WINDOW), 2 * WINDOW, c)

        sink_sum = _two_at_a_time(nb - 1, lambda n, c: loop(n + 1, c), sink_sum)
        dsk_ref[0, :, 0:LANES] = jnp.broadcast_to(-jnp.sum(sink_sum[:WINDOW], axis=0, keepdims=True), (1, LANES))
        dsk_ref[0, :, LANES:2 * LANES] = jnp.broadcast_to(-jnp.sum(sink_sum[WINDOW:], axis=0, keepdims=True), (1, LANES))

    q, k, v, head = _attn_a_specs(s)
    kv_out = pl.BlockSpec((s, LANES), lambda p: (0, p // 8))
    return _grid_call(
        body, name=name, grid=(N_PAIRS,),
        in_specs=[q, k, v, head, head, _pair_spec(s), _stat_spec(s), _pair_spec(s)],
        out_specs=[_pair_spec(s), kv_out, kv_out, head],
        out_shape=[jax.ShapeDtypeStruct((s, BRANCH), BF16), jax.ShapeDtypeStruct((s, KV_A), F32),
                   jax.ShapeDtypeStruct((s, KV_A), F32), jax.ShapeDtypeStruct((N_PAIRS, 1, 2 * LANES), F32)],
        args=(qkv, qkv, qkv, slopes, sinks, o, lse, do), semantics=("arbitrary",), exchange=exchange)


def _layer_kind(i):
    return i % 3, i // 3


GATHER_FIRST = [("in", 0)]
GATHER_BEHIND = {("qkv", 0): [("out", 0)], ("attn", 0): [("in", 1)], ("attn", 1): [("out", 1), ("in", 2), ("out", 2)],
                 ("attn", 2): [("in", 3), ("out", 3)]}


def _forward_backward(x, target, g_pre, g_post, sinks_a, b_f_c, shards, chip, place):
    s = x.shape[0]
    slopes = _per_head_lanes(jnp.asarray(_alibi_slopes()))
    w_in, w_out, wf_t = {}, {}, {}

    def lands_side_by_side(key):
        return key[0] == "in" and shards[key].shape[1] % LANES == 0

    def gather(keys):
        return _GatherExchange([shards[k] for k in keys], [lands_side_by_side(k) for k in keys])

    def deliver(keys, gathered):
        for key, g in zip(keys, gathered):
            side, layer = key
            sh = shards[key]
            if side == "out":
                g = lax.dynamic_update_slice(g, sh[None], (chip, 0, 0))
                w_out[layer] = g.reshape(4 * sh.shape[0], sh.shape[1])
            elif lands_side_by_side(key):
                w_in[layer] = _place_columns(g, sh, chip, f"own_block_in_l{layer}")
            else:
                g = lax.dynamic_update_slice(g, sh[None], (chip, 0, 0))
                tail = 4 * BRANCH - 3 * sh.shape[1]
                w_in[layer] = jnp.concatenate([g[0], g[1], g[2], g[3][:, :tail]], axis=1)
                wf_t[layer] = g[3][:, tail:].T

    deliver(GATHER_FIRST, _exchange_call(gather(GATHER_FIRST), "gather_first_weights"))
    saved = []
    for i in range(DEPTH):
        kind, j = _layer_kind(i)
        tag = f"l{i}"
        w = w_in[i]
        nqkv = A_QKV if kind == 0 else B_QKV
        tn = 512 if kind == 0 else 1024
        h, h_t = _rmsnorm_fwd(x, g_pre[i:i + 1], f"prenorm_{tag}")
        behind = GATHER_BEHIND.get(("qkv", i))
        qkv = _matmul(h, w, out_dtype=BF16, name=f"inproj_qkv_{tag}", n=nqkv, tn=tn,
                      exchange=gather(behind) if behind else None)
        if behind:
            qkv, arrived = qkv
            deliver(behind, arrived)
        z = _matmul(h, w, out_dtype=F32, name=f"inproj_gate_{tag}", n=BRANCH, b_off=nqkv // tn, tn=tn)
        behind = GATHER_BEHIND.get(("attn", i))
        exchange = gather(behind) if behind else None
        if kind == 0:
            sink_l = _per_head_lanes(sinks_a[j])
            (o, lse), arrived = _attn_a_fwd(qkv, slopes, sink_l, f"attn_a_fwd_{tag}", exchange)
            extra = (sink_l, lse)
        elif kind == 1:
            (o, extra), arrived = _attn_b_fwd(qkv, f"attn_b_fwd_{tag}", exchange)
        else:
            b_col = jnp.broadcast_to(b_f_c[j].astype(F32)[:, None], (N_HEADS, LANES))
            xf, cum = _fgate_fwd(h, wf_t[i], b_col, f"fgate_fwd_{tag}")
            cum4 = _to_cum4(cum, _fox_tile(s))
            (o, lse), arrived = _attn_c_fwd(qkv, cum4, f"attn_c_fwd_{tag}", exchange)
            extra = (xf, cum4, lse)
        if behind:
            deliver(behind, arrived)
        x_next, y, u_t = _gated_out_proj(o, z, w_out[i], x, g_post[i:i + 1], f"outproj_{tag}")
        saved.append((x, h, h_t, qkv, z, o, u_t, y, extra))
        x = x_next

    dx, loss_part = _loss_and_grad(x, target)

    d_g_pre, d_g_post = [None] * DEPTH, [None] * DEPTH
    d_sinks = [None, None]
    d_b_f = None
    reduced = {}
    pending = None

    def finish_reduce(layer, side, own, arr):
        kind, j = _layer_kind(layer)
        reduced[(side, kind)] = _sum_chips(own, arr, place, f"shard_sum_{side}_l{layer}", j, 2 if kind == 0 else 1,
                                           into=reduced.get((side, kind)))

    for i in reversed(range(DEPTH)):
        kind, j = _layer_kind(i)
        tag = f"l{i}"
        x_in, h, h_t, qkv, z, o, u_t, y, extra = saved[i]
        tn = 512 if kind == 0 else 1024
        (dy, d_g_post[i], do, dz), _ = _gated_out_proj_bwd(dx, y, g_post[i:i + 1], w_out[i], o, z, f"outproj_bwd_{tag}")
        dw_out = _matmul(u_t, dy, out_dtype=BF16, name=f"dw_out_{tag}")
        dw_out = dw_out.reshape(4, dw_out.shape[0] // 4, dw_out.shape[1])
        dh_f = None
        exchange = _SiblingExchange([dw_out])
        if pending:
            exchange = _BothExchanges(exchange, _ScatterExchange([pending[1]]))
        if kind == 0:
            sink_l, lse = extra
            (dq, dk, dv, dsk), arrived = _attn_a_bwd(qkv, slopes, sink_l, o, lse, do, f"attn_a_bwd_{tag}", exchange)
            d_sinks[j] = dsk[:, 0, ::LANES].reshape(N_HEADS)
            parts = [dq, dk.astype(BF16), dv.astype(BF16), dz]
        elif kind == 1:
            (dq, dk, dv), arrived = _attn_b_bwd(qkv, extra, do, f"attn_b_bwd_{tag}", exchange)
            parts = [dq, dk, dv, dz]
        else:
            xf, cum4, lse = extra
            (dq, dk, dv, dcum4), arrived = _attn_c_bwd(qkv, cum4, o, lse, do, f"attn_c_bwd_{tag}", exchange)
            d_wf_t, dh_f, db = _fgate_bwd(_from_cum4(dcum4), xf, h, wf_t[i], f"fgate_bwd_{tag}")
            d_b_f = db[:, 0]
            parts = [dq, dk, dv, dz]
        sum_out = _add_pairs(dw_out, arrived[0], place, f"chip_sum_out_{tag}")
        if pending:
            finish_reduce(pending[0], "in", pending[1], arrived[1])
        dproj = jnp.concatenate(parts, axis=1)
        scatter_out = _ScatterExchange([sum_out])
        if kind == 2:
            dw_main, arrived = _matmul(h_t, dproj, out_dtype=BF16, name=f"dw_in_{tag}", tn=tn, exchange=scatter_out)
            cc = shards[("in", i)].shape[1]
            dw_in = jnp.stack([dw_main[:, :cc], dw_main[:, cc:2 * cc], dw_main[:, 2 * cc:3 * cc],
                               jnp.concatenate([dw_main[:, 3 * cc:], d_wf_t.T.astype(BF16)], axis=1)])
        else:
            dw_in, arrived = _matmul(h_t, dproj, out_dtype=BF16, name=f"dw_in_{tag}", col_blocks=4,
                                     tn=1152 if kind == 0 else 1024, exchange=scatter_out)
        finish_reduce(i, "out", sum_out, arrived[0])
        (dx, d_g_pre[i]), (their_in,) = _in_proj_bwd(
            dproj, w_in[i], dh_f, dx, x_in, g_pre[i:i + 1], f"inproj_bwd_{tag}", 1536 if kind == 0 else 1024,
            exchange=_SiblingExchange([dw_in]))
        pending = (i, _add_pairs(dw_in, their_in, place, f"chip_sum_in_{tag}"))

    return dict(loss=loss_part, dx=dx, g_pre=jnp.concatenate(d_g_pre, axis=0), g_post=jnp.concatenate(d_g_post, axis=0),
                sinks_a=jnp.stack(d_sinks), b_f_c=d_b_f[None, :], reduced=reduced, last_sum=pending[1])


def _place():
    x, y, c = lax.axis_index("x"), lax.axis_index("y"), lax.axis_index("c")
    others = [(1 - x, y), (x, 1 - y), (1 - x, 1 - y)]
    return x, y, c, others


def _half_rows(ref_rows, which):
    half = ref_rows // 2
    return pl.ds(pl.multiple_of(which * half, half), half)


def _remote(src, dst, sems, k, device):
    send, recv = sems
    return pltpu.make_async_remote_copy(src_ref=src, dst_ref=dst, send_sem=send.at[k], recv_sem=recv.at[k],
                                        device_id=device, device_id_type=MESH)


def _hbm_call(body, name, ins, out_shapes, n_remote, aliases=None):
    any_spec = pl.BlockSpec(memory_space=pl.ANY)
    return pl.pallas_call(
        body, name=name, in_specs=[any_spec] * len(ins), out_specs=[any_spec] * len(out_shapes),
        out_shape=out_shapes, input_output_aliases=aliases or {},
        scratch_shapes=[pltpu.SemaphoreType.DMA((n_remote,)), pltpu.SemaphoreType.DMA((n_remote,))],
    )(*ins)


class _GatherExchange:
    SEMS = 8

    def __init__(self, shards, side_by_side):
        self.ins = list(shards)
        self.side_by_side = list(side_by_side)
        self.out_shapes = [jax.ShapeDtypeStruct((a.shape[0], 4 * a.shape[1]) if wide else (4,) + a.shape, a.dtype)
                           for a, wide in zip(shards, side_by_side)]
        self.n_sems = self.SEMS * len(shards)
        self.aliases = {}

    def _copies(self, ins, outs, sems):
        x, y, c, _ = _place()
        me, diag = 2 * x + y, 2 * (1 - x) + (1 - y)
        nbr = [((1 - x, y, c), 2 * (1 - x) + y), ((x, 1 - y, c), 2 * x + (1 - y))]
        sibling = (x, y, 1 - c)
        table = []
        for w, (src, dst, wide) in enumerate(zip(ins, outs, self.side_by_side)):
            rows, cols = src.shape
            half, quarter = rows // 2, rows // 4

            def slot(chip, core, piece=None, dst=dst, wide=wide, cols=cols, half=half, quarter=quarter):
                start, size = (core * half, half) if piece is None else (core * half + piece * quarter, quarter)
                which = pl.ds(pl.multiple_of(start, quarter), size)
                return dst.at[which, pl.ds(pl.multiple_of(chip * cols, LANES), cols)] if wide else dst.at[chip, which]

            k0 = self.SEMS * w
            cp = lambda s_, d_, k, dev: _remote(s_, d_, sems, k0 + k, dev)
            mine_src = src.at[pl.ds(pl.multiple_of(c * half, half), half)]
            d = dict(
                send=[cp(mine_src, slot(me, c), k, nbr[k][0]) for k in range(2)],
                got=[cp(slot(nbr[k][1], c), slot(nbr[k][1], c), k, nbr[k][0]) for k in range(2)],
                fwd=[cp(slot(nbr[k][1], c, k), slot(nbr[k][1], c, k), 2 + k, nbr[1 - k][0]) for k in range(2)],
                got_fwd=[cp(slot(diag, c, k), slot(diag, c, k), 2 + k, nbr[1 - k][0]) for k in range(2)],
                pass_=[cp(slot(nbr[k][1], c), slot(nbr[k][1], c), 4 + k, sibling) for k in range(2)]
                + [cp(slot(diag, c, k), slot(diag, c, k), 6 + k, sibling) for k in range(2)],
                got_pass=[cp(slot(nbr[k][1], 1 - c), slot(nbr[k][1], 1 - c), 4 + k, sibling) for k in range(2)]
                + [cp(slot(diag, 1 - c, k), slot(diag, 1 - c, k), 6 + k, sibling) for k in range(2)])
            table.append(d)
        return table

    def start(self, ins, outs, sems):
        for d in self._copies(ins, outs, sems):
            for cp in d["send"]:
                cp.start()

    def mid(self, ins, outs, sems):
        for d in self._copies(ins, outs, sems):
            for k in range(2):
                d["got"][k].wait_recv()
                d["fwd"][k].start()
                d["pass_"][k].start()

    def finish(self, ins, outs, sems):
        table = self._copies(ins, outs, sems)
        for d in table:
            for k in range(2):
                d["got_fwd"][k].wait_recv()
                d["pass_"][2 + k].start()
        for d in table:
            for cp in d["got_pass"]:
                cp.wait_recv()
            for cp in d["send"] + d["fwd"] + d["pass_"]:
                cp.wait_send()


class _SemaphoresFrom:
    def __init__(self, ref, start):
        self._ref, self._start = ref, start

    @property
    def at(self):
        return self

    def __getitem__(self, k):
        return self._ref.at[self._start + k]


class _BothExchanges:
    def __init__(self, first, second):
        self.parts = (first, second)
        self.ins = first.ins + second.ins
        self.out_shapes = first.out_shapes + second.out_shapes
        self.n_sems = first.n_sems + second.n_sems
        self.aliases = {}

    def _each(self, phase, ins, outs, sems):
        i0 = o0 = s0 = 0
        for ex in self.parts:
            n_in, n_out = len(ex.ins), len(ex.out_shapes)
            getattr(ex, phase)(ins[i0:i0 + n_in], outs[o0:o0 + n_out], tuple(_SemaphoresFrom(r, s0) for r in sems))
            i0, o0, s0 = i0 + n_in, o0 + n_out, s0 + ex.n_sems

    def start(self, ins, outs, sems):
        self._each("start", ins, outs, sems)

    def mid(self, ins, outs, sems):
        self._each("mid", ins, outs, sems)

    def finish(self, ins, outs, sems):
        self._each("finish", ins, outs, sems)


def _place_columns(wide, block, chip, name):
    rows, cc = block.shape
    tr = min(512, rows)

    def body(c_ref, b_ref, w_ref, o_ref):
        o_ref[...] = b_ref[...]

    return pl.pallas_call(
        body, name=name,
        grid_spec=pltpu.PrefetchScalarGridSpec(
            num_scalar_prefetch=1, grid=(rows // tr,),
            in_specs=[pl.BlockSpec((tr, cc), lambda r, c_ref: (r, 0)), pl.BlockSpec(memory_space=pl.ANY)],
            out_specs=pl.BlockSpec((tr, cc), lambda r, c_ref: (r, c_ref[0]))),
        out_shape=jax.ShapeDtypeStruct(wide.shape, wide.dtype), input_output_aliases={2: 0},
        compiler_params=_params(("parallel",)),
    )(chip.astype(jnp.int32).reshape(1), block, wide)


def _exchange_call(ex, name):
    n_in, n_out = len(ex.ins), len(ex.out_shapes)

    def body(*refs):
        ins, outs, sems = refs[:n_in], refs[n_in:n_in + n_out], refs[n_in + n_out:]
        ex.start(ins, outs, sems)
        ex.mid(ins, outs, sems)
        ex.finish(ins, outs, sems)

    return _hbm_call(body, name, ex.ins, ex.out_shapes, ex.n_sems, aliases=ex.aliases)


def _grid_call(body, *, name, grid, in_specs, out_specs, out_shape, args, scratch_shapes=(), semantics, exchange=None):
    if exchange is None:
        res = pl.pallas_call(body, name=name, grid=grid, in_specs=list(in_specs), out_specs=list(out_specs),
                             out_shape=list(out_shape), scratch_shapes=list(scratch_shapes),
                             compiler_params=_params(semantics))(*args)
        return res, []
    n_in, n_out, n_scr = len(args), len(out_shape), len(scratch_shapes)
    x_in, x_out = len(exchange.ins), len(exchange.out_shapes)
    steps = math.prod(grid)

    def wrapped(*refs):
        core_in, ex_in = refs[:n_in], refs[n_in:n_in + x_in]
        rest = refs[n_in + x_in:]
        core_out, ex_out = rest[:n_out], rest[n_out:n_out + x_out]
        scratch, sems = rest[n_out + x_out:n_out + x_out + n_scr], rest[n_out + x_out + n_scr:]
        step = 0
        for axis, extent in enumerate(grid):
            step = step * extent + pl.program_id(axis)

        @pl.when(step == 0)
        def _():
            exchange.start(ex_in, ex_out, sems)

        body(*core_in, *core_out, *scratch)

        @pl.when(step == max((3 * steps) // 4 - 1, 0))
        def _():
            exchange.mid(ex_in, ex_out, sems)

        @pl.when(step == steps - 1)
        def _():
            exchange.finish(ex_in, ex_out, sems)

    any_spec = pl.BlockSpec(memory_space=pl.ANY)
    res = pl.pallas_call(
        wrapped, name=name, grid=grid,
        in_specs=list(in_specs) + [any_spec] * x_in, out_specs=list(out_specs) + [any_spec] * x_out,
        out_shape=list(out_shape) + list(exchange.out_shapes),
        input_output_aliases={n_in + a: n_out + b for a, b in exchange.aliases.items()},
        scratch_shapes=list(scratch_shapes) + [pltpu.SemaphoreType.DMA((exchange.n_sems,)),
                                               pltpu.SemaphoreType.DMA((exchange.n_sems,))],
        compiler_params=_params(("arbitrary",) * len(grid)),
    )(*args, *exchange.ins)
    return res[:n_out], res[n_out:]


class _SiblingExchange:
    def __init__(self, parts):
        self.ins = list(parts)
        self.out_shapes = [jax.ShapeDtypeStruct((4, a.shape[1] // 2, a.shape[2]), a.dtype) for a in parts]
        self.n_sems = len(parts)
        self.aliases = {}

    def _copies(self, ins, outs, sems):
        x, y, c, _ = _place()
        return [_remote(src.at[:, _half_rows(src.shape[1], 1 - c)], dst, sems, w, (x, y, 1 - c))
                for w, (src, dst) in enumerate(zip(ins, outs))]

    def start(self, ins, outs, sems):
        for cp in self._copies(ins, outs, sems):
            cp.start()

    def mid(self, ins, outs, sems):
        pass

    def finish(self, ins, outs, sems):
        for cp in self._copies(ins, outs, sems):
            cp.wait_recv()
            cp.wait_send()


class _ScatterExchange:
    def __init__(self, sums):
        self.ins = list(sums)
        self.out_shapes = [jax.ShapeDtypeStruct(a.shape, a.dtype) for a in sums]
        self.n_sems = 3 * len(sums)
        self.aliases = {}

    def _copies(self, ins, outs, sems):
        x, y, c, others = _place()
        me = 2 * x + y
        table = []
        for w, (src, dst) in enumerate(zip(ins, outs)):
            for j, (px, py) in enumerate(others):
                there = 2 * px + py
                send = _remote(src.at[there], dst.at[me], sems, 3 * w + j, (px, py, c))
                landed = _remote(dst.at[there], dst.at[there], sems, 3 * w + j, (px, py, c))
                table.append((send, landed))
        return table

    def start(self, ins, outs, sems):
        for send, _ in self._copies(ins, outs, sems):
            send.start()

    def mid(self, ins, outs, sems):
        pass

    def finish(self, ins, outs, sems):
        table = self._copies(ins, outs, sems)
        for _, landed in table:
            landed.wait_recv()
        for send, _ in table:
            send.wait_send()


def _sibling_join(shards, name):
    n = len(shards)

    def body(*refs):
        ins, outs, sems = refs[:n], refs[n:2 * n], refs[2 * n:2 * n + 2]
        x, y, c, _ = _place()
        pend = []
        for w in range(n):
            rows = ins[w].shape[1]
            mine, theirs = _half_rows(rows, c), _half_rows(rows, 1 - c)
            cp = _remote(ins[w].at[:, mine], outs[w].at[:, mine], sems, w, (x, y, 1 - c))
            cp.start()
            pend.append((cp, _remote(ins[w].at[:, theirs], outs[w].at[:, theirs], sems, w, (x, y, 1 - c))))
        for cp, landed in pend:
            landed.wait_recv()
            cp.wait_send()

    out_shapes = [jax.ShapeDtypeStruct(a.shape, a.dtype) for a in shards]
    return _hbm_call(body, name, shards, out_shapes, n, aliases={w: w for w in range(n)})


SMALL_ROWS = 136


def _all_reduce_small(vec):
    def body(v_ref, o_ref, buf, send, recv, loc):
        x, y, c, _ = _place()
        me = 4 * x + 2 * y + c
        lc = pltpu.make_async_copy(v_ref, buf.at[me], loc.at[0])
        lc.start()
        cps = []
        for k in range(1, 8):
            fx, fy, fc = (k >> 2) & 1, (k >> 1) & 1, k & 1
            peer = (x ^ fx, y ^ fy, c ^ fc)
            cp = pltpu.make_async_remote_copy(src_ref=v_ref, dst_ref=buf.at[me], send_sem=send.at[k - 1],
                                              recv_sem=recv.at[k - 1], device_id=peer, device_id_type=MESH)
            cp.start()
            cps.append((cp, 4 * peer[0] + 2 * peer[1] + peer[2]))
        for k, (cp, src) in enumerate(cps):
            pltpu.make_async_remote_copy(src_ref=v_ref, dst_ref=buf.at[src], send_sem=send.at[k], recv_sem=recv.at[k],
                                         device_id=(x, y, c), device_id_type=MESH).wait_recv()
        for cp, _ in cps:
            cp.wait_send()
        lc.wait()
        total = buf[0]
        for k in range(1, 8):
            total = total + buf[k]
        o_ref[...] = total

    vm = pl.BlockSpec(memory_space=pltpu.VMEM)
    return pl.pallas_call(
        body, name="all_reduce_small", in_specs=[vm], out_specs=vm,
        out_shape=jax.ShapeDtypeStruct(vec.shape, F32),
        scratch_shapes=[pltpu.VMEM((8,) + vec.shape, F32), pltpu.SemaphoreType.DMA((7,)),
                        pltpu.SemaphoreType.DMA((7,)), pltpu.SemaphoreType.DMA((1,))],
    )(vec)


SUM_ROWS = 256


def _add_pairs(part, theirs, place, name):
    four, rh, cc = theirs.shape
    tr = min(SUM_ROWS, rh)
    halves = part.reshape(four, 2, rh, cc)

    def body(p_ref, a_ref, b_ref, o_ref):
        o_ref[0] = (a_ref[0, 0].astype(F32) + b_ref[0].astype(F32)).astype(o_ref.dtype)

    spec = pl.BlockSpec((1, tr, cc), lambda k, r, p_ref: (k, r, 0))
    return pl.pallas_call(
        body, name=name,
        grid_spec=pltpu.PrefetchScalarGridSpec(
            num_scalar_prefetch=1, grid=(four, rh // tr),
            in_specs=[pl.BlockSpec((1, 1, tr, cc), lambda k, r, p_ref: (k, p_ref[1], r, 0)), spec], out_specs=spec),
        out_shape=jax.ShapeDtypeStruct(theirs.shape, theirs.dtype),
        compiler_params=_params(("parallel", "parallel")),
    )(place, halves, theirs)


def _sum_chips(own, arrived, place, name, layer, n_layers, into=None):
    four, rh, cc = own.shape
    tr = min(SUM_ROWS, rh)
    nr = rh // tr

    def body(p_ref, own_ref, arr_ref, *rest):
        o_ref = rest[-1]
        x, y = lax.axis_index("x"), lax.axis_index("y")
        tot = own_ref[0].astype(F32)
        for px, py in ((1 - x, y), (x, 1 - y), (1 - x, 1 - y)):
            tot = tot + arr_ref[2 * px + py].astype(F32)
        o_ref[0] = tot

    in_specs = [pl.BlockSpec((1, tr, cc), lambda r, p_ref: (p_ref[0], r, 0)),
                pl.BlockSpec((4, tr, cc), lambda r, p_ref: (0, r, 0))]
    args, aliases = [place, own, arrived], {}
    if into is not None:
        in_specs.append(pl.BlockSpec(memory_space=pl.ANY))
        args.append(into)
        aliases = {3: 0}
    return pl.pallas_call(
        body, name=name,
        grid_spec=pltpu.PrefetchScalarGridSpec(
            num_scalar_prefetch=1, grid=(nr,), in_specs=in_specs,
            out_specs=pl.BlockSpec((1, tr, cc), lambda r, p_ref: (layer, p_ref[1] * nr + r, 0))),
        out_shape=jax.ShapeDtypeStruct((n_layers, 2 * rh, cc), F32), input_output_aliases=aliases,
        compiler_params=_params(("parallel",)),
    )(*args)


ADAM_ROWS = 256


def _adamw(w, g, m, v, name):
    shape = w.shape
    as3 = lambda a: a.reshape((-1,) + shape[-2:])
    layers, rows, cc = as3(w).shape
    by_rows = rows % min(ADAM_ROWS, rows) == 0
    tr, tc = (min(ADAM_ROWS, rows), cc) if by_rows else (rows, ADAM_ROWS)
    assert rows % tr == 0 and cc % tc == 0

    def body(w_ref, g_ref, m_ref, v_ref, d_ref, nm_ref, nv_ref):
        _adamw_update(w_ref, g_ref, m_ref, v_ref, d_ref, nm_ref, nv_ref)

    spec = pl.BlockSpec((1, tr, tc), (lambda l, i: (l, i, 0)) if by_rows else (lambda l, i: (l, 0, i)))
    sh = jax.ShapeDtypeStruct((layers, rows, cc), F32)
    outs = pl.pallas_call(
        body, name=name, grid=(layers, (rows // tr) * (cc // tc)), in_specs=[spec] * 4, out_specs=[spec] * 3,
        out_shape=[sh] * 3,
        compiler_params=_params(("parallel", "parallel")),
    )(as3(w), as3(g), as3(m), as3(v))
    return [o.reshape(shape) for o in outs]


def _adamw_update(w_ref, g_ref, m_ref, v_ref, d_ref, nm_ref, nv_ref):
    c1 = 1.0 - ADAM_B1 ** ADAM_STEP
    c2 = 1.0 - ADAM_B2 ** ADAM_STEP
    gv = g_ref[...]
    nm = ADAM_B1 * m_ref[...] + (1.0 - ADAM_B1) * gv
    nv = ADAM_B2 * v_ref[...] + (1.0 - ADAM_B2) * (gv * gv)
    nm_ref[...] = nm
    nv_ref[...] = nv
    d_ref[...] = -ADAM_LR * ((nm / c1) / (jnp.sqrt(nv / c2) + ADAM_EPS) + ADAM_WD * w_ref[...])


ADAM_MANY_STEPS = 16


def _adamw_many(quads, name, exchange=None):
    n = ADAM_MANY_STEPS
    specs = []
    for w, _, _, _ in quads:
        layers, rows, cc = w.shape
        if rows % (8 * n) == 0:
            specs.append(pl.BlockSpec((layers, rows // n, cc), lambda i: (0, i, 0)))
        else:
            assert cc % (LANES * n) == 0, (name, w.shape)
            specs.append(pl.BlockSpec((layers, rows, cc // n), lambda i: (0, 0, i)))

    def body(*refs):
        ins, outs = refs[:4 * len(quads)], refs[4 * len(quads):]
        for q in range(len(quads)):
            _adamw_update(*ins[4 * q:4 * q + 4], *outs[3 * q:3 * q + 3])

    res, arrived = _grid_call(
        body, name=name, grid=(n,), in_specs=[s for s in specs for _ in range(4)],
        out_specs=[s for s in specs for _ in range(3)],
        out_shape=[jax.ShapeDtypeStruct(w.shape, F32) for w, _, _, _ in quads for _ in range(3)],
        args=tuple(a for quad in quads for a in quad), semantics=("parallel",), exchange=exchange)
    return [list(res[3 * q:3 * q + 3]) for q in range(len(quads))], arrived


def _pack_small(g_pre, g_post, sinks_a, b_f_c, loss_row):
    pad = lambda a: jnp.pad(a.reshape(1, -1).astype(F32), ((0, 0), (0, LANES - a.size)))
    rows = [g_pre.astype(F32).reshape(-1, LANES), g_post.astype(F32).reshape(-1, LANES), pad(sinks_a), pad(b_f_c), loss_row]
    packed = jnp.concatenate(rows, axis=0)
    return jnp.pad(packed, ((0, SMALL_ROWS - packed.shape[0]), (0, 0)))


def _unpack_small(p):
    n = DEPTH * D_MODEL // LANES
    return (p[:n].reshape(DEPTH, D_MODEL), p[n:2 * n].reshape(DEPTH, D_MODEL), p[2 * n, :2 * N_HEADS].reshape(2, N_HEADS),
            p[2 * n + 1, :N_HEADS].reshape(1, N_HEADS), p[2 * n + 2, 0])


def kernel(x, g_pre, g_post, w_in_a, w_out_a, sinks_a, w_in_b, w_out_b, w_in_c, b_f_c, w_out_c, loss_target, m_g_pre, m_g_post, m_w_in_a, m_w_out_a, m_sinks_a, m_w_in_b, m_w_out_b, m_w_in_c, m_b_f_c, m_w_out_c, v_g_pre, v_g_post, v_w_in_a, v_w_out_a, v_sinks_a, v_w_in_b, v_w_out_b, v_w_in_c, v_b_f_c, v_w_out_c):
    big_w = [w_in_a, w_out_a, w_in_b, w_out_b, w_in_c, w_out_c]
    big_m = [m_w_in_a, m_w_out_a, m_w_in_b, m_w_out_b, m_w_in_c, m_w_out_c]
    big_v = [v_w_in_a, v_w_out_a, v_w_in_b, v_w_out_b, v_w_in_c, v_w_out_c]

    chip = 2 * lax.axis_index("x") + lax.axis_index("y")
    place = jnp.stack([chip, lax.axis_index("c")]).astype(jnp.int32)
    by_kind = {0: (w_in_a, w_out_a), 1: (w_in_b, w_out_b), 2: (w_in_c, w_out_c)}
    shards = {}
    for i in range(DEPTH):
        kind, j = _layer_kind(i)
        shards[("in", i)] = by_kind[kind][0][j].astype(BF16)
        shards[("out", i)] = by_kind[kind][1][j].astype(BF16)

    res = _forward_backward(x[0], loss_target[0], g_pre, g_post, sinks_a, b_f_c, shards, chip, place)
    reduced = res["reduced"]
    rest = [("out", 0), ("in", 1), ("out", 1), ("in", 2), ("out", 2)]
    grads = [None] + list(_sibling_join([reduced[k] for k in rest], "grad_sibling_join"))

    small = _unpack_small(_all_reduce_small(
        _pack_small(res["g_pre"], res["g_post"], res["sinks_a"], res["b_f_c"], res["loss"])))
    g_small, loss = small[:4], small[4]

    zero_row = jnp.zeros((1, LANES), F32)
    pk = lambda a: _pack_small(a[0], a[1], a[2], a[3], zero_row)
    sm = _adamw(pk([g_pre, g_post, sinks_a, b_f_c]), pk(g_small), pk([m_g_pre, m_g_post, m_sinks_a, m_b_f_c]),
                pk([v_g_pre, v_g_post, v_sinks_a, v_b_f_c]), "adamw_small")
    sm = [_unpack_small(a)[:4] for a in sm]
    turned = lambda a: jnp.swapaxes(a, 1, 2)
    g_c = lax.optimization_barrier(turned(grads[4]))
    grads[4] = turned(g_c)
    quads = [(turned(w), g_c, turned(m), turned(v)) if k == 4 else (w, grads[k], m, v)
             for k, (w, m, v) in enumerate(zip(big_w, big_m, big_v)) if k > 0]
    bigs, arrived = _adamw_many(quads, "adamw_rest", exchange=_ScatterExchange([res["last_sum"]]))
    bigs[3] = [turned(o) for o in bigs[3]]
    half = _sum_chips(res["last_sum"], arrived[0], place, "shard_sum_in_l0", 0, 2, into=reduced[("in", 0)])
    grads[0] = _sibling_join([half], "grad_sibling_join_w_in_a")[0]
    bigs = [_adamw(w_in_a, grads[0], m_w_in_a, v_w_in_a, "adamw_w_in_a")] + bigs

    def ordered(small4, big6):
        return [small4[0], small4[1], big6[0], big6[1], small4[2], big6[2], big6[3], big6[4], small4[3], big6[5]]

    out = [loss, res["dx"][None], *ordered(g_small, grads)]
    for k in range(3):
        out += ordered(sm[k], [b[k] for b in bigs])
    return tuple(out)
```

```python
import functools
import math

import numpy as np
import jax
import jax.numpy as jnp
from jax import lax
from jax.experimental import pallas as pl
from jax.experimental.pallas import tpu as pltpu

F32 = jnp.float32
BF16 = jnp.bfloat16

D_MODEL = 2048
DEPTH = 4
N_HEADS = 32
HEAD_DIM = 64
LANES = 128
N_PAIRS = N_HEADS * HEAD_DIM // LANES
BRANCH = N_HEADS * HEAD_DIM
N_KV_A = 4
KV_A = N_KV_A * HEAD_DIM
WINDOW = 128
NORM_EPS = 1e-6
NEG = -1e30
Q_SCALE = HEAD_DIM ** -0.5

A_QKV = BRANCH + 2 * KV_A
B_QKV = 3 * BRANCH

ADAM_LR = 0.001
ADAM_B1 = 0.9
ADAM_B2 = 0.999
ADAM_EPS = 1e-08
ADAM_WD = 0.01
ADAM_STEP = 10

MESH = pl.DeviceIdType.MESH

_NT = (((1,), (1,)), ((), ()))
_TN = (((0,), (0,)), ((), ()))


def _params(sem=None):
    return pltpu.CompilerParams(dimension_semantics=sem)


def _matmul(a, b, *, out_dtype, name, n=None, b_off=0, tm=1024, tn=1024, col_blocks=None, exchange=None):
    (m, k), nn = a.shape, (n or b.shape[1])
    tm, tn = min(tm, m), min(tn, nn)
    assert m % tm == 0 and nn % tn == 0, (name, m, nn, tm, tn)

    def body(a_ref, b_ref, o_ref):
        p = jnp.dot(a_ref[...], b_ref[...], preferred_element_type=F32)
        o_ref[...] = p.astype(o_ref.dtype).reshape(o_ref.shape)

    in_specs = [pl.BlockSpec((tm, k), lambda i, j: (i, 0)), pl.BlockSpec((k, tn), lambda i, j: (0, j + b_off))]
    if col_blocks is None:
        out_spec = pl.BlockSpec((tm, tn), lambda i, j: (i, j))
        out_shape = jax.ShapeDtypeStruct((m, nn), out_dtype)
    else:
        per = nn // col_blocks // tn
        assert per * tn * col_blocks == nn, (name, nn, tn, col_blocks)
        out_spec = pl.BlockSpec((1, tm, tn), lambda i, j: (j // per, i, j % per))
        out_shape = jax.ShapeDtypeStruct((col_blocks, m, nn // col_blocks), out_dtype)
    (res,), arrived = _grid_call(
        body, name=name, grid=(m // tm, nn // tn), in_specs=in_specs, out_specs=[out_spec], out_shape=[out_shape],
        args=(a, b), semantics=("parallel", "parallel"), exchange=exchange)
    return res if exchange is None else (res, arrived)


ROW_TILE = 256


def _row_call(body, name, ins, outs, *, s):
    tr = min(ROW_TILE, s)
    spec = {"row": lambda sh: pl.BlockSpec((tr, sh[1]), lambda i: (i, 0)),
            "vec": lambda sh: pl.BlockSpec((1, sh[1]), lambda i: (0, 0)),
            "col": lambda sh: pl.BlockSpec((sh[0], tr), lambda i: (0, i))}
    in_specs = [spec[kind](a.shape) for a, kind in ins]
    out_specs = [spec[kind](sh.shape) for sh, kind in outs]
    return pl.pallas_call(
        body, name=name, grid=(s // tr,), in_specs=in_specs, out_specs=out_specs,
        out_shape=[sh for sh, _ in outs],
        compiler_params=_params(("arbitrary",)),
    )(*[a for a, _ in ins])


def _rsqrt_ms(v):
    return lax.rsqrt(jnp.mean(v * v, axis=-1, keepdims=True) + NORM_EPS)


def _rmsnorm_fwd(x, g, name):
    s, d = x.shape

    def body(x_ref, g_ref, h_ref, ht_ref):
        xv = x_ref[...]
        h = xv * _rsqrt_ms(xv) * g_ref[...]
        h_ref[...] = h.astype(BF16)
        ht_ref[...] = h.T.astype(BF16)

    return _row_call(body, name, [(x, "row"), (g, "vec")],
                     [(jax.ShapeDtypeStruct((s, d), BF16), "row"), (jax.ShapeDtypeStruct((d, s), BF16), "col")], s=s)


PROJ_ROWS = 256


def _resident(shape):
    return pl.BlockSpec(shape, lambda i: (0,) * len(shape), pipeline_mode=pl.Buffered(1))


def _gated_out_proj(o, z, w_out, x, g, name):
    s, d = x.shape
    tm = min(PROJ_ROWS, s)

    def body(o_ref, z_ref, w_ref, x_ref, g_ref, xn_ref, y_ref, ut_ref):
        zv = z_ref[...]
        u = o_ref[...] * (zv * jax.nn.sigmoid(zv))
        ut_ref[...] = u.T.astype(BF16)
        y = jnp.dot(u.astype(BF16), w_ref[...], preferred_element_type=F32)
        y_ref[...] = y
        xn_ref[...] = x_ref[...] + y * _rsqrt_ms(y) * g_ref[...]

    row = pl.BlockSpec((tm, d), lambda i: (i, 0))
    return pl.pallas_call(
        body, name=name, grid=(s // tm,),
        in_specs=[row, row, _resident(w_out.shape), row, _resident((1, d))],
        out_specs=[row, row, pl.BlockSpec((d, tm), lambda i: (0, i))],
        out_shape=[jax.ShapeDtypeStruct((s, d), F32), jax.ShapeDtypeStruct((s, d), F32), jax.ShapeDtypeStruct((d, s), BF16)],
        compiler_params=_params(("parallel",)),
    )(o, z, w_out, x, g)


def _gated_out_proj_bwd(dx, y, g, w_out, o, z, name, exchange=None):
    s, d = dx.shape
    tm = min(PROJ_ROWS, s)

    def body(dx_ref, y_ref, g_ref, w_ref, o_ref, z_ref, dy_ref, dg_ref, do_ref, dz_ref):
        dy, dg = _norm_bwd_rows(dx_ref[...], y_ref[...], g_ref[...])
        dyb = dy.astype(BF16)
        dy_ref[...] = dyb

        @pl.when(pl.program_id(0) == 0)
        def _():
            dg_ref[...] = jnp.zeros_like(dg_ref)

        dg_ref[...] += jnp.sum(dg, axis=0, keepdims=True)
        du = lax.dot_general(dyb, w_ref[...], _NT, preferred_element_type=F32)
        zv = z_ref[...]
        sig = jax.nn.sigmoid(zv)
        do_ref[...] = (du * (zv * sig)).astype(BF16)
        dz_ref[...] = (du * o_ref[...] * (sig * (1.0 + zv * (1.0 - sig)))).astype(BF16)

    row = pl.BlockSpec((tm, d), lambda i: (i, 0))
    vec = pl.BlockSpec((1, d), lambda i: (0, 0))
    bf = jax.ShapeDtypeStruct((s, d), BF16)
    return _grid_call(
        body, name=name, grid=(s // tm,),
        in_specs=[row, row, _resident((1, d)), _resident(w_out.shape), row, row],
        out_specs=[row, vec, row, row], out_shape=[bf, jax.ShapeDtypeStruct((1, d), F32), bf, bf],
        args=(dx, y, g, w_out, o, z), semantics=("arbitrary",), exchange=exchange)


IN_BWD_ROWS = 512


def _in_proj_bwd(dproj, w_in, extra, dx, x, g, name, tk, exchange=None):
    s, d = x.shape
    k = dproj.shape[1]
    tm = min(IN_BWD_ROWS, s)
    nk = k // tk
    assert k % tk == 0 and s % tm == 0, (name, k, tk)
    has_extra = extra is not None

    def body(a_ref, b_ref, *rest):
        if has_extra:
            e_ref, rest = rest[0], rest[1:]
        dx_ref, x_ref, g_ref, o_ref, dg_ref, acc_ref = rest
        i, kk = pl.program_id(0), pl.program_id(1)
        p = lax.dot_general(a_ref[...], b_ref[...], _NT, preferred_element_type=F32)

        @pl.when(kk == 0)
        def _():
            acc_ref[...] = p

        @pl.when(kk > 0)
        def _():
            acc_ref[...] += p

        @pl.when((i == 0) & (kk == 0))
        def _():
            dg_ref[...] = jnp.zeros_like(dg_ref)

        @pl.when(kk == nk - 1)
        def _():
            def rows_chunk(c, _):
                r = pl.ds(pl.multiple_of(c * LANES, LANES), LANES)
                dh = acc_ref[r, :] + e_ref[r, :] if has_extra else acc_ref[r, :]
                dv, dg = _norm_bwd_rows(dh, x_ref[r, :], g_ref[...])
                o_ref[r, :] = dx_ref[r, :] + dv
                dg_ref[...] += jnp.sum(dg, axis=0, keepdims=True)
                return 0

            lax.fori_loop(0, tm // LANES, rows_chunk, 0)

    row = pl.BlockSpec((tm, d), lambda i, kk: (i, 0))
    vec = pl.BlockSpec((1, d), lambda i, kk: (0, 0))
    in_specs = [pl.BlockSpec((tm, tk), lambda i, kk: (i, kk)), pl.BlockSpec((d, tk), lambda i, kk: (0, kk))]
    args = [dproj, w_in]
    if has_extra:
        in_specs.append(row)
        args.append(extra)
    return _grid_call(
        body, name=name, grid=(s // tm, nk), in_specs=in_specs + [row, row, vec], out_specs=[row, vec],
        out_shape=[jax.ShapeDtypeStruct((s, d), F32), jax.ShapeDtypeStruct((1, d), F32)],
        args=tuple(args) + (dx, x, g), scratch_shapes=[pltpu.VMEM((tm, d), F32)], semantics=("arbitrary", "arbitrary"),
        exchange=exchange)


def _loss_and_grad(x, target):
    s, d = x.shape

    def body(x_ref, t_ref, dx_ref, l_ref):
        err = x_ref[...] - t_ref[...]
        dx_ref[...] = err * (1.0 / d)
        part = jnp.sum(jnp.sum(err * err, axis=1, keepdims=True), axis=0, keepdims=True) * (0.5 / d)

        @pl.when(pl.program_id(0) == 0)
        def _():
            l_ref[...] = jnp.zeros_like(l_ref)

        l_ref[...] += jnp.broadcast_to(part, l_ref.shape)

    return _row_call(body, "loss_head", [(x, "row"), (target, "row")],
                     [(jax.ShapeDtypeStruct((s, d), F32), "row"),
                      (jax.ShapeDtypeStruct((1, LANES), F32), "vec")], s=s)


def _norm_bwd_rows(dn, v, g):
    r = _rsqrt_ms(v)
    a = dn * g
    dv = r * (a - v * (r * r) * jnp.mean(a * v, axis=-1, keepdims=True))
    return dv, dn * v * r


def _lane_is_first_head():
    return lax.broadcasted_iota(jnp.int32, (1, LANES), 1) < HEAD_DIM


def _bcast_lanes(col):
    return jnp.broadcast_to(col, (col.shape[0], LANES))


def _pair_spec(s, off=0, width=LANES):
    return pl.BlockSpec((s, width), lambda p: (0, p + off))


def _stack_heads(pair, first):
    return jnp.concatenate([jnp.where(first, pair, 0), jnp.where(first, 0, pair)], axis=0).astype(BF16)


def _stacked_mask(t, strict):
    row = lax.broadcasted_iota(jnp.int32, (2 * t, t), 0)
    col = lax.broadcasted_iota(jnp.int32, (2 * t, t), 1)
    query = jnp.where(row >= t, row - t, row)
    return col < query if strict else col <= query


def _steps_in_groups(n, step, carry, widths=(2, 1)):
    done = 0
    for width in widths:
        def group(jj, c, width=width, done=done):
            for k in range(width):
                c = step(done + width * jj + k, c)
            return c

        trips = (n - done) // width
        carry = lax.fori_loop(0, trips, group, carry)
        done = done + width * trips
    return carry


FULL_ATTENTION_WIDTHS = (4, 2, 1)


def _rowsum_heads(prod, first):
    return (jnp.sum(jnp.where(first, prod, 0.0), axis=1, keepdims=True),
            jnp.sum(jnp.where(first, 0.0, prod), axis=1, keepdims=True))


def _softplus_parts(z):
    e = jnp.exp(-jnp.abs(z))
    sp = jnp.maximum(z, 0.0) + jnp.log(1.0 + e)
    r = 1.0 / (1.0 + e)
    return sp, jnp.where(z >= 0, r, e * r)


def _sb_tile(s):
    return min(256, s)


def _attn_b_fwd(qkv, name, exchange=None):
    s = qkv.shape[0]
    t = _sb_tile(s)
    nq = s // t

    def body(q_ref, k_ref, v_ref, o_ref, lt_ref):
        first = _lane_is_first_head()
        before = _stacked_mask(t, strict=True)
        tri = (lax.broadcasted_iota(jnp.int32, (t, t), 0) >= lax.broadcasted_iota(jnp.int32, (t, t), 1)).astype(BF16)

        def tile(j, carry, diag, qs):
            c, acc = carry
            c0 = pl.multiple_of(j * t, t)
            k2 = k_ref[pl.ds(c0, t), :]
            v2 = v_ref[pl.ds(c0, t), :]
            z = lax.dot_general(qs, k2, _NT, preferred_element_type=F32)
            sp, _ = _softplus_parts(z)
            lf = jnp.where(before, -sp, 0.0) if diag else -sp
            incl = jnp.dot(lf.astype(BF16), tri, preferred_element_type=F32)
            a = jnp.exp(z + c + incl)
            if diag:
                a = jnp.where(before, a, 0.0)
            pv = jnp.dot(a.astype(BF16), v2, preferred_element_type=F32)
            return c + incl[:, 0:1], acc + jnp.where(first, pv[:t], pv[t:])

        def qblock(i, _):
            r0 = pl.multiple_of(i * t, t)
            qs = _stack_heads(q_ref[pl.ds(r0, t), :] * Q_SCALE, first)
            carry = tile(i, (jnp.zeros((2 * t, 1), F32), jnp.zeros((t, LANES), F32)), True, qs)
            carry = _steps_in_groups(i, lambda j, c: tile(i - 1 - j, c, False, qs), carry, FULL_ATTENTION_WIDTHS)
            o_ref[pl.ds(r0, t), :] = carry[1]
            lt_ref[pl.ds(r0, t), 0:LANES] = _bcast_lanes(carry[0][:t])
            lt_ref[pl.ds(r0, t), LANES:2 * LANES] = _bcast_lanes(carry[0][t:])
            return 0

        lax.fori_loop(0, nq, qblock, 0)

    return _grid_call(
        body, name=name, grid=(N_PAIRS,),
        in_specs=[_pair_spec(s), _pair_spec(s, N_PAIRS), _pair_spec(s, 2 * N_PAIRS)],
        out_specs=[_pair_spec(s), _stat_spec(s)],
        out_shape=[jax.ShapeDtypeStruct((s, BRANCH), F32), jax.ShapeDtypeStruct((s, N_HEADS * LANES), F32)],
        args=(qkv, qkv, qkv), semantics=("parallel",), exchange=exchange)


def _attn_b_bwd(qkv, ltot, do, name, exchange=None):
    s = qkv.shape[0]
    t = _sb_tile(s)
    nq = s // t

    def body(q_ref, k_ref, v_ref, lt_ref, do_ref, dq_ref, dk_ref, dv_ref, dk_acc, dv_acc):
        first = _lane_is_first_head()
        before = _stacked_mask(t, strict=True)
        tri = (lax.broadcasted_iota(jnp.int32, (t, t), 0) <= lax.broadcasted_iota(jnp.int32, (t, t), 1)).astype(BF16)
        dk_acc[...] = jnp.zeros_like(dk_acc)
        dv_acc[...] = jnp.zeros_like(dv_acc)

        def tile(j, carry, diag, qs, dos, lt):
            p_l, p_g, dq_acc = carry
            c0 = pl.multiple_of(j * t, t)
            k2 = k_ref[pl.ds(c0, t), :]
            v2 = v_ref[pl.ds(c0, t), :]
            z = lax.dot_general(qs, k2, _NT, preferred_element_type=F32)
            sp, sig = _softplus_parts(z)
            lf = jnp.where(before, -sp, 0.0) if diag else -sp
            pref_l = jnp.dot(lf.astype(BF16), tri, preferred_element_type=F32)
            a = jnp.exp(z + ((lt - p_l) - pref_l + lf))
            if diag:
                a = jnp.where(before, a, 0.0)
            g = a * lax.dot_general(dos, v2, _NT, preferred_element_type=F32)
            pref_g = jnp.dot(g.astype(BF16), tri, preferred_element_type=F32)
            dz = g - sig * (p_g + pref_g)
            if diag:
                dz = jnp.where(before, dz, 0.0)
            dzb = dz.astype(BF16)
            dq = jnp.dot(dzb, k2, preferred_element_type=F32)
            dk_acc[pl.ds(c0, t), :] += lax.dot_general(dzb, qs, _TN, preferred_element_type=F32)
            dv_acc[pl.ds(c0, t), :] += lax.dot_general(a.astype(BF16), dos, _TN, preferred_element_type=F32)
            return p_l + pref_l[:, t - 1:t], p_g + pref_g[:, t - 1:t], dq_acc + jnp.where(first, dq[:t], dq[t:])

        def qblock(i, _):
            r0 = pl.multiple_of(i * t, t)
            qs = _stack_heads(q_ref[pl.ds(r0, t), :] * Q_SCALE, first)
            dos = _stack_heads(do_ref[pl.ds(r0, t), :], first)
            lt = jnp.concatenate([lt_ref[pl.ds(r0, t), 0:1], lt_ref[pl.ds(r0, t), LANES:LANES + 1]], axis=0)
            zero = jnp.zeros((2 * t, 1), F32)
            carry = (zero, zero, jnp.zeros((t, LANES), F32))
            carry = _steps_in_groups(i, lambda j, c: tile(j, c, False, qs, dos, lt), carry, FULL_ATTENTION_WIDTHS)
            carry = tile(i, carry, True, qs, dos, lt)
            dq_ref[pl.ds(r0, t), :] = (carry[2] * Q_SCALE).astype(BF16)
            return 0

        lax.fori_loop(0, nq, qblock, 0)
        dk_ref[...] = dk_acc[...].astype(BF16)
        dv_ref[...] = dv_acc[...].astype(BF16)

    out = jax.ShapeDtypeStruct((s, BRANCH), BF16)
    return _grid_call(
        body, name=name, grid=(N_PAIRS,),
        in_specs=[_pair_spec(s), _pair_spec(s, N_PAIRS), _pair_spec(s, 2 * N_PAIRS), _stat_spec(s), _pair_spec(s)],
        out_specs=[_pair_spec(s)] * 3, out_shape=[out] * 3,
        scratch_shapes=[pltpu.VMEM((s, LANES), F32), pltpu.VMEM((s, LANES), F32)],
        args=(qkv, qkv, qkv, ltot, do), semantics=("parallel",), exchange=exchange)


def _fox_tile(s):
    return min(256, s)


def _stat_spec(s):
    return pl.BlockSpec((s, 2 * LANES), lambda p: (0, p))


def _cum_spec(nt, t):
    return pl.BlockSpec((1, nt, 2, t), lambda p: (p, 0, 0, 0))


def _attn_c_fwd(qkv, cum4, name, exchange=None):
    s = qkv.shape[0]
    t = _fox_tile(s)
    nq = s // t

    def body(q_ref, k_ref, v_ref, c_ref, o_ref, lse_ref):
        first = _lane_is_first_head()
        causal = _stacked_mask(t, strict=False)

        def tile(j, carry, diag, qs):
            c0 = pl.multiple_of(j * t, t)
            k2 = k_ref[pl.ds(c0, t), :]
            v2 = v_ref[pl.ds(c0, t), :]
            cs = c_ref[0, j]
            m_prev, l_prev, acc = carry
            z = lax.dot_general(qs, k2, _NT, preferred_element_type=F32)
            sc = jnp.concatenate([z[:t] - cs[0:1, :], z[t:] - cs[1:2, :]], axis=0)
            if diag:
                sc = jnp.where(causal, sc, NEG)
            m_new = jnp.maximum(m_prev, jnp.max(sc, axis=1, keepdims=True))
            alpha = jnp.exp(m_prev - m_new)
            p = jnp.exp(sc - m_new)
            l_new = alpha * l_prev + jnp.sum(p, axis=1, keepdims=True)
            pv = jnp.dot(p.astype(BF16), v2, preferred_element_type=F32)
            acc = jnp.where(first, acc * alpha[:t] + pv[:t], acc * alpha[t:] + pv[t:])
            return m_new, l_new, acc

        def qblock(i, _):
            r0 = pl.multiple_of(i * t, t)
            qs = _stack_heads(q_ref[pl.ds(r0, t), :] * Q_SCALE, first)
            carry = (jnp.full((2 * t, 1), NEG, F32), jnp.zeros((2 * t, 1), F32), jnp.zeros((t, LANES), F32))
            carry = _steps_in_groups(i, lambda j, c: tile(j, c, False, qs), carry, FULL_ATTENTION_WIDTHS)
            m, l, acc = tile(i, carry, True, qs)
            inv = 1.0 / l
            lse = m + jnp.log(l)
            o_ref[pl.ds(r0, t), :] = acc * jnp.where(first, inv[:t], inv[t:])
            lse_ref[pl.ds(r0, t), 0:LANES] = _bcast_lanes(lse[:t])
            lse_ref[pl.ds(r0, t), LANES:2 * LANES] = _bcast_lanes(lse[t:])
            return 0

        lax.fori_loop(0, nq, qblock, 0)

    return _grid_call(
        body, name=name, grid=(N_PAIRS,),
        in_specs=[_pair_spec(s), _pair_spec(s, N_PAIRS), _pair_spec(s, 2 * N_PAIRS), _cum_spec(nq, t)],
        out_specs=[_pair_spec(s), _stat_spec(s)],
        out_shape=[jax.ShapeDtypeStruct((s, BRANCH), F32), jax.ShapeDtypeStruct((s, N_HEADS * LANES), F32)],
        args=(qkv, qkv, qkv, cum4), semantics=("parallel",), exchange=exchange)


def _attn_c_bwd(qkv, cum4, o, lse, do, name, exchange=None):
    s = qkv.shape[0]
    t = _fox_tile(s)
    nq = s // t

    def body(q_ref, k_ref, v_ref, c_ref, o_ref, lse_ref, do_ref, dq_ref, dk_ref, dv_ref, dc_ref, dk_acc, dv_acc):
        first = _lane_is_first_head()
        causal = _stacked_mask(t, strict=False)
        eye = lax.broadcasted_iota(jnp.int32, (t, t), 0) == lax.broadcasted_iota(jnp.int32, (t, t), 1)
        dk_acc[...] = jnp.zeros_like(dk_acc)
        dv_acc[...] = jnp.zeros_like(dv_acc)
        dc_ref[...] = jnp.zeros_like(dc_ref)

        def tile(j, carry, diag, qs, dos, delta, lse):
            dq_acc, rs = carry
            c0 = pl.multiple_of(j * t, t)
            k2 = k_ref[pl.ds(c0, t), :]
            v2 = v_ref[pl.ds(c0, t), :]
            cs = c_ref[0, j]
            z = lax.dot_general(qs, k2, _NT, preferred_element_type=F32)
            sc = jnp.concatenate([z[:t] - cs[0:1, :], z[t:] - cs[1:2, :]], axis=0)
            p = jnp.exp(sc - lse)
            if diag:
                p = jnp.where(causal, p, 0.0)
            ds = p * (lax.dot_general(dos, v2, _NT, preferred_element_type=F32) - delta)
            dsb = ds.astype(BF16)
            dq = jnp.dot(dsb, k2, preferred_element_type=F32)
            dk_acc[pl.ds(c0, t), :] += lax.dot_general(dsb, qs, _TN, preferred_element_type=F32)
            dv_acc[pl.ds(c0, t), :] += lax.dot_general(p.astype(BF16), dos, _TN, preferred_element_type=F32)
            col_sums = jnp.concatenate([jnp.sum(ds[:t], axis=0, keepdims=True), jnp.sum(ds[t:], axis=0, keepdims=True)], axis=0)
            dc_ref[0, j] = dc_ref[0, j] - col_sums
            return dq_acc + jnp.where(first, dq[:t], dq[t:]), rs + jnp.sum(ds, axis=1, keepdims=True)

        def qblock(i, _):
            r0 = pl.multiple_of(i * t, t)
            do2 = do_ref[pl.ds(r0, t), :]
            qs = _stack_heads(q_ref[pl.ds(r0, t), :] * Q_SCALE, first)
            dos = _stack_heads(do2, first)
            delta = jnp.concatenate(_rowsum_heads(do2.astype(F32) * o_ref[pl.ds(r0, t), :], first), axis=0)
            lse = jnp.concatenate([lse_ref[pl.ds(r0, t), 0:1], lse_ref[pl.ds(r0, t), LANES:LANES + 1]], axis=0)
            carry = (jnp.zeros((t, LANES), F32), jnp.zeros((2 * t, 1), F32))
            carry = _steps_in_groups(i, lambda j, c: tile(j, c, False, qs, dos, delta, lse), carry, FULL_ATTENTION_WIDTHS)
            dq_acc, rs = tile(i, carry, True, qs, dos, delta, lse)
            dq_ref[pl.ds(r0, t), :] = (dq_acc * Q_SCALE).astype(BF16)
            as_row = lambda col_vec: jnp.sum(jnp.where(eye, col_vec, 0.0), axis=0, keepdims=True)
            dc_ref[0, i] = dc_ref[0, i] + jnp.concatenate([as_row(rs[:t]), as_row(rs[t:])], axis=0)
            return 0

        lax.fori_loop(0, nq, qblock, 0)
        dk_ref[...] = dk_acc[...].astype(BF16)
        dv_ref[...] = dv_acc[...].astype(BF16)

    out = jax.ShapeDtypeStruct((s, BRANCH), BF16)
    return _grid_call(
        body, name=name, grid=(N_PAIRS,),
        in_specs=[_pair_spec(s), _pair_spec(s, N_PAIRS), _pair_spec(s, 2 * N_PAIRS), _cum_spec(nq, t),
                  _pair_spec(s), _stat_spec(s), _pair_spec(s)],
        out_specs=[_pair_spec(s)] * 3 + [_cum_spec(nq, t)],
        out_shape=[out] * 3 + [jax.ShapeDtypeStruct(cum4.shape, F32)],
        scratch_shapes=[pltpu.VMEM((s, LANES), F32), pltpu.VMEM((s, LANES), F32)],
        args=(qkv, qkv, qkv, cum4, o, lse, do), semantics=("parallel",), exchange=exchange)


FG_CHUNK = 512


def _tri_dot3(x, t):
    hi = x.astype(BF16)
    r1 = x - hi.astype(F32)
    mid = r1.astype(BF16)
    lo = (r1 - mid.astype(F32)).astype(BF16)
    return (jnp.dot(hi, t, preferred_element_type=F32) + jnp.dot(mid, t, preferred_element_type=F32)
            + jnp.dot(lo, t, preferred_element_type=F32))


def _fgate_fwd(h, wf_t, b_col, name):
    s = h.shape[0]
    c = min(FG_CHUNK, s)

    def body(h_ref, w_ref, b_ref, xf_ref, cum_ref, carry_ref):
        @pl.when(pl.program_id(0) == 0)
        def _():
            carry_ref[...] = jnp.zeros_like(carry_ref)

        xf = lax.dot_general(w_ref[...], h_ref[...], _NT, preferred_element_type=F32) + b_ref[:, 0:1]
        xf_ref[...] = xf
        logf = jnp.minimum(xf, 0.0) - jnp.log(1.0 + jnp.exp(-jnp.abs(xf)))
        row = lax.broadcasted_iota(jnp.int32, (c, c), 0)
        col = lax.broadcasted_iota(jnp.int32, (c, c), 1)
        cum = _tri_dot3(logf, (row <= col).astype(BF16)) + carry_ref[:, 0:1]
        cum_ref[...] = cum
        carry_ref[...] = _bcast_lanes(cum[:, c - 1:c])

    out = jax.ShapeDtypeStruct((N_HEADS, s), F32)
    return pl.pallas_call(
        body, name=name, grid=(s // c,),
        in_specs=[pl.BlockSpec((c, D_MODEL), lambda i: (i, 0)),
                  pl.BlockSpec((N_HEADS, D_MODEL), lambda i: (0, 0)),
                  pl.BlockSpec((N_HEADS, LANES), lambda i: (0, 0))],
        out_specs=[pl.BlockSpec((N_HEADS, c), lambda i: (0, i))] * 2,
        out_shape=[out, out],
        scratch_shapes=[pltpu.VMEM((N_HEADS, LANES), F32)],
        compiler_params=_params(("arbitrary",)),
    )(h, wf_t, b_col)


def _fgate_bwd(dcum, xf, h, wf_t, name):
    s = h.shape[0]
    c = min(FG_CHUNK, s)
    n = s // c

    def body(dc_ref, xf_ref, h_ref, w_ref, dw_ref, dh_ref, db_ref, carry_ref):
        @pl.when(pl.program_id(0) == 0)
        def _():
            carry_ref[...] = jnp.zeros_like(carry_ref)
            dw_ref[...] = jnp.zeros_like(dw_ref)
            db_ref[...] = jnp.zeros_like(db_ref)

        row = lax.broadcasted_iota(jnp.int32, (c, c), 0)
        col = lax.broadcasted_iota(jnp.int32, (c, c), 1)
        dlogf = _tri_dot3(dc_ref[...], (row >= col).astype(BF16)) + carry_ref[:, 0:1]
        carry_ref[...] = _bcast_lanes(dlogf[:, 0:1])
        xf = xf_ref[...]
        e = jnp.exp(-jnp.abs(xf))
        r = 1.0 / (1.0 + e)
        dxf = dlogf * jnp.where(xf >= 0, e * r, r)
        db_ref[...] += _bcast_lanes(jnp.sum(dxf, axis=1, keepdims=True))
        dxb = dxf.astype(BF16)
        dw_ref[...] += jnp.dot(dxb, h_ref[...], preferred_element_type=F32)
        dh_ref[...] = lax.dot_general(dxb, w_ref[...], _TN, preferred_element_type=F32)

    rev = lambda i: n - 1 - i
    return pl.pallas_call(
        body, name=name, grid=(n,),
        in_specs=[pl.BlockSpec((N_HEADS, c), lambda i: (0, rev(i))),
                  pl.BlockSpec((N_HEADS, c), lambda i: (0, rev(i))),
                  pl.BlockSpec((c, D_MODEL), lambda i: (rev(i), 0)),
                  pl.BlockSpec((N_HEADS, D_MODEL), lambda i: (0, 0))],
        out_specs=[pl.BlockSpec((N_HEADS, D_MODEL), lambda i: (0, 0)),
                   pl.BlockSpec((c, D_MODEL), lambda i: (rev(i), 0)),
                   pl.BlockSpec((N_HEADS, LANES), lambda i: (0, 0))],
        out_shape=[jax.ShapeDtypeStruct((N_HEADS, D_MODEL), F32), jax.ShapeDtypeStruct((s, D_MODEL), F32),
                   jax.ShapeDtypeStruct((N_HEADS, LANES), F32)],
        scratch_shapes=[pltpu.VMEM((N_HEADS, LANES), F32)],
        compiler_params=_params(("arbitrary",)),
    )(dcum, xf, h, wf_t)


def _to_cum4(v, t):
    s = v.shape[1]
    return v.reshape(N_PAIRS, 2, s // t, t).transpose(0, 2, 1, 3)


def _from_cum4(v4):
    p, nt, two, t = v4.shape
    return v4.transpose(0, 2, 1, 3).reshape(p * two, nt * t)


def _alibi_slopes():
    return (2.0 ** (-8.0 * np.arange(1, N_HEADS + 1, dtype=np.float32) / N_HEADS)).astype(np.float32)


def _per_head_lanes(v):
    return jnp.repeat(v.astype(F32).reshape(N_PAIRS, 1, 2), LANES, axis=2)


def _attn_a_specs(s):
    q = _pair_spec(s)
    k = pl.BlockSpec((s, LANES), lambda p: (0, N_PAIRS + p // 8))
    v = pl.BlockSpec((s, LANES), lambda p: (0, N_PAIRS + KV_A // LANES + p // 8))
    head = pl.BlockSpec((1, 1, 2 * LANES), lambda p: (p, 0, 0))
    return q, k, v, head


def _attn_a_geometry(p, slope_ref, sink_ref):
    kv_half = (p // 4) % 2
    kv_first = kv_half == 0
    lane_first = _lane_is_first_head()
    kv_lanes = (lax.broadcasted_iota(jnp.int32, (1, LANES), 1) // HEAD_DIM) == kv_half
    row = lax.broadcasted_iota(jnp.int32, (2 * WINDOW, 2 * WINDOW), 0)
    cj = lax.broadcasted_iota(jnp.int32, (2 * WINDOW, 2 * WINDOW), 1)
    second = row >= WINDOW
    dist = WINDOW + jnp.where(second, row - WINDOW, row) - cj
    valid = (dist >= 0) & (dist < WINDOW)
    per_row = lambda ref: jnp.where(second[:, 0:1], ref[0, :, LANES:LANES + 1], ref[0, :, 0:1])
    return kv_first, lane_first, kv_lanes, per_row(slope_ref) * dist.astype(F32), valid, per_row(sink_ref)


def _swap_halves(x):
    return pltpu.roll(x, HEAD_DIM, 1)


def _attn_a_fwd(qkv, slopes, sinks, name, exchange=None):
    s = qkv.shape[0]
    nb = s // WINDOW

    def body(q_ref, k_ref, v_ref, sl_ref, sk_ref, o_ref, lse_ref):
        kv_first, lane_first, kv_lanes, bias, valid, sink = _attn_a_geometry(pl.program_id(0), sl_ref, sk_ref)

        def block(r0, k0, width):
            q2 = q_ref[pl.ds(r0, WINDOW), :].astype(F32) * Q_SCALE
            q2r = _swap_halves(q2)
            xs = jnp.concatenate([jnp.where(kv_first, q2, q2r), jnp.where(kv_first, q2r, q2)], axis=0).astype(BF16)
            km = jnp.where(kv_lanes, k_ref[pl.ds(k0, width), :], 0).astype(BF16)
            vm = jnp.where(kv_lanes, v_ref[pl.ds(k0, width), :], 0).astype(BF16)
            sc = lax.dot_general(xs, km, _NT, preferred_element_type=F32) - bias[:, 2 * WINDOW - width:]
            sc = jnp.where(valid[:, 2 * WINDOW - width:], sc, NEG)
            m = jnp.maximum(jnp.max(sc, axis=1, keepdims=True), sink)
            pr = jnp.exp(sc - m)
            l = jnp.sum(pr, axis=1, keepdims=True) + jnp.exp(sink - m)
            os = jnp.dot(pr.astype(BF16), vm, preferred_element_type=F32) * (1.0 / l)
            lse = m + jnp.log(l)
            lse_ref[pl.ds(r0, WINDOW), 0:LANES] = _bcast_lanes(lse[:WINDOW])
            lse_ref[pl.ds(r0, WINDOW), LANES:2 * LANES] = _bcast_lanes(lse[WINDOW:])
            oa = jnp.where(kv_first, os[:WINDOW], _swap_halves(os[:WINDOW]))
            ob = jnp.where(kv_first, _swap_halves(os[WINDOW:]), os[WINDOW:])
            o_ref[pl.ds(r0, WINDOW), :] = jnp.where(lane_first, oa, ob)

        block(0, 0, WINDOW)

        def loop(n, _):
            r0 = pl.multiple_of(n * WINDOW, WINDOW)
            block(r0, pl.multiple_of(r0 - WINDOW, WINDOW), 2 * WINDOW)
            return 0

        _steps_in_groups(nb - 1, lambda n, c: loop(n + 1, c), 0)

    q, k, v, head = _attn_a_specs(s)
    return _grid_call(
        body, name=name, grid=(N_PAIRS,),
        in_specs=[q, k, v, head, head],
        out_specs=[_pair_spec(s), _stat_spec(s)],
        out_shape=[jax.ShapeDtypeStruct((s, BRANCH), F32), jax.ShapeDtypeStruct((s, N_HEADS * LANES), F32)],
        args=(qkv, qkv, qkv, slopes, sinks), semantics=("parallel",), exchange=exchange)


def _attn_a_bwd(qkv, slopes, sinks, o, lse, do, name, exchange=None):
    s = qkv.shape[0]
    nb = s // WINDOW

    def body(q_ref, k_ref, v_ref, sl_ref, sk_ref, o_ref, lse_ref, do_ref, dq_ref, dk_ref, dv_ref, dsk_ref):
        p_id = pl.program_id(0)
        kv_first, lane_first, kv_lanes, bias, valid, sink = _attn_a_geometry(p_id, sl_ref, sk_ref)

        @pl.when(p_id % 8 == 0)
        def _():
            dk_ref[...] = jnp.zeros_like(dk_ref)
            dv_ref[...] = jnp.zeros_like(dv_ref)

        def align(v2):
            v2r = _swap_halves(v2)
            both = jnp.concatenate([jnp.where(kv_first, v2, v2r), jnp.where(kv_first, v2r, v2)], axis=0)
            return jnp.where(kv_lanes, both, 0.0).astype(BF16)

        def block(r0, k0, width, sink_sum):
            xq = align(q_ref[pl.ds(r0, WINDOW), :].astype(F32) * Q_SCALE)
            do2 = do_ref[pl.ds(r0, WINDOW), :].astype(F32)
            xdo = align(do2)
            delta = jnp.concatenate(_rowsum_heads(do2 * o_ref[pl.ds(r0, WINDOW), :], lane_first), axis=0)
            lse = jnp.concatenate([lse_ref[pl.ds(r0, WINDOW), 0:1], lse_ref[pl.ds(r0, WINDOW), LANES:LANES + 1]], axis=0)
            km = jnp.where(kv_lanes, k_ref[pl.ds(k0, width), :], 0).astype(BF16)
            vm = jnp.where(kv_lanes, v_ref[pl.ds(k0, width), :], 0).astype(BF16)
            sc = lax.dot_general(xq, km, _NT, preferred_element_type=F32) - bias[:, 2 * WINDOW - width:]
            pr = jnp.where(valid[:, 2 * WINDOW - width:], jnp.exp(sc - lse), 0.0)
            ds = pr * (lax.dot_general(xdo, vm, _NT, preferred_element_type=F32) - delta)
            dsb = ds.astype(BF16)
            dq_al = jnp.dot(dsb, km, preferred_element_type=F32)
            dk_ref[pl.ds(k0, width), :] += lax.dot_general(dsb, xq, _TN, preferred_element_type=F32)
            dv_ref[pl.ds(k0, width), :] += lax.dot_general(pr.astype(BF16), xdo, _TN, preferred_element_type=F32)
            dqa = jnp.where(kv_first, dq_al[:WINDOW], _swap_halves(dq_al[:WINDOW]))
            dqb = jnp.where(kv_first, _swap_halves(dq_al[WINDOW:]), dq_al[WINDOW:])
            dq_ref[pl.ds(r0, WINDOW), :] = (jnp.where(lane_first, dqa, dqb) * Q_SCALE).astype(BF16)
            return sink_sum + jnp.exp(sink - lse) * delta

        sink_sum = block(0, 0, WINDOW, jnp.zeros((2 * WINDOW, 1), F32))

        def loop(n, c):
            r0 = pl.multiple_of(n * WINDOW, WINDOW)
            return block(r0, pl.multiple_of(r0 - WINDOW, WINDOW), 2 * WINDOW, c)

        sink_sum = _steps_in_groups(nb - 1, lambda n, c: loop(n + 1, c), sink_sum)
        dsk_ref[0, :, 0:LANES] = jnp.broadcast_to(-jnp.sum(sink_sum[:WINDOW], axis=0, keepdims=True), (1, LANES))
        dsk_ref[0, :, LANES:2 * LANES] = jnp.broadcast_to(-jnp.sum(sink_sum[WINDOW:], axis=0, keepdims=True), (1, LANES))

    q, k, v, head = _attn_a_specs(s)
    kv_out = pl.BlockSpec((s, LANES), lambda p: (0, p // 8))
    return _grid_call(
        body, name=name, grid=(N_PAIRS,),
        in_specs=[q, k, v, head, head, _pair_spec(s), _stat_spec(s), _pair_spec(s)],
        out_specs=[_pair_spec(s), kv_out, kv_out, head],
        out_shape=[jax.ShapeDtypeStruct((s, BRANCH), BF16), jax.ShapeDtypeStruct((s, KV_A), F32),
                   jax.ShapeDtypeStruct((s, KV_A), F32), jax.ShapeDtypeStruct((N_PAIRS, 1, 2 * LANES), F32)],
        args=(qkv, qkv, qkv, slopes, sinks, o, lse, do), semantics=("arbitrary",), exchange=exchange)


def _layer_kind(i):
    return i % 3, i // 3


GATHER_FIRST = [("in", 0)]
GATHER_BEHIND = {("qkv", 0): [("out", 0)], ("attn", 0): [("in", 1)], ("attn", 1): [("out", 1), ("in", 2), ("out", 2)],
                 ("attn", 2): [("in", 3), ("out", 3)]}


def _forward_backward(x, target, g_pre, g_post, sinks_a, b_f_c, shards, chip, place):
    s = x.shape[0]
    slopes = _per_head_lanes(jnp.asarray(_alibi_slopes()))
    w_in, w_out, wf_t = {}, {}, {}

    def lands_side_by_side(key):
        return key[0] == "in" and shards[key].shape[1] % LANES == 0

    def gather(keys):
        return _GatherExchange([shards[k] for k in keys], [lands_side_by_side(k) for k in keys])

    def deliver(keys, gathered):
        for key, g in zip(keys, gathered):
            side, layer = key
            sh = shards[key]
            if side == "out":
                g = lax.dynamic_update_slice(g, sh[None], (chip, 0, 0))
                w_out[layer] = g.reshape(4 * sh.shape[0], sh.shape[1])
            elif lands_side_by_side(key):
                w_in[layer] = _place_columns(g, sh, chip, f"own_block_in_l{layer}")
            else:
                g = lax.dynamic_update_slice(g, sh[None], (chip, 0, 0))
                w = g.transpose(1, 0, 2).reshape(sh.shape[0], 4 * sh.shape[1])
                w_in[layer], wf_t[layer] = w[:, :4 * BRANCH], w[:, 4 * BRANCH:].T

    deliver(GATHER_FIRST, _exchange_call(gather(GATHER_FIRST), "gather_first_weights"))
    saved = []
    for i in range(DEPTH):
        kind, j = _layer_kind(i)
        tag = f"l{i}"
        w = w_in[i]
        nqkv = A_QKV if kind == 0 else B_QKV
        tn = 512 if kind == 0 else 1024
        h, h_t = _rmsnorm_fwd(x, g_pre[i:i + 1], f"prenorm_{tag}")
        behind = GATHER_BEHIND.get(("qkv", i))
        qkv = _matmul(h, w, out_dtype=BF16, name=f"inproj_qkv_{tag}", n=nqkv, tn=tn,
                      exchange=gather(behind) if behind else None)
        if behind:
            qkv, arrived = qkv
            deliver(behind, arrived)
        z = _matmul(h, w, out_dtype=F32, name=f"inproj_gate_{tag}", n=BRANCH, b_off=nqkv // tn, tn=tn)
        behind = GATHER_BEHIND.get(("attn", i))
        exchange = gather(behind) if behind else None
        if kind == 0:
            sink_l = _per_head_lanes(sinks_a[j])
            (o, lse), arrived = _attn_a_fwd(qkv, slopes, sink_l, f"attn_a_fwd_{tag}", exchange)
            extra = (sink_l, lse)
        elif kind == 1:
            (o, extra), arrived = _attn_b_fwd(qkv, f"attn_b_fwd_{tag}", exchange)
        else:
            b_col = jnp.broadcast_to(b_f_c[j].astype(F32)[:, None], (N_HEADS, LANES))
            xf, cum = _fgate_fwd(h, wf_t[i], b_col, f"fgate_fwd_{tag}")
            cum4 = _to_cum4(cum, _fox_tile(s))
            (o, lse), arrived = _attn_c_fwd(qkv, cum4, f"attn_c_fwd_{tag}", exchange)
            extra = (xf, cum4, lse)
        if behind:
            deliver(behind, arrived)
        x_next, y, u_t = _gated_out_proj(o, z, w_out[i], x, g_post[i:i + 1], f"outproj_{tag}")
        saved.append((x, h, h_t, qkv, z, o, u_t, y, extra))
        x = x_next

    dx, loss_part = _loss_and_grad(x, target)

    d_g_pre, d_g_post = [None] * DEPTH, [None] * DEPTH
    d_sinks = [None, None]
    d_b_f = None
    reduced = {}
    pending = None

    def finish_reduce(layer, side, own, arr):
        kind, j = _layer_kind(layer)
        reduced[(side, kind)] = _sum_chips(own, arr, place, f"shard_sum_{side}_l{layer}", j, 2 if kind == 0 else 1,
                                           into=reduced.get((side, kind)))

    for i in reversed(range(DEPTH)):
        kind, j = _layer_kind(i)
        tag = f"l{i}"
        x_in, h, h_t, qkv, z, o, u_t, y, extra = saved[i]
        tn = 512 if kind == 0 else 1024
        (dy, d_g_post[i], do, dz), _ = _gated_out_proj_bwd(dx, y, g_post[i:i + 1], w_out[i], o, z, f"outproj_bwd_{tag}")
        dw_out = _matmul(u_t, dy, out_dtype=BF16, name=f"dw_out_{tag}")
        dw_out = dw_out.reshape(4, dw_out.shape[0] // 4, dw_out.shape[1])
        dh_f = None
        exchange = _SiblingExchange([dw_out])
        if pending:
            exchange = _BothExchanges(exchange, _ScatterExchange([pending[1]]))
        if kind == 0:
            sink_l, lse = extra
            (dq, dk, dv, dsk), arrived = _attn_a_bwd(qkv, slopes, sink_l, o, lse, do, f"attn_a_bwd_{tag}", exchange)
            d_sinks[j] = dsk[:, 0, ::LANES].reshape(N_HEADS)
            parts = [dq, dk.astype(BF16), dv.astype(BF16), dz]
        elif kind == 1:
            (dq, dk, dv), arrived = _attn_b_bwd(qkv, extra, do, f"attn_b_bwd_{tag}", exchange)
            parts = [dq, dk, dv, dz]
        else:
            xf, cum4, lse = extra
            (dq, dk, dv, dcum4), arrived = _attn_c_bwd(qkv, cum4, o, lse, do, f"attn_c_bwd_{tag}", exchange)
            d_wf_t, dh_f, db = _fgate_bwd(_from_cum4(dcum4), xf, h, wf_t[i], f"fgate_bwd_{tag}")
            d_b_f = db[:, 0]
            parts = [dq, dk, dv, dz]
        sum_out = _add_pairs(dw_out, arrived[0], place, f"chip_sum_out_{tag}")
        if pending:
            finish_reduce(pending[0], "in", pending[1], arrived[1])
        dproj = jnp.concatenate(parts, axis=1)
        scatter_out = _ScatterExchange([sum_out])
        if kind == 2:
            dw_in, arrived = _matmul(h_t, dproj, out_dtype=F32, name=f"dw_in_{tag}", tn=tn, exchange=scatter_out)
            dw_in = jnp.concatenate([dw_in, d_wf_t.T], axis=1)
            dw_in = dw_in.reshape(dw_in.shape[0], 4, dw_in.shape[1] // 4).transpose(1, 0, 2).astype(BF16)
        else:
            dw_in, arrived = _matmul(h_t, dproj, out_dtype=BF16, name=f"dw_in_{tag}", col_blocks=4,
                                     tn=1152 if kind == 0 else 1024, exchange=scatter_out)
        finish_reduce(i, "out", sum_out, arrived[0])
        (dx, d_g_pre[i]), (their_in,) = _in_proj_bwd(
            dproj, w_in[i], dh_f, dx, x_in, g_pre[i:i + 1], f"inproj_bwd_{tag}", 1536 if kind == 0 else 1024,
            exchange=_SiblingExchange([dw_in]))
        pending = (i, _add_pairs(dw_in, their_in, place, f"chip_sum_in_{tag}"))

    return dict(loss=loss_part, dx=dx, g_pre=jnp.concatenate(d_g_pre, axis=0), g_post=jnp.concatenate(d_g_post, axis=0),
                sinks_a=jnp.stack(d_sinks), b_f_c=d_b_f[None, :], reduced=reduced, last_sum=pending[1])


def _place():
    x, y, c = lax.axis_index("x"), lax.axis_index("y"), lax.axis_index("c")
    others = [(1 - x, y), (x, 1 - y), (1 - x, 1 - y)]
    return x, y, c, others


def _half_rows(ref_rows, which):
    half = ref_rows // 2
    return pl.ds(pl.multiple_of(which * half, half), half)


def _remote(src, dst, sems, k, device):
    send, recv = sems
    return pltpu.make_async_remote_copy(src_ref=src, dst_ref=dst, send_sem=send.at[k], recv_sem=recv.at[k],
                                        device_id=device, device_id_type=MESH)


def _hbm_call(body, name, ins, out_shapes, n_remote, aliases=None):
    any_spec = pl.BlockSpec(memory_space=pl.ANY)
    return pl.pallas_call(
        body, name=name, in_specs=[any_spec] * len(ins), out_specs=[any_spec] * len(out_shapes),
        out_shape=out_shapes, input_output_aliases=aliases or {},
        scratch_shapes=[pltpu.SemaphoreType.DMA((n_remote,)), pltpu.SemaphoreType.DMA((n_remote,))],
    )(*ins)


class _GatherExchange:
    SEMS = 8

    def __init__(self, shards, side_by_side):
        self.ins = list(shards)
        self.side_by_side = list(side_by_side)
        self.out_shapes = [jax.ShapeDtypeStruct((a.shape[0], 4 * a.shape[1]) if wide else (4,) + a.shape, a.dtype)
                           for a, wide in zip(shards, side_by_side)]
        self.n_sems = self.SEMS * len(shards)
        self.aliases = {}

    def _copies(self, ins, outs, sems):
        x, y, c, _ = _place()
        me, diag = 2 * x + y, 2 * (1 - x) + (1 - y)
        nbr = [((1 - x, y, c), 2 * (1 - x) + y), ((x, 1 - y, c), 2 * x + (1 - y))]
        sibling = (x, y, 1 - c)
        table = []
        for w, (src, dst, wide) in enumerate(zip(ins, outs, self.side_by_side)):
            rows, cols = src.shape
            half, quarter = rows // 2, rows // 4

            def slot(chip, core, piece=None, dst=dst, wide=wide, cols=cols, half=half, quarter=quarter):
                start, size = (core * half, half) if piece is None else (core * half + piece * quarter, quarter)
                which = pl.ds(pl.multiple_of(start, quarter), size)
                return dst.at[which, pl.ds(pl.multiple_of(chip * cols, LANES), cols)] if wide else dst.at[chip, which]

            k0 = self.SEMS * w
            cp = lambda s_, d_, k, dev: _remote(s_, d_, sems, k0 + k, dev)
            mine_src = src.at[pl.ds(pl.multiple_of(c * half, half), half)]
            d = dict(
                send=[cp(mine_src, slot(me, c), k, nbr[k][0]) for k in range(2)],
                got=[cp(slot(nbr[k][1], c), slot(nbr[k][1], c), k, nbr[k][0]) for k in range(2)],
                fwd=[cp(slot(nbr[k][1], c, k), slot(nbr[k][1], c, k), 2 + k, nbr[1 - k][0]) for k in range(2)],
                got_fwd=[cp(slot(diag, c, k), slot(diag, c, k), 2 + k, nbr[1 - k][0]) for k in range(2)],
                pass_=[cp(slot(nbr[k][1], c), slot(nbr[k][1], c), 4 + k, sibling) for k in range(2)]
                + [cp(slot(diag, c, k), slot(diag, c, k), 6 + k, sibling) for k in range(2)],
                got_pass=[cp(slot(nbr[k][1], 1 - c), slot(nbr[k][1], 1 - c), 4 + k, sibling) for k in range(2)]
                + [cp(slot(diag, 1 - c, k), slot(diag, 1 - c, k), 6 + k, sibling) for k in range(2)])
            table.append(d)
        return table

    def start(self, ins, outs, sems):
        for d in self._copies(ins, outs, sems):
            for cp in d["send"]:
                cp.start()

    def mid(self, ins, outs, sems):
        for d in self._copies(ins, outs, sems):
            for k in range(2):
                d["got"][k].wait_recv()
                d["fwd"][k].start()
                d["pass_"][k].start()

    def finish(self, ins, outs, sems):
        table = self._copies(ins, outs, sems)
        for d in table:
            for k in range(2):
                d["got_fwd"][k].wait_recv()
                d["pass_"][2 + k].start()
        for d in table:
            for cp in d["got_pass"]:
                cp.wait_recv()
            for cp in d["send"] + d["fwd"] + d["pass_"]:
                cp.wait_send()


class _SemaphoresFrom:
    def __init__(self, ref, start):
        self._ref, self._start = ref, start

    @property
    def at(self):
        return self

    def __getitem__(self, k):
        return self._ref.at[self._start + k]


class _BothExchanges:
    def __init__(self, first, second):
        self.parts = (first, second)
        self.ins = first.ins + second.ins
        self.out_shapes = first.out_shapes + second.out_shapes
        self.n_sems = first.n_sems + second.n_sems
        self.aliases = {}

    def _each(self, phase, ins, outs, sems):
        i0 = o0 = s0 = 0
        for ex in self.parts:
            n_in, n_out = len(ex.ins), len(ex.out_shapes)
            getattr(ex, phase)(ins[i0:i0 + n_in], outs[o0:o0 + n_out], tuple(_SemaphoresFrom(r, s0) for r in sems))
            i0, o0, s0 = i0 + n_in, o0 + n_out, s0 + ex.n_sems

    def start(self, ins, outs, sems):
        self._each("start", ins, outs, sems)

    def mid(self, ins, outs, sems):
        self._each("mid", ins, outs, sems)

    def finish(self, ins, outs, sems):
        self._each("finish", ins, outs, sems)


def _place_columns(wide, block, chip, name):
    rows, cc = block.shape
    tr = min(512, rows)

    def body(c_ref, b_ref, w_ref, o_ref):
        o_ref[...] = b_ref[...]

    return pl.pallas_call(
        body, name=name,
        grid_spec=pltpu.PrefetchScalarGridSpec(
            num_scalar_prefetch=1, grid=(rows // tr,),
            in_specs=[pl.BlockSpec((tr, cc), lambda r, c_ref: (r, 0)), pl.BlockSpec(memory_space=pl.ANY)],
            out_specs=pl.BlockSpec((tr, cc), lambda r, c_ref: (r, c_ref[0]))),
        out_shape=jax.ShapeDtypeStruct(wide.shape, wide.dtype), input_output_aliases={2: 0},
        compiler_params=_params(("parallel",)),
    )(chip.astype(jnp.int32).reshape(1), block, wide)


def _exchange_call(ex, name):
    n_in, n_out = len(ex.ins), len(ex.out_shapes)

    def body(*refs):
        ins, outs, sems = refs[:n_in], refs[n_in:n_in + n_out], refs[n_in + n_out:]
        ex.start(ins, outs, sems)
        ex.mid(ins, outs, sems)
        ex.finish(ins, outs, sems)

    return _hbm_call(body, name, ex.ins, ex.out_shapes, ex.n_sems, aliases=ex.aliases)


def _grid_call(body, *, name, grid, in_specs, out_specs, out_shape, args, scratch_shapes=(), semantics, exchange=None):
    if exchange is None:
        res = pl.pallas_call(body, name=name, grid=grid, in_specs=list(in_specs), out_specs=list(out_specs),
                             out_shape=list(out_shape), scratch_shapes=list(scratch_shapes),
                             compiler_params=_params(semantics))(*args)
        return res, []
    n_in, n_out, n_scr = len(args), len(out_shape), len(scratch_shapes)
    x_in, x_out = len(exchange.ins), len(exchange.out_shapes)
    steps = math.prod(grid)

    def wrapped(*refs):
        core_in, ex_in = refs[:n_in], refs[n_in:n_in + x_in]
        rest = refs[n_in + x_in:]
        core_out, ex_out = rest[:n_out], rest[n_out:n_out + x_out]
        scratch, sems = rest[n_out + x_out:n_out + x_out + n_scr], rest[n_out + x_out + n_scr:]
        step = 0
        for axis, extent in enumerate(grid):
            step = step * extent + pl.program_id(axis)

        @pl.when(step == 0)
        def _():
            exchange.start(ex_in, ex_out, sems)

        body(*core_in, *core_out, *scratch)

        @pl.when(step == max((3 * steps) // 4 - 1, 0))
        def _():
            exchange.mid(ex_in, ex_out, sems)

        @pl.when(step == steps - 1)
        def _():
            exchange.finish(ex_in, ex_out, sems)

    any_spec = pl.BlockSpec(memory_space=pl.ANY)
    res = pl.pallas_call(
        wrapped, name=name, grid=grid,
        in_specs=list(in_specs) + [any_spec] * x_in, out_specs=list(out_specs) + [any_spec] * x_out,
        out_shape=list(out_shape) + list(exchange.out_shapes),
        input_output_aliases={n_in + a: n_out + b for a, b in exchange.aliases.items()},
        scratch_shapes=list(scratch_shapes) + [pltpu.SemaphoreType.DMA((exchange.n_sems,)),
                                               pltpu.SemaphoreType.DMA((exchange.n_sems,))],
        compiler_params=_params(("arbitrary",) * len(grid)),
    )(*args, *exchange.ins)
    return res[:n_out], res[n_out:]


class _SiblingExchange:
    def __init__(self, parts):
        self.ins = list(parts)
        self.out_shapes = [jax.ShapeDtypeStruct((4, a.shape[1] // 2, a.shape[2]), a.dtype) for a in parts]
        self.n_sems = len(parts)
        self.aliases = {}

    def _copies(self, ins, outs, sems):
        x, y, c, _ = _place()
        return [_remote(src.at[:, _half_rows(src.shape[1], 1 - c)], dst, sems, w, (x, y, 1 - c))
                for w, (src, dst) in enumerate(zip(ins, outs))]

    def start(self, ins, outs, sems):
        for cp in self._copies(ins, outs, sems):
            cp.start()

    def mid(self, ins, outs, sems):
        pass

    def finish(self, ins, outs, sems):
        for cp in self._copies(ins, outs, sems):
            cp.wait_recv()
            cp.wait_send()


class _ScatterExchange:
    def __init__(self, sums):
        self.ins = list(sums)
        self.out_shapes = [jax.ShapeDtypeStruct(a.shape, a.dtype) for a in sums]
        self.n_sems = 3 * len(sums)
        self.aliases = {}

    def _copies(self, ins, outs, sems):
        x, y, c, others = _place()
        me = 2 * x + y
        table = []
        for w, (src, dst) in enumerate(zip(ins, outs)):
            for j, (px, py) in enumerate(others):
                there = 2 * px + py
                send = _remote(src.at[there], dst.at[me], sems, 3 * w + j, (px, py, c))
                landed = _remote(dst.at[there], dst.at[there], sems, 3 * w + j, (px, py, c))
                table.append((send, landed))
        return table

    def start(self, ins, outs, sems):
        for send, _ in self._copies(ins, outs, sems):
            send.start()

    def mid(self, ins, outs, sems):
        pass

    def finish(self, ins, outs, sems):
        table = self._copies(ins, outs, sems)
        for _, landed in table:
            landed.wait_recv()
        for send, _ in table:
            send.wait_send()


def _sibling_join(shards, name):
    n = len(shards)

    def body(*refs):
        ins, outs, sems = refs[:n], refs[n:2 * n], refs[2 * n:2 * n + 2]
        x, y, c, _ = _place()
        pend = []
        for w in range(n):
            rows = ins[w].shape[1]
            mine, theirs = _half_rows(rows, c), _half_rows(rows, 1 - c)
            cp = _remote(ins[w].at[:, mine], outs[w].at[:, mine], sems, w, (x, y, 1 - c))
            cp.start()
            pend.append((cp, _remote(ins[w].at[:, theirs], outs[w].at[:, theirs], sems, w, (x, y, 1 - c))))
        for cp, landed in pend:
            landed.wait_recv()
            cp.wait_send()

    out_shapes = [jax.ShapeDtypeStruct(a.shape, a.dtype) for a in shards]
    return _hbm_call(body, name, shards, out_shapes, n, aliases={w: w for w in range(n)})


SMALL_ROWS = 136


def _all_reduce_small(vec):
    def body(v_ref, o_ref, buf, send, recv, loc):
        x, y, c, _ = _place()
        me = 4 * x + 2 * y + c
        lc = pltpu.make_async_copy(v_ref, buf.at[me], loc.at[0])
        lc.start()
        cps = []
        for k in range(1, 8):
            fx, fy, fc = (k >> 2) & 1, (k >> 1) & 1, k & 1
            peer = (x ^ fx, y ^ fy, c ^ fc)
            cp = pltpu.make_async_remote_copy(src_ref=v_ref, dst_ref=buf.at[me], send_sem=send.at[k - 1],
                                              recv_sem=recv.at[k - 1], device_id=peer, device_id_type=MESH)
            cp.start()
            cps.append((cp, 4 * peer[0] + 2 * peer[1] + peer[2]))
        for k, (cp, src) in enumerate(cps):
            pltpu.make_async_remote_copy(src_ref=v_ref, dst_ref=buf.at[src], send_sem=send.at[k], recv_sem=recv.at[k],
                                         device_id=(x, y, c), device_id_type=MESH).wait_recv()
        for cp, _ in cps:
            cp.wait_send()
        lc.wait()
        total = buf[0]
        for k in range(1, 8):
            total = total + buf[k]
        o_ref[...] = total

    vm = pl.BlockSpec(memory_space=pltpu.VMEM)
    return pl.pallas_call(
        body, name="all_reduce_small", in_specs=[vm], out_specs=vm,
        out_shape=jax.ShapeDtypeStruct(vec.shape, F32),
        scratch_shapes=[pltpu.VMEM((8,) + vec.shape, F32), pltpu.SemaphoreType.DMA((7,)),
                        pltpu.SemaphoreType.DMA((7,)), pltpu.SemaphoreType.DMA((1,))],
    )(vec)


SUM_ROWS = 256


def _add_pairs(part, theirs, place, name):
    four, rh, cc = theirs.shape
    tr = min(SUM_ROWS, rh)
    halves = part.reshape(four, 2, rh, cc)

    def body(p_ref, a_ref, b_ref, o_ref):
        o_ref[0] = (a_ref[0, 0].astype(F32) + b_ref[0].astype(F32)).astype(o_ref.dtype)

    spec = pl.BlockSpec((1, tr, cc), lambda k, r, p_ref: (k, r, 0))
    return pl.pallas_call(
        body, name=name,
        grid_spec=pltpu.PrefetchScalarGridSpec(
            num_scalar_prefetch=1, grid=(four, rh // tr),
            in_specs=[pl.BlockSpec((1, 1, tr, cc), lambda k, r, p_ref: (k, p_ref[1], r, 0)), spec], out_specs=spec),
        out_shape=jax.ShapeDtypeStruct(theirs.shape, theirs.dtype),
        compiler_params=_params(("parallel", "parallel")),
    )(place, halves, theirs)


def _sum_chips(own, arrived, place, name, layer, n_layers, into=None):
    four, rh, cc = own.shape
    tr = min(SUM_ROWS, rh)
    nr = rh // tr

    def body(p_ref, own_ref, arr_ref, *rest):
        o_ref = rest[-1]
        x, y = lax.axis_index("x"), lax.axis_index("y")
        tot = own_ref[0].astype(F32)
        for px, py in ((1 - x, y), (x, 1 - y), (1 - x, 1 - y)):
            tot = tot + arr_ref[2 * px + py].astype(F32)
        o_ref[0] = tot

    in_specs = [pl.BlockSpec((1, tr, cc), lambda r, p_ref: (p_ref[0], r, 0)),
                pl.BlockSpec((4, tr, cc), lambda r, p_ref: (0, r, 0))]
    args, aliases = [place, own, arrived], {}
    if into is not None:
        in_specs.append(pl.BlockSpec(memory_space=pl.ANY))
        args.append(into)
        aliases = {3: 0}
    return pl.pallas_call(
        body, name=name,
        grid_spec=pltpu.PrefetchScalarGridSpec(
            num_scalar_prefetch=1, grid=(nr,), in_specs=in_specs,
            out_specs=pl.BlockSpec((1, tr, cc), lambda r, p_ref: (layer, p_ref[1] * nr + r, 0))),
        out_shape=jax.ShapeDtypeStruct((n_layers, 2 * rh, cc), F32), input_output_aliases=aliases,
        compiler_params=_params(("parallel",)),
    )(*args)


ADAM_ROWS = 256


def _adamw(w, g, m, v, name):
    shape = w.shape
    as3 = lambda a: a.reshape((-1,) + shape[-2:])
    layers, rows, cc = as3(w).shape
    by_rows = rows % min(ADAM_ROWS, rows) == 0
    tr, tc = (min(ADAM_ROWS, rows), cc) if by_rows else (rows, ADAM_ROWS)
    assert rows % tr == 0 and cc % tc == 0

    def body(w_ref, g_ref, m_ref, v_ref, d_ref, nm_ref, nv_ref):
        _adamw_update(w_ref, g_ref, m_ref, v_ref, d_ref, nm_ref, nv_ref)

    spec = pl.BlockSpec((1, tr, tc), (lambda l, i: (l, i, 0)) if by_rows else (lambda l, i: (l, 0, i)))
    sh = jax.ShapeDtypeStruct((layers, rows, cc), F32)
    outs = pl.pallas_call(
        body, name=name, grid=(layers, (rows // tr) * (cc // tc)), in_specs=[spec] * 4, out_specs=[spec] * 3,
        out_shape=[sh] * 3,
        compiler_params=_params(("parallel", "parallel")),
    )(as3(w), as3(g), as3(m), as3(v))
    return [o.reshape(shape) for o in outs]


def _adamw_update(w_ref, g_ref, m_ref, v_ref, d_ref, nm_ref, nv_ref):
    c1 = 1.0 - ADAM_B1 ** ADAM_STEP
    c2 = 1.0 - ADAM_B2 ** ADAM_STEP
    gv = g_ref[...]
    nm = ADAM_B1 * m_ref[...] + (1.0 - ADAM_B1) * gv
    nv = ADAM_B2 * v_ref[...] + (1.0 - ADAM_B2) * (gv * gv)
    nm_ref[...] = nm
    nv_ref[...] = nv
    d_ref[...] = -ADAM_LR * ((nm / c1) / (jnp.sqrt(nv / c2) + ADAM_EPS) + ADAM_WD * w_ref[...])


ADAM_MANY_STEPS = 16


def _adamw_many(quads, name, exchange=None):
    n = ADAM_MANY_STEPS
    specs = []
    for w, _, _, _ in quads:
        layers, rows, cc = w.shape
        if rows % (8 * n) == 0:
            specs.append(pl.BlockSpec((layers, rows // n, cc), lambda i: (0, i, 0)))
        else:
            assert cc % (LANES * n) == 0, (name, w.shape)
            specs.append(pl.BlockSpec((layers, rows, cc // n), lambda i: (0, 0, i)))

    def body(*refs):
        ins, outs = refs[:4 * len(quads)], refs[4 * len(quads):]
        for q in range(len(quads)):
            _adamw_update(*ins[4 * q:4 * q + 4], *outs[3 * q:3 * q + 3])

    res, arrived = _grid_call(
        body, name=name, grid=(n,), in_specs=[s for s in specs for _ in range(4)],
        out_specs=[s for s in specs for _ in range(3)],
        out_shape=[jax.ShapeDtypeStruct(w.shape, F32) for w, _, _, _ in quads for _ in range(3)],
        args=tuple(a for quad in quads for a in quad), semantics=("parallel",), exchange=exchange)
    return [list(res[3 * q:3 * q + 3]) for q in range(len(quads))], arrived


def _pack_small(g_pre, g_post, sinks_a, b_f_c, loss_row):
    pad = lambda a: jnp.pad(a.reshape(1, -1).astype(F32), ((0, 0), (0, LANES - a.size)))
    rows = [g_pre.astype(F32).reshape(-1, LANES), g_post.astype(F32).reshape(-1, LANES), pad(sinks_a), pad(b_f_c), loss_row]
    packed = jnp.concatenate(rows, axis=0)
    return jnp.pad(packed, ((0, SMALL_ROWS - packed.shape[0]), (0, 0)))


def _unpack_small(p):
    n = DEPTH * D_MODEL // LANES
    return (p[:n].reshape(DEPTH, D_MODEL), p[n:2 * n].reshape(DEPTH, D_MODEL), p[2 * n, :2 * N_HEADS].reshape(2, N_HEADS),
            p[2 * n + 1, :N_HEADS].reshape(1, N_HEADS), p[2 * n + 2, 0])


def kernel(x, g_pre, g_post, w_in_a, w_out_a, sinks_a, w_in_b, w_out_b, w_in_c, b_f_c, w_out_c, loss_target, m_g_pre, m_g_post, m_w_in_a, m_w_out_a, m_sinks_a, m_w_in_b, m_w_out_b, m_w_in_c, m_b_f_c, m_w_out_c, v_g_pre, v_g_post, v_w_in_a, v_w_out_a, v_sinks_a, v_w_in_b, v_w_out_b, v_w_in_c, v_b_f_c, v_w_out_c):
    big_w = [w_in_a, w_out_a, w_in_b, w_out_b, w_in_c, w_out_c]
    big_m = [m_w_in_a, m_w_out_a, m_w_in_b, m_w_out_b, m_w_in_c, m_w_out_c]
    big_v = [v_w_in_a, v_w_out_a, v_w_in_b, v_w_out_b, v_w_in_c, v_w_out_c]

    chip = 2 * lax.axis_index("x") + lax.axis_index("y")
    place = jnp.stack([chip, lax.axis_index("c")]).astype(jnp.int32)
    by_kind = {0: (w_in_a, w_out_a), 1: (w_in_b, w_out_b), 2: (w_in_c, w_out_c)}
    shards = {}
    for i in range(DEPTH):
        kind, j = _layer_kind(i)
        shards[("in", i)] = by_kind[kind][0][j].astype(BF16)
        shards[("out", i)] = by_kind[kind][1][j].astype(BF16)

    res = _forward_backward(x[0], loss_target[0], g_pre, g_post, sinks_a, b_f_c, shards, chip, place)
    reduced = res["reduced"]
    rest = [("out", 0), ("in", 1), ("out", 1), ("in", 2), ("out", 2)]
    grads = [None] + list(_sibling_join([reduced[k] for k in rest], "grad_sibling_join"))

    small = _unpack_small(_all_reduce_small(
        _pack_small(res["g_pre"], res["g_post"], res["sinks_a"], res["b_f_c"], res["loss"])))
    g_small, loss = small[:4], small[4]

    zero_row = jnp.zeros((1, LANES), F32)
    pk = lambda a: _pack_small(a[0], a[1], a[2], a[3], zero_row)
    sm = _adamw(pk([g_pre, g_post, sinks_a, b_f_c]), pk(g_small), pk([m_g_pre, m_g_post, m_sinks_a, m_b_f_c]),
                pk([v_g_pre, v_g_post, v_sinks_a, v_b_f_c]), "adamw_small")
    sm = [_unpack_small(a)[:4] for a in sm]
    turned = lambda a: jnp.swapaxes(a, 1, 2)
    g_c = lax.optimization_barrier(turned(grads[4]))
    grads[4] = turned(g_c)
    quads = [(turned(w), g_c, turned(m), turned(v)) if k == 4 else (w, grads[k], m, v)
             for k, (w, m, v) in enumerate(zip(big_w, big_m, big_v)) if k > 0]
    bigs, arrived = _adamw_many(quads, "adamw_rest", exchange=_ScatterExchange([res["last_sum"]]))
    bigs[3] = [turned(o) for o in bigs[3]]
    half = _sum_chips(res["last_sum"], arrived[0], place, "shard_sum_in_l0", 0, 2, into=reduced[("in", 0)])
    grads[0] = _sibling_join([half], "grad_sibling_join_w_in_a")[0]
    bigs = [_adamw(w_in_a, grads[0], m_w_in_a, v_w_in_a, "adamw_w_in_a")] + bigs

    def ordered(small4, big6):
        return [small4[0], small4[1], big6[0], big6[1], small4[2], big6[2], big6[3], big6[4], small4[3], big6[5]]

    out = [loss, res["dx"][None], *ordered(g_small, grads)]
    for k in range(3):
        out += ordered(sm[k], [b[k] for b in bigs])
    return tuple(out)
```

```python
import functools
import math

import numpy as np
import jax
import jax.numpy as jnp
from jax import lax
from jax.experimental import pallas as pl
from jax.experimental.pallas import tpu as pltpu

F32 = jnp.float32
BF16 = jnp.bfloat16

D_MODEL = 2048
DEPTH = 4
N_HEADS = 32
HEAD_DIM = 64
LANES = 128
N_PAIRS = N_HEADS * HEAD_DIM // LANES
BRANCH = N_HEADS * HEAD_DIM
N_KV_A = 4
KV_A = N_KV_A * HEAD_DIM
WINDOW = 128
NORM_EPS = 1e-6
NEG = -1e30
Q_SCALE = HEAD_DIM ** -0.5

A_QKV = BRANCH + 2 * KV_A
B_QKV = 3 * BRANCH

ADAM_LR = 0.001
ADAM_B1 = 0.9
ADAM_B2 = 0.999
ADAM_EPS = 1e-08
ADAM_WD = 0.01
ADAM_STEP = 10

MESH = pl.DeviceIdType.MESH

_NT = (((1,), (1,)), ((), ()))
_TN = (((0,), (0,)), ((), ()))


def _params(sem=None):
    return pltpu.CompilerParams(dimension_semantics=sem)


def _matmul(a, b, *, out_dtype, name, n=None, b_off=0, tm=1024, tn=1024, col_blocks=None, exchange=None):
    (m, k), nn = a.shape, (n or b.shape[1])
    tm, tn = min(tm, m), min(tn, nn)
    assert m % tm == 0 and nn % tn == 0, (name, m, nn, tm, tn)

    def body(a_ref, b_ref, o_ref):
        p = jnp.dot(a_ref[...], b_ref[...], preferred_element_type=F32)
        o_ref[...] = p.astype(o_ref.dtype).reshape(o_ref.shape)

    in_specs = [pl.BlockSpec((tm, k), lambda i, j: (i, 0)), pl.BlockSpec((k, tn), lambda i, j: (0, j + b_off))]
    if col_blocks is None:
        out_spec = pl.BlockSpec((tm, tn), lambda i, j: (i, j))
        out_shape = jax.ShapeDtypeStruct((m, nn), out_dtype)
    else:
        per = nn // col_blocks // tn
        assert per * tn * col_blocks == nn, (name, nn, tn, col_blocks)
        out_spec = pl.BlockSpec((1, tm, tn), lambda i, j: (j // per, i, j % per))
        out_shape = jax.ShapeDtypeStruct((col_blocks, m, nn // col_blocks), out_dtype)
    (res,), arrived = _grid_call(
        body, name=name, grid=(m // tm, nn // tn), in_specs=in_specs, out_specs=[out_spec], out_shape=[out_shape],
        args=(a, b), semantics=("parallel", "parallel"), exchange=exchange)
    return res if exchange is None else (res, arrived)


ROW_TILE = 256


def _row_call(body, name, ins, outs, *, s):
    tr = min(ROW_TILE, s)
    spec = {"row": lambda sh: pl.BlockSpec((tr, sh[1]), lambda i: (i, 0)),
            "vec": lambda sh: pl.BlockSpec((1, sh[1]), lambda i: (0, 0)),
            "col": lambda sh: pl.BlockSpec((sh[0], tr), lambda i: (0, i))}
    in_specs = [spec[kind](a.shape) for a, kind in ins]
    out_specs = [spec[kind](sh.shape) for sh, kind in outs]
    return pl.pallas_call(
        body, name=name, grid=(s // tr,), in_specs=in_specs, out_specs=out_specs,
        out_shape=[sh for sh, _ in outs],
        compiler_params=_params(("arbitrary",)),
    )(*[a for a, _ in ins])


def _rsqrt_ms(v):
    return lax.rsqrt(jnp.mean(v * v, axis=-1, keepdims=True) + NORM_EPS)


def _rmsnorm_fwd(x, g, name):
    s, d = x.shape

    def body(x_ref, g_ref, h_ref, ht_ref):
        xv = x_ref[...]
        h = xv * _rsqrt_ms(xv) * g_ref[...]
        h_ref[...] = h.astype(BF16)
        ht_ref[...] = h.T.astype(BF16)

    return _row_call(body, name, [(x, "row"), (g, "vec")],
                     [(jax.ShapeDtypeStruct((s, d), BF16), "row"), (jax.ShapeDtypeStruct((d, s), BF16), "col")], s=s)


PROJ_ROWS = 256


def _resident(shape):
    return pl.BlockSpec(shape, lambda i: (0,) * len(shape), pipeline_mode=pl.Buffered(1))


def _gated_out_proj(o, z, w_out, x, g, name):
    s, d = x.shape
    tm = min(PROJ_ROWS, s)

    def body(o_ref, z_ref, w_ref, x_ref, g_ref, xn_ref, y_ref, ut_ref):
        zv = z_ref[...]
        u = o_ref[...] * (zv * jax.nn.sigmoid(zv))
        ut_ref[...] = u.T.astype(BF16)
        y = jnp.dot(u.astype(BF16), w_ref[...], preferred_element_type=F32)
        y_ref[...] = y
        xn_ref[...] = x_ref[...] + y * _rsqrt_ms(y) * g_ref[...]

    row = pl.BlockSpec((tm, d), lambda i: (i, 0))
    return pl.pallas_call(
        body, name=name, grid=(s // tm,),
        in_specs=[row, row, _resident(w_out.shape), row, _resident((1, d))],
        out_specs=[row, row, pl.BlockSpec((d, tm), lambda i: (0, i))],
        out_shape=[jax.ShapeDtypeStruct((s, d), F32), jax.ShapeDtypeStruct((s, d), F32), jax.ShapeDtypeStruct((d, s), BF16)],
        compiler_params=_params(("parallel",)),
    )(o, z, w_out, x, g)


def _gated_out_proj_bwd(dx, y, g, w_out, o, z, name, exchange=None):
    s, d = dx.shape
    tm = min(PROJ_ROWS, s)

    def body(dx_ref, y_ref, g_ref, w_ref, o_ref, z_ref, dy_ref, dg_ref, do_ref, dz_ref):
        dy, dg = _norm_bwd_rows(dx_ref[...], y_ref[...], g_ref[...])
        dyb = dy.astype(BF16)
        dy_ref[...] = dyb

        @pl.when(pl.program_id(0) == 0)
        def _():
            dg_ref[...] = jnp.zeros_like(dg_ref)

        dg_ref[...] += jnp.sum(dg, axis=0, keepdims=True)
        du = lax.dot_general(dyb, w_ref[...], _NT, preferred_element_type=F32)
        zv = z_ref[...]
        sig = jax.nn.sigmoid(zv)
        do_ref[...] = (du * (zv * sig)).astype(BF16)
        dz_ref[...] = (du * o_ref[...] * (sig * (1.0 + zv * (1.0 - sig)))).astype(BF16)

    row = pl.BlockSpec((tm, d), lambda i: (i, 0))
    vec = pl.BlockSpec((1, d), lambda i: (0, 0))
    bf = jax.ShapeDtypeStruct((s, d), BF16)
    return _grid_call(
        body, name=name, grid=(s // tm,),
        in_specs=[row, row, _resident((1, d)), _resident(w_out.shape), row, row],
        out_specs=[row, vec, row, row], out_shape=[bf, jax.ShapeDtypeStruct((1, d), F32), bf, bf],
        args=(dx, y, g, w_out, o, z), semantics=("arbitrary",), exchange=exchange)


IN_BWD_ROWS = 512


def _in_proj_bwd(dproj, w_in, extra, dx, x, g, name, tk, exchange=None):
    s, d = x.shape
    k = dproj.shape[1]
    tm = min(IN_BWD_ROWS, s)
    nk = k // tk
    assert k % tk == 0 and s % tm == 0, (name, k, tk)
    has_extra = extra is not None

    def body(a_ref, b_ref, *rest):
        if has_extra:
            e_ref, rest = rest[0], rest[1:]
        dx_ref, x_ref, g_ref, o_ref, dg_ref, acc_ref = rest
        i, kk = pl.program_id(0), pl.program_id(1)
        p = lax.dot_general(a_ref[...], b_ref[...], _NT, preferred_element_type=F32)

        @pl.when(kk == 0)
        def _():
            acc_ref[...] = p

        @pl.when(kk > 0)
        def _():
            acc_ref[...] += p

        @pl.when((i == 0) & (kk == 0))
        def _():
            dg_ref[...] = jnp.zeros_like(dg_ref)

        @pl.when(kk == nk - 1)
        def _():
            def rows_chunk(c, _):
                r = pl.ds(pl.multiple_of(c * LANES, LANES), LANES)
                dh = acc_ref[r, :] + e_ref[r, :] if has_extra else acc_ref[r, :]
                dv, dg = _norm_bwd_rows(dh, x_ref[r, :], g_ref[...])
                o_ref[r, :] = dx_ref[r, :] + dv
                dg_ref[...] += jnp.sum(dg, axis=0, keepdims=True)
                return 0

            lax.fori_loop(0, tm // LANES, rows_chunk, 0)

    row = pl.BlockSpec((tm, d), lambda i, kk: (i, 0))
    vec = pl.BlockSpec((1, d), lambda i, kk: (0, 0))
    in_specs = [pl.BlockSpec((tm, tk), lambda i, kk: (i, kk)), pl.BlockSpec((d, tk), lambda i, kk: (0, kk))]
    args = [dproj, w_in]
    if has_extra:
        in_specs.append(row)
        args.append(extra)
    return _grid_call(
        body, name=name, grid=(s // tm, nk), in_specs=in_specs + [row, row, vec], out_specs=[row, vec],
        out_shape=[jax.ShapeDtypeStruct((s, d), F32), jax.ShapeDtypeStruct((1, d), F32)],
        args=tuple(args) + (dx, x, g), scratch_shapes=[pltpu.VMEM((tm, d), F32)], semantics=("arbitrary", "arbitrary"),
        exchange=exchange)


def _loss_and_grad(x, target):
    s, d = x.shape

    def body(x_ref, t_ref, dx_ref, l_ref):
        err = x_ref[...] - t_ref[...]
        dx_ref[...] = err * (1.0 / d)
        part = jnp.sum(jnp.sum(err * err, axis=1, keepdims=True), axis=0, keepdims=True) * (0.5 / d)

        @pl.when(pl.program_id(0) == 0)
        def _():
            l_ref[...] = jnp.zeros_like(l_ref)

        l_ref[...] += jnp.broadcast_to(part, l_ref.shape)

    return _row_call(body, "loss_head", [(x, "row"), (target, "row")],
                     [(jax.ShapeDtypeStruct((s, d), F32), "row"),
                      (jax.ShapeDtypeStruct((1, LANES), F32), "vec")], s=s)


def _norm_bwd_rows(dn, v, g):
    r = _rsqrt_ms(v)
    a = dn * g
    dv = r * (a - v * (r * r) * jnp.mean(a * v, axis=-1, keepdims=True))
    return dv, dn * v * r


def _lane_is_first_head():
    return lax.broadcasted_iota(jnp.int32, (1, LANES), 1) < HEAD_DIM


def _bcast_lanes(col):
    return jnp.broadcast_to(col, (col.shape[0], LANES))


def _pair_spec(s, off=0, width=LANES):
    return pl.BlockSpec((s, width), lambda p: (0, p + off))


def _stack_heads(pair, first):
    return jnp.concatenate([jnp.where(first, pair, 0), jnp.where(first, 0, pair)], axis=0).astype(BF16)


def _stacked_mask(t, strict):
    row = lax.broadcasted_iota(jnp.int32, (2 * t, t), 0)
    col = lax.broadcasted_iota(jnp.int32, (2 * t, t), 1)
    query = jnp.where(row >= t, row - t, row)
    return col < query if strict else col <= query


def _steps_in_groups(n, step, carry, widths=(2, 1)):
    done = 0
    for width in widths:
        def group(jj, c, width=width, done=done):
            for k in range(width):
                c = step(done + width * jj + k, c)
            return c

        trips = (n - done) // width
        carry = lax.fori_loop(0, trips, group, carry)
        done = done + width * trips
    return carry


FULL_ATTENTION_WIDTHS = (4, 2, 1)


def _rowsum_heads(prod, first):
    return (jnp.sum(jnp.where(first, prod, 0.0), axis=1, keepdims=True),
            jnp.sum(jnp.where(first, 0.0, prod), axis=1, keepdims=True))


def _softplus_parts(z):
    e = jnp.exp(-jnp.abs(z))
    sp = jnp.maximum(z, 0.0) + jnp.log(1.0 + e)
    r = 1.0 / (1.0 + e)
    return sp, jnp.where(z >= 0, r, e * r)


def _sb_tile(s):
    return min(256, s)


def _attn_b_fwd(qkv, name, exchange=None):
    s = qkv.shape[0]
    t = _sb_tile(s)
    nq = s // t

    def body(q_ref, k_ref, v_ref, o_ref, lt_ref):
        first = _lane_is_first_head()
        before = _stacked_mask(t, strict=True)
        tri = (lax.broadcasted_iota(jnp.int32, (t, t), 0) >= lax.broadcasted_iota(jnp.int32, (t, t), 1)).astype(BF16)

        def tile(j, carry, diag, qs):
            c, acc = carry
            c0 = pl.multiple_of(j * t, t)
            k2 = k_ref[pl.ds(c0, t), :]
            v2 = v_ref[pl.ds(c0, t), :]
            z = lax.dot_general(qs, k2, _NT, preferred_element_type=F32)
            sp, _ = _softplus_parts(z)
            lf = jnp.where(before, -sp, 0.0) if diag else -sp
            incl = jnp.dot(lf.astype(BF16), tri, preferred_element_type=F32)
            a = jnp.exp(z + c + incl)
            if diag:
                a = jnp.where(before, a, 0.0)
            pv = jnp.dot(a.astype(BF16), v2, preferred_element_type=F32)
            return c + incl[:, 0:1], acc + jnp.where(first, pv[:t], pv[t:])

        def qblock(i, _):
            r0 = pl.multiple_of(i * t, t)
            qs = _stack_heads(q_ref[pl.ds(r0, t), :] * Q_SCALE, first)
            carry = tile(i, (jnp.zeros((2 * t, 1), F32), jnp.zeros((t, LANES), F32)), True, qs)
            carry = _steps_in_groups(i, lambda j, c: tile(i - 1 - j, c, False, qs), carry, FULL_ATTENTION_WIDTHS)
            o_ref[pl.ds(r0, t), :] = carry[1]
            lt_ref[pl.ds(r0, t), 0:LANES] = _bcast_lanes(carry[0][:t])
            lt_ref[pl.ds(r0, t), LANES:2 * LANES] = _bcast_lanes(carry[0][t:])
            return 0

        lax.fori_loop(0, nq, qblock, 0)

    return _grid_call(
        body, name=name, grid=(N_PAIRS,),
        in_specs=[_pair_spec(s), _pair_spec(s, N_PAIRS), _pair_spec(s, 2 * N_PAIRS)],
        out_specs=[_pair_spec(s), _stat_spec(s)],
        out_shape=[jax.ShapeDtypeStruct((s, BRANCH), F32), jax.ShapeDtypeStruct((s, N_HEADS * LANES), F32)],
        args=(qkv, qkv, qkv), semantics=("parallel",), exchange=exchange)


def _attn_b_bwd(qkv, ltot, do, name, exchange=None):
    s = qkv.shape[0]
    t = _sb_tile(s)
    nq = s // t

    def body(q_ref, k_ref, v_ref, lt_ref, do_ref, dq_ref, dk_ref, dv_ref, dk_acc, dv_acc):
        first = _lane_is_first_head()
        before = _stacked_mask(t, strict=True)
        tri = (lax.broadcasted_iota(jnp.int32, (t, t), 0) <= lax.broadcasted_iota(jnp.int32, (t, t), 1)).astype(BF16)
        dk_acc[...] = jnp.zeros_like(dk_acc)
        dv_acc[...] = jnp.zeros_like(dv_acc)

        def tile(j, carry, diag, qs, dos, lt):
            p_l, p_g, dq_acc = carry
            c0 = pl.multiple_of(j * t, t)
            k2 = k_ref[pl.ds(c0, t), :]
            v2 = v_ref[pl.ds(c0, t), :]
            z = lax.dot_general(qs, k2, _NT, preferred_element_type=F32)
            sp, sig = _softplus_parts(z)
            lf = jnp.where(before, -sp, 0.0) if diag else -sp
            pref_l = jnp.dot(lf.astype(BF16), tri, preferred_element_type=F32)
            a = jnp.exp(z + ((lt - p_l) - pref_l + lf))
            if diag:
                a = jnp.where(before, a, 0.0)
            g = a * lax.dot_general(dos, v2, _NT, preferred_element_type=F32)
            pref_g = jnp.dot(g.astype(BF16), tri, preferred_element_type=F32)
            dz = g - sig * (p_g + pref_g)
            if diag:
                dz = jnp.where(before, dz, 0.0)
            dzb = dz.astype(BF16)
            dq = jnp.dot(dzb, k2, preferred_element_type=F32)
            dk_acc[pl.ds(c0, t), :] += lax.dot_general(dzb, qs, _TN, preferred_element_type=F32)
            dv_acc[pl.ds(c0, t), :] += lax.dot_general(a.astype(BF16), dos, _TN, preferred_element_type=F32)
            return p_l + pref_l[:, t - 1:t], p_g + pref_g[:, t - 1:t], dq_acc + jnp.where(first, dq[:t], dq[t:])

        def qblock(i, _):
            r0 = pl.multiple_of(i * t, t)
            qs = _stack_heads(q_ref[pl.ds(r0, t), :] * Q_SCALE, first)
            dos = _stack_heads(do_ref[pl.ds(r0, t), :], first)
            lt = jnp.concatenate([lt_ref[pl.ds(r0, t), 0:1], lt_ref[pl.ds(r0, t), LANES:LANES + 1]], axis=0)
            zero = jnp.zeros((2 * t, 1), F32)
            carry = (zero, zero, jnp.zeros((t, LANES), F32))
            carry = _steps_in_groups(i, lambda j, c: tile(j, c, False, qs, dos, lt), carry, FULL_ATTENTION_WIDTHS)
            carry = tile(i, carry, True, qs, dos, lt)
            dq_ref[pl.ds(r0, t), :] = (carry[2] * Q_SCALE).astype(BF16)
            return 0

        lax.fori_loop(0, nq, qblock, 0)
        dk_ref[...] = dk_acc[...].astype(BF16)
        dv_ref[...] = dv_acc[...].astype(BF16)

    out = jax.ShapeDtypeStruct((s, BRANCH), BF16)
    return _grid_call(
        body, name=name, grid=(N_PAIRS,),
        in_specs=[_pair_spec(s), _pair_spec(s, N_PAIRS), _pair_spec(s, 2 * N_PAIRS), _stat_spec(s), _pair_spec(s)],
        out_specs=[_pair_spec(s)] * 3, out_shape=[out] * 3,
        scratch_shapes=[pltpu.VMEM((s, LANES), F32), pltpu.VMEM((s, LANES), F32)],
        args=(qkv, qkv, qkv, ltot, do), semantics=("parallel",), exchange=exchange)


def _fox_tile(s):
    return min(256, s)


def _stat_spec(s):
    return pl.BlockSpec((s, 2 * LANES), lambda p: (0, p))


def _cum_spec(nt, t):
    return pl.BlockSpec((1, nt, 2, t), lambda p: (p, 0, 0, 0))


def _attn_c_fwd(qkv, cum4, name, exchange=None):
    s = qkv.shape[0]
    t = _fox_tile(s)
    nq = s // t

    def body(q_ref, k_ref, v_ref, c_ref, o_ref, lse_ref):
        first = _lane_is_first_head()
        causal = _stacked_mask(t, strict=False)

        def tile(j, carry, diag, qs):
            c0 = pl.multiple_of(j * t, t)
            k2 = k_ref[pl.ds(c0, t), :]
            v2 = v_ref[pl.ds(c0, t), :]
            cs = c_ref[0, j]
            m_prev, l_prev, acc = carry
            z = lax.dot_general(qs, k2, _NT, preferred_element_type=F32)
            sc = jnp.concatenate([z[:t] - cs[0:1, :], z[t:] - cs[1:2, :]], axis=0)
            if diag:
                sc = jnp.where(causal, sc, NEG)
            m_new = jnp.maximum(m_prev, jnp.max(sc, axis=1, keepdims=True))
            alpha = jnp.exp(m_prev - m_new)
            p = jnp.exp(sc - m_new)
            l_new = alpha * l_prev + jnp.sum(p, axis=1, keepdims=True)
            pv = jnp.dot(p.astype(BF16), v2, preferred_element_type=F32)
            acc = jnp.where(first, acc * alpha[:t] + pv[:t], acc * alpha[t:] + pv[t:])
            return m_new, l_new, acc

        def qblock(i, _):
            r0 = pl.multiple_of(i * t, t)
            qs = _stack_heads(q_ref[pl.ds(r0, t), :] * Q_SCALE, first)
            carry = (jnp.full((2 * t, 1), NEG, F32), jnp.zeros((2 * t, 1), F32), jnp.zeros((t, LANES), F32))
            carry = _steps_in_groups(i, lambda j, c: tile(j, c, False, qs), carry, FULL_ATTENTION_WIDTHS)
            m, l, acc = tile(i, carry, True, qs)
            inv = 1.0 / l
            lse = m + jnp.log(l)
            o_ref[pl.ds(r0, t), :] = acc * jnp.where(first, inv[:t], inv[t:])
            lse_ref[pl.ds(r0, t), 0:LANES] = _bcast_lanes(lse[:t])
            lse_ref[pl.ds(r0, t), LANES:2 * LANES] = _bcast_lanes(lse[t:])
            return 0

        lax.fori_loop(0, nq, qblock, 0)

    return _grid_call(
        body, name=name, grid=(N_PAIRS,),
        in_specs=[_pair_spec(s), _pair_spec(s, N_PAIRS), _pair_spec(s, 2 * N_PAIRS), _cum_spec(nq, t)],
        out_specs=[_pair_spec(s), _stat_spec(s)],
        out_shape=[jax.ShapeDtypeStruct((s, BRANCH), F32), jax.ShapeDtypeStruct((s, N_HEADS * LANES), F32)],
        args=(qkv, qkv, qkv, cum4), semantics=("parallel",), exchange=exchange)


def _attn_c_bwd(qkv, cum4, o, lse, do, name, exchange=None):
    s = qkv.shape[0]
    t = _fox_tile(s)
    nq = s // t

    def body(q_ref, k_ref, v_ref, c_ref, o_ref, lse_ref, do_ref, dq_ref, dk_ref, dv_ref, dc_ref, dk_acc, dv_acc):
        first = _lane_is_first_head()
        causal = _stacked_mask(t, strict=False)
        eye = lax.broadcasted_iota(jnp.int32, (t, t), 0) == lax.broadcasted_iota(jnp.int32, (t, t), 1)
        dk_acc[...] = jnp.zeros_like(dk_acc)
        dv_acc[...] = jnp.zeros_like(dv_acc)
        dc_ref[...] = jnp.zeros_like(dc_ref)

        def tile(j, carry, diag, qs, dos, delta, lse):
            dq_acc, rs = carry
            c0 = pl.multiple_of(j * t, t)
            k2 = k_ref[pl.ds(c0, t), :]
            v2 = v_ref[pl.ds(c0, t), :]
            cs = c_ref[0, j]
            z = lax.dot_general(qs, k2, _NT, preferred_element_type=F32)
            sc = jnp.concatenate([z[:t] - cs[0:1, :], z[t:] - cs[1:2, :]], axis=0)
            p = jnp.exp(sc - lse)
            if diag:
                p = jnp.where(causal, p, 0.0)
            ds = p * (lax.dot_general(dos, v2, _NT, preferred_element_type=F32) - delta)
            dsb = ds.astype(BF16)
            dq = jnp.dot(dsb, k2, preferred_element_type=F32)
            dk_acc[pl.ds(c0, t), :] += lax.dot_general(dsb, qs, _TN, preferred_element_type=F32)
            dv_acc[pl.ds(c0, t), :] += lax.dot_general(p.astype(BF16), dos, _TN, preferred_element_type=F32)
            col_sums = jnp.concatenate([jnp.sum(ds[:t], axis=0, keepdims=True), jnp.sum(ds[t:], axis=0, keepdims=True)], axis=0)
            dc_ref[0, j] = dc_ref[0, j] - col_sums
            return dq_acc + jnp.where(first, dq[:t], dq[t:]), rs + jnp.sum(ds, axis=1, keepdims=True)

        def qblock(i, _):
            r0 = pl.multiple_of(i * t, t)
            do2 = do_ref[pl.ds(r0, t), :]
            qs = _stack_heads(q_ref[pl.ds(r0, t), :] * Q_SCALE, first)
            dos = _stack_heads(do2, first)
            delta = jnp.concatenate(_rowsum_heads(do2.astype(F32) * o_ref[pl.ds(r0, t), :], first), axis=0)
            lse = jnp.concatenate([lse_ref[pl.ds(r0, t), 0:1], lse_ref[pl.ds(r0, t), LANES:LANES + 1]], axis=0)
            carry = (jnp.zeros((t, LANES), F32), jnp.zeros((2 * t, 1), F32))
            carry = _steps_in_groups(i, lambda j, c: tile(j, c, False, qs, dos, delta, lse), carry, FULL_ATTENTION_WIDTHS)
            dq_acc, rs = tile(i, carry, True, qs, dos, delta, lse)
            dq_ref[pl.ds(r0, t), :] = (dq_acc * Q_SCALE).astype(BF16)
            as_row = lambda col_vec: jnp.sum(jnp.where(eye, col_vec, 0.0), axis=0, keepdims=True)
            dc_ref[0, i] = dc_ref[0, i] + jnp.concatenate([as_row(rs[:t]), as_row(rs[t:])], axis=0)
            return 0

        lax.fori_loop(0, nq, qblock, 0)
        dk_ref[...] = dk_acc[...].astype(BF16)
        dv_ref[...] = dv_acc[...].astype(BF16)

    out = jax.ShapeDtypeStruct((s, BRANCH), BF16)
    return _grid_call(
        body, name=name, grid=(N_PAIRS,),
        in_specs=[_pair_spec(s), _pair_spec(s, N_PAIRS), _pair_spec(s, 2 * N_PAIRS), _cum_spec(nq, t),
                  _pair_spec(s), _stat_spec(s), _pair_spec(s)],
        out_specs=[_pair_spec(s)] * 3 + [_cum_spec(nq, t)],
        out_shape=[out] * 3 + [jax.ShapeDtypeStruct(cum4.shape, F32)],
        scratch_shapes=[pltpu.VMEM((s, LANES), F32), pltpu.VMEM((s, LANES), F32)],
        args=(qkv, qkv, qkv, cum4, o, lse, do), semantics=("parallel",), exchange=exchange)


FG_CHUNK = 512


def _tri_dot3(x, t):
    hi = x.astype(BF16)
    r1 = x - hi.astype(F32)
    mid = r1.astype(BF16)
    lo = (r1 - mid.astype(F32)).astype(BF16)
    return (jnp.dot(hi, t, preferred_element_type=F32) + jnp.dot(mid, t, preferred_element_type=F32)
            + jnp.dot(lo, t, preferred_element_type=F32))


def _fgate_fwd(h, wf_t, b_col, name):
    s = h.shape[0]
    c = min(FG_CHUNK, s)

    def body(h_ref, w_ref, b_ref, xf_ref, cum_ref, carry_ref):
        @pl.when(pl.program_id(0) == 0)
        def _():
            carry_ref[...] = jnp.zeros_like(carry_ref)

        xf = lax.dot_general(w_ref[...], h_ref[...], _NT, preferred_element_type=F32) + b_ref[:, 0:1]
        xf_ref[...] = xf
        logf = jnp.minimum(xf, 0.0) - jnp.log(1.0 + jnp.exp(-jnp.abs(xf)))
        row = lax.broadcasted_iota(jnp.int32, (c, c), 0)
        col = lax.broadcasted_iota(jnp.int32, (c, c), 1)
        cum = _tri_dot3(logf, (row <= col).astype(BF16)) + carry_ref[:, 0:1]
        cum_ref[...] = cum
        carry_ref[...] = _bcast_lanes(cum[:, c - 1:c])

    out = jax.ShapeDtypeStruct((N_HEADS, s), F32)
    return pl.pallas_call(
        body, name=name, grid=(s // c,),
        in_specs=[pl.BlockSpec((c, D_MODEL), lambda i: (i, 0)),
                  pl.BlockSpec((N_HEADS, D_MODEL), lambda i: (0, 0)),
                  pl.BlockSpec((N_HEADS, LANES), lambda i: (0, 0))],
        out_specs=[pl.BlockSpec((N_HEADS, c), lambda i: (0, i))] * 2,
        out_shape=[out, out],
        scratch_shapes=[pltpu.VMEM((N_HEADS, LANES), F32)],
        compiler_params=_params(("arbitrary",)),
    )(h, wf_t, b_col)


def _fgate_bwd(dcum, xf, h, wf_t, name):
    s = h.shape[0]
    c = min(FG_CHUNK, s)
    n = s // c

    def body(dc_ref, xf_ref, h_ref, w_ref, dw_ref, dh_ref, db_ref, carry_ref):
        @pl.when(pl.program_id(0) == 0)
        def _():
            carry_ref[...] = jnp.zeros_like(carry_ref)
            dw_ref[...] = jnp.zeros_like(dw_ref)
            db_ref[...] = jnp.zeros_like(db_ref)

        row = lax.broadcasted_iota(jnp.int32, (c, c), 0)
        col = lax.broadcasted_iota(jnp.int32, (c, c), 1)
        dlogf = _tri_dot3(dc_ref[...], (row >= col).astype(BF16)) + carry_ref[:, 0:1]
        carry_ref[...] = _bcast_lanes(dlogf[:, 0:1])
        xf = xf_ref[...]
        e = jnp.exp(-jnp.abs(xf))
        r = 1.0 / (1.0 + e)
        dxf = dlogf * jnp.where(xf >= 0, e * r, r)
        db_ref[...] += _bcast_lanes(jnp.sum(dxf, axis=1, keepdims=True))
        dxb = dxf.astype(BF16)
        dw_ref[...] += jnp.dot(dxb, h_ref[...], preferred_element_type=F32)
        dh_ref[...] = lax.dot_general(dxb, w_ref[...], _TN, preferred_element_type=F32)

    rev = lambda i: n - 1 - i
    return pl.pallas_call(
        body, name=name, grid=(n,),
        in_specs=[pl.BlockSpec((N_HEADS, c), lambda i: (0, rev(i))),
                  pl.BlockSpec((N_HEADS, c), lambda i: (0, rev(i))),
                  pl.BlockSpec((c, D_MODEL), lambda i: (rev(i), 0)),
                  pl.BlockSpec((N_HEADS, D_MODEL), lambda i: (0, 0))],
        out_specs=[pl.BlockSpec((N_HEADS, D_MODEL), lambda i: (0, 0)),
                   pl.BlockSpec((c, D_MODEL), lambda i: (rev(i), 0)),
                   pl.BlockSpec((N_HEADS, LANES), lambda i: (0, 0))],
        out_shape=[jax.ShapeDtypeStruct((N_HEADS, D_MODEL), F32), jax.ShapeDtypeStruct((s, D_MODEL), F32),
                   jax.ShapeDtypeStruct((N_HEADS, LANES), F32)],
        scratch_shapes=[pltpu.VMEM((N_HEADS, LANES), F32)],
        compiler_params=_params(("arbitrary",)),
    )(dcum, xf, h, wf_t)


def _to_cum4(v, t):
    s = v.shape[1]
    return v.reshape(N_PAIRS, 2, s // t, t).transpose(0, 2, 1, 3)


def _from_cum4(v4):
    p, nt, two, t = v4.shape
    return v4.transpose(0, 2, 1, 3).reshape(p * two, nt * t)


def _alibi_slopes():
    return (2.0 ** (-8.0 * np.arange(1, N_HEADS + 1, dtype=np.float32) / N_HEADS)).astype(np.float32)


def _per_head_lanes(v):
    return jnp.repeat(v.astype(F32).reshape(N_PAIRS, 1, 2), LANES, axis=2)


def _attn_a_specs(s):
    q = _pair_spec(s)
    k = pl.BlockSpec((s, LANES), lambda p: (0, N_PAIRS + p // 8))
    v = pl.BlockSpec((s, LANES), lambda p: (0, N_PAIRS + KV_A // LANES + p // 8))
    head = pl.BlockSpec((1, 1, 2 * LANES), lambda p: (p, 0, 0))
    return q, k, v, head


def _attn_a_geometry(p, slope_ref, sink_ref):
    kv_half = (p // 4) % 2
    kv_first = kv_half == 0
    lane_first = _lane_is_first_head()
    kv_lanes = (lax.broadcasted_iota(jnp.int32, (1, LANES), 1) // HEAD_DIM) == kv_half
    row = lax.broadcasted_iota(jnp.int32, (2 * WINDOW, 2 * WINDOW), 0)
    cj = lax.broadcasted_iota(jnp.int32, (2 * WINDOW, 2 * WINDOW), 1)
    second = row >= WINDOW
    dist = WINDOW + jnp.where(second, row - WINDOW, row) - cj
    valid = (dist >= 0) & (dist < WINDOW)
    per_row = lambda ref: jnp.where(second[:, 0:1], ref[0, :, LANES:LANES + 1], ref[0, :, 0:1])
    return kv_first, lane_first, kv_lanes, per_row(slope_ref) * dist.astype(F32), valid, per_row(sink_ref)


def _swap_halves(x):
    return pltpu.roll(x, HEAD_DIM, 1)


def _attn_a_fwd(qkv, slopes, sinks, name, exchange=None):
    s = qkv.shape[0]
    nb = s // WINDOW

    def body(q_ref, k_ref, v_ref, sl_ref, sk_ref, o_ref, lse_ref):
        kv_first, lane_first, kv_lanes, bias, valid, sink = _attn_a_geometry(pl.program_id(0), sl_ref, sk_ref)

        def block(r0, k0, width):
            q2 = q_ref[pl.ds(r0, WINDOW), :].astype(F32) * Q_SCALE
            q2r = _swap_halves(q2)
            xs = jnp.concatenate([jnp.where(kv_first, q2, q2r), jnp.where(kv_first, q2r, q2)], axis=0).astype(BF16)
            km = jnp.where(kv_lanes, k_ref[pl.ds(k0, width), :], 0).astype(BF16)
            vm = jnp.where(kv_lanes, v_ref[pl.ds(k0, width), :], 0).astype(BF16)
            sc = lax.dot_general(xs, km, _NT, preferred_element_type=F32) - bias[:, 2 * WINDOW - width:]
            sc = jnp.where(valid[:, 2 * WINDOW - width:], sc, NEG)
            m = jnp.maximum(jnp.max(sc, axis=1, keepdims=True), sink)
            pr = jnp.exp(sc - m)
            l = jnp.sum(pr, axis=1, keepdims=True) + jnp.exp(sink - m)
            os = jnp.dot(pr.astype(BF16), vm, preferred_element_type=F32) * (1.0 / l)
            lse = m + jnp.log(l)
            lse_ref[pl.ds(r0, WINDOW), 0:LANES] = _bcast_lanes(lse[:WINDOW])
            lse_ref[pl.ds(r0, WINDOW), LANES:2 * LANES] = _bcast_lanes(lse[WINDOW:])
            oa = jnp.where(kv_first, os[:WINDOW], _swap_halves(os[:WINDOW]))
            ob = jnp.where(kv_first, _swap_halves(os[WINDOW:]), os[WINDOW:])
            o_ref[pl.ds(r0, WINDOW), :] = jnp.where(lane_first, oa, ob)

        block(0, 0, WINDOW)

        def loop(n, _):
            r0 = pl.multiple_of(n * WINDOW, WINDOW)
            block(r0, pl.multiple_of(r0 - WINDOW, WINDOW), 2 * WINDOW)
            return 0

        _steps_in_groups(nb - 1, lambda n, c: loop(n + 1, c), 0)

    q, k, v, head = _attn_a_specs(s)
    return _grid_call(
        body, name=name, grid=(N_PAIRS,),
        in_specs=[q, k, v, head, head],
        out_specs=[_pair_spec(s), _stat_spec(s)],
        out_shape=[jax.ShapeDtypeStruct((s, BRANCH), F32), jax.ShapeDtypeStruct((s, N_HEADS * LANES), F32)],
        args=(qkv, qkv, qkv, slopes, sinks), semantics=("parallel",), exchange=exchange)


def _attn_a_bwd(qkv, slopes, sinks, o, lse, do, name, exchange=None):
    s = qkv.shape[0]
    nb = s // WINDOW

    def body(q_ref, k_ref, v_ref, sl_ref, sk_ref, o_ref, lse_ref, do_ref, dq_ref, dk_ref, dv_ref, dsk_ref):
        p_id = pl.program_id(0)
        kv_first, lane_first, kv_lanes, bias, valid, sink = _attn_a_geometry(p_id, sl_ref, sk_ref)

        @pl.when(p_id % 8 == 0)
        def _():
            dk_ref[...] = jnp.zeros_like(dk_ref)
            dv_ref[...] = jnp.zeros_like(dv_ref)

        def align(v2):
            v2r = _swap_halves(v2)
            both = jnp.concatenate([jnp.where(kv_first, v2, v2r), jnp.where(kv_first, v2r, v2)], axis=0)
            return jnp.where(kv_lanes, both, 0.0).astype(BF16)

        def block(r0, k0, width, sink_sum):
            xq = align(q_ref[pl.ds(r0, WINDOW), :].astype(F32) * Q_SCALE)
            do2 = do_ref[pl.ds(r0, WINDOW), :].astype(F32)
            xdo = align(do2)
            delta = jnp.concatenate(_rowsum_heads(do2 * o_ref[pl.ds(r0, WINDOW), :], lane_first), axis=0)
            lse = jnp.concatenate([lse_ref[pl.ds(r0, WINDOW), 0:1], lse_ref[pl.ds(r0, WINDOW), LANES:LANES + 1]], axis=0)
            km = jnp.where(kv_lanes, k_ref[pl.ds(k0, width), :], 0).astype(BF16)
            vm = jnp.where(kv_lanes, v_ref[pl.ds(k0, width), :], 0).astype(BF16)
            sc = lax.dot_general(xq, km, _NT, preferred_element_type=F32) - bias[:, 2 * WINDOW - width:]
            pr = jnp.where(valid[:, 2 * WINDOW - width:], jnp.exp(sc - lse), 0.0)
            ds = pr * (lax.dot_general(xdo, vm, _NT, preferred_element_type=F32) - delta)
            dsb = ds.astype(BF16)
            dq_al = jnp.dot(dsb, km, preferred_element_type=F32)
            dk_ref[pl.ds(k0, width), :] += lax.dot_general(dsb, xq, _TN, preferred_element_type=F32)
            dv_ref[pl.ds(k0, width), :] += lax.dot_general(pr.astype(BF16), xdo, _TN, preferred_element_type=F32)
            dqa = jnp.where(kv_first, dq_al[:WINDOW], _swap_halves(dq_al[:WINDOW]))
            dqb = jnp.where(kv_first, _swap_halves(dq_al[WINDOW:]), dq_al[WINDOW:])
            dq_ref[pl.ds(r0, WINDOW), :] = (jnp.where(lane_first, dqa, dqb) * Q_SCALE).astype(BF16)
            return sink_sum + jnp.exp(sink - lse) * delta

        sink_sum = block(0, 0, WINDOW, jnp.zeros((2 * WINDOW, 1), F32))

        def loop(n, c):
            r0 = pl.multiple_of(n * WINDOW, WINDOW)
            return block(r0, pl.multiple_of(r0 - WINDOW, WINDOW), 2 * WINDOW, c)

        sink_sum = _steps_in_groups(nb - 1, lambda n, c: loop(n + 1, c), sink_sum)
        dsk_ref[0, :, 0:LANES] = jnp.broadcast_to(-jnp.sum(sink_sum[:WINDOW], axis=0, keepdims=True), (1, LANES))
        dsk_ref[0, :, LANES:2 * LANES] = jnp.broadcast_to(-jnp.sum(sink_sum[WINDOW:], axis=0, keepdims=True), (1, LANES))

    q, k, v, head = _attn_a_specs(s)
    kv_out = pl.BlockSpec((s, LANES), lambda p: (0, p // 8))
    return _grid_call(
        body, name=name, grid=(N_PAIRS,),
        in_specs=[q, k, v, head, head, _pair_spec(s), _stat_spec(s), _pair_spec(s)],
        out_specs=[_pair_spec(s), kv_out, kv_out, head],
        out_shape=[jax.ShapeDtypeStruct((s, BRANCH), BF16), jax.ShapeDtypeStruct((s, KV_A), F32),
                   jax.ShapeDtypeStruct((s, KV_A), F32), jax.ShapeDtypeStruct((N_PAIRS, 1, 2 * LANES), F32)],
        args=(qkv, qkv, qkv, slopes, sinks, o, lse, do), semantics=("arbitrary",), exchange=exchange)


def _layer_kind(i):
    return i % 3, i // 3


GATHER_FIRST = [("in", 0)]
GATHER_BEHIND = {("qkv", 0): [("out", 0)], ("attn", 0): [("in", 1)], ("attn", 1): [("out", 1), ("in", 2), ("out", 2)],
                 ("attn", 2): [("in", 3), ("out", 3)]}


def _forward_backward(x, target, g_pre, g_post, sinks_a, b_f_c, shards, chip, place):
    s = x.shape[0]
    slopes = _per_head_lanes(jnp.asarray(_alibi_slopes()))
    w_in, w_out, wf_t = {}, {}, {}

    def lands_side_by_side(key):
        return key[0] == "in" and shards[key].shape[1] % LANES == 0

    def gather(keys):
        return _GatherExchange([shards[k] for k in keys], [lands_side_by_side(k) for k in keys])

    def deliver(keys, gathered):
        for key, g in zip(keys, gathered):
            side, layer = key
            sh = shards[key]
            if side == "out":
                g = lax.dynamic_update_slice(g, sh[None], (chip, 0, 0))
                w_out[layer] = g.reshape(4 * sh.shape[0], sh.shape[1])
            elif lands_side_by_side(key):
                w_in[layer] = _place_columns(g, sh, chip, f"own_block_in_l{layer}")
            else:
                g = lax.dynamic_update_slice(g, sh[None], (chip, 0, 0))
                w = g.transpose(1, 0, 2).reshape(sh.shape[0], 4 * sh.shape[1])
                w_in[layer], wf_t[layer] = lax.optimization_barrier((w[:, :4 * BRANCH], w[:, 4 * BRANCH:].T))

    deliver(GATHER_FIRST, _exchange_call(gather(GATHER_FIRST), "gather_first_weights"))
    saved = []
    for i in range(DEPTH):
        kind, j = _layer_kind(i)
        tag = f"l{i}"
        w = w_in[i]
        nqkv = A_QKV if kind == 0 else B_QKV
        tn = 512 if kind == 0 else 1024
        h, h_t = _rmsnorm_fwd(x, g_pre[i:i + 1], f"prenorm_{tag}")
        behind = GATHER_BEHIND.get(("qkv", i))
        qkv = _matmul(h, w, out_dtype=BF16, name=f"inproj_qkv_{tag}", n=nqkv, tn=tn,
                      exchange=gather(behind) if behind else None)
        if behind:
            qkv, arrived = qkv
            deliver(behind, arrived)
        z = _matmul(h, w, out_dtype=F32, name=f"inproj_gate_{tag}", n=BRANCH, b_off=nqkv // tn, tn=tn)
        behind = GATHER_BEHIND.get(("attn", i))
        exchange = gather(behind) if behind else None
        if kind == 0:
            sink_l = _per_head_lanes(sinks_a[j])
            (o, lse), arrived = _attn_a_fwd(qkv, slopes, sink_l, f"attn_a_fwd_{tag}", exchange)
            extra = (sink_l, lse)
        elif kind == 1:
            (o, extra), arrived = _attn_b_fwd(qkv, f"attn_b_fwd_{tag}", exchange)
        else:
            b_col = jnp.broadcast_to(b_f_c[j].astype(F32)[:, None], (N_HEADS, LANES))
            xf, cum = _fgate_fwd(h, wf_t[i], b_col, f"fgate_fwd_{tag}")
            cum4 = _to_cum4(cum, _fox_tile(s))
            (o, lse), arrived = _attn_c_fwd(qkv, cum4, f"attn_c_fwd_{tag}", exchange)
            extra = (xf, cum4, lse)
        if behind:
            deliver(behind, arrived)
        x_next, y, u_t = _gated_out_proj(o, z, w_out[i], x, g_post[i:i + 1], f"outproj_{tag}")
        saved.append((x, h, h_t, qkv, z, o, u_t, y, extra))
        x = x_next

    dx, loss_part = _loss_and_grad(x, target)

    d_g_pre, d_g_post = [None] * DEPTH, [None] * DEPTH
    d_sinks = [None, None]
    d_b_f = None
    reduced = {}
    pending = None

    def finish_reduce(layer, side, own, arr):
        kind, j = _layer_kind(layer)
        reduced[(side, kind)] = _sum_chips(own, arr, place, f"shard_sum_{side}_l{layer}", j, 2 if kind == 0 else 1,
                                           into=reduced.get((side, kind)))

    for i in reversed(range(DEPTH)):
        kind, j = _layer_kind(i)
        tag = f"l{i}"
        x_in, h, h_t, qkv, z, o, u_t, y, extra = saved[i]
        tn = 512 if kind == 0 else 1024
        (dy, d_g_post[i], do, dz), _ = _gated_out_proj_bwd(dx, y, g_post[i:i + 1], w_out[i], o, z, f"outproj_bwd_{tag}")
        dw_out = _matmul(u_t, dy, out_dtype=BF16, name=f"dw_out_{tag}")
        dw_out = dw_out.reshape(4, dw_out.shape[0] // 4, dw_out.shape[1])
        dh_f = None
        exchange = _SiblingExchange([dw_out])
        if pending:
            exchange = _BothExchanges(exchange, _ScatterExchange([pending[1]]))
        if kind == 0:
            sink_l, lse = extra
            (dq, dk, dv, dsk), arrived = _attn_a_bwd(qkv, slopes, sink_l, o, lse, do, f"attn_a_bwd_{tag}", exchange)
            d_sinks[j] = dsk[:, 0, ::LANES].reshape(N_HEADS)
            parts = [dq, dk.astype(BF16), dv.astype(BF16), dz]
        elif kind == 1:
            (dq, dk, dv), arrived = _attn_b_bwd(qkv, extra, do, f"attn_b_bwd_{tag}", exchange)
            parts = [dq, dk, dv, dz]
        else:
            xf, cum4, lse = extra
            (dq, dk, dv, dcum4), arrived = _attn_c_bwd(qkv, cum4, o, lse, do, f"attn_c_bwd_{tag}", exchange)
            d_wf_t, dh_f, db = _fgate_bwd(_from_cum4(dcum4), xf, h, wf_t[i], f"fgate_bwd_{tag}")
            d_b_f = db[:, 0]
            parts = [dq, dk, dv, dz]
        sum_out = _add_pairs(dw_out, arrived[0], place, f"chip_sum_out_{tag}")
        if pending:
            finish_reduce(pending[0], "in", pending[1], arrived[1])
        dproj = jnp.concatenate(parts, axis=1)
        scatter_out = _ScatterExchange([sum_out])
        if kind == 2:
            dw_in, arrived = _matmul(h_t, dproj, out_dtype=F32, name=f"dw_in_{tag}", tn=tn, exchange=scatter_out)
            dw_in = jnp.concatenate([dw_in, d_wf_t.T], axis=1)
            dw_in = dw_in.reshape(dw_in.shape[0], 4, dw_in.shape[1] // 4).transpose(1, 0, 2).astype(BF16)
        else:
            dw_in, arrived = _matmul(h_t, dproj, out_dtype=BF16, name=f"dw_in_{tag}", col_blocks=4,
                                     tn=1152 if kind == 0 else 1024, exchange=scatter_out)
        finish_reduce(i, "out", sum_out, arrived[0])
        (dx, d_g_pre[i]), (their_in,) = _in_proj_bwd(
            dproj, w_in[i], dh_f, dx, x_in, g_pre[i:i + 1], f"inproj_bwd_{tag}", 1536 if kind == 0 else 1024,
            exchange=_SiblingExchange([dw_in]))
        pending = (i, _add_pairs(dw_in, their_in, place, f"chip_sum_in_{tag}"))

    return dict(loss=loss_part, dx=dx, g_pre=jnp.concatenate(d_g_pre, axis=0), g_post=jnp.concatenate(d_g_post, axis=0),
                sinks_a=jnp.stack(d_sinks), b_f_c=d_b_f[None, :], reduced=reduced, last_sum=pending[1])


def _place():
    x, y, c = lax.axis_index("x"), lax.axis_index("y"), lax.axis_index("c")
    others = [(1 - x, y), (x, 1 - y), (1 - x, 1 - y)]
    return x, y, c, others


def _half_rows(ref_rows, which):
    half = ref_rows // 2
    return pl.ds(pl.multiple_of(which * half, half), half)


def _remote(src, dst, sems, k, device):
    send, recv = sems
    return pltpu.make_async_remote_copy(src_ref=src, dst_ref=dst, send_sem=send.at[k], recv_sem=recv.at[k],
                                        device_id=device, device_id_type=MESH)


def _hbm_call(body, name, ins, out_shapes, n_remote, aliases=None):
    any_spec = pl.BlockSpec(memory_space=pl.ANY)
    return pl.pallas_call(
        body, name=name, in_specs=[any_spec] * len(ins), out_specs=[any_spec] * len(out_shapes),
        out_shape=out_shapes, input_output_aliases=aliases or {},
        scratch_shapes=[pltpu.SemaphoreType.DMA((n_remote,)), pltpu.SemaphoreType.DMA((n_remote,))],
    )(*ins)


class _GatherExchange:
    SEMS = 8

    def __init__(self, shards, side_by_side):
        self.ins = list(shards)
        self.side_by_side = list(side_by_side)
        self.out_shapes = [jax.ShapeDtypeStruct((a.shape[0], 4 * a.shape[1]) if wide else (4,) + a.shape, a.dtype)
                           for a, wide in zip(shards, side_by_side)]
        self.n_sems = self.SEMS * len(shards)
        self.aliases = {}

    def _copies(self, ins, outs, sems):
        x, y, c, _ = _place()
        me, diag = 2 * x + y, 2 * (1 - x) + (1 - y)
        nbr = [((1 - x, y, c), 2 * (1 - x) + y), ((x, 1 - y, c), 2 * x + (1 - y))]
        sibling = (x, y, 1 - c)
        table = []
        for w, (src, dst, wide) in enumerate(zip(ins, outs, self.side_by_side)):
            rows, cols = src.shape
            half, quarter = rows // 2, rows // 4

            def slot(chip, core, piece=None, dst=dst, wide=wide, cols=cols, half=half, quarter=quarter):
                start, size = (core * half, half) if piece is None else (core * half + piece * quarter, quarter)
                which = pl.ds(pl.multiple_of(start, quarter), size)
                return dst.at[which, pl.ds(pl.multiple_of(chip * cols, LANES), cols)] if wide else dst.at[chip, which]

            k0 = self.SEMS * w
            cp = lambda s_, d_, k, dev: _remote(s_, d_, sems, k0 + k, dev)
            mine_src = src.at[pl.ds(pl.multiple_of(c * half, half), half)]
            d = dict(
                send=[cp(mine_src, slot(me, c), k, nbr[k][0]) for k in range(2)],
                got=[cp(slot(nbr[k][1], c), slot(nbr[k][1], c), k, nbr[k][0]) for k in range(2)],
                fwd=[cp(slot(nbr[k][1], c, k), slot(nbr[k][1], c, k), 2 + k, nbr[1 - k][0]) for k in range(2)],
                got_fwd=[cp(slot(diag, c, k), slot(diag, c, k), 2 + k, nbr[1 - k][0]) for k in range(2)],
                pass_=[cp(slot(nbr[k][1], c), slot(nbr[k][1], c), 4 + k, sibling) for k in range(2)]
                + [cp(slot(diag, c, k), slot(diag, c, k), 6 + k, sibling) for k in range(2)],
                got_pass=[cp(slot(nbr[k][1], 1 - c), slot(nbr[k][1], 1 - c), 4 + k, sibling) for k in range(2)]
                + [cp(slot(diag, 1 - c, k), slot(diag, 1 - c, k), 6 + k, sibling) for k in range(2)])
            table.append(d)
        return table

    def start(self, ins, outs, sems):
        for d in self._copies(ins, outs, sems):
            for cp in d["send"]:
                cp.start()

    def mid(self, ins, outs, sems):
        for d in self._copies(ins, outs, sems):
            for k in range(2):
                d["got"][k].wait_recv()
                d["fwd"][k].start()
                d["pass_"][k].start()

    def finish(self, ins, outs, sems):
        table = self._copies(ins, outs, sems)
        for d in table:
            for k in range(2):
                d["got_fwd"][k].wait_recv()
                d["pass_"][2 + k].start()
        for d in table:
            for cp in d["got_pass"]:
                cp.wait_recv()
            for cp in d["send"] + d["fwd"] + d["pass_"]:
                cp.wait_send()


class _SemaphoresFrom:
    def __init__(self, ref, start):
        self._ref, self._start = ref, start

    @property
    def at(self):
        return self

    def __getitem__(self, k):
        return self._ref.at[self._start + k]


class _BothExchanges:
    def __init__(self, first, second):
        self.parts = (first, second)
        self.ins = first.ins + second.ins
        self.out_shapes = first.out_shapes + second.out_shapes
        self.n_sems = first.n_sems + second.n_sems
        self.aliases = {}

    def _each(self, phase, ins, outs, sems):
        i0 = o0 = s0 = 0
        for ex in self.parts:
            n_in, n_out = len(ex.ins), len(ex.out_shapes)
            getattr(ex, phase)(ins[i0:i0 + n_in], outs[o0:o0 + n_out], tuple(_SemaphoresFrom(r, s0) for r in sems))
            i0, o0, s0 = i0 + n_in, o0 + n_out, s0 + ex.n_sems

    def start(self, ins, outs, sems):
        self._each("start", ins, outs, sems)

    def mid(self, ins, outs, sems):
        self._each("mid", ins, outs, sems)

    def finish(self, ins, outs, sems):
        self._each("finish", ins, outs, sems)


def _place_columns(wide, block, chip, name):
    rows, cc = block.shape
    tr = min(512, rows)

    def body(c_ref, b_ref, w_ref, o_ref):
        o_ref[...] = b_ref[...]

    return pl.pallas_call(
        body, name=name,
        grid_spec=pltpu.PrefetchScalarGridSpec(
            num_scalar_prefetch=1, grid=(rows // tr,),
            in_specs=[pl.BlockSpec((tr, cc), lambda r, c_ref: (r, 0)), pl.BlockSpec(memory_space=pl.ANY)],
            out_specs=pl.BlockSpec((tr, cc), lambda r, c_ref: (r, c_ref[0]))),
        out_shape=jax.ShapeDtypeStruct(wide.shape, wide.dtype), input_output_aliases={2: 0},
        compiler_params=_params(("parallel",)),
    )(chip.astype(jnp.int32).reshape(1), block, wide)


def _exchange_call(ex, name):
    n_in, n_out = len(ex.ins), len(ex.out_shapes)

    def body(*refs):
        ins, outs, sems = refs[:n_in], refs[n_in:n_in + n_out], refs[n_in + n_out:]
        ex.start(ins, outs, sems)
        ex.mid(ins, outs, sems)
        ex.finish(ins, outs, sems)

    return _hbm_call(body, name, ex.ins, ex.out_shapes, ex.n_sems, aliases=ex.aliases)


def _grid_call(body, *, name, grid, in_specs, out_specs, out_shape, args, scratch_shapes=(), semantics, exchange=None):
    if exchange is None:
        res = pl.pallas_call(body, name=name, grid=grid, in_specs=list(in_specs), out_specs=list(out_specs),
                             out_shape=list(out_shape), scratch_shapes=list(scratch_shapes),
                             compiler_params=_params(semantics))(*args)
        return res, []
    n_in, n_out, n_scr = len(args), len(out_shape), len(scratch_shapes)
    x_in, x_out = len(exchange.ins), len(exchange.out_shapes)
    steps = math.prod(grid)

    def wrapped(*refs):
        core_in, ex_in = refs[:n_in], refs[n_in:n_in + x_in]
        rest = refs[n_in + x_in:]
        core_out, ex_out = rest[:n_out], rest[n_out:n_out + x_out]
        scratch, sems = rest[n_out + x_out:n_out + x_out + n_scr], rest[n_out + x_out + n_scr:]
        step = 0
        for axis, extent in enumerate(grid):
            step = step * extent + pl.program_id(axis)

        @pl.when(step == 0)
        def _():
            exchange.start(ex_in, ex_out, sems)

        body(*core_in, *core_out, *scratch)

        @pl.when(step == max((3 * steps) // 4 - 1, 0))
        def _():
            exchange.mid(ex_in, ex_out, sems)

        @pl.when(step == steps - 1)
        def _():
            exchange.finish(ex_in, ex_out, sems)

    any_spec = pl.BlockSpec(memory_space=pl.ANY)
    res = pl.pallas_call(
        wrapped, name=name, grid=grid,
        in_specs=list(in_specs) + [any_spec] * x_in, out_specs=list(out_specs) + [any_spec] * x_out,
        out_shape=list(out_shape) + list(exchange.out_shapes),
        input_output_aliases={n_in + a: n_out + b for a, b in exchange.aliases.items()},
        scratch_shapes=list(scratch_shapes) + [pltpu.SemaphoreType.DMA((exchange.n_sems,)),
                                               pltpu.SemaphoreType.DMA((exchange.n_sems,))],
        compiler_params=_params(("arbitrary",) * len(grid)),
    )(*args, *exchange.ins)
    return res[:n_out], res[n_out:]


class _SiblingExchange:
    def __init__(self, parts):
        self.ins = list(parts)
        self.out_shapes = [jax.ShapeDtypeStruct((4, a.shape[1] // 2, a.shape[2]), a.dtype) for a in parts]
        self.n_sems = len(parts)
        self.aliases = {}

    def _copies(self, ins, outs, sems):
        x, y, c, _ = _place()
        return [_remote(src.at[:, _half_rows(src.shape[1], 1 - c)], dst, sems, w, (x, y, 1 - c))
                for w, (src, dst) in enumerate(zip(ins, outs))]

    def start(self, ins, outs, sems):
        for cp in self._copies(ins, outs, sems):
            cp.start()

    def mid(self, ins, outs, sems):
        pass

    def finish(self, ins, outs, sems):
        for cp in self._copies(ins, outs, sems):
            cp.wait_recv()
            cp.wait_send()


class _ScatterExchange:
    def __init__(self, sums):
        self.ins = list(sums)
        self.out_shapes = [jax.ShapeDtypeStruct(a.shape, a.dtype) for a in sums]
        self.n_sems = 3 * len(sums)
        self.aliases = {}

    def _copies(self, ins, outs, sems):
        x, y, c, others = _place()
        me = 2 * x + y
        table = []
        for w, (src, dst) in enumerate(zip(ins, outs)):
            for j, (px, py) in enumerate(others):
                there = 2 * px + py
                send = _remote(src.at[there], dst.at[me], sems, 3 * w + j, (px, py, c))
                landed = _remote(dst.at[there], dst.at[there], sems, 3 * w + j, (px, py, c))
                table.append((send, landed))
        return table

    def start(self, ins, outs, sems):
        for send, _ in self._copies(ins, outs, sems):
            send.start()

    def mid(self, ins, outs, sems):
        pass

    def finish(self, ins, outs, sems):
        table = self._copies(ins, outs, sems)
        for _, landed in table:
            landed.wait_recv()
        for send, _ in table:
            send.wait_send()


def _sibling_join(shards, name):
    n = len(shards)

    def body(*refs):
        ins, outs, sems = refs[:n], refs[n:2 * n], refs[2 * n:2 * n + 2]
        x, y, c, _ = _place()
        pend = []
        for w in range(n):
            rows = ins[w].shape[1]
            mine, theirs = _half_rows(rows, c), _half_rows(rows, 1 - c)
            cp = _remote(ins[w].at[:, mine], outs[w].at[:, mine], sems, w, (x, y, 1 - c))
            cp.start()
            pend.append((cp, _remote(ins[w].at[:, theirs], outs[w].at[:, theirs], sems, w, (x, y, 1 - c))))
        for cp, landed in pend:
            landed.wait_recv()
            cp.wait_send()

    out_shapes = [jax.ShapeDtypeStruct(a.shape, a.dtype) for a in shards]
    return _hbm_call(body, name, shards, out_shapes, n, aliases={w: w for w in range(n)})


SMALL_ROWS = 136


def _all_reduce_small(vec):
    def body(v_ref, o_ref, buf, send, recv, loc):
        x, y, c, _ = _place()
        me = 4 * x + 2 * y + c
        lc = pltpu.make_async_copy(v_ref, buf.at[me], loc.at[0])
        lc.start()
        cps = []
        for k in range(1, 8):
            fx, fy, fc = (k >> 2) & 1, (k >> 1) & 1, k & 1
            peer = (x ^ fx, y ^ fy, c ^ fc)
            cp = pltpu.make_async_remote_copy(src_ref=v_ref, dst_ref=buf.at[me], send_sem=send.at[k - 1],
                                              recv_sem=recv.at[k - 1], device_id=peer, device_id_type=MESH)
            cp.start()
            cps.append((cp, 4 * peer[0] + 2 * peer[1] + peer[2]))
        for k, (cp, src) in enumerate(cps):
            pltpu.make_async_remote_copy(src_ref=v_ref, dst_ref=buf.at[src], send_sem=send.at[k], recv_sem=recv.at[k],
                                         device_id=(x, y, c), device_id_type=MESH).wait_recv()
        for cp, _ in cps:
            cp.wait_send()
        lc.wait()
        total = buf[0]
        for k in range(1, 8):
            total = total + buf[k]
        o_ref[...] = total

    vm = pl.BlockSpec(memory_space=pltpu.VMEM)
    return pl.pallas_call(
        body, name="all_reduce_small", in_specs=[vm], out_specs=vm,
        out_shape=jax.ShapeDtypeStruct(vec.shape, F32),
        scratch_shapes=[pltpu.VMEM((8,) + vec.shape, F32), pltpu.SemaphoreType.DMA((7,)),
                        pltpu.SemaphoreType.DMA((7,)), pltpu.SemaphoreType.DMA((1,))],
    )(vec)


SUM_ROWS = 256


def _add_pairs(part, theirs, place, name):
    four, rh, cc = theirs.shape
    tr = min(SUM_ROWS, rh)
    halves = part.reshape(four, 2, rh, cc)

    def body(p_ref, a_ref, b_ref, o_ref):
        o_ref[0] = (a_ref[0, 0].astype(F32) + b_ref[0].astype(F32)).astype(o_ref.dtype)

    spec = pl.BlockSpec((1, tr, cc), lambda k, r, p_ref: (k, r, 0))
    return pl.pallas_call(
        body, name=name,
        grid_spec=pltpu.PrefetchScalarGridSpec(
            num_scalar_prefetch=1, grid=(four, rh // tr),
            in_specs=[pl.BlockSpec((1, 1, tr, cc), lambda k, r, p_ref: (k, p_ref[1], r, 0)), spec], out_specs=spec),
        out_shape=jax.ShapeDtypeStruct(theirs.shape, theirs.dtype),
        compiler_params=_params(("parallel", "parallel")),
    )(place, halves, theirs)


def _sum_chips(own, arrived, place, name, layer, n_layers, into=None):
    four, rh, cc = own.shape
    tr = min(SUM_ROWS, rh)
    nr = rh // tr

    def body(p_ref, own_ref, arr_ref, *rest):
        o_ref = rest[-1]
        x, y = lax.axis_index("x"), lax.axis_index("y")
        tot = own_ref[0].astype(F32)
        for px, py in ((1 - x, y), (x, 1 - y), (1 - x, 1 - y)):
            tot = tot + arr_ref[2 * px + py].astype(F32)
        o_ref[0] = tot

    in_specs = [pl.BlockSpec((1, tr, cc), lambda r, p_ref: (p_ref[0], r, 0)),
                pl.BlockSpec((4, tr, cc), lambda r, p_ref: (0, r, 0))]
    args, aliases = [place, own, arrived], {}
    if into is not None:
        in_specs.append(pl.BlockSpec(memory_space=pl.ANY))
        args.append(into)
        aliases = {3: 0}
    return pl.pallas_call(
        body, name=name,
        grid_spec=pltpu.PrefetchScalarGridSpec(
            num_scalar_prefetch=1, grid=(nr,), in_specs=in_specs,
            out_specs=pl.BlockSpec((1, tr, cc), lambda r, p_ref: (layer, p_ref[1] * nr + r, 0))),
        out_shape=jax.ShapeDtypeStruct((n_layers, 2 * rh, cc), F32), input_output_aliases=aliases,
        compiler_params=_params(("parallel",)),
    )(*args)


ADAM_ROWS = 256


def _adamw(w, g, m, v, name):
    shape = w.shape
    as3 = lambda a: a.reshape((-1,) + shape[-2:])
    layers, rows, cc = as3(w).shape
    by_rows = rows % min(ADAM_ROWS, rows) == 0
    tr, tc = (min(ADAM_ROWS, rows), cc) if by_rows else (rows, ADAM_ROWS)
    assert rows % tr == 0 and cc % tc == 0

    def body(w_ref, g_ref, m_ref, v_ref, d_ref, nm_ref, nv_ref):
        _adamw_update(w_ref, g_ref, m_ref, v_ref, d_ref, nm_ref, nv_ref)

    spec = pl.BlockSpec((1, tr, tc), (lambda l, i: (l, i, 0)) if by_rows else (lambda l, i: (l, 0, i)))
    sh = jax.ShapeDtypeStruct((layers, rows, cc), F32)
    outs = pl.pallas_call(
        body, name=name, grid=(layers, (rows // tr) * (cc // tc)), in_specs=[spec] * 4, out_specs=[spec] * 3,
        out_shape=[sh] * 3,
        compiler_params=_params(("parallel", "parallel")),
    )(as3(w), as3(g), as3(m), as3(v))
    return [o.reshape(shape) for o in outs]


def _adamw_update(w_ref, g_ref, m_ref, v_ref, d_ref, nm_ref, nv_ref):
    c1 = 1.0 - ADAM_B1 ** ADAM_STEP
    c2 = 1.0 - ADAM_B2 ** ADAM_STEP
    gv = g_ref[...]
    nm = ADAM_B1 * m_ref[...] + (1.0 - ADAM_B1) * gv
    nv = ADAM_B2 * v_ref[...] + (1.0 - ADAM_B2) * (gv * gv)
    nm_ref[...] = nm
    nv_ref[...] = nv
    d_ref[...] = -ADAM_LR * ((nm / c1) / (jnp.sqrt(nv / c2) + ADAM_EPS) + ADAM_WD * w_ref[...])


ADAM_MANY_STEPS = 16


def _adamw_many(quads, name, exchange=None):
    n = ADAM_MANY_STEPS
    specs = []
    for w, _, _, _ in quads:
        layers, rows, cc = w.shape
        if rows % (8 * n) == 0:
            specs.append(pl.BlockSpec((layers, rows // n, cc), lambda i: (0, i, 0)))
        else:
            assert cc % (LANES * n) == 0, (name, w.shape)
            specs.append(pl.BlockSpec((layers, rows, cc // n), lambda i: (0, 0, i)))

    def body(*refs):
        ins, outs = refs[:4 * len(quads)], refs[4 * len(quads):]
        for q in range(len(quads)):
            _adamw_update(*ins[4 * q:4 * q + 4], *outs[3 * q:3 * q + 3])

    res, arrived = _grid_call(
        body, name=name, grid=(n,), in_specs=[s for s in specs for _ in range(4)],
        out_specs=[s for s in specs for _ in range(3)],
        out_shape=[jax.ShapeDtypeStruct(w.shape, F32) for w, _, _, _ in quads for _ in range(3)],
        args=tuple(a for quad in quads for a in quad), semantics=("parallel",), exchange=exchange)
    return [list(res[3 * q:3 * q + 3]) for q in range(len(quads))], arrived


def _pack_small(g_pre, g_post, sinks_a, b_f_c, loss_row):
    pad = lambda a: jnp.pad(a.reshape(1, -1).astype(F32), ((0, 0), (0, LANES - a.size)))
    rows = [g_pre.astype(F32).reshape(-1, LANES), g_post.astype(F32).reshape(-1, LANES), pad(sinks_a), pad(b_f_c), loss_row]
    packed = jnp.concatenate(rows, axis=0)
    return jnp.pad(packed, ((0, SMALL_ROWS - packed.shape[0]), (0, 0)))


def _unpack_small(p):
    n = DEPTH * D_MODEL // LANES
    return (p[:n].reshape(DEPTH, D_MODEL), p[n:2 * n].reshape(DEPTH, D_MODEL), p[2 * n, :2 * N_HEADS].reshape(2, N_HEADS),
            p[2 * n + 1, :N_HEADS].reshape(1, N_HEADS), p[2 * n + 2, 0])


def kernel(x, g_pre, g_post, w_in_a, w_out_a, sinks_a, w_in_b, w_out_b, w_in_c, b_f_c, w_out_c, loss_target, m_g_pre, m_g_post, m_w_in_a, m_w_out_a, m_sinks_a, m_w_in_b, m_w_out_b, m_w_in_c, m_b_f_c, m_w_out_c, v_g_pre, v_g_post, v_w_in_a, v_w_out_a, v_sinks_a, v_w_in_b, v_w_out_b, v_w_in_c, v_b_f_c, v_w_out_c):
    big_w = [w_in_a, w_out_a, w_in_b, w_out_b, w_in_c, w_out_c]
    big_m = [m_w_in_a, m_w_out_a, m_w_in_b, m_w_out_b, m_w_in_c, m_w_out_c]
    big_v = [v_w_in_a, v_w_out_a, v_w_in_b, v_w_out_b, v_w_in_c, v_w_out_c]

    chip = 2 * lax.axis_index("x") + lax.axis_index("y")
    place = jnp.stack([chip, lax.axis_index("c")]).astype(jnp.int32)
    by_kind = {0: (w_in_a, w_out_a), 1: (w_in_b, w_out_b), 2: (w_in_c, w_out_c)}
    shards = {}
    for i in range(DEPTH):
        kind, j = _layer_kind(i)
        shards[("in", i)] = by_kind[kind][0][j].astype(BF16)
        shards[("out", i)] = by_kind[kind][1][j].astype(BF16)

    res = _forward_backward(x[0], loss_target[0], g_pre, g_post, sinks_a, b_f_c, shards, chip, place)
    reduced = res["reduced"]
    rest = [("out", 0), ("in", 1), ("out", 1), ("in", 2), ("out", 2)]
    grads = [None] + list(_sibling_join([reduced[k] for k in rest], "grad_sibling_join"))

    small = _unpack_small(_all_reduce_small(
        _pack_small(res["g_pre"], res["g_post"], res["sinks_a"], res["b_f_c"], res["loss"])))
    g_small, loss = small[:4], small[4]

    zero_row = jnp.zeros((1, LANES), F32)
    pk = lambda a: _pack_small(a[0], a[1], a[2], a[3], zero_row)
    sm = _adamw(pk([g_pre, g_post, sinks_a, b_f_c]), pk(g_small), pk([m_g_pre, m_g_post, m_sinks_a, m_b_f_c]),
                pk([v_g_pre, v_g_post, v_sinks_a, v_b_f_c]), "adamw_small")
    sm = [_unpack_small(a)[:4] for a in sm]
    turned = lambda a: jnp.swapaxes(a, 1, 2)
    g_c = lax.optimization_barrier(turned(grads[4]))
    grads[4] = turned(g_c)
    quads = [(turned(w), g_c, turned(m), turned(v)) if k == 4 else (w, grads[k], m, v)
             for k, (w, m, v) in enumerate(zip(big_w, big_m, big_v)) if k > 0]
    bigs, arrived = _adamw_many(quads, "adamw_rest", exchange=_ScatterExchange([res["last_sum"]]))
    bigs[3] = [turned(o) for o in bigs[3]]
    half = _sum_chips(res["last_sum"], arrived[0], place, "shard_sum_in_l0", 0, 2, into=reduced[("in", 0)])
    grads[0] = _sibling_join([half], "grad_sibling_join_w_in_a")[0]
    bigs = [_adamw(w_in_a, grads[0], m_w_in_a, v_w_in_a, "adamw_w_in_a")] + bigs

    def ordered(small4, big6):
        return [small4[0], small4[1], big6[0], big6[1], small4[2], big6[2], big6[3], big6[4], small4[3], big6[5]]

    out = [loss, res["dx"][None], *ordered(g_small, grads)]
    for k in range(3):
        out += ordered(sm[k], [b[k] for b in bigs])
    return tuple(out)
```

```python
import math

import numpy as np
import jax
import jax.numpy as jnp
from jax import lax
from jax.experimental import pallas as pl
from jax.experimental.pallas import tpu as pltpu

F32 = jnp.float32
BF16 = jnp.bfloat16

D_MODEL = 2048
DEPTH = 4
N_HEADS = 32
HEAD_DIM = 64
LANES = 128
N_PAIRS = N_HEADS * HEAD_DIM // LANES
BRANCH = N_HEADS * HEAD_DIM
N_KV_A = 4
KV_A = N_KV_A * HEAD_DIM
WINDOW = 128
NORM_EPS = 1e-6
NEG = -1e30
Q_SCALE = HEAD_DIM ** -0.5

A_QKV = BRANCH + 2 * KV_A
B_QKV = 3 * BRANCH

ADAM_LR = 0.001
ADAM_B1 = 0.9
ADAM_B2 = 0.999
ADAM_EPS = 1e-08
ADAM_WD = 0.01
ADAM_STEP = 10

MESH = pl.DeviceIdType.MESH

_NT = (((1,), (1,)), ((), ()))
_TN = (((0,), (0,)), ((), ()))


def _params(sem=None):
    return pltpu.CompilerParams(dimension_semantics=sem)


def _matmul(a, b, *, out_dtype, name, n=None, b_off=0, tm=1024, tn=1024, col_blocks=None, exchange=None):
    (m, k), nn = a.shape, (n or b.shape[1])
    tm, tn = min(tm, m), min(tn, nn)
    assert m % tm == 0 and nn % tn == 0, (name, m, nn, tm, tn)

    def body(a_ref, b_ref, o_ref):
        p = jnp.dot(a_ref[...], b_ref[...], preferred_element_type=F32)
        o_ref[...] = p.astype(o_ref.dtype).reshape(o_ref.shape)

    in_specs = [pl.BlockSpec((tm, k), lambda i, j: (i, 0)), pl.BlockSpec((k, tn), lambda i, j: (0, j + b_off))]
    if col_blocks is None:
        out_spec = pl.BlockSpec((tm, tn), lambda i, j: (i, j))
        out_shape = jax.ShapeDtypeStruct((m, nn), out_dtype)
    else:
        per = nn // col_blocks // tn
        assert per * tn * col_blocks == nn, (name, nn, tn, col_blocks)
        out_spec = pl.BlockSpec((1, tm, tn), lambda i, j: (j // per, i, j % per))
        out_shape = jax.ShapeDtypeStruct((col_blocks, m, nn // col_blocks), out_dtype)
    (res,), arrived = _grid_call(
        body, name=name, grid=(m // tm, nn // tn), in_specs=in_specs, out_specs=[out_spec], out_shape=[out_shape],
        args=(a, b), semantics=("parallel", "parallel"), exchange=exchange)
    return res if exchange is None else (res, arrived)


ROW_TILE = 256


def _row_call(body, name, ins, outs, *, s):
    tr = min(ROW_TILE, s)
    spec = {"row": lambda sh: pl.BlockSpec((tr, sh[1]), lambda i: (i, 0)),
            "vec": lambda sh: pl.BlockSpec((1, sh[1]), lambda i: (0, 0)),
            "col": lambda sh: pl.BlockSpec((sh[0], tr), lambda i: (0, i))}
    in_specs = [spec[kind](a.shape) for a, kind in ins]
    out_specs = [spec[kind](sh.shape) for sh, kind in outs]
    return pl.pallas_call(
        body, name=name, grid=(s // tr,), in_specs=in_specs, out_specs=out_specs,
        out_shape=[sh for sh, _ in outs],
        compiler_params=_params(("arbitrary",)),
    )(*[a for a, _ in ins])


def _rsqrt_ms(v):
    return lax.rsqrt(jnp.mean(v * v, axis=-1, keepdims=True) + NORM_EPS)


def _rmsnorm_fwd(x, g, name):
    s, d = x.shape

    def body(x_ref, g_ref, h_ref, ht_ref):
        xv = x_ref[...]
        h = xv * _rsqrt_ms(xv) * g_ref[...]
        h_ref[...] = h.astype(BF16)
        ht_ref[...] = h.T.astype(BF16)

    return _row_call(body, name, [(x, "row"), (g, "vec")],
                     [(jax.ShapeDtypeStruct((s, d), BF16), "row"), (jax.ShapeDtypeStruct((d, s), BF16), "col")], s=s)


PROJ_ROWS = 256


def _resident(shape):
    return pl.BlockSpec(shape, lambda i: (0,) * len(shape), pipeline_mode=pl.Buffered(1))


def _gated_out_proj(o, z, w_out, x, g, name):
    s, d = x.shape
    tm = min(PROJ_ROWS, s)

    def body(o_ref, z_ref, w_ref, x_ref, g_ref, xn_ref, y_ref, ut_ref):
        zv = z_ref[...]
        u = o_ref[...] * (zv * jax.nn.sigmoid(zv))
        ut_ref[...] = u.T.astype(BF16)
        y = jnp.dot(u.astype(BF16), w_ref[...], preferred_element_type=F32)
        y_ref[...] = y
        xn_ref[...] = x_ref[...] + y * _rsqrt_ms(y) * g_ref[...]

    row = pl.BlockSpec((tm, d), lambda i: (i, 0))
    return pl.pallas_call(
        body, name=name, grid=(s // tm,),
        in_specs=[row, row, _resident(w_out.shape), row, _resident((1, d))],
        out_specs=[row, row, pl.BlockSpec((d, tm), lambda i: (0, i))],
        out_shape=[jax.ShapeDtypeStruct((s, d), F32), jax.ShapeDtypeStruct((s, d), F32), jax.ShapeDtypeStruct((d, s), BF16)],
        compiler_params=_params(("parallel",)),
    )(o, z, w_out, x, g)


def _gated_out_proj_bwd(dx, y, g, w_out, o, z, name, exchange=None):
    s, d = dx.shape
    tm = min(PROJ_ROWS, s)

    def body(dx_ref, y_ref, g_ref, w_ref, o_ref, z_ref, dy_ref, dg_ref, do_ref, dz_ref):
        dy, dg = _norm_bwd_rows(dx_ref[...], y_ref[...], g_ref[...])
        dyb = dy.astype(BF16)
        dy_ref[...] = dyb

        @pl.when(pl.program_id(0) == 0)
        def _():
            dg_ref[...] = jnp.zeros_like(dg_ref)

        dg_ref[...] += jnp.sum(dg, axis=0, keepdims=True)
        du = lax.dot_general(dyb, w_ref[...], _NT, preferred_element_type=F32)
        zv = z_ref[...]
        sig = jax.nn.sigmoid(zv)
        do_ref[...] = (du * (zv * sig)).astype(BF16)
        dz_ref[...] = (du * o_ref[...] * (sig * (1.0 + zv * (1.0 - sig)))).astype(BF16)

    row = pl.BlockSpec((tm, d), lambda i: (i, 0))
    vec = pl.BlockSpec((1, d), lambda i: (0, 0))
    bf = jax.ShapeDtypeStruct((s, d), BF16)
    return _grid_call(
        body, name=name, grid=(s // tm,),
        in_specs=[row, row, _resident((1, d)), _resident(w_out.shape), row, row],
        out_specs=[row, vec, row, row], out_shape=[bf, jax.ShapeDtypeStruct((1, d), F32), bf, bf],
        args=(dx, y, g, w_out, o, z), semantics=("arbitrary",), exchange=exchange)


IN_BWD_ROWS = 512


def _in_proj_bwd(dproj, w_in, extra, dx, x, g, name, tk, exchange=None):
    s, d = x.shape
    k = dproj.shape[1]
    tm = min(IN_BWD_ROWS, s)
    nk = k // tk
    assert k % tk == 0 and s % tm == 0, (name, k, tk)
    has_extra = extra is not None

    def body(a_ref, b_ref, *rest):
        if has_extra:
            e_ref, rest = rest[0], rest[1:]
        dx_ref, x_ref, g_ref, o_ref, dg_ref, acc_ref = rest
        i, kk = pl.program_id(0), pl.program_id(1)
        p = lax.dot_general(a_ref[...], b_ref[...], _NT, preferred_element_type=F32)

        @pl.when(kk == 0)
        def _():
            acc_ref[...] = p

        @pl.when(kk > 0)
        def _():
            acc_ref[...] += p

        @pl.when((i == 0) & (kk == 0))
        def _():
            dg_ref[...] = jnp.zeros_like(dg_ref)

        @pl.when(kk == nk - 1)
        def _():
            def rows_chunk(c, _):
                r = pl.ds(pl.multiple_of(c * LANES, LANES), LANES)
                dh = acc_ref[r, :] + e_ref[r, :] if has_extra else acc_ref[r, :]
                dv, dg = _norm_bwd_rows(dh, x_ref[r, :], g_ref[...])
                o_ref[r, :] = dx_ref[r, :] + dv
                dg_ref[...] += jnp.sum(dg, axis=0, keepdims=True)
                return 0

            lax.fori_loop(0, tm // LANES, rows_chunk, 0)

    row = pl.BlockSpec((tm, d), lambda i, kk: (i, 0))
    vec = pl.BlockSpec((1, d), lambda i, kk: (0, 0))
    in_specs = [pl.BlockSpec((tm, tk), lambda i, kk: (i, kk)), pl.BlockSpec((d, tk), lambda i, kk: (0, kk))]
    args = [dproj, w_in]
    if has_extra:
        in_specs.append(row)
        args.append(extra)
    return _grid_call(
        body, name=name, grid=(s // tm, nk), in_specs=in_specs + [row, row, vec], out_specs=[row, vec],
        out_shape=[jax.ShapeDtypeStruct((s, d), F32), jax.ShapeDtypeStruct((1, d), F32)],
        args=tuple(args) + (dx, x, g), scratch_shapes=[pltpu.VMEM((tm, d), F32)], semantics=("arbitrary", "arbitrary"),
        exchange=exchange)


def _loss_and_grad(x, target):
    s, d = x.shape

    def body(x_ref, t_ref, dx_ref, l_ref):
        err = x_ref[...] - t_ref[...]
        dx_ref[...] = err * (1.0 / d)
        part = jnp.sum(jnp.sum(err * err, axis=1, keepdims=True), axis=0, keepdims=True) * (0.5 / d)

        @pl.when(pl.program_id(0) == 0)
        def _():
            l_ref[...] = jnp.zeros_like(l_ref)

        l_ref[...] += jnp.broadcast_to(part, l_ref.shape)

    return _row_call(body, "loss_head", [(x, "row"), (target, "row")],
                     [(jax.ShapeDtypeStruct((s, d), F32), "row"),
                      (jax.ShapeDtypeStruct((1, LANES), F32), "vec")], s=s)


def _norm_bwd_rows(dn, v, g):
    r = _rsqrt_ms(v)
    a = dn * g
    dv = r * (a - v * (r * r) * jnp.mean(a * v, axis=-1, keepdims=True))
    return dv, dn * v * r


def _lane_is_first_head():
    return lax.broadcasted_iota(jnp.int32, (1, LANES), 1) < HEAD_DIM


def _bcast_lanes(col):
    return jnp.broadcast_to(col, (col.shape[0], LANES))


def _pair_spec(s, off=0, width=LANES):
    return pl.BlockSpec((s, width), lambda p: (0, p + off))


def _stack_heads(pair, first):
    return jnp.concatenate([jnp.where(first, pair, 0), jnp.where(first, 0, pair)], axis=0).astype(BF16)


def _stacked_mask(t, strict):
    row = lax.broadcasted_iota(jnp.int32, (2 * t, t), 0)
    col = lax.broadcasted_iota(jnp.int32, (2 * t, t), 1)
    query = jnp.where(row >= t, row - t, row)
    return col < query if strict else col <= query


def _steps_in_groups(n, step, carry, widths=(2, 1)):
    done = 0
    for width in widths:
        def group(jj, c, width=width, done=done):
            for k in range(width):
                c = step(done + width * jj + k, c)
            return c

        trips = (n - done) // width
        carry = lax.fori_loop(0, trips, group, carry)
        done = done + width * trips
    return carry


FULL_ATTENTION_WIDTHS = (4, 2, 1)


def _rowsum_heads(prod, first):
    return (jnp.sum(jnp.where(first, prod, 0.0), axis=1, keepdims=True),
            jnp.sum(jnp.where(first, 0.0, prod), axis=1, keepdims=True))


def _softplus_parts(z):
    e = jnp.exp(-jnp.abs(z))
    sp = jnp.maximum(z, 0.0) + jnp.log(1.0 + e)
    r = 1.0 / (1.0 + e)
    return sp, jnp.where(z >= 0, r, e * r)


def _sb_tile(s):
    return min(256, s)


def _attn_b_fwd(qkv, name, exchange=None):
    s = qkv.shape[0]
    t = _sb_tile(s)
    nq = s // t

    def body(q_ref, k_ref, v_ref, o_ref, lt_ref):
        first = _lane_is_first_head()
        before = _stacked_mask(t, strict=True)
        tri = (lax.broadcasted_iota(jnp.int32, (t, t), 0) >= lax.broadcasted_iota(jnp.int32, (t, t), 1)).astype(BF16)

        def tile(j, carry, diag, qs):
            c, acc = carry
            c0 = pl.multiple_of(j * t, t)
            k2 = k_ref[pl.ds(c0, t), :]
            v2 = v_ref[pl.ds(c0, t), :]
            z = lax.dot_general(qs, k2, _NT, preferred_element_type=F32)
            sp, _ = _softplus_parts(z)
            lf = jnp.where(before, -sp, 0.0) if diag else -sp
            incl = jnp.dot(lf.astype(BF16), tri, preferred_element_type=F32)
            a = jnp.exp(z + c + incl)
            if diag:
                a = jnp.where(before, a, 0.0)
            pv = jnp.dot(a.astype(BF16), v2, preferred_element_type=F32)
            return c + incl[:, 0:1], acc + jnp.where(first, pv[:t], pv[t:])

        def qblock(i, _):
            r0 = pl.multiple_of(i * t, t)
            qs = _stack_heads(q_ref[pl.ds(r0, t), :] * Q_SCALE, first)
            carry = tile(i, (jnp.zeros((2 * t, 1), F32), jnp.zeros((t, LANES), F32)), True, qs)
            carry = _steps_in_groups(i, lambda j, c: tile(i - 1 - j, c, False, qs), carry, FULL_ATTENTION_WIDTHS)
            o_ref[pl.ds(r0, t), :] = carry[1]
            lt_ref[pl.ds(r0, t), 0:LANES] = _bcast_lanes(carry[0][:t])
            lt_ref[pl.ds(r0, t), LANES:2 * LANES] = _bcast_lanes(carry[0][t:])
            return 0

        lax.fori_loop(0, nq, qblock, 0)

    return _grid_call(
        body, name=name, grid=(N_PAIRS,),
        in_specs=[_pair_spec(s), _pair_spec(s, N_PAIRS), _pair_spec(s, 2 * N_PAIRS)],
        out_specs=[_pair_spec(s), _stat_spec(s)],
        out_shape=[jax.ShapeDtypeStruct((s, BRANCH), F32), jax.ShapeDtypeStruct((s, N_HEADS * LANES), F32)],
        args=(qkv, qkv, qkv), semantics=("parallel",), exchange=exchange)


def _attn_b_bwd(qkv, ltot, do, name, exchange=None):
    s = qkv.shape[0]
    t = _sb_tile(s)
    nq = s // t

    def body(q_ref, k_ref, v_ref, lt_ref, do_ref, dq_ref, dk_ref, dv_ref, dk_acc, dv_acc):
        first = _lane_is_first_head()
        before = _stacked_mask(t, strict=True)
        tri = (lax.broadcasted_iota(jnp.int32, (t, t), 0) <= lax.broadcasted_iota(jnp.int32, (t, t), 1)).astype(BF16)
        dk_acc[...] = jnp.zeros_like(dk_acc)
        dv_acc[...] = jnp.zeros_like(dv_acc)

        def tile(j, carry, diag, qs, dos, lt):
            p_l, p_g, dq_acc = carry
            c0 = pl.multiple_of(j * t, t)
            k2 = k_ref[pl.ds(c0, t), :]
            v2 = v_ref[pl.ds(c0, t), :]
            z = lax.dot_general(qs, k2, _NT, preferred_element_type=F32)
            sp, sig = _softplus_parts(z)
            lf = jnp.where(before, -sp, 0.0) if diag else -sp
            pref_l = jnp.dot(lf.astype(BF16), tri, preferred_element_type=F32)
            a = jnp.exp(z + ((lt - p_l) - pref_l + lf))
            if diag:
                a = jnp.where(before, a, 0.0)
            g = a * lax.dot_general(dos, v2, _NT, preferred_element_type=F32)
            pref_g = jnp.dot(g.astype(BF16), tri, preferred_element_type=F32)
            dz = g - sig * (p_g + pref_g)
            if diag:
                dz = jnp.where(before, dz, 0.0)
            dzb = dz.astype(BF16)
            dq = jnp.dot(dzb, k2, preferred_element_type=F32)
            dk_acc[pl.ds(c0, t), :] += lax.dot_general(dzb, qs, _TN, preferred_element_type=F32)
            dv_acc[pl.ds(c0, t), :] += lax.dot_general(a.astype(BF16), dos, _TN, preferred_element_type=F32)
            return p_l + pref_l[:, t - 1:t], p_g + pref_g[:, t - 1:t], dq_acc + jnp.where(first, dq[:t], dq[t:])

        def qblock(i, _):
            r0 = pl.multiple_of(i * t, t)
            qs = _stack_heads(q_ref[pl.ds(r0, t), :] * Q_SCALE, first)
            dos = _stack_heads(do_ref[pl.ds(r0, t), :], first)
            lt = jnp.concatenate([lt_ref[pl.ds(r0, t), 0:1], lt_ref[pl.ds(r0, t), LANES:LANES + 1]], axis=0)
            zero = jnp.zeros((2 * t, 1), F32)
            carry = (zero, zero, jnp.zeros((t, LANES), F32))
            carry = _steps_in_groups(i, lambda j, c: tile(j, c, False, qs, dos, lt), carry, FULL_ATTENTION_WIDTHS)
            carry = tile(i, carry, True, qs, dos, lt)
            dq_ref[pl.ds(r0, t), :] = (carry[2] * Q_SCALE).astype(BF16)
            return 0

        lax.fori_loop(0, nq, qblock, 0)
        dk_ref[...] = dk_acc[...].astype(BF16)
        dv_ref[...] = dv_acc[...].astype(BF16)

    out = jax.ShapeDtypeStruct((s, BRANCH), BF16)
    return _grid_call(
        body, name=name, grid=(N_PAIRS,),
        in_specs=[_pair_spec(s), _pair_spec(s, N_PAIRS), _pair_spec(s, 2 * N_PAIRS), _stat_spec(s), _pair_spec(s)],
        out_specs=[_pair_spec(s)] * 3, out_shape=[out] * 3,
        scratch_shapes=[pltpu.VMEM((s, LANES), F32), pltpu.VMEM((s, LANES), F32)],
        args=(qkv, qkv, qkv, ltot, do), semantics=("parallel",), exchange=exchange)


def _fox_tile(s):
    return min(256, s)


def _stat_spec(s):
    return pl.BlockSpec((s, 2 * LANES), lambda p: (0, p))


def _cum_spec(nt, t):
    return pl.BlockSpec((1, nt, 2, t), lambda p: (p, 0, 0, 0))


def _attn_c_fwd(qkv, cum4, name, exchange=None):
    s = qkv.shape[0]
    t = _fox_tile(s)
    nq = s // t

    def body(q_ref, k_ref, v_ref, c_ref, o_ref, lse_ref):
        first = _lane_is_first_head()
        causal = _stacked_mask(t, strict=False)

        def tile(j, carry, diag, qs):
            c0 = pl.multiple_of(j * t, t)
            k2 = k_ref[pl.ds(c0, t), :]
            v2 = v_ref[pl.ds(c0, t), :]
            cs = c_ref[0, j]
            m_prev, l_prev, acc = carry
            z = lax.dot_general(qs, k2, _NT, preferred_element_type=F32)
            sc = jnp.concatenate([z[:t] - cs[0:1, :], z[t:] - cs[1:2, :]], axis=0)
            if diag:
                sc = jnp.where(causal, sc, NEG)
            m_new = jnp.maximum(m_prev, jnp.max(sc, axis=1, keepdims=True))
            alpha = jnp.exp(m_prev - m_new)
            p = jnp.exp(sc - m_new)
            l_new = alpha * l_prev + jnp.sum(p, axis=1, keepdims=True)
            pv = jnp.dot(p.astype(BF16), v2, preferred_element_type=F32)
            acc = jnp.where(first, acc * alpha[:t] + pv[:t], acc * alpha[t:] + pv[t:])
            return m_new, l_new, acc

        def qblock(i, _):
            r0 = pl.multiple_of(i * t, t)
            qs = _stack_heads(q_ref[pl.ds(r0, t), :] * Q_SCALE, first)
            carry = (jnp.full((2 * t, 1), NEG, F32), jnp.zeros((2 * t, 1), F32), jnp.zeros((t, LANES), F32))
            carry = _steps_in_groups(i, lambda j, c: tile(j, c, False, qs), carry, FULL_ATTENTION_WIDTHS)
            m, l, acc = tile(i, carry, True, qs)
            inv = 1.0 / l
            lse = m + jnp.log(l)
            o_ref[pl.ds(r0, t), :] = acc * jnp.where(first, inv[:t], inv[t:])
            lse_ref[pl.ds(r0, t), 0:LANES] = _bcast_lanes(lse[:t])
            lse_ref[pl.ds(r0, t), LANES:2 * LANES] = _bcast_lanes(lse[t:])
            return 0

        lax.fori_loop(0, nq, qblock, 0)

    return _grid_call(
        body, name=name, grid=(N_PAIRS,),
        in_specs=[_pair_spec(s), _pair_spec(s, N_PAIRS), _pair_spec(s, 2 * N_PAIRS), _cum_spec(nq, t)],
        out_specs=[_pair_spec(s), _stat_spec(s)],
        out_shape=[jax.ShapeDtypeStruct((s, BRANCH), F32), jax.ShapeDtypeStruct((s, N_HEADS * LANES), F32)],
        args=(qkv, qkv, qkv, cum4), semantics=("parallel",), exchange=exchange)


def _attn_c_bwd(qkv, cum4, o, lse, do, name, exchange=None):
    s = qkv.shape[0]
    t = _fox_tile(s)
    nq = s // t

    def body(q_ref, k_ref, v_ref, c_ref, o_ref, lse_ref, do_ref, dq_ref, dk_ref, dv_ref, dc_ref, dk_acc, dv_acc):
        first = _lane_is_first_head()
        causal = _stacked_mask(t, strict=False)
        eye = lax.broadcasted_iota(jnp.int32, (t, t), 0) == lax.broadcasted_iota(jnp.int32, (t, t), 1)
        dk_acc[...] = jnp.zeros_like(dk_acc)
        dv_acc[...] = jnp.zeros_like(dv_acc)
        dc_ref[...] = jnp.zeros_like(dc_ref)

        def tile(j, carry, diag, qs, dos, delta, lse):
            dq_acc, rs = carry
            c0 = pl.multiple_of(j * t, t)
            k2 = k_ref[pl.ds(c0, t), :]
            v2 = v_ref[pl.ds(c0, t), :]
            cs = c_ref[0, j]
            z = lax.dot_general(qs, k2, _NT, preferred_element_type=F32)
            sc = jnp.concatenate([z[:t] - cs[0:1, :], z[t:] - cs[1:2, :]], axis=0)
            p = jnp.exp(sc - lse)
            if diag:
                p = jnp.where(causal, p, 0.0)
            ds = p * (lax.dot_general(dos, v2, _NT, preferred_element_type=F32) - delta)
            dsb = ds.astype(BF16)
            dq = jnp.dot(dsb, k2, preferred_element_type=F32)
            dk_acc[pl.ds(c0, t), :] += lax.dot_general(dsb, qs, _TN, preferred_element_type=F32)
            dv_acc[pl.ds(c0, t), :] += lax.dot_general(p.astype(BF16), dos, _TN, preferred_element_type=F32)
            col_sums = jnp.concatenate([jnp.sum(ds[:t], axis=0, keepdims=True), jnp.sum(ds[t:], axis=0, keepdims=True)], axis=0)
            dc_ref[0, j] = dc_ref[0, j] - col_sums
            return dq_acc + jnp.where(first, dq[:t], dq[t:]), rs + jnp.sum(ds, axis=1, keepdims=True)

        def qblock(i, _):
            r0 = pl.multiple_of(i * t, t)
            do2 = do_ref[pl.ds(r0, t), :]
            qs = _stack_heads(q_ref[pl.ds(r0, t), :] * Q_SCALE, first)
            dos = _stack_heads(do2, first)
            delta = jnp.concatenate(_rowsum_heads(do2.astype(F32) * o_ref[pl.ds(r0, t), :], first), axis=0)
            lse = jnp.concatenate([lse_ref[pl.ds(r0, t), 0:1], lse_ref[pl.ds(r0, t), LANES:LANES + 1]], axis=0)
            carry = (jnp.zeros((t, LANES), F32), jnp.zeros((2 * t, 1), F32))
            carry = _steps_in_groups(i, lambda j, c: tile(j, c, False, qs, dos, delta, lse), carry, FULL_ATTENTION_WIDTHS)
            dq_acc, rs = tile(i, carry, True, qs, dos, delta, lse)
            dq_ref[pl.ds(r0, t), :] = (dq_acc * Q_SCALE).astype(BF16)
            as_row = lambda col_vec: jnp.sum(jnp.where(eye, col_vec, 0.0), axis=0, keepdims=True)
            dc_ref[0, i] = dc_ref[0, i] + jnp.concatenate([as_row(rs[:t]), as_row(rs[t:])], axis=0)
            return 0

        lax.fori_loop(0, nq, qblock, 0)
        dk_ref[...] = dk_acc[...].astype(BF16)
        dv_ref[...] = dv_acc[...].astype(BF16)

    out = jax.ShapeDtypeStruct((s, BRANCH), BF16)
    return _grid_call(
        body, name=name, grid=(N_PAIRS,),
        in_specs=[_pair_spec(s), _pair_spec(s, N_PAIRS), _pair_spec(s, 2 * N_PAIRS), _cum_spec(nq, t),
                  _pair_spec(s), _stat_spec(s), _pair_spec(s)],
        out_specs=[_pair_spec(s)] * 3 + [_cum_spec(nq, t)],
        out_shape=[out] * 3 + [jax.ShapeDtypeStruct(cum4.shape, F32)],
        scratch_shapes=[pltpu.VMEM((s, LANES), F32), pltpu.VMEM((s, LANES), F32)],
        args=(qkv, qkv, qkv, cum4, o, lse, do), semantics=("parallel",), exchange=exchange)


FG_CHUNK = 512


def _tri_dot3(x, t):
    hi = x.astype(BF16)
    r1 = x - hi.astype(F32)
    mid = r1.astype(BF16)
    lo = (r1 - mid.astype(F32)).astype(BF16)
    return (jnp.dot(hi, t, preferred_element_type=F32) + jnp.dot(mid, t, preferred_element_type=F32)
            + jnp.dot(lo, t, preferred_element_type=F32))


def _fgate_fwd(h, wf_t, b_col, name):
    s = h.shape[0]
    c = min(FG_CHUNK, s)

    def body(h_ref, w_ref, b_ref, xf_ref, cum_ref, carry_ref):
        @pl.when(pl.program_id(0) == 0)
        def _():
            carry_ref[...] = jnp.zeros_like(carry_ref)

        xf = lax.dot_general(w_ref[...], h_ref[...], _NT, preferred_element_type=F32) + b_ref[:, 0:1]
        xf_ref[...] = xf
        logf = jnp.minimum(xf, 0.0) - jnp.log(1.0 + jnp.exp(-jnp.abs(xf)))
        row = lax.broadcasted_iota(jnp.int32, (c, c), 0)
        col = lax.broadcasted_iota(jnp.int32, (c, c), 1)
        cum = _tri_dot3(logf, (row <= col).astype(BF16)) + carry_ref[:, 0:1]
        cum_ref[...] = cum
        carry_ref[...] = _bcast_lanes(cum[:, c - 1:c])

    out = jax.ShapeDtypeStruct((N_HEADS, s), F32)
    return pl.pallas_call(
        body, name=name, grid=(s // c,),
        in_specs=[pl.BlockSpec((c, D_MODEL), lambda i: (i, 0)),
                  pl.BlockSpec((N_HEADS, D_MODEL), lambda i: (0, 0)),
                  pl.BlockSpec((N_HEADS, LANES), lambda i: (0, 0))],
        out_specs=[pl.BlockSpec((N_HEADS, c), lambda i: (0, i))] * 2,
        out_shape=[out, out],
        scratch_shapes=[pltpu.VMEM((N_HEADS, LANES), F32)],
        compiler_params=_params(("arbitrary",)),
    )(h, wf_t, b_col)


def _fgate_bwd(dcum, xf, h, wf_t, name):
    s = h.shape[0]
    c = min(FG_CHUNK, s)
    n = s // c

    def body(dc_ref, xf_ref, h_ref, w_ref, dw_ref, dh_ref, db_ref, carry_ref):
        @pl.when(pl.program_id(0) == 0)
        def _():
            carry_ref[...] = jnp.zeros_like(carry_ref)
            dw_ref[...] = jnp.zeros_like(dw_ref)
            db_ref[...] = jnp.zeros_like(db_ref)

        row = lax.broadcasted_iota(jnp.int32, (c, c), 0)
        col = lax.broadcasted_iota(jnp.int32, (c, c), 1)
        dlogf = _tri_dot3(dc_ref[...], (row >= col).astype(BF16)) + carry_ref[:, 0:1]
        carry_ref[...] = _bcast_lanes(dlogf[:, 0:1])
        xf = xf_ref[...]
        e = jnp.exp(-jnp.abs(xf))
        r = 1.0 / (1.0 + e)
        dxf = dlogf * jnp.where(xf >= 0, e * r, r)
        db_ref[...] += _bcast_lanes(jnp.sum(dxf, axis=1, keepdims=True))
        dxb = dxf.astype(BF16)
        dw_ref[...] += jnp.dot(dxb, h_ref[...], preferred_element_type=F32)
        dh_ref[...] = lax.dot_general(dxb, w_ref[...], _TN, preferred_element_type=F32)

    rev = lambda i: n - 1 - i
    return pl.pallas_call(
        body, name=name, grid=(n,),
        in_specs=[pl.BlockSpec((N_HEADS, c), lambda i: (0, rev(i))),
                  pl.BlockSpec((N_HEADS, c), lambda i: (0, rev(i))),
                  pl.BlockSpec((c, D_MODEL), lambda i: (rev(i), 0)),
                  pl.BlockSpec((N_HEADS, D_MODEL), lambda i: (0, 0))],
        out_specs=[pl.BlockSpec((N_HEADS, D_MODEL), lambda i: (0, 0)),
                   pl.BlockSpec((c, D_MODEL), lambda i: (rev(i), 0)),
                   pl.BlockSpec((N_HEADS, LANES), lambda i: (0, 0))],
        out_shape=[jax.ShapeDtypeStruct((N_HEADS, D_MODEL), F32), jax.ShapeDtypeStruct((s, D_MODEL), F32),
                   jax.ShapeDtypeStruct((N_HEADS, LANES), F32)],
        scratch_shapes=[pltpu.VMEM((N_HEADS, LANES), F32)],
        compiler_params=_params(("arbitrary",)),
    )(dcum, xf, h, wf_t)


def _to_cum4(v, t):
    s = v.shape[1]
    return v.reshape(N_PAIRS, 2, s // t, t).transpose(0, 2, 1, 3)


def _from_cum4(v4):
    p, nt, two, t = v4.shape
    return v4.transpose(0, 2, 1, 3).reshape(p * two, nt * t)


def _alibi_slopes():
    return (2.0 ** (-8.0 * np.arange(1, N_HEADS + 1, dtype=np.float32) / N_HEADS)).astype(np.float32)


def _per_head_lanes(v):
    return jnp.repeat(v.astype(F32).reshape(N_PAIRS, 1, 2), LANES, axis=2)


def _attn_a_specs(s):
    q = _pair_spec(s)
    k = pl.BlockSpec((s, LANES), lambda p: (0, N_PAIRS + p // 8))
    v = pl.BlockSpec((s, LANES), lambda p: (0, N_PAIRS + KV_A // LANES + p // 8))
    head = pl.BlockSpec((1, 1, 2 * LANES), lambda p: (p, 0, 0))
    return q, k, v, head


def _attn_a_geometry(p, slope_ref, sink_ref):
    kv_half = (p // 4) % 2
    kv_first = kv_half == 0
    lane_first = _lane_is_first_head()
    kv_lanes = (lax.broadcasted_iota(jnp.int32, (1, LANES), 1) // HEAD_DIM) == kv_half
    row = lax.broadcasted_iota(jnp.int32, (2 * WINDOW, 2 * WINDOW), 0)
    cj = lax.broadcasted_iota(jnp.int32, (2 * WINDOW, 2 * WINDOW), 1)
    second = row >= WINDOW
    dist = WINDOW + jnp.where(second, row - WINDOW, row) - cj
    valid = (dist >= 0) & (dist < WINDOW)
    per_row = lambda ref: jnp.where(second[:, 0:1], ref[0, :, LANES:LANES + 1], ref[0, :, 0:1])
    return kv_first, lane_first, kv_lanes, per_row(slope_ref) * dist.astype(F32), valid, per_row(sink_ref)


def _swap_halves(x):
    return pltpu.roll(x, HEAD_DIM, 1)


def _attn_a_fwd(qkv, slopes, sinks, name, exchange=None):
    s = qkv.shape[0]
    nb = s // WINDOW

    def body(q_ref, k_ref, v_ref, sl_ref, sk_ref, o_ref, lse_ref):
        kv_first, lane_first, kv_lanes, bias, valid, sink = _attn_a_geometry(pl.program_id(0), sl_ref, sk_ref)

        def block(r0, k0, width):
            q2 = q_ref[pl.ds(r0, WINDOW), :].astype(F32) * Q_SCALE
            q2r = _swap_halves(q2)
            xs = jnp.concatenate([jnp.where(kv_first, q2, q2r), jnp.where(kv_first, q2r, q2)], axis=0).astype(BF16)
            km = jnp.where(kv_lanes, k_ref[pl.ds(k0, width), :], 0).astype(BF16)
            vm = jnp.where(kv_lanes, v_ref[pl.ds(k0, width), :], 0).astype(BF16)
            sc = lax.dot_general(xs, km, _NT, preferred_element_type=F32) - bias[:, 2 * WINDOW - width:]
            sc = jnp.where(valid[:, 2 * WINDOW - width:], sc, NEG)
            m = jnp.maximum(jnp.max(sc, axis=1, keepdims=True), sink)
            pr = jnp.exp(sc - m)
            l = jnp.sum(pr, axis=1, keepdims=True) + jnp.exp(sink - m)
            os = jnp.dot(pr.astype(BF16), vm, preferred_element_type=F32) * (1.0 / l)
            lse = m + jnp.log(l)
            lse_ref[pl.ds(r0, WINDOW), 0:LANES] = _bcast_lanes(lse[:WINDOW])
            lse_ref[pl.ds(r0, WINDOW), LANES:2 * LANES] = _bcast_lanes(lse[WINDOW:])
            oa = jnp.where(kv_first, os[:WINDOW], _swap_halves(os[:WINDOW]))
            ob = jnp.where(kv_first, _swap_halves(os[WINDOW:]), os[WINDOW:])
            o_ref[pl.ds(r0, WINDOW), :] = jnp.where(lane_first, oa, ob)

        block(0, 0, WINDOW)

        def loop(n, _):
            r0 = pl.multiple_of(n * WINDOW, WINDOW)
            block(r0, pl.multiple_of(r0 - WINDOW, WINDOW), 2 * WINDOW)
            return 0

        _steps_in_groups(nb - 1, lambda n, c: loop(n + 1, c), 0)

    q, k, v, head = _attn_a_specs(s)
    return _grid_call(
        body, name=name, grid=(N_PAIRS,),
        in_specs=[q, k, v, head, head],
        out_specs=[_pair_spec(s), _stat_spec(s)],
        out_shape=[jax.ShapeDtypeStruct((s, BRANCH), F32), jax.ShapeDtypeStruct((s, N_HEADS * LANES), F32)],
        args=(qkv, qkv, qkv, slopes, sinks), semantics=("parallel",), exchange=exchange)


def _attn_a_bwd(qkv, slopes, sinks, o, lse, do, name, exchange=None):
    s = qkv.shape[0]
    nb = s // WINDOW

    def body(q_ref, k_ref, v_ref, sl_ref, sk_ref, o_ref, lse_ref, do_ref, dq_ref, dk_ref, dv_ref, dsk_ref):
        p_id = pl.program_id(0)
        kv_first, lane_first, kv_lanes, bias, valid, sink = _attn_a_geometry(p_id, sl_ref, sk_ref)

        @pl.when(p_id % 8 == 0)
        def _():
            dk_ref[...] = jnp.zeros_like(dk_ref)
            dv_ref[...] = jnp.zeros_like(dv_ref)

        def align(v2):
            v2r = _swap_halves(v2)
            both = jnp.concatenate([jnp.where(kv_first, v2, v2r), jnp.where(kv_first, v2r, v2)], axis=0)
            return jnp.where(kv_lanes, both, 0.0).astype(BF16)

        def block(r0, k0, width, sink_sum):
            xq = align(q_ref[pl.ds(r0, WINDOW), :].astype(F32) * Q_SCALE)
            do2 = do_ref[pl.ds(r0, WINDOW), :].astype(F32)
            xdo = align(do2)
            delta = jnp.concatenate(_rowsum_heads(do2 * o_ref[pl.ds(r0, WINDOW), :], lane_first), axis=0)
            lse = jnp.concatenate([lse_ref[pl.ds(r0, WINDOW), 0:1], lse_ref[pl.ds(r0, WINDOW), LANES:LANES + 1]], axis=0)
            km = jnp.where(kv_lanes, k_ref[pl.ds(k0, width), :], 0).astype(BF16)
            vm = jnp.where(kv_lanes, v_ref[pl.ds(k0, width), :], 0).astype(BF16)
            sc = lax.dot_general(xq, km, _NT, preferred_element_type=F32) - bias[:, 2 * WINDOW - width:]
            pr = jnp.where(valid[:, 2 * WINDOW - width:], jnp.exp(sc - lse), 0.0)
            ds = pr * (lax.dot_general(xdo, vm, _NT, preferred_element_type=F32) - delta)
            dsb = ds.astype(BF16)
            dq_al = jnp.dot(dsb, km, preferred_element_type=F32)
            dk_ref[pl.ds(k0, width), :] += lax.dot_general(dsb, xq, _TN, preferred_element_type=F32)
            dv_ref[pl.ds(k0, width), :] += lax.dot_general(pr.astype(BF16), xdo, _TN, preferred_element_type=F32)
            dqa = jnp.where(kv_first, dq_al[:WINDOW], _swap_halves(dq_al[:WINDOW]))
            dqb = jnp.where(kv_first, _swap_halves(dq_al[WINDOW:]), dq_al[WINDOW:])
            dq_ref[pl.ds(r0, WINDOW), :] = (jnp.where(lane_first, dqa, dqb) * Q_SCALE).astype(BF16)
            return sink_sum + jnp.exp(sink - lse) * delta

        sink_sum = block(0, 0, WINDOW, jnp.zeros((2 * WINDOW, 1), F32))

        def loop(n, c):
            r0 = pl.multiple_of(n * WINDOW, WINDOW)
            return block(r0, pl.multiple_of(r0 - WINDOW, WINDOW), 2 * WINDOW, c)

        sink_sum = _steps_in_groups(nb - 1, lambda n, c: loop(n + 1, c), sink_sum, FULL_ATTENTION_WIDTHS)
        dsk_ref[0, :, 0:LANES] = jnp.broadcast_to(-jnp.sum(sink_sum[:WINDOW], axis=0, keepdims=True), (1, LANES))
        dsk_ref[0, :, LANES:2 * LANES] = jnp.broadcast_to(-jnp.sum(sink_sum[WINDOW:], axis=0, keepdims=True), (1, LANES))

    q, k, v, head = _attn_a_specs(s)
    kv_out = pl.BlockSpec((s, LANES), lambda p: (0, p // 8))
    return _grid_call(
        body, name=name, grid=(N_PAIRS,),
        in_specs=[q, k, v, head, head, _pair_spec(s), _stat_spec(s), _pair_spec(s)],
        out_specs=[_pair_spec(s), kv_out, kv_out, head],
        out_shape=[jax.ShapeDtypeStruct((s, BRANCH), BF16), jax.ShapeDtypeStruct((s, KV_A), F32),
                   jax.ShapeDtypeStruct((s, KV_A), F32), jax.ShapeDtypeStruct((N_PAIRS, 1, 2 * LANES), F32)],
        args=(qkv, qkv, qkv, slopes, sinks, o, lse, do), semantics=("arbitrary",), exchange=exchange)


def _layer_kind(i):
    return i % 3, i // 3


GATHER_FIRST = [("in", 0)]
GATHER_BEHIND = {("qkv", 0): [("out", 0)], ("attn", 0): [("in", 1)], ("attn", 1): [("out", 1), ("in", 2), ("out", 2)],
                 ("attn", 2): [("in", 3), ("out", 3)]}


def _forward_backward(x, target, g_pre, g_post, sinks_a, b_f_c, shards, chip, place):
    s = x.shape[0]
    slopes = _per_head_lanes(jnp.asarray(_alibi_slopes()))
    w_in, w_out, wf_t = {}, {}, {}

    def lands_side_by_side(key):
        return key[0] == "in" and shards[key].shape[1] % LANES == 0

    def gather(keys):
        return _GatherExchange([shards[k] for k in keys], [lands_side_by_side(k) for k in keys])

    def deliver(keys, gathered):
        for key, g in zip(keys, gathered):
            side, layer = key
            sh = shards[key]
            if side == "out":
                g = lax.dynamic_update_slice(g, sh[None], (chip, 0, 0))
                w_out[layer] = g.reshape(4 * sh.shape[0], sh.shape[1])
            elif lands_side_by_side(key):
                w_in[layer] = _place_columns(g, sh, chip, f"own_block_in_l{layer}")
            else:
                g = lax.dynamic_update_slice(g, sh[None], (chip, 0, 0))
                w = g.transpose(1, 0, 2).reshape(sh.shape[0], 4 * sh.shape[1])
                w_in[layer], wf_t[layer] = lax.optimization_barrier((w[:, :4 * BRANCH], w[:, 4 * BRANCH:].T))

    deliver(GATHER_FIRST, _exchange_call(gather(GATHER_FIRST), "gather_first_weights"))
    saved = []
    for i in range(DEPTH):
        kind, j = _layer_kind(i)
        tag = f"l{i}"
        w = w_in[i]
        nqkv = A_QKV if kind == 0 else B_QKV
        tn = 512 if kind == 0 else 1024
        h, h_t = _rmsnorm_fwd(x, g_pre[i:i + 1], f"prenorm_{tag}")
        behind = GATHER_BEHIND.get(("qkv", i))
        qkv = _matmul(h, w, out_dtype=BF16, name=f"inproj_qkv_{tag}", n=nqkv, tn=tn,
                      exchange=gather(behind) if behind else None)
        if behind:
            qkv, arrived = qkv
            deliver(behind, arrived)
        z = _matmul(h, w, out_dtype=F32, name=f"inproj_gate_{tag}", n=BRANCH, b_off=nqkv // tn, tn=tn)
        behind = GATHER_BEHIND.get(("attn", i))
        exchange = gather(behind) if behind else None
        if kind == 0:
            sink_l = _per_head_lanes(sinks_a[j])
            (o, lse), arrived = _attn_a_fwd(qkv, slopes, sink_l, f"attn_a_fwd_{tag}", exchange)
            extra = (sink_l, lse)
        elif kind == 1:
            (o, extra), arrived = _attn_b_fwd(qkv, f"attn_b_fwd_{tag}", exchange)
        else:
            b_col = jnp.broadcast_to(b_f_c[j].astype(F32)[:, None], (N_HEADS, LANES))
            xf, cum = _fgate_fwd(h, wf_t[i], b_col, f"fgate_fwd_{tag}")
            cum4 = _to_cum4(cum, _fox_tile(s))
            (o, lse), arrived = _attn_c_fwd(qkv, cum4, f"attn_c_fwd_{tag}", exchange)
            extra = (xf, cum4, lse)
        if behind:
            deliver(behind, arrived)
        x_next, y, u_t = _gated_out_proj(o, z, w_out[i], x, g_post[i:i + 1], f"outproj_{tag}")
        saved.append((x, h, h_t, qkv, z, o, u_t, y, extra))
        x = x_next

    dx, loss_part = _loss_and_grad(x, target)

    d_g_pre, d_g_post = [None] * DEPTH, [None] * DEPTH
    d_sinks = [None, None]
    d_b_f = None
    reduced = {}
    pending = None

    def finish_reduce(layer, side, own, arr):
        kind, j = _layer_kind(layer)
        reduced[(side, kind)] = _sum_chips(own, arr, place, f"shard_sum_{side}_l{layer}", j, 2 if kind == 0 else 1,
                                           into=reduced.get((side, kind)))

    for i in reversed(range(DEPTH)):
        kind, j = _layer_kind(i)
        tag = f"l{i}"
        x_in, h, h_t, qkv, z, o, u_t, y, extra = saved[i]
        tn = 512 if kind == 0 else 1024
        (dy, d_g_post[i], do, dz), _ = _gated_out_proj_bwd(dx, y, g_post[i:i + 1], w_out[i], o, z, f"outproj_bwd_{tag}")
        dw_out = _matmul(u_t, dy, out_dtype=BF16, name=f"dw_out_{tag}")
        dw_out = dw_out.reshape(4, dw_out.shape[0] // 4, dw_out.shape[1])
        dh_f = None
        exchange = _SiblingExchange([dw_out])
        if pending:
            exchange = _BothExchanges(exchange, _ScatterExchange([pending[1]]))
        if kind == 0:
            sink_l, lse = extra
            (dq, dk, dv, dsk), arrived = _attn_a_bwd(qkv, slopes, sink_l, o, lse, do, f"attn_a_bwd_{tag}", exchange)
            d_sinks[j] = dsk[:, 0, ::LANES].reshape(N_HEADS)
            parts = [dq, dk.astype(BF16), dv.astype(BF16), dz]
        elif kind == 1:
            (dq, dk, dv), arrived = _attn_b_bwd(qkv, extra, do, f"attn_b_bwd_{tag}", exchange)
            parts = [dq, dk, dv, dz]
        else:
            xf, cum4, lse = extra
            (dq, dk, dv, dcum4), arrived = _attn_c_bwd(qkv, cum4, o, lse, do, f"attn_c_bwd_{tag}", exchange)
            d_wf_t, dh_f, db = _fgate_bwd(_from_cum4(dcum4), xf, h, wf_t[i], f"fgate_bwd_{tag}")
            d_b_f = db[:, 0]
            parts = [dq, dk, dv, dz]
        sum_out = _add_pairs(dw_out, arrived[0], place, f"chip_sum_out_{tag}")
        if pending:
            finish_reduce(pending[0], "in", pending[1], arrived[1])
        dproj = jnp.concatenate(parts, axis=1)
        scatter_out = _ScatterExchange([sum_out])
        if kind == 2:
            dw_in, arrived = _matmul(h_t, dproj, out_dtype=F32, name=f"dw_in_{tag}", tn=tn, exchange=scatter_out)
            dw_in = jnp.concatenate([dw_in, d_wf_t.T], axis=1)
            dw_in = dw_in.reshape(dw_in.shape[0], 4, dw_in.shape[1] // 4).transpose(1, 0, 2).astype(BF16)
        else:
            dw_in, arrived = _matmul(h_t, dproj, out_dtype=BF16, name=f"dw_in_{tag}", col_blocks=4,
                                     tn=1152 if kind == 0 else 1024, exchange=scatter_out)
        finish_reduce(i, "out", sum_out, arrived[0])
        (dx, d_g_pre[i]), (their_in,) = _in_proj_bwd(
            dproj, w_in[i], dh_f, dx, x_in, g_pre[i:i + 1], f"inproj_bwd_{tag}", 1536 if kind == 0 else 1024,
            exchange=_SiblingExchange([dw_in]))
        pending = (i, _add_pairs(dw_in, their_in, place, f"chip_sum_in_{tag}"))

    return dict(loss=loss_part, dx=dx, g_pre=jnp.concatenate(d_g_pre, axis=0), g_post=jnp.concatenate(d_g_post, axis=0),
                sinks_a=jnp.stack(d_sinks), b_f_c=d_b_f[None, :], reduced=reduced, last_sum=pending[1])


def _place():
    x, y, c = lax.axis_index("x"), lax.axis_index("y"), lax.axis_index("c")
    others = [(1 - x, y), (x, 1 - y), (1 - x, 1 - y)]
    return x, y, c, others


def _half_rows(ref_rows, which):
    half = ref_rows // 2
    return pl.ds(pl.multiple_of(which * half, half), half)


def _remote(src, dst, sems, k, device):
    send, recv = sems
    return pltpu.make_async_remote_copy(src_ref=src, dst_ref=dst, send_sem=send.at[k], recv_sem=recv.at[k],
                                        device_id=device, device_id_type=MESH)


def _hbm_call(body, name, ins, out_shapes, n_remote, aliases=None):
    any_spec = pl.BlockSpec(memory_space=pl.ANY)
    return pl.pallas_call(
        body, name=name, in_specs=[any_spec] * len(ins), out_specs=[any_spec] * len(out_shapes),
        out_shape=out_shapes, input_output_aliases=aliases or {},
        scratch_shapes=[pltpu.SemaphoreType.DMA((n_remote,)), pltpu.SemaphoreType.DMA((n_remote,))],
    )(*ins)


class _GatherExchange:
    SEMS = 8

    def __init__(self, shards, side_by_side):
        self.ins = list(shards)
        self.side_by_side = list(side_by_side)
        self.out_shapes = [jax.ShapeDtypeStruct((a.shape[0], 4 * a.shape[1]) if wide else (4,) + a.shape, a.dtype)
                           for a, wide in zip(shards, side_by_side)]
        self.n_sems = self.SEMS * len(shards)
        self.aliases = {}

    def _copies(self, ins, outs, sems):
        x, y, c, _ = _place()
        me, diag = 2 * x + y, 2 * (1 - x) + (1 - y)
        nbr = [((1 - x, y, c), 2 * (1 - x) + y), ((x, 1 - y, c), 2 * x + (1 - y))]
        sibling = (x, y, 1 - c)
        table = []
        for w, (src, dst, wide) in enumerate(zip(ins, outs, self.side_by_side)):
            rows, cols = src.shape
            half, quarter = rows // 2, rows // 4

            def slot(chip, core, piece=None, dst=dst, wide=wide, cols=cols, half=half, quarter=quarter):
                start, size = (core * half, half) if piece is None else (core * half + piece * quarter, quarter)
                which = pl.ds(pl.multiple_of(start, quarter), size)
                return dst.at[which, pl.ds(pl.multiple_of(chip * cols, LANES), cols)] if wide else dst.at[chip, which]

            k0 = self.SEMS * w
            cp = lambda s_, d_, k, dev: _remote(s_, d_, sems, k0 + k, dev)
            mine_src = src.at[pl.ds(pl.multiple_of(c * half, half), half)]
            d = dict(
                send=[cp(mine_src, slot(me, c), k, nbr[k][0]) for k in range(2)],
                got=[cp(slot(nbr[k][1], c), slot(nbr[k][1], c), k, nbr[k][0]) for k in range(2)],
                fwd=[cp(slot(nbr[k][1], c, k), slot(nbr[k][1], c, k), 2 + k, nbr[1 - k][0]) for k in range(2)],
                got_fwd=[cp(slot(diag, c, k), slot(diag, c, k), 2 + k, nbr[1 - k][0]) for k in range(2)],
                pass_=[cp(slot(nbr[k][1], c), slot(nbr[k][1], c), 4 + k, sibling) for k in range(2)]
                + [cp(slot(diag, c, k), slot(diag, c, k), 6 + k, sibling) for k in range(2)],
                got_pass=[cp(slot(nbr[k][1], 1 - c), slot(nbr[k][1], 1 - c), 4 + k, sibling) for k in range(2)]
                + [cp(slot(diag, 1 - c, k), slot(diag, 1 - c, k), 6 + k, sibling) for k in range(2)])
            table.append(d)
        return table

    def start(self, ins, outs, sems):
        for d in self._copies(ins, outs, sems):
            for cp in d["send"]:
                cp.start()

    def mid(self, ins, outs, sems):
        for d in self._copies(ins, outs, sems):
            for k in range(2):
                d["got"][k].wait_recv()
                d["fwd"][k].start()
                d["pass_"][k].start()

    def finish(self, ins, outs, sems):
        table = self._copies(ins, outs, sems)
        for d in table:
            for k in range(2):
                d["got_fwd"][k].wait_recv()
                d["pass_"][2 + k].start()
        for d in table:
            for cp in d["got_pass"]:
                cp.wait_recv()
            for cp in d["send"] + d["fwd"] + d["pass_"]:
                cp.wait_send()


class _SemaphoresFrom:
    def __init__(self, ref, start):
        self._ref, self._start = ref, start

    @property
    def at(self):
        return self

    def __getitem__(self, k):
        return self._ref.at[self._start + k]


class _BothExchanges:
    def __init__(self, first, second):
        self.parts = (first, second)
        self.ins = first.ins + second.ins
        self.out_shapes = first.out_shapes + second.out_shapes
        self.n_sems = first.n_sems + second.n_sems
        self.aliases = {}

    def _each(self, phase, ins, outs, sems):
        i0 = o0 = s0 = 0
        for ex in self.parts:
            n_in, n_out = len(ex.ins), len(ex.out_shapes)
            getattr(ex, phase)(ins[i0:i0 + n_in], outs[o0:o0 + n_out], tuple(_SemaphoresFrom(r, s0) for r in sems))
            i0, o0, s0 = i0 + n_in, o0 + n_out, s0 + ex.n_sems

    def start(self, ins, outs, sems):
        self._each("start", ins, outs, sems)

    def mid(self, ins, outs, sems):
        self._each("mid", ins, outs, sems)

    def finish(self, ins, outs, sems):
        self._each("finish", ins, outs, sems)


def _place_columns(wide, block, chip, name):
    rows, cc = block.shape
    tr = min(512, rows)

    def body(c_ref, b_ref, w_ref, o_ref):
        o_ref[...] = b_ref[...]

    return pl.pallas_call(
        body, name=name,
        grid_spec=pltpu.PrefetchScalarGridSpec(
            num_scalar_prefetch=1, grid=(rows // tr,),
            in_specs=[pl.BlockSpec((tr, cc), lambda r, c_ref: (r, 0)), pl.BlockSpec(memory_space=pl.ANY)],
            out_specs=pl.BlockSpec((tr, cc), lambda r, c_ref: (r, c_ref[0]))),
        out_shape=jax.ShapeDtypeStruct(wide.shape, wide.dtype), input_output_aliases={2: 0},
        compiler_params=_params(("parallel",)),
    )(chip.astype(jnp.int32).reshape(1), block, wide)


def _exchange_call(ex, name):
    n_in, n_out = len(ex.ins), len(ex.out_shapes)

    def body(*refs):
        ins, outs, sems = refs[:n_in], refs[n_in:n_in + n_out], refs[n_in + n_out:]
        ex.start(ins, outs, sems)
        ex.mid(ins, outs, sems)
        ex.finish(ins, outs, sems)

    return _hbm_call(body, name, ex.ins, ex.out_shapes, ex.n_sems, aliases=ex.aliases)


def _grid_call(body, *, name, grid, in_specs, out_specs, out_shape, args, scratch_shapes=(), semantics, exchange=None):
    if exchange is None:
        res = pl.pallas_call(body, name=name, grid=grid, in_specs=list(in_specs), out_specs=list(out_specs),
                             out_shape=list(out_shape), scratch_shapes=list(scratch_shapes),
                             compiler_params=_params(semantics))(*args)
        return res, []
    n_in, n_out, n_scr = len(args), len(out_shape), len(scratch_shapes)
    x_in, x_out = len(exchange.ins), len(exchange.out_shapes)
    steps = math.prod(grid)

    def wrapped(*refs):
        core_in, ex_in = refs[:n_in], refs[n_in:n_in + x_in]
        rest = refs[n_in + x_in:]
        core_out, ex_out = rest[:n_out], rest[n_out:n_out + x_out]
        scratch, sems = rest[n_out + x_out:n_out + x_out + n_scr], rest[n_out + x_out + n_scr:]
        step = 0
        for axis, extent in enumerate(grid):
            step = step * extent + pl.program_id(axis)

        @pl.when(step == 0)
        def _():
            exchange.start(ex_in, ex_out, sems)

        body(*core_in, *core_out, *scratch)

        @pl.when(step == max((3 * steps) // 4 - 1, 0))
        def _():
            exchange.mid(ex_in, ex_out, sems)

        @pl.when(step == steps - 1)
        def _():
            exchange.finish(ex_in, ex_out, sems)

    any_spec = pl.BlockSpec(memory_space=pl.ANY)
    res = pl.pallas_call(
        wrapped, name=name, grid=grid,
        in_specs=list(in_specs) + [any_spec] * x_in, out_specs=list(out_specs) + [any_spec] * x_out,
        out_shape=list(out_shape) + list(exchange.out_shapes),
        input_output_aliases={n_in + a: n_out + b for a, b in exchange.aliases.items()},
        scratch_shapes=list(scratch_shapes) + [pltpu.SemaphoreType.DMA((exchange.n_sems,)),
                                               pltpu.SemaphoreType.DMA((exchange.n_sems,))],
        compiler_params=_params(("arbitrary",) * len(grid)),
    )(*args, *exchange.ins)
    return res[:n_out], res[n_out:]


class _SiblingExchange:
    def __init__(self, parts):
        self.ins = list(parts)
        self.out_shapes = [jax.ShapeDtypeStruct((4, a.shape[1] // 2, a.shape[2]), a.dtype) for a in parts]
        self.n_sems = len(parts)
        self.aliases = {}

    def _copies(self, ins, outs, sems):
        x, y, c, _ = _place()
        return [_remote(src.at[:, _half_rows(src.shape[1], 1 - c)], dst, sems, w, (x, y, 1 - c))
                for w, (src, dst) in enumerate(zip(ins, outs))]

    def start(self, ins, outs, sems):
        for cp in self._copies(ins, outs, sems):
            cp.start()

    def mid(self, ins, outs, sems):
        pass

    def finish(self, ins, outs, sems):
        for cp in self._copies(ins, outs, sems):
            cp.wait_recv()
            cp.wait_send()


class _ScatterExchange:
    def __init__(self, sums):
        self.ins = list(sums)
        self.out_shapes = [jax.ShapeDtypeStruct(a.shape, a.dtype) for a in sums]
        self.n_sems = 3 * len(sums)
        self.aliases = {}

    def _copies(self, ins, outs, sems):
        x, y, c, others = _place()
        me = 2 * x + y
        table = []
        for w, (src, dst) in enumerate(zip(ins, outs)):
            for j, (px, py) in enumerate(others):
                there = 2 * px + py
                send = _remote(src.at[there], dst.at[me], sems, 3 * w + j, (px, py, c))
                landed = _remote(dst.at[there], dst.at[there], sems, 3 * w + j, (px, py, c))
                table.append((send, landed))
        return table

    def start(self, ins, outs, sems):
        for send, _ in self._copies(ins, outs, sems):
            send.start()

    def mid(self, ins, outs, sems):
        pass

    def finish(self, ins, outs, sems):
        table = self._copies(ins, outs, sems)
        for _, landed in table:
            landed.wait_recv()
        for send, _ in table:
            send.wait_send()


def _sibling_join(shards, name):
    n = len(shards)

    def body(*refs):
        ins, outs, sems = refs[:n], refs[n:2 * n], refs[2 * n:2 * n + 2]
        x, y, c, _ = _place()
        pend = []
        for w in range(n):
            rows = ins[w].shape[1]
            mine, theirs = _half_rows(rows, c), _half_rows(rows, 1 - c)
            cp = _remote(ins[w].at[:, mine], outs[w].at[:, mine], sems, w, (x, y, 1 - c))
            cp.start()
            pend.append((cp, _remote(ins[w].at[:, theirs], outs[w].at[:, theirs], sems, w, (x, y, 1 - c))))
        for cp, landed in pend:
            landed.wait_recv()
            cp.wait_send()

    out_shapes = [jax.ShapeDtypeStruct(a.shape, a.dtype) for a in shards]
    return _hbm_call(body, name, shards, out_shapes, n, aliases={w: w for w in range(n)})


SMALL_ROWS = 136


def _all_reduce_small(vec):
    def body(v_ref, o_ref, buf, send, recv, loc):
        x, y, c, _ = _place()
        me = 4 * x + 2 * y + c
        lc = pltpu.make_async_copy(v_ref, buf.at[me], loc.at[0])
        lc.start()
        cps = []
        for k in range(1, 8):
            fx, fy, fc = (k >> 2) & 1, (k >> 1) & 1, k & 1
            peer = (x ^ fx, y ^ fy, c ^ fc)
            cp = pltpu.make_async_remote_copy(src_ref=v_ref, dst_ref=buf.at[me], send_sem=send.at[k - 1],
                                              recv_sem=recv.at[k - 1], device_id=peer, device_id_type=MESH)
            cp.start()
            cps.append((cp, 4 * peer[0] + 2 * peer[1] + peer[2]))
        for k, (cp, src) in enumerate(cps):
            pltpu.make_async_remote_copy(src_ref=v_ref, dst_ref=buf.at[src], send_sem=send.at[k], recv_sem=recv.at[k],
                                         device_id=(x, y, c), device_id_type=MESH).wait_recv()
        for cp, _ in cps:
            cp.wait_send()
        lc.wait()
        total = buf[0]
        for k in range(1, 8):
            total = total + buf[k]
        o_ref[...] = total

    vm = pl.BlockSpec(memory_space=pltpu.VMEM)
    return pl.pallas_call(
        body, name="all_reduce_small", in_specs=[vm], out_specs=vm,
        out_shape=jax.ShapeDtypeStruct(vec.shape, F32),
        scratch_shapes=[pltpu.VMEM((8,) + vec.shape, F32), pltpu.SemaphoreType.DMA((7,)),
                        pltpu.SemaphoreType.DMA((7,)), pltpu.SemaphoreType.DMA((1,))],
    )(vec)


SUM_ROWS = 256


def _add_pairs(part, theirs, place, name):
    four, rh, cc = theirs.shape
    tr = min(SUM_ROWS, rh)
    halves = part.reshape(four, 2, rh, cc)

    def body(p_ref, a_ref, b_ref, o_ref):
        o_ref[0] = (a_ref[0, 0].astype(F32) + b_ref[0].astype(F32)).astype(o_ref.dtype)

    spec = pl.BlockSpec((1, tr, cc), lambda k, r, p_ref: (k, r, 0))
    return pl.pallas_call(
        body, name=name,
        grid_spec=pltpu.PrefetchScalarGridSpec(
            num_scalar_prefetch=1, grid=(four, rh // tr),
            in_specs=[pl.BlockSpec((1, 1, tr, cc), lambda k, r, p_ref: (k, p_ref[1], r, 0)), spec], out_specs=spec),
        out_shape=jax.ShapeDtypeStruct(theirs.shape, theirs.dtype),
        compiler_params=_params(("parallel", "parallel")),
    )(place, halves, theirs)


def _sum_chips(own, arrived, place, name, layer, n_layers, into=None):
    four, rh, cc = own.shape
    tr = min(SUM_ROWS, rh)
    nr = rh // tr

    def body(p_ref, own_ref, arr_ref, *rest):
        o_ref = rest[-1]
        x, y = lax.axis_index("x"), lax.axis_index("y")
        tot = own_ref[0].astype(F32)
        for px, py in ((1 - x, y), (x, 1 - y), (1 - x, 1 - y)):
            tot = tot + arr_ref[2 * px + py].astype(F32)
        o_ref[0] = tot

    in_specs = [pl.BlockSpec((1, tr, cc), lambda r, p_ref: (p_ref[0], r, 0)),
                pl.BlockSpec((4, tr, cc), lambda r, p_ref: (0, r, 0))]
    args, aliases = [place, own, arrived], {}
    if into is not None:
        in_specs.append(pl.BlockSpec(memory_space=pl.ANY))
        args.append(into)
        aliases = {3: 0}
    return pl.pallas_call(
        body, name=name,
        grid_spec=pltpu.PrefetchScalarGridSpec(
            num_scalar_prefetch=1, grid=(nr,), in_specs=in_specs,
            out_specs=pl.BlockSpec((1, tr, cc), lambda r, p_ref: (layer, p_ref[1] * nr + r, 0))),
        out_shape=jax.ShapeDtypeStruct((n_layers, 2 * rh, cc), F32), input_output_aliases=aliases,
        compiler_params=_params(("parallel",)),
    )(*args)


ADAM_ROWS = 256


def _adamw(w, g, m, v, name):
    shape = w.shape
    as3 = lambda a: a.reshape((-1,) + shape[-2:])
    layers, rows, cc = as3(w).shape
    by_rows = rows % min(ADAM_ROWS, rows) == 0
    tr, tc = (min(ADAM_ROWS, rows), cc) if by_rows else (rows, ADAM_ROWS)
    assert rows % tr == 0 and cc % tc == 0

    def body(w_ref, g_ref, m_ref, v_ref, d_ref, nm_ref, nv_ref):
        _adamw_update(w_ref, g_ref, m_ref, v_ref, d_ref, nm_ref, nv_ref)

    spec = pl.BlockSpec((1, tr, tc), (lambda l, i: (l, i, 0)) if by_rows else (lambda l, i: (l, 0, i)))
    sh = jax.ShapeDtypeStruct((layers, rows, cc), F32)
    outs = pl.pallas_call(
        body, name=name, grid=(layers, (rows // tr) * (cc // tc)), in_specs=[spec] * 4, out_specs=[spec] * 3,
        out_shape=[sh] * 3,
        compiler_params=_params(("parallel", "parallel")),
    )(as3(w), as3(g), as3(m), as3(v))
    return [o.reshape(shape) for o in outs]


def _adamw_update(w_ref, g_ref, m_ref, v_ref, d_ref, nm_ref, nv_ref):
    c1 = 1.0 - ADAM_B1 ** ADAM_STEP
    c2 = 1.0 - ADAM_B2 ** ADAM_STEP
    gv = g_ref[...]
    nm = ADAM_B1 * m_ref[...] + (1.0 - ADAM_B1) * gv
    nv = ADAM_B2 * v_ref[...] + (1.0 - ADAM_B2) * (gv * gv)
    nm_ref[...] = nm
    nv_ref[...] = nv
    d_ref[...] = -ADAM_LR * ((nm / c1) / (jnp.sqrt(nv / c2) + ADAM_EPS) + ADAM_WD * w_ref[...])


ADAM_MANY_STEPS = 16


def _adamw_many(quads, name, exchange=None):
    n = ADAM_MANY_STEPS
    specs = []
    for w, _, _, _ in quads:
        layers, rows, cc = w.shape
        if rows % (8 * n) == 0:
            specs.append(pl.BlockSpec((layers, rows // n, cc), lambda i: (0, i, 0)))
        else:
            assert cc % (LANES * n) == 0, (name, w.shape)
            specs.append(pl.BlockSpec((layers, rows, cc // n), lambda i: (0, 0, i)))

    def body(*refs):
        ins, outs = refs[:4 * len(quads)], refs[4 * len(quads):]
        for q in range(len(quads)):
            _adamw_update(*ins[4 * q:4 * q + 4], *outs[3 * q:3 * q + 3])

    res, arrived = _grid_call(
        body, name=name, grid=(n,), in_specs=[s for s in specs for _ in range(4)],
        out_specs=[s for s in specs for _ in range(3)],
        out_shape=[jax.ShapeDtypeStruct(w.shape, F32) for w, _, _, _ in quads for _ in range(3)],
        args=tuple(a for quad in quads for a in quad), semantics=("parallel",), exchange=exchange)
    return [list(res[3 * q:3 * q + 3]) for q in range(len(quads))], arrived


def _pack_small(g_pre, g_post, sinks_a, b_f_c, loss_row):
    pad = lambda a: jnp.pad(a.reshape(1, -1).astype(F32), ((0, 0), (0, LANES - a.size)))
    rows = [g_pre.astype(F32).reshape(-1, LANES), g_post.astype(F32).reshape(-1, LANES), pad(sinks_a), pad(b_f_c), loss_row]
    packed = jnp.concatenate(rows, axis=0)
    return jnp.pad(packed, ((0, SMALL_ROWS - packed.shape[0]), (0, 0)))


def _unpack_small(p):
    n = DEPTH * D_MODEL // LANES
    return (p[:n].reshape(DEPTH, D_MODEL), p[n:2 * n].reshape(DEPTH, D_MODEL), p[2 * n, :2 * N_HEADS].reshape(2, N_HEADS),
            p[2 * n + 1, :N_HEADS].reshape(1, N_HEADS), p[2 * n + 2, 0])


def kernel(x, g_pre, g_post, w_in_a, w_out_a, sinks_a, w_in_b, w_out_b, w_in_c, b_f_c, w_out_c, loss_target, m_g_pre, m_g_post, m_w_in_a, m_w_out_a, m_sinks_a, m_w_in_b, m_w_out_b, m_w_in_c, m_b_f_c, m_w_out_c, v_g_pre, v_g_post, v_w_in_a, v_w_out_a, v_sinks_a, v_w_in_b, v_w_out_b, v_w_in_c, v_b_f_c, v_w_out_c):
    big_w = [w_in_a, w_out_a, w_in_b, w_out_b, w_in_c, w_out_c]
    big_m = [m_w_in_a, m_w_out_a, m_w_in_b, m_w_out_b, m_w_in_c, m_w_out_c]
    big_v = [v_w_in_a, v_w_out_a, v_w_in_b, v_w_out_b, v_w_in_c, v_w_out_c]

    chip = 2 * lax.axis_index("x") + lax.axis_index("y")
    place = jnp.stack([chip, lax.axis_index("c")]).astype(jnp.int32)
    by_kind = {0: (w_in_a, w_out_a), 1: (w_in_b, w_out_b), 2: (w_in_c, w_out_c)}
    shards = {}
    for i in range(DEPTH):
        kind, j = _layer_kind(i)
        shards[("in", i)] = by_kind[kind][0][j].astype(BF16)
        shards[("out", i)] = by_kind[kind][1][j].astype(BF16)

    res = _forward_backward(x[0], loss_target[0], g_pre, g_post, sinks_a, b_f_c, shards, chip, place)
    reduced = res["reduced"]
    rest = [("out", 0), ("in", 1), ("out", 1), ("in", 2), ("out", 2)]
    grads = [None] + list(_sibling_join([reduced[k] for k in rest], "grad_sibling_join"))

    small = _unpack_small(_all_reduce_small(
        _pack_small(res["g_pre"], res["g_post"], res["sinks_a"], res["b_f_c"], res["loss"])))
    g_small, loss = small[:4], small[4]

    zero_row = jnp.zeros((1, LANES), F32)
    pk = lambda a: _pack_small(a[0], a[1], a[2], a[3], zero_row)
    sm = _adamw(pk([g_pre, g_post, sinks_a, b_f_c]), pk(g_small), pk([m_g_pre, m_g_post, m_sinks_a, m_b_f_c]),
                pk([v_g_pre, v_g_post, v_sinks_a, v_b_f_c]), "adamw_small")
    sm = [_unpack_small(a)[:4] for a in sm]
    turned = lambda a: jnp.swapaxes(a, 1, 2)
    g_c = lax.optimization_barrier(turned(grads[4]))
    grads[4] = turned(g_c)
    quads = [(w_in_b, grads[2], m_w_in_b, v_w_in_b), (turned(w_in_c), g_c, turned(m_w_in_c), turned(v_w_in_c))]
    (upd_in_b, upd_in_c), arrived = _adamw_many(quads, "adamw_in_b_c", exchange=_ScatterExchange([res["last_sum"]]))
    half = _sum_chips(res["last_sum"], arrived[0], place, "shard_sum_in_l0", 0, 2, into=reduced[("in", 0)])
    grads[0] = _sibling_join([half], "grad_sibling_join_w_in_a")[0]
    bigs = [_adamw(w_in_a, grads[0], m_w_in_a, v_w_in_a, "adamw_w_in_a"),
            _adamw(w_out_a, grads[1], m_w_out_a, v_w_out_a, "adamw_w_out_a"),
            upd_in_b,
            _adamw(w_out_b, grads[3], m_w_out_b, v_w_out_b, "adamw_w_out_b"),
            [turned(o) for o in upd_in_c],
            _adamw(w_out_c, grads[5], m_w_out_c, v_w_out_c, "adamw_w_out_c")]

    def ordered(small4, big6):
        return [small4[0], small4[1], big6[0], big6[1], small4[2], big6[2], big6[3], big6[4], small4[3], big6[5]]

    out = [loss, res["dx"][None], *ordered(g_small, grads)]
    for k in range(3):
        out += ordered(sm[k], [b[k] for b in bigs])
    return tuple(out)
```

```python
import math

import numpy as np
import jax
import jax.numpy as jnp
from jax import lax
from jax.experimental import pallas as pl
from jax.experimental.pallas import tpu as pltpu

F32 = jnp.float32
BF16 = jnp.bfloat16

D_MODEL = 2048
DEPTH = 4
N_HEADS = 32
HEAD_DIM = 64
LANES = 128
N_PAIRS = N_HEADS * HEAD_DIM // LANES
BRANCH = N_HEADS * HEAD_DIM
N_KV_A = 4
KV_A = N_KV_A * HEAD_DIM
WINDOW = 128
NORM_EPS = 1e-6
NEG = -1e30
Q_SCALE = HEAD_DIM ** -0.5

A_QKV = BRANCH + 2 * KV_A
B_QKV = 3 * BRANCH

ADAM_LR = 0.001
ADAM_B1 = 0.9
ADAM_B2 = 0.999
ADAM_EPS = 1e-08
ADAM_WD = 0.01
ADAM_STEP = 10

MESH = pl.DeviceIdType.MESH

_NT = (((1,), (1,)), ((), ()))
_TN = (((0,), (0,)), ((), ()))


def _params(sem=None):
    return pltpu.CompilerParams(dimension_semantics=sem)


def _matmul(a, b, *, out_dtype, name, n=None, b_off=0, tm=1024, tn=1024, col_blocks=None, exchange=None):
    (m, k), nn = a.shape, (n or b.shape[1])
    tm, tn = min(tm, m), min(tn, nn)
    assert m % tm == 0 and nn % tn == 0, (name, m, nn, tm, tn)

    def body(a_ref, b_ref, o_ref):
        p = jnp.dot(a_ref[...], b_ref[...], preferred_element_type=F32)
        o_ref[...] = p.astype(o_ref.dtype).reshape(o_ref.shape)

    in_specs = [pl.BlockSpec((tm, k), lambda i, j: (i, 0)), pl.BlockSpec((k, tn), lambda i, j: (0, j + b_off))]
    if col_blocks is None:
        out_spec = pl.BlockSpec((tm, tn), lambda i, j: (i, j))
        out_shape = jax.ShapeDtypeStruct((m, nn), out_dtype)
    else:
        per = nn // col_blocks // tn
        assert per * tn * col_blocks == nn, (name, nn, tn, col_blocks)
        out_spec = pl.BlockSpec((1, tm, tn), lambda i, j: (j // per, i, j % per))
        out_shape = jax.ShapeDtypeStruct((col_blocks, m, nn // col_blocks), out_dtype)
    (res,), arrived = _grid_call(
        body, name=name, grid=(m // tm, nn // tn), in_specs=in_specs, out_specs=[out_spec], out_shape=[out_shape],
        args=(a, b), semantics=("parallel", "parallel"), exchange=exchange)
    return res if exchange is None else (res, arrived)


ROW_TILE = 256


def _row_call(body, name, ins, outs, *, s):
    tr = min(ROW_TILE, s)
    spec = {"row": lambda sh: pl.BlockSpec((tr, sh[1]), lambda i: (i, 0)),
            "vec": lambda sh: pl.BlockSpec((1, sh[1]), lambda i: (0, 0)),
            "col": lambda sh: pl.BlockSpec((sh[0], tr), lambda i: (0, i))}
    in_specs = [spec[kind](a.shape) for a, kind in ins]
    out_specs = [spec[kind](sh.shape) for sh, kind in outs]
    return pl.pallas_call(
        body, name=name, grid=(s // tr,), in_specs=in_specs, out_specs=out_specs,
        out_shape=[sh for sh, _ in outs],
        compiler_params=_params(("arbitrary",)),
    )(*[a for a, _ in ins])


def _rsqrt_ms(v):
    return lax.rsqrt(jnp.mean(v * v, axis=-1, keepdims=True) + NORM_EPS)


def _rmsnorm_fwd(x, g, name):
    s, d = x.shape

    def body(x_ref, g_ref, h_ref, ht_ref):
        xv = x_ref[...]
        h = xv * _rsqrt_ms(xv) * g_ref[...]
        h_ref[...] = h.astype(BF16)
        ht_ref[...] = h.T.astype(BF16)

    return _row_call(body, name, [(x, "row"), (g, "vec")],
                     [(jax.ShapeDtypeStruct((s, d), BF16), "row"), (jax.ShapeDtypeStruct((d, s), BF16), "col")], s=s)


PROJ_ROWS = 256


def _resident(shape):
    return pl.BlockSpec(shape, lambda i: (0,) * len(shape), pipeline_mode=pl.Buffered(1))


def _gated_out_proj(o, z, w_out, x, g, name):
    s, d = x.shape
    tm = min(PROJ_ROWS, s)

    def body(o_ref, z_ref, w_ref, x_ref, g_ref, xn_ref, y_ref, ut_ref):
        zv = z_ref[...]
        u = o_ref[...] * (zv * jax.nn.sigmoid(zv))
        ut_ref[...] = u.T.astype(BF16)
        y = jnp.dot(u.astype(BF16), w_ref[...], preferred_element_type=F32)
        y_ref[...] = y
        xn_ref[...] = x_ref[...] + y * _rsqrt_ms(y) * g_ref[...]

    row = pl.BlockSpec((tm, d), lambda i: (i, 0))
    return pl.pallas_call(
        body, name=name, grid=(s // tm,),
        in_specs=[row, row, _resident(w_out.shape), row, _resident((1, d))],
        out_specs=[row, row, pl.BlockSpec((d, tm), lambda i: (0, i))],
        out_shape=[jax.ShapeDtypeStruct((s, d), F32), jax.ShapeDtypeStruct((s, d), F32), jax.ShapeDtypeStruct((d, s), BF16)],
        compiler_params=_params(("parallel",)),
    )(o, z, w_out, x, g)


def _gated_out_proj_bwd(dx, y, g, w_out, o, z, name, exchange=None):
    s, d = dx.shape
    tm = min(PROJ_ROWS, s)

    def body(dx_ref, y_ref, g_ref, w_ref, o_ref, z_ref, dy_ref, dg_ref, do_ref, dz_ref):
        dy, dg = _norm_bwd_rows(dx_ref[...], y_ref[...], g_ref[...])
        dyb = dy.astype(BF16)
        dy_ref[...] = dyb

        @pl.when(pl.program_id(0) == 0)
        def _():
            dg_ref[...] = jnp.zeros_like(dg_ref)

        dg_ref[...] += jnp.sum(dg, axis=0, keepdims=True)
        du = lax.dot_general(dyb, w_ref[...], _NT, preferred_element_type=F32)
        zv = z_ref[...]
        sig = jax.nn.sigmoid(zv)
        do_ref[...] = (du * (zv * sig)).astype(BF16)
        dz_ref[...] = (du * o_ref[...] * (sig * (1.0 + zv * (1.0 - sig)))).astype(BF16)

    row = pl.BlockSpec((tm, d), lambda i: (i, 0))
    vec = pl.BlockSpec((1, d), lambda i: (0, 0))
    bf = jax.ShapeDtypeStruct((s, d), BF16)
    return _grid_call(
        body, name=name, grid=(s // tm,),
        in_specs=[row, row, _resident((1, d)), _resident(w_out.shape), row, row],
        out_specs=[row, vec, row, row], out_shape=[bf, jax.ShapeDtypeStruct((1, d), F32), bf, bf],
        args=(dx, y, g, w_out, o, z), semantics=("arbitrary",), exchange=exchange)


IN_BWD_ROWS = 512


def _in_proj_bwd(dproj, w_in, extra, dx, x, g, name, tk, exchange=None):
    s, d = x.shape
    k = dproj.shape[1]
    tm = min(IN_BWD_ROWS, s)
    nk = k // tk
    assert k % tk == 0 and s % tm == 0, (name, k, tk)
    has_extra = extra is not None

    def body(a_ref, b_ref, *rest):
        if has_extra:
            e_ref, rest = rest[0], rest[1:]
        dx_ref, x_ref, g_ref, o_ref, dg_ref, acc_ref = rest
        i, kk = pl.program_id(0), pl.program_id(1)
        p = lax.dot_general(a_ref[...], b_ref[...], _NT, preferred_element_type=F32)

        @pl.when(kk == 0)
        def _():
            acc_ref[...] = p

        @pl.when(kk > 0)
        def _():
            acc_ref[...] += p

        @pl.when((i == 0) & (kk == 0))
        def _():
            dg_ref[...] = jnp.zeros_like(dg_ref)

        @pl.when(kk == nk - 1)
        def _():
            def rows_chunk(c, _):
                r = pl.ds(pl.multiple_of(c * LANES, LANES), LANES)
                dh = acc_ref[r, :] + e_ref[r, :] if has_extra else acc_ref[r, :]
                dv, dg = _norm_bwd_rows(dh, x_ref[r, :], g_ref[...])
                o_ref[r, :] = dx_ref[r, :] + dv
                dg_ref[...] += jnp.sum(dg, axis=0, keepdims=True)
                return 0

            lax.fori_loop(0, tm // LANES, rows_chunk, 0)

    row = pl.BlockSpec((tm, d), lambda i, kk: (i, 0))
    vec = pl.BlockSpec((1, d), lambda i, kk: (0, 0))
    in_specs = [pl.BlockSpec((tm, tk), lambda i, kk: (i, kk)), pl.BlockSpec((d, tk), lambda i, kk: (0, kk))]
    args = [dproj, w_in]
    if has_extra:
        in_specs.append(row)
        args.append(extra)
    return _grid_call(
        body, name=name, grid=(s // tm, nk), in_specs=in_specs + [row, row, vec], out_specs=[row, vec],
        out_shape=[jax.ShapeDtypeStruct((s, d), F32), jax.ShapeDtypeStruct((1, d), F32)],
        args=tuple(args) + (dx, x, g), scratch_shapes=[pltpu.VMEM((tm, d), F32)], semantics=("arbitrary", "arbitrary"),
        exchange=exchange)


def _loss_and_grad(x, target):
    s, d = x.shape

    def body(x_ref, t_ref, dx_ref, l_ref):
        err = x_ref[...] - t_ref[...]
        dx_ref[...] = err * (1.0 / d)
        part = jnp.sum(jnp.sum(err * err, axis=1, keepdims=True), axis=0, keepdims=True) * (0.5 / d)

        @pl.when(pl.program_id(0) == 0)
        def _():
            l_ref[...] = jnp.zeros_like(l_ref)

        l_ref[...] += jnp.broadcast_to(part, l_ref.shape)

    return _row_call(body, "loss_head", [(x, "row"), (target, "row")],
                     [(jax.ShapeDtypeStruct((s, d), F32), "row"),
                      (jax.ShapeDtypeStruct((1, LANES), F32), "vec")], s=s)


def _norm_bwd_rows(dn, v, g):
    r = _rsqrt_ms(v)
    a = dn * g
    dv = r * (a - v * (r * r) * jnp.mean(a * v, axis=-1, keepdims=True))
    return dv, dn * v * r


def _lane_is_first_head():
    return lax.broadcasted_iota(jnp.int32, (1, LANES), 1) < HEAD_DIM


def _bcast_lanes(col):
    return jnp.broadcast_to(col, (col.shape[0], LANES))


def _pair_spec(s, off=0, width=LANES):
    return pl.BlockSpec((s, width), lambda p: (0, p + off))


def _stack_heads(pair, first):
    return jnp.concatenate([jnp.where(first, pair, 0), jnp.where(first, 0, pair)], axis=0).astype(BF16)


def _stacked_mask(t, strict):
    row = lax.broadcasted_iota(jnp.int32, (2 * t, t), 0)
    col = lax.broadcasted_iota(jnp.int32, (2 * t, t), 1)
    query = jnp.where(row >= t, row - t, row)
    return col < query if strict else col <= query


def _steps_in_groups(n, step, carry, widths=(2, 1)):
    done = 0
    for width in widths:
        def group(jj, c, width=width, done=done):
            for k in range(width):
                c = step(done + width * jj + k, c)
            return c

        trips = (n - done) // width
        carry = lax.fori_loop(0, trips, group, carry)
        done = done + width * trips
    return carry


FULL_ATTENTION_WIDTHS = (4, 2, 1)


def _rowsum_heads(prod, first):
    return (jnp.sum(jnp.where(first, prod, 0.0), axis=1, keepdims=True),
            jnp.sum(jnp.where(first, 0.0, prod), axis=1, keepdims=True))


LOG2_E = 1.4426950408889634


def _softplus_parts(z2):
    e = jnp.exp2(-jnp.abs(z2))
    sp2 = jnp.maximum(z2, 0.0) + jnp.log2(1.0 + e)
    r = 1.0 / (1.0 + e)
    return sp2, jnp.where(z2 >= 0, r, e * r)


def _sb_tile(s):
    return min(256, s)


def _attn_b_fwd(qkv, name, exchange=None):
    s = qkv.shape[0]
    t = _sb_tile(s)
    nq = s // t

    def body(q_ref, k_ref, v_ref, o_ref, lt_ref):
        first = _lane_is_first_head()
        before = _stacked_mask(t, strict=True)
        tri = (lax.broadcasted_iota(jnp.int32, (t, t), 0) >= lax.broadcasted_iota(jnp.int32, (t, t), 1)).astype(BF16)

        def tile(j, carry, diag, qs):
            c, acc = carry
            c0 = pl.multiple_of(j * t, t)
            k2 = k_ref[pl.ds(c0, t), :]
            v2 = v_ref[pl.ds(c0, t), :]
            z = lax.dot_general(qs, k2, _NT, preferred_element_type=F32) * LOG2_E
            sp, _ = _softplus_parts(z)
            lf = jnp.where(before, -sp, 0.0) if diag else -sp
            incl = jnp.dot(lf.astype(BF16), tri, preferred_element_type=F32)
            a = jnp.exp2(z + c + incl)
            if diag:
                a = jnp.where(before, a, 0.0)
            pv = jnp.dot(a.astype(BF16), v2, preferred_element_type=F32)
            return c + incl[:, 0:1], acc + jnp.where(first, pv[:t], pv[t:])

        def qblock(i, _):
            r0 = pl.multiple_of(i * t, t)
            qs = _stack_heads(q_ref[pl.ds(r0, t), :] * Q_SCALE, first)
            carry = tile(i, (jnp.zeros((2 * t, 1), F32), jnp.zeros((t, LANES), F32)), True, qs)
            carry = _steps_in_groups(i, lambda j, c: tile(i - 1 - j, c, False, qs), carry, FULL_ATTENTION_WIDTHS)
            o_ref[pl.ds(r0, t), :] = carry[1]
            lt_ref[pl.ds(r0, t), 0:LANES] = _bcast_lanes(carry[0][:t])
            lt_ref[pl.ds(r0, t), LANES:2 * LANES] = _bcast_lanes(carry[0][t:])
            return 0

        lax.fori_loop(0, nq, qblock, 0)

    return _grid_call(
        body, name=name, grid=(N_PAIRS,),
        in_specs=[_pair_spec(s), _pair_spec(s, N_PAIRS), _pair_spec(s, 2 * N_PAIRS)],
        out_specs=[_pair_spec(s), _stat_spec(s)],
        out_shape=[jax.ShapeDtypeStruct((s, BRANCH), F32), jax.ShapeDtypeStruct((s, N_HEADS * LANES), F32)],
        args=(qkv, qkv, qkv), semantics=("parallel",), exchange=exchange)


def _attn_b_bwd(qkv, ltot, do, name, exchange=None):
    s = qkv.shape[0]
    t = _sb_tile(s)
    nq = s // t

    def body(q_ref, k_ref, v_ref, lt_ref, do_ref, dq_ref, dk_ref, dv_ref, dk_acc, dv_acc):
        first = _lane_is_first_head()
        before = _stacked_mask(t, strict=True)
        tri = (lax.broadcasted_iota(jnp.int32, (t, t), 0) <= lax.broadcasted_iota(jnp.int32, (t, t), 1)).astype(BF16)
        dk_acc[...] = jnp.zeros_like(dk_acc)
        dv_acc[...] = jnp.zeros_like(dv_acc)

        def tile(j, carry, diag, qs, dos, lt):
            p_l, p_g, dq_acc = carry
            c0 = pl.multiple_of(j * t, t)
            k2 = k_ref[pl.ds(c0, t), :]
            v2 = v_ref[pl.ds(c0, t), :]
            z = lax.dot_general(qs, k2, _NT, preferred_element_type=F32) * LOG2_E
            sp, sig = _softplus_parts(z)
            lf = jnp.where(before, -sp, 0.0) if diag else -sp
            pref_l = jnp.dot(lf.astype(BF16), tri, preferred_element_type=F32)
            a = jnp.exp2(z + ((lt - p_l) - pref_l + lf))
            if diag:
                a = jnp.where(before, a, 0.0)
            g = a * lax.dot_general(dos, v2, _NT, preferred_element_type=F32)
            pref_g = jnp.dot(g.astype(BF16), tri, preferred_element_type=F32)
            dz = g - sig * (p_g + pref_g)
            if diag:
                dz = jnp.where(before, dz, 0.0)
            dzb = dz.astype(BF16)
            dq = jnp.dot(dzb, k2, preferred_element_type=F32)
            dk_acc[pl.ds(c0, t), :] += lax.dot_general(dzb, qs, _TN, preferred_element_type=F32)
            dv_acc[pl.ds(c0, t), :] += lax.dot_general(a.astype(BF16), dos, _TN, preferred_element_type=F32)
            return p_l + pref_l[:, t - 1:t], p_g + pref_g[:, t - 1:t], dq_acc + jnp.where(first, dq[:t], dq[t:])

        def qblock(i, _):
            r0 = pl.multiple_of(i * t, t)
            qs = _stack_heads(q_ref[pl.ds(r0, t), :] * Q_SCALE, first)
            dos = _stack_heads(do_ref[pl.ds(r0, t), :], first)
            lt = jnp.concatenate([lt_ref[pl.ds(r0, t), 0:1], lt_ref[pl.ds(r0, t), LANES:LANES + 1]], axis=0)
            zero = jnp.zeros((2 * t, 1), F32)
            carry = (zero, zero, jnp.zeros((t, LANES), F32))
            carry = _steps_in_groups(i, lambda j, c: tile(j, c, False, qs, dos, lt), carry, FULL_ATTENTION_WIDTHS)
            carry = tile(i, carry, True, qs, dos, lt)
            dq_ref[pl.ds(r0, t), :] = (carry[2] * Q_SCALE).astype(BF16)
            return 0

        lax.fori_loop(0, nq, qblock, 0)
        dk_ref[...] = dk_acc[...].astype(BF16)
        dv_ref[...] = dv_acc[...].astype(BF16)

    out = jax.ShapeDtypeStruct((s, BRANCH), BF16)
    return _grid_call(
        body, name=name, grid=(N_PAIRS,),
        in_specs=[_pair_spec(s), _pair_spec(s, N_PAIRS), _pair_spec(s, 2 * N_PAIRS), _stat_spec(s), _pair_spec(s)],
        out_specs=[_pair_spec(s)] * 3, out_shape=[out] * 3,
        scratch_shapes=[pltpu.VMEM((s, LANES), F32), pltpu.VMEM((s, LANES), F32)],
        args=(qkv, qkv, qkv, ltot, do), semantics=("parallel",), exchange=exchange)


def _fox_tile(s):
    return min(256, s)


def _stat_spec(s):
    return pl.BlockSpec((s, 2 * LANES), lambda p: (0, p))


def _cum_spec(nt, t):
    return pl.BlockSpec((1, nt, 2, t), lambda p: (p, 0, 0, 0))


def _attn_c_fwd(qkv, cum4, name, exchange=None):
    s = qkv.shape[0]
    t = _fox_tile(s)
    nq = s // t

    def body(q_ref, k_ref, v_ref, c_ref, o_ref, lse_ref):
        first = _lane_is_first_head()
        causal = _stacked_mask(t, strict=False)

        def tile(j, carry, diag, qs):
            c0 = pl.multiple_of(j * t, t)
            k2 = k_ref[pl.ds(c0, t), :]
            v2 = v_ref[pl.ds(c0, t), :]
            cs = c_ref[0, j]
            m_prev, l_prev, acc = carry
            z = lax.dot_general(qs, k2, _NT, preferred_element_type=F32)
            sc = jnp.concatenate([z[:t] - cs[0:1, :], z[t:] - cs[1:2, :]], axis=0)
            if diag:
                sc = jnp.where(causal, sc, NEG)
            m_new = jnp.maximum(m_prev, jnp.max(sc, axis=1, keepdims=True))
            alpha = jnp.exp(m_prev - m_new)
            p = jnp.exp(sc - m_new)
            l_new = alpha * l_prev + jnp.sum(p, axis=1, keepdims=True)
            pv = jnp.dot(p.astype(BF16), v2, preferred_element_type=F32)
            acc = jnp.where(first, acc * alpha[:t] + pv[:t], acc * alpha[t:] + pv[t:])
            return m_new, l_new, acc

        def qblock(i, _):
            r0 = pl.multiple_of(i * t, t)
            qs = _stack_heads(q_ref[pl.ds(r0, t), :] * Q_SCALE, first)
            carry = (jnp.full((2 * t, 1), NEG, F32), jnp.zeros((2 * t, 1), F32), jnp.zeros((t, LANES), F32))
            carry = _steps_in_groups(i, lambda j, c: tile(j, c, False, qs), carry, FULL_ATTENTION_WIDTHS)
            m, l, acc = tile(i, carry, True, qs)
            inv = 1.0 / l
            lse = m + jnp.log(l)
            o_ref[pl.ds(r0, t), :] = acc * jnp.where(first, inv[:t], inv[t:])
            lse_ref[pl.ds(r0, t), 0:LANES] = _bcast_lanes(lse[:t])
            lse_ref[pl.ds(r0, t), LANES:2 * LANES] = _bcast_lanes(lse[t:])
            return 0

        lax.fori_loop(0, nq, qblock, 0)

    return _grid_call(
        body, name=name, grid=(N_PAIRS,),
        in_specs=[_pair_spec(s), _pair_spec(s, N_PAIRS), _pair_spec(s, 2 * N_PAIRS), _cum_spec(nq, t)],
        out_specs=[_pair_spec(s), _stat_spec(s)],
        out_shape=[jax.ShapeDtypeStruct((s, BRANCH), F32), jax.ShapeDtypeStruct((s, N_HEADS * LANES), F32)],
        args=(qkv, qkv, qkv, cum4), semantics=("parallel",), exchange=exchange)


def _attn_c_bwd(qkv, cum4, o, lse, do, name, exchange=None):
    s = qkv.shape[0]
    t = _fox_tile(s)
    nq = s // t

    def body(q_ref, k_ref, v_ref, c_ref, o_ref, lse_ref, do_ref, dq_ref, dk_ref, dv_ref, dc_ref, dk_acc, dv_acc):
        first = _lane_is_first_head()
        causal = _stacked_mask(t, strict=False)
        eye = lax.broadcasted_iota(jnp.int32, (t, t), 0) == lax.broadcasted_iota(jnp.int32, (t, t), 1)
        dk_acc[...] = jnp.zeros_like(dk_acc)
        dv_acc[...] = jnp.zeros_like(dv_acc)
        dc_ref[...] = jnp.zeros_like(dc_ref)

        def tile(j, carry, diag, qs, dos, delta, lse):
            dq_acc, rs = carry
            c0 = pl.multiple_of(j * t, t)
            k2 = k_ref[pl.ds(c0, t), :]
            v2 = v_ref[pl.ds(c0, t), :]
            cs = c_ref[0, j]
            z = lax.dot_general(qs, k2, _NT, preferred_element_type=F32)
            sc = jnp.concatenate([z[:t] - cs[0:1, :], z[t:] - cs[1:2, :]], axis=0)
            p = jnp.exp(sc - lse)
            if diag:
                p = jnp.where(causal, p, 0.0)
            ds = p * (lax.dot_general(dos, v2, _NT, preferred_element_type=F32) - delta)
            dsb = ds.astype(BF16)
            dq = jnp.dot(dsb, k2, preferred_element_type=F32)
            dk_acc[pl.ds(c0, t), :] += lax.dot_general(dsb, qs, _TN, preferred_element_type=F32)
            dv_acc[pl.ds(c0, t), :] += lax.dot_general(p.astype(BF16), dos, _TN, preferred_element_type=F32)
            col_sums = jnp.concatenate([jnp.sum(ds[:t], axis=0, keepdims=True), jnp.sum(ds[t:], axis=0, keepdims=True)], axis=0)
            dc_ref[0, j] = dc_ref[0, j] - col_sums
            return dq_acc + jnp.where(first, dq[:t], dq[t:]), rs + jnp.sum(ds, axis=1, keepdims=True)

        def qblock(i, _):
            r0 = pl.multiple_of(i * t, t)
            do2 = do_ref[pl.ds(r0, t), :]
            qs = _stack_heads(q_ref[pl.ds(r0, t), :] * Q_SCALE, first)
            dos = _stack_heads(do2, first)
            delta = jnp.concatenate(_rowsum_heads(do2.astype(F32) * o_ref[pl.ds(r0, t), :], first), axis=0)
            lse = jnp.concatenate([lse_ref[pl.ds(r0, t), 0:1], lse_ref[pl.ds(r0, t), LANES:LANES + 1]], axis=0)
            carry = (jnp.zeros((t, LANES), F32), jnp.zeros((2 * t, 1), F32))
            carry = _steps_in_groups(i, lambda j, c: tile(j, c, False, qs, dos, delta, lse), carry, FULL_ATTENTION_WIDTHS)
            dq_acc, rs = tile(i, carry, True, qs, dos, delta, lse)
            dq_ref[pl.ds(r0, t), :] = (dq_acc * Q_SCALE).astype(BF16)
            as_row = lambda col_vec: jnp.sum(jnp.where(eye, col_vec, 0.0), axis=0, keepdims=True)
            dc_ref[0, i] = dc_ref[0, i] + jnp.concatenate([as_row(rs[:t]), as_row(rs[t:])], axis=0)
            return 0

        lax.fori_loop(0, nq, qblock, 0)
        dk_ref[...] = dk_acc[...].astype(BF16)
        dv_ref[...] = dv_acc[...].astype(BF16)

    out = jax.ShapeDtypeStruct((s, BRANCH), BF16)
    return _grid_call(
        body, name=name, grid=(N_PAIRS,),
        in_specs=[_pair_spec(s), _pair_spec(s, N_PAIRS), _pair_spec(s, 2 * N_PAIRS), _cum_spec(nq, t),
                  _pair_spec(s), _stat_spec(s), _pair_spec(s)],
        out_specs=[_pair_spec(s)] * 3 + [_cum_spec(nq, t)],
        out_shape=[out] * 3 + [jax.ShapeDtypeStruct(cum4.shape, F32)],
        scratch_shapes=[pltpu.VMEM((s, LANES), F32), pltpu.VMEM((s, LANES), F32)],
        args=(qkv, qkv, qkv, cum4, o, lse, do), semantics=("parallel",), exchange=exchange)


FG_CHUNK = 512


def _tri_dot3(x, t):
    hi = x.astype(BF16)
    r1 = x - hi.astype(F32)
    mid = r1.astype(BF16)
    lo = (r1 - mid.astype(F32)).astype(BF16)
    return (jnp.dot(hi, t, preferred_element_type=F32) + jnp.dot(mid, t, preferred_element_type=F32)
            + jnp.dot(lo, t, preferred_element_type=F32))


def _fgate_fwd(h, wf_t, b_col, name):
    s = h.shape[0]
    c = min(FG_CHUNK, s)

    def body(h_ref, w_ref, b_ref, xf_ref, cum_ref, carry_ref):
        @pl.when(pl.program_id(0) == 0)
        def _():
            carry_ref[...] = jnp.zeros_like(carry_ref)

        xf = lax.dot_general(w_ref[...], h_ref[...], _NT, preferred_element_type=F32) + b_ref[:, 0:1]
        xf_ref[...] = xf
        logf = jnp.minimum(xf, 0.0) - jnp.log(1.0 + jnp.exp(-jnp.abs(xf)))
        row = lax.broadcasted_iota(jnp.int32, (c, c), 0)
        col = lax.broadcasted_iota(jnp.int32, (c, c), 1)
        cum = _tri_dot3(logf, (row <= col).astype(BF16)) + carry_ref[:, 0:1]
        cum_ref[...] = cum
        carry_ref[...] = _bcast_lanes(cum[:, c - 1:c])

    out = jax.ShapeDtypeStruct((N_HEADS, s), F32)
    return pl.pallas_call(
        body, name=name, grid=(s // c,),
        in_specs=[pl.BlockSpec((c, D_MODEL), lambda i: (i, 0)),
                  pl.BlockSpec((N_HEADS, D_MODEL), lambda i: (0, 0)),
                  pl.BlockSpec((N_HEADS, LANES), lambda i: (0, 0))],
        out_specs=[pl.BlockSpec((N_HEADS, c), lambda i: (0, i))] * 2,
        out_shape=[out, out],
        scratch_shapes=[pltpu.VMEM((N_HEADS, LANES), F32)],
        compiler_params=_params(("arbitrary",)),
    )(h, wf_t, b_col)


def _fgate_bwd(dcum, xf, h, wf_t, name):
    s = h.shape[0]
    c = min(FG_CHUNK, s)
    n = s // c

    def body(dc_ref, xf_ref, h_ref, w_ref, dw_ref, dh_ref, db_ref, carry_ref):
        @pl.when(pl.program_id(0) == 0)
        def _():
            carry_ref[...] = jnp.zeros_like(carry_ref)
            dw_ref[...] = jnp.zeros_like(dw_ref)
            db_ref[...] = jnp.zeros_like(db_ref)

        row = lax.broadcasted_iota(jnp.int32, (c, c), 0)
        col = lax.broadcasted_iota(jnp.int32, (c, c), 1)
        dlogf = _tri_dot3(dc_ref[...], (row >= col).astype(BF16)) + carry_ref[:, 0:1]
        carry_ref[...] = _bcast_lanes(dlogf[:, 0:1])
        xf = xf_ref[...]
        e = jnp.exp(-jnp.abs(xf))
        r = 1.0 / (1.0 + e)
        dxf = dlogf * jnp.where(xf >= 0, e * r, r)
        db_ref[...] += _bcast_lanes(jnp.sum(dxf, axis=1, keepdims=True))
        dxb = dxf.astype(BF16)
        dw_ref[...] += jnp.dot(dxb, h_ref[...], preferred_element_type=F32)
        dh_ref[...] = lax.dot_general(dxb, w_ref[...], _TN, preferred_element_type=F32)

    rev = lambda i: n - 1 - i
    return pl.pallas_call(
        body, name=name, grid=(n,),
        in_specs=[pl.BlockSpec((N_HEADS, c), lambda i: (0, rev(i))),
                  pl.BlockSpec((N_HEADS, c), lambda i: (0, rev(i))),
                  pl.BlockSpec((c, D_MODEL), lambda i: (rev(i), 0)),
                  pl.BlockSpec((N_HEADS, D_MODEL), lambda i: (0, 0))],
        out_specs=[pl.BlockSpec((N_HEADS, D_MODEL), lambda i: (0, 0)),
                   pl.BlockSpec((c, D_MODEL), lambda i: (rev(i), 0)),
                   pl.BlockSpec((N_HEADS, LANES), lambda i: (0, 0))],
        out_shape=[jax.ShapeDtypeStruct((N_HEADS, D_MODEL), F32), jax.ShapeDtypeStruct((s, D_MODEL), F32),
                   jax.ShapeDtypeStruct((N_HEADS, LANES), F32)],
        scratch_shapes=[pltpu.VMEM((N_HEADS, LANES), F32)],
        compiler_params=_params(("arbitrary",)),
    )(dcum, xf, h, wf_t)


def _to_cum4(v, t):
    s = v.shape[1]
    return v.reshape(N_PAIRS, 2, s // t, t).transpose(0, 2, 1, 3)


def _from_cum4(v4):
    p, nt, two, t = v4.shape
    return v4.transpose(0, 2, 1, 3).reshape(p * two, nt * t)


def _alibi_slopes():
    return (2.0 ** (-8.0 * np.arange(1, N_HEADS + 1, dtype=np.float32) / N_HEADS)).astype(np.float32)


def _per_head_lanes(v):
    return jnp.repeat(v.astype(F32).reshape(N_PAIRS, 1, 2), LANES, axis=2)


def _attn_a_specs(s):
    q = _pair_spec(s)
    k = pl.BlockSpec((s, LANES), lambda p: (0, N_PAIRS + p // 8))
    v = pl.BlockSpec((s, LANES), lambda p: (0, N_PAIRS + KV_A // LANES + p // 8))
    head = pl.BlockSpec((1, 1, 2 * LANES), lambda p: (p, 0, 0))
    return q, k, v, head


def _attn_a_geometry(p, slope_ref, sink_ref):
    kv_half = (p // 4) % 2
    kv_first = kv_half == 0
    lane_first = _lane_is_first_head()
    kv_lanes = (lax.broadcasted_iota(jnp.int32, (1, LANES), 1) // HEAD_DIM) == kv_half
    row = lax.broadcasted_iota(jnp.int32, (2 * WINDOW, 2 * WINDOW), 0)
    cj = lax.broadcasted_iota(jnp.int32, (2 * WINDOW, 2 * WINDOW), 1)
    second = row >= WINDOW
    dist = WINDOW + jnp.where(second, row - WINDOW, row) - cj
    valid = (dist >= 0) & (dist < WINDOW)
    per_row = lambda ref: jnp.where(second[:, 0:1], ref[0, :, LANES:LANES + 1], ref[0, :, 0:1])
    return kv_first, lane_first, kv_lanes, per_row(slope_ref) * dist.astype(F32), valid, per_row(sink_ref)


def _swap_halves(x):
    return pltpu.roll(x, HEAD_DIM, 1)


def _attn_a_fwd(qkv, slopes, sinks, name, exchange=None):
    s = qkv.shape[0]
    nb = s // WINDOW

    def body(q_ref, k_ref, v_ref, sl_ref, sk_ref, o_ref, lse_ref):
        kv_first, lane_first, kv_lanes, bias, valid, sink = _attn_a_geometry(pl.program_id(0), sl_ref, sk_ref)

        def block(r0, k0, width):
            q2 = q_ref[pl.ds(r0, WINDOW), :].astype(F32) * Q_SCALE
            q2r = _swap_halves(q2)
            xs = jnp.concatenate([jnp.where(kv_first, q2, q2r), jnp.where(kv_first, q2r, q2)], axis=0).astype(BF16)
            km = jnp.where(kv_lanes, k_ref[pl.ds(k0, width), :], 0).astype(BF16)
            vm = jnp.where(kv_lanes, v_ref[pl.ds(k0, width), :], 0).astype(BF16)
            sc = lax.dot_general(xs, km, _NT, preferred_element_type=F32) - bias[:, 2 * WINDOW - width:]
            sc = jnp.where(valid[:, 2 * WINDOW - width:], sc, NEG)
            m = jnp.maximum(jnp.max(sc, axis=1, keepdims=True), sink)
            pr = jnp.exp(sc - m)
            l = jnp.sum(pr, axis=1, keepdims=True) + jnp.exp(sink - m)
            os = jnp.dot(pr.astype(BF16), vm, preferred_element_type=F32) * (1.0 / l)
            lse = m + jnp.log(l)
            lse_ref[pl.ds(r0, WINDOW), 0:LANES] = _bcast_lanes(lse[:WINDOW])
            lse_ref[pl.ds(r0, WINDOW), LANES:2 * LANES] = _bcast_lanes(lse[WINDOW:])
            oa = jnp.where(kv_first, os[:WINDOW], _swap_halves(os[:WINDOW]))
            ob = jnp.where(kv_first, _swap_halves(os[WINDOW:]), os[WINDOW:])
            o_ref[pl.ds(r0, WINDOW), :] = jnp.where(lane_first, oa, ob)

        block(0, 0, WINDOW)

        def loop(n, _):
            r0 = pl.multiple_of(n * WINDOW, WINDOW)
            block(r0, pl.multiple_of(r0 - WINDOW, WINDOW), 2 * WINDOW)
            return 0

        _steps_in_groups(nb - 1, lambda n, c: loop(n + 1, c), 0)

    q, k, v, head = _attn_a_specs(s)
    return _grid_call(
        body, name=name, grid=(N_PAIRS,),
        in_specs=[q, k, v, head, head],
        out_specs=[_pair_spec(s), _stat_spec(s)],
        out_shape=[jax.ShapeDtypeStruct((s, BRANCH), F32), jax.ShapeDtypeStruct((s, N_HEADS * LANES), F32)],
        args=(qkv, qkv, qkv, slopes, sinks), semantics=("parallel",), exchange=exchange)


def _attn_a_bwd(qkv, slopes, sinks, o, lse, do, name, exchange=None):
    s = qkv.shape[0]
    nb = s // WINDOW

    def body(q_ref, k_ref, v_ref, sl_ref, sk_ref, o_ref, lse_ref, do_ref, dq_ref, dk_ref, dv_ref, dsk_ref):
        p_id = pl.program_id(0)
        kv_first, lane_first, kv_lanes, bias, valid, sink = _attn_a_geometry(p_id, sl_ref, sk_ref)

        @pl.when(p_id % 8 == 0)
        def _():
            dk_ref[...] = jnp.zeros_like(dk_ref)
            dv_ref[...] = jnp.zeros_like(dv_ref)

        def align(v2):
            v2r = _swap_halves(v2)
            both = jnp.concatenate([jnp.where(kv_first, v2, v2r), jnp.where(kv_first, v2r, v2)], axis=0)
            return jnp.where(kv_lanes, both, 0.0).astype(BF16)

        def block(r0, k0, width, sink_sum):
            xq = align(q_ref[pl.ds(r0, WINDOW), :].astype(F32) * Q_SCALE)
            do2 = do_ref[pl.ds(r0, WINDOW), :].astype(F32)
            xdo = align(do2)
            delta = jnp.concatenate(_rowsum_heads(do2 * o_ref[pl.ds(r0, WINDOW), :], lane_first), axis=0)
            lse = jnp.concatenate([lse_ref[pl.ds(r0, WINDOW), 0:1], lse_ref[pl.ds(r0, WINDOW), LANES:LANES + 1]], axis=0)
            km = jnp.where(kv_lanes, k_ref[pl.ds(k0, width), :], 0).astype(BF16)
            vm = jnp.where(kv_lanes, v_ref[pl.ds(k0, width), :], 0).astype(BF16)
            sc = lax.dot_general(xq, km, _NT, preferred_element_type=F32) - bias[:, 2 * WINDOW - width:]
            pr = jnp.where(valid[:, 2 * WINDOW - width:], jnp.exp(sc - lse), 0.0)
            ds = pr * (lax.dot_general(xdo, vm, _NT, preferred_element_type=F32) - delta)
            dsb = ds.astype(BF16)
            dq_al = jnp.dot(dsb, km, preferred_element_type=F32)
            dk_ref[pl.ds(k0, width), :] += lax.dot_general(dsb, xq, _TN, preferred_element_type=F32)
            dv_ref[pl.ds(k0, width), :] += lax.dot_general(pr.astype(BF16), xdo, _TN, preferred_element_type=F32)
            dqa = jnp.where(kv_first, dq_al[:WINDOW], _swap_halves(dq_al[:WINDOW]))
            dqb = jnp.where(kv_first, _swap_halves(dq_al[WINDOW:]), dq_al[WINDOW:])
            dq_ref[pl.ds(r0, WINDOW), :] = (jnp.where(lane_first, dqa, dqb) * Q_SCALE).astype(BF16)
            return sink_sum + jnp.exp(sink - lse) * delta

        sink_sum = block(0, 0, WINDOW, jnp.zeros((2 * WINDOW, 1), F32))

        def loop(n, c):
            r0 = pl.multiple_of(n * WINDOW, WINDOW)
            return block(r0, pl.multiple_of(r0 - WINDOW, WINDOW), 2 * WINDOW, c)

        sink_sum = _steps_in_groups(nb - 1, lambda n, c: loop(n + 1, c), sink_sum, FULL_ATTENTION_WIDTHS)
        dsk_ref[0, :, 0:LANES] = jnp.broadcast_to(-jnp.sum(sink_sum[:WINDOW], axis=0, keepdims=True), (1, LANES))
        dsk_ref[0, :, LANES:2 * LANES] = jnp.broadcast_to(-jnp.sum(sink_sum[WINDOW:], axis=0, keepdims=True), (1, LANES))

    q, k, v, head = _attn_a_specs(s)
    kv_out = pl.BlockSpec((s, LANES), lambda p: (0, p // 8))
    return _grid_call(
        body, name=name, grid=(N_PAIRS,),
        in_specs=[q, k, v, head, head, _pair_spec(s), _stat_spec(s), _pair_spec(s)],
        out_specs=[_pair_spec(s), kv_out, kv_out, head],
        out_shape=[jax.ShapeDtypeStruct((s, BRANCH), BF16), jax.ShapeDtypeStruct((s, KV_A), F32),
                   jax.ShapeDtypeStruct((s, KV_A), F32), jax.ShapeDtypeStruct((N_PAIRS, 1, 2 * LANES), F32)],
        args=(qkv, qkv, qkv, slopes, sinks, o, lse, do), semantics=("arbitrary",), exchange=exchange)


def _layer_kind(i):
    return i % 3, i // 3


GATHER_FIRST = [("in", 0)]
GATHER_BEHIND = {("qkv", 0): [("out", 0)], ("attn", 0): [("in", 1)], ("attn", 1): [("out", 1), ("in", 2), ("out", 2)],
                 ("attn", 2): [("in", 3), ("out", 3)]}


def _forward_backward(x, target, g_pre, g_post, sinks_a, b_f_c, shards, chip, place):
    s = x.shape[0]
    slopes = _per_head_lanes(jnp.asarray(_alibi_slopes()))
    w_in, w_out, wf_t = {}, {}, {}

    def lands_side_by_side(key):
        return key[0] == "in" and shards[key].shape[1] % LANES == 0

    def gather(keys):
        return _GatherExchange([shards[k] for k in keys], [lands_side_by_side(k) for k in keys])

    def deliver(keys, gathered):
        for key, g in zip(keys, gathered):
            side, layer = key
            sh = shards[key]
            if side == "out":
                g = lax.dynamic_update_slice(g, sh[None], (chip, 0, 0))
                w_out[layer] = g.reshape(4 * sh.shape[0], sh.shape[1])
            elif lands_side_by_side(key):
                w_in[layer] = _place_columns(g, sh, chip, f"own_block_in_l{layer}")
            else:
                g = lax.dynamic_update_slice(g, sh[None], (chip, 0, 0))
                w = g.transpose(1, 0, 2).reshape(sh.shape[0], 4 * sh.shape[1])
                w_in[layer], wf_t[layer] = lax.optimization_barrier((w[:, :4 * BRANCH], w[:, 4 * BRANCH:].T))

    deliver(GATHER_FIRST, _exchange_call(gather(GATHER_FIRST), "gather_first_weights"))
    saved = []
    for i in range(DEPTH):
        kind, j = _layer_kind(i)
        tag = f"l{i}"
        w = w_in[i]
        nqkv = A_QKV if kind == 0 else B_QKV
        tn = 512 if kind == 0 else 1024
        h, h_t = _rmsnorm_fwd(x, g_pre[i:i + 1], f"prenorm_{tag}")
        behind = GATHER_BEHIND.get(("qkv", i))
        qkv = _matmul(h, w, out_dtype=BF16, name=f"inproj_qkv_{tag}", n=nqkv, tn=tn,
                      exchange=gather(behind) if behind else None)
        if behind:
            qkv, arrived = qkv
            deliver(behind, arrived)
        z = _matmul(h, w, out_dtype=F32, name=f"inproj_gate_{tag}", n=BRANCH, b_off=nqkv // tn, tn=tn)
        behind = GATHER_BEHIND.get(("attn", i))
        exchange = gather(behind) if behind else None
        if kind == 0:
            sink_l = _per_head_lanes(sinks_a[j])
            (o, lse), arrived = _attn_a_fwd(qkv, slopes, sink_l, f"attn_a_fwd_{tag}", exchange)
            extra = (sink_l, lse)
        elif kind == 1:
            (o, extra), arrived = _attn_b_fwd(qkv, f"attn_b_fwd_{tag}", exchange)
        else:
            b_col = jnp.broadcast_to(b_f_c[j].astype(F32)[:, None], (N_HEADS, LANES))
            xf, cum = _fgate_fwd(h, wf_t[i], b_col, f"fgate_fwd_{tag}")
            cum4 = _to_cum4(cum, _fox_tile(s))
            (o, lse), arrived = _attn_c_fwd(qkv, cum4, f"attn_c_fwd_{tag}", exchange)
            extra = (xf, cum4, lse)
        if behind:
            deliver(behind, arrived)
        x_next, y, u_t = _gated_out_proj(o, z, w_out[i], x, g_post[i:i + 1], f"outproj_{tag}")
        saved.append((x, h, h_t, qkv, z, o, u_t, y, extra))
        x = x_next

    dx, loss_part = _loss_and_grad(x, target)

    d_g_pre, d_g_post = [None] * DEPTH, [None] * DEPTH
    d_sinks = [None, None]
    d_b_f = None
    reduced = {}
    pending = None

    def finish_reduce(layer, side, own, arr):
        kind, j = _layer_kind(layer)
        reduced[(side, kind)] = _sum_chips(own, arr, place, f"shard_sum_{side}_l{layer}", j, 2 if kind == 0 else 1,
                                           into=reduced.get((side, kind)))

    for i in reversed(range(DEPTH)):
        kind, j = _layer_kind(i)
        tag = f"l{i}"
        x_in, h, h_t, qkv, z, o, u_t, y, extra = saved[i]
        tn = 512 if kind == 0 else 1024
        (dy, d_g_post[i], do, dz), _ = _gated_out_proj_bwd(dx, y, g_post[i:i + 1], w_out[i], o, z, f"outproj_bwd_{tag}")
        dw_out = _matmul(u_t, dy, out_dtype=BF16, name=f"dw_out_{tag}")
        dw_out = dw_out.reshape(4, dw_out.shape[0] // 4, dw_out.shape[1])
        dh_f = None
        exchange = _SiblingExchange([dw_out])
        if pending:
            exchange = _BothExchanges(exchange, _ScatterExchange([pending[1]]))
        if kind == 0:
            sink_l, lse = extra
            (dq, dk, dv, dsk), arrived = _attn_a_bwd(qkv, slopes, sink_l, o, lse, do, f"attn_a_bwd_{tag}", exchange)
            d_sinks[j] = dsk[:, 0, ::LANES].reshape(N_HEADS)
            parts = [dq, dk.astype(BF16), dv.astype(BF16), dz]
        elif kind == 1:
            (dq, dk, dv), arrived = _attn_b_bwd(qkv, extra, do, f"attn_b_bwd_{tag}", exchange)
            parts = [dq, dk, dv, dz]
        else:
            xf, cum4, lse = extra
            (dq, dk, dv, dcum4), arrived = _attn_c_bwd(qkv, cum4, o, lse, do, f"attn_c_bwd_{tag}", exchange)
            d_wf_t, dh_f, db = _fgate_bwd(_from_cum4(dcum4), xf, h, wf_t[i], f"fgate_bwd_{tag}")
            d_b_f = db[:, 0]
            parts = [dq, dk, dv, dz]
        sum_out = _add_pairs(dw_out, arrived[0], place, f"chip_sum_out_{tag}")
        if pending:
            finish_reduce(pending[0], "in", pending[1], arrived[1])
        dproj = jnp.concatenate(parts, axis=1)
        scatter_out = _ScatterExchange([sum_out])
        if kind == 2:
            dw_in, arrived = _matmul(h_t, dproj, out_dtype=F32, name=f"dw_in_{tag}", tn=tn, exchange=scatter_out)
            dw_in = jnp.concatenate([dw_in, d_wf_t.T], axis=1)
            dw_in = dw_in.reshape(dw_in.shape[0], 4, dw_in.shape[1] // 4).transpose(1, 0, 2).astype(BF16)
        else:
            dw_in, arrived = _matmul(h_t, dproj, out_dtype=BF16, name=f"dw_in_{tag}", col_blocks=4,
                                     tn=1152 if kind == 0 else 1024, exchange=scatter_out)
        finish_reduce(i, "out", sum_out, arrived[0])
        (dx, d_g_pre[i]), (their_in,) = _in_proj_bwd(
            dproj, w_in[i], dh_f, dx, x_in, g_pre[i:i + 1], f"inproj_bwd_{tag}", 1536 if kind == 0 else 1024,
            exchange=_SiblingExchange([dw_in]))
        pending = (i, _add_pairs(dw_in, their_in, place, f"chip_sum_in_{tag}"))

    return dict(loss=loss_part, dx=dx, g_pre=jnp.concatenate(d_g_pre, axis=0), g_post=jnp.concatenate(d_g_post, axis=0),
                sinks_a=jnp.stack(d_sinks), b_f_c=d_b_f[None, :], reduced=reduced, last_sum=pending[1])


def _place():
    x, y, c = lax.axis_index("x"), lax.axis_index("y"), lax.axis_index("c")
    others = [(1 - x, y), (x, 1 - y), (1 - x, 1 - y)]
    return x, y, c, others


def _half_rows(ref_rows, which):
    half = ref_rows // 2
    return pl.ds(pl.multiple_of(which * half, half), half)


def _remote(src, dst, sems, k, device):
    send, recv = sems
    return pltpu.make_async_remote_copy(src_ref=src, dst_ref=dst, send_sem=send.at[k], recv_sem=recv.at[k],
                                        device_id=device, device_id_type=MESH)


def _hbm_call(body, name, ins, out_shapes, n_remote, aliases=None):
    any_spec = pl.BlockSpec(memory_space=pl.ANY)
    return pl.pallas_call(
        body, name=name, in_specs=[any_spec] * len(ins), out_specs=[any_spec] * len(out_shapes),
        out_shape=out_shapes, input_output_aliases=aliases or {},
        scratch_shapes=[pltpu.SemaphoreType.DMA((n_remote,)), pltpu.SemaphoreType.DMA((n_remote,))],
    )(*ins)


class _GatherExchange:
    SEMS = 8

    def __init__(self, shards, side_by_side):
        self.ins = list(shards)
        self.side_by_side = list(side_by_side)
        self.out_shapes = [jax.ShapeDtypeStruct((a.shape[0], 4 * a.shape[1]) if wide else (4,) + a.shape, a.dtype)
                           for a, wide in zip(shards, side_by_side)]
        self.n_sems = self.SEMS * len(shards)
        self.aliases = {}

    def _copies(self, ins, outs, sems):
        x, y, c, _ = _place()
        me, diag = 2 * x + y, 2 * (1 - x) + (1 - y)
        nbr = [((1 - x, y, c), 2 * (1 - x) + y), ((x, 1 - y, c), 2 * x + (1 - y))]
        sibling = (x, y, 1 - c)
        table = []
        for w, (src, dst, wide) in enumerate(zip(ins, outs, self.side_by_side)):
            rows, cols = src.shape
            half, quarter = rows // 2, rows // 4

            def slot(chip, core, piece=None, dst=dst, wide=wide, cols=cols, half=half, quarter=quarter):
                start, size = (core * half, half) if piece is None else (core * half + piece * quarter, quarter)
                which = pl.ds(pl.multiple_of(start, quarter), size)
                return dst.at[which, pl.ds(pl.multiple_of(chip * cols, LANES), cols)] if wide else dst.at[chip, which]

            k0 = self.SEMS * w
            cp = lambda s_, d_, k, dev: _remote(s_, d_, sems, k0 + k, dev)
            mine_src = src.at[pl.ds(pl.multiple_of(c * half, half), half)]
            d = dict(
                send=[cp(mine_src, slot(me, c), k, nbr[k][0]) for k in range(2)],
                got=[cp(slot(nbr[k][1], c), slot(nbr[k][1], c), k, nbr[k][0]) for k in range(2)],
                fwd=[cp(slot(nbr[k][1], c, k), slot(nbr[k][1], c, k), 2 + k, nbr[1 - k][0]) for k in range(2)],
                got_fwd=[cp(slot(diag, c, k), slot(diag, c, k), 2 + k, nbr[1 - k][0]) for k in range(2)],
                pass_=[cp(slot(nbr[k][1], c), slot(nbr[k][1], c), 4 + k, sibling) for k in range(2)]
                + [cp(slot(diag, c, k), slot(diag, c, k), 6 + k, sibling) for k in range(2)],
                got_pass=[cp(slot(nbr[k][1], 1 - c), slot(nbr[k][1], 1 - c), 4 + k, sibling) for k in range(2)]
                + [cp(slot(diag, 1 - c, k), slot(diag, 1 - c, k), 6 + k, sibling) for k in range(2)])
            table.append(d)
        return table

    def start(self, ins, outs, sems):
        for d in self._copies(ins, outs, sems):
            for cp in d["send"]:
                cp.start()

    def mid(self, ins, outs, sems):
        for d in self._copies(ins, outs, sems):
            for k in range(2):
                d["got"][k].wait_recv()
                d["fwd"][k].start()
                d["pass_"][k].start()

    def finish(self, ins, outs, sems):
        table = self._copies(ins, outs, sems)
        for d in table:
            for k in range(2):
                d["got_fwd"][k].wait_recv()
                d["pass_"][2 + k].start()
        for d in table:
            for cp in d["got_pass"]:
                cp.wait_recv()
            for cp in d["send"] + d["fwd"] + d["pass_"]:
                cp.wait_send()


class _SemaphoresFrom:
    def __init__(self, ref, start):
        self._ref, self._start = ref, start

    @property
    def at(self):
        return self

    def __getitem__(self, k):
        return self._ref.at[self._start + k]


class _BothExchanges:
    def __init__(self, first, second):
        self.parts = (first, second)
        self.ins = first.ins + second.ins
        self.out_shapes = first.out_shapes + second.out_shapes
        self.n_sems = first.n_sems + second.n_sems
        self.aliases = {}

    def _each(self, phase, ins, outs, sems):
        i0 = o0 = s0 = 0
        for ex in self.parts:
            n_in, n_out = len(ex.ins), len(ex.out_shapes)
            getattr(ex, phase)(ins[i0:i0 + n_in], outs[o0:o0 + n_out], tuple(_SemaphoresFrom(r, s0) for r in sems))
            i0, o0, s0 = i0 + n_in, o0 + n_out, s0 + ex.n_sems

    def start(self, ins, outs, sems):
        self._each("start", ins, outs, sems)

    def mid(self, ins, outs, sems):
        self._each("mid", ins, outs, sems)

    def finish(self, ins, outs, sems):
        self._each("finish", ins, outs, sems)


def _place_columns(wide, block, chip, name):
    rows, cc = block.shape
    tr = min(512, rows)

    def body(c_ref, b_ref, w_ref, o_ref):
        o_ref[...] = b_ref[...]

    return pl.pallas_call(
        body, name=name,
        grid_spec=pltpu.PrefetchScalarGridSpec(
            num_scalar_prefetch=1, grid=(rows // tr,),
            in_specs=[pl.BlockSpec((tr, cc), lambda r, c_ref: (r, 0)), pl.BlockSpec(memory_space=pl.ANY)],
            out_specs=pl.BlockSpec((tr, cc), lambda r, c_ref: (r, c_ref[0]))),
        out_shape=jax.ShapeDtypeStruct(wide.shape, wide.dtype), input_output_aliases={2: 0},
        compiler_params=_params(("parallel",)),
    )(chip.astype(jnp.int32).reshape(1), block, wide)


def _exchange_call(ex, name):
    n_in, n_out = len(ex.ins), len(ex.out_shapes)

    def body(*refs):
        ins, outs, sems = refs[:n_in], refs[n_in:n_in + n_out], refs[n_in + n_out:]
        ex.start(ins, outs, sems)
        ex.mid(ins, outs, sems)
        ex.finish(ins, outs, sems)

    return _hbm_call(body, name, ex.ins, ex.out_shapes, ex.n_sems, aliases=ex.aliases)


def _grid_call(body, *, name, grid, in_specs, out_specs, out_shape, args, scratch_shapes=(), semantics, exchange=None):
    if exchange is None:
        res = pl.pallas_call(body, name=name, grid=grid, in_specs=list(in_specs), out_specs=list(out_specs),
                             out_shape=list(out_shape), scratch_shapes=list(scratch_shapes),
                             compiler_params=_params(semantics))(*args)
        return res, []
    n_in, n_out, n_scr = len(args), len(out_shape), len(scratch_shapes)
    x_in, x_out = len(exchange.ins), len(exchange.out_shapes)
    steps = math.prod(grid)

    def wrapped(*refs):
        core_in, ex_in = refs[:n_in], refs[n_in:n_in + x_in]
        rest = refs[n_in + x_in:]
        core_out, ex_out = rest[:n_out], rest[n_out:n_out + x_out]
        scratch, sems = rest[n_out + x_out:n_out + x_out + n_scr], rest[n_out + x_out + n_scr:]
        step = 0
        for axis, extent in enumerate(grid):
            step = step * extent + pl.program_id(axis)

        @pl.when(step == 0)
        def _():
            exchange.start(ex_in, ex_out, sems)

        body(*core_in, *core_out, *scratch)

        @pl.when(step == max((3 * steps) // 4 - 1, 0))
        def _():
            exchange.mid(ex_in, ex_out, sems)

        @pl.when(step == steps - 1)
        def _():
            exchange.finish(ex_in, ex_out, sems)

    any_spec = pl.BlockSpec(memory_space=pl.ANY)
    res = pl.pallas_call(
        wrapped, name=name, grid=grid,
        in_specs=list(in_specs) + [any_spec] * x_in, out_specs=list(out_specs) + [any_spec] * x_out,
        out_shape=list(out_shape) + list(exchange.out_shapes),
        input_output_aliases={n_in + a: n_out + b for a, b in exchange.aliases.items()},
        scratch_shapes=list(scratch_shapes) + [pltpu.SemaphoreType.DMA((exchange.n_sems,)),
                                               pltpu.SemaphoreType.DMA((exchange.n_sems,))],
        compiler_params=_params(("arbitrary",) * len(grid)),
    )(*args, *exchange.ins)
    return res[:n_out], res[n_out:]


class _SiblingExchange:
    def __init__(self, parts):
        self.ins = list(parts)
        self.out_shapes = [jax.ShapeDtypeStruct((4, a.shape[1] // 2, a.shape[2]), a.dtype) for a in parts]
        self.n_sems = len(parts)
        self.aliases = {}

    def _copies(self, ins, outs, sems):
        x, y, c, _ = _place()
        return [_remote(src.at[:, _half_rows(src.shape[1], 1 - c)], dst, sems, w, (x, y, 1 - c))
                for w, (src, dst) in enumerate(zip(ins, outs))]

    def start(self, ins, outs, sems):
        for cp in self._copies(ins, outs, sems):
            cp.start()

    def mid(self, ins, outs, sems):
        pass

    def finish(self, ins, outs, sems):
        for cp in self._copies(ins, outs, sems):
            cp.wait_recv()
            cp.wait_send()


class _ScatterExchange:
    def __init__(self, sums):
        self.ins = list(sums)
        self.out_shapes = [jax.ShapeDtypeStruct(a.shape, a.dtype) for a in sums]
        self.n_sems = 3 * len(sums)
        self.aliases = {}

    def _copies(self, ins, outs, sems):
        x, y, c, others = _place()
        me = 2 * x + y
        table = []
        for w, (src, dst) in enumerate(zip(ins, outs)):
            for j, (px, py) in enumerate(others):
                there = 2 * px + py
                send = _remote(src.at[there], dst.at[me], sems, 3 * w + j, (px, py, c))
                landed = _remote(dst.at[there], dst.at[there], sems, 3 * w + j, (px, py, c))
                table.append((send, landed))
        return table

    def start(self, ins, outs, sems):
        for send, _ in self._copies(ins, outs, sems):
            send.start()

    def mid(self, ins, outs, sems):
        pass

    def finish(self, ins, outs, sems):
        table = self._copies(ins, outs, sems)
        for _, landed in table:
            landed.wait_recv()
        for send, _ in table:
            send.wait_send()


def _sibling_join(shards, name):
    n = len(shards)

    def body(*refs):
        ins, outs, sems = refs[:n], refs[n:2 * n], refs[2 * n:2 * n + 2]
        x, y, c, _ = _place()
        pend = []
        for w in range(n):
            rows = ins[w].shape[1]
            mine, theirs = _half_rows(rows, c), _half_rows(rows, 1 - c)
            cp = _remote(ins[w].at[:, mine], outs[w].at[:, mine], sems, w, (x, y, 1 - c))
            cp.start()
            pend.append((cp, _remote(ins[w].at[:, theirs], outs[w].at[:, theirs], sems, w, (x, y, 1 - c))))
        for cp, landed in pend:
            landed.wait_recv()
            cp.wait_send()

    out_shapes = [jax.ShapeDtypeStruct(a.shape, a.dtype) for a in shards]
    return _hbm_call(body, name, shards, out_shapes, n, aliases={w: w for w in range(n)})


SMALL_ROWS = 136


def _all_reduce_small(vec):
    def body(v_ref, o_ref, buf, send, recv, loc):
        x, y, c, _ = _place()
        me = 4 * x + 2 * y + c
        lc = pltpu.make_async_copy(v_ref, buf.at[me], loc.at[0])
        lc.start()
        cps = []
        for k in range(1, 8):
            fx, fy, fc = (k >> 2) & 1, (k >> 1) & 1, k & 1
            peer = (x ^ fx, y ^ fy, c ^ fc)
            cp = pltpu.make_async_remote_copy(src_ref=v_ref, dst_ref=buf.at[me], send_sem=send.at[k - 1],
                                              recv_sem=recv.at[k - 1], device_id=peer, device_id_type=MESH)
            cp.start()
            cps.append((cp, 4 * peer[0] + 2 * peer[1] + peer[2]))
        for k, (cp, src) in enumerate(cps):
            pltpu.make_async_remote_copy(src_ref=v_ref, dst_ref=buf.at[src], send_sem=send.at[k], recv_sem=recv.at[k],
                                         device_id=(x, y, c), device_id_type=MESH).wait_recv()
        for cp, _ in cps:
            cp.wait_send()
        lc.wait()
        total = buf[0]
        for k in range(1, 8):
            total = total + buf[k]
        o_ref[...] = total

    vm = pl.BlockSpec(memory_space=pltpu.VMEM)
    return pl.pallas_call(
        body, name="all_reduce_small", in_specs=[vm], out_specs=vm,
        out_shape=jax.ShapeDtypeStruct(vec.shape, F32),
        scratch_shapes=[pltpu.VMEM((8,) + vec.shape, F32), pltpu.SemaphoreType.DMA((7,)),
                        pltpu.SemaphoreType.DMA((7,)), pltpu.SemaphoreType.DMA((1,))],
    )(vec)


SUM_ROWS = 256


def _add_pairs(part, theirs, place, name):
    four, rh, cc = theirs.shape
    tr = min(SUM_ROWS, rh)
    halves = part.reshape(four, 2, rh, cc)

    def body(p_ref, a_ref, b_ref, o_ref):
        o_ref[0] = (a_ref[0, 0].astype(F32) + b_ref[0].astype(F32)).astype(o_ref.dtype)

    spec = pl.BlockSpec((1, tr, cc), lambda k, r, p_ref: (k, r, 0))
    return pl.pallas_call(
        body, name=name,
        grid_spec=pltpu.PrefetchScalarGridSpec(
            num_scalar_prefetch=1, grid=(four, rh // tr),
            in_specs=[pl.BlockSpec((1, 1, tr, cc), lambda k, r, p_ref: (k, p_ref[1], r, 0)), spec], out_specs=spec),
        out_shape=jax.ShapeDtypeStruct(theirs.shape, theirs.dtype),
        compiler_params=_params(("parallel", "parallel")),
    )(place, halves, theirs)


def _sum_chips(own, arrived, place, name, layer, n_layers, into=None):
    four, rh, cc = own.shape
    tr = min(SUM_ROWS, rh)
    nr = rh // tr

    def body(p_ref, own_ref, arr_ref, *rest):
        o_ref = rest[-1]
        x, y = lax.axis_index("x"), lax.axis_index("y")
        tot = own_ref[0].astype(F32)
        for px, py in ((1 - x, y), (x, 1 - y), (1 - x, 1 - y)):
            tot = tot + arr_ref[2 * px + py].astype(F32)
        o_ref[0] = tot

    in_specs = [pl.BlockSpec((1, tr, cc), lambda r, p_ref: (p_ref[0], r, 0)),
                pl.BlockSpec((4, tr, cc), lambda r, p_ref: (0, r, 0))]
    args, aliases = [place, own, arrived], {}
    if into is not None:
        in_specs.append(pl.BlockSpec(memory_space=pl.ANY))
        args.append(into)
        aliases = {3: 0}
    return pl.pallas_call(
        body, name=name,
        grid_spec=pltpu.PrefetchScalarGridSpec(
            num_scalar_prefetch=1, grid=(nr,), in_specs=in_specs,
            out_specs=pl.BlockSpec((1, tr, cc), lambda r, p_ref: (layer, p_ref[1] * nr + r, 0))),
        out_shape=jax.ShapeDtypeStruct((n_layers, 2 * rh, cc), F32), input_output_aliases=aliases,
        compiler_params=_params(("parallel",)),
    )(*args)


ADAM_ROWS = 256


def _adamw(w, g, m, v, name):
    shape = w.shape
    as3 = lambda a: a.reshape((-1,) + shape[-2:])
    layers, rows, cc = as3(w).shape
    by_rows = rows % min(ADAM_ROWS, rows) == 0
    tr, tc = (min(ADAM_ROWS, rows), cc) if by_rows else (rows, ADAM_ROWS)
    assert rows % tr == 0 and cc % tc == 0

    def body(w_ref, g_ref, m_ref, v_ref, d_ref, nm_ref, nv_ref):
        _adamw_update(w_ref, g_ref, m_ref, v_ref, d_ref, nm_ref, nv_ref)

    spec = pl.BlockSpec((1, tr, tc), (lambda l, i: (l, i, 0)) if by_rows else (lambda l, i: (l, 0, i)))
    sh = jax.ShapeDtypeStruct((layers, rows, cc), F32)
    outs = pl.pallas_call(
        body, name=name, grid=(layers, (rows // tr) * (cc // tc)), in_specs=[spec] * 4, out_specs=[spec] * 3,
        out_shape=[sh] * 3,
        compiler_params=_params(("parallel", "parallel")),
    )(as3(w), as3(g), as3(m), as3(v))
    return [o.reshape(shape) for o in outs]


def _adamw_update(w_ref, g_ref, m_ref, v_ref, d_ref, nm_ref, nv_ref):
    c1 = 1.0 - ADAM_B1 ** ADAM_STEP
    c2 = 1.0 - ADAM_B2 ** ADAM_STEP
    gv = g_ref[...]
    nm = ADAM_B1 * m_ref[...] + (1.0 - ADAM_B1) * gv
    nv = ADAM_B2 * v_ref[...] + (1.0 - ADAM_B2) * (gv * gv)
    nm_ref[...] = nm
    nv_ref[...] = nv
    d_ref[...] = -ADAM_LR * ((nm / c1) / (jnp.sqrt(nv / c2) + ADAM_EPS) + ADAM_WD * w_ref[...])


ADAM_MANY_STEPS = 16


def _adamw_many(quads, name, exchange=None):
    n = ADAM_MANY_STEPS
    specs = []
    for w, _, _, _ in quads:
        layers, rows, cc = w.shape
        if rows % (8 * n) == 0:
            specs.append(pl.BlockSpec((layers, rows // n, cc), lambda i: (0, i, 0)))
        else:
            assert cc % (LANES * n) == 0, (name, w.shape)
            specs.append(pl.BlockSpec((layers, rows, cc // n), lambda i: (0, 0, i)))

    def body(*refs):
        ins, outs = refs[:4 * len(quads)], refs[4 * len(quads):]
        for q in range(len(quads)):
            _adamw_update(*ins[4 * q:4 * q + 4], *outs[3 * q:3 * q + 3])

    res, arrived = _grid_call(
        body, name=name, grid=(n,), in_specs=[s for s in specs for _ in range(4)],
        out_specs=[s for s in specs for _ in range(3)],
        out_shape=[jax.ShapeDtypeStruct(w.shape, F32) for w, _, _, _ in quads for _ in range(3)],
        args=tuple(a for quad in quads for a in quad), semantics=("parallel",), exchange=exchange)
    return [list(res[3 * q:3 * q + 3]) for q in range(len(quads))], arrived


def _pack_small(g_pre, g_post, sinks_a, b_f_c, loss_row):
    pad = lambda a: jnp.pad(a.reshape(1, -1).astype(F32), ((0, 0), (0, LANES - a.size)))
    rows = [g_pre.astype(F32).reshape(-1, LANES), g_post.astype(F32).reshape(-1, LANES), pad(sinks_a), pad(b_f_c), loss_row]
    packed = jnp.concatenate(rows, axis=0)
    return jnp.pad(packed, ((0, SMALL_ROWS - packed.shape[0]), (0, 0)))


def _unpack_small(p):
    n = DEPTH * D_MODEL // LANES
    return (p[:n].reshape(DEPTH, D_MODEL), p[n:2 * n].reshape(DEPTH, D_MODEL), p[2 * n, :2 * N_HEADS].reshape(2, N_HEADS),
            p[2 * n + 1, :N_HEADS].reshape(1, N_HEADS), p[2 * n + 2, 0])


def kernel(x, g_pre, g_post, w_in_a, w_out_a, sinks_a, w_in_b, w_out_b, w_in_c, b_f_c, w_out_c, loss_target, m_g_pre, m_g_post, m_w_in_a, m_w_out_a, m_sinks_a, m_w_in_b, m_w_out_b, m_w_in_c, m_b_f_c, m_w_out_c, v_g_pre, v_g_post, v_w_in_a, v_w_out_a, v_sinks_a, v_w_in_b, v_w_out_b, v_w_in_c, v_b_f_c, v_w_out_c):
    big_w = [w_in_a, w_out_a, w_in_b, w_out_b, w_in_c, w_out_c]
    big_m = [m_w_in_a, m_w_out_a, m_w_in_b, m_w_out_b, m_w_in_c, m_w_out_c]
    big_v = [v_w_in_a, v_w_out_a, v_w_in_b, v_w_out_b, v_w_in_c, v_w_out_c]

    chip = 2 * lax.axis_index("x") + lax.axis_index("y")
    place = jnp.stack([chip, lax.axis_index("c")]).astype(jnp.int32)
    by_kind = {0: (w_in_a, w_out_a), 1: (w_in_b, w_out_b), 2: (w_in_c, w_out_c)}
    shards = {}
    for i in range(DEPTH):
        kind, j = _layer_kind(i)
        shards[("in", i)] = by_kind[kind][0][j].astype(BF16)
        shards[("out", i)] = by_kind[kind][1][j].astype(BF16)

    res = _forward_backward(x[0], loss_target[0], g_pre, g_post, sinks_a, b_f_c, shards, chip, place)
    reduced = res["reduced"]
    rest = [("out", 0), ("in", 1), ("out", 1), ("in", 2), ("out", 2)]
    grads = [None] + list(_sibling_join([reduced[k] for k in rest], "grad_sibling_join"))

    small = _unpack_small(_all_reduce_small(
        _pack_small(res["g_pre"], res["g_post"], res["sinks_a"], res["b_f_c"], res["loss"])))
    g_small, loss = small[:4], small[4]

    zero_row = jnp.zeros((1, LANES), F32)
    pk = lambda a: _pack_small(a[0], a[1], a[2], a[3], zero_row)
    sm = _adamw(pk([g_pre, g_post, sinks_a, b_f_c]), pk(g_small), pk([m_g_pre, m_g_post, m_sinks_a, m_b_f_c]),
                pk([v_g_pre, v_g_post, v_sinks_a, v_b_f_c]), "adamw_small")
    sm = [_unpack_small(a)[:4] for a in sm]
    turned = lambda a: jnp.swapaxes(a, 1, 2)
    g_c = lax.optimization_barrier(turned(grads[4]))
    grads[4] = turned(g_c)
    quads = [(w_in_b, grads[2], m_w_in_b, v_w_in_b), (turned(w_in_c), g_c, turned(m_w_in_c), turned(v_w_in_c))]
    (upd_in_b, upd_in_c), arrived = _adamw_many(quads, "adamw_in_b_c", exchange=_ScatterExchange([res["last_sum"]]))
    half = _sum_chips(res["last_sum"], arrived[0], place, "shard_sum_in_l0", 0, 2, into=reduced[("in", 0)])
    grads[0] = _sibling_join([half], "grad_sibling_join_w_in_a")[0]
    bigs = [_adamw(w_in_a, grads[0], m_w_in_a, v_w_in_a, "adamw_w_in_a"),
            _adamw(w_out_a, grads[1], m_w_out_a, v_w_out_a, "adamw_w_out_a"),
            upd_in_b,
            _adamw(w_out_b, grads[3], m_w_out_b, v_w_out_b, "adamw_w_out_b"),
            [turned(o) for o in upd_in_c],
            _adamw(w_out_c, grads[5], m_w_out_c, v_w_out_c, "adamw_w_out_c")]

    def ordered(small4, big6):
        return [small4[0], small4[1], big6[0], big6[1], small4[2], big6[2], big6[3], big6[4], small4[3], big6[5]]

    out = [loss, res["dx"][None], *ordered(g_small, grads)]
    for k in range(3):
        out += ordered(sm[k], [b[k] for b in bigs])
    return tuple(out)
```

```python
import math

import numpy as np
import jax
import jax.numpy as jnp
from jax import lax
from jax.experimental import pallas as pl
from jax.experimental.pallas import tpu as pltpu

F32 = jnp.float32
BF16 = jnp.bfloat16

D_MODEL = 2048
DEPTH = 4
N_HEADS = 32
HEAD_DIM = 64
LANES = 128
N_PAIRS = N_HEADS * HEAD_DIM // LANES
BRANCH = N_HEADS * HEAD_DIM
N_KV_A = 4
KV_A = N_KV_A * HEAD_DIM
WINDOW = 128
NORM_EPS = 1e-6
NEG = -1e30
Q_SCALE = HEAD_DIM ** -0.5

A_QKV = BRANCH + 2 * KV_A
B_QKV = 3 * BRANCH

ADAM_LR = 0.001
ADAM_B1 = 0.9
ADAM_B2 = 0.999
ADAM_EPS = 1e-08
ADAM_WD = 0.01
ADAM_STEP = 10

MESH = pl.DeviceIdType.MESH

_NT = (((1,), (1,)), ((), ()))
_TN = (((0,), (0,)), ((), ()))


def _params(sem=None):
    return pltpu.CompilerParams(dimension_semantics=sem)


def _matmul(a, b, *, out_dtype, name, n=None, b_off=0, tm=1024, tn=1024, col_blocks=None, exchange=None):
    (m, k), nn = a.shape, (n or b.shape[1])
    tm, tn = min(tm, m), min(tn, nn)
    assert m % tm == 0 and nn % tn == 0, (name, m, nn, tm, tn)

    def body(a_ref, b_ref, o_ref):
        p = jnp.dot(a_ref[...], b_ref[...], preferred_element_type=F32)
        o_ref[...] = p.astype(o_ref.dtype).reshape(o_ref.shape)

    in_specs = [pl.BlockSpec((tm, k), lambda i, j: (i, 0)), pl.BlockSpec((k, tn), lambda i, j: (0, j + b_off))]
    if col_blocks is None:
        out_spec = pl.BlockSpec((tm, tn), lambda i, j: (i, j))
        out_shape = jax.ShapeDtypeStruct((m, nn), out_dtype)
    else:
        per = nn // col_blocks // tn
        assert per * tn * col_blocks == nn, (name, nn, tn, col_blocks)
        out_spec = pl.BlockSpec((1, tm, tn), lambda i, j: (j // per, i, j % per))
        out_shape = jax.ShapeDtypeStruct((col_blocks, m, nn // col_blocks), out_dtype)
    (res,), arrived = _grid_call(
        body, name=name, grid=(m // tm, nn // tn), in_specs=in_specs, out_specs=[out_spec], out_shape=[out_shape],
        args=(a, b), semantics=("parallel", "parallel"), exchange=exchange)
    return res if exchange is None else (res, arrived)


ROW_TILE = 256


def _row_call(body, name, ins, outs, *, s, exchange=None):
    tr = min(ROW_TILE, s)
    spec = {"row": lambda sh: pl.BlockSpec((tr, sh[1]), lambda i: (i, 0)),
            "vec": lambda sh: pl.BlockSpec((1, sh[1]), lambda i: (0, 0)),
            "col": lambda sh: pl.BlockSpec((sh[0], tr), lambda i: (0, i))}
    in_specs = [spec[kind](a.shape) for a, kind in ins]
    out_specs = [spec[kind](sh.shape) for sh, kind in outs]
    res, arrived = _grid_call(
        body, name=name, grid=(s // tr,), in_specs=in_specs, out_specs=out_specs, out_shape=[sh for sh, _ in outs],
        args=tuple(a for a, _ in ins), semantics=("arbitrary",), exchange=exchange)
    return res if exchange is None else (res, arrived)


def _rsqrt_ms(v):
    return lax.rsqrt(jnp.mean(v * v, axis=-1, keepdims=True) + NORM_EPS)


def _rmsnorm_fwd(x, g, name, exchange=None):
    s, d = x.shape

    def body(x_ref, g_ref, h_ref, ht_ref):
        xv = x_ref[...]
        h = xv * _rsqrt_ms(xv) * g_ref[...]
        h_ref[...] = h.astype(BF16)
        ht_ref[...] = h.T.astype(BF16)

    return _row_call(body, name, [(x, "row"), (g, "vec")],
                     [(jax.ShapeDtypeStruct((s, d), BF16), "row"), (jax.ShapeDtypeStruct((d, s), BF16), "col")], s=s,
                     exchange=exchange)


PROJ_ROWS = 256


def _resident(shape):
    return pl.BlockSpec(shape, lambda i: (0,) * len(shape), pipeline_mode=pl.Buffered(1))


def _gated_out_proj(o, z, w_out, x, g, name):
    s, d = x.shape
    tm = min(PROJ_ROWS, s)

    def body(o_ref, z_ref, w_ref, x_ref, g_ref, xn_ref, y_ref, ut_ref):
        zv = z_ref[...]
        u = o_ref[...] * (zv * jax.nn.sigmoid(zv))
        ut_ref[...] = u.T.astype(BF16)
        y = jnp.dot(u.astype(BF16), w_ref[...], preferred_element_type=F32)
        y_ref[...] = y
        xn_ref[...] = x_ref[...] + y * _rsqrt_ms(y) * g_ref[...]

    row = pl.BlockSpec((tm, d), lambda i: (i, 0))
    return pl.pallas_call(
        body, name=name, grid=(s // tm,),
        in_specs=[row, row, _resident(w_out.shape), row, _resident((1, d))],
        out_specs=[row, row, pl.BlockSpec((d, tm), lambda i: (0, i))],
        out_shape=[jax.ShapeDtypeStruct((s, d), F32), jax.ShapeDtypeStruct((s, d), F32), jax.ShapeDtypeStruct((d, s), BF16)],
        compiler_params=_params(("parallel",)),
    )(o, z, w_out, x, g)


def _gated_out_proj_bwd(dx, y, g, w_out, o, z, name, exchange=None):
    s, d = dx.shape
    tm = min(PROJ_ROWS, s)

    def body(dx_ref, y_ref, g_ref, w_ref, o_ref, z_ref, dy_ref, dg_ref, do_ref, dz_ref):
        dy, dg = _norm_bwd_rows(dx_ref[...], y_ref[...], g_ref[...])
        dyb = dy.astype(BF16)
        dy_ref[...] = dyb

        @pl.when(pl.program_id(0) == 0)
        def _():
            dg_ref[...] = jnp.zeros_like(dg_ref)

        dg_ref[...] += jnp.sum(dg, axis=0, keepdims=True)
        du = lax.dot_general(dyb, w_ref[...], _NT, preferred_element_type=F32)
        zv = z_ref[...]
        sig = jax.nn.sigmoid(zv)
        do_ref[...] = (du * (zv * sig)).astype(BF16)
        dz_ref[...] = (du * o_ref[...] * (sig * (1.0 + zv * (1.0 - sig)))).astype(BF16)

    row = pl.BlockSpec((tm, d), lambda i: (i, 0))
    vec = pl.BlockSpec((1, d), lambda i: (0, 0))
    bf = jax.ShapeDtypeStruct((s, d), BF16)
    return _grid_call(
        body, name=name, grid=(s // tm,),
        in_specs=[row, row, _resident((1, d)), _resident(w_out.shape), row, row],
        out_specs=[row, vec, row, row], out_shape=[bf, jax.ShapeDtypeStruct((1, d), F32), bf, bf],
        args=(dx, y, g, w_out, o, z), semantics=("arbitrary",), exchange=exchange)


IN_BWD_ROWS = 512


def _in_proj_bwd(dproj, w_in, extra, dx, x, g, name, tk, exchange=None):
    s, d = x.shape
    k = dproj.shape[1]
    tm = min(IN_BWD_ROWS, s)
    nk = k // tk
    assert k % tk == 0 and s % tm == 0, (name, k, tk)
    has_extra = extra is not None

    def body(a_ref, b_ref, *rest):
        if has_extra:
            e_ref, rest = rest[0], rest[1:]
        dx_ref, x_ref, g_ref, o_ref, dg_ref, acc_ref = rest
        i, kk = pl.program_id(0), pl.program_id(1)
        p = lax.dot_general(a_ref[...], b_ref[...], _NT, preferred_element_type=F32)

        @pl.when(kk == 0)
        def _():
            acc_ref[...] = p

        @pl.when(kk > 0)
        def _():
            acc_ref[...] += p

        @pl.when((i == 0) & (kk == 0))
        def _():
            dg_ref[...] = jnp.zeros_like(dg_ref)

        @pl.when(kk == nk - 1)
        def _():
            def rows_chunk(c, _):
                r = pl.ds(pl.multiple_of(c * LANES, LANES), LANES)
                dh = acc_ref[r, :] + e_ref[r, :] if has_extra else acc_ref[r, :]
                dv, dg = _norm_bwd_rows(dh, x_ref[r, :], g_ref[...])
                o_ref[r, :] = dx_ref[r, :] + dv
                dg_ref[...] += jnp.sum(dg, axis=0, keepdims=True)
                return 0

            lax.fori_loop(0, tm // LANES, rows_chunk, 0)

    row = pl.BlockSpec((tm, d), lambda i, kk: (i, 0))
    vec = pl.BlockSpec((1, d), lambda i, kk: (0, 0))
    in_specs = [pl.BlockSpec((tm, tk), lambda i, kk: (i, kk)), pl.BlockSpec((d, tk), lambda i, kk: (0, kk))]
    args = [dproj, w_in]
    if has_extra:
        in_specs.append(row)
        args.append(extra)
    return _grid_call(
        body, name=name, grid=(s // tm, nk), in_specs=in_specs + [row, row, vec], out_specs=[row, vec],
        out_shape=[jax.ShapeDtypeStruct((s, d), F32), jax.ShapeDtypeStruct((1, d), F32)],
        args=tuple(args) + (dx, x, g), scratch_shapes=[pltpu.VMEM((tm, d), F32)], semantics=("arbitrary", "arbitrary"),
        exchange=exchange)


def _loss_and_grad(x, target):
    s, d = x.shape

    def body(x_ref, t_ref, dx_ref, l_ref):
        err = x_ref[...] - t_ref[...]
        dx_ref[...] = err * (1.0 / d)
        part = jnp.sum(jnp.sum(err * err, axis=1, keepdims=True), axis=0, keepdims=True) * (0.5 / d)

        @pl.when(pl.program_id(0) == 0)
        def _():
            l_ref[...] = jnp.zeros_like(l_ref)

        l_ref[...] += jnp.broadcast_to(part, l_ref.shape)

    return _row_call(body, "loss_head", [(x, "row"), (target, "row")],
                     [(jax.ShapeDtypeStruct((s, d), F32), "row"),
                      (jax.ShapeDtypeStruct((1, LANES), F32), "vec")], s=s)


def _norm_bwd_rows(dn, v, g):
    r = _rsqrt_ms(v)
    a = dn * g
    dv = r * (a - v * (r * r) * jnp.mean(a * v, axis=-1, keepdims=True))
    return dv, dn * v * r


def _lane_is_first_head():
    return lax.broadcasted_iota(jnp.int32, (1, LANES), 1) < HEAD_DIM


def _bcast_lanes(col):
    return jnp.broadcast_to(col, (col.shape[0], LANES))


def _pair_spec(s, off=0, width=LANES):
    return pl.BlockSpec((s, width), lambda p: (0, p + off))


def _stack_heads(pair, first):
    return jnp.concatenate([jnp.where(first, pair, 0), jnp.where(first, 0, pair)], axis=0).astype(BF16)


def _stacked_mask(t, strict):
    row = lax.broadcasted_iota(jnp.int32, (2 * t, t), 0)
    col = lax.broadcasted_iota(jnp.int32, (2 * t, t), 1)
    query = jnp.where(row >= t, row - t, row)
    return col < query if strict else col <= query


def _steps_in_groups(n, step, carry, widths=(2, 1)):
    done = 0
    for width in widths:
        def group(jj, c, width=width, done=done):
            for k in range(width):
                c = step(done + width * jj + k, c)
            return c

        trips = (n - done) // width
        carry = lax.fori_loop(0, trips, group, carry)
        done = done + width * trips
    return carry


FULL_ATTENTION_WIDTHS = (4, 2, 1)


def _rowsum_heads(prod, first):
    return (jnp.sum(jnp.where(first, prod, 0.0), axis=1, keepdims=True),
            jnp.sum(jnp.where(first, 0.0, prod), axis=1, keepdims=True))


def _softplus_parts(z):
    e = jnp.exp(-jnp.abs(z))
    sp = jnp.maximum(z, 0.0) + jnp.log(1.0 + e)
    r = 1.0 / (1.0 + e)
    return sp, jnp.where(z >= 0, r, e * r)


def _sb_tile(s):
    return min(256, s)


def _attn_b_fwd(qkv, name, exchange=None):
    s = qkv.shape[0]
    t = _sb_tile(s)
    nq = s // t

    def body(q_ref, k_ref, v_ref, o_ref, lt_ref):
        first = _lane_is_first_head()
        before = _stacked_mask(t, strict=True)
        tri = (lax.broadcasted_iota(jnp.int32, (t, t), 0) >= lax.broadcasted_iota(jnp.int32, (t, t), 1)).astype(BF16)

        def tile(j, carry, diag, qs):
            c, acc = carry
            c0 = pl.multiple_of(j * t, t)
            k2 = k_ref[pl.ds(c0, t), :]
            v2 = v_ref[pl.ds(c0, t), :]
            z = lax.dot_general(qs, k2, _NT, preferred_element_type=F32)
            sp, _ = _softplus_parts(z)
            lf = jnp.where(before, -sp, 0.0) if diag else -sp
            incl = jnp.dot(lf.astype(BF16), tri, preferred_element_type=F32)
            a = jnp.exp(z + c + incl)
            if diag:
                a = jnp.where(before, a, 0.0)
            pv = jnp.dot(a.astype(BF16), v2, preferred_element_type=F32)
            return c + incl[:, 0:1], acc + jnp.where(first, pv[:t], pv[t:])

        def qblock(i, _):
            r0 = pl.multiple_of(i * t, t)
            qs = _stack_heads(q_ref[pl.ds(r0, t), :] * Q_SCALE, first)
            carry = tile(i, (jnp.zeros((2 * t, 1), F32), jnp.zeros((t, LANES), F32)), True, qs)
            carry = _steps_in_groups(i, lambda j, c: tile(i - 1 - j, c, False, qs), carry, FULL_ATTENTION_WIDTHS)
            o_ref[pl.ds(r0, t), :] = carry[1]
            lt_ref[pl.ds(r0, t), 0:LANES] = _bcast_lanes(carry[0][:t])
            lt_ref[pl.ds(r0, t), LANES:2 * LANES] = _bcast_lanes(carry[0][t:])
            return 0

        lax.fori_loop(0, nq, qblock, 0)

    return _grid_call(
        body, name=name, grid=(N_PAIRS,),
        in_specs=[_pair_spec(s), _pair_spec(s, N_PAIRS), _pair_spec(s, 2 * N_PAIRS)],
        out_specs=[_pair_spec(s), _stat_spec(s)],
        out_shape=[jax.ShapeDtypeStruct((s, BRANCH), F32), jax.ShapeDtypeStruct((s, N_HEADS * LANES), F32)],
        args=(qkv, qkv, qkv), semantics=("parallel",), exchange=exchange)


def _attn_b_bwd(qkv, ltot, do, name, exchange=None):
    s = qkv.shape[0]
    t = _sb_tile(s)
    nq = s // t

    def body(q_ref, k_ref, v_ref, lt_ref, do_ref, dq_ref, dk_ref, dv_ref, dk_acc, dv_acc):
        first = _lane_is_first_head()
        before = _stacked_mask(t, strict=True)
        tri = (lax.broadcasted_iota(jnp.int32, (t, t), 0) <= lax.broadcasted_iota(jnp.int32, (t, t), 1)).astype(BF16)
        dk_acc[...] = jnp.zeros_like(dk_acc)
        dv_acc[...] = jnp.zeros_like(dv_acc)

        def tile(j, carry, diag, qs, dos, lt):
            p_l, p_g, dq_acc = carry
            c0 = pl.multiple_of(j * t, t)
            k2 = k_ref[pl.ds(c0, t), :]
            v2 = v_ref[pl.ds(c0, t), :]
            z = lax.dot_general(qs, k2, _NT, preferred_element_type=F32)
            sp, sig = _softplus_parts(z)
            lf = jnp.where(before, -sp, 0.0) if diag else -sp
            pref_l = jnp.dot(lf.astype(BF16), tri, preferred_element_type=F32)
            a = jnp.exp(z + ((lt - p_l) - pref_l + lf))
            if diag:
                a = jnp.where(before, a, 0.0)
            g = a * lax.dot_general(dos, v2, _NT, preferred_element_type=F32)
            pref_g = jnp.dot(g.astype(BF16), tri, preferred_element_type=F32)
            dz = g - sig * (p_g + pref_g)
            if diag:
                dz = jnp.where(before, dz, 0.0)
            dzb = dz.astype(BF16)
            dq = jnp.dot(dzb, k2, preferred_element_type=F32)
            dk_acc[pl.ds(c0, t), :] += lax.dot_general(dzb, qs, _TN, preferred_element_type=F32)
            dv_acc[pl.ds(c0, t), :] += lax.dot_general(a.astype(BF16), dos, _TN, preferred_element_type=F32)
            return p_l + pref_l[:, t - 1:t], p_g + pref_g[:, t - 1:t], dq_acc + jnp.where(first, dq[:t], dq[t:])

        def qblock(i, _):
            r0 = pl.multiple_of(i * t, t)
            qs = _stack_heads(q_ref[pl.ds(r0, t), :] * Q_SCALE, first)
            dos = _stack_heads(do_ref[pl.ds(r0, t), :], first)
            lt = jnp.concatenate([lt_ref[pl.ds(r0, t), 0:1], lt_ref[pl.ds(r0, t), LANES:LANES + 1]], axis=0)
            zero = jnp.zeros((2 * t, 1), F32)
            carry = (zero, zero, jnp.zeros((t, LANES), F32))
            carry = _steps_in_groups(i, lambda j, c: tile(j, c, False, qs, dos, lt), carry, FULL_ATTENTION_WIDTHS)
            carry = tile(i, carry, True, qs, dos, lt)
            dq_ref[pl.ds(r0, t), :] = (carry[2] * Q_SCALE).astype(BF16)
            return 0

        lax.fori_loop(0, nq, qblock, 0)
        dk_ref[...] = dk_acc[...].astype(BF16)
        dv_ref[...] = dv_acc[...].astype(BF16)

    out = jax.ShapeDtypeStruct((s, BRANCH), BF16)
    return _grid_call(
        body, name=name, grid=(N_PAIRS,),
        in_specs=[_pair_spec(s), _pair_spec(s, N_PAIRS), _pair_spec(s, 2 * N_PAIRS), _stat_spec(s), _pair_spec(s)],
        out_specs=[_pair_spec(s)] * 3, out_shape=[out] * 3,
        scratch_shapes=[pltpu.VMEM((s, LANES), F32), pltpu.VMEM((s, LANES), F32)],
        args=(qkv, qkv, qkv, ltot, do), semantics=("parallel",), exchange=exchange)


def _fox_tile(s):
    return min(256, s)


def _stat_spec(s):
    return pl.BlockSpec((s, 2 * LANES), lambda p: (0, p))


def _cum_spec(nt, t):
    return pl.BlockSpec((1, nt, 2, t), lambda p: (p, 0, 0, 0))


def _attn_c_fwd(qkv, cum4, name, exchange=None):
    s = qkv.shape[0]
    t = _fox_tile(s)
    nq = s // t

    def body(q_ref, k_ref, v_ref, c_ref, o_ref, lse_ref):
        first = _lane_is_first_head()
        causal = _stacked_mask(t, strict=False)

        def tile(j, carry, diag, qs):
            c0 = pl.multiple_of(j * t, t)
            k2 = k_ref[pl.ds(c0, t), :]
            v2 = v_ref[pl.ds(c0, t), :]
            cs = c_ref[0, j]
            m_prev, l_prev, acc = carry
            z = lax.dot_general(qs, k2, _NT, preferred_element_type=F32)
            sc = jnp.concatenate([z[:t] - cs[0:1, :], z[t:] - cs[1:2, :]], axis=0)
            if diag:
                sc = jnp.where(causal, sc, NEG)
            m_new = jnp.maximum(m_prev, jnp.max(sc, axis=1, keepdims=True))
            alpha = jnp.exp(m_prev - m_new)
            p = jnp.exp(sc - m_new)
            l_new = alpha * l_prev + jnp.sum(p, axis=1, keepdims=True)
            pv = jnp.dot(p.astype(BF16), v2, preferred_element_type=F32)
            acc = jnp.where(first, acc * alpha[:t] + pv[:t], acc * alpha[t:] + pv[t:])
            return m_new, l_new, acc

        def qblock(i, _):
            r0 = pl.multiple_of(i * t, t)
            qs = _stack_heads(q_ref[pl.ds(r0, t), :] * Q_SCALE, first)
            carry = (jnp.full((2 * t, 1), NEG, F32), jnp.zeros((2 * t, 1), F32), jnp.zeros((t, LANES), F32))
            carry = _steps_in_groups(i, lambda j, c: tile(j, c, False, qs), carry, FULL_ATTENTION_WIDTHS)
            m, l, acc = tile(i, carry, True, qs)
            inv = 1.0 / l
            lse = m + jnp.log(l)
            o_ref[pl.ds(r0, t), :] = acc * jnp.where(first, inv[:t], inv[t:])
            lse_ref[pl.ds(r0, t), 0:LANES] = _bcast_lanes(lse[:t])
            lse_ref[pl.ds(r0, t), LANES:2 * LANES] = _bcast_lanes(lse[t:])
            return 0

        lax.fori_loop(0, nq, qblock, 0)

    return _grid_call(
        body, name=name, grid=(N_PAIRS,),
        in_specs=[_pair_spec(s), _pair_spec(s, N_PAIRS), _pair_spec(s, 2 * N_PAIRS), _cum_spec(nq, t)],
        out_specs=[_pair_spec(s), _stat_spec(s)],
        out_shape=[jax.ShapeDtypeStruct((s, BRANCH), F32), jax.ShapeDtypeStruct((s, N_HEADS * LANES), F32)],
        args=(qkv, qkv, qkv, cum4), semantics=("parallel",), exchange=exchange)


def _attn_c_bwd(qkv, cum4, o, lse, do, name, exchange=None):
    s = qkv.shape[0]
    t = _fox_tile(s)
    nq = s // t

    def body(q_ref, k_ref, v_ref, c_ref, o_ref, lse_ref, do_ref, dq_ref, dk_ref, dv_ref, dc_ref, dk_acc, dv_acc):
        first = _lane_is_first_head()
        causal = _stacked_mask(t, strict=False)
        eye = lax.broadcasted_iota(jnp.int32, (t, t), 0) == lax.broadcasted_iota(jnp.int32, (t, t), 1)
        dk_acc[...] = jnp.zeros_like(dk_acc)
        dv_acc[...] = jnp.zeros_like(dv_acc)
        dc_ref[...] = jnp.zeros_like(dc_ref)

        def tile(j, carry, diag, qs, dos, delta, lse):
            dq_acc, rs = carry
            c0 = pl.multiple_of(j * t, t)
            k2 = k_ref[pl.ds(c0, t), :]
            v2 = v_ref[pl.ds(c0, t), :]
            cs = c_ref[0, j]
            z = lax.dot_general(qs, k2, _NT, preferred_element_type=F32)
            sc = jnp.concatenate([z[:t] - cs[0:1, :], z[t:] - cs[1:2, :]], axis=0)
            p = jnp.exp(sc - lse)
            if diag:
                p = jnp.where(causal, p, 0.0)
            ds = p * (lax.dot_general(dos, v2, _NT, preferred_element_type=F32) - delta)
            dsb = ds.astype(BF16)
            dq = jnp.dot(dsb, k2, preferred_element_type=F32)
            dk_acc[pl.ds(c0, t), :] += lax.dot_general(dsb, qs, _TN, preferred_element_type=F32)
            dv_acc[pl.ds(c0, t), :] += lax.dot_general(p.astype(BF16), dos, _TN, preferred_element_type=F32)
            col_sums = jnp.concatenate([jnp.sum(ds[:t], axis=0, keepdims=True), jnp.sum(ds[t:], axis=0, keepdims=True)], axis=0)
            dc_ref[0, j] = dc_ref[0, j] - col_sums
            return dq_acc + jnp.where(first, dq[:t], dq[t:]), rs + jnp.sum(ds, axis=1, keepdims=True)

        def qblock(i, _):
            r0 = pl.multiple_of(i * t, t)
            do2 = do_ref[pl.ds(r0, t), :]
            qs = _stack_heads(q_ref[pl.ds(r0, t), :] * Q_SCALE, first)
            dos = _stack_heads(do2, first)
            delta = jnp.concatenate(_rowsum_heads(do2.astype(F32) * o_ref[pl.ds(r0, t), :], first), axis=0)
            lse = jnp.concatenate([lse_ref[pl.ds(r0, t), 0:1], lse_ref[pl.ds(r0, t), LANES:LANES + 1]], axis=0)
            carry = (jnp.zeros((t, LANES), F32), jnp.zeros((2 * t, 1), F32))
            carry = _steps_in_groups(i, lambda j, c: tile(j, c, False, qs, dos, delta, lse), carry, FULL_ATTENTION_WIDTHS)
            dq_acc, rs = tile(i, carry, True, qs, dos, delta, lse)
            dq_ref[pl.ds(r0, t), :] = (dq_acc * Q_SCALE).astype(BF16)
            as_row = lambda col_vec: jnp.sum(jnp.where(eye, col_vec, 0.0), axis=0, keepdims=True)
            dc_ref[0, i] = dc_ref[0, i] + jnp.concatenate([as_row(rs[:t]), as_row(rs[t:])], axis=0)
            return 0

        lax.fori_loop(0, nq, qblock, 0)
        dk_ref[...] = dk_acc[...].astype(BF16)
        dv_ref[...] = dv_acc[...].astype(BF16)

    out = jax.ShapeDtypeStruct((s, BRANCH), BF16)
    return _grid_call(
        body, name=name, grid=(N_PAIRS,),
        in_specs=[_pair_spec(s), _pair_spec(s, N_PAIRS), _pair_spec(s, 2 * N_PAIRS), _cum_spec(nq, t),
                  _pair_spec(s), _stat_spec(s), _pair_spec(s)],
        out_specs=[_pair_spec(s)] * 3 + [_cum_spec(nq, t)],
        out_shape=[out] * 3 + [jax.ShapeDtypeStruct(cum4.shape, F32)],
        scratch_shapes=[pltpu.VMEM((s, LANES), F32), pltpu.VMEM((s, LANES), F32)],
        args=(qkv, qkv, qkv, cum4, o, lse, do), semantics=("parallel",), exchange=exchange)


FG_CHUNK = 512


def _tri_dot3(x, t):
    hi = x.astype(BF16)
    r1 = x - hi.astype(F32)
    mid = r1.astype(BF16)
    lo = (r1 - mid.astype(F32)).astype(BF16)
    return (jnp.dot(hi, t, preferred_element_type=F32) + jnp.dot(mid, t, preferred_element_type=F32)
            + jnp.dot(lo, t, preferred_element_type=F32))


def _fgate_fwd(h, wf_t, b_col, name):
    s = h.shape[0]
    c = min(FG_CHUNK, s)

    def body(h_ref, w_ref, b_ref, xf_ref, cum_ref, carry_ref):
        @pl.when(pl.program_id(0) == 0)
        def _():
            carry_ref[...] = jnp.zeros_like(carry_ref)

        xf = lax.dot_general(w_ref[...], h_ref[...], _NT, preferred_element_type=F32) + b_ref[:, 0:1]
        xf_ref[...] = xf
        logf = jnp.minimum(xf, 0.0) - jnp.log(1.0 + jnp.exp(-jnp.abs(xf)))
        row = lax.broadcasted_iota(jnp.int32, (c, c), 0)
        col = lax.broadcasted_iota(jnp.int32, (c, c), 1)
        cum = _tri_dot3(logf, (row <= col).astype(BF16)) + carry_ref[:, 0:1]
        cum_ref[...] = cum
        carry_ref[...] = _bcast_lanes(cum[:, c - 1:c])

    out = jax.ShapeDtypeStruct((N_HEADS, s), F32)
    return pl.pallas_call(
        body, name=name, grid=(s // c,),
        in_specs=[pl.BlockSpec((c, D_MODEL), lambda i: (i, 0)),
                  pl.BlockSpec((N_HEADS, D_MODEL), lambda i: (0, 0)),
                  pl.BlockSpec((N_HEADS, LANES), lambda i: (0, 0))],
        out_specs=[pl.BlockSpec((N_HEADS, c), lambda i: (0, i))] * 2,
        out_shape=[out, out],
        scratch_shapes=[pltpu.VMEM((N_HEADS, LANES), F32)],
        compiler_params=_params(("arbitrary",)),
    )(h, wf_t, b_col)


def _fgate_bwd(dcum, xf, h, wf_t, name):
    s = h.shape[0]
    c = min(FG_CHUNK, s)
    n = s // c

    def body(dc_ref, xf_ref, h_ref, w_ref, dw_ref, dh_ref, db_ref, carry_ref):
        @pl.when(pl.program_id(0) == 0)
        def _():
            carry_ref[...] = jnp.zeros_like(carry_ref)
            dw_ref[...] = jnp.zeros_like(dw_ref)
            db_ref[...] = jnp.zeros_like(db_ref)

        row = lax.broadcasted_iota(jnp.int32, (c, c), 0)
        col = lax.broadcasted_iota(jnp.int32, (c, c), 1)
        dlogf = _tri_dot3(dc_ref[...], (row >= col).astype(BF16)) + carry_ref[:, 0:1]
        carry_ref[...] = _bcast_lanes(dlogf[:, 0:1])
        xf = xf_ref[...]
        e = jnp.exp(-jnp.abs(xf))
        r = 1.0 / (1.0 + e)
        dxf = dlogf * jnp.where(xf >= 0, e * r, r)
        db_ref[...] += _bcast_lanes(jnp.sum(dxf, axis=1, keepdims=True))
        dxb = dxf.astype(BF16)
        dw_ref[...] += jnp.dot(dxb, h_ref[...], preferred_element_type=F32)
        dh_ref[...] = lax.dot_general(dxb, w_ref[...], _TN, preferred_element_type=F32)

    rev = lambda i: n - 1 - i
    return pl.pallas_call(
        body, name=name, grid=(n,),
        in_specs=[pl.BlockSpec((N_HEADS, c), lambda i: (0, rev(i))),
                  pl.BlockSpec((N_HEADS, c), lambda i: (0, rev(i))),
                  pl.BlockSpec((c, D_MODEL), lambda i: (rev(i), 0)),
                  pl.BlockSpec((N_HEADS, D_MODEL), lambda i: (0, 0))],
        out_specs=[pl.BlockSpec((N_HEADS, D_MODEL), lambda i: (0, 0)),
                   pl.BlockSpec((c, D_MODEL), lambda i: (rev(i), 0)),
                   pl.BlockSpec((N_HEADS, LANES), lambda i: (0, 0))],
        out_shape=[jax.ShapeDtypeStruct((N_HEADS, D_MODEL), F32), jax.ShapeDtypeStruct((s, D_MODEL), F32),
                   jax.ShapeDtypeStruct((N_HEADS, LANES), F32)],
        scratch_shapes=[pltpu.VMEM((N_HEADS, LANES), F32)],
        compiler_params=_params(("arbitrary",)),
    )(dcum, xf, h, wf_t)


def _to_cum4(v, t):
    s = v.shape[1]
    return v.reshape(N_PAIRS, 2, s // t, t).transpose(0, 2, 1, 3)


def _from_cum4(v4):
    p, nt, two, t = v4.shape
    return v4.transpose(0, 2, 1, 3).reshape(p * two, nt * t)


def _alibi_slopes():
    return (2.0 ** (-8.0 * np.arange(1, N_HEADS + 1, dtype=np.float32) / N_HEADS)).astype(np.float32)


def _per_head_lanes(v):
    return jnp.repeat(v.astype(F32).reshape(N_PAIRS, 1, 2), LANES, axis=2)


def _attn_a_specs(s):
    q = _pair_spec(s)
    k = pl.BlockSpec((s, LANES), lambda p: (0, N_PAIRS + p // 8))
    v = pl.BlockSpec((s, LANES), lambda p: (0, N_PAIRS + KV_A // LANES + p // 8))
    head = pl.BlockSpec((1, 1, 2 * LANES), lambda p: (p, 0, 0))
    return q, k, v, head


def _attn_a_geometry(p, slope_ref, sink_ref):
    kv_half = (p // 4) % 2
    kv_first = kv_half == 0
    lane_first = _lane_is_first_head()
    kv_lanes = (lax.broadcasted_iota(jnp.int32, (1, LANES), 1) // HEAD_DIM) == kv_half
    row = lax.broadcasted_iota(jnp.int32, (2 * WINDOW, 2 * WINDOW), 0)
    cj = lax.broadcasted_iota(jnp.int32, (2 * WINDOW, 2 * WINDOW), 1)
    second = row >= WINDOW
    dist = WINDOW + jnp.where(second, row - WINDOW, row) - cj
    valid = (dist >= 0) & (dist < WINDOW)
    per_row = lambda ref: jnp.where(second[:, 0:1], ref[0, :, LANES:LANES + 1], ref[0, :, 0:1])
    return kv_first, lane_first, kv_lanes, per_row(slope_ref) * dist.astype(F32), valid, per_row(sink_ref)


def _swap_halves(x):
    return pltpu.roll(x, HEAD_DIM, 1)


def _attn_a_fwd(qkv, slopes, sinks, name, exchange=None):
    s = qkv.shape[0]
    nb = s // WINDOW

    def body(q_ref, k_ref, v_ref, sl_ref, sk_ref, o_ref, lse_ref):
        kv_first, lane_first, kv_lanes, bias, valid, sink = _attn_a_geometry(pl.program_id(0), sl_ref, sk_ref)

        def block(r0, k0, width):
            q2 = q_ref[pl.ds(r0, WINDOW), :].astype(F32) * Q_SCALE
            q2r = _swap_halves(q2)
            xs = jnp.concatenate([jnp.where(kv_first, q2, q2r), jnp.where(kv_first, q2r, q2)], axis=0).astype(BF16)
            km = jnp.where(kv_lanes, k_ref[pl.ds(k0, width), :], 0).astype(BF16)
            vm = jnp.where(kv_lanes, v_ref[pl.ds(k0, width), :], 0).astype(BF16)
            sc = lax.dot_general(xs, km, _NT, preferred_element_type=F32) - bias[:, 2 * WINDOW - width:]
            sc = jnp.where(valid[:, 2 * WINDOW - width:], sc, NEG)
            m = jnp.maximum(jnp.max(sc, axis=1, keepdims=True), sink)
            pr = jnp.exp(sc - m)
            l = jnp.sum(pr, axis=1, keepdims=True) + jnp.exp(sink - m)
            os = jnp.dot(pr.astype(BF16), vm, preferred_element_type=F32) * (1.0 / l)
            lse = m + jnp.log(l)
            lse_ref[pl.ds(r0, WINDOW), 0:LANES] = _bcast_lanes(lse[:WINDOW])
            lse_ref[pl.ds(r0, WINDOW), LANES:2 * LANES] = _bcast_lanes(lse[WINDOW:])
            oa = jnp.where(kv_first, os[:WINDOW], _swap_halves(os[:WINDOW]))
            ob = jnp.where(kv_first, _swap_halves(os[WINDOW:]), os[WINDOW:])
            o_ref[pl.ds(r0, WINDOW), :] = jnp.where(lane_first, oa, ob)

        block(0, 0, WINDOW)

        def loop(n, _):
            r0 = pl.multiple_of(n * WINDOW, WINDOW)
            block(r0, pl.multiple_of(r0 - WINDOW, WINDOW), 2 * WINDOW)
            return 0

        _steps_in_groups(nb - 1, lambda n, c: loop(n + 1, c), 0)

    q, k, v, head = _attn_a_specs(s)
    return _grid_call(
        body, name=name, grid=(N_PAIRS,),
        in_specs=[q, k, v, head, head],
        out_specs=[_pair_spec(s), _stat_spec(s)],
        out_shape=[jax.ShapeDtypeStruct((s, BRANCH), F32), jax.ShapeDtypeStruct((s, N_HEADS * LANES), F32)],
        args=(qkv, qkv, qkv, slopes, sinks), semantics=("parallel",), exchange=exchange)


def _attn_a_bwd(qkv, slopes, sinks, o, lse, do, name, exchange=None):
    s = qkv.shape[0]
    nb = s // WINDOW

    def body(q_ref, k_ref, v_ref, sl_ref, sk_ref, o_ref, lse_ref, do_ref, dq_ref, dk_ref, dv_ref, dsk_ref):
        p_id = pl.program_id(0)
        kv_first, lane_first, kv_lanes, bias, valid, sink = _attn_a_geometry(p_id, sl_ref, sk_ref)

        @pl.when(p_id % 8 == 0)
        def _():
            dk_ref[...] = jnp.zeros_like(dk_ref)
            dv_ref[...] = jnp.zeros_like(dv_ref)

        def align(v2):
            v2r = _swap_halves(v2)
            both = jnp.concatenate([jnp.where(kv_first, v2, v2r), jnp.where(kv_first, v2r, v2)], axis=0)
            return jnp.where(kv_lanes, both, 0.0).astype(BF16)

        def block(r0, k0, width, sink_sum):
            xq = align(q_ref[pl.ds(r0, WINDOW), :].astype(F32) * Q_SCALE)
            do2 = do_ref[pl.ds(r0, WINDOW), :].astype(F32)
            xdo = align(do2)
            delta = jnp.concatenate(_rowsum_heads(do2 * o_ref[pl.ds(r0, WINDOW), :], lane_first), axis=0)
            lse = jnp.concatenate([lse_ref[pl.ds(r0, WINDOW), 0:1], lse_ref[pl.ds(r0, WINDOW), LANES:LANES + 1]], axis=0)
            km = jnp.where(kv_lanes, k_ref[pl.ds(k0, width), :], 0).astype(BF16)
            vm = jnp.where(kv_lanes, v_ref[pl.ds(k0, width), :], 0).astype(BF16)
            sc = lax.dot_general(xq, km, _NT, preferred_element_type=F32) - bias[:, 2 * WINDOW - width:]
            pr = jnp.where(valid[:, 2 * WINDOW - width:], jnp.exp(sc - lse), 0.0)
            ds = pr * (lax.dot_general(xdo, vm, _NT, preferred_element_type=F32) - delta)
            dsb = ds.astype(BF16)
            dq_al = jnp.dot(dsb, km, preferred_element_type=F32)
            dk_ref[pl.ds(k0, width), :] += lax.dot_general(dsb, xq, _TN, preferred_element_type=F32)
            dv_ref[pl.ds(k0, width), :] += lax.dot_general(pr.astype(BF16), xdo, _TN, preferred_element_type=F32)
            dqa = jnp.where(kv_first, dq_al[:WINDOW], _swap_halves(dq_al[:WINDOW]))
            dqb = jnp.where(kv_first, _swap_halves(dq_al[WINDOW:]), dq_al[WINDOW:])
            dq_ref[pl.ds(r0, WINDOW), :] = (jnp.where(lane_first, dqa, dqb) * Q_SCALE).astype(BF16)
            return sink_sum + jnp.exp(sink - lse) * delta

        sink_sum = block(0, 0, WINDOW, jnp.zeros((2 * WINDOW, 1), F32))

        def loop(n, c):
            r0 = pl.multiple_of(n * WINDOW, WINDOW)
            return block(r0, pl.multiple_of(r0 - WINDOW, WINDOW), 2 * WINDOW, c)

        sink_sum = _steps_in_groups(nb - 1, lambda n, c: loop(n + 1, c), sink_sum, FULL_ATTENTION_WIDTHS)
        dsk_ref[0, :, 0:LANES] = jnp.broadcast_to(-jnp.sum(sink_sum[:WINDOW], axis=0, keepdims=True), (1, LANES))
        dsk_ref[0, :, LANES:2 * LANES] = jnp.broadcast_to(-jnp.sum(sink_sum[WINDOW:], axis=0, keepdims=True), (1, LANES))

    q, k, v, head = _attn_a_specs(s)
    kv_out = pl.BlockSpec((s, LANES), lambda p: (0, p // 8))
    return _grid_call(
        body, name=name, grid=(N_PAIRS,),
        in_specs=[q, k, v, head, head, _pair_spec(s), _stat_spec(s), _pair_spec(s)],
        out_specs=[_pair_spec(s), kv_out, kv_out, head],
        out_shape=[jax.ShapeDtypeStruct((s, BRANCH), BF16), jax.ShapeDtypeStruct((s, KV_A), F32),
                   jax.ShapeDtypeStruct((s, KV_A), F32), jax.ShapeDtypeStruct((N_PAIRS, 1, 2 * LANES), F32)],
        args=(qkv, qkv, qkv, slopes, sinks, o, lse, do), semantics=("arbitrary",), exchange=exchange)


def _layer_kind(i):
    return i % 3, i // 3


GATHER_FIRST = [("in", 0)]
GATHER_BEHIND = {("qkv", 0): [("out", 0)], ("attn", 0): [("in", 1)], ("attn", 1): [("out", 1), ("in", 2), ("out", 2)],
                 ("attn", 2): [("in", 3), ("out", 3)]}


def _forward_backward(x, target, g_pre, g_post, sinks_a, b_f_c, shards, chip, place):
    s = x.shape[0]
    slopes = _per_head_lanes(jnp.asarray(_alibi_slopes()))
    w_in, w_out, wf_t = {}, {}, {}

    def lands_side_by_side(key):
        return key[0] == "in" and shards[key].shape[1] % LANES == 0

    def gather(keys):
        return _GatherExchange([shards[k] for k in keys], [lands_side_by_side(k) for k in keys])

    def deliver(keys, gathered):
        for key, g in zip(keys, gathered):
            side, layer = key
            sh = shards[key]
            if side == "out":
                g = lax.dynamic_update_slice(g, sh[None], (chip, 0, 0))
                w_out[layer] = g.reshape(4 * sh.shape[0], sh.shape[1])
            elif lands_side_by_side(key):
                w_in[layer] = _place_columns(g, sh, chip, f"own_block_in_l{layer}")
            else:
                g = lax.dynamic_update_slice(g, sh[None], (chip, 0, 0))
                w = g.transpose(1, 0, 2).reshape(sh.shape[0], 4 * sh.shape[1])
                w_in[layer], wf_t[layer] = lax.optimization_barrier((w[:, :4 * BRANCH], w[:, 4 * BRANCH:].T))

    saved = []
    for i in range(DEPTH):
        kind, j = _layer_kind(i)
        tag = f"l{i}"
        nqkv = A_QKV if kind == 0 else B_QKV
        tn = 512 if kind == 0 else 1024
        if i == 0:
            (h, h_t), arrived = _rmsnorm_fwd(x, g_pre[i:i + 1], f"prenorm_{tag}", gather(GATHER_FIRST))
            deliver(GATHER_FIRST, arrived)
        else:
            h, h_t = _rmsnorm_fwd(x, g_pre[i:i + 1], f"prenorm_{tag}")
        w = w_in[i]
        behind = GATHER_BEHIND.get(("qkv", i))
        qkv = _matmul(h, w, out_dtype=BF16, name=f"inproj_qkv_{tag}", n=nqkv, tn=tn,
                      exchange=gather(behind) if behind else None)
        if behind:
            qkv, arrived = qkv
            deliver(behind, arrived)
        z = _matmul(h, w, out_dtype=F32, name=f"inproj_gate_{tag}", n=BRANCH, b_off=nqkv // tn, tn=tn)
        behind = GATHER_BEHIND.get(("attn", i))
        exchange = gather(behind) if behind else None
        if kind == 0:
            sink_l = _per_head_lanes(sinks_a[j])
            (o, lse), arrived = _attn_a_fwd(qkv, slopes, sink_l, f"attn_a_fwd_{tag}", exchange)
            extra = (sink_l, lse)
        elif kind == 1:
            (o, extra), arrived = _attn_b_fwd(qkv, f"attn_b_fwd_{tag}", exchange)
        else:
            b_col = jnp.broadcast_to(b_f_c[j].astype(F32)[:, None], (N_HEADS, LANES))
            xf, cum = _fgate_fwd(h, wf_t[i], b_col, f"fgate_fwd_{tag}")
            cum4 = _to_cum4(cum, _fox_tile(s))
            (o, lse), arrived = _attn_c_fwd(qkv, cum4, f"attn_c_fwd_{tag}", exchange)
            extra = (xf, cum4, lse)
        if behind:
            deliver(behind, arrived)
        x_next, y, u_t = _gated_out_proj(o, z, w_out[i], x, g_post[i:i + 1], f"outproj_{tag}")
        saved.append((x, h, h_t, qkv, z, o, u_t, y, extra))
        x = x_next

    dx, loss_part = _loss_and_grad(x, target)

    d_g_pre, d_g_post = [None] * DEPTH, [None] * DEPTH
    d_sinks = [None, None]
    d_b_f = None
    reduced = {}
    pending = None

    def finish_reduce(layer, side, own, arr):
        kind, j = _layer_kind(layer)
        reduced[(side, kind)] = _sum_chips(own, arr, place, f"shard_sum_{side}_l{layer}", j, 2 if kind == 0 else 1,
                                           into=reduced.get((side, kind)))

    for i in reversed(range(DEPTH)):
        kind, j = _layer_kind(i)
        tag = f"l{i}"
        x_in, h, h_t, qkv, z, o, u_t, y, extra = saved[i]
        tn = 512 if kind == 0 else 1024
        (dy, d_g_post[i], do, dz), _ = _gated_out_proj_bwd(dx, y, g_post[i:i + 1], w_out[i], o, z, f"outproj_bwd_{tag}")
        dw_out = _matmul(u_t, dy, out_dtype=BF16, name=f"dw_out_{tag}")
        dw_out = dw_out.reshape(4, dw_out.shape[0] // 4, dw_out.shape[1])
        dh_f = None
        exchange = _SiblingExchange([dw_out])
        if pending:
            exchange = _BothExchanges(exchange, _ScatterExchange([pending[1]]))
        if kind == 0:
            sink_l, lse = extra
            (dq, dk, dv, dsk), arrived = _attn_a_bwd(qkv, slopes, sink_l, o, lse, do, f"attn_a_bwd_{tag}", exchange)
            d_sinks[j] = dsk[:, 0, ::LANES].reshape(N_HEADS)
            parts = [dq, dk.astype(BF16), dv.astype(BF16), dz]
        elif kind == 1:
            (dq, dk, dv), arrived = _attn_b_bwd(qkv, extra, do, f"attn_b_bwd_{tag}", exchange)
            parts = [dq, dk, dv, dz]
        else:
            xf, cum4, lse = extra
            (dq, dk, dv, dcum4), arrived = _attn_c_bwd(qkv, cum4, o, lse, do, f"attn_c_bwd_{tag}", exchange)
            d_wf_t, dh_f, db = _fgate_bwd(_from_cum4(dcum4), xf, h, wf_t[i], f"fgate_bwd_{tag}")
            d_b_f = db[:, 0]
            parts = [dq, dk, dv, dz]
        sum_out = _add_pairs(dw_out, arrived[0], place, f"chip_sum_out_{tag}")
        if pending:
            finish_reduce(pending[0], "in", pending[1], arrived[1])
        dproj = jnp.concatenate(parts, axis=1)
        scatter_out = _ScatterExchange([sum_out])
        if kind == 2:
            dw_in, arrived = _matmul(h_t, dproj, out_dtype=F32, name=f"dw_in_{tag}", tn=tn, exchange=scatter_out)
            dw_in = jnp.concatenate([dw_in, d_wf_t.T], axis=1)
            dw_in = dw_in.reshape(dw_in.shape[0], 4, dw_in.shape[1] // 4).transpose(1, 0, 2).astype(BF16)
        else:
            dw_in, arrived = _matmul(h_t, dproj, out_dtype=BF16, name=f"dw_in_{tag}", col_blocks=4,
                                     tn=1152 if kind == 0 else 1024, exchange=scatter_out)
        finish_reduce(i, "out", sum_out, arrived[0])
        (dx, d_g_pre[i]), (their_in,) = _in_proj_bwd(
            dproj, w_in[i], dh_f, dx, x_in, g_pre[i:i + 1], f"inproj_bwd_{tag}", 1536 if kind == 0 else 1024,
            exchange=_SiblingExchange([dw_in]))
        pending = (i, _add_pairs(dw_in, their_in, place, f"chip_sum_in_{tag}"))

    return dict(loss=loss_part, dx=dx, g_pre=jnp.concatenate(d_g_pre, axis=0), g_post=jnp.concatenate(d_g_post, axis=0),
                sinks_a=jnp.stack(d_sinks), b_f_c=d_b_f[None, :], reduced=reduced, last_sum=pending[1])


def _place():
    x, y, c = lax.axis_index("x"), lax.axis_index("y"), lax.axis_index("c")
    others = [(1 - x, y), (x, 1 - y), (1 - x, 1 - y)]
    return x, y, c, others


def _half_rows(ref_rows, which):
    half = ref_rows // 2
    return pl.ds(pl.multiple_of(which * half, half), half)


def _remote(src, dst, sems, k, device):
    send, recv = sems
    return pltpu.make_async_remote_copy(src_ref=src, dst_ref=dst, send_sem=send.at[k], recv_sem=recv.at[k],
                                        device_id=device, device_id_type=MESH)


def _hbm_call(body, name, ins, out_shapes, n_remote, aliases=None):
    any_spec = pl.BlockSpec(memory_space=pl.ANY)
    return pl.pallas_call(
        body, name=name, in_specs=[any_spec] * len(ins), out_specs=[any_spec] * len(out_shapes),
        out_shape=out_shapes, input_output_aliases=aliases or {},
        scratch_shapes=[pltpu.SemaphoreType.DMA((n_remote,)), pltpu.SemaphoreType.DMA((n_remote,))],
    )(*ins)


class _GatherExchange:
    SEMS = 8

    def __init__(self, shards, side_by_side):
        self.ins = list(shards)
        self.side_by_side = list(side_by_side)
        self.out_shapes = [jax.ShapeDtypeStruct((a.shape[0], 4 * a.shape[1]) if wide else (4,) + a.shape, a.dtype)
                           for a, wide in zip(shards, side_by_side)]
        self.n_sems = self.SEMS * len(shards)
        self.aliases = {}

    def _copies(self, ins, outs, sems):
        x, y, c, _ = _place()
        me, diag = 2 * x + y, 2 * (1 - x) + (1 - y)
        nbr = [((1 - x, y, c), 2 * (1 - x) + y), ((x, 1 - y, c), 2 * x + (1 - y))]
        sibling = (x, y, 1 - c)
        table = []
        for w, (src, dst, wide) in enumerate(zip(ins, outs, self.side_by_side)):
            rows, cols = src.shape
            half, quarter = rows // 2, rows // 4

            def slot(chip, core, piece=None, dst=dst, wide=wide, cols=cols, half=half, quarter=quarter):
                start, size = (core * half, half) if piece is None else (core * half + piece * quarter, quarter)
                which = pl.ds(pl.multiple_of(start, quarter), size)
                return dst.at[which, pl.ds(pl.multiple_of(chip * cols, LANES), cols)] if wide else dst.at[chip, which]

            k0 = self.SEMS * w
            cp = lambda s_, d_, k, dev: _remote(s_, d_, sems, k0 + k, dev)
            mine_src = src.at[pl.ds(pl.multiple_of(c * half, half), half)]
            d = dict(
                send=[cp(mine_src, slot(me, c), k, nbr[k][0]) for k in range(2)],
                got=[cp(slot(nbr[k][1], c), slot(nbr[k][1], c), k, nbr[k][0]) for k in range(2)],
                fwd=[cp(slot(nbr[k][1], c, k), slot(nbr[k][1], c, k), 2 + k, nbr[1 - k][0]) for k in range(2)],
                got_fwd=[cp(slot(diag, c, k), slot(diag, c, k), 2 + k, nbr[1 - k][0]) for k in range(2)],
                pass_=[cp(slot(nbr[k][1], c), slot(nbr[k][1], c), 4 + k, sibling) for k in range(2)]
                + [cp(slot(diag, c, k), slot(diag, c, k), 6 + k, sibling) for k in range(2)],
                got_pass=[cp(slot(nbr[k][1], 1 - c), slot(nbr[k][1], 1 - c), 4 + k, sibling) for k in range(2)]
                + [cp(slot(diag, 1 - c, k), slot(diag, 1 - c, k), 6 + k, sibling) for k in range(2)])
            table.append(d)
        return table

    def start(self, ins, outs, sems):
        for d in self._copies(ins, outs, sems):
            for cp in d["send"]:
                cp.start()

    def mid(self, ins, outs, sems):
        for d in self._copies(ins, outs, sems):
            for k in range(2):
                d["got"][k].wait_recv()
                d["fwd"][k].start()
                d["pass_"][k].start()

    def finish(self, ins, outs, sems):
        table = self._copies(ins, outs, sems)
        for d in table:
            for k in range(2):
                d["got_fwd"][k].wait_recv()
                d["pass_"][2 + k].start()
        for d in table:
            for cp in d["got_pass"]:
                cp.wait_recv()
            for cp in d["send"] + d["fwd"] + d["pass_"]:
                cp.wait_send()


class _SemaphoresFrom:
    def __init__(self, ref, start):
        self._ref, self._start = ref, start

    @property
    def at(self):
        return self

    def __getitem__(self, k):
        return self._ref.at[self._start + k]


class _BothExchanges:
    def __init__(self, first, second):
        self.parts = (first, second)
        self.ins = first.ins + second.ins
        self.out_shapes = first.out_shapes + second.out_shapes
        self.n_sems = first.n_sems + second.n_sems
        self.aliases = {}

    def _each(self, phase, ins, outs, sems):
        i0 = o0 = s0 = 0
        for ex in self.parts:
            n_in, n_out = len(ex.ins), len(ex.out_shapes)
            getattr(ex, phase)(ins[i0:i0 + n_in], outs[o0:o0 + n_out], tuple(_SemaphoresFrom(r, s0) for r in sems))
            i0, o0, s0 = i0 + n_in, o0 + n_out, s0 + ex.n_sems

    def start(self, ins, outs, sems):
        self._each("start", ins, outs, sems)

    def mid(self, ins, outs, sems):
        self._each("mid", ins, outs, sems)

    def finish(self, ins, outs, sems):
        self._each("finish", ins, outs, sems)


def _place_columns(wide, block, chip, name):
    rows, cc = block.shape
    tr = min(512, rows)

    def body(c_ref, b_ref, w_ref, o_ref):
        o_ref[...] = b_ref[...]

    return pl.pallas_call(
        body, name=name,
        grid_spec=pltpu.PrefetchScalarGridSpec(
            num_scalar_prefetch=1, grid=(rows // tr,),
            in_specs=[pl.BlockSpec((tr, cc), lambda r, c_ref: (r, 0)), pl.BlockSpec(memory_space=pl.ANY)],
            out_specs=pl.BlockSpec((tr, cc), lambda r, c_ref: (r, c_ref[0]))),
        out_shape=jax.ShapeDtypeStruct(wide.shape, wide.dtype), input_output_aliases={2: 0},
        compiler_params=_params(("parallel",)),
    )(chip.astype(jnp.int32).reshape(1), block, wide)


def _grid_call(body, *, name, grid, in_specs, out_specs, out_shape, args, scratch_shapes=(), semantics, exchange=None):
    if exchange is None:
        res = pl.pallas_call(body, name=name, grid=grid, in_specs=list(in_specs), out_specs=list(out_specs),
                             out_shape=list(out_shape), scratch_shapes=list(scratch_shapes),
                             compiler_params=_params(semantics))(*args)
        return res, []
    n_in, n_out, n_scr = len(args), len(out_shape), len(scratch_shapes)
    x_in, x_out = len(exchange.ins), len(exchange.out_shapes)
    steps = math.prod(grid)

    def wrapped(*refs):
        core_in, ex_in = refs[:n_in], refs[n_in:n_in + x_in]
        rest = refs[n_in + x_in:]
        core_out, ex_out = rest[:n_out], rest[n_out:n_out + x_out]
        scratch, sems = rest[n_out + x_out:n_out + x_out + n_scr], rest[n_out + x_out + n_scr:]
        step = 0
        for axis, extent in enumerate(grid):
            step = step * extent + pl.program_id(axis)

        @pl.when(step == 0)
        def _():
            exchange.start(ex_in, ex_out, sems)

        body(*core_in, *core_out, *scratch)

        @pl.when(step == max((3 * steps) // 4 - 1, 0))
        def _():
            exchange.mid(ex_in, ex_out, sems)

        @pl.when(step == steps - 1)
        def _():
            exchange.finish(ex_in, ex_out, sems)

    any_spec = pl.BlockSpec(memory_space=pl.ANY)
    res = pl.pallas_call(
        wrapped, name=name, grid=grid,
        in_specs=list(in_specs) + [any_spec] * x_in, out_specs=list(out_specs) + [any_spec] * x_out,
        out_shape=list(out_shape) + list(exchange.out_shapes),
        input_output_aliases={n_in + a: n_out + b for a, b in exchange.aliases.items()},
        scratch_shapes=list(scratch_shapes) + [pltpu.SemaphoreType.DMA((exchange.n_sems,)),
                                               pltpu.SemaphoreType.DMA((exchange.n_sems,))],
        compiler_params=_params(("arbitrary",) * len(grid)),
    )(*args, *exchange.ins)
    return res[:n_out], res[n_out:]


class _SiblingExchange:
    def __init__(self, parts):
        self.ins = list(parts)
        self.out_shapes = [jax.ShapeDtypeStruct((4, a.shape[1] // 2, a.shape[2]), a.dtype) for a in parts]
        self.n_sems = len(parts)
        self.aliases = {}

    def _copies(self, ins, outs, sems):
        x, y, c, _ = _place()
        return [_remote(src.at[:, _half_rows(src.shape[1], 1 - c)], dst, sems, w, (x, y, 1 - c))
                for w, (src, dst) in enumerate(zip(ins, outs))]

    def start(self, ins, outs, sems):
        for cp in self._copies(ins, outs, sems):
            cp.start()

    def mid(self, ins, outs, sems):
        pass

    def finish(self, ins, outs, sems):
        for cp in self._copies(ins, outs, sems):
            cp.wait_recv()
            cp.wait_send()


class _ScatterExchange:
    def __init__(self, sums):
        self.ins = list(sums)
        self.out_shapes = [jax.ShapeDtypeStruct(a.shape, a.dtype) for a in sums]
        self.n_sems = 3 * len(sums)
        self.aliases = {}

    def _copies(self, ins, outs, sems):
        x, y, c, others = _place()
        me = 2 * x + y
        table = []
        for w, (src, dst) in enumerate(zip(ins, outs)):
            for j, (px, py) in enumerate(others):
                there = 2 * px + py
                send = _remote(src.at[there], dst.at[me], sems, 3 * w + j, (px, py, c))
                landed = _remote(dst.at[there], dst.at[there], sems, 3 * w + j, (px, py, c))
                table.append((send, landed))
        return table

    def start(self, ins, outs, sems):
        for send, _ in self._copies(ins, outs, sems):
            send.start()

    def mid(self, ins, outs, sems):
        pass

    def finish(self, ins, outs, sems):
        table = self._copies(ins, outs, sems)
        for _, landed in table:
            landed.wait_recv()
        for send, _ in table:
            send.wait_send()


def _sibling_join(shards, name):
    n = len(shards)

    def body(*refs):
        ins, outs, sems = refs[:n], refs[n:2 * n], refs[2 * n:2 * n + 2]
        x, y, c, _ = _place()
        pend = []
        for w in range(n):
            rows = ins[w].shape[1]
            mine, theirs = _half_rows(rows, c), _half_rows(rows, 1 - c)
            cp = _remote(ins[w].at[:, mine], outs[w].at[:, mine], sems, w, (x, y, 1 - c))
            cp.start()
            pend.append((cp, _remote(ins[w].at[:, theirs], outs[w].at[:, theirs], sems, w, (x, y, 1 - c))))
        for cp, landed in pend:
            landed.wait_recv()
            cp.wait_send()

    out_shapes = [jax.ShapeDtypeStruct(a.shape, a.dtype) for a in shards]
    return _hbm_call(body, name, shards, out_shapes, n, aliases={w: w for w in range(n)})


SMALL_ROWS = 136


def _all_reduce_small(vec):
    def body(v_ref, o_ref, buf, send, recv, loc):
        x, y, c, _ = _place()
        me = 4 * x + 2 * y + c
        lc = pltpu.make_async_copy(v_ref, buf.at[me], loc.at[0])
        lc.start()
        cps = []
        for k in range(1, 8):
            fx, fy, fc = (k >> 2) & 1, (k >> 1) & 1, k & 1
            peer = (x ^ fx, y ^ fy, c ^ fc)
            cp = pltpu.make_async_remote_copy(src_ref=v_ref, dst_ref=buf.at[me], send_sem=send.at[k - 1],
                                              recv_sem=recv.at[k - 1], device_id=peer, device_id_type=MESH)
            cp.start()
            cps.append((cp, 4 * peer[0] + 2 * peer[1] + peer[2]))
        for k, (cp, src) in enumerate(cps):
            pltpu.make_async_remote_copy(src_ref=v_ref, dst_ref=buf.at[src], send_sem=send.at[k], recv_sem=recv.at[k],
                                         device_id=(x, y, c), device_id_type=MESH).wait_recv()
        for cp, _ in cps:
            cp.wait_send()
        lc.wait()
        total = buf[0]
        for k in range(1, 8):
            total = total + buf[k]
        o_ref[...] = total

    vm = pl.BlockSpec(memory_space=pltpu.VMEM)
    return pl.pallas_call(
        body, name="all_reduce_small", in_specs=[vm], out_specs=vm,
        out_shape=jax.ShapeDtypeStruct(vec.shape, F32),
        scratch_shapes=[pltpu.VMEM((8,) + vec.shape, F32), pltpu.SemaphoreType.DMA((7,)),
                        pltpu.SemaphoreType.DMA((7,)), pltpu.SemaphoreType.DMA((1,))],
    )(vec)


SUM_ROWS = 256


def _add_pairs(part, theirs, place, name):
    four, rh, cc = theirs.shape
    tr = min(SUM_ROWS, rh)
    halves = part.reshape(four, 2, rh, cc)

    def body(p_ref, a_ref, b_ref, o_ref):
        o_ref[0] = (a_ref[0, 0].astype(F32) + b_ref[0].astype(F32)).astype(o_ref.dtype)

    spec = pl.BlockSpec((1, tr, cc), lambda k, r, p_ref: (k, r, 0))
    return pl.pallas_call(
        body, name=name,
        grid_spec=pltpu.PrefetchScalarGridSpec(
            num_scalar_prefetch=1, grid=(four, rh // tr),
            in_specs=[pl.BlockSpec((1, 1, tr, cc), lambda k, r, p_ref: (k, p_ref[1], r, 0)), spec], out_specs=spec),
        out_shape=jax.ShapeDtypeStruct(theirs.shape, theirs.dtype),
        compiler_params=_params(("parallel", "parallel")),
    )(place, halves, theirs)


def _sum_chips(own, arrived, place, name, layer, n_layers, into=None):
    four, rh, cc = own.shape
    tr = min(SUM_ROWS, rh)
    nr = rh // tr

    def body(p_ref, own_ref, arr_ref, *rest):
        o_ref = rest[-1]
        x, y = lax.axis_index("x"), lax.axis_index("y")
        tot = own_ref[0].astype(F32)
        for px, py in ((1 - x, y), (x, 1 - y), (1 - x, 1 - y)):
            tot = tot + arr_ref[2 * px + py].astype(F32)
        o_ref[0] = tot

    in_specs = [pl.BlockSpec((1, tr, cc), lambda r, p_ref: (p_ref[0], r, 0)),
                pl.BlockSpec((4, tr, cc), lambda r, p_ref: (0, r, 0))]
    args, aliases = [place, own, arrived], {}
    if into is not None:
        in_specs.append(pl.BlockSpec(memory_space=pl.ANY))
        args.append(into)
        aliases = {3: 0}
    return pl.pallas_call(
        body, name=name,
        grid_spec=pltpu.PrefetchScalarGridSpec(
            num_scalar_prefetch=1, grid=(nr,), in_specs=in_specs,
            out_specs=pl.BlockSpec((1, tr, cc), lambda r, p_ref: (layer, p_ref[1] * nr + r, 0))),
        out_shape=jax.ShapeDtypeStruct((n_layers, 2 * rh, cc), F32), input_output_aliases=aliases,
        compiler_params=_params(("parallel",)),
    )(*args)


ADAM_ROWS = 256


def _adamw(w, g, m, v, name):
    shape = w.shape
    as3 = lambda a: a.reshape((-1,) + shape[-2:])
    layers, rows, cc = as3(w).shape
    by_rows = rows % min(ADAM_ROWS, rows) == 0
    tr, tc = (min(ADAM_ROWS, rows), cc) if by_rows else (rows, ADAM_ROWS)
    assert rows % tr == 0 and cc % tc == 0

    def body(w_ref, g_ref, m_ref, v_ref, d_ref, nm_ref, nv_ref):
        _adamw_update(w_ref, g_ref, m_ref, v_ref, d_ref, nm_ref, nv_ref)

    spec = pl.BlockSpec((1, tr, tc), (lambda l, i: (l, i, 0)) if by_rows else (lambda l, i: (l, 0, i)))
    sh = jax.ShapeDtypeStruct((layers, rows, cc), F32)
    outs = pl.pallas_call(
        body, name=name, grid=(layers, (rows // tr) * (cc // tc)), in_specs=[spec] * 4, out_specs=[spec] * 3,
        out_shape=[sh] * 3,
        compiler_params=_params(("parallel", "parallel")),
    )(as3(w), as3(g), as3(m), as3(v))
    return [o.reshape(shape) for o in outs]


def _adamw_update(w_ref, g_ref, m_ref, v_ref, d_ref, nm_ref, nv_ref):
    c1 = 1.0 - ADAM_B1 ** ADAM_STEP
    c2 = 1.0 - ADAM_B2 ** ADAM_STEP
    gv = g_ref[...]
    nm = ADAM_B1 * m_ref[...] + (1.0 - ADAM_B1) * gv
    nv = ADAM_B2 * v_ref[...] + (1.0 - ADAM_B2) * (gv * gv)
    nm_ref[...] = nm
    nv_ref[...] = nv
    d_ref[...] = -ADAM_LR * ((nm / c1) / (jnp.sqrt(nv / c2) + ADAM_EPS) + ADAM_WD * w_ref[...])


ADAM_MANY_STEPS = 16


def _adamw_many(quads, name, exchange=None):
    n = ADAM_MANY_STEPS
    specs = []
    for w, _, _, _ in quads:
        layers, rows, cc = w.shape
        if rows % (8 * n) == 0:
            specs.append(pl.BlockSpec((layers, rows // n, cc), lambda i: (0, i, 0)))
        else:
            assert cc % (LANES * n) == 0, (name, w.shape)
            specs.append(pl.BlockSpec((layers, rows, cc // n), lambda i: (0, 0, i)))

    def body(*refs):
        ins, outs = refs[:4 * len(quads)], refs[4 * len(quads):]
        for q in range(len(quads)):
            _adamw_update(*ins[4 * q:4 * q + 4], *outs[3 * q:3 * q + 3])

    res, arrived = _grid_call(
        body, name=name, grid=(n,), in_specs=[s for s in specs for _ in range(4)],
        out_specs=[s for s in specs for _ in range(3)],
        out_shape=[jax.ShapeDtypeStruct(w.shape, F32) for w, _, _, _ in quads for _ in range(3)],
        args=tuple(a for quad in quads for a in quad), semantics=("parallel",), exchange=exchange)
    return [list(res[3 * q:3 * q + 3]) for q in range(len(quads))], arrived


def _pack_small(g_pre, g_post, sinks_a, b_f_c, loss_row):
    pad = lambda a: jnp.pad(a.reshape(1, -1).astype(F32), ((0, 0), (0, LANES - a.size)))
    rows = [g_pre.astype(F32).reshape(-1, LANES), g_post.astype(F32).reshape(-1, LANES), pad(sinks_a), pad(b_f_c), loss_row]
    packed = jnp.concatenate(rows, axis=0)
    return jnp.pad(packed, ((0, SMALL_ROWS - packed.shape[0]), (0, 0)))


def _unpack_small(p):
    n = DEPTH * D_MODEL // LANES
    return (p[:n].reshape(DEPTH, D_MODEL), p[n:2 * n].reshape(DEPTH, D_MODEL), p[2 * n, :2 * N_HEADS].reshape(2, N_HEADS),
            p[2 * n + 1, :N_HEADS].reshape(1, N_HEADS), p[2 * n + 2, 0])


def kernel(x, g_pre, g_post, w_in_a, w_out_a, sinks_a, w_in_b, w_out_b, w_in_c, b_f_c, w_out_c, loss_target, m_g_pre, m_g_post, m_w_in_a, m_w_out_a, m_sinks_a, m_w_in_b, m_w_out_b, m_w_in_c, m_b_f_c, m_w_out_c, v_g_pre, v_g_post, v_w_in_a, v_w_out_a, v_sinks_a, v_w_in_b, v_w_out_b, v_w_in_c, v_b_f_c, v_w_out_c):
    big_w = [w_in_a, w_out_a, w_in_b, w_out_b, w_in_c, w_out_c]
    big_m = [m_w_in_a, m_w_out_a, m_w_in_b, m_w_out_b, m_w_in_c, m_w_out_c]
    big_v = [v_w_in_a, v_w_out_a, v_w_in_b, v_w_out_b, v_w_in_c, v_w_out_c]

    chip = 2 * lax.axis_index("x") + lax.axis_index("y")
    place = jnp.stack([chip, lax.axis_index("c")]).astype(jnp.int32)
    by_kind = {0: (w_in_a, w_out_a), 1: (w_in_b, w_out_b), 2: (w_in_c, w_out_c)}
    shards = {}
    for i in range(DEPTH):
        kind, j = _layer_kind(i)
        shards[("in", i)] = by_kind[kind][0][j].astype(BF16)
        shards[("out", i)] = by_kind[kind][1][j].astype(BF16)

    res = _forward_backward(x[0], loss_target[0], g_pre, g_post, sinks_a, b_f_c, shards, chip, place)
    reduced = res["reduced"]
    rest = [("out", 0), ("in", 1), ("out", 1), ("in", 2), ("out", 2)]
    grads = [None] + list(_sibling_join([reduced[k] for k in rest], "grad_sibling_join"))

    small = _unpack_small(_all_reduce_small(
        _pack_small(res["g_pre"], res["g_post"], res["sinks_a"], res["b_f_c"], res["loss"])))
    g_small, loss = small[:4], small[4]

    zero_row = jnp.zeros((1, LANES), F32)
    pk = lambda a: _pack_small(a[0], a[1], a[2], a[3], zero_row)
    sm = _adamw(pk([g_pre, g_post, sinks_a, b_f_c]), pk(g_small), pk([m_g_pre, m_g_post, m_sinks_a, m_b_f_c]),
                pk([v_g_pre, v_g_post, v_sinks_a, v_b_f_c]), "adamw_small")
    sm = [_unpack_small(a)[:4] for a in sm]
    turned = lambda a: jnp.swapaxes(a, 1, 2)
    g_c = lax.optimization_barrier(turned(grads[4]))
    grads[4] = turned(g_c)
    quads = [(w_in_b, grads[2], m_w_in_b, v_w_in_b), (turned(w_in_c), g_c, turned(m_w_in_c), turned(v_w_in_c))]
    (upd_in_b, upd_in_c), arrived = _adamw_many(quads, "adamw_in_b_c", exchange=_ScatterExchange([res["last_sum"]]))
    half = _sum_chips(res["last_sum"], arrived[0], place, "shard_sum_in_l0", 0, 2, into=reduced[("in", 0)])
    grads[0] = _sibling_join([half], "grad_sibling_join_w_in_a")[0]
    bigs = [_adamw(w_in_a, grads[0], m_w_in_a, v_w_in_a, "adamw_w_in_a"),
            _adamw(w_out_a, grads[1], m_w_out_a, v_w_out_a, "adamw_w_out_a"),
            upd_in_b,
            _adamw(w_out_b, grads[3], m_w_out_b, v_w_out_b, "adamw_w_out_b"),
            [turned(o) for o in upd_in_c],
            _adamw(w_out_c, grads[5], m_w_out_c, v_w_out_c, "adamw_w_out_c")]

    def ordered(small4, big6):
        return [small4[0], small4[1], big6[0], big6[1], small4[2], big6[2], big6[3], big6[4], small4[3], big6[5]]

    out = [loss, res["dx"][None], *ordered(g_small, grads)]
    for k in range(3):
        out += ordered(sm[k], [b[k] for b in bigs])
    return tuple(out)
```

```python
import math

import numpy as np
import jax
import jax.numpy as jnp
from jax import lax
from jax.experimental import pallas as pl
from jax.experimental.pallas import tpu as pltpu

F32 = jnp.float32
BF16 = jnp.bfloat16

D_MODEL = 2048
DEPTH = 4
N_HEADS = 32
HEAD_DIM = 64
LANES = 128
N_PAIRS = N_HEADS * HEAD_DIM // LANES
BRANCH = N_HEADS * HEAD_DIM
N_KV_A = 4
KV_A = N_KV_A * HEAD_DIM
WINDOW = 128
NORM_EPS = 1e-6
NEG = -1e30
Q_SCALE = HEAD_DIM ** -0.5

A_QKV = BRANCH + 2 * KV_A
B_QKV = 3 * BRANCH

ADAM_LR = 0.001
ADAM_B1 = 0.9
ADAM_B2 = 0.999
ADAM_EPS = 1e-08
ADAM_WD = 0.01
ADAM_STEP = 10

MESH = pl.DeviceIdType.MESH

_NT = (((1,), (1,)), ((), ()))
_TN = (((0,), (0,)), ((), ()))


def _params(sem=None):
    return pltpu.CompilerParams(dimension_semantics=sem)


def _matmul(a, b, *, out_dtype, name, n=None, b_off=0, tm=1024, tn=1024, col_blocks=None, exchange=None):
    (m, k), nn = a.shape, (n or b.shape[1])
    tm, tn = min(tm, m), min(tn, nn)
    assert m % tm == 0 and nn % tn == 0, (name, m, nn, tm, tn)

    def body(a_ref, b_ref, o_ref):
        p = jnp.dot(a_ref[...], b_ref[...], preferred_element_type=F32)
        o_ref[...] = p.astype(o_ref.dtype).reshape(o_ref.shape)

    in_specs = [pl.BlockSpec((tm, k), lambda i, j: (i, 0)), pl.BlockSpec((k, tn), lambda i, j: (0, j + b_off))]
    if col_blocks is None:
        out_spec = pl.BlockSpec((tm, tn), lambda i, j: (i, j))
        out_shape = jax.ShapeDtypeStruct((m, nn), out_dtype)
    else:
        per = nn // col_blocks // tn
        assert per * tn * col_blocks == nn, (name, nn, tn, col_blocks)
        out_spec = pl.BlockSpec((1, tm, tn), lambda i, j: (j // per, i, j % per))
        out_shape = jax.ShapeDtypeStruct((col_blocks, m, nn // col_blocks), out_dtype)
    (res,), arrived = _grid_call(
        body, name=name, grid=(m // tm, nn // tn), in_specs=in_specs, out_specs=[out_spec], out_shape=[out_shape],
        args=(a, b), semantics=("parallel", "parallel"), exchange=exchange)
    return res if exchange is None else (res, arrived)


ROW_TILE = 256


def _row_call(body, name, ins, outs, *, s, exchange=None):
    tr = min(ROW_TILE, s)
    spec = {"row": lambda sh: pl.BlockSpec((tr, sh[1]), lambda i: (i, 0)),
            "vec": lambda sh: pl.BlockSpec((1, sh[1]), lambda i: (0, 0)),
            "col": lambda sh: pl.BlockSpec((sh[0], tr), lambda i: (0, i))}
    in_specs = [spec[kind](a.shape) for a, kind in ins]
    out_specs = [spec[kind](sh.shape) for sh, kind in outs]
    res, arrived = _grid_call(
        body, name=name, grid=(s // tr,), in_specs=in_specs, out_specs=out_specs, out_shape=[sh for sh, _ in outs],
        args=tuple(a for a, _ in ins), semantics=("arbitrary",), exchange=exchange)
    return res if exchange is None else (res, arrived)


def _rsqrt_ms(v):
    return lax.rsqrt(jnp.mean(v * v, axis=-1, keepdims=True) + NORM_EPS)


def _rmsnorm_fwd(x, g, name, exchange=None):
    s, d = x.shape

    def body(x_ref, g_ref, h_ref, ht_ref):
        xv = x_ref[...]
        h = xv * _rsqrt_ms(xv) * g_ref[...]
        h_ref[...] = h.astype(BF16)
        ht_ref[...] = h.T.astype(BF16)

    return _row_call(body, name, [(x, "row"), (g, "vec")],
                     [(jax.ShapeDtypeStruct((s, d), BF16), "row"), (jax.ShapeDtypeStruct((d, s), BF16), "col")], s=s,
                     exchange=exchange)


PROJ_ROWS = 256


def _resident(shape):
    return pl.BlockSpec(shape, lambda i: (0,) * len(shape), pipeline_mode=pl.Buffered(1))


def _gated_out_proj(o, z, w_out, x, g, name):
    s, d = x.shape
    tm = min(PROJ_ROWS, s)

    def body(o_ref, z_ref, w_ref, x_ref, g_ref, xn_ref, y_ref, ut_ref):
        zv = z_ref[...]
        u = o_ref[...] * (zv * jax.nn.sigmoid(zv))
        ut_ref[...] = u.T.astype(BF16)
        y = jnp.dot(u.astype(BF16), w_ref[...], preferred_element_type=F32)
        y_ref[...] = y
        xn_ref[...] = x_ref[...] + y * _rsqrt_ms(y) * g_ref[...]

    row = pl.BlockSpec((tm, d), lambda i: (i, 0))
    return pl.pallas_call(
        body, name=name, grid=(s // tm,),
        in_specs=[row, row, _resident(w_out.shape), row, _resident((1, d))],
        out_specs=[row, row, pl.BlockSpec((d, tm), lambda i: (0, i))],
        out_shape=[jax.ShapeDtypeStruct((s, d), F32), jax.ShapeDtypeStruct((s, d), F32), jax.ShapeDtypeStruct((d, s), BF16)],
        compiler_params=_params(("parallel",)),
    )(o, z, w_out, x, g)


def _gated_out_proj_bwd(dx, y, g, w_out, o, z, name, exchange=None):
    s, d = dx.shape
    tm = min(PROJ_ROWS, s)

    def body(dx_ref, y_ref, g_ref, w_ref, o_ref, z_ref, dy_ref, dg_ref, do_ref, dz_ref):
        dy, dg = _norm_bwd_rows(dx_ref[...], y_ref[...], g_ref[...])
        dyb = dy.astype(BF16)
        dy_ref[...] = dyb

        @pl.when(pl.program_id(0) == 0)
        def _():
            dg_ref[...] = jnp.zeros_like(dg_ref)

        dg_ref[...] += jnp.sum(dg, axis=0, keepdims=True)
        du = lax.dot_general(dyb, w_ref[...], _NT, preferred_element_type=F32)
        zv = z_ref[...]
        sig = jax.nn.sigmoid(zv)
        do_ref[...] = (du * (zv * sig)).astype(BF16)
        dz_ref[...] = (du * o_ref[...] * (sig * (1.0 + zv * (1.0 - sig)))).astype(BF16)

    row = pl.BlockSpec((tm, d), lambda i: (i, 0))
    vec = pl.BlockSpec((1, d), lambda i: (0, 0))
    bf = jax.ShapeDtypeStruct((s, d), BF16)
    return _grid_call(
        body, name=name, grid=(s // tm,),
        in_specs=[row, row, _resident((1, d)), _resident(w_out.shape), row, row],
        out_specs=[row, vec, row, row], out_shape=[bf, jax.ShapeDtypeStruct((1, d), F32), bf, bf],
        args=(dx, y, g, w_out, o, z), semantics=("arbitrary",), exchange=exchange)


IN_BWD_ROWS = 512


def _in_proj_bwd(dproj, w_in, extra, dx, x, g, name, tk, exchange=None):
    s, d = x.shape
    k = dproj.shape[1]
    tm = min(IN_BWD_ROWS, s)
    nk = k // tk
    assert k % tk == 0 and s % tm == 0, (name, k, tk)
    has_extra = extra is not None

    def body(a_ref, b_ref, *rest):
        if has_extra:
            e_ref, rest = rest[0], rest[1:]
        dx_ref, x_ref, g_ref, o_ref, dg_ref, acc_ref = rest
        i, kk = pl.program_id(0), pl.program_id(1)
        p = lax.dot_general(a_ref[...], b_ref[...], _NT, preferred_element_type=F32)

        @pl.when(kk == 0)
        def _():
            acc_ref[...] = p

        @pl.when(kk > 0)
        def _():
            acc_ref[...] += p

        @pl.when((i == 0) & (kk == 0))
        def _():
            dg_ref[...] = jnp.zeros_like(dg_ref)

        @pl.when(kk == nk - 1)
        def _():
            def rows_chunk(c, _):
                r = pl.ds(pl.multiple_of(c * LANES, LANES), LANES)
                dh = acc_ref[r, :] + e_ref[r, :] if has_extra else acc_ref[r, :]
                dv, dg = _norm_bwd_rows(dh, x_ref[r, :], g_ref[...])
                o_ref[r, :] = dx_ref[r, :] + dv
                dg_ref[...] += jnp.sum(dg, axis=0, keepdims=True)
                return 0

            lax.fori_loop(0, tm // LANES, rows_chunk, 0)

    row = pl.BlockSpec((tm, d), lambda i, kk: (i, 0))
    vec = pl.BlockSpec((1, d), lambda i, kk: (0, 0))
    in_specs = [pl.BlockSpec((tm, tk), lambda i, kk: (i, kk)), pl.BlockSpec((d, tk), lambda i, kk: (0, kk))]
    args = [dproj, w_in]
    if has_extra:
        in_specs.append(row)
        args.append(extra)
    return _grid_call(
        body, name=name, grid=(s // tm, nk), in_specs=in_specs + [row, row, vec], out_specs=[row, vec],
        out_shape=[jax.ShapeDtypeStruct((s, d), F32), jax.ShapeDtypeStruct((1, d), F32)],
        args=tuple(args) + (dx, x, g), scratch_shapes=[pltpu.VMEM((tm, d), F32)], semantics=("arbitrary", "arbitrary"),
        exchange=exchange)


def _loss_and_grad(x, target):
    s, d = x.shape

    def body(x_ref, t_ref, dx_ref, l_ref):
        err = x_ref[...] - t_ref[...]
        dx_ref[...] = err * (1.0 / d)
        part = jnp.sum(jnp.sum(err * err, axis=1, keepdims=True), axis=0, keepdims=True) * (0.5 / d)

        @pl.when(pl.program_id(0) == 0)
        def _():
            l_ref[...] = jnp.zeros_like(l_ref)

        l_ref[...] += jnp.broadcast_to(part, l_ref.shape)

    return _row_call(body, "loss_head", [(x, "row"), (target, "row")],
                     [(jax.ShapeDtypeStruct((s, d), F32), "row"),
                      (jax.ShapeDtypeStruct((1, LANES), F32), "vec")], s=s)


def _norm_bwd_rows(dn, v, g):
    r = _rsqrt_ms(v)
    a = dn * g
    dv = r * (a - v * (r * r) * jnp.mean(a * v, axis=-1, keepdims=True))
    return dv, dn * v * r


def _lane_is_first_head():
    return lax.broadcasted_iota(jnp.int32, (1, LANES), 1) < HEAD_DIM


def _bcast_lanes(col):
    return jnp.broadcast_to(col, (col.shape[0], LANES))


def _pair_spec(s, off=0, width=LANES):
    return pl.BlockSpec((s, width), lambda p: (0, p + off))


def _stack_heads(pair, first):
    return jnp.concatenate([jnp.where(first, pair, 0), jnp.where(first, 0, pair)], axis=0).astype(BF16)


def _stacked_mask(t, strict):
    row = lax.broadcasted_iota(jnp.int32, (2 * t, t), 0)
    col = lax.broadcasted_iota(jnp.int32, (2 * t, t), 1)
    query = jnp.where(row >= t, row - t, row)
    return col < query if strict else col <= query


def _steps_in_groups(n, step, carry, widths=(2, 1)):
    done = 0
    for width in widths:
        def group(jj, c, width=width, done=done):
            for k in range(width):
                c = step(done + width * jj + k, c)
            return c

        trips = (n - done) // width
        carry = lax.fori_loop(0, trips, group, carry)
        done = done + width * trips
    return carry


FULL_ATTENTION_WIDTHS = (4, 2, 1)


def _rowsum_heads(prod, first):
    return (jnp.sum(jnp.where(first, prod, 0.0), axis=1, keepdims=True),
            jnp.sum(jnp.where(first, 0.0, prod), axis=1, keepdims=True))


def _softplus_parts(z):
    e = jnp.exp(-jnp.abs(z))
    sp = jnp.maximum(z, 0.0) + jnp.log(1.0 + e)
    r = 1.0 / (1.0 + e)
    return sp, jnp.where(z >= 0, r, e * r)


def _sb_tile(s):
    return min(256, s)


def _attn_b_fwd(qkv, name, exchange=None):
    s = qkv.shape[0]
    t = _sb_tile(s)
    nq = s // t

    def body(q_ref, k_ref, v_ref, o_ref, lt_ref):
        first = _lane_is_first_head()
        before = _stacked_mask(t, strict=True)
        tri = (lax.broadcasted_iota(jnp.int32, (t, t), 0) >= lax.broadcasted_iota(jnp.int32, (t, t), 1)).astype(BF16)

        def tile(j, carry, diag, qs):
            c, acc = carry
            c0 = pl.multiple_of(j * t, t)
            k2 = k_ref[pl.ds(c0, t), :]
            v2 = v_ref[pl.ds(c0, t), :]
            z = lax.dot_general(qs, k2, _NT, preferred_element_type=F32)
            sp, _ = _softplus_parts(z)
            lf = jnp.where(before, -sp, 0.0) if diag else -sp
            incl = jnp.dot(lf.astype(BF16), tri, preferred_element_type=F32)
            a = jnp.exp(z + c + incl)
            if diag:
                a = jnp.where(before, a, 0.0)
            pv = jnp.dot(a.astype(BF16), v2, preferred_element_type=F32)
            return c + incl[:, 0:1], acc + jnp.where(first, pv[:t], pv[t:])

        def qblock(i, _):
            r0 = pl.multiple_of(i * t, t)
            qs = _stack_heads(q_ref[pl.ds(r0, t), :] * Q_SCALE, first)
            carry = tile(i, (jnp.zeros((2 * t, 1), F32), jnp.zeros((t, LANES), F32)), True, qs)
            carry = _steps_in_groups(i, lambda j, c: tile(i - 1 - j, c, False, qs), carry, FULL_ATTENTION_WIDTHS)
            o_ref[pl.ds(r0, t), :] = carry[1]
            lt_ref[pl.ds(r0, t), 0:LANES] = _bcast_lanes(carry[0][:t])
            lt_ref[pl.ds(r0, t), LANES:2 * LANES] = _bcast_lanes(carry[0][t:])
            return 0

        lax.fori_loop(0, nq, qblock, 0)

    return _grid_call(
        body, name=name, grid=(N_PAIRS,),
        in_specs=[_pair_spec(s), _pair_spec(s, N_PAIRS), _pair_spec(s, 2 * N_PAIRS)],
        out_specs=[_pair_spec(s), _stat_spec(s)],
        out_shape=[jax.ShapeDtypeStruct((s, BRANCH), F32), jax.ShapeDtypeStruct((s, N_HEADS * LANES), F32)],
        args=(qkv, qkv, qkv), semantics=("parallel",), exchange=exchange)


def _attn_b_bwd(qkv, ltot, do, name, exchange=None):
    s = qkv.shape[0]
    t = _sb_tile(s)
    nq = s // t

    def body(q_ref, k_ref, v_ref, lt_ref, do_ref, dq_ref, dk_ref, dv_ref, dk_acc, dv_acc):
        first = _lane_is_first_head()
        before = _stacked_mask(t, strict=True)
        tri = (lax.broadcasted_iota(jnp.int32, (t, t), 0) <= lax.broadcasted_iota(jnp.int32, (t, t), 1)).astype(BF16)
        dk_acc[...] = jnp.zeros_like(dk_acc)
        dv_acc[...] = jnp.zeros_like(dv_acc)

        def tile(j, carry, diag, qs, dos, lt):
            p_l, p_g, dq_acc = carry
            c0 = pl.multiple_of(j * t, t)
            k2 = k_ref[pl.ds(c0, t), :]
            v2 = v_ref[pl.ds(c0, t), :]
            z = lax.dot_general(qs, k2, _NT, preferred_element_type=F32)
            sp, sig = _softplus_parts(z)
            lf = jnp.where(before, -sp, 0.0) if diag else -sp
            pref_l = jnp.dot(lf.astype(BF16), tri, preferred_element_type=F32)
            a = jnp.exp(z + ((lt - p_l) - pref_l + lf))
            if diag:
                a = jnp.where(before, a, 0.0)
            g = a * lax.dot_general(dos, v2, _NT, preferred_element_type=F32)
            pref_g = jnp.dot(g.astype(BF16), tri, preferred_element_type=F32)
            dz = g - sig * (p_g + pref_g)
            if diag:
                dz = jnp.where(before, dz, 0.0)
            dzb = dz.astype(BF16)
            dq = jnp.dot(dzb, k2, preferred_element_type=F32)
            dk_acc[pl.ds(c0, t), :] += lax.dot_general(dzb, qs, _TN, preferred_element_type=F32)
            dv_acc[pl.ds(c0, t), :] += lax.dot_general(a.astype(BF16), dos, _TN, preferred_element_type=F32)
            return p_l + pref_l[:, t - 1:t], p_g + pref_g[:, t - 1:t], dq_acc + jnp.where(first, dq[:t], dq[t:])

        def qblock(i, _):
            r0 = pl.multiple_of(i * t, t)
            qs = _stack_heads(q_ref[pl.ds(r0, t), :] * Q_SCALE, first)
            dos = _stack_heads(do_ref[pl.ds(r0, t), :], first)
            lt = jnp.concatenate([lt_ref[pl.ds(r0, t), 0:1], lt_ref[pl.ds(r0, t), LANES:LANES + 1]], axis=0)
            zero = jnp.zeros((2 * t, 1), F32)
            carry = (zero, zero, jnp.zeros((t, LANES), F32))
            carry = _steps_in_groups(i, lambda j, c: tile(j, c, False, qs, dos, lt), carry, FULL_ATTENTION_WIDTHS)
            carry = tile(i, carry, True, qs, dos, lt)
            dq_ref[pl.ds(r0, t), :] = (carry[2] * Q_SCALE).astype(BF16)
            return 0

        lax.fori_loop(0, nq, qblock, 0)
        dk_ref[...] = dk_acc[...].astype(BF16)
        dv_ref[...] = dv_acc[...].astype(BF16)

    out = jax.ShapeDtypeStruct((s, BRANCH), BF16)
    return _grid_call(
        body, name=name, grid=(N_PAIRS,),
        in_specs=[_pair_spec(s), _pair_spec(s, N_PAIRS), _pair_spec(s, 2 * N_PAIRS), _stat_spec(s), _pair_spec(s)],
        out_specs=[_pair_spec(s)] * 3, out_shape=[out] * 3,
        scratch_shapes=[pltpu.VMEM((s, LANES), F32), pltpu.VMEM((s, LANES), F32)],
        args=(qkv, qkv, qkv, ltot, do), semantics=("parallel",), exchange=exchange)


def _fox_tile(s):
    return min(256, s)


def _stat_spec(s):
    return pl.BlockSpec((s, 2 * LANES), lambda p: (0, p))


def _cum_spec(nt, t):
    return pl.BlockSpec((1, nt, 2, t), lambda p: (p, 0, 0, 0))


def _attn_c_fwd(qkv, cum4, name, exchange=None):
    s = qkv.shape[0]
    t = _fox_tile(s)
    nq = s // t

    def body(q_ref, k_ref, v_ref, c_ref, o_ref, lse_ref):
        first = _lane_is_first_head()
        causal = _stacked_mask(t, strict=False)

        def tile(j, carry, diag, qs):
            c0 = pl.multiple_of(j * t, t)
            k2 = k_ref[pl.ds(c0, t), :]
            v2 = v_ref[pl.ds(c0, t), :]
            cs = c_ref[0, j]
            m_prev, l_prev, acc = carry
            z = lax.dot_general(qs, k2, _NT, preferred_element_type=F32)
            sc = jnp.concatenate([z[:t] - cs[0:1, :], z[t:] - cs[1:2, :]], axis=0)
            if diag:
                sc = jnp.where(causal, sc, NEG)
            m_new = jnp.maximum(m_prev, jnp.max(sc, axis=1, keepdims=True))
            alpha = jnp.exp(m_prev - m_new)
            p = jnp.exp(sc - m_new)
            l_new = alpha * l_prev + jnp.sum(p, axis=1, keepdims=True)
            pv = jnp.dot(p.astype(BF16), v2, preferred_element_type=F32)
            acc = jnp.where(first, acc * alpha[:t] + pv[:t], acc * alpha[t:] + pv[t:])
            return m_new, l_new, acc

        def qblock(i, _):
            r0 = pl.multiple_of(i * t, t)
            qs = _stack_heads(q_ref[pl.ds(r0, t), :] * Q_SCALE, first)
            carry = (jnp.full((2 * t, 1), NEG, F32), jnp.zeros((2 * t, 1), F32), jnp.zeros((t, LANES), F32))
            carry = _steps_in_groups(i, lambda j, c: tile(j, c, False, qs), carry, FULL_ATTENTION_WIDTHS)
            m, l, acc = tile(i, carry, True, qs)
            inv = 1.0 / l
            lse = m + jnp.log(l)
            o_ref[pl.ds(r0, t), :] = acc * jnp.where(first, inv[:t], inv[t:])
            lse_ref[pl.ds(r0, t), 0:LANES] = _bcast_lanes(lse[:t])
            lse_ref[pl.ds(r0, t), LANES:2 * LANES] = _bcast_lanes(lse[t:])
            return 0

        lax.fori_loop(0, nq, qblock, 0)

    return _grid_call(
        body, name=name, grid=(N_PAIRS,),
        in_specs=[_pair_spec(s), _pair_spec(s, N_PAIRS), _pair_spec(s, 2 * N_PAIRS), _cum_spec(nq, t)],
        out_specs=[_pair_spec(s), _stat_spec(s)],
        out_shape=[jax.ShapeDtypeStruct((s, BRANCH), F32), jax.ShapeDtypeStruct((s, N_HEADS * LANES), F32)],
        args=(qkv, qkv, qkv, cum4), semantics=("parallel",), exchange=exchange)


def _attn_c_bwd(qkv, cum4, o, lse, do, name, exchange=None):
    s = qkv.shape[0]
    t = _fox_tile(s)
    nq = s // t

    def body(q_ref, k_ref, v_ref, c_ref, o_ref, lse_ref, do_ref, dq_ref, dk_ref, dv_ref, dc_ref, dk_acc, dv_acc):
        first = _lane_is_first_head()
        causal = _stacked_mask(t, strict=False)
        eye = lax.broadcasted_iota(jnp.int32, (t, t), 0) == lax.broadcasted_iota(jnp.int32, (t, t), 1)
        dk_acc[...] = jnp.zeros_like(dk_acc)
        dv_acc[...] = jnp.zeros_like(dv_acc)
        dc_ref[...] = jnp.zeros_like(dc_ref)

        def tile(j, carry, diag, qs, dos, delta, lse):
            dq_acc, rs = carry
            c0 = pl.multiple_of(j * t, t)
            k2 = k_ref[pl.ds(c0, t), :]
            v2 = v_ref[pl.ds(c0, t), :]
            cs = c_ref[0, j]
            z = lax.dot_general(qs, k2, _NT, preferred_element_type=F32)
            sc = jnp.concatenate([z[:t] - cs[0:1, :], z[t:] - cs[1:2, :]], axis=0)
            p = jnp.exp(sc - lse)
            if diag:
                p = jnp.where(causal, p, 0.0)
            ds = p * (lax.dot_general(dos, v2, _NT, preferred_element_type=F32) - delta)
            dsb = ds.astype(BF16)
            dq = jnp.dot(dsb, k2, preferred_element_type=F32)
            dk_acc[pl.ds(c0, t), :] += lax.dot_general(dsb, qs, _TN, preferred_element_type=F32)
            dv_acc[pl.ds(c0, t), :] += lax.dot_general(p.astype(BF16), dos, _TN, preferred_element_type=F32)
            col_sums = jnp.concatenate([jnp.sum(ds[:t], axis=0, keepdims=True), jnp.sum(ds[t:], axis=0, keepdims=True)], axis=0)
            dc_ref[0, j] = dc_ref[0, j] - col_sums
            return dq_acc + jnp.where(first, dq[:t], dq[t:]), rs + jnp.sum(ds, axis=1, keepdims=True)

        def qblock(i, _):
            r0 = pl.multiple_of(i * t, t)
            do2 = do_ref[pl.ds(r0, t), :]
            qs = _stack_heads(q_ref[pl.ds(r0, t), :] * Q_SCALE, first)
            dos = _stack_heads(do2, first)
            delta = jnp.concatenate(_rowsum_heads(do2.astype(F32) * o_ref[pl.ds(r0, t), :], first), axis=0)
            lse = jnp.concatenate([lse_ref[pl.ds(r0, t), 0:1], lse_ref[pl.ds(r0, t), LANES:LANES + 1]], axis=0)
            carry = (jnp.zeros((t, LANES), F32), jnp.zeros((2 * t, 1), F32))
            carry = _steps_in_groups(i, lambda j, c: tile(j, c, False, qs, dos, delta, lse), carry, FULL_ATTENTION_WIDTHS)
            dq_acc, rs = tile(i, carry, True, qs, dos, delta, lse)
            dq_ref[pl.ds(r0, t), :] = (dq_acc * Q_SCALE).astype(BF16)
            as_row = lambda col_vec: jnp.sum(jnp.where(eye, col_vec, 0.0), axis=0, keepdims=True)
            dc_ref[0, i] = dc_ref[0, i] + jnp.concatenate([as_row(rs[:t]), as_row(rs[t:])], axis=0)
            return 0

        lax.fori_loop(0, nq, qblock, 0)
        dk_ref[...] = dk_acc[...].astype(BF16)
        dv_ref[...] = dv_acc[...].astype(BF16)

    out = jax.ShapeDtypeStruct((s, BRANCH), BF16)
    return _grid_call(
        body, name=name, grid=(N_PAIRS,),
        in_specs=[_pair_spec(s), _pair_spec(s, N_PAIRS), _pair_spec(s, 2 * N_PAIRS), _cum_spec(nq, t),
                  _pair_spec(s), _stat_spec(s), _pair_spec(s)],
        out_specs=[_pair_spec(s)] * 3 + [_cum_spec(nq, t)],
        out_shape=[out] * 3 + [jax.ShapeDtypeStruct(cum4.shape, F32)],
        scratch_shapes=[pltpu.VMEM((s, LANES), F32), pltpu.VMEM((s, LANES), F32)],
        args=(qkv, qkv, qkv, cum4, o, lse, do), semantics=("parallel",), exchange=exchange)


FG_CHUNK = 512


def _tri_dot3(x, t):
    hi = x.astype(BF16)
    r1 = x - hi.astype(F32)
    mid = r1.astype(BF16)
    lo = (r1 - mid.astype(F32)).astype(BF16)
    return (jnp.dot(hi, t, preferred_element_type=F32) + jnp.dot(mid, t, preferred_element_type=F32)
            + jnp.dot(lo, t, preferred_element_type=F32))


def _fgate_fwd(h, wf_t, b_col, name):
    s = h.shape[0]
    c = min(FG_CHUNK, s)

    def body(h_ref, w_ref, b_ref, xf_ref, cum_ref, carry_ref):
        @pl.when(pl.program_id(0) == 0)
        def _():
            carry_ref[...] = jnp.zeros_like(carry_ref)

        xf = lax.dot_general(w_ref[...], h_ref[...], _NT, preferred_element_type=F32) + b_ref[:, 0:1]
        xf_ref[...] = xf
        logf = jnp.minimum(xf, 0.0) - jnp.log(1.0 + jnp.exp(-jnp.abs(xf)))
        row = lax.broadcasted_iota(jnp.int32, (c, c), 0)
        col = lax.broadcasted_iota(jnp.int32, (c, c), 1)
        cum = _tri_dot3(logf, (row <= col).astype(BF16)) + carry_ref[:, 0:1]
        cum_ref[...] = cum
        carry_ref[...] = _bcast_lanes(cum[:, c - 1:c])

    out = jax.ShapeDtypeStruct((N_HEADS, s), F32)
    return pl.pallas_call(
        body, name=name, grid=(s // c,),
        in_specs=[pl.BlockSpec((c, D_MODEL), lambda i: (i, 0)),
                  pl.BlockSpec((N_HEADS, D_MODEL), lambda i: (0, 0)),
                  pl.BlockSpec((N_HEADS, LANES), lambda i: (0, 0))],
        out_specs=[pl.BlockSpec((N_HEADS, c), lambda i: (0, i))] * 2,
        out_shape=[out, out],
        scratch_shapes=[pltpu.VMEM((N_HEADS, LANES), F32)],
        compiler_params=_params(("arbitrary",)),
    )(h, wf_t, b_col)


def _fgate_bwd(dcum, xf, h, wf_t, name):
    s = h.shape[0]
    c = min(FG_CHUNK, s)
    n = s // c

    def body(dc_ref, xf_ref, h_ref, w_ref, dw_ref, dh_ref, db_ref, carry_ref):
        @pl.when(pl.program_id(0) == 0)
        def _():
            carry_ref[...] = jnp.zeros_like(carry_ref)
            dw_ref[...] = jnp.zeros_like(dw_ref)
            db_ref[...] = jnp.zeros_like(db_ref)

        row = lax.broadcasted_iota(jnp.int32, (c, c), 0)
        col = lax.broadcasted_iota(jnp.int32, (c, c), 1)
        dlogf = _tri_dot3(dc_ref[...], (row >= col).astype(BF16)) + carry_ref[:, 0:1]
        carry_ref[...] = _bcast_lanes(dlogf[:, 0:1])
        xf = xf_ref[...]
        e = jnp.exp(-jnp.abs(xf))
        r = 1.0 / (1.0 + e)
        dxf = dlogf * jnp.where(xf >= 0, e * r, r)
        db_ref[...] += _bcast_lanes(jnp.sum(dxf, axis=1, keepdims=True))
        dxb = dxf.astype(BF16)
        dw_ref[...] += jnp.dot(dxb, h_ref[...], preferred_element_type=F32)
        dh_ref[...] = lax.dot_general(dxb, w_ref[...], _TN, preferred_element_type=F32)

    rev = lambda i: n - 1 - i
    return pl.pallas_call(
        body, name=name, grid=(n,),
        in_specs=[pl.BlockSpec((N_HEADS, c), lambda i: (0, rev(i))),
                  pl.BlockSpec((N_HEADS, c), lambda i: (0, rev(i))),
                  pl.BlockSpec((c, D_MODEL), lambda i: (rev(i), 0)),
                  pl.BlockSpec((N_HEADS, D_MODEL), lambda i: (0, 0))],
        out_specs=[pl.BlockSpec((N_HEADS, D_MODEL), lambda i: (0, 0)),
                   pl.BlockSpec((c, D_MODEL), lambda i: (rev(i), 0)),
                   pl.BlockSpec((N_HEADS, LANES), lambda i: (0, 0))],
        out_shape=[jax.ShapeDtypeStruct((N_HEADS, D_MODEL), F32), jax.ShapeDtypeStruct((s, D_MODEL), F32),
                   jax.ShapeDtypeStruct((N_HEADS, LANES), F32)],
        scratch_shapes=[pltpu.VMEM((N_HEADS, LANES), F32)],
        compiler_params=_params(("arbitrary",)),
    )(dcum, xf, h, wf_t)


def _to_cum4(v, t):
    s = v.shape[1]
    return v.reshape(N_PAIRS, 2, s // t, t).transpose(0, 2, 1, 3)


def _from_cum4(v4):
    p, nt, two, t = v4.shape
    return v4.transpose(0, 2, 1, 3).reshape(p * two, nt * t)


def _alibi_slopes():
    return (2.0 ** (-8.0 * np.arange(1, N_HEADS + 1, dtype=np.float32) / N_HEADS)).astype(np.float32)


def _per_head_lanes(v):
    return jnp.repeat(v.astype(F32).reshape(N_PAIRS, 1, 2), LANES, axis=2)


def _attn_a_specs(s):
    q = _pair_spec(s)
    k = pl.BlockSpec((s, LANES), lambda p: (0, N_PAIRS + p // 8))
    v = pl.BlockSpec((s, LANES), lambda p: (0, N_PAIRS + KV_A // LANES + p // 8))
    head = pl.BlockSpec((1, 1, 2 * LANES), lambda p: (p, 0, 0))
    return q, k, v, head


def _attn_a_geometry(p, slope_ref, sink_ref):
    kv_half = (p // 4) % 2
    kv_first = kv_half == 0
    lane_first = _lane_is_first_head()
    kv_lanes = (lax.broadcasted_iota(jnp.int32, (1, LANES), 1) // HEAD_DIM) == kv_half
    row = lax.broadcasted_iota(jnp.int32, (2 * WINDOW, 2 * WINDOW), 0)
    cj = lax.broadcasted_iota(jnp.int32, (2 * WINDOW, 2 * WINDOW), 1)
    second = row >= WINDOW
    dist = WINDOW + jnp.where(second, row - WINDOW, row) - cj
    valid = (dist >= 0) & (dist < WINDOW)
    per_row = lambda ref: jnp.where(second[:, 0:1], ref[0, :, LANES:LANES + 1], ref[0, :, 0:1])
    return kv_first, lane_first, kv_lanes, per_row(slope_ref) * dist.astype(F32), valid, per_row(sink_ref)


def _swap_halves(x):
    return pltpu.roll(x, HEAD_DIM, 1)


def _attn_a_fwd(qkv, slopes, sinks, name, exchange=None):
    s = qkv.shape[0]
    nb = s // WINDOW

    def body(q_ref, k_ref, v_ref, sl_ref, sk_ref, o_ref, lse_ref):
        kv_first, lane_first, kv_lanes, bias, valid, sink = _attn_a_geometry(pl.program_id(0), sl_ref, sk_ref)

        def block(r0, k0, width):
            q2 = q_ref[pl.ds(r0, WINDOW), :].astype(F32) * Q_SCALE
            q2r = _swap_halves(q2)
            xs = jnp.concatenate([jnp.where(kv_first, q2, q2r), jnp.where(kv_first, q2r, q2)], axis=0).astype(BF16)
            km = jnp.where(kv_lanes, k_ref[pl.ds(k0, width), :], 0).astype(BF16)
            vm = jnp.where(kv_lanes, v_ref[pl.ds(k0, width), :], 0).astype(BF16)
            sc = lax.dot_general(xs, km, _NT, preferred_element_type=F32) - bias[:, 2 * WINDOW - width:]
            sc = jnp.where(valid[:, 2 * WINDOW - width:], sc, NEG)
            m = jnp.maximum(jnp.max(sc, axis=1, keepdims=True), sink)
            pr = jnp.exp(sc - m)
            l = jnp.sum(pr, axis=1, keepdims=True) + jnp.exp(sink - m)
            os = jnp.dot(pr.astype(BF16), vm, preferred_element_type=F32) * (1.0 / l)
            lse = m + jnp.log(l)
            lse_ref[pl.ds(r0, WINDOW), 0:LANES] = _bcast_lanes(lse[:WINDOW])
            lse_ref[pl.ds(r0, WINDOW), LANES:2 * LANES] = _bcast_lanes(lse[WINDOW:])
            oa = jnp.where(kv_first, os[:WINDOW], _swap_halves(os[:WINDOW]))
            ob = jnp.where(kv_first, _swap_halves(os[WINDOW:]), os[WINDOW:])
            o_ref[pl.ds(r0, WINDOW), :] = jnp.where(lane_first, oa, ob)

        block(0, 0, WINDOW)

        def loop(n, _):
            r0 = pl.multiple_of(n * WINDOW, WINDOW)
            block(r0, pl.multiple_of(r0 - WINDOW, WINDOW), 2 * WINDOW)
            return 0

        _steps_in_groups(nb - 1, lambda n, c: loop(n + 1, c), 0)

    q, k, v, head = _attn_a_specs(s)
    return _grid_call(
        body, name=name, grid=(N_PAIRS,),
        in_specs=[q, k, v, head, head],
        out_specs=[_pair_spec(s), _stat_spec(s)],
        out_shape=[jax.ShapeDtypeStruct((s, BRANCH), F32), jax.ShapeDtypeStruct((s, N_HEADS * LANES), F32)],
        args=(qkv, qkv, qkv, slopes, sinks), semantics=("parallel",), exchange=exchange)


def _attn_a_bwd(qkv, slopes, sinks, o, lse, do, name, exchange=None):
    s = qkv.shape[0]
    nb = s // WINDOW

    def body(q_ref, k_ref, v_ref, sl_ref, sk_ref, o_ref, lse_ref, do_ref, dq_ref, dk_ref, dv_ref, dsk_ref):
        p_id = pl.program_id(0)
        kv_first, lane_first, kv_lanes, bias, valid, sink = _attn_a_geometry(p_id, sl_ref, sk_ref)

        @pl.when(p_id % 8 == 0)
        def _():
            dk_ref[...] = jnp.zeros_like(dk_ref)
            dv_ref[...] = jnp.zeros_like(dv_ref)

        def align(v2):
            v2r = _swap_halves(v2)
            both = jnp.concatenate([jnp.where(kv_first, v2, v2r), jnp.where(kv_first, v2r, v2)], axis=0)
            return jnp.where(kv_lanes, both, 0.0).astype(BF16)

        def block(r0, k0, width, sink_sum):
            xq = align(q_ref[pl.ds(r0, WINDOW), :].astype(F32) * Q_SCALE)
            do2 = do_ref[pl.ds(r0, WINDOW), :].astype(F32)
            xdo = align(do2)
            delta = jnp.concatenate(_rowsum_heads(do2 * o_ref[pl.ds(r0, WINDOW), :], lane_first), axis=0)
            lse = jnp.concatenate([lse_ref[pl.ds(r0, WINDOW), 0:1], lse_ref[pl.ds(r0, WINDOW), LANES:LANES + 1]], axis=0)
            km = jnp.where(kv_lanes, k_ref[pl.ds(k0, width), :], 0).astype(BF16)
            vm = jnp.where(kv_lanes, v_ref[pl.ds(k0, width), :], 0).astype(BF16)
            sc = lax.dot_general(xq, km, _NT, preferred_element_type=F32) - bias[:, 2 * WINDOW - width:]
            pr = jnp.where(valid[:, 2 * WINDOW - width:], jnp.exp(sc - lse), 0.0)
            ds = pr * (lax.dot_general(xdo, vm, _NT, preferred_element_type=F32) - delta)
            dsb = ds.astype(BF16)
            dq_al = jnp.dot(dsb, km, preferred_element_type=F32)
            dk_ref[pl.ds(k0, width), :] += lax.dot_general(dsb, xq, _TN, preferred_element_type=F32)
            dv_ref[pl.ds(k0, width), :] += lax.dot_general(pr.astype(BF16), xdo, _TN, preferred_element_type=F32)
            dqa = jnp.where(kv_first, dq_al[:WINDOW], _swap_halves(dq_al[:WINDOW]))
            dqb = jnp.where(kv_first, _swap_halves(dq_al[WINDOW:]), dq_al[WINDOW:])
            dq_ref[pl.ds(r0, WINDOW), :] = (jnp.where(lane_first, dqa, dqb) * Q_SCALE).astype(BF16)
            return sink_sum + jnp.exp(sink - lse) * delta

        sink_sum = block(0, 0, WINDOW, jnp.zeros((2 * WINDOW, 1), F32))

        def loop(n, c):
            r0 = pl.multiple_of(n * WINDOW, WINDOW)
            return block(r0, pl.multiple_of(r0 - WINDOW, WINDOW), 2 * WINDOW, c)

        sink_sum = _steps_in_groups(nb - 1, lambda n, c: loop(n + 1, c), sink_sum, FULL_ATTENTION_WIDTHS)
        dsk_ref[0, :, 0:LANES] = jnp.broadcast_to(-jnp.sum(sink_sum[:WINDOW], axis=0, keepdims=True), (1, LANES))
        dsk_ref[0, :, LANES:2 * LANES] = jnp.broadcast_to(-jnp.sum(sink_sum[WINDOW:], axis=0, keepdims=True), (1, LANES))

    q, k, v, head = _attn_a_specs(s)
    kv_out = pl.BlockSpec((s, LANES), lambda p: (0, p // 8))
    return _grid_call(
        body, name=name, grid=(N_PAIRS,),
        in_specs=[q, k, v, head, head, _pair_spec(s), _stat_spec(s), _pair_spec(s)],
        out_specs=[_pair_spec(s), kv_out, kv_out, head],
        out_shape=[jax.ShapeDtypeStruct((s, BRANCH), BF16), jax.ShapeDtypeStruct((s, KV_A), F32),
                   jax.ShapeDtypeStruct((s, KV_A), F32), jax.ShapeDtypeStruct((N_PAIRS, 1, 2 * LANES), F32)],
        args=(qkv, qkv, qkv, slopes, sinks, o, lse, do), semantics=("arbitrary",), exchange=exchange)


def _layer_kind(i):
    return i % 3, i // 3


GATHER_FIRST = [("in", 0)]
GATHER_BEHIND = {("qkv", 0): [("out", 0)], ("attn", 0): [("in", 1)], ("attn", 1): [("out", 1), ("in", 2), ("out", 2)],
                 ("attn", 2): [("in", 3), ("out", 3)]}


def _forward_backward(x, target, g_pre, g_post, sinks_a, b_f_c, shards, chip, place):
    s = x.shape[0]
    slopes = _per_head_lanes(jnp.asarray(_alibi_slopes()))
    w_in, w_out, wf_t = {}, {}, {}

    def lands_side_by_side(key):
        return key[0] == "in" and shards[key].shape[1] % LANES == 0

    def gather(keys):
        return _GatherExchange([shards[k] for k in keys], [lands_side_by_side(k) for k in keys])

    def deliver(keys, gathered):
        for key, g in zip(keys, gathered):
            side, layer = key
            sh = shards[key]
            if side == "out":
                g = lax.dynamic_update_slice(g, sh[None], (chip, 0, 0))
                w_out[layer] = g.reshape(4 * sh.shape[0], sh.shape[1])
            elif lands_side_by_side(key):
                w_in[layer] = _place_columns(g, sh, chip, f"own_block_in_l{layer}")
            else:
                g = lax.dynamic_update_slice(g, sh[None], (chip, 0, 0))
                w = g.transpose(1, 0, 2).reshape(sh.shape[0], 4 * sh.shape[1])
                w_in[layer], wf_t[layer] = lax.optimization_barrier((w[:, :4 * BRANCH], w[:, 4 * BRANCH:].T))

    saved = []
    for i in range(DEPTH):
        kind, j = _layer_kind(i)
        tag = f"l{i}"
        nqkv = A_QKV if kind == 0 else B_QKV
        tn = 512 if kind == 0 else 1024
        if i == 0:
            (h, h_t), arrived = _rmsnorm_fwd(x, g_pre[i:i + 1], f"prenorm_{tag}", gather(GATHER_FIRST))
            deliver(GATHER_FIRST, arrived)
        else:
            h, h_t = _rmsnorm_fwd(x, g_pre[i:i + 1], f"prenorm_{tag}")
        w = w_in[i]
        behind = GATHER_BEHIND.get(("qkv", i))
        qkv = _matmul(h, w, out_dtype=BF16, name=f"inproj_qkv_{tag}", n=nqkv, tn=tn,
                      exchange=gather(behind) if behind else None)
        if behind:
            qkv, arrived = qkv
            deliver(behind, arrived)
        z = _matmul(h, w, out_dtype=F32, name=f"inproj_gate_{tag}", n=BRANCH, b_off=nqkv // tn, tn=tn)
        behind = GATHER_BEHIND.get(("attn", i))
        exchange = gather(behind) if behind else None
        if kind == 0:
            sink_l = _per_head_lanes(sinks_a[j])
            (o, lse), arrived = _attn_a_fwd(qkv, slopes, sink_l, f"attn_a_fwd_{tag}", exchange)
            extra = (sink_l, lse)
        elif kind == 1:
            (o, extra), arrived = _attn_b_fwd(qkv, f"attn_b_fwd_{tag}", exchange)
        else:
            b_col = jnp.broadcast_to(b_f_c[j].astype(F32)[:, None], (N_HEADS, LANES))
            xf, cum = _fgate_fwd(h, wf_t[i], b_col, f"fgate_fwd_{tag}")
            cum4 = _to_cum4(cum, _fox_tile(s))
            (o, lse), arrived = _attn_c_fwd(qkv, cum4, f"attn_c_fwd_{tag}", exchange)
            extra = (xf, cum4, lse)
        if behind:
            deliver(behind, arrived)
        x_next, y, u_t = _gated_out_proj(o, z, w_out[i], x, g_post[i:i + 1], f"outproj_{tag}")
        saved.append((x, h, h_t, qkv, z, o, u_t, y, extra))
        x = x_next

    dx, loss_part = _loss_and_grad(x, target)

    d_g_pre, d_g_post = [None] * DEPTH, [None] * DEPTH
    d_sinks = [None, None]
    d_b_f = None
    reduced = {}
    pending = None

    def finish_reduce(layer, side, own, arr):
        kind, j = _layer_kind(layer)
        reduced[(side, kind)] = _sum_chips(own, arr, place, f"shard_sum_{side}_l{layer}", j, 2 if kind == 0 else 1,
                                           into=reduced.get((side, kind)))

    for i in reversed(range(DEPTH)):
        kind, j = _layer_kind(i)
        tag = f"l{i}"
        x_in, h, h_t, qkv, z, o, u_t, y, extra = saved[i]
        tn = 512 if kind == 0 else 1024
        (dy, d_g_post[i], do, dz), _ = _gated_out_proj_bwd(dx, y, g_post[i:i + 1], w_out[i], o, z, f"outproj_bwd_{tag}")
        dw_out = _matmul(u_t, dy, out_dtype=BF16, name=f"dw_out_{tag}")
        dw_out = dw_out.reshape(4, dw_out.shape[0] // 4, dw_out.shape[1])
        dh_f = None
        exchange = _SiblingExchange([dw_out])
        if pending:
            exchange = _BothExchanges(exchange, _ScatterExchange([pending[1]]))
        if kind == 0:
            sink_l, lse = extra
            (dq, dk, dv, dsk), arrived = _attn_a_bwd(qkv, slopes, sink_l, o, lse, do, f"attn_a_bwd_{tag}", exchange)
            d_sinks[j] = dsk[:, 0, ::LANES].reshape(N_HEADS)
            parts = [dq, dk.astype(BF16), dv.astype(BF16), dz]
        elif kind == 1:
            (dq, dk, dv), arrived = _attn_b_bwd(qkv, extra, do, f"attn_b_bwd_{tag}", exchange)
            parts = [dq, dk, dv, dz]
        else:
            xf, cum4, lse = extra
            (dq, dk, dv, dcum4), arrived = _attn_c_bwd(qkv, cum4, o, lse, do, f"attn_c_bwd_{tag}", exchange)
            d_wf_t, dh_f, db = _fgate_bwd(_from_cum4(dcum4), xf, h, wf_t[i], f"fgate_bwd_{tag}")
            d_b_f = db[:, 0]
            parts = [dq, dk, dv, dz]
        sum_out = _add_pairs(dw_out, arrived[0], place, f"chip_sum_out_{tag}")
        if pending:
            finish_reduce(pending[0], "in", pending[1], arrived[1])
        dproj = jnp.concatenate(parts, axis=1)
        scatter_out = _ScatterExchange([sum_out])
        if kind == 2:
            dw_in, arrived = _matmul(h_t, dproj, out_dtype=F32, name=f"dw_in_{tag}", tn=tn, exchange=scatter_out)
            dw_in = jnp.concatenate([dw_in, d_wf_t.T], axis=1)
            dw_in = dw_in.reshape(dw_in.shape[0], 4, dw_in.shape[1] // 4).transpose(1, 0, 2).astype(BF16)
        else:
            dw_in, arrived = _matmul(h_t, dproj, out_dtype=BF16, name=f"dw_in_{tag}", col_blocks=4,
                                     tn=1152 if kind == 0 else 1024, exchange=scatter_out)
        finish_reduce(i, "out", sum_out, arrived[0])
        (dx, d_g_pre[i]), (their_in,) = _in_proj_bwd(
            dproj, w_in[i], dh_f, dx, x_in, g_pre[i:i + 1], f"inproj_bwd_{tag}", {0: 1536, 1: 2048, 2: 1024}[kind],
            exchange=_SiblingExchange([dw_in]))
        pending = (i, _add_pairs(dw_in, their_in, place, f"chip_sum_in_{tag}"))

    return dict(loss=loss_part, dx=dx, g_pre=jnp.concatenate(d_g_pre, axis=0), g_post=jnp.concatenate(d_g_post, axis=0),
                sinks_a=jnp.stack(d_sinks), b_f_c=d_b_f[None, :], reduced=reduced, last_sum=pending[1])


def _place():
    x, y, c = lax.axis_index("x"), lax.axis_index("y"), lax.axis_index("c")
    others = [(1 - x, y), (x, 1 - y), (1 - x, 1 - y)]
    return x, y, c, others


def _half_rows(ref_rows, which):
    half = ref_rows // 2
    return pl.ds(pl.multiple_of(which * half, half), half)


def _remote(src, dst, sems, k, device):
    send, recv = sems
    return pltpu.make_async_remote_copy(src_ref=src, dst_ref=dst, send_sem=send.at[k], recv_sem=recv.at[k],
                                        device_id=device, device_id_type=MESH)


def _hbm_call(body, name, ins, out_shapes, n_remote, aliases=None):
    any_spec = pl.BlockSpec(memory_space=pl.ANY)
    return pl.pallas_call(
        body, name=name, in_specs=[any_spec] * len(ins), out_specs=[any_spec] * len(out_shapes),
        out_shape=out_shapes, input_output_aliases=aliases or {},
        scratch_shapes=[pltpu.SemaphoreType.DMA((n_remote,)), pltpu.SemaphoreType.DMA((n_remote,))],
    )(*ins)


class _GatherExchange:
    SEMS = 8

    def __init__(self, shards, side_by_side):
        self.ins = list(shards)
        self.side_by_side = list(side_by_side)
        self.out_shapes = [jax.ShapeDtypeStruct((a.shape[0], 4 * a.shape[1]) if wide else (4,) + a.shape, a.dtype)
                           for a, wide in zip(shards, side_by_side)]
        self.n_sems = self.SEMS * len(shards)
        self.aliases = {}

    def _copies(self, ins, outs, sems):
        x, y, c, _ = _place()
        me, diag = 2 * x + y, 2 * (1 - x) + (1 - y)
        nbr = [((1 - x, y, c), 2 * (1 - x) + y), ((x, 1 - y, c), 2 * x + (1 - y))]
        sibling = (x, y, 1 - c)
        table = []
        for w, (src, dst, wide) in enumerate(zip(ins, outs, self.side_by_side)):
            rows, cols = src.shape
            half, quarter = rows // 2, rows // 4

            def slot(chip, core, piece=None, dst=dst, wide=wide, cols=cols, half=half, quarter=quarter):
                start, size = (core * half, half) if piece is None else (core * half + piece * quarter, quarter)
                which = pl.ds(pl.multiple_of(start, quarter), size)
                return dst.at[which, pl.ds(pl.multiple_of(chip * cols, LANES), cols)] if wide else dst.at[chip, which]

            k0 = self.SEMS * w
            cp = lambda s_, d_, k, dev: _remote(s_, d_, sems, k0 + k, dev)
            mine_src = src.at[pl.ds(pl.multiple_of(c * half, half), half)]
            d = dict(
                send=[cp(mine_src, slot(me, c), k, nbr[k][0]) for k in range(2)],
                got=[cp(slot(nbr[k][1], c), slot(nbr[k][1], c), k, nbr[k][0]) for k in range(2)],
                fwd=[cp(slot(nbr[k][1], c, k), slot(nbr[k][1], c, k), 2 + k, nbr[1 - k][0]) for k in range(2)],
                got_fwd=[cp(slot(diag, c, k), slot(diag, c, k), 2 + k, nbr[1 - k][0]) for k in range(2)],
                pass_=[cp(slot(nbr[k][1], c), slot(nbr[k][1], c), 4 + k, sibling) for k in range(2)]
                + [cp(slot(diag, c, k), slot(diag, c, k), 6 + k, sibling) for k in range(2)],
                got_pass=[cp(slot(nbr[k][1], 1 - c), slot(nbr[k][1], 1 - c), 4 + k, sibling) for k in range(2)]
                + [cp(slot(diag, 1 - c, k), slot(diag, 1 - c, k), 6 + k, sibling) for k in range(2)])
            table.append(d)
        return table

    def start(self, ins, outs, sems):
        for d in self._copies(ins, outs, sems):
            for cp in d["send"]:
                cp.start()

    def mid(self, ins, outs, sems):
        for d in self._copies(ins, outs, sems):
            for k in range(2):
                d["got"][k].wait_recv()
                d["fwd"][k].start()
                d["pass_"][k].start()

    def finish(self, ins, outs, sems):
        table = self._copies(ins, outs, sems)
        for d in table:
            for k in range(2):
                d["got_fwd"][k].wait_recv()
                d["pass_"][2 + k].start()
        for d in table:
            for cp in d["got_pass"]:
                cp.wait_recv()
            for cp in d["send"] + d["fwd"] + d["pass_"]:
                cp.wait_send()


class _SemaphoresFrom:
    def __init__(self, ref, start):
        self._ref, self._start = ref, start

    @property
    def at(self):
        return self

    def __getitem__(self, k):
        return self._ref.at[self._start + k]


class _BothExchanges:
    def __init__(self, first, second):
        self.parts = (first, second)
        self.ins = first.ins + second.ins
        self.out_shapes = first.out_shapes + second.out_shapes
        self.n_sems = first.n_sems + second.n_sems
        self.aliases = {}

    def _each(self, phase, ins, outs, sems):
        i0 = o0 = s0 = 0
        for ex in self.parts:
            n_in, n_out = len(ex.ins), len(ex.out_shapes)
            getattr(ex, phase)(ins[i0:i0 + n_in], outs[o0:o0 + n_out], tuple(_SemaphoresFrom(r, s0) for r in sems))
            i0, o0, s0 = i0 + n_in, o0 + n_out, s0 + ex.n_sems

    def start(self, ins, outs, sems):
        self._each("start", ins, outs, sems)

    def mid(self, ins, outs, sems):
        self._each("mid", ins, outs, sems)

    def finish(self, ins, outs, sems):
        self._each("finish", ins, outs, sems)


def _place_columns(wide, block, chip, name):
    rows, cc = block.shape
    tr = min(512, rows)

    def body(c_ref, b_ref, w_ref, o_ref):
        o_ref[...] = b_ref[...]

    return pl.pallas_call(
        body, name=name,
        grid_spec=pltpu.PrefetchScalarGridSpec(
            num_scalar_prefetch=1, grid=(rows // tr,),
            in_specs=[pl.BlockSpec((tr, cc), lambda r, c_ref: (r, 0)), pl.BlockSpec(memory_space=pl.ANY)],
            out_specs=pl.BlockSpec((tr, cc), lambda r, c_ref: (r, c_ref[0]))),
        out_shape=jax.ShapeDtypeStruct(wide.shape, wide.dtype), input_output_aliases={2: 0},
        compiler_params=_params(("parallel",)),
    )(chip.astype(jnp.int32).reshape(1), block, wide)


def _grid_call(body, *, name, grid, in_specs, out_specs, out_shape, args, scratch_shapes=(), semantics, exchange=None):
    if exchange is None:
        res = pl.pallas_call(body, name=name, grid=grid, in_specs=list(in_specs), out_specs=list(out_specs),
                             out_shape=list(out_shape), scratch_shapes=list(scratch_shapes),
                             compiler_params=_params(semantics))(*args)
        return res, []
    n_in, n_out, n_scr = len(args), len(out_shape), len(scratch_shapes)
    x_in, x_out = len(exchange.ins), len(exchange.out_shapes)
    steps = math.prod(grid)

    def wrapped(*refs):
        core_in, ex_in = refs[:n_in], refs[n_in:n_in + x_in]
        rest = refs[n_in + x_in:]
        core_out, ex_out = rest[:n_out], rest[n_out:n_out + x_out]
        scratch, sems = rest[n_out + x_out:n_out + x_out + n_scr], rest[n_out + x_out + n_scr:]
        step = 0
        for axis, extent in enumerate(grid):
            step = step * extent + pl.program_id(axis)

        @pl.when(step == 0)
        def _():
            exchange.start(ex_in, ex_out, sems)

        body(*core_in, *core_out, *scratch)

        @pl.when(step == max((3 * steps) // 4 - 1, 0))
        def _():
            exchange.mid(ex_in, ex_out, sems)

        @pl.when(step == steps - 1)
        def _():
            exchange.finish(ex_in, ex_out, sems)

    any_spec = pl.BlockSpec(memory_space=pl.ANY)
    res = pl.pallas_call(
        wrapped, name=name, grid=grid,
        in_specs=list(in_specs) + [any_spec] * x_in, out_specs=list(out_specs) + [any_spec] * x_out,
        out_shape=list(out_shape) + list(exchange.out_shapes),
        input_output_aliases={n_in + a: n_out + b for a, b in exchange.aliases.items()},
        scratch_shapes=list(scratch_shapes) + [pltpu.SemaphoreType.DMA((exchange.n_sems,)),
                                               pltpu.SemaphoreType.DMA((exchange.n_sems,))],
        compiler_params=_params(("arbitrary",) * len(grid)),
    )(*args, *exchange.ins)
    return res[:n_out], res[n_out:]


class _SiblingExchange:
    def __init__(self, parts):
        self.ins = list(parts)
        self.out_shapes = [jax.ShapeDtypeStruct((4, a.shape[1] // 2, a.shape[2]), a.dtype) for a in parts]
        self.n_sems = len(parts)
        self.aliases = {}

    def _copies(self, ins, outs, sems):
        x, y, c, _ = _place()
        return [_remote(src.at[:, _half_rows(src.shape[1], 1 - c)], dst, sems, w, (x, y, 1 - c))
                for w, (src, dst) in enumerate(zip(ins, outs))]

    def start(self, ins, outs, sems):
        for cp in self._copies(ins, outs, sems):
            cp.start()

    def mid(self, ins, outs, sems):
        pass

    def finish(self, ins, outs, sems):
        for cp in self._copies(ins, outs, sems):
            cp.wait_recv()
            cp.wait_send()


class _ScatterExchange:
    def __init__(self, sums):
        self.ins = list(sums)
        self.out_shapes = [jax.ShapeDtypeStruct(a.shape, a.dtype) for a in sums]
        self.n_sems = 3 * len(sums)
        self.aliases = {}

    def _copies(self, ins, outs, sems):
        x, y, c, others = _place()
        me = 2 * x + y
        table = []
        for w, (src, dst) in enumerate(zip(ins, outs)):
            for j, (px, py) in enumerate(others):
                there = 2 * px + py
                send = _remote(src.at[there], dst.at[me], sems, 3 * w + j, (px, py, c))
                landed = _remote(dst.at[there], dst.at[there], sems, 3 * w + j, (px, py, c))
                table.append((send, landed))
        return table

    def start(self, ins, outs, sems):
        for send, _ in self._copies(ins, outs, sems):
            send.start()

    def mid(self, ins, outs, sems):
        pass

    def finish(self, ins, outs, sems):
        table = self._copies(ins, outs, sems)
        for _, landed in table:
            landed.wait_recv()
        for send, _ in table:
            send.wait_send()


def _sibling_join(shards, name):
    n = len(shards)

    def body(*refs):
        ins, outs, sems = refs[:n], refs[n:2 * n], refs[2 * n:2 * n + 2]
        x, y, c, _ = _place()
        pend = []
        for w in range(n):
            rows = ins[w].shape[1]
            mine, theirs = _half_rows(rows, c), _half_rows(rows, 1 - c)
            cp = _remote(ins[w].at[:, mine], outs[w].at[:, mine], sems, w, (x, y, 1 - c))
            cp.start()
            pend.append((cp, _remote(ins[w].at[:, theirs], outs[w].at[:, theirs], sems, w, (x, y, 1 - c))))
        for cp, landed in pend:
            landed.wait_recv()
            cp.wait_send()

    out_shapes = [jax.ShapeDtypeStruct(a.shape, a.dtype) for a in shards]
    return _hbm_call(body, name, shards, out_shapes, n, aliases={w: w for w in range(n)})


SMALL_ROWS = 136


def _all_reduce_small(vec):
    def body(v_ref, o_ref, buf, send, recv, loc):
        x, y, c, _ = _place()
        me = 4 * x + 2 * y + c
        lc = pltpu.make_async_copy(v_ref, buf.at[me], loc.at[0])
        lc.start()
        cps = []
        for k in range(1, 8):
            fx, fy, fc = (k >> 2) & 1, (k >> 1) & 1, k & 1
            peer = (x ^ fx, y ^ fy, c ^ fc)
            cp = pltpu.make_async_remote_copy(src_ref=v_ref, dst_ref=buf.at[me], send_sem=send.at[k - 1],
                                              recv_sem=recv.at[k - 1], device_id=peer, device_id_type=MESH)
            cp.start()
            cps.append((cp, 4 * peer[0] + 2 * peer[1] + peer[2]))
        for k, (cp, src) in enumerate(cps):
            pltpu.make_async_remote_copy(src_ref=v_ref, dst_ref=buf.at[src], send_sem=send.at[k], recv_sem=recv.at[k],
                                         device_id=(x, y, c), device_id_type=MESH).wait_recv()
        for cp, _ in cps:
            cp.wait_send()
        lc.wait()
        total = buf[0]
        for k in range(1, 8):
            total = total + buf[k]
        o_ref[...] = total

    vm = pl.BlockSpec(memory_space=pltpu.VMEM)
    return pl.pallas_call(
        body, name="all_reduce_small", in_specs=[vm], out_specs=vm,
        out_shape=jax.ShapeDtypeStruct(vec.shape, F32),
        scratch_shapes=[pltpu.VMEM((8,) + vec.shape, F32), pltpu.SemaphoreType.DMA((7,)),
                        pltpu.SemaphoreType.DMA((7,)), pltpu.SemaphoreType.DMA((1,))],
    )(vec)


SUM_ROWS = 256


def _add_pairs(part, theirs, place, name):
    four, rh, cc = theirs.shape
    tr = min(SUM_ROWS, rh)
    halves = part.reshape(four, 2, rh, cc)

    def body(p_ref, a_ref, b_ref, o_ref):
        o_ref[0] = (a_ref[0, 0].astype(F32) + b_ref[0].astype(F32)).astype(o_ref.dtype)

    spec = pl.BlockSpec((1, tr, cc), lambda k, r, p_ref: (k, r, 0))
    return pl.pallas_call(
        body, name=name,
        grid_spec=pltpu.PrefetchScalarGridSpec(
            num_scalar_prefetch=1, grid=(four, rh // tr),
            in_specs=[pl.BlockSpec((1, 1, tr, cc), lambda k, r, p_ref: (k, p_ref[1], r, 0)), spec], out_specs=spec),
        out_shape=jax.ShapeDtypeStruct(theirs.shape, theirs.dtype),
        compiler_params=_params(("parallel", "parallel")),
    )(place, halves, theirs)


def _sum_chips(own, arrived, place, name, layer, n_layers, into=None):
    four, rh, cc = own.shape
    tr = min(SUM_ROWS, rh)
    nr = rh // tr

    def body(p_ref, own_ref, arr_ref, *rest):
        o_ref = rest[-1]
        x, y = lax.axis_index("x"), lax.axis_index("y")
        tot = own_ref[0].astype(F32)
        for px, py in ((1 - x, y), (x, 1 - y), (1 - x, 1 - y)):
            tot = tot + arr_ref[2 * px + py].astype(F32)
        o_ref[0] = tot

    in_specs = [pl.BlockSpec((1, tr, cc), lambda r, p_ref: (p_ref[0], r, 0)),
                pl.BlockSpec((4, tr, cc), lambda r, p_ref: (0, r, 0))]
    args, aliases = [place, own, arrived], {}
    if into is not None:
        in_specs.append(pl.BlockSpec(memory_space=pl.ANY))
        args.append(into)
        aliases = {3: 0}
    return pl.pallas_call(
        body, name=name,
        grid_spec=pltpu.PrefetchScalarGridSpec(
            num_scalar_prefetch=1, grid=(nr,), in_specs=in_specs,
            out_specs=pl.BlockSpec((1, tr, cc), lambda r, p_ref: (layer, p_ref[1] * nr + r, 0))),
        out_shape=jax.ShapeDtypeStruct((n_layers, 2 * rh, cc), F32), input_output_aliases=aliases,
        compiler_params=_params(("parallel",)),
    )(*args)


ADAM_ROWS = 256


def _adamw(w, g, m, v, name):
    shape = w.shape
    as3 = lambda a: a.reshape((-1,) + shape[-2:])
    layers, rows, cc = as3(w).shape
    by_rows = rows % min(ADAM_ROWS, rows) == 0
    tr, tc = (min(ADAM_ROWS, rows), cc) if by_rows else (rows, ADAM_ROWS)
    assert rows % tr == 0 and cc % tc == 0

    def body(w_ref, g_ref, m_ref, v_ref, d_ref, nm_ref, nv_ref):
        _adamw_update(w_ref, g_ref, m_ref, v_ref, d_ref, nm_ref, nv_ref)

    spec = pl.BlockSpec((1, tr, tc), (lambda l, i: (l, i, 0)) if by_rows else (lambda l, i: (l, 0, i)))
    sh = jax.ShapeDtypeStruct((layers, rows, cc), F32)
    outs = pl.pallas_call(
        body, name=name, grid=(layers, (rows // tr) * (cc // tc)), in_specs=[spec] * 4, out_specs=[spec] * 3,
        out_shape=[sh] * 3,
        compiler_params=_params(("parallel", "parallel")),
    )(as3(w), as3(g), as3(m), as3(v))
    return [o.reshape(shape) for o in outs]


def _adamw_update(w_ref, g_ref, m_ref, v_ref, d_ref, nm_ref, nv_ref):
    c1 = 1.0 - ADAM_B1 ** ADAM_STEP
    c2 = 1.0 - ADAM_B2 ** ADAM_STEP
    gv = g_ref[...]
    nm = ADAM_B1 * m_ref[...] + (1.0 - ADAM_B1) * gv
    nv = ADAM_B2 * v_ref[...] + (1.0 - ADAM_B2) * (gv * gv)
    nm_ref[...] = nm
    nv_ref[...] = nv
    d_ref[...] = -ADAM_LR * ((nm / c1) / (jnp.sqrt(nv / c2) + ADAM_EPS) + ADAM_WD * w_ref[...])


ADAM_MANY_STEPS = 16


def _adamw_many(quads, name, exchange=None):
    n = ADAM_MANY_STEPS
    specs = []
    for w, _, _, _ in quads:
        layers, rows, cc = w.shape
        if rows % (8 * n) == 0:
            specs.append(pl.BlockSpec((layers, rows // n, cc), lambda i: (0, i, 0)))
        else:
            assert cc % (LANES * n) == 0, (name, w.shape)
            specs.append(pl.BlockSpec((layers, rows, cc // n), lambda i: (0, 0, i)))

    def body(*refs):
        ins, outs = refs[:4 * len(quads)], refs[4 * len(quads):]
        for q in range(len(quads)):
            _adamw_update(*ins[4 * q:4 * q + 4], *outs[3 * q:3 * q + 3])

    res, arrived = _grid_call(
        body, name=name, grid=(n,), in_specs=[s for s in specs for _ in range(4)],
        out_specs=[s for s in specs for _ in range(3)],
        out_shape=[jax.ShapeDtypeStruct(w.shape, F32) for w, _, _, _ in quads for _ in range(3)],
        args=tuple(a for quad in quads for a in quad), semantics=("parallel",), exchange=exchange)
    return [list(res[3 * q:3 * q + 3]) for q in range(len(quads))], arrived


def _pack_small(g_pre, g_post, sinks_a, b_f_c, loss_row):
    pad = lambda a: jnp.pad(a.reshape(1, -1).astype(F32), ((0, 0), (0, LANES - a.size)))
    rows = [g_pre.astype(F32).reshape(-1, LANES), g_post.astype(F32).reshape(-1, LANES), pad(sinks_a), pad(b_f_c), loss_row]
    packed = jnp.concatenate(rows, axis=0)
    return jnp.pad(packed, ((0, SMALL_ROWS - packed.shape[0]), (0, 0)))


def _unpack_small(p):
    n = DEPTH * D_MODEL // LANES
    return (p[:n].reshape(DEPTH, D_MODEL), p[n:2 * n].reshape(DEPTH, D_MODEL), p[2 * n, :2 * N_HEADS].reshape(2, N_HEADS),
            p[2 * n + 1, :N_HEADS].reshape(1, N_HEADS), p[2 * n + 2, 0])


def kernel(x, g_pre, g_post, w_in_a, w_out_a, sinks_a, w_in_b, w_out_b, w_in_c, b_f_c, w_out_c, loss_target, m_g_pre, m_g_post, m_w_in_a, m_w_out_a, m_sinks_a, m_w_in_b, m_w_out_b, m_w_in_c, m_b_f_c, m_w_out_c, v_g_pre, v_g_post, v_w_in_a, v_w_out_a, v_sinks_a, v_w_in_b, v_w_out_b, v_w_in_c, v_b_f_c, v_w_out_c):
    big_w = [w_in_a, w_out_a, w_in_b, w_out_b, w_in_c, w_out_c]
    big_m = [m_w_in_a, m_w_out_a, m_w_in_b, m_w_out_b, m_w_in_c, m_w_out_c]
    big_v = [v_w_in_a, v_w_out_a, v_w_in_b, v_w_out_b, v_w_in_c, v_w_out_c]

    chip = 2 * lax.axis_index("x") + lax.axis_index("y")
    place = jnp.stack([chip, lax.axis_index("c")]).astype(jnp.int32)
    by_kind = {0: (w_in_a, w_out_a), 1: (w_in_b, w_out_b), 2: (w_in_c, w_out_c)}
    shards = {}
    for i in range(DEPTH):
        kind, j = _layer_kind(i)
        shards[("in", i)] = by_kind[kind][0][j].astype(BF16)
        shards[("out", i)] = by_kind[kind][1][j].astype(BF16)

    res = _forward_backward(x[0], loss_target[0], g_pre, g_post, sinks_a, b_f_c, shards, chip, place)
    reduced = res["reduced"]
    rest = [("out", 0), ("in", 1), ("out", 1), ("in", 2), ("out", 2)]
    grads = [None] + list(_sibling_join([reduced[k] for k in rest], "grad_sibling_join"))

    small = _unpack_small(_all_reduce_small(
        _pack_small(res["g_pre"], res["g_post"], res["sinks_a"], res["b_f_c"], res["loss"])))
    g_small, loss = small[:4], small[4]

    zero_row = jnp.zeros((1, LANES), F32)
    pk = lambda a: _pack_small(a[0], a[1], a[2], a[3], zero_row)
    sm = _adamw(pk([g_pre, g_post, sinks_a, b_f_c]), pk(g_small), pk([m_g_pre, m_g_post, m_sinks_a, m_b_f_c]),
                pk([v_g_pre, v_g_post, v_sinks_a, v_b_f_c]), "adamw_small")
    sm = [_unpack_small(a)[:4] for a in sm]
    turned = lambda a: jnp.swapaxes(a, 1, 2)
    g_c = lax.optimization_barrier(turned(grads[4]))
    grads[4] = turned(g_c)
    quads = [(w_in_b, grads[2], m_w_in_b, v_w_in_b), (turned(w_in_c), g_c, turned(m_w_in_c), turned(v_w_in_c))]
    (upd_in_b, upd_in_c), arrived = _adamw_many(quads, "adamw_in_b_c", exchange=_ScatterExchange([res["last_sum"]]))
    half = _sum_chips(res["last_sum"], arrived[0], place, "shard_sum_in_l0", 0, 2, into=reduced[("in", 0)])
    grads[0] = _sibling_join([half], "grad_sibling_join_w_in_a")[0]
    bigs = [_adamw(w_in_a, grads[0], m_w_in_a, v_w_in_a, "adamw_w_in_a"),
            _adamw(w_out_a, grads[1], m_w_out_a, v_w_out_a, "adamw_w_out_a"),
            upd_in_b,
            _adamw(w_out_b, grads[3], m_w_out_b, v_w_out_b, "adamw_w_out_b"),
            [turned(o) for o in upd_in_c],
            _adamw(w_out_c, grads[5], m_w_out_c, v_w_out_c, "adamw_w_out_c")]

    def ordered(small4, big6):
        return [small4[0], small4[1], big6[0], big6[1], small4[2], big6[2], big6[3], big6[4], small4[3], big6[5]]

    out = [loss, res["dx"][None], *ordered(g_small, grads)]
    for k in range(3):
        out += ordered(sm[k], [b[k] for b in bigs])
    return tuple(out)
```
